```python
import math
import jax
import jax.numpy as jnp
from jax import lax
import numpy as np

D_MODEL = 1024
BATCH = 4
SEQ = 8192
DEPTH = 2

N_BRANCH = 4
HEAD_DIM = 64
MIX_W = D_MODEL // N_BRANCH
N_HEADS_MIX = MIX_W // HEAD_DIM
N_META = 16
CHUNK = 64
PAD = CHUNK - N_META
NEG = -1e30
RWKV_LORA_W = 64
RWKV_LORA_A = 64
RWKV_LORA_G = 128
RWKV_GN_EPS = HEAD_DIM * 1e-5
GLA_DK = HEAD_DIM // 2
GLA_LORA = 16
GLA_TAU = 16.0
DSA_KV_RANK = 128
IDX_HEADS = 8
IDX_DIM = 32
TOPK_MAX = 256
Q_BLOCK = 128
N_BUCKETS = 32
MAX_DISTANCE = 128
CONV_W = 4
N_GROUPS = 4
EXPERTS_PER_GROUP = 4
N_EXPERTS = N_GROUPS * EXPERTS_PER_GROUP
TOP_K_INNER = 2
D_EXPERT = D_MODEL // 4
DN_ALPHA = (2 * DEPTH) ** 0.25
DN_BETA = (8 * DEPTH) ** -0.25
LN_EPS = 1e-5

RWKV_SPLITS = (MIX_W, MIX_W, MIX_W, RWKV_LORA_W, RWKV_LORA_A, RWKV_LORA_G)
GLA_SPLITS = (N_HEADS_MIX * GLA_DK, N_HEADS_MIX * GLA_DK, MIX_W, GLA_LORA, MIX_W)
DSA_SPLITS = (MIX_W, DSA_KV_RANK, IDX_HEADS * IDX_DIM, IDX_DIM, IDX_HEADS)
MLSTM_SPLITS = (MIX_W, MIX_W, MIX_W, N_HEADS_MIX, N_HEADS_MIX, MIX_W)
IN_SPLITS = (sum(RWKV_SPLITS), sum(GLA_SPLITS), sum(DSA_SPLITS), sum(MLSTM_SPLITS), N_BRANCH * D_MODEL)
IN_COLS = sum(IN_SPLITS)

kernel_name = 'hybrid_gated_rwkv7_gla_dsa_mlstm_hmoe'


def _split(a, sizes):
    return jnp.split(a, [int(s) for s in np.cumsum(sizes)[:-1]], axis=-1)


def _layer_norm(x, g, b):
    xf = x.astype(jnp.float32)
    mu = jnp.mean(xf, -1, keepdims=True)
    var = jnp.mean(jnp.square(xf - mu), -1, keepdims=True)
    return ((xf - mu) * lax.rsqrt(var + LN_EPS)).astype(x.dtype) * g + b


def _std_norm(y, eps):
    mu = jnp.mean(y, -1, keepdims=True)
    var = jnp.mean(jnp.square(y - mu), -1, keepdims=True)
    return (y - mu) * lax.rsqrt(var + eps)


def _rms(y, eps=1e-6):
    return y * lax.rsqrt(jnp.mean(jnp.square(y), -1, keepdims=True) + eps)


def _token_shift(a):
    return jnp.pad(a, ((0, 0), (1, 0), (0, 0)))[:, :-1]


def _causal_dwconv(a, w, b):
    out = lax.conv_general_dilated(a, w.astype(a.dtype)[:, None, :], (1,), [(CONV_W - 1, 0)],
                                   dimension_numbers=('NWC', 'WIO', 'NWC'),
                                   feature_group_count=a.shape[-1])
    return out + b


def _to_chunks(t, n_heads, fill=0.0):
    B, T, C = t.shape
    t = jnp.pad(t, ((0, 0), (PAD, 0), (0, 0)), constant_values=fill)
    return t.reshape(B, (T + PAD) // CHUNK, CHUNK, n_heads, C // n_heads).transpose(0, 3, 1, 2, 4)


def _from_chunks(t):
    B, H, NC, L, d = t.shape
    return t.transpose(0, 2, 3, 1, 4).reshape(B, NC * L, H, d)[:, PAD:]


def _t5_bucket(dist):
    max_exact = N_BUCKETS // 2
    n = jnp.maximum(dist, 0)
    large = max_exact + (jnp.log(jnp.maximum(n, 1).astype(jnp.float32) / max_exact)
                         / math.log(MAX_DISTANCE / max_exact) * (N_BUCKETS - max_exact)).astype(jnp.int32)
    return jnp.where(n < max_exact, n, jnp.minimum(large, N_BUCKETS - 1))


def _rwkv7_mixer(p, mu, w_up, w0, a_up, a0, g_up, k_k, k_a, r_k, gn_g, gn_b):
    B, T, _ = p.shape
    H, N = N_HEADS_MIX, HEAD_DIM
    p = p + (_token_shift(p) - p) * mu
    r, k, v, xw, xa, xg = _split(p, RWKV_SPLITS)
    w_log = -jax.nn.softplus(-(w0 + jnp.tanh(xw) @ w_up)) - 0.5
    decay = jnp.exp(-jnp.exp(w_log))
    a = jax.nn.sigmoid(a0 + xa @ a_up)
    g = jax.nn.sigmoid(xg) @ g_up
    kk = (k * k_k).reshape(B, T, H, N)
    kk = kk / jnp.maximum(jnp.sqrt(jnp.sum(kk * kk, -1, keepdims=True)), 1e-12)
    k = k * (1.0 + (a - 1.0) * k_a)
    r, decay, k, v, a = (t.reshape(B, T, H, N) for t in (r, decay, k, v, a))

    def step(s, inp):
        r_t, w_t, k_t, v_t, kk_t, a_t = inp
        s_kk = jnp.einsum('bhvk,bhk->bhv', s, kk_t)
        s = (s * w_t[:, :, None, :] - s_kk[..., None] * (kk_t * a_t)[:, :, None, :]
             + v_t[..., None] * k_t[:, :, None, :])
        return s, jnp.einsum('bhvk,bhk->bhv', s, r_t)

    s0 = jnp.zeros((B, H, N, N), jnp.float32)
    _, y = lax.scan(step, s0, tuple(jnp.moveaxis(t, 1, 0) for t in (r, decay, k, v, kk, a)))
    y = jnp.moveaxis(y, 0, 1)
    y = _std_norm(y, RWKV_GN_EPS).reshape(B, T, H * N) * gn_g + gn_b
    bonus = jnp.sum(r * k * r_k, -1, keepdims=True) * v
    return (y + bonus.reshape(B, T, H * N)) * g


def _gla_mixer(p, a_up, a_b, norm_g):
    B, T, _ = p.shape
    H = N_HEADS_MIX
    q, k, v, xa, og = _split(p, GLA_SPLITS)
    la = jax.nn.log_sigmoid(xa @ a_up + a_b) / GLA_TAU
    q = _to_chunks(q, H) * GLA_DK ** -0.5
    k = _to_chunks(k, H)
    v = _to_chunks(v, H)
    la = _to_chunks(la, H)
    b = jnp.cumsum(la, axis=3)
    b_last = b[:, :, :, -1:]
    q_g = q * jnp.exp(b)
    att = jnp.einsum('bhcld,bhcsd->bhcls', q_g, k * jnp.exp(-b))
    att = jnp.where(jnp.tril(jnp.ones((CHUNK, CHUNK), bool)), att, 0.0)
    o = jnp.einsum('bhcls,bhcsv->bhclv', att, v)
    u = jnp.einsum('bhcsd,bhcsv->bhcdv', k * jnp.exp(b_last - b), v)
    dec = jnp.exp(b_last[:, :, :, 0])

    def step(s, inp):
        dec_c, u_c = inp
        return dec_c[..., None] * s + u_c, s

    s0 = jnp.zeros((B, H, GLA_DK, HEAD_DIM), jnp.float32)
    _, s_in = lax.scan(step, s0, (jnp.moveaxis(dec, 2, 0), jnp.moveaxis(u, 2, 0)))
    o = o + jnp.einsum('bhcld,bhcdv->bhclv', q_g, jnp.moveaxis(s_in, 0, 2))
    o = _rms(_from_chunks(o)) * norm_g
    return o.reshape(B, T, H * HEAD_DIM) * jax.nn.silu(og)


def _dsa_mixer(p, kv_norm_g, w_uk, w_uv, rel_bias, topk):
    B, T, _ = p.shape
    H, N = N_HEADS_MIX, HEAD_DIM
    q, ckv, qi, ki, wi = _split(p, DSA_SPLITS)
    c = _rms(ckv) * kv_norm_g
    k = c @ w_uk
    v = c @ w_uv
    n_blk = -(-T // Q_BLOCK)
    tq_len = n_blk * Q_BLOCK

    def blocks(t):
        t = jnp.pad(t, ((0, 0), (0, tq_len - T)) + ((0, 0),) * (t.ndim - 2))
        return jnp.moveaxis(t.reshape((B, n_blk, Q_BLOCK) + t.shape[2:]), 1, 0)

    q_b = blocks(q.reshape(B, T, H, N) * N ** -0.5)
    qi_b = blocks(qi.reshape(B, T, IDX_HEADS, IDX_DIM))
    wi_b = blocks(wi * (IDX_HEADS * IDX_DIM) ** -0.5)
    pos_b = jnp.arange(tq_len, dtype=jnp.int32).reshape(n_blk, Q_BLOCK)
    key_pos = jnp.arange(T, dtype=jnp.int32)
    gather = jax.vmap(lambda src, idx: src[idx])

    def attend(args):
        qb, qib, wib, tq = args
        rel = jax.nn.relu(jnp.einsum('bqhd,bsd->bqhs', qib, ki))
        score = jnp.einsum('bqhs,bqh->bqs', rel, wib)
        score = jnp.where(key_pos < N_META, jnp.inf, score)
        score = jnp.where(key_pos[None, :] <= tq[:, None], score, -jnp.inf)
        _, idx = lax.top_k(score, topk)
        valid = idx <= tq[None, :, None]
        k_sel = gather(k, idx)
        v_sel = gather(v, idx)
        logits = jnp.einsum('bqhd,bqkd->bhqk', qb, k_sel)
        bias = rel_bias[_t5_bucket(tq[None, :, None] - idx)]
        logits = logits + jnp.moveaxis(bias, -1, 1)
        logits = jnp.where(valid[:, None], logits, -jnp.inf)
        prob = jax.nn.softmax(logits, axis=-1)
        return jnp.einsum('bhqk,bqkd->bqhd', prob, v_sel)

    out = lax.map(attend, (q_b, qi_b, wi_b, pos_b))
    return jnp.moveaxis(out, 0, 1).reshape(B, tq_len, H * N)[:, :T]


def _mlstm_mixer(p, conv_w, conv_b, i_b, f_b, norm_g):
    B, T, _ = p.shape
    H, N = N_HEADS_MIX, HEAD_DIM
    q, k, v, ig, fg, og = _split(p, MLSTM_SPLITS)
    qk = jax.nn.silu(_causal_dwconv(jnp.concatenate([q, k], -1), conv_w, conv_b))
    q = _to_chunks(qk[..., :MIX_W], H)
    k = _to_chunks(qk[..., MIX_W:], H) * N ** -0.5
    v = _to_chunks(v, H)
    li = _to_chunks(ig + i_b, H, NEG)[..., 0]
    lf = _to_chunks(jax.nn.log_sigmoid(fg + f_b), H)[..., 0]
    b = jnp.cumsum(lf, -1)
    b_last = b[..., -1]
    g_loc = b_last[..., None] - b + li
    m_loc = jnp.max(g_loc, -1)
    w_loc = jnp.exp(g_loc - m_loc[..., None])
    c_loc = jnp.einsum('bhcs,bhcsd,bhcsv->bhcdv', w_loc, k, v)
    n_loc = jnp.einsum('bhcs,bhcsd->bhcd', w_loc, k)

    def step(carry, inp):
        c_st, n_st, m_st = carry
        bl, ml, cl, nl = inp
        m_new = jnp.maximum(bl + m_st, ml)
        s_old = jnp.exp(bl + m_st - m_new)
        s_new = jnp.exp(ml - m_new)
        return ((s_old[..., None, None] * c_st + s_new[..., None, None] * cl,
                 s_old[..., None] * n_st + s_new[..., None] * nl, m_new), (c_st, n_st, m_st))

    init = (jnp.zeros((B, H, N, N), jnp.float32), jnp.zeros((B, H, N), jnp.float32),
            jnp.zeros((B, H), jnp.float32))
    _, (c_in, n_in, m_in) = lax.scan(step, init, tuple(jnp.moveaxis(t, 2, 0) for t in (b_last, m_loc, c_loc, n_loc)))
    c_in, n_in, m_in = (jnp.moveaxis(t, 0, 2) for t in (c_in, n_in, m_in))
    causal = jnp.tril(jnp.ones((CHUNK, CHUNK), bool))
    d_log = jnp.where(causal, b[..., :, None] - b[..., None, :] + li[..., None, :], -jnp.inf)
    inter = b + m_in[..., None]
    m_t = jnp.maximum(inter, jnp.max(d_log, -1))
    s_w = jnp.exp(d_log - m_t[..., None]) * jnp.einsum('bhcld,bhcsd->bhcls', q, k)
    w_inter = jnp.exp(inter - m_t)
    num = (jnp.einsum('bhcls,bhcsv->bhclv', s_w, v)
           + w_inter[..., None] * jnp.einsum('bhcld,bhcdv->bhclv', q, c_in))
    den = jnp.sum(s_w, -1) + w_inter * jnp.einsum('bhcld,bhcd->bhcl', q, n_in)
    h = num / jnp.maximum(jnp.abs(den), jnp.exp(-m_t))[..., None]
    h = _from_chunks(h) * jax.nn.sigmoid(og).reshape(B, T, H, N)
    return _std_norm(h, 1e-5).reshape(B, T, H * N) * norm_g


def _hier_moe(x, w_grp, b_grp, w_rt, b_rt, w_gate, w_up, w_down):
    B, T, D = x.shape
    xf = x.reshape(B * T, D)
    g_logit = (xf @ w_grp).astype(jnp.float32) + b_grp
    g_sel = jnp.argmax(g_logit, -1)
    p_grp = jnp.take_along_axis(jax.nn.softmax(g_logit, -1), g_sel[:, None], 1)
    e_logit = ((xf @ w_rt).astype(jnp.float32) + b_rt).reshape(-1, N_GROUPS, EXPERTS_PER_GROUP)
    e_logit = jnp.take_along_axis(e_logit, g_sel[:, None, None], 1)[:, 0]
    top_val, top_idx = lax.top_k(e_logit, TOP_K_INNER)
    w = jax.nn.softmax(top_val, -1) * p_grp
    e_idx = g_sel[:, None] * EXPERTS_PER_GROUP + top_idx
    gate = jnp.einsum('nke,nk->ne', jax.nn.one_hot(e_idx, N_EXPERTS, dtype=jnp.float32), w).astype(x.dtype)
    y = jnp.zeros_like(xf)
    for e in range(N_EXPERTS):
        hid = jax.nn.silu(xf @ w_gate[e]) * (xf @ w_up[e])
        y = y + gate[:, e:e + 1] * (hid @ w_down[e])
    return y.reshape(B, T, D)


def setup_inputs(seed: int = 0) -> dict:
    key = jax.random.key(seed)
    keys = iter([jax.random.fold_in(key, i) for i in range(64)])

    def nrm(shape, scale):
        return jax.random.normal(next(keys), shape, jnp.float32) * scale

    def unif(shape, lo, hi):
        return jax.random.uniform(next(keys), shape, jnp.float32, minval=lo, maxval=hi)

    def gain(shape):
        return 1.0 + nrm(shape, 0.02)

    L, D, H, N, W = DEPTH, D_MODEL, N_HEADS_MIX, HEAD_DIM, MIX_W
    return {
        'x': nrm((BATCH, SEQ, D), 1.0),
        'meta': nrm((N_META, D), 1.0),
        'ln_in_g': gain((D,)),
        'ln_in_b': nrm((D,), 0.02),
        'rel_bias': nrm((N_BUCKETS, H), 0.5),
        'w_in': nrm((L, D, IN_COLS), D ** -0.5),
        'rwkv_mu': unif((L, sum(RWKV_SPLITS)), 0.0, 1.0),
        'rwkv_w_up': nrm((L, RWKV_LORA_W, W), RWKV_LORA_W ** -0.5),
        'rwkv_w0': unif((L, W), -6.0, 1.0),
        'rwkv_a_up': nrm((L, RWKV_LORA_A, W), RWKV_LORA_A ** -0.5),
        'rwkv_a0': nrm((L, W), 0.1),
        'rwkv_g_up': nrm((L, RWKV_LORA_G, W), RWKV_LORA_G ** -0.5),
        'rwkv_k_k': 0.85 + nrm((L, W), 0.02),
        'rwkv_k_a': gain((L, W)),
        'rwkv_r_k': nrm((L, H, N), 0.1),
        'rwkv_gn_g': gain((L, W)),
        'rwkv_gn_b': nrm((L, W), 0.02),
        'gla_a_up': nrm((L, GLA_LORA, H * GLA_DK), GLA_LORA ** -0.5),
        'gla_a_b': nrm((L, H * GLA_DK), 0.1),
        'gla_norm_g': gain((L, N)),
        'dsa_kv_norm_g': gain((L, DSA_KV_RANK)),
        'dsa_w_uk': nrm((L, DSA_KV_RANK, N), DSA_KV_RANK ** -0.5),
        'dsa_w_uv': nrm((L, DSA_KV_RANK, N), DSA_KV_RANK ** -0.5),
        'mlstm_conv_w': nrm((L, CONV_W, 2 * W), CONV_W ** -0.5),
        'mlstm_conv_b': nrm((L, 2 * W), 0.02),
        'mlstm_i_b': nrm((L, H), 0.5),
        'mlstm_f_b': unif((L, H), 3.0, 6.0),
        'mlstm_norm_g': gain((L, W)),
        'w_branch': nrm((L, N_BRANCH, W, D), DN_BETA * W ** -0.5),
        'w_out': nrm((L, D, D), DN_BETA * D ** -0.5),
        'ln1_g': gain((L, D)),
        'ln1_b': nrm((L, D), 0.02),
        'moe_w_grp': nrm((L, D, N_GROUPS), D ** -0.5),
        'moe_b_grp': nrm((L, N_GROUPS), 0.01),
        'moe_w_rt': nrm((L, D, N_EXPERTS), D ** -0.5),
        'moe_b_rt': nrm((L, N_EXPERTS), 0.01),
        'moe_w_gate': nrm((L, N_EXPERTS, D, D_EXPERT), D ** -0.5),
        'moe_w_up': nrm((L, N_EXPERTS, D, D_EXPERT), D ** -0.5),
        'moe_w_down': nrm((L, N_EXPERTS, D_EXPERT, D), DN_BETA * D_EXPERT ** -0.5),
        'ln2_g': gain((L, D)),
        'ln2_b': nrm((L, D), 0.02),
    }


def reference(x, meta, ln_in_g, ln_in_b, rel_bias, w_in,
              rwkv_mu, rwkv_w_up, rwkv_w0, rwkv_a_up, rwkv_a0, rwkv_g_up, rwkv_k_k, rwkv_k_a, rwkv_r_k,
              rwkv_gn_g, rwkv_gn_b,
              gla_a_up, gla_a_b, gla_norm_g,
              dsa_kv_norm_g, dsa_w_uk, dsa_w_uv,
              mlstm_conv_w, mlstm_conv_b, mlstm_i_b, mlstm_f_b, mlstm_norm_g,
              w_branch, w_out, ln1_g, ln1_b,
              moe_w_grp, moe_b_grp, moe_w_rt, moe_b_rt, moe_w_gate, moe_w_up, moe_w_down, ln2_g, ln2_b):
    B, S, D = x.shape
    topk = min(TOPK_MAX, S // 4)
    h = jnp.concatenate([jnp.broadcast_to(meta.astype(x.dtype), (B, N_META, D)), x], axis=1)
    h = _layer_norm(h, ln_in_g, ln_in_b)
    for l in range(DEPTH):
        p = (h @ w_in[l]).astype(jnp.float32)
        p_a, p_b, p_c, p_d, p_g = _split(p, IN_SPLITS)
        y_a = _rwkv7_mixer(p_a, rwkv_mu[l], rwkv_w_up[l], rwkv_w0[l], rwkv_a_up[l], rwkv_a0[l], rwkv_g_up[l],
                           rwkv_k_k[l], rwkv_k_a[l], rwkv_r_k[l], rwkv_gn_g[l], rwkv_gn_b[l])
        y_b = _gla_mixer(p_b, gla_a_up[l], gla_a_b[l], gla_norm_g[l])
        y_c = _dsa_mixer(p_c, dsa_kv_norm_g[l], dsa_w_uk[l], dsa_w_uv[l], rel_bias, topk)
        y_d = _mlstm_mixer(p_d, mlstm_conv_w[l], mlstm_conv_b[l], mlstm_i_b[l], mlstm_f_b[l], mlstm_norm_g[l])
        gates = jnp.split(jax.nn.sigmoid(p_g), N_BRANCH, axis=-1)
        merged = jnp.zeros_like(h)
        for i, y in enumerate((y_a, y_b, y_c, y_d)):
            merged = merged + gates[i].astype(h.dtype) * (y.astype(h.dtype) @ w_branch[l, i])
        h = _layer_norm(DN_ALPHA * h + merged @ w_out[l], ln1_g[l], ln1_b[l])
        moe_out = _hier_moe(h, moe_w_grp[l], moe_b_grp[l], moe_w_rt[l], moe_b_rt[l],
                            moe_w_gate[l], moe_w_up[l], moe_w_down[l])
        h = _layer_norm(DN_ALPHA * h + moe_out, ln2_g[l], ln2_b[l])
    return h[:, N_META:]
```

```python
import functools
import math

import numpy as np
import jax
import jax.numpy as jnp
from jax import lax
from jax.experimental import pallas as pl
from jax.experimental.pallas import tpu as pltpu

F32 = jnp.float32
BF16 = jnp.bfloat16

D_MODEL = 1024
HEAD_DIM = 64
N_HEADS = 4
MIX_W = 256
N_META = 16
CHUNK = 64
LANES = 128
ROW_TILE = 128
FRONT = ROW_TILE
FP = FRONT - N_META
NEG = -1e30
LN_EPS = 1e-5
DEPTH = 2
DN_ALPHA = (2 * DEPTH) ** 0.25

RWKV_GN_EPS = HEAD_DIM * 1e-5
GLA_DK = 32
GLA_TAU = 16.0
DSA_KV_RANK = 128
IDX_HEADS = 8
IDX_DIM = 32
TOPK_MAX = 256
N_BUCKETS = 32
MAX_DISTANCE = 128
CONV_W = 4
N_GROUPS = 4
EPG = 4
N_EXPERTS = 16
D_EXPERT = 256

INT_MIN = -(2 ** 31)
VMEM_LIMIT = 56 * 1024 * 1024


def _cparams(*sem):
    return pltpu.CompilerParams(dimension_semantics=tuple(sem), vmem_limit_bytes=VMEM_LIMIT)


def _pick_tile(n, target):
    best = LANES
    t = LANES
    while t <= min(n, target):
        if n % t == 0:
            best = t
        t += LANES
    return best


def _bdot(a, b):
    return jnp.dot(a.astype(BF16), b.astype(BF16), preferred_element_type=F32)


def _bdot_nt(a, b):
    return lax.dot_general(a.astype(BF16), b.astype(BF16), (((1,), (1,)), ((), ())),
                           preferred_element_type=F32)


def _bdot_tn(a, b):
    return lax.dot_general(a.astype(BF16), b.astype(BF16), (((0,), (0,)), ((), ())),
                           preferred_element_type=F32)


def _split(a):
    hi = a.astype(BF16)
    lo = (a - hi.astype(F32)).astype(BF16)
    return hi, lo


_NN = (((1,), (0,)), ((), ()))
_NT = (((1,), (1,)), ((), ()))
_TN = (((0,), (0,)), ((), ()))


def _dot3(a, b, dims=_NN):
    ah, al = _split(a)
    bh, bl = _split(b)
    dg = lambda x, y: lax.dot_general(x, y, dims, preferred_element_type=F32)
    return dg(ah, bh) + (dg(ah, bl) + dg(al, bh))


def _dot_exact_lhs(a_bf16, b):
    bh, bl = _split(b)
    return (jnp.dot(a_bf16, bh, preferred_element_type=F32)
            + jnp.dot(a_bf16, bl, preferred_element_type=F32))


def _dot_exact_rhs(a, b_bf16):
    ah, al = _split(a)
    return (jnp.dot(ah, b_bf16, preferred_element_type=F32)
            + jnp.dot(al, b_bf16, preferred_element_type=F32))


def _sigmoid(x):
    return 1.0 / (1.0 + jnp.exp(-x))


def _log_sigmoid(x):
    return jnp.minimum(x, 0.0) - jnp.log(1.0 + jnp.exp(-jnp.abs(x)))


def _silu(x):
    return x * _sigmoid(x)


def _iota(shape, dim):
    return lax.broadcasted_iota(jnp.int32, shape, dim)


def _tri_incl(n):
    return (_iota((n, n), 1) <= _iota((n, n), 0))


def _head_ones():
    return ((_iota((MIX_W, MIX_W), 0) // HEAD_DIM) == (_iota((MIX_W, MIX_W), 1) // HEAD_DIM)).astype(BF16)


def _row_ids(rows):
    return pl.program_id(1) * ROW_TILE + _iota((rows, 1), 0)


def _embed_kernel(x_ref, meta_ref, g_ref, b_ref, h_ref, hb_ref):
    j = pl.program_id(1)
    src = jnp.where(j == 0, meta_ref[...], x_ref[0])
    mu = jnp.mean(src, -1, keepdims=True)
    xc = src - mu
    var = jnp.mean(xc * xc, -1, keepdims=True)
    y = xc * lax.rsqrt(var + LN_EPS) * g_ref[...] + b_ref[...]
    h_ref[0] = y
    hb_ref[0] = y.astype(BF16)


def _embed(x, meta, g, b):
    B, S, D = x.shape
    TP = S + FRONT
    meta_pad = jnp.concatenate([jnp.zeros((FP, D), F32), meta.astype(F32)], axis=0)
    return pl.pallas_call(
        _embed_kernel,
        grid=(B, TP // ROW_TILE),
        in_specs=[
            pl.BlockSpec((1, ROW_TILE, D), lambda b, j: (b, jnp.maximum(j - 1, 0), 0)),
            pl.BlockSpec((ROW_TILE, D), lambda b, j: (0, 0)),
            pl.BlockSpec((1, D), lambda b, j: (0, 0)),
            pl.BlockSpec((1, D), lambda b, j: (0, 0)),
        ],
        out_specs=[
            pl.BlockSpec((1, ROW_TILE, D), lambda b, j: (b, j, 0)),
            pl.BlockSpec((1, ROW_TILE, D), lambda b, j: (b, j, 0)),
        ],
        out_shape=[jax.ShapeDtypeStruct((B, TP, D), F32), jax.ShapeDtypeStruct((B, TP, D), BF16)],
        compiler_params=_cparams("parallel", "arbitrary"),
        name="embed_ln",
    )(x, meta_pad, g.reshape(1, D), b.reshape(1, D))


def _proj_kernel(h_ref, w_ref, o_ref, *, act):
    y = jnp.dot(h_ref[...], w_ref[...], preferred_element_type=F32)
    if act == "sigmoid":
        y = _sigmoid(y)
    o_ref[...] = y.astype(o_ref.dtype)


def _proj(hb, w, act=None, out_dtype=F32):
    N, D = hb.shape
    W = w.shape[1]
    tn = W if W <= 1152 else 1024
    tm = _pick_tile(N, 640)
    return pl.pallas_call(
        functools.partial(_proj_kernel, act=act),
        grid=(W // tn, N // tm),
        in_specs=[pl.BlockSpec((tm, D), lambda j, i: (i, 0)),
                  pl.BlockSpec((D, tn), lambda j, i: (0, j))],
        out_specs=pl.BlockSpec((tm, tn), lambda j, i: (i, j)),
        out_shape=jax.ShapeDtypeStruct((N, W), out_dtype),
        compiler_params=_cparams("arbitrary", "arbitrary"),
        name="in_proj",
    )(hb, w)


def _rwkv_kernel(p_ref, mu_ref, wup_ref, w0_ref, aup_ref, a0_ref, gup_ref, kk_ref, ka_ref, rk_ref,
                 gng_ref, gnb_ref, y_ref, carry_ref, s_ref):
    j = pl.program_id(1)

    @pl.when(j == 0)
    def _():
        carry_ref[...] = jnp.zeros_like(carry_ref)
        s_ref[...] = jnp.zeros_like(s_ref)

    rows = _row_ids(ROW_TILE)
    valid = rows >= FP
    p = jnp.where(valid, p_ref[0], 0.0)
    prev = jnp.where(_iota((ROW_TILE, 1), 0) == 0, carry_ref[...], pltpu.roll(p, 1, 0))
    carry_ref[...] = p[ROW_TILE - 1:ROW_TILE, :]
    ps = p + (prev - p) * mu_ref[...]

    r = ps[:, 0:256]
    k = ps[:, 256:512]
    v = ps[:, 512:768]
    lora_in = ps[:, 768:896]
    xg = ps[:, 896:1024]
    w_log = _log_sigmoid(w0_ref[...] + _bdot(jnp.tanh(lora_in), wup_ref[...])) - 0.5
    lw = jnp.where(valid, -jnp.exp(w_log), 0.0)
    alpha = _sigmoid(a0_ref[...] + _bdot(lora_in, aup_ref[...]))
    gate = _bdot(_sigmoid(xg), gup_ref[...])

    ones_h = _head_ones()
    kk = k * kk_ref[...]
    kk = kk / jnp.maximum(jnp.sqrt(_dot_exact_rhs(kk * kk, ones_h)), 1e-12)
    k = k * (1.0 + (alpha - 1.0) * ka_ref[...])
    kka = kk * alpha

    tri = _tri_incl(CHUNK)
    tri_b = tri.astype(BF16)
    strict = _iota((CHUNK, CHUNK), 1) < _iota((CHUNK, CHUNK), 0)
    eye = (_iota((CHUNK, CHUNK), 1) == _iota((CHUNK, CHUNK), 0)).astype(F32)

    y_chunks = []
    for c in range(ROW_TILE // CHUNK):
        sl = slice(c * CHUNK, (c + 1) * CHUNK)
        lw_c = lw[sl]
        cum = _dot_exact_lhs(tri_b, lw_c)
        cum_last = cum[CHUNK - 1:CHUNK, :]
        p_in = jnp.exp(cum)
        p_inv = jnp.exp(-cum)
        p_ex = jnp.exp(cum - lw_c)
        p_tail = jnp.exp(cum_last - cum)
        a_t = -kk[sl] * p_ex
        b_t = kka[sl] * p_inv
        k_t = k[sl] * p_inv
        r_t = r[sl] * p_in
        k_b = k[sl] * p_tail
        b_b = kka[sl] * p_tail
        p_last = jnp.exp(cum_last)
        v_c = v[sl]
        y_heads = []
        for h in range(N_HEADS):
            hs = slice(h * HEAD_DIM, (h + 1) * HEAD_DIM)
            a_h, b_h, k_h, r_h, v_h = a_t[:, hs], b_t[:, hs], k_t[:, hs], r_t[:, hs], v_c[:, hs]
            s_h = s_ref[h]
            a_ab = jnp.where(strict, _dot3(a_h, b_h, _NT), 0.0)
            a_ak = jnp.where(strict, _dot3(a_h, k_h, _NT), 0.0)
            a_rb = jnp.where(tri, _dot3(r_h, b_h, _NT), 0.0)
            a_rk = jnp.where(tri, _dot3(r_h, k_h, _NT), 0.0)
            inv = eye + a_ab
            pw = a_ab
            for _ in range(5):
                pw = _dot3(pw, pw)
                inv = inv + _dot3(inv, pw)
            u = _dot3(inv, _dot3(a_h, s_h, _NT) + _dot3(a_ak, v_h))
            y_heads.append(_dot3(r_h, s_h, _NT) + _dot3(a_rk, v_h) + _dot3(a_rb, u))
            s_ref[h] = (s_h * p_last[:, hs] + _dot3(v_h, k_b[:, hs], _TN) + _dot3(u, b_b[:, hs], _TN))
        y_chunks.append(jnp.concatenate(y_heads, axis=1))
    y = jnp.concatenate(y_chunks, axis=0)

    mean = _dot_exact_rhs(y, ones_h) * (1.0 / HEAD_DIM)
    yc = y - mean
    var = _dot_exact_rhs(yc * yc, ones_h) * (1.0 / HEAD_DIM)
    yn = yc * lax.rsqrt(var + RWKV_GN_EPS) * gng_ref[...] + gnb_ref[...]
    bonus = _dot_exact_rhs(r * k * rk_ref[...], ones_h) * v
    y_ref[0] = ((yn + bonus) * gate).astype(y_ref.dtype)


def _rwkv(p_a, B, TP, mu, w_up, w0, a_up, a0, g_up, k_k, k_a, r_k, gn_g, gn_b):
    W = MIX_W
    z64 = jnp.zeros((64, W), F32)
    wup_pad = jnp.concatenate([w_up, z64], axis=0).astype(BF16)
    aup_pad = jnp.concatenate([z64, a_up], axis=0).astype(BF16)
    row = lambda a: a.reshape(1, -1).astype(F32)
    full = lambda shape: pl.BlockSpec(shape, lambda b, j: (0,) * len(shape))
    return pl.pallas_call(
        _rwkv_kernel,
        grid=(B, TP // ROW_TILE),
        in_specs=[pl.BlockSpec((1, ROW_TILE, 1024), lambda b, j: (b, j, 0)),
                  full((1, 1024)), full((128, W)), full((1, W)), full((128, W)), full((1, W)),
                  full((128, W)), full((1, W)), full((1, W)), full((1, W)), full((1, W)), full((1, W))],
        out_specs=pl.BlockSpec((1, ROW_TILE, W), lambda b, j: (b, j, 0)),
        out_shape=jax.ShapeDtypeStruct((B, TP, W), BF16),
        scratch_shapes=[pltpu.VMEM((1, 1024), F32), pltpu.VMEM((N_HEADS, HEAD_DIM, HEAD_DIM), F32)],
        compiler_params=_cparams("parallel", "arbitrary"),
        name="rwkv7",
    )(p_a.reshape(B, TP, 1024), row(mu), wup_pad, row(w0), aup_pad, row(a0), g_up.astype(BF16),
      row(k_k), row(k_a), row(r_k), row(gn_g), row(gn_b))


def _gla_kernel(p_ref, aup_ref, ab_ref, ng_ref, y_ref, s_ref):
    j = pl.program_id(1)

    @pl.when(j == 0)
    def _():
        s_ref[...] = jnp.zeros_like(s_ref)

    valid = _row_ids(ROW_TILE) >= FP
    p = jnp.where(valid, p_ref[0], 0.0)
    q = p[:, 0:128] * (GLA_DK ** -0.5)
    k = p[:, 128:256]
    v = p[:, 256:512]
    og = p[:, 512:768]
    la = _log_sigmoid(_bdot(p[:, 768:896], aup_ref[...]) + ab_ref[...]) * (1.0 / GLA_TAU)
    la = jnp.where(valid, la, 0.0)

    tri = _tri_incl(CHUNK)
    tri_b = tri.astype(BF16)
    o_chunks = []
    for c in range(ROW_TILE // CHUNK):
        sl = slice(c * CHUNK, (c + 1) * CHUNK)
        b = _dot_exact_lhs(tri_b, la[sl])
        b_last = b[CHUNK - 1:CHUNK, :]
        q_g = q[sl] * jnp.exp(b)
        k_g = k[sl] * jnp.exp(-b)
        k_l = k[sl] * jnp.exp(b_last - b)
        dec = jnp.exp(b_last)
        v_c = v[sl]
        o_heads = []
        for h in range(N_HEADS):
            ks = slice(h * GLA_DK, (h + 1) * GLA_DK)
            vs = slice(h * HEAD_DIM, (h + 1) * HEAD_DIM)
            s_h = s_ref[h]
            att = jnp.where(tri, _bdot_nt(q_g[:, ks], k_g[:, ks]), 0.0)
            o_heads.append(_bdot(att, v_c[:, vs]) + _bdot_nt(q_g[:, ks], s_h))
            s_ref[h] = s_h * dec[:, ks] + _bdot_tn(v_c[:, vs], k_l[:, ks])
        o_chunks.append(jnp.concatenate(o_heads, axis=1))
    o = jnp.concatenate(o_chunks, axis=0)
    ms = _dot_exact_rhs(o * o, _head_ones()) * (1.0 / HEAD_DIM)
    y = o * lax.rsqrt(ms + 1e-6) * ng_ref[...] * _silu(og)
    y_ref[0] = y.astype(y_ref.dtype)


def _gla(p_b, B, TP, a_up, a_b, norm_g):
    aup_pad = jnp.zeros((128, 128), F32).at[:a_up.shape[0]].set(a_up).astype(BF16)
    full = lambda shape: pl.BlockSpec(shape, lambda b, j: (0,) * len(shape))
    return pl.pallas_call(
        _gla_kernel,
        grid=(B, TP // ROW_TILE),
        in_specs=[pl.BlockSpec((1, ROW_TILE, 896), lambda b, j: (b, j, 0)),
                  full((128, 128)), full((1, 128)), full((1, MIX_W))],
        out_specs=pl.BlockSpec((1, ROW_TILE, MIX_W), lambda b, j: (b, j, 0)),
        out_shape=jax.ShapeDtypeStruct((B, TP, MIX_W), BF16),
        scratch_shapes=[pltpu.VMEM((N_HEADS, HEAD_DIM, GLA_DK), F32)],
        compiler_params=_cparams("parallel", "arbitrary"),
        name="gla",
    )(p_b.reshape(B, TP, 896), aup_pad, a_b.reshape(1, 128).astype(F32),
      jnp.tile(norm_g.astype(F32), N_HEADS).reshape(1, MIX_W))


def _mlstm_kernel(p_ref, cw_ref, cb_ref, ib_ref, fb_ref, ng_ref, y_ref, carry_ref, c_ref, n_ref, m_ref):
    j = pl.program_id(1)

    @pl.when(j == 0)
    def _():
        carry_ref[...] = jnp.zeros_like(carry_ref)
        c_ref[...] = jnp.zeros_like(c_ref)
        n_ref[...] = jnp.zeros_like(n_ref)
        m_ref[...] = jnp.zeros_like(m_ref)

    valid = _row_ids(ROW_TILE) >= FP
    p = jnp.where(valid, p_ref[0], 0.0)
    a = p[:, 0:512]
    ext = jnp.concatenate([carry_ref[...], a], axis=0)
    carry_ref[...] = a[ROW_TILE - 8:ROW_TILE, :]
    conv = cb_ref[...] + a * cw_ref[CONV_W - 1:CONV_W, :]
    for s in range(1, CONV_W):
        conv = conv + pltpu.roll(ext, s, 0)[8:8 + ROW_TILE, :] * cw_ref[CONV_W - 1 - s:CONV_W - s, :]
    qk = _silu(conv)
    q = jnp.where(valid, qk[:, 0:MIX_W], 0.0)
    k = jnp.where(valid, qk[:, MIX_W:2 * MIX_W], 0.0) * (HEAD_DIM ** -0.5)
    v = p[:, 512:768]
    og = p[:, 768:1024]
    gates = p[:, 1024:1152]
    li_all = jnp.where(valid, gates + ib_ref[...], NEG)
    lf_all = jnp.where(valid, _log_sigmoid(gates + fb_ref[...]), 0.0)

    tri = _tri_incl(CHUNK)
    tri_b = tri.astype(BF16)
    h_chunks = []
    for c in range(ROW_TILE // CHUNK):
        sl = slice(c * CHUNK, (c + 1) * CHUNK)
        li_c = li_all[sl]
        b_c = _dot_exact_lhs(tri_b, lf_all[sl])
        li_t = li_c.T
        b_t = b_c.T
        h_heads = []
        for h in range(N_HEADS):
            hs = slice(h * HEAD_DIM, (h + 1) * HEAD_DIM)
            q_h, k_h, v_h = q[sl, hs], k[sl, hs], v[sl, hs]
            b_col = b_c[:, N_HEADS + h:N_HEADS + h + 1]
            b_row = b_t[N_HEADS + h:N_HEADS + h + 1, :]
            li_col = li_c[:, h:h + 1]
            li_row = li_t[h:h + 1, :]
            b_last = b_col[CHUNK - 1:CHUNK, :]
            c_in, n_in, m_in = c_ref[h], n_ref[h], m_ref[h]
            d_log = jnp.where(tri, b_col - b_row + li_row, -jnp.inf)
            inter = b_col + m_in
            m_t = jnp.maximum(inter, jnp.max(d_log, axis=1, keepdims=True))
            s_w = jnp.exp(d_log - m_t) * _bdot_nt(q_h, k_h)
            w_inter = jnp.exp(inter - m_t)
            num = _bdot(s_w, v_h) + w_inter * _bdot(q_h, c_in)
            den = jnp.sum(s_w, axis=1, keepdims=True) + w_inter * jnp.sum(q_h * n_in, axis=1, keepdims=True)
            h_heads.append(num / jnp.maximum(jnp.abs(den), jnp.exp(-m_t)))
            g_loc = b_last - b_col + li_col
            m_loc = jnp.max(g_loc, axis=0, keepdims=True)
            kw = k_h * jnp.exp(g_loc - m_loc)
            m_new = jnp.maximum(b_last + m_in, m_loc)
            s_old = jnp.exp(b_last + m_in - m_new)
            s_new = jnp.exp(m_loc - m_new)
            c_ref[h] = s_old * c_in + s_new * _bdot_tn(kw, v_h)
            n_ref[h] = s_old * n_in + s_new * jnp.sum(kw, axis=0, keepdims=True)
            m_ref[h] = m_new
        h_chunks.append(jnp.concatenate(h_heads, axis=1))
    hh = jnp.concatenate(h_chunks, axis=0) * _sigmoid(og)
    ones_h = _head_ones()
    mean = _dot_exact_rhs(hh, ones_h) * (1.0 / HEAD_DIM)
    hc = hh - mean
    var = _dot_exact_rhs(hc * hc, ones_h) * (1.0 / HEAD_DIM)
    y_ref[0] = (hc * lax.rsqrt(var + 1e-5) * ng_ref[...]).astype(y_ref.dtype)


def _mlstm(p_d, B, TP, conv_w, conv_b, i_b, f_b, norm_g):
    ib = jnp.zeros((1, LANES), F32).at[0, 0:N_HEADS].set(i_b)
    fb = jnp.zeros((1, LANES), F32).at[0, N_HEADS:2 * N_HEADS].set(f_b)
    full = lambda shape: pl.BlockSpec(shape, lambda b, j: (0,) * len(shape))
    return pl.pallas_call(
        _mlstm_kernel,
        grid=(B, TP // ROW_TILE),
        in_specs=[pl.BlockSpec((1, ROW_TILE, 1152), lambda b, j: (b, j, 0)),
                  full((CONV_W, 512)), full((1, 512)), full((1, LANES)), full((1, LANES)), full((1, MIX_W))],
        out_specs=pl.BlockSpec((1, ROW_TILE, MIX_W), lambda b, j: (b, j, 0)),
        out_shape=jax.ShapeDtypeStruct((B, TP, MIX_W), BF16),
        scratch_shapes=[pltpu.VMEM((8, 512), F32),
                        pltpu.VMEM((N_HEADS, HEAD_DIM, HEAD_DIM), F32),
                        pltpu.VMEM((N_HEADS, 1, HEAD_DIM), F32),
                        pltpu.VMEM((N_HEADS, 1, 1), F32)],
        compiler_params=_cparams("parallel", "arbitrary"),
        name="mlstm",
    )(p_d.reshape(B, TP, 1152), conv_w.astype(F32), conv_b.reshape(1, 512).astype(F32), ib, fb,
      norm_g.reshape(1, MIX_W).astype(F32))


def _dsa_prep_kernel(h_ref, w_ref, wkit_ref, kvg_ref, wukt_ref, wuv_ref,
                     q_ref, qi_ref, wi_ref, kt_ref, v_ref, kit_ref):
    hb = h_ref[...]
    tm = hb.shape[0]
    p = jnp.dot(hb, w_ref[...], preferred_element_type=F32)
    q_ref[...] = (p[:, 0:256] * (HEAD_DIM ** -0.5)).astype(BF16)
    qi_ref[...] = p[:, 256:512].astype(BF16)
    wi_ref[...] = p[:, 640:768] * ((IDX_HEADS * IDX_DIM) ** -0.5)
    ckv = p[:, 512:640]
    c = ckv * lax.rsqrt(jnp.mean(ckv * ckv, -1, keepdims=True) + 1e-6) * kvg_ref[...]
    cb = c.astype(BF16)
    kt = lax.dot_general(wukt_ref[...], cb, _NT, preferred_element_type=F32)
    kit = lax.dot_general(wkit_ref[...], hb, _NT, preferred_element_type=F32)
    v_ref[...] = jnp.dot(cb, wuv_ref[...], preferred_element_type=F32).astype(BF16)
    for t in range(tm // LANES):
        kt_ref[t] = kt[:, t * LANES:(t + 1) * LANES].astype(BF16)
        kit_ref[t] = kit[:, t * LANES:(t + 1) * LANES].astype(BF16)


def _dsa_prep(hb, w_c, w_kit, kv_norm_g, w_uk, w_uv):
    N, D = hb.shape
    tm = _pick_tile(N, 640)
    nt = tm // LANES
    full = lambda shape: pl.BlockSpec(shape, lambda i: (0,) * len(shape))
    return pl.pallas_call(
        _dsa_prep_kernel,
        grid=(N // tm,),
        in_specs=[pl.BlockSpec((tm, D), lambda i: (i, 0)),
                  full((D, 768)), full((IDX_DIM, D)), full((1, DSA_KV_RANK)),
                  full((HEAD_DIM, DSA_KV_RANK)), full((DSA_KV_RANK, HEAD_DIM))],
        out_specs=[pl.BlockSpec((tm, 256), lambda i: (i, 0)),
                   pl.BlockSpec((tm, 256), lambda i: (i, 0)),
                   pl.BlockSpec((tm, LANES), lambda i: (i, 0)),
                   pl.BlockSpec((nt, HEAD_DIM, LANES), lambda i: (i, 0, 0)),
                   pl.BlockSpec((tm, HEAD_DIM), lambda i: (i, 0)),
                   pl.BlockSpec((nt, IDX_DIM, LANES), lambda i: (i, 0, 0))],
        out_shape=[jax.ShapeDtypeStruct((N, 256), BF16),
                   jax.ShapeDtypeStruct((N, 256), BF16),
                   jax.ShapeDtypeStruct((N, LANES), F32),
                   jax.ShapeDtypeStruct((N // LANES, HEAD_DIM, LANES), BF16),
                   jax.ShapeDtypeStruct((N, HEAD_DIM), BF16),
                   jax.ShapeDtypeStruct((N // LANES, IDX_DIM, LANES), BF16)],
        compiler_params=_cparams("arbitrary"),
        name="dsa_prep",
    )(hb, w_c, w_kit, kv_norm_g.reshape(1, DSA_KV_RANK).astype(F32),
      w_uk.T.astype(BF16), w_uv.astype(BF16))


def _dsa_kernel(q_ref, qi_ref, wi_ref, kt_ref, v_ref, kit_ref, bias_ref, y_ref,
                sk_ref, tau_ref, jmax_ref, m_ref, l_ref, acc_ref, *, topk):
    i = pl.program_id(1)
    nk = i + 1
    QT = ROW_TILE
    t_row = i * QT + _iota((QT, 1), 0)

    qi = qi_ref[...]
    qi_stack = jnp.concatenate([qi[:, h * IDX_DIM:(h + 1) * IDX_DIM] for h in range(IDX_HEADS)], axis=0)
    wi = wi_ref[...]
    wi_cols = [wi[:, h:h + 1] for h in range(IDX_HEADS)]

    def score_body(kt, carry):
        rel = jnp.dot(qi_stack, kit_ref[kt], preferred_element_type=F32)
        score = jnp.zeros((QT, LANES), F32)
        for h in range(IDX_HEADS):
            score = score + jnp.maximum(rel[h * QT:(h + 1) * QT, :], 0.0) * wi_cols[h]
        score = jnp.where(score == 0.0, 0.0, score)
        s_pos = kt * LANES + _iota((QT, LANES), 1)
        score = jnp.where(s_pos < FP + N_META, jnp.inf, score)
        bits = lax.bitcast_convert_type(score, jnp.int32)
        key = jnp.where(bits < 0, bits ^ jnp.int32(0x7FFFFFFF), bits)
        ok = (s_pos >= FP) & (s_pos <= t_row)
        sk_ref[kt] = jnp.where(ok, key, jnp.int32(INT_MIN))
        return carry

    lax.fori_loop(0, nk, score_body, 0)

    def count(pred_fn):
        def body(kt, acc):
            return acc + jnp.where(pred_fn(sk_ref[kt], kt), 1, 0)
        acc = lax.fori_loop(0, nk, body, jnp.zeros((QT, LANES), jnp.int32))
        return jnp.sum(acc, axis=1, keepdims=True)

    def bit_body(it, tau):
        cand = tau + jnp.left_shift(jnp.int32(1), 31 - it)
        cnt = count(lambda sk, kt: sk >= cand)
        return jnp.where(cnt >= topk, cand, tau)

    tau = lax.fori_loop(0, 32, bit_body, jnp.full((QT, 1), INT_MIN, jnp.int32))
    tau = jnp.maximum(tau, jnp.int32(INT_MIN + 1))
    tau_ref[...] = tau
    n_gt = count(lambda sk, kt: sk > tau)
    n_ge = count(lambda sk, kt: sk >= tau)
    need = topk - n_gt
    jmax_ref[...] = jnp.full((QT, 1), 2 ** 30, jnp.int32)

    @pl.when(jnp.max(n_ge - topk) > 0)
    def _():
        n_bits = max(1, int(math.ceil(math.log2(sk_ref.shape[0] * LANES + 1))))

        def pos_body(it, x):
            cand = x + jnp.left_shift(jnp.int32(1), n_bits - 1 - it)
            cnt = count(lambda sk, kt: (sk == tau) & (kt * LANES + _iota((QT, LANES), 1) < cand))
            return jnp.where(cnt < need, cand, x)

        x = lax.fori_loop(0, n_bits, pos_body, jnp.zeros((QT, 1), jnp.int32))
        jmax_ref[...] = jnp.where(n_ge > topk, x, jnp.int32(2 ** 30))

    jmax = jmax_ref[...]

    q = q_ref[...]
    q_stack = jnp.concatenate([q[:, h * HEAD_DIM:(h + 1) * HEAD_DIM] for h in range(N_HEADS)], axis=0)
    m_ref[...] = jnp.full(m_ref.shape, NEG, F32)
    l_ref[...] = jnp.zeros(l_ref.shape, F32)
    acc_ref[...] = jnp.zeros(acc_ref.shape, F32)

    def att_body(kt, carry):
        sk = sk_ref[kt]
        s_pos = kt * LANES + _iota((QT, LANES), 1)
        sel = (sk > tau) | ((sk == tau) & (s_pos <= jmax))
        logits = jnp.dot(q_stack, kt_ref[kt], preferred_element_type=F32)
        logits = logits.reshape(N_HEADS, QT, LANES) + bias_ref[jnp.minimum(i - kt, 2)]
        logits = jnp.where(sel[None], logits, NEG).reshape(N_HEADS * QT, LANES)
        m_old = m_ref[...]
        m_new = jnp.maximum(m_old, jnp.max(logits, axis=1, keepdims=True))
        pr = jnp.exp(logits - m_new)
        pr = jnp.where(jnp.concatenate([sel] * N_HEADS, axis=0), pr, 0.0)
        corr = jnp.exp(m_old - m_new)
        l_ref[...] = corr * l_ref[...] + jnp.sum(pr, axis=1, keepdims=True)
        v_t = v_ref[pl.ds(pl.multiple_of(kt * LANES, LANES), LANES), :]
        acc_ref[...] = corr * acc_ref[...] + jnp.dot(pr.astype(BF16), v_t, preferred_element_type=F32)
        m_ref[...] = m_new
        return carry

    lax.fori_loop(0, nk, att_body, 0)
    out = acc_ref[...] / jnp.maximum(l_ref[...], 1e-30)
    y_ref[...] = jnp.concatenate([out[h * QT:(h + 1) * QT, :] for h in range(N_HEADS)], axis=1).astype(y_ref.dtype)


def _t5_bucket(dist):
    max_exact = N_BUCKETS // 2
    n = jnp.maximum(dist, 0)
    large = max_exact + (jnp.log(jnp.maximum(n, 1).astype(F32) / max_exact)
                         / math.log(MAX_DISTANCE / max_exact) * (N_BUCKETS - max_exact)).astype(jnp.int32)
    return jnp.where(n < max_exact, n, jnp.minimum(large, N_BUCKETS - 1))


def _bias_tables(rel_bias):
    per_dist = rel_bias[_t5_bucket(jnp.arange(2 * ROW_TILE, dtype=jnp.int32))]
    ii = np.arange(ROW_TILE)[:, None] - np.arange(ROW_TILE)[None, :]
    tabs = [per_dist[np.clip(r * ROW_TILE + ii, 0, 2 * ROW_TILE - 1)] for r in (0, 1)]
    tabs.append(jnp.broadcast_to(per_dist[2 * ROW_TILE - 1], (ROW_TILE, ROW_TILE, N_HEADS)))
    return jnp.stack(tabs).transpose(0, 3, 1, 2).astype(F32)


def _dsa(q, qi, wi, kt, v, kit, bias_tab, B, TP, topk):
    nq = TP // ROW_TILE
    return pl.pallas_call(
        functools.partial(_dsa_kernel, topk=topk),
        grid=(B, nq),
        in_specs=[pl.BlockSpec((ROW_TILE, 256), lambda b, i: (b * nq + i, 0)),
                  pl.BlockSpec((ROW_TILE, 256), lambda b, i: (b * nq + i, 0)),
                  pl.BlockSpec((ROW_TILE, LANES), lambda b, i: (b * nq + i, 0)),
                  pl.BlockSpec((nq, HEAD_DIM, LANES), lambda b, i: (b, 0, 0)),
                  pl.BlockSpec((TP, HEAD_DIM), lambda b, i: (b, 0)),
                  pl.BlockSpec((nq, IDX_DIM, LANES), lambda b, i: (b, 0, 0)),
                  pl.BlockSpec((3, N_HEADS, ROW_TILE, LANES), lambda b, i: (0, 0, 0, 0))],
        out_specs=pl.BlockSpec((ROW_TILE, MIX_W), lambda b, i: (b * nq + i, 0)),
        out_shape=jax.ShapeDtypeStruct((B * TP, MIX_W), BF16),
        scratch_shapes=[pltpu.VMEM((nq, ROW_TILE, LANES), jnp.int32),
                        pltpu.VMEM((ROW_TILE, 1), jnp.int32),
                        pltpu.VMEM((ROW_TILE, 1), jnp.int32),
                        pltpu.VMEM((N_HEADS * ROW_TILE, 1), F32),
                        pltpu.VMEM((N_HEADS * ROW_TILE, 1), F32),
                        pltpu.VMEM((N_HEADS * ROW_TILE, HEAD_DIM), F32)],
        compiler_params=_cparams("parallel", "arbitrary"),
        name="dsa_attend",
    )(q, qi, wi, kt, v, kit, bias_tab)


def _layer_norm_rows(z, g, b):
    mu = jnp.mean(z, -1, keepdims=True)
    zc = z - mu
    var = jnp.mean(zc * zc, -1, keepdims=True)
    return zc * lax.rsqrt(var + LN_EPS) * g + b


def _merge_kernel(h_ref, g_ref, ya_ref, yb_ref, yc_ref, yd_ref, wb_ref, wo_ref, lg_ref, lb_ref,
                  h1_ref, h1b_ref):
    merged = None
    for i, y_ref in enumerate((ya_ref, yb_ref, yc_ref, yd_ref)):
        t = g_ref[:, i * D_MODEL:(i + 1) * D_MODEL] * jnp.dot(y_ref[...], wb_ref[i], preferred_element_type=F32)
        merged = t if merged is None else merged + t
    z = DN_ALPHA * h_ref[...] + jnp.dot(merged.astype(BF16), wo_ref[...], preferred_element_type=F32)
    y = _layer_norm_rows(z, lg_ref[...], lb_ref[...])
    h1_ref[...] = y
    h1b_ref[...] = y.astype(BF16)


def _merge(h, gates, ys, w_branch, w_out, ln_g, ln_b):
    N, D = h.shape
    tm = _pick_tile(N, 512)
    full = lambda shape: pl.BlockSpec(shape, lambda i: (0,) * len(shape))
    tok = lambda w: pl.BlockSpec((tm, w), lambda i: (i, 0))
    return pl.pallas_call(
        _merge_kernel,
        grid=(N // tm,),
        in_specs=[tok(D), tok(4 * D), tok(MIX_W), tok(MIX_W), tok(MIX_W), tok(MIX_W),
                  full((4, MIX_W, D)), full((D, D)), full((1, D)), full((1, D))],
        out_specs=[tok(D), tok(D)],
        out_shape=[jax.ShapeDtypeStruct((N, D), F32), jax.ShapeDtypeStruct((N, D), BF16)],
        compiler_params=_cparams("arbitrary"),
        name="merge_out_ln",
    )(h, gates, *ys, w_branch.astype(BF16), w_out.astype(BF16),
      ln_g.reshape(1, D).astype(F32), ln_b.reshape(1, D).astype(F32))


def _moe_kernel(h_ref, hb_ref, wr_ref, br_ref, wg_ref, wu_ref, wd_ref, lg_ref, lb_ref, o_ref, ob_ref,
                gate_ref, acc_ref):
    e = pl.program_id(1)
    xb = hb_ref[...]
    tm = xb.shape[0]
    lane = _iota((tm, LANES), 1)

    @pl.when(e == 0)
    def _():
        logit = jnp.dot(xb, wr_ref[...], preferred_element_type=F32) + br_ref[...]
        big = jnp.int32(LANES)
        gl = jnp.where(lane < N_GROUPS, logit, -jnp.inf)
        gmax = jnp.max(gl, axis=1, keepdims=True)
        g_sel = jnp.min(jnp.where(gl == gmax, lane, big), axis=1, keepdims=True)
        p_grp = 1.0 / jnp.sum(jnp.exp(gl - gmax), axis=1, keepdims=True)
        lo = N_GROUPS + g_sel * EPG
        el = jnp.where((lane >= lo) & (lane < lo + EPG), logit, -jnp.inf)
        v1 = jnp.max(el, axis=1, keepdims=True)
        i1 = jnp.min(jnp.where(el == v1, lane, big), axis=1, keepdims=True)
        el2 = jnp.where(lane == i1, -jnp.inf, el)
        v2 = jnp.max(el2, axis=1, keepdims=True)
        i2 = jnp.min(jnp.where(el2 == v2, lane, big), axis=1, keepdims=True)
        e2 = jnp.exp(v2 - v1)
        w1 = p_grp / (1.0 + e2)
        w2 = p_grp * e2 / (1.0 + e2)
        gate_ref[...] = jnp.where(lane == i1, w1, 0.0) + jnp.where(lane == i2, w2, 0.0)
        acc_ref[...] = jnp.zeros_like(acc_ref)

    g_e = jnp.sum(jnp.where(lane == e + N_GROUPS, gate_ref[...], 0.0), axis=1, keepdims=True)
    hid = _silu(jnp.dot(xb, wg_ref[0], preferred_element_type=F32)) * jnp.dot(xb, wu_ref[0], preferred_element_type=F32)
    acc_ref[...] += g_e * jnp.dot(hid.astype(BF16), wd_ref[0], preferred_element_type=F32)

    @pl.when(e == N_EXPERTS - 1)
    def _():
        y = _layer_norm_rows(DN_ALPHA * h_ref[...] + acc_ref[...], lg_ref[...], lb_ref[...])
        o_ref[...] = y
        ob_ref[...] = y.astype(BF16)


def _moe(h1, h1b, w_grp, b_grp, w_rt, b_rt, w_gate, w_up, w_down, ln_g, ln_b):
    N, D = h1.shape
    tm = _pick_tile(N, 640)
    w_r = jnp.zeros((D, LANES), F32).at[:, 0:N_GROUPS].set(w_grp).at[:, N_GROUPS:N_GROUPS + N_EXPERTS].set(w_rt)
    b_r = jnp.zeros((1, LANES), F32).at[0, 0:N_GROUPS].set(b_grp).at[0, N_GROUPS:N_GROUPS + N_EXPERTS].set(b_rt)
    full = lambda shape: pl.BlockSpec(shape, lambda i, e: (0,) * len(shape))
    tok = lambda w: pl.BlockSpec((tm, w), lambda i, e: (i, 0))
    return pl.pallas_call(
        _moe_kernel,
        grid=(N // tm, N_EXPERTS),
        in_specs=[tok(D), tok(D), full((D, LANES)), full((1, LANES)),
                  pl.BlockSpec((1, D, D_EXPERT), lambda i, e: (e, 0, 0)),
                  pl.BlockSpec((1, D, D_EXPERT), lambda i, e: (e, 0, 0)),
                  pl.BlockSpec((1, D_EXPERT, D), lambda i, e: (e, 0, 0)),
                  full((1, D)), full((1, D))],
        out_specs=[tok(D), tok(D)],
        out_shape=[jax.ShapeDtypeStruct((N, D), F32), jax.ShapeDtypeStruct((N, D), BF16)],
        scratch_shapes=[pltpu.VMEM((tm, LANES), F32), pltpu.VMEM((tm, D), F32)],
        compiler_params=_cparams("arbitrary", "arbitrary"),
        name="hier_moe_ln",
    )(h1, h1b, w_r.astype(BF16), b_r, w_gate.astype(BF16), w_up.astype(BF16), w_down.astype(BF16),
      ln_g.reshape(1, D).astype(F32), ln_b.reshape(1, D).astype(F32))


def _pad_cols(w, width):
    return jnp.pad(w, ((0, 0), (0, width - w.shape[1])))


def _split_w_in(w):
    o = 0
    w_a = w[:, o:o + 1024]; o += 1024
    gq, gk, gv, ga, gg = (w[:, o:o + 128], w[:, o + 128:o + 256], w[:, o + 256:o + 512],
                          w[:, o + 512:o + 528], w[:, o + 528:o + 784]); o += 784
    w_b = _pad_cols(jnp.concatenate([gq, gk, gv, gg, ga], axis=1), 896)
    cq, ckv, cqi, cki, cwi = (w[:, o:o + 256], w[:, o + 256:o + 384], w[:, o + 384:o + 640],
                              w[:, o + 640:o + 672], w[:, o + 672:o + 680]); o += 680
    w_c = _pad_cols(jnp.concatenate([cq, cqi, ckv, cwi], axis=1), 768)
    dq, dk, dv, di, df, do = (w[:, o:o + 256], w[:, o + 256:o + 512], w[:, o + 512:o + 768],
                              w[:, o + 768:o + 772], w[:, o + 772:o + 776], w[:, o + 776:o + 1032]); o += 1032
    w_d = _pad_cols(jnp.concatenate([dq, dk, dv, do, di, df], axis=1), 1152)
    w_g = w[:, o:o + 4096]
    bf = lambda a: a.astype(BF16)
    return bf(w_a), bf(w_b), bf(w_c), bf(cki.T), bf(w_d), bf(w_g)


def kernel(x, meta, ln_in_g, ln_in_b, rel_bias, w_in, rwkv_mu, rwkv_w_up, rwkv_w0, rwkv_a_up, rwkv_a0, rwkv_g_up, rwkv_k_k, rwkv_k_a, rwkv_r_k, rwkv_gn_g, rwkv_gn_b, gla_a_up, gla_a_b, gla_norm_g, dsa_kv_norm_g, dsa_w_uk, dsa_w_uv, mlstm_conv_w, mlstm_conv_b, mlstm_i_b, mlstm_f_b, mlstm_norm_g, w_branch, w_out, ln1_g, ln1_b, moe_w_grp, moe_b_grp, moe_w_rt, moe_b_rt, moe_w_gate, moe_w_up, moe_w_down, ln2_g, ln2_b):
    B, S, D = x.shape
    assert D == D_MODEL and S % ROW_TILE == 0
    TP = S + FRONT
    N = B * TP
    topk = min(TOPK_MAX, S // 4)
    bias_tab = _bias_tables(rel_bias)

    h, hb = _embed(x, meta, ln_in_g, ln_in_b)
    h = h.reshape(N, D)
    hb = hb.reshape(N, D)
    for l in range(DEPTH):
        w_a, w_b, w_c, w_kit, w_d, w_g = _split_w_in(w_in[l])
        p_a = _proj(hb, w_a)
        p_b = _proj(hb, w_b)
        p_d = _proj(hb, w_d)
        gates = _proj(hb, w_g, act="sigmoid")
        q, qi, wi, kt, v, kit = _dsa_prep(hb, w_c, w_kit, dsa_kv_norm_g[l], dsa_w_uk[l], dsa_w_uv[l])
        y_a = _rwkv(p_a, B, TP, rwkv_mu[l], rwkv_w_up[l], rwkv_w0[l], rwkv_a_up[l], rwkv_a0[l], rwkv_g_up[l],
                    rwkv_k_k[l], rwkv_k_a[l], rwkv_r_k[l], rwkv_gn_g[l], rwkv_gn_b[l])
        y_b = _gla(p_b, B, TP, gla_a_up[l], gla_a_b[l], gla_norm_g[l])
        y_c = _dsa(q, qi, wi, kt, v, kit, bias_tab, B, TP, topk)
        y_d = _mlstm(p_d, B, TP, mlstm_conv_w[l], mlstm_conv_b[l], mlstm_i_b[l], mlstm_f_b[l], mlstm_norm_g[l])
        ys = (y_a.reshape(N, MIX_W), y_b.reshape(N, MIX_W), y_c, y_d.reshape(N, MIX_W))
        h1, h1b = _merge(h, gates, ys, w_branch[l], w_out[l], ln1_g[l], ln1_b[l])
        h, hb = _moe(h1, h1b, moe_w_grp[l], moe_b_grp[l], moe_w_rt[l], moe_b_rt[l],
                     moe_w_gate[l], moe_w_up[l], moe_w_down[l], ln2_g[l], ln2_b[l])
    return h.reshape(B, TP, D)[:, FRONT:]
```

```python
import functools
import math

import numpy as np
import jax
import jax.numpy as jnp
from jax import lax
from jax.experimental import pallas as pl
from jax.experimental.pallas import tpu as pltpu

F32 = jnp.float32
BF16 = jnp.bfloat16

D_MODEL = 1024
HEAD_DIM = 64
N_HEADS = 4
MIX_W = 256
N_META = 16
CHUNK = 64
LANES = 128
ROW_TILE = 128
FRONT = ROW_TILE
FP = FRONT - N_META
NEG = -1e30
LN_EPS = 1e-5
DEPTH = 2
DN_ALPHA = (2 * DEPTH) ** 0.25

RWKV_GN_EPS = HEAD_DIM * 1e-5
GLA_DK = 32
GLA_TAU = 16.0
DSA_KV_RANK = 128
IDX_HEADS = 8
IDX_DIM = 32
TOPK_MAX = 256
N_BUCKETS = 32
MAX_DISTANCE = 128
CONV_W = 4
N_GROUPS = 4
EPG = 4
N_EXPERTS = 16
D_EXPERT = 256

INT_MIN = -(2 ** 31)
KEY_INF = 0x7F800000
VMEM_LIMIT = 56 * 1024 * 1024


def _cparams(*sem):
    return pltpu.CompilerParams(dimension_semantics=tuple(sem), vmem_limit_bytes=VMEM_LIMIT)


def _pick_tile(n, target):
    best = LANES
    t = LANES
    while t <= min(n, target):
        if n % t == 0:
            best = t
        t += LANES
    return best


def _bdot(a, b):
    return jnp.dot(a.astype(BF16), b.astype(BF16), preferred_element_type=F32)


def _bdot_nt(a, b):
    return lax.dot_general(a.astype(BF16), b.astype(BF16), (((1,), (1,)), ((), ())),
                           preferred_element_type=F32)


def _bdot_tn(a, b):
    return lax.dot_general(a.astype(BF16), b.astype(BF16), (((0,), (0,)), ((), ())),
                           preferred_element_type=F32)


def _split(a):
    hi = a.astype(BF16)
    lo = (a - hi.astype(F32)).astype(BF16)
    return hi, lo


_NN = (((1,), (0,)), ((), ()))
_NT = (((1,), (1,)), ((), ()))
_TN = (((0,), (0,)), ((), ()))


def _dot3(a, b, dims=_NN):
    ah, al = _split(a)
    bh, bl = _split(b)
    dg = lambda x, y: lax.dot_general(x, y, dims, preferred_element_type=F32)
    return dg(ah, bh) + (dg(ah, bl) + dg(al, bh))


def _dot_exact_lhs(a_bf16, b):
    bh, bl = _split(b)
    return (jnp.dot(a_bf16, bh, preferred_element_type=F32)
            + jnp.dot(a_bf16, bl, preferred_element_type=F32))


def _dot_exact_rhs(a, b_bf16):
    ah, al = _split(a)
    return (jnp.dot(ah, b_bf16, preferred_element_type=F32)
            + jnp.dot(al, b_bf16, preferred_element_type=F32))


def _sigmoid(x):
    return 1.0 / (1.0 + jnp.exp(-x))


def _log_sigmoid(x):
    return jnp.minimum(x, 0.0) - jnp.log(1.0 + jnp.exp(-jnp.abs(x)))


def _silu(x):
    return x * _sigmoid(x)


def _iota(shape, dim):
    return lax.broadcasted_iota(jnp.int32, shape, dim)


def _tri_incl(n):
    return (_iota((n, n), 1) <= _iota((n, n), 0))


def _head_ones():
    return ((_iota((MIX_W, MIX_W), 0) // HEAD_DIM) == (_iota((MIX_W, MIX_W), 1) // HEAD_DIM)).astype(BF16)


def _row_ids(rows):
    return pl.program_id(1) * ROW_TILE + _iota((rows, 1), 0)


def _embed_kernel(x_ref, meta_ref, g_ref, b_ref, h_ref, hb_ref):
    j = pl.program_id(1)
    src = jnp.where(j == 0, meta_ref[...], x_ref[0])
    mu = jnp.mean(src, -1, keepdims=True)
    xc = src - mu
    var = jnp.mean(xc * xc, -1, keepdims=True)
    y = xc * lax.rsqrt(var + LN_EPS) * g_ref[...] + b_ref[...]
    h_ref[0] = y
    hb_ref[0] = y.astype(BF16)


def _embed(x, meta, g, b):
    B, S, D = x.shape
    TP = S + FRONT
    meta_pad = jnp.concatenate([jnp.zeros((FP, D), F32), meta.astype(F32)], axis=0)
    return pl.pallas_call(
        _embed_kernel,
        grid=(B, TP // ROW_TILE),
        in_specs=[
            pl.BlockSpec((1, ROW_TILE, D), lambda b, j: (b, jnp.maximum(j - 1, 0), 0)),
            pl.BlockSpec((ROW_TILE, D), lambda b, j: (0, 0)),
            pl.BlockSpec((1, D), lambda b, j: (0, 0)),
            pl.BlockSpec((1, D), lambda b, j: (0, 0)),
        ],
        out_specs=[
            pl.BlockSpec((1, ROW_TILE, D), lambda b, j: (b, j, 0)),
            pl.BlockSpec((1, ROW_TILE, D), lambda b, j: (b, j, 0)),
        ],
        out_shape=[jax.ShapeDtypeStruct((B, TP, D), F32), jax.ShapeDtypeStruct((B, TP, D), BF16)],
        compiler_params=_cparams("parallel", "arbitrary"),
        name="embed_ln",
    )(x, meta_pad, g.reshape(1, D), b.reshape(1, D))


def _proj_kernel(h_ref, w_ref, o_ref, *, act):
    y = jnp.dot(h_ref[...], w_ref[...], preferred_element_type=F32)
    if act == "sigmoid":
        y = _sigmoid(y)
    o_ref[...] = y.astype(o_ref.dtype)


def _proj(hb, w, act=None, out_dtype=F32):
    N, D = hb.shape
    W = w.shape[1]
    tn = W if W <= 1152 else 1024
    tm = _pick_tile(N, 640)
    return pl.pallas_call(
        functools.partial(_proj_kernel, act=act),
        grid=(W // tn, N // tm),
        in_specs=[pl.BlockSpec((tm, D), lambda j, i: (i, 0)),
                  pl.BlockSpec((D, tn), lambda j, i: (0, j))],
        out_specs=pl.BlockSpec((tm, tn), lambda j, i: (i, j)),
        out_shape=jax.ShapeDtypeStruct((N, W), out_dtype),
        compiler_params=_cparams("arbitrary", "arbitrary"),
        name="in_proj",
    )(hb, w)


def _rwkv_kernel(p_ref, mu_ref, wup_ref, w0_ref, aup_ref, a0_ref, gup_ref, kk_ref, ka_ref, rk_ref,
                 gng_ref, gnb_ref, y_ref, carry_ref, s_ref):
    j = pl.program_id(1)

    @pl.when(j == 0)
    def _():
        carry_ref[...] = jnp.zeros_like(carry_ref)
        s_ref[...] = jnp.zeros_like(s_ref)

    rows = _row_ids(ROW_TILE)
    valid = rows >= FP
    p = jnp.where(valid, p_ref[0], 0.0)
    prev = jnp.where(_iota((ROW_TILE, 1), 0) == 0, carry_ref[...], pltpu.roll(p, 1, 0))
    carry_ref[...] = p[ROW_TILE - 1:ROW_TILE, :]
    ps = p + (prev - p) * mu_ref[...]

    r = ps[:, 0:256]
    k = ps[:, 256:512]
    v = ps[:, 512:768]
    lora_in = ps[:, 768:896]
    xg = ps[:, 896:1024]
    w_log = _log_sigmoid(w0_ref[...] + _bdot(jnp.tanh(lora_in), wup_ref[...])) - 0.5
    lw = jnp.where(valid, -jnp.exp(w_log), 0.0)
    alpha = _sigmoid(a0_ref[...] + _bdot(lora_in, aup_ref[...]))
    gate = _bdot(_sigmoid(xg), gup_ref[...])

    ones_h = _head_ones()
    kk = k * kk_ref[...]
    kk = kk / jnp.maximum(jnp.sqrt(_dot_exact_rhs(kk * kk, ones_h)), 1e-12)
    k = k * (1.0 + (alpha - 1.0) * ka_ref[...])
    kka = kk * alpha

    tri = _tri_incl(CHUNK)
    tri_b = tri.astype(BF16)
    strict = _iota((CHUNK, CHUNK), 1) < _iota((CHUNK, CHUNK), 0)
    eye = (_iota((CHUNK, CHUNK), 1) == _iota((CHUNK, CHUNK), 0)).astype(F32)

    y_chunks = []
    for c in range(ROW_TILE // CHUNK):
        sl = slice(c * CHUNK, (c + 1) * CHUNK)
        lw_c = lw[sl]
        cum = _dot_exact_lhs(tri_b, lw_c)
        cum_last = cum[CHUNK - 1:CHUNK, :]
        p_in = jnp.exp(cum)
        p_inv = jnp.exp(-cum)
        p_ex = jnp.exp(cum - lw_c)
        p_tail = jnp.exp(cum_last - cum)
        a_t = -kk[sl] * p_ex
        b_t = kka[sl] * p_inv
        k_t = k[sl] * p_inv
        r_t = r[sl] * p_in
        k_b = k[sl] * p_tail
        b_b = kka[sl] * p_tail
        p_last = jnp.exp(cum_last)
        v_c = v[sl]
        y_heads = []
        for h in range(N_HEADS):
            hs = slice(h * HEAD_DIM, (h + 1) * HEAD_DIM)
            a_h, b_h, k_h, r_h, v_h = a_t[:, hs], b_t[:, hs], k_t[:, hs], r_t[:, hs], v_c[:, hs]
            s_h = s_ref[h]
            a_ab = jnp.where(strict, _dot3(a_h, b_h, _NT), 0.0)
            a_ak = jnp.where(strict, _dot3(a_h, k_h, _NT), 0.0)
            a_rb = jnp.where(tri, _dot3(r_h, b_h, _NT), 0.0)
            a_rk = jnp.where(tri, _dot3(r_h, k_h, _NT), 0.0)
            inv = eye + a_ab
            pw = a_ab
            for _ in range(5):
                pw = _dot3(pw, pw)
                inv = inv + _dot3(inv, pw)
            u = _dot3(inv, _dot3(a_h, s_h, _NT) + _dot3(a_ak, v_h))
            y_heads.append(_dot3(r_h, s_h, _NT) + _dot3(a_rk, v_h) + _dot3(a_rb, u))
            s_ref[h] = (s_h * p_last[:, hs] + _dot3(v_h, k_b[:, hs], _TN) + _dot3(u, b_b[:, hs], _TN))
        y_chunks.append(jnp.concatenate(y_heads, axis=1))
    y = jnp.concatenate(y_chunks, axis=0)

    mean = _dot_exact_rhs(y, ones_h) * (1.0 / HEAD_DIM)
    yc = y - mean
    var = _dot_exact_rhs(yc * yc, ones_h) * (1.0 / HEAD_DIM)
    yn = yc * lax.rsqrt(var + RWKV_GN_EPS) * gng_ref[...] + gnb_ref[...]
    bonus = _dot_exact_rhs(r * k * rk_ref[...], ones_h) * v
    y_ref[0] = ((yn + bonus) * gate).astype(y_ref.dtype)


def _rwkv(p_a, B, TP, mu, w_up, w0, a_up, a0, g_up, k_k, k_a, r_k, gn_g, gn_b):
    W = MIX_W
    z64 = jnp.zeros((64, W), F32)
    wup_pad = jnp.concatenate([w_up, z64], axis=0).astype(BF16)
    aup_pad = jnp.concatenate([z64, a_up], axis=0).astype(BF16)
    row = lambda a: a.reshape(1, -1).astype(F32)
    full = lambda shape: pl.BlockSpec(shape, lambda b, j: (0,) * len(shape))
    return pl.pallas_call(
        _rwkv_kernel,
        grid=(B, TP // ROW_TILE),
        in_specs=[pl.BlockSpec((1, ROW_TILE, 1024), lambda b, j: (b, j, 0)),
                  full((1, 1024)), full((128, W)), full((1, W)), full((128, W)), full((1, W)),
                  full((128, W)), full((1, W)), full((1, W)), full((1, W)), full((1, W)), full((1, W))],
        out_specs=pl.BlockSpec((1, ROW_TILE, W), lambda b, j: (b, j, 0)),
        out_shape=jax.ShapeDtypeStruct((B, TP, W), BF16),
        scratch_shapes=[pltpu.VMEM((1, 1024), F32), pltpu.VMEM((N_HEADS, HEAD_DIM, HEAD_DIM), F32)],
        compiler_params=_cparams("parallel", "arbitrary"),
        name="rwkv7",
    )(p_a.reshape(B, TP, 1024), row(mu), wup_pad, row(w0), aup_pad, row(a0), g_up.astype(BF16),
      row(k_k), row(k_a), row(r_k), row(gn_g), row(gn_b))


def _gla_kernel(p_ref, aup_ref, ab_ref, ng_ref, y_ref, s_ref):
    j = pl.program_id(1)

    @pl.when(j == 0)
    def _():
        s_ref[...] = jnp.zeros_like(s_ref)

    valid = _row_ids(ROW_TILE) >= FP
    p = jnp.where(valid, p_ref[0], 0.0)
    q = p[:, 0:128] * (GLA_DK ** -0.5)
    k = p[:, 128:256]
    v = p[:, 256:512]
    og = p[:, 512:768]
    la = _log_sigmoid(_bdot(p[:, 768:896], aup_ref[...]) + ab_ref[...]) * (1.0 / GLA_TAU)
    la = jnp.where(valid, la, 0.0)

    tri = _tri_incl(CHUNK)
    tri_b = tri.astype(BF16)
    o_chunks = []
    for c in range(ROW_TILE // CHUNK):
        sl = slice(c * CHUNK, (c + 1) * CHUNK)
        b = _dot_exact_lhs(tri_b, la[sl])
        b_last = b[CHUNK - 1:CHUNK, :]
        q_g = q[sl] * jnp.exp(b)
        k_g = k[sl] * jnp.exp(-b)
        k_l = k[sl] * jnp.exp(b_last - b)
        dec = jnp.exp(b_last)
        v_c = v[sl]
        o_heads = []
        for h in range(N_HEADS):
            ks = slice(h * GLA_DK, (h + 1) * GLA_DK)
            vs = slice(h * HEAD_DIM, (h + 1) * HEAD_DIM)
            s_h = s_ref[h]
            att = jnp.where(tri, _bdot_nt(q_g[:, ks], k_g[:, ks]), 0.0)
            o_heads.append(_bdot(att, v_c[:, vs]) + _bdot_nt(q_g[:, ks], s_h))
            s_ref[h] = s_h * dec[:, ks] + _bdot_tn(v_c[:, vs], k_l[:, ks])
        o_chunks.append(jnp.concatenate(o_heads, axis=1))
    o = jnp.concatenate(o_chunks, axis=0)
    ms = _dot_exact_rhs(o * o, _head_ones()) * (1.0 / HEAD_DIM)
    y = o * lax.rsqrt(ms + 1e-6) * ng_ref[...] * _silu(og)
    y_ref[0] = y.astype(y_ref.dtype)


def _gla(p_b, B, TP, a_up, a_b, norm_g):
    aup_pad = jnp.zeros((128, 128), F32).at[:a_up.shape[0]].set(a_up).astype(BF16)
    full = lambda shape: pl.BlockSpec(shape, lambda b, j: (0,) * len(shape))
    return pl.pallas_call(
        _gla_kernel,
        grid=(B, TP // ROW_TILE),
        in_specs=[pl.BlockSpec((1, ROW_TILE, 896), lambda b, j: (b, j, 0)),
                  full((128, 128)), full((1, 128)), full((1, MIX_W))],
        out_specs=pl.BlockSpec((1, ROW_TILE, MIX_W), lambda b, j: (b, j, 0)),
        out_shape=jax.ShapeDtypeStruct((B, TP, MIX_W), BF16),
        scratch_shapes=[pltpu.VMEM((N_HEADS, HEAD_DIM, GLA_DK), F32)],
        compiler_params=_cparams("parallel", "arbitrary"),
        name="gla",
    )(p_b.reshape(B, TP, 896), aup_pad, a_b.reshape(1, 128).astype(F32),
      jnp.tile(norm_g.astype(F32), N_HEADS).reshape(1, MIX_W))


def _mlstm_kernel(p_ref, cw_ref, cb_ref, ib_ref, fb_ref, ng_ref, y_ref, carry_ref, c_ref, n_ref, m_ref):
    j = pl.program_id(1)

    @pl.when(j == 0)
    def _():
        carry_ref[...] = jnp.zeros_like(carry_ref)
        c_ref[...] = jnp.zeros_like(c_ref)
        n_ref[...] = jnp.zeros_like(n_ref)
        m_ref[...] = jnp.zeros_like(m_ref)

    valid = _row_ids(ROW_TILE) >= FP
    p = jnp.where(valid, p_ref[0], 0.0)
    a = p[:, 0:512]
    ext = jnp.concatenate([carry_ref[...], a], axis=0)
    carry_ref[...] = a[ROW_TILE - 8:ROW_TILE, :]
    conv = cb_ref[...] + a * cw_ref[CONV_W - 1:CONV_W, :]
    for s in range(1, CONV_W):
        conv = conv + pltpu.roll(ext, s, 0)[8:8 + ROW_TILE, :] * cw_ref[CONV_W - 1 - s:CONV_W - s, :]
    qk = _silu(conv)
    q = jnp.where(valid, qk[:, 0:MIX_W], 0.0)
    k = jnp.where(valid, qk[:, MIX_W:2 * MIX_W], 0.0) * (HEAD_DIM ** -0.5)
    v = p[:, 512:768]
    og = p[:, 768:1024]
    gates = p[:, 1024:1152]
    li_all = jnp.where(valid, gates + ib_ref[...], NEG)
    lf_all = jnp.where(valid, _log_sigmoid(gates + fb_ref[...]), 0.0)

    tri = _tri_incl(CHUNK)
    tri_b = tri.astype(BF16)
    h_chunks = []
    for c in range(ROW_TILE // CHUNK):
        sl = slice(c * CHUNK, (c + 1) * CHUNK)
        li_c = li_all[sl]
        b_c = _dot_exact_lhs(tri_b, lf_all[sl])
        li_t = li_c.T
        b_t = b_c.T
        h_heads = []
        for h in range(N_HEADS):
            hs = slice(h * HEAD_DIM, (h + 1) * HEAD_DIM)
            q_h, k_h, v_h = q[sl, hs], k[sl, hs], v[sl, hs]
            b_col = b_c[:, N_HEADS + h:N_HEADS + h + 1]
            b_row = b_t[N_HEADS + h:N_HEADS + h + 1, :]
            li_col = li_c[:, h:h + 1]
            li_row = li_t[h:h + 1, :]
            b_last = b_col[CHUNK - 1:CHUNK, :]
            c_in, n_in, m_in = c_ref[h], n_ref[h], m_ref[h]
            d_log = jnp.where(tri, b_col - b_row + li_row, -jnp.inf)
            inter = b_col + m_in
            m_t = jnp.maximum(inter, jnp.max(d_log, axis=1, keepdims=True))
            s_w = jnp.exp(d_log - m_t) * _bdot_nt(q_h, k_h)
            w_inter = jnp.exp(inter - m_t)
            num = _bdot(s_w, v_h) + w_inter * _bdot(q_h, c_in)
            den = jnp.sum(s_w, axis=1, keepdims=True) + w_inter * jnp.sum(q_h * n_in, axis=1, keepdims=True)
            h_heads.append(num / jnp.maximum(jnp.abs(den), jnp.exp(-m_t)))
            g_loc = b_last - b_col + li_col
            m_loc = jnp.max(g_loc, axis=0, keepdims=True)
            kw = k_h * jnp.exp(g_loc - m_loc)
            m_new = jnp.maximum(b_last + m_in, m_loc)
            s_old = jnp.exp(b_last + m_in - m_new)
            s_new = jnp.exp(m_loc - m_new)
            c_ref[h] = s_old * c_in + s_new * _bdot_tn(kw, v_h)
            n_ref[h] = s_old * n_in + s_new * jnp.sum(kw, axis=0, keepdims=True)
            m_ref[h] = m_new
        h_chunks.append(jnp.concatenate(h_heads, axis=1))
    hh = jnp.concatenate(h_chunks, axis=0) * _sigmoid(og)
    ones_h = _head_ones()
    mean = _dot_exact_rhs(hh, ones_h) * (1.0 / HEAD_DIM)
    hc = hh - mean
    var = _dot_exact_rhs(hc * hc, ones_h) * (1.0 / HEAD_DIM)
    y_ref[0] = (hc * lax.rsqrt(var + 1e-5) * ng_ref[...]).astype(y_ref.dtype)


def _mlstm(p_d, B, TP, conv_w, conv_b, i_b, f_b, norm_g):
    ib = jnp.zeros((1, LANES), F32).at[0, 0:N_HEADS].set(i_b)
    fb = jnp.zeros((1, LANES), F32).at[0, N_HEADS:2 * N_HEADS].set(f_b)
    full = lambda shape: pl.BlockSpec(shape, lambda b, j: (0,) * len(shape))
    return pl.pallas_call(
        _mlstm_kernel,
        grid=(B, TP // ROW_TILE),
        in_specs=[pl.BlockSpec((1, ROW_TILE, 1152), lambda b, j: (b, j, 0)),
                  full((CONV_W, 512)), full((1, 512)), full((1, LANES)), full((1, LANES)), full((1, MIX_W))],
        out_specs=pl.BlockSpec((1, ROW_TILE, MIX_W), lambda b, j: (b, j, 0)),
        out_shape=jax.ShapeDtypeStruct((B, TP, MIX_W), BF16),
        scratch_shapes=[pltpu.VMEM((8, 512), F32),
                        pltpu.VMEM((N_HEADS, HEAD_DIM, HEAD_DIM), F32),
                        pltpu.VMEM((N_HEADS, 1, HEAD_DIM), F32),
                        pltpu.VMEM((N_HEADS, 1, 1), F32)],
        compiler_params=_cparams("parallel", "arbitrary"),
        name="mlstm",
    )(p_d.reshape(B, TP, 1152), conv_w.astype(F32), conv_b.reshape(1, 512).astype(F32), ib, fb,
      norm_g.reshape(1, MIX_W).astype(F32))


V_ROWS = 80
WT_ROWS = 528


def _dsa_prep_kernel(h_ref, wt_ref, wn_ref, kvg_ref, wuk_ref, wuvt_ref,
                     qt_ref, qit_ref, wit_ref, k_ref, ki_ref, vt_ref):
    hb = h_ref[...]
    tm = hb.shape[0]
    pt = lax.dot_general(wt_ref[...], hb, _NT, preferred_element_type=F32)
    pn = jnp.dot(hb, wn_ref[...], preferred_element_type=F32)
    ckv = pn[:, 0:DSA_KV_RANK]
    c = ckv * lax.rsqrt(jnp.mean(ckv * ckv, -1, keepdims=True) + 1e-6) * kvg_ref[...]
    cb = c.astype(BF16)
    k_ref[...] = jnp.dot(cb, wuk_ref[...], preferred_element_type=F32).astype(BF16)
    ki_ref[...] = pn[:, DSA_KV_RANK:DSA_KV_RANK + IDX_DIM].astype(BF16)
    vt = lax.dot_general(wuvt_ref[...], cb, _NT, preferred_element_type=F32)
    vt = jnp.where(_iota((V_ROWS, tm), 0) == HEAD_DIM, 1.0, vt)
    for t in range(tm // LANES):
        cs = slice(t * LANES, (t + 1) * LANES)
        for h in range(N_HEADS):
            qt_ref[t, :, h * LANES:(h + 1) * LANES] = (
                pt[h * HEAD_DIM:(h + 1) * HEAD_DIM, cs] * (HEAD_DIM ** -0.5)).astype(BF16)
        for h in range(IDX_HEADS):
            qit_ref[t, :, h * LANES:(h + 1) * LANES] = pt[MIX_W + h * IDX_DIM:MIX_W + (h + 1) * IDX_DIM, cs].astype(BF16)
        wit_ref[t] = pt[2 * MIX_W:2 * MIX_W + IDX_HEADS, cs] * ((IDX_HEADS * IDX_DIM) ** -0.5)
        vt_ref[t] = vt[:, cs].astype(BF16)


def _dsa_prep(hb, w_t, w_n, kv_norm_g, w_uk, w_uv):
    N, D = hb.shape
    tm = _pick_tile(N, 640)
    nt = tm // LANES
    full = lambda shape: pl.BlockSpec(shape, lambda i: (0,) * len(shape))
    wuvt = jnp.pad(w_uv.T, ((0, V_ROWS - HEAD_DIM), (0, 0))).astype(BF16)
    return pl.pallas_call(
        _dsa_prep_kernel,
        grid=(N // tm,),
        in_specs=[pl.BlockSpec((tm, D), lambda i: (i, 0)),
                  full((WT_ROWS, D)), full((D, 256)), full((1, DSA_KV_RANK)),
                  full((DSA_KV_RANK, HEAD_DIM)), full((V_ROWS, DSA_KV_RANK))],
        out_specs=[pl.BlockSpec((nt, HEAD_DIM, N_HEADS * LANES), lambda i: (i, 0, 0)),
                   pl.BlockSpec((nt, IDX_DIM, IDX_HEADS * LANES), lambda i: (i, 0, 0)),
                   pl.BlockSpec((nt, IDX_HEADS, LANES), lambda i: (i, 0, 0)),
                   pl.BlockSpec((tm, HEAD_DIM), lambda i: (i, 0)),
                   pl.BlockSpec((tm, IDX_DIM), lambda i: (i, 0)),
                   pl.BlockSpec((nt, V_ROWS, LANES), lambda i: (i, 0, 0))],
        out_shape=[jax.ShapeDtypeStruct((N // LANES, HEAD_DIM, N_HEADS * LANES), BF16),
                   jax.ShapeDtypeStruct((N // LANES, IDX_DIM, IDX_HEADS * LANES), BF16),
                   jax.ShapeDtypeStruct((N // LANES, IDX_HEADS, LANES), F32),
                   jax.ShapeDtypeStruct((N, HEAD_DIM), BF16),
                   jax.ShapeDtypeStruct((N, IDX_DIM), BF16),
                   jax.ShapeDtypeStruct((N // LANES, V_ROWS, LANES), BF16)],
        compiler_params=_cparams("arbitrary"),
        name="dsa_prep",
    )(hb, w_t, w_n, kv_norm_g.reshape(1, DSA_KV_RANK).astype(F32), w_uk.astype(BF16), wuvt)


def _loop_groups(lo, hi, fn):
    n = jnp.maximum(hi - lo, 0)
    n4 = lax.shift_right_logical(n, 2)

    def body(j, c):
        fn([lo + 4 * j + u for u in range(4)])
        return c

    lax.fori_loop(0, n4, body, 0)
    rest = lo + 4 * n4

    @pl.when((n & 2) == 2)
    def _():
        fn([rest, rest + 1])

    @pl.when((n & 1) == 1)
    def _():
        fn([hi - 1])


def _dsa_kernel(qt_ref, qit_ref, wit_ref, k_ref, ki_ref, vt_ref, bias_ref, y_ref,
                sk_ref, m_ref, acc_ref, *, topk):
    i = pl.program_id(1)
    nk = i + 1
    QT = ROW_TILE
    HQ = N_HEADS * QT
    t_lane = i * QT + _iota((LANES, QT), 1)
    key_pos = lambda kt: kt * LANES + _iota((LANES, QT), 0)
    rows = lambda kt: pl.ds(pl.multiple_of(kt * LANES, LANES), LANES)
    per_head = lambda fn: jnp.concatenate([fn(slice(h * QT, (h + 1) * QT)) for h in range(N_HEADS)], axis=1)

    qit = qit_ref[0]
    wit = wit_ref[0]

    def score_tile(kt, edge):
        rel = jnp.dot(ki_ref[rows(kt), :], qit, preferred_element_type=F32)
        score = jnp.maximum(rel[:, 0:QT], 0.0) * wit[0:1, :]
        for h in range(1, IDX_HEADS):
            score = score + jnp.maximum(rel[:, h * QT:(h + 1) * QT], 0.0) * wit[h:h + 1, :]
        score = jnp.where(score == 0.0, 0.0, score)
        bits = lax.bitcast_convert_type(score, jnp.int32)
        key = jnp.where(bits < 0, bits ^ jnp.int32(0x7FFFFFFF), bits)
        if edge:
            s_pos = key_pos(kt)
            key = jnp.where(s_pos < FP + N_META, jnp.int32(KEY_INF), key)
            key = jnp.where((s_pos >= FP) & (s_pos <= t_lane), key, jnp.int32(INT_MIN))
        sk_ref[kt] = key

    score_tile(0, True)
    _loop_groups(1, i, lambda kts: [score_tile(kt, False) for kt in kts])

    @pl.when(i > 0)
    def _():
        score_tile(i, True)

    def count(pred_fn):
        def body(kt, acc):
            return acc + jnp.where(pred_fn(sk_ref[kt], kt), 1, 0)

        def body4(j, acc):
            for u in range(4):
                acc = body(4 * j + u, acc)
            return acc

        n4 = lax.shift_right_logical(nk, 2)
        acc = lax.fori_loop(0, n4, body4, jnp.zeros((LANES, QT), jnp.int32))
        acc = lax.fori_loop(4 * n4, nk, body, acc)
        return jnp.sum(acc, axis=0, keepdims=True)

    def bit_body(it, tau):
        cand = tau + jnp.left_shift(jnp.int32(1), 31 - it)
        cnt = count(lambda sk, kt: sk >= cand)
        return jnp.where(cnt >= topk, cand, tau)

    tau = lax.fori_loop(0, 32, bit_body, jnp.full((1, QT), INT_MIN, jnp.int32))
    tau = jnp.maximum(tau, jnp.int32(INT_MIN + 1))
    n_gt = count(lambda sk, kt: sk > tau)
    n_ge = count(lambda sk, kt: sk >= tau)
    need = topk - n_gt

    @pl.when(jnp.max(n_ge - topk) > 0)
    def _():
        n_bits = max(1, int(math.ceil(math.log2(sk_ref.shape[0] * LANES + 1))))

        def pos_body(it, x):
            cand = x + jnp.left_shift(jnp.int32(1), n_bits - 1 - it)
            cnt = count(lambda sk, kt: (sk == tau) & (key_pos(kt) < cand))
            return jnp.where(cnt < need, cand, x)

        x = lax.fori_loop(0, n_bits, pos_body, jnp.zeros((1, QT), jnp.int32))
        jmax = jnp.where(n_ge > topk, x, jnp.int32(2 ** 30))

        def drop_body(kt, c):
            sk = sk_ref[kt]
            sk_ref[kt] = jnp.where((sk == tau) & (key_pos(kt) > jmax), jnp.int32(INT_MIN), sk)
            return c

        lax.fori_loop(0, nk, drop_body, 0)

    qt = qt_ref[0]
    m_ref[...] = jnp.full((1, HQ), NEG, F32)
    acc_ref[...] = jnp.zeros((V_ROWS, HQ), F32)

    def attend(kts, near):
        lgs = []
        for kt in kts:
            lg = jnp.dot(k_ref[rows(kt), :], qt, preferred_element_type=F32)
            if near is not None:
                lg = lg + bias_ref[near]
            sel = sk_ref[kt] >= tau
            lgs.append(per_head(lambda hs: jnp.where(sel, lg[:, hs], NEG)))
        tmax = lgs[0]
        for lg in lgs[1:]:
            tmax = jnp.maximum(tmax, lg)
        m_old = m_ref[...]
        m_new = jnp.maximum(m_old, jnp.max(tmax, axis=0, keepdims=True))
        pv = None
        for kt, lg in zip(kts, lgs):
            t = jnp.dot(vt_ref[kt], jnp.exp(lg - m_new).astype(BF16), preferred_element_type=F32)
            pv = t if pv is None else pv + t
        acc_ref[...] = acc_ref[...] * jnp.exp(m_old - m_new) + pv
        m_ref[...] = m_new

    attend([i], 0)

    @pl.when(i > 0)
    def _():
        attend([i - 1], 1)

    _loop_groups(0, i - 1, lambda kts: attend(kts, None))
    acc = acc_ref[...]
    out = acc[0:HEAD_DIM, :] / jnp.maximum(acc[HEAD_DIM:HEAD_DIM + 1, :], 1e-30)
    y_ref[...] = per_head(lambda hs: out[:, hs].T).astype(y_ref.dtype)


def _t5_bucket(dist):
    max_exact = N_BUCKETS // 2
    n = jnp.maximum(dist, 0)
    large = max_exact + (jnp.log(jnp.maximum(n, 1).astype(F32) / max_exact)
                         / math.log(MAX_DISTANCE / max_exact) * (N_BUCKETS - max_exact)).astype(jnp.int32)
    return jnp.where(n < max_exact, n, jnp.minimum(large, N_BUCKETS - 1))


def _bias_tables(rel_bias):
    per_dist = rel_bias[_t5_bucket(jnp.arange(2 * ROW_TILE, dtype=jnp.int32))]
    q_minus_s = np.arange(ROW_TILE)[None, :] - np.arange(ROW_TILE)[:, None]
    far = per_dist[2 * ROW_TILE - 1]
    tabs = [per_dist[np.clip(r * ROW_TILE + q_minus_s, 0, 2 * ROW_TILE - 1)] - far for r in (0, 1)]
    return jnp.stack(tabs).transpose(0, 1, 3, 2).reshape(2, ROW_TILE, N_HEADS * ROW_TILE).astype(F32)


def _dsa(qt, qit, wit, k, ki, vt, bias_tab, B, TP, topk):
    nq = TP // ROW_TILE
    return pl.pallas_call(
        functools.partial(_dsa_kernel, topk=topk),
        grid=(B, nq),
        in_specs=[pl.BlockSpec((1, HEAD_DIM, N_HEADS * LANES), lambda b, i: (b * nq + i, 0, 0)),
                  pl.BlockSpec((1, IDX_DIM, IDX_HEADS * LANES), lambda b, i: (b * nq + i, 0, 0)),
                  pl.BlockSpec((1, IDX_HEADS, LANES), lambda b, i: (b * nq + i, 0, 0)),
                  pl.BlockSpec((TP, HEAD_DIM), lambda b, i: (b, 0)),
                  pl.BlockSpec((TP, IDX_DIM), lambda b, i: (b, 0)),
                  pl.BlockSpec((nq, V_ROWS, LANES), lambda b, i: (b, 0, 0)),
                  pl.BlockSpec((2, ROW_TILE, N_HEADS * ROW_TILE), lambda b, i: (0, 0, 0))],
        out_specs=pl.BlockSpec((ROW_TILE, MIX_W), lambda b, i: (b * nq + i, 0)),
        out_shape=jax.ShapeDtypeStruct((B * TP, MIX_W), BF16),
        scratch_shapes=[pltpu.VMEM((nq, LANES, ROW_TILE), jnp.int32),
                        pltpu.VMEM((1, N_HEADS * ROW_TILE), F32),
                        pltpu.VMEM((V_ROWS, N_HEADS * ROW_TILE), F32)],
        compiler_params=_cparams("parallel", "arbitrary"),
        name="dsa_attend",
    )(qt, qit, wit, k, ki, vt, bias_tab)


def _layer_norm_rows(z, g, b):
    mu = jnp.mean(z, -1, keepdims=True)
    zc = z - mu
    var = jnp.mean(zc * zc, -1, keepdims=True)
    return zc * lax.rsqrt(var + LN_EPS) * g + b


def _merge_kernel(h_ref, g_ref, ya_ref, yb_ref, yc_ref, yd_ref, wb_ref, wo_ref, lg_ref, lb_ref,
                  h1_ref, h1b_ref):
    merged = None
    for i, y_ref in enumerate((ya_ref, yb_ref, yc_ref, yd_ref)):
        t = g_ref[:, i * D_MODEL:(i + 1) * D_MODEL] * jnp.dot(y_ref[...], wb_ref[i], preferred_element_type=F32)
        merged = t if merged is None else merged + t
    z = DN_ALPHA * h_ref[...] + jnp.dot(merged.astype(BF16), wo_ref[...], preferred_element_type=F32)
    y = _layer_norm_rows(z, lg_ref[...], lb_ref[...])
    h1_ref[...] = y
    h1b_ref[...] = y.astype(BF16)


def _merge(h, gates, ys, w_branch, w_out, ln_g, ln_b):
    N, D = h.shape
    tm = _pick_tile(N, 512)
    full = lambda shape: pl.BlockSpec(shape, lambda i: (0,) * len(shape))
    tok = lambda w: pl.BlockSpec((tm, w), lambda i: (i, 0))
    return pl.pallas_call(
        _merge_kernel,
        grid=(N // tm,),
        in_specs=[tok(D), tok(4 * D), tok(MIX_W), tok(MIX_W), tok(MIX_W), tok(MIX_W),
                  full((4, MIX_W, D)), full((D, D)), full((1, D)), full((1, D))],
        out_specs=[tok(D), tok(D)],
        out_shape=[jax.ShapeDtypeStruct((N, D), F32), jax.ShapeDtypeStruct((N, D), BF16)],
        compiler_params=_cparams("arbitrary"),
        name="merge_out_ln",
    )(h, gates, *ys, w_branch.astype(BF16), w_out.astype(BF16),
      ln_g.reshape(1, D).astype(F32), ln_b.reshape(1, D).astype(F32))


def _moe_kernel(h_ref, hb_ref, wr_ref, br_ref, wg_ref, wu_ref, wd_ref, lg_ref, lb_ref, o_ref, ob_ref,
                gate_ref, acc_ref):
    e = pl.program_id(1)
    xb = hb_ref[...]
    tm = xb.shape[0]
    lane = _iota((tm, LANES), 1)

    @pl.when(e == 0)
    def _():
        logit = jnp.dot(xb, wr_ref[...], preferred_element_type=F32) + br_ref[...]
        big = jnp.int32(LANES)
        gl = jnp.where(lane < N_GROUPS, logit, -jnp.inf)
        gmax = jnp.max(gl, axis=1, keepdims=True)
        g_sel = jnp.min(jnp.where(gl == gmax, lane, big), axis=1, keepdims=True)
        p_grp = 1.0 / jnp.sum(jnp.exp(gl - gmax), axis=1, keepdims=True)
        lo = N_GROUPS + g_sel * EPG
        el = jnp.where((lane >= lo) & (lane < lo + EPG), logit, -jnp.inf)
        v1 = jnp.max(el, axis=1, keepdims=True)
        i1 = jnp.min(jnp.where(el == v1, lane, big), axis=1, keepdims=True)
        el2 = jnp.where(lane == i1, -jnp.inf, el)
        v2 = jnp.max(el2, axis=1, keepdims=True)
        i2 = jnp.min(jnp.where(el2 == v2, lane, big), axis=1, keepdims=True)
        e2 = jnp.exp(v2 - v1)
        w1 = p_grp / (1.0 + e2)
        w2 = p_grp * e2 / (1.0 + e2)
        gate_ref[...] = jnp.where(lane == i1, w1, 0.0) + jnp.where(lane == i2, w2, 0.0)
        acc_ref[...] = jnp.zeros_like(acc_ref)

    g_e = jnp.sum(jnp.where(lane == e + N_GROUPS, gate_ref[...], 0.0), axis=1, keepdims=True)
    hid = _silu(jnp.dot(xb, wg_ref[0], preferred_element_type=F32)) * jnp.dot(xb, wu_ref[0], preferred_element_type=F32)
    acc_ref[...] += g_e * jnp.dot(hid.astype(BF16), wd_ref[0], preferred_element_type=F32)

    @pl.when(e == N_EXPERTS - 1)
    def _():
        y = _layer_norm_rows(DN_ALPHA * h_ref[...] + acc_ref[...], lg_ref[...], lb_ref[...])
        o_ref[...] = y
        ob_ref[...] = y.astype(BF16)


def _moe(h1, h1b, w_grp, b_grp, w_rt, b_rt, w_gate, w_up, w_down, ln_g, ln_b):
    N, D = h1.shape
    tm = _pick_tile(N, 640)
    w_r = jnp.zeros((D, LANES), F32).at[:, 0:N_GROUPS].set(w_grp).at[:, N_GROUPS:N_GROUPS + N_EXPERTS].set(w_rt)
    b_r = jnp.zeros((1, LANES), F32).at[0, 0:N_GROUPS].set(b_grp).at[0, N_GROUPS:N_GROUPS + N_EXPERTS].set(b_rt)
    full = lambda shape: pl.BlockSpec(shape, lambda i, e: (0,) * len(shape))
    tok = lambda w: pl.BlockSpec((tm, w), lambda i, e: (i, 0))
    return pl.pallas_call(
        _moe_kernel,
        grid=(N // tm, N_EXPERTS),
        in_specs=[tok(D), tok(D), full((D, LANES)), full((1, LANES)),
                  pl.BlockSpec((1, D, D_EXPERT), lambda i, e: (e, 0, 0)),
                  pl.BlockSpec((1, D, D_EXPERT), lambda i, e: (e, 0, 0)),
                  pl.BlockSpec((1, D_EXPERT, D), lambda i, e: (e, 0, 0)),
                  full((1, D)), full((1, D))],
        out_specs=[tok(D), tok(D)],
        out_shape=[jax.ShapeDtypeStruct((N, D), F32), jax.ShapeDtypeStruct((N, D), BF16)],
        scratch_shapes=[pltpu.VMEM((tm, LANES), F32), pltpu.VMEM((tm, D), F32)],
        compiler_params=_cparams("arbitrary", "arbitrary"),
        name="hier_moe_ln",
    )(h1, h1b, w_r.astype(BF16), b_r, w_gate.astype(BF16), w_up.astype(BF16), w_down.astype(BF16),
      ln_g.reshape(1, D).astype(F32), ln_b.reshape(1, D).astype(F32))


def _pad_cols(w, width):
    return jnp.pad(w, ((0, 0), (0, width - w.shape[1])))


def _split_w_in(w):
    o = 0
    w_a = w[:, o:o + 1024]; o += 1024
    gq, gk, gv, ga, gg = (w[:, o:o + 128], w[:, o + 128:o + 256], w[:, o + 256:o + 512],
                          w[:, o + 512:o + 528], w[:, o + 528:o + 784]); o += 784
    w_b = _pad_cols(jnp.concatenate([gq, gk, gv, gg, ga], axis=1), 896)
    cq, ckv, cqi, cki, cwi = (w[:, o:o + 256], w[:, o + 256:o + 384], w[:, o + 384:o + 640],
                              w[:, o + 640:o + 672], w[:, o + 672:o + 680]); o += 680
    w_t = jnp.pad(jnp.concatenate([cq.T, cqi.T, cwi.T], axis=0), ((0, WT_ROWS - 2 * MIX_W - IDX_HEADS), (0, 0)))
    w_n = _pad_cols(jnp.concatenate([ckv, cki], axis=1), 256)
    dq, dk, dv, di, df, do = (w[:, o:o + 256], w[:, o + 256:o + 512], w[:, o + 512:o + 768],
                              w[:, o + 768:o + 772], w[:, o + 772:o + 776], w[:, o + 776:o + 1032]); o += 1032
    w_d = _pad_cols(jnp.concatenate([dq, dk, dv, do, di, df], axis=1), 1152)
    w_g = w[:, o:o + 4096]
    bf = lambda a: a.astype(BF16)
    return bf(w_a), bf(w_b), bf(w_t), bf(w_n), bf(w_d), bf(w_g)


def kernel(x, meta, ln_in_g, ln_in_b, rel_bias, w_in, rwkv_mu, rwkv_w_up, rwkv_w0, rwkv_a_up, rwkv_a0, rwkv_g_up, rwkv_k_k, rwkv_k_a, rwkv_r_k, rwkv_gn_g, rwkv_gn_b, gla_a_up, gla_a_b, gla_norm_g, dsa_kv_norm_g, dsa_w_uk, dsa_w_uv, mlstm_conv_w, mlstm_conv_b, mlstm_i_b, mlstm_f_b, mlstm_norm_g, w_branch, w_out, ln1_g, ln1_b, moe_w_grp, moe_b_grp, moe_w_rt, moe_b_rt, moe_w_gate, moe_w_up, moe_w_down, ln2_g, ln2_b):
    B, S, D = x.shape
    assert D == D_MODEL and S % ROW_TILE == 0
    TP = S + FRONT
    N = B * TP
    topk = min(TOPK_MAX, S // 4)
    bias_tab = _bias_tables(rel_bias)

    h, hb = _embed(x, meta, ln_in_g, ln_in_b)
    h = h.reshape(N, D)
    hb = hb.reshape(N, D)
    for l in range(DEPTH):
        w_a, w_b, w_t, w_n, w_d, w_g = _split_w_in(w_in[l])
        p_a = _proj(hb, w_a)
        p_b = _proj(hb, w_b)
        p_d = _proj(hb, w_d)
        gates = _proj(hb, w_g, act="sigmoid")
        qt, qit, wit, k, ki, vt = _dsa_prep(hb, w_t, w_n, dsa_kv_norm_g[l], dsa_w_uk[l], dsa_w_uv[l])
        y_a = _rwkv(p_a, B, TP, rwkv_mu[l], rwkv_w_up[l], rwkv_w0[l], rwkv_a_up[l], rwkv_a0[l], rwkv_g_up[l],
                    rwkv_k_k[l], rwkv_k_a[l], rwkv_r_k[l], rwkv_gn_g[l], rwkv_gn_b[l])
        y_b = _gla(p_b, B, TP, gla_a_up[l], gla_a_b[l], gla_norm_g[l])
        y_c = _dsa(qt, qit, wit, k, ki, vt, bias_tab, B, TP, topk)
        y_d = _mlstm(p_d, B, TP, mlstm_conv_w[l], mlstm_conv_b[l], mlstm_i_b[l], mlstm_f_b[l], mlstm_norm_g[l])
        ys = (y_a.reshape(N, MIX_W), y_b.reshape(N, MIX_W), y_c, y_d.reshape(N, MIX_W))
        h1, h1b = _merge(h, gates, ys, w_branch[l], w_out[l], ln1_g[l], ln1_b[l])
        h, hb = _moe(h1, h1b, moe_w_grp[l], moe_b_grp[l], moe_w_rt[l], moe_b_rt[l],
                     moe_w_gate[l], moe_w_up[l], moe_w_down[l], ln2_g[l], ln2_b[l])
    return h.reshape(B, TP, D)[:, FRONT:]
```

```python
import functools
import math

import numpy as np
import jax
import jax.numpy as jnp
from jax import lax
from jax.experimental import pallas as pl
from jax.experimental.pallas import tpu as pltpu

F32 = jnp.float32
BF16 = jnp.bfloat16

D_MODEL = 1024
HEAD_DIM = 64
N_HEADS = 4
MIX_W = 256
N_META = 16
CHUNK = 64
LANES = 128
ROW_TILE = 128
FRONT = ROW_TILE
FP = FRONT - N_META
NEG = -1e30
LN_EPS = 1e-5
DEPTH = 2
DN_ALPHA = (2 * DEPTH) ** 0.25

RWKV_GN_EPS = HEAD_DIM * 1e-5
GLA_DK = 32
GLA_TAU = 16.0
DSA_KV_RANK = 128
IDX_HEADS = 8
IDX_DIM = 32
TOPK_MAX = 256
N_BUCKETS = 32
MAX_DISTANCE = 128
CONV_W = 4
N_GROUPS = 4
EPG = 4
N_EXPERTS = 16
D_EXPERT = 256

INT_MIN = -(2 ** 31)
KEY_INF = 0x7F800000
VMEM_LIMIT = 56 * 1024 * 1024


def _cparams(*sem):
    return pltpu.CompilerParams(dimension_semantics=tuple(sem), vmem_limit_bytes=VMEM_LIMIT)


def _pick_tile(n, target):
    best = LANES
    t = LANES
    while t <= min(n, target):
        if n % t == 0:
            best = t
        t += LANES
    return best


def _bdot(a, b):
    return jnp.dot(a.astype(BF16), b.astype(BF16), preferred_element_type=F32)


def _bdot_nt(a, b):
    return lax.dot_general(a.astype(BF16), b.astype(BF16), (((1,), (1,)), ((), ())),
                           preferred_element_type=F32)


def _bdot_tn(a, b):
    return lax.dot_general(a.astype(BF16), b.astype(BF16), (((0,), (0,)), ((), ())),
                           preferred_element_type=F32)


def _split(a):
    hi = a.astype(BF16)
    lo = (a - hi.astype(F32)).astype(BF16)
    return hi, lo


_NN = (((1,), (0,)), ((), ()))
_NT = (((1,), (1,)), ((), ()))
_TN = (((0,), (0,)), ((), ()))


def _dot3(a, b, dims=_NN):
    ah, al = _split(a)
    bh, bl = _split(b)
    dg = lambda x, y: lax.dot_general(x, y, dims, preferred_element_type=F32)
    return dg(ah, bh) + (dg(ah, bl) + dg(al, bh))


def _dot_exact_lhs(a_bf16, b):
    bh, bl = _split(b)
    return (jnp.dot(a_bf16, bh, preferred_element_type=F32)
            + jnp.dot(a_bf16, bl, preferred_element_type=F32))


def _dot_exact_rhs(a, b_bf16):
    ah, al = _split(a)
    return (jnp.dot(ah, b_bf16, preferred_element_type=F32)
            + jnp.dot(al, b_bf16, preferred_element_type=F32))


def _sigmoid(x):
    return 1.0 / (1.0 + jnp.exp(-x))


def _log_sigmoid(x):
    return jnp.minimum(x, 0.0) - jnp.log(1.0 + jnp.exp(-jnp.abs(x)))


def _silu(x):
    return x * _sigmoid(x)


def _iota(shape, dim):
    return lax.broadcasted_iota(jnp.int32, shape, dim)


def _tri_incl(n):
    return (_iota((n, n), 1) <= _iota((n, n), 0))


def _head_ones():
    return ((_iota((MIX_W, MIX_W), 0) // HEAD_DIM) == (_iota((MIX_W, MIX_W), 1) // HEAD_DIM)).astype(BF16)


def _row_ids(rows):
    return pl.program_id(1) * ROW_TILE + _iota((rows, 1), 0)


def _embed_kernel(x_ref, meta_ref, g_ref, b_ref, h_ref, hb_ref):
    j = pl.program_id(1)
    src = jnp.where(j == 0, meta_ref[...], x_ref[0])
    mu = jnp.mean(src, -1, keepdims=True)
    xc = src - mu
    var = jnp.mean(xc * xc, -1, keepdims=True)
    y = xc * lax.rsqrt(var + LN_EPS) * g_ref[...] + b_ref[...]
    h_ref[0] = y
    hb_ref[0] = y.astype(BF16)


def _embed(x, meta, g, b):
    B, S, D = x.shape
    TP = S + FRONT
    meta_pad = jnp.concatenate([jnp.zeros((FP, D), F32), meta.astype(F32)], axis=0)
    return pl.pallas_call(
        _embed_kernel,
        grid=(B, TP // ROW_TILE),
        in_specs=[
            pl.BlockSpec((1, ROW_TILE, D), lambda b, j: (b, jnp.maximum(j - 1, 0), 0)),
            pl.BlockSpec((ROW_TILE, D), lambda b, j: (0, 0)),
            pl.BlockSpec((1, D), lambda b, j: (0, 0)),
            pl.BlockSpec((1, D), lambda b, j: (0, 0)),
        ],
        out_specs=[
            pl.BlockSpec((1, ROW_TILE, D), lambda b, j: (b, j, 0)),
            pl.BlockSpec((1, ROW_TILE, D), lambda b, j: (b, j, 0)),
        ],
        out_shape=[jax.ShapeDtypeStruct((B, TP, D), F32), jax.ShapeDtypeStruct((B, TP, D), BF16)],
        compiler_params=_cparams("parallel", "arbitrary"),
        name="embed_ln",
    )(x, meta_pad, g.reshape(1, D), b.reshape(1, D))


def _proj_kernel(h_ref, w_ref, o_ref, *, act):
    y = jnp.dot(h_ref[...], w_ref[...], preferred_element_type=F32)
    if act == "sigmoid":
        y = _sigmoid(y)
    o_ref[...] = y.astype(o_ref.dtype)


def _proj(hb, w, act=None, out_dtype=F32):
    N, D = hb.shape
    W = w.shape[1]
    tn = W if W <= 1152 else 1024
    tm = _pick_tile(N, 640)
    return pl.pallas_call(
        functools.partial(_proj_kernel, act=act),
        grid=(W // tn, N // tm),
        in_specs=[pl.BlockSpec((tm, D), lambda j, i: (i, 0)),
                  pl.BlockSpec((D, tn), lambda j, i: (0, j))],
        out_specs=pl.BlockSpec((tm, tn), lambda j, i: (i, j)),
        out_shape=jax.ShapeDtypeStruct((N, W), out_dtype),
        compiler_params=_cparams("arbitrary", "arbitrary"),
        name="in_proj",
    )(hb, w)


def _rwkv_kernel(p_ref, mu_ref, wup_ref, w0_ref, aup_ref, a0_ref, gup_ref, kk_ref, ka_ref, rk_ref,
                 gng_ref, gnb_ref, y_ref, carry_ref, s_ref):
    j = pl.program_id(0)
    nb = p_ref.shape[0]
    n_chunks = ROW_TILE // CHUNK

    @pl.when(j == 0)
    def _():
        carry_ref[...] = jnp.zeros_like(carry_ref)
        s_ref[...] = jnp.zeros_like(s_ref)

    valid = (j * ROW_TILE + _iota((ROW_TILE, 1), 0)) >= FP
    first_row = _iota((ROW_TILE, 1), 0) == 0
    ones_h = _head_ones()
    tri = _tri_incl(CHUNK)
    tri_b = tri.astype(BF16)
    strict = _iota((CHUNK, CHUNK), 1) < _iota((CHUNK, CHUNK), 0)
    eye = (_iota((CHUNK, CHUNK), 1) == _iota((CHUNK, CHUNK), 0)).astype(F32)
    heads = [slice(h * HEAD_DIM, (h + 1) * HEAD_DIM) for h in range(N_HEADS)]

    pro = []
    unit = {}
    for b in range(nb):
        p = jnp.where(valid, p_ref[b], 0.0)
        prev = jnp.where(first_row, carry_ref[b], pltpu.roll(p, 1, 0))
        carry_ref[b] = p[ROW_TILE - 1:ROW_TILE, :]
        ps = p + (prev - p) * mu_ref[...]
        r = ps[:, 0:256]
        k = ps[:, 256:512]
        v = ps[:, 512:768]
        lora_in = ps[:, 768:896]
        xg = ps[:, 896:1024]
        w_log = _log_sigmoid(w0_ref[...] + _bdot(jnp.tanh(lora_in), wup_ref[...])) - 0.5
        lw = jnp.where(valid, -jnp.exp(w_log), 0.0)
        alpha = _sigmoid(a0_ref[...] + _bdot(lora_in, aup_ref[...]))
        gate = _bdot(_sigmoid(xg), gup_ref[...])
        kk = k * kk_ref[...]
        kk = kk / jnp.maximum(jnp.sqrt(_dot_exact_rhs(kk * kk, ones_h)), 1e-12)
        k = k * (1.0 + (alpha - 1.0) * ka_ref[...])
        kka = kk * alpha
        pro.append((r, k, v, gate))
        for c in range(n_chunks):
            sl = slice(c * CHUNK, (c + 1) * CHUNK)
            lw_c = lw[sl]
            cum = _dot_exact_lhs(tri_b, lw_c)
            cum_last = cum[CHUNK - 1:CHUNK, :]
            p_inv = jnp.exp(-cum)
            p_tail = jnp.exp(cum_last - cum)
            unit[b, c] = dict(a=-kk[sl] * jnp.exp(cum - lw_c), b=kka[sl] * p_inv, k=k[sl] * p_inv,
                              r=r[sl] * jnp.exp(cum), kb=k[sl] * p_tail, bb=kka[sl] * p_tail,
                              pl=jnp.exp(cum_last), v=v[sl])

    keys = [(b, c, h) for b in range(nb) for c in range(n_chunks) for h in range(N_HEADS)]
    part = lambda name, key: unit[key[0], key[1]][name][:, heads[key[2]]]
    a_ab = {q: jnp.where(strict, _dot3(part("a", q), part("b", q), _NT), 0.0) for q in keys}
    a_ak = {q: jnp.where(strict, _dot3(part("a", q), part("k", q), _NT), 0.0) for q in keys}
    a_rb = {q: jnp.where(tri, _dot3(part("r", q), part("b", q), _NT), 0.0) for q in keys}
    a_rk = {q: jnp.where(tri, _dot3(part("r", q), part("k", q), _NT), 0.0) for q in keys}
    inv = {q: eye + a_ab[q] for q in keys}
    pw = a_ab
    for _ in range(5):
        pw = {q: _dot3(pw[q], pw[q]) for q in keys}
        inv = {q: inv[q] + _dot3(inv[q], pw[q]) for q in keys}
    ak_v = {q: _dot3(a_ak[q], part("v", q)) for q in keys}
    rk_v = {q: _dot3(a_rk[q], part("v", q)) for q in keys}
    kb_v = {q: _dot3(part("v", q), part("kb", q), _TN) for q in keys}

    bh = [(b, h) for b in range(nb) for h in range(N_HEADS)]
    state = {q: s_ref[q[0], q[1]] for q in bh}
    y_parts = {}
    for c in range(n_chunks):
        full = lambda q: (q[0], c, q[1])
        a_s = {q: _dot3(part("a", full(q)), state[q], _NT) for q in bh}
        r_s = {q: _dot3(part("r", full(q)), state[q], _NT) for q in bh}
        u = {q: _dot3(inv[full(q)], a_s[q] + ak_v[full(q)]) for q in bh}
        for q in bh:
            y_parts[full(q)] = r_s[q] + rk_v[full(q)] + _dot3(a_rb[full(q)], u[q])
        state = {q: (state[q] * part("pl", full(q)) + kb_v[full(q)] + _dot3(u[q], part("bb", full(q)), _TN))
                 for q in bh}
    for q in bh:
        s_ref[q[0], q[1]] = state[q]

    for b in range(nb):
        r, k, v, gate = pro[b]
        y = jnp.concatenate([jnp.concatenate([y_parts[b, c, h] for h in range(N_HEADS)], axis=1)
                             for c in range(n_chunks)], axis=0)
        mean = _dot_exact_rhs(y, ones_h) * (1.0 / HEAD_DIM)
        yc = y - mean
        var = _dot_exact_rhs(yc * yc, ones_h) * (1.0 / HEAD_DIM)
        yn = yc * lax.rsqrt(var + RWKV_GN_EPS) * gng_ref[...] + gnb_ref[...]
        bonus = _dot_exact_rhs(r * k * rk_ref[...], ones_h) * v
        y_ref[b] = ((yn + bonus) * gate).astype(y_ref.dtype)


def _rwkv(p_a, B, TP, mu, w_up, w0, a_up, a0, g_up, k_k, k_a, r_k, gn_g, gn_b):
    W = MIX_W
    z64 = jnp.zeros((64, W), F32)
    wup_pad = jnp.concatenate([w_up, z64], axis=0).astype(BF16)
    aup_pad = jnp.concatenate([z64, a_up], axis=0).astype(BF16)
    row = lambda a: a.reshape(1, -1).astype(F32)
    full = lambda shape: pl.BlockSpec(shape, lambda j: (0,) * len(shape))
    return pl.pallas_call(
        _rwkv_kernel,
        grid=(TP // ROW_TILE,),
        in_specs=[pl.BlockSpec((B, ROW_TILE, 1024), lambda j: (0, j, 0)),
                  full((1, 1024)), full((128, W)), full((1, W)), full((128, W)), full((1, W)),
                  full((128, W)), full((1, W)), full((1, W)), full((1, W)), full((1, W)), full((1, W))],
        out_specs=pl.BlockSpec((B, ROW_TILE, W), lambda j: (0, j, 0)),
        out_shape=jax.ShapeDtypeStruct((B, TP, W), BF16),
        scratch_shapes=[pltpu.VMEM((B, 1, 1024), F32), pltpu.VMEM((B, N_HEADS, HEAD_DIM, HEAD_DIM), F32)],
        compiler_params=_cparams("arbitrary"),
        name="rwkv7",
    )(p_a.reshape(B, TP, 1024), row(mu), wup_pad, row(w0), aup_pad, row(a0), g_up.astype(BF16),
      row(k_k), row(k_a), row(r_k), row(gn_g), row(gn_b))


def _gla_kernel(p_ref, aup_ref, ab_ref, ng_ref, y_ref, s_ref):
    j = pl.program_id(1)

    @pl.when(j == 0)
    def _():
        s_ref[...] = jnp.zeros_like(s_ref)

    valid = _row_ids(ROW_TILE) >= FP
    p = jnp.where(valid, p_ref[0], 0.0)
    q = p[:, 0:128] * (GLA_DK ** -0.5)
    k = p[:, 128:256]
    v = p[:, 256:512]
    og = p[:, 512:768]
    la = _log_sigmoid(_bdot(p[:, 768:896], aup_ref[...]) + ab_ref[...]) * (1.0 / GLA_TAU)
    la = jnp.where(valid, la, 0.0)

    tri = _tri_incl(CHUNK)
    tri_b = tri.astype(BF16)
    o_chunks = []
    for c in range(ROW_TILE // CHUNK):
        sl = slice(c * CHUNK, (c + 1) * CHUNK)
        b = _dot_exact_lhs(tri_b, la[sl])
        b_last = b[CHUNK - 1:CHUNK, :]
        q_g = q[sl] * jnp.exp(b)
        k_g = k[sl] * jnp.exp(-b)
        k_l = k[sl] * jnp.exp(b_last - b)
        dec = jnp.exp(b_last)
        v_c = v[sl]
        o_heads = []
        for h in range(N_HEADS):
            ks = slice(h * GLA_DK, (h + 1) * GLA_DK)
            vs = slice(h * HEAD_DIM, (h + 1) * HEAD_DIM)
            s_h = s_ref[h]
            att = jnp.where(tri, _bdot_nt(q_g[:, ks], k_g[:, ks]), 0.0)
            o_heads.append(_bdot(att, v_c[:, vs]) + _bdot_nt(q_g[:, ks], s_h))
            s_ref[h] = s_h * dec[:, ks] + _bdot_tn(v_c[:, vs], k_l[:, ks])
        o_chunks.append(jnp.concatenate(o_heads, axis=1))
    o = jnp.concatenate(o_chunks, axis=0)
    ms = _dot_exact_rhs(o * o, _head_ones()) * (1.0 / HEAD_DIM)
    y = o * lax.rsqrt(ms + 1e-6) * ng_ref[...] * _silu(og)
    y_ref[0] = y.astype(y_ref.dtype)


def _gla(p_b, B, TP, a_up, a_b, norm_g):
    aup_pad = jnp.zeros((128, 128), F32).at[:a_up.shape[0]].set(a_up).astype(BF16)
    full = lambda shape: pl.BlockSpec(shape, lambda b, j: (0,) * len(shape))
    return pl.pallas_call(
        _gla_kernel,
        grid=(B, TP // ROW_TILE),
        in_specs=[pl.BlockSpec((1, ROW_TILE, 896), lambda b, j: (b, j, 0)),
                  full((128, 128)), full((1, 128)), full((1, MIX_W))],
        out_specs=pl.BlockSpec((1, ROW_TILE, MIX_W), lambda b, j: (b, j, 0)),
        out_shape=jax.ShapeDtypeStruct((B, TP, MIX_W), BF16),
        scratch_shapes=[pltpu.VMEM((N_HEADS, HEAD_DIM, GLA_DK), F32)],
        compiler_params=_cparams("parallel", "arbitrary"),
        name="gla",
    )(p_b.reshape(B, TP, 896), aup_pad, a_b.reshape(1, 128).astype(F32),
      jnp.tile(norm_g.astype(F32), N_HEADS).reshape(1, MIX_W))


def _mlstm_kernel(p_ref, cw_ref, cb_ref, ib_ref, fb_ref, ng_ref, y_ref, carry_ref, c_ref, n_ref, m_ref):
    j = pl.program_id(1)

    @pl.when(j == 0)
    def _():
        carry_ref[...] = jnp.zeros_like(carry_ref)
        c_ref[...] = jnp.zeros_like(c_ref)
        n_ref[...] = jnp.zeros_like(n_ref)
        m_ref[...] = jnp.zeros_like(m_ref)

    valid = _row_ids(ROW_TILE) >= FP
    p = jnp.where(valid, p_ref[0], 0.0)
    a = p[:, 0:512]
    ext = jnp.concatenate([carry_ref[...], a], axis=0)
    carry_ref[...] = a[ROW_TILE - 8:ROW_TILE, :]
    conv = cb_ref[...] + a * cw_ref[CONV_W - 1:CONV_W, :]
    for s in range(1, CONV_W):
        conv = conv + pltpu.roll(ext, s, 0)[8:8 + ROW_TILE, :] * cw_ref[CONV_W - 1 - s:CONV_W - s, :]
    qk = _silu(conv)
    q = jnp.where(valid, qk[:, 0:MIX_W], 0.0)
    k = jnp.where(valid, qk[:, MIX_W:2 * MIX_W], 0.0) * (HEAD_DIM ** -0.5)
    v = p[:, 512:768]
    og = p[:, 768:1024]
    gates = p[:, 1024:1152]
    li_all = jnp.where(valid, gates + ib_ref[...], NEG)
    lf_all = jnp.where(valid, _log_sigmoid(gates + fb_ref[...]), 0.0)

    tri = _tri_incl(CHUNK)
    tri_b = tri.astype(BF16)
    h_chunks = []
    for c in range(ROW_TILE // CHUNK):
        sl = slice(c * CHUNK, (c + 1) * CHUNK)
        li_c = li_all[sl]
        b_c = _dot_exact_lhs(tri_b, lf_all[sl])
        li_t = li_c.T
        b_t = b_c.T
        h_heads = []
        for h in range(N_HEADS):
            hs = slice(h * HEAD_DIM, (h + 1) * HEAD_DIM)
            q_h, k_h, v_h = q[sl, hs], k[sl, hs], v[sl, hs]
            b_col = b_c[:, N_HEADS + h:N_HEADS + h + 1]
            b_row = b_t[N_HEADS + h:N_HEADS + h + 1, :]
            li_col = li_c[:, h:h + 1]
            li_row = li_t[h:h + 1, :]
            b_last = b_col[CHUNK - 1:CHUNK, :]
            c_in, n_in, m_in = c_ref[h], n_ref[h], m_ref[h]
            d_log = jnp.where(tri, b_col - b_row + li_row, -jnp.inf)
            inter = b_col + m_in
            m_t = jnp.maximum(inter, jnp.max(d_log, axis=1, keepdims=True))
            s_w = jnp.exp(d_log - m_t) * _bdot_nt(q_h, k_h)
            w_inter = jnp.exp(inter - m_t)
            num = _bdot(s_w, v_h) + w_inter * _bdot(q_h, c_in)
            den = jnp.sum(s_w, axis=1, keepdims=True) + w_inter * jnp.sum(q_h * n_in, axis=1, keepdims=True)
            h_heads.append(num / jnp.maximum(jnp.abs(den), jnp.exp(-m_t)))
            g_loc = b_last - b_col + li_col
            m_loc = jnp.max(g_loc, axis=0, keepdims=True)
            kw = k_h * jnp.exp(g_loc - m_loc)
            m_new = jnp.maximum(b_last + m_in, m_loc)
            s_old = jnp.exp(b_last + m_in - m_new)
            s_new = jnp.exp(m_loc - m_new)
            c_ref[h] = s_old * c_in + s_new * _bdot_tn(kw, v_h)
            n_ref[h] = s_old * n_in + s_new * jnp.sum(kw, axis=0, keepdims=True)
            m_ref[h] = m_new
        h_chunks.append(jnp.concatenate(h_heads, axis=1))
    hh = jnp.concatenate(h_chunks, axis=0) * _sigmoid(og)
    ones_h = _head_ones()
    mean = _dot_exact_rhs(hh, ones_h) * (1.0 / HEAD_DIM)
    hc = hh - mean
    var = _dot_exact_rhs(hc * hc, ones_h) * (1.0 / HEAD_DIM)
    y_ref[0] = (hc * lax.rsqrt(var + 1e-5) * ng_ref[...]).astype(y_ref.dtype)


def _mlstm(p_d, B, TP, conv_w, conv_b, i_b, f_b, norm_g):
    ib = jnp.zeros((1, LANES), F32).at[0, 0:N_HEADS].set(i_b)
    fb = jnp.zeros((1, LANES), F32).at[0, N_HEADS:2 * N_HEADS].set(f_b)
    full = lambda shape: pl.BlockSpec(shape, lambda b, j: (0,) * len(shape))
    return pl.pallas_call(
        _mlstm_kernel,
        grid=(B, TP // ROW_TILE),
        in_specs=[pl.BlockSpec((1, ROW_TILE, 1152), lambda b, j: (b, j, 0)),
                  full((CONV_W, 512)), full((1, 512)), full((1, LANES)), full((1, LANES)), full((1, MIX_W))],
        out_specs=pl.BlockSpec((1, ROW_TILE, MIX_W), lambda b, j: (b, j, 0)),
        out_shape=jax.ShapeDtypeStruct((B, TP, MIX_W), BF16),
        scratch_shapes=[pltpu.VMEM((8, 512), F32),
                        pltpu.VMEM((N_HEADS, HEAD_DIM, HEAD_DIM), F32),
                        pltpu.VMEM((N_HEADS, 1, HEAD_DIM), F32),
                        pltpu.VMEM((N_HEADS, 1, 1), F32)],
        compiler_params=_cparams("parallel", "arbitrary"),
        name="mlstm",
    )(p_d.reshape(B, TP, 1152), conv_w.astype(F32), conv_b.reshape(1, 512).astype(F32), ib, fb,
      norm_g.reshape(1, MIX_W).astype(F32))


V_ROWS = 80
WT_ROWS = 528


def _dsa_prep_kernel(h_ref, wt_ref, wn_ref, kvg_ref, wuk_ref, wuvt_ref,
                     qt_ref, qit_ref, wit_ref, k_ref, ki_ref, vt_ref):
    hb = h_ref[...]
    tm = hb.shape[0]
    pt = lax.dot_general(wt_ref[...], hb, _NT, preferred_element_type=F32)
    pn = jnp.dot(hb, wn_ref[...], preferred_element_type=F32)
    ckv = pn[:, 0:DSA_KV_RANK]
    c = ckv * lax.rsqrt(jnp.mean(ckv * ckv, -1, keepdims=True) + 1e-6) * kvg_ref[...]
    cb = c.astype(BF16)
    k_ref[...] = jnp.dot(cb, wuk_ref[...], preferred_element_type=F32).astype(BF16)
    ki_ref[...] = pn[:, DSA_KV_RANK:DSA_KV_RANK + IDX_DIM].astype(BF16)
    vt = lax.dot_general(wuvt_ref[...], cb, _NT, preferred_element_type=F32)
    vt = jnp.where(_iota((V_ROWS, tm), 0) == HEAD_DIM, 1.0, vt)
    for t in range(tm // LANES):
        cs = slice(t * LANES, (t + 1) * LANES)
        for h in range(N_HEADS):
            qt_ref[t, :, h * LANES:(h + 1) * LANES] = (
                pt[h * HEAD_DIM:(h + 1) * HEAD_DIM, cs] * (HEAD_DIM ** -0.5)).astype(BF16)
        for h in range(IDX_HEADS):
            qit_ref[t, :, h * LANES:(h + 1) * LANES] = pt[MIX_W + h * IDX_DIM:MIX_W + (h + 1) * IDX_DIM, cs].astype(BF16)
        wit_ref[t] = pt[2 * MIX_W:2 * MIX_W + IDX_HEADS, cs] * ((IDX_HEADS * IDX_DIM) ** -0.5)
        vt_ref[t] = vt[:, cs].astype(BF16)


def _dsa_prep(hb, w_t, w_n, kv_norm_g, w_uk, w_uv):
    N, D = hb.shape
    tm = _pick_tile(N, 640)
    nt = tm // LANES
    full = lambda shape: pl.BlockSpec(shape, lambda i: (0,) * len(shape))
    wuvt = jnp.pad(w_uv.T, ((0, V_ROWS - HEAD_DIM), (0, 0))).astype(BF16)
    return pl.pallas_call(
        _dsa_prep_kernel,
        grid=(N // tm,),
        in_specs=[pl.BlockSpec((tm, D), lambda i: (i, 0)),
                  full((WT_ROWS, D)), full((D, 256)), full((1, DSA_KV_RANK)),
                  full((DSA_KV_RANK, HEAD_DIM)), full((V_ROWS, DSA_KV_RANK))],
        out_specs=[pl.BlockSpec((nt, HEAD_DIM, N_HEADS * LANES), lambda i: (i, 0, 0)),
                   pl.BlockSpec((nt, IDX_DIM, IDX_HEADS * LANES), lambda i: (i, 0, 0)),
                   pl.BlockSpec((nt, IDX_HEADS, LANES), lambda i: (i, 0, 0)),
                   pl.BlockSpec((tm, HEAD_DIM), lambda i: (i, 0)),
                   pl.BlockSpec((tm, IDX_DIM), lambda i: (i, 0)),
                   pl.BlockSpec((nt, V_ROWS, LANES), lambda i: (i, 0, 0))],
        out_shape=[jax.ShapeDtypeStruct((N // LANES, HEAD_DIM, N_HEADS * LANES), BF16),
                   jax.ShapeDtypeStruct((N // LANES, IDX_DIM, IDX_HEADS * LANES), BF16),
                   jax.ShapeDtypeStruct((N // LANES, IDX_HEADS, LANES), F32),
                   jax.ShapeDtypeStruct((N, HEAD_DIM), BF16),
                   jax.ShapeDtypeStruct((N, IDX_DIM), BF16),
                   jax.ShapeDtypeStruct((N // LANES, V_ROWS, LANES), BF16)],
        compiler_params=_cparams("arbitrary"),
        name="dsa_prep",
    )(hb, w_t, w_n, kv_norm_g.reshape(1, DSA_KV_RANK).astype(F32), w_uk.astype(BF16), wuvt)


def _loop_groups(lo, hi, fn):
    n = jnp.maximum(hi - lo, 0)
    n4 = lax.shift_right_logical(n, 2)

    def body(j, c):
        fn([lo + 4 * j + u for u in range(4)])
        return c

    lax.fori_loop(0, n4, body, 0)
    rest = lo + 4 * n4

    @pl.when((n & 2) == 2)
    def _():
        fn([rest, rest + 1])

    @pl.when((n & 1) == 1)
    def _():
        fn([hi - 1])


def _dsa_kernel(qt_ref, qit_ref, wit_ref, k_ref, ki_ref, vt_ref, bias_ref, y_ref,
                sk_ref, m_ref, acc_ref, *, topk):
    i = pl.program_id(1)
    nk = i + 1
    QT = ROW_TILE
    HQ = N_HEADS * QT
    t_lane = i * QT + _iota((LANES, QT), 1)
    key_pos = lambda kt: kt * LANES + _iota((LANES, QT), 0)
    rows = lambda kt: pl.ds(pl.multiple_of(kt * LANES, LANES), LANES)
    per_head = lambda fn: jnp.concatenate([fn(slice(h * QT, (h + 1) * QT)) for h in range(N_HEADS)], axis=1)

    qit = qit_ref[0]
    wit = wit_ref[0]

    def score_tile(kt, edge):
        rel = jnp.dot(ki_ref[rows(kt), :], qit, preferred_element_type=F32)
        score = jnp.maximum(rel[:, 0:QT], 0.0) * wit[0:1, :]
        for h in range(1, IDX_HEADS):
            score = score + jnp.maximum(rel[:, h * QT:(h + 1) * QT], 0.0) * wit[h:h + 1, :]
        score = jnp.where(score == 0.0, 0.0, score)
        bits = lax.bitcast_convert_type(score, jnp.int32)
        key = jnp.where(bits < 0, bits ^ jnp.int32(0x7FFFFFFF), bits)
        if edge:
            s_pos = key_pos(kt)
            key = jnp.where(s_pos < FP + N_META, jnp.int32(KEY_INF), key)
            key = jnp.where((s_pos >= FP) & (s_pos <= t_lane), key, jnp.int32(INT_MIN))
        sk_ref[kt] = key

    score_tile(0, True)
    _loop_groups(1, i, lambda kts: [score_tile(kt, False) for kt in kts])

    @pl.when(i > 0)
    def _():
        score_tile(i, True)

    def count(pred_fn):
        def body(kt, acc):
            return acc + jnp.where(pred_fn(sk_ref[kt], kt), 1, 0)

        def body4(j, acc):
            for u in range(4):
                acc = body(4 * j + u, acc)
            return acc

        n4 = lax.shift_right_logical(nk, 2)
        acc = lax.fori_loop(0, n4, body4, jnp.zeros((LANES, QT), jnp.int32))
        acc = lax.fori_loop(4 * n4, nk, body, acc)
        return jnp.sum(acc, axis=0, keepdims=True)

    def bit_body(it, tau):
        cand = tau + jnp.left_shift(jnp.int32(1), 31 - it)
        cnt = count(lambda sk, kt: sk >= cand)
        return jnp.where(cnt >= topk, cand, tau)

    tau = lax.fori_loop(0, 32, bit_body, jnp.full((1, QT), INT_MIN, jnp.int32))
    tau = jnp.maximum(tau, jnp.int32(INT_MIN + 1))
    n_gt = count(lambda sk, kt: sk > tau)
    n_ge = count(lambda sk, kt: sk >= tau)
    need = topk - n_gt

    @pl.when(jnp.max(n_ge - topk) > 0)
    def _():
        n_bits = max(1, int(math.ceil(math.log2(sk_ref.shape[0] * LANES + 1))))

        def pos_body(it, x):
            cand = x + jnp.left_shift(jnp.int32(1), n_bits - 1 - it)
            cnt = count(lambda sk, kt: (sk == tau) & (key_pos(kt) < cand))
            return jnp.where(cnt < need, cand, x)

        x = lax.fori_loop(0, n_bits, pos_body, jnp.zeros((1, QT), jnp.int32))
        jmax = jnp.where(n_ge > topk, x, jnp.int32(2 ** 30))

        def drop_body(kt, c):
            sk = sk_ref[kt]
            sk_ref[kt] = jnp.where((sk == tau) & (key_pos(kt) > jmax), jnp.int32(INT_MIN), sk)
            return c

        lax.fori_loop(0, nk, drop_body, 0)

    qt = qt_ref[0]
    m_ref[...] = jnp.full((1, HQ), NEG, F32)
    acc_ref[...] = jnp.zeros((V_ROWS, HQ), F32)

    def attend(kts, near):
        lgs = []
        for kt in kts:
            lg = jnp.dot(k_ref[rows(kt), :], qt, preferred_element_type=F32)
            if near is not None:
                lg = lg + bias_ref[near]
            sel = sk_ref[kt] >= tau
            lgs.append(per_head(lambda hs: jnp.where(sel, lg[:, hs], NEG)))
        tmax = lgs[0]
        for lg in lgs[1:]:
            tmax = jnp.maximum(tmax, lg)
        m_old = m_ref[...]
        m_new = jnp.maximum(m_old, jnp.max(tmax, axis=0, keepdims=True))
        pv = None
        for kt, lg in zip(kts, lgs):
            t = jnp.dot(vt_ref[kt], jnp.exp(lg - m_new).astype(BF16), preferred_element_type=F32)
            pv = t if pv is None else pv + t
        acc_ref[...] = acc_ref[...] * jnp.exp(m_old - m_new) + pv
        m_ref[...] = m_new

    attend([i], 0)

    @pl.when(i > 0)
    def _():
        attend([i - 1], 1)

    _loop_groups(0, i - 1, lambda kts: attend(kts, None))
    acc = acc_ref[...]
    out = acc[0:HEAD_DIM, :] / jnp.maximum(acc[HEAD_DIM:HEAD_DIM + 1, :], 1e-30)
    y_ref[...] = per_head(lambda hs: out[:, hs].T).astype(y_ref.dtype)


def _t5_bucket(dist):
    max_exact = N_BUCKETS // 2
    n = jnp.maximum(dist, 0)
    large = max_exact + (jnp.log(jnp.maximum(n, 1).astype(F32) / max_exact)
                         / math.log(MAX_DISTANCE / max_exact) * (N_BUCKETS - max_exact)).astype(jnp.int32)
    return jnp.where(n < max_exact, n, jnp.minimum(large, N_BUCKETS - 1))


def _bias_tables(rel_bias):
    per_dist = rel_bias[_t5_bucket(jnp.arange(2 * ROW_TILE, dtype=jnp.int32))]
    q_minus_s = np.arange(ROW_TILE)[None, :] - np.arange(ROW_TILE)[:, None]
    far = per_dist[2 * ROW_TILE - 1]
    tabs = [per_dist[np.clip(r * ROW_TILE + q_minus_s, 0, 2 * ROW_TILE - 1)] - far for r in (0, 1)]
    return jnp.stack(tabs).transpose(0, 1, 3, 2).reshape(2, ROW_TILE, N_HEADS * ROW_TILE).astype(F32)


def _dsa(qt, qit, wit, k, ki, vt, bias_tab, B, TP, topk):
    nq = TP // ROW_TILE
    return pl.pallas_call(
        functools.partial(_dsa_kernel, topk=topk),
        grid=(B, nq),
        in_specs=[pl.BlockSpec((1, HEAD_DIM, N_HEADS * LANES), lambda b, i: (b * nq + i, 0, 0)),
                  pl.BlockSpec((1, IDX_DIM, IDX_HEADS * LANES), lambda b, i: (b * nq + i, 0, 0)),
                  pl.BlockSpec((1, IDX_HEADS, LANES), lambda b, i: (b * nq + i, 0, 0)),
                  pl.BlockSpec((TP, HEAD_DIM), lambda b, i: (b, 0)),
                  pl.BlockSpec((TP, IDX_DIM), lambda b, i: (b, 0)),
                  pl.BlockSpec((nq, V_ROWS, LANES), lambda b, i: (b, 0, 0)),
                  pl.BlockSpec((2, ROW_TILE, N_HEADS * ROW_TILE), lambda b, i: (0, 0, 0))],
        out_specs=pl.BlockSpec((ROW_TILE, MIX_W), lambda b, i: (b * nq + i, 0)),
        out_shape=jax.ShapeDtypeStruct((B * TP, MIX_W), BF16),
        scratch_shapes=[pltpu.VMEM((nq, LANES, ROW_TILE), jnp.int32),
                        pltpu.VMEM((1, N_HEADS * ROW_TILE), F32),
                        pltpu.VMEM((V_ROWS, N_HEADS * ROW_TILE), F32)],
        compiler_params=_cparams("parallel", "arbitrary"),
        name="dsa_attend",
    )(qt, qit, wit, k, ki, vt, bias_tab)


def _layer_norm_rows(z, g, b):
    mu = jnp.mean(z, -1, keepdims=True)
    zc = z - mu
    var = jnp.mean(zc * zc, -1, keepdims=True)
    return zc * lax.rsqrt(var + LN_EPS) * g + b


def _merge_kernel(h_ref, g_ref, ya_ref, yb_ref, yc_ref, yd_ref, wb_ref, wo_ref, lg_ref, lb_ref,
                  h1_ref, h1b_ref):
    merged = None
    for i, y_ref in enumerate((ya_ref, yb_ref, yc_ref, yd_ref)):
        t = g_ref[:, i * D_MODEL:(i + 1) * D_MODEL] * jnp.dot(y_ref[...], wb_ref[i], preferred_element_type=F32)
        merged = t if merged is None else merged + t
    z = DN_ALPHA * h_ref[...] + jnp.dot(merged.astype(BF16), wo_ref[...], preferred_element_type=F32)
    y = _layer_norm_rows(z, lg_ref[...], lb_ref[...])
    h1_ref[...] = y
    h1b_ref[...] = y.astype(BF16)


def _merge(h, gates, ys, w_branch, w_out, ln_g, ln_b):
    N, D = h.shape
    tm = _pick_tile(N, 512)
    full = lambda shape: pl.BlockSpec(shape, lambda i: (0,) * len(shape))
    tok = lambda w: pl.BlockSpec((tm, w), lambda i: (i, 0))
    return pl.pallas_call(
        _merge_kernel,
        grid=(N // tm,),
        in_specs=[tok(D), tok(4 * D), tok(MIX_W), tok(MIX_W), tok(MIX_W), tok(MIX_W),
                  full((4, MIX_W, D)), full((D, D)), full((1, D)), full((1, D))],
        out_specs=[tok(D), tok(D)],
        out_shape=[jax.ShapeDtypeStruct((N, D), F32), jax.ShapeDtypeStruct((N, D), BF16)],
        compiler_params=_cparams("arbitrary"),
        name="merge_out_ln",
    )(h, gates, *ys, w_branch.astype(BF16), w_out.astype(BF16),
      ln_g.reshape(1, D).astype(F32), ln_b.reshape(1, D).astype(F32))


def _moe_kernel(h_ref, hb_ref, wr_ref, br_ref, wg_ref, wu_ref, wd_ref, lg_ref, lb_ref, o_ref, ob_ref,
                gate_ref, acc_ref):
    e = pl.program_id(1)
    xb = hb_ref[...]
    tm = xb.shape[0]
    lane = _iota((tm, LANES), 1)

    @pl.when(e == 0)
    def _():
        logit = jnp.dot(xb, wr_ref[...], preferred_element_type=F32) + br_ref[...]
        big = jnp.int32(LANES)
        gl = jnp.where(lane < N_GROUPS, logit, -jnp.inf)
        gmax = jnp.max(gl, axis=1, keepdims=True)
        g_sel = jnp.min(jnp.where(gl == gmax, lane, big), axis=1, keepdims=True)
        p_grp = 1.0 / jnp.sum(jnp.exp(gl - gmax), axis=1, keepdims=True)
        lo = N_GROUPS + g_sel * EPG
        el = jnp.where((lane >= lo) & (lane < lo + EPG), logit, -jnp.inf)
        v1 = jnp.max(el, axis=1, keepdims=True)
        i1 = jnp.min(jnp.where(el == v1, lane, big), axis=1, keepdims=True)
        el2 = jnp.where(lane == i1, -jnp.inf, el)
        v2 = jnp.max(el2, axis=1, keepdims=True)
        i2 = jnp.min(jnp.where(el2 == v2, lane, big), axis=1, keepdims=True)
        e2 = jnp.exp(v2 - v1)
        w1 = p_grp / (1.0 + e2)
        w2 = p_grp * e2 / (1.0 + e2)
        gate_ref[...] = jnp.where(lane == i1, w1, 0.0) + jnp.where(lane == i2, w2, 0.0)
        acc_ref[...] = jnp.zeros_like(acc_ref)

    g_e = jnp.sum(jnp.where(lane == e + N_GROUPS, gate_ref[...], 0.0), axis=1, keepdims=True)
    hid = _silu(jnp.dot(xb, wg_ref[0], preferred_element_type=F32)) * jnp.dot(xb, wu_ref[0], preferred_element_type=F32)
    acc_ref[...] += g_e * jnp.dot(hid.astype(BF16), wd_ref[0], preferred_element_type=F32)

    @pl.when(e == N_EXPERTS - 1)
    def _():
        y = _layer_norm_rows(DN_ALPHA * h_ref[...] + acc_ref[...], lg_ref[...], lb_ref[...])
        o_ref[...] = y
        ob_ref[...] = y.astype(BF16)


def _moe(h1, h1b, w_grp, b_grp, w_rt, b_rt, w_gate, w_up, w_down, ln_g, ln_b):
    N, D = h1.shape
    tm = _pick_tile(N, 640)
    w_r = jnp.zeros((D, LANES), F32).at[:, 0:N_GROUPS].set(w_grp).at[:, N_GROUPS:N_GROUPS + N_EXPERTS].set(w_rt)
    b_r = jnp.zeros((1, LANES), F32).at[0, 0:N_GROUPS].set(b_grp).at[0, N_GROUPS:N_GROUPS + N_EXPERTS].set(b_rt)
    full = lambda shape: pl.BlockSpec(shape, lambda i, e: (0,) * len(shape))
    tok = lambda w: pl.BlockSpec((tm, w), lambda i, e: (i, 0))
    return pl.pallas_call(
        _moe_kernel,
        grid=(N // tm, N_EXPERTS),
        in_specs=[tok(D), tok(D), full((D, LANES)), full((1, LANES)),
                  pl.BlockSpec((1, D, D_EXPERT), lambda i, e: (e, 0, 0)),
                  pl.BlockSpec((1, D, D_EXPERT), lambda i, e: (e, 0, 0)),
                  pl.BlockSpec((1, D_EXPERT, D), lambda i, e: (e, 0, 0)),
                  full((1, D)), full((1, D))],
        out_specs=[tok(D), tok(D)],
        out_shape=[jax.ShapeDtypeStruct((N, D), F32), jax.ShapeDtypeStruct((N, D), BF16)],
        scratch_shapes=[pltpu.VMEM((tm, LANES), F32), pltpu.VMEM((tm, D), F32)],
        compiler_params=_cparams("arbitrary", "arbitrary"),
        name="hier_moe_ln",
    )(h1, h1b, w_r.astype(BF16), b_r, w_gate.astype(BF16), w_up.astype(BF16), w_down.astype(BF16),
      ln_g.reshape(1, D).astype(F32), ln_b.reshape(1, D).astype(F32))


def _pad_cols(w, width):
    return jnp.pad(w, ((0, 0), (0, width - w.shape[1])))


def _split_w_in(w):
    o = 0
    w_a = w[:, o:o + 1024]; o += 1024
    gq, gk, gv, ga, gg = (w[:, o:o + 128], w[:, o + 128:o + 256], w[:, o + 256:o + 512],
                          w[:, o + 512:o + 528], w[:, o + 528:o + 784]); o += 784
    w_b = _pad_cols(jnp.concatenate([gq, gk, gv, gg, ga], axis=1), 896)
    cq, ckv, cqi, cki, cwi = (w[:, o:o + 256], w[:, o + 256:o + 384], w[:, o + 384:o + 640],
                              w[:, o + 640:o + 672], w[:, o + 672:o + 680]); o += 680
    w_t = jnp.pad(jnp.concatenate([cq.T, cqi.T, cwi.T], axis=0), ((0, WT_ROWS - 2 * MIX_W - IDX_HEADS), (0, 0)))
    w_n = _pad_cols(jnp.concatenate([ckv, cki], axis=1), 256)
    dq, dk, dv, di, df, do = (w[:, o:o + 256], w[:, o + 256:o + 512], w[:, o + 512:o + 768],
                              w[:, o + 768:o + 772], w[:, o + 772:o + 776], w[:, o + 776:o + 1032]); o += 1032
    w_d = _pad_cols(jnp.concatenate([dq, dk, dv, do, di, df], axis=1), 1152)
    w_g = w[:, o:o + 4096]
    bf = lambda a: a.astype(BF16)
    return bf(w_a), bf(w_b), bf(w_t), bf(w_n), bf(w_d), bf(w_g)


def kernel(x, meta, ln_in_g, ln_in_b, rel_bias, w_in, rwkv_mu, rwkv_w_up, rwkv_w0, rwkv_a_up, rwkv_a0, rwkv_g_up, rwkv_k_k, rwkv_k_a, rwkv_r_k, rwkv_gn_g, rwkv_gn_b, gla_a_up, gla_a_b, gla_norm_g, dsa_kv_norm_g, dsa_w_uk, dsa_w_uv, mlstm_conv_w, mlstm_conv_b, mlstm_i_b, mlstm_f_b, mlstm_norm_g, w_branch, w_out, ln1_g, ln1_b, moe_w_grp, moe_b_grp, moe_w_rt, moe_b_rt, moe_w_gate, moe_w_up, moe_w_down, ln2_g, ln2_b):
    B, S, D = x.shape
    assert D == D_MODEL and S % ROW_TILE == 0
    TP = S + FRONT
    N = B * TP
    topk = min(TOPK_MAX, S // 4)
    bias_tab = _bias_tables(rel_bias)

    h, hb = _embed(x, meta, ln_in_g, ln_in_b)
    h = h.reshape(N, D)
    hb = hb.reshape(N, D)
    for l in range(DEPTH):
        w_a, w_b, w_t, w_n, w_d, w_g = _split_w_in(w_in[l])
        p_a = _proj(hb, w_a)
        p_b = _proj(hb, w_b)
        p_d = _proj(hb, w_d)
        gates = _proj(hb, w_g, act="sigmoid")
        qt, qit, wit, k, ki, vt = _dsa_prep(hb, w_t, w_n, dsa_kv_norm_g[l], dsa_w_uk[l], dsa_w_uv[l])
        y_a = _rwkv(p_a, B, TP, rwkv_mu[l], rwkv_w_up[l], rwkv_w0[l], rwkv_a_up[l], rwkv_a0[l], rwkv_g_up[l],
                    rwkv_k_k[l], rwkv_k_a[l], rwkv_r_k[l], rwkv_gn_g[l], rwkv_gn_b[l])
        y_b = _gla(p_b, B, TP, gla_a_up[l], gla_a_b[l], gla_norm_g[l])
        y_c = _dsa(qt, qit, wit, k, ki, vt, bias_tab, B, TP, topk)
        y_d = _mlstm(p_d, B, TP, mlstm_conv_w[l], mlstm_conv_b[l], mlstm_i_b[l], mlstm_f_b[l], mlstm_norm_g[l])
        ys = (y_a.reshape(N, MIX_W), y_b.reshape(N, MIX_W), y_c, y_d.reshape(N, MIX_W))
        h1, h1b = _merge(h, gates, ys, w_branch[l], w_out[l], ln1_g[l], ln1_b[l])
        h, hb = _moe(h1, h1b, moe_w_grp[l], moe_b_grp[l], moe_w_rt[l], moe_b_rt[l],
                     moe_w_gate[l], moe_w_up[l], moe_w_down[l], ln2_g[l], ln2_b[l])
    return h.reshape(B, TP, D)[:, FRONT:]
```

```python
import functools
import math

import numpy as np
import jax
import jax.numpy as jnp
from jax import lax
from jax.experimental import pallas as pl
from jax.experimental.pallas import tpu as pltpu

F32 = jnp.float32
BF16 = jnp.bfloat16

D_MODEL = 1024
HEAD_DIM = 64
N_HEADS = 4
MIX_W = 256
N_META = 16
CHUNK = 64
LANES = 128
ROW_TILE = 128
FRONT = ROW_TILE
FP = FRONT - N_META
NEG = -1e30
LN_EPS = 1e-5
DEPTH = 2
DN_ALPHA = (2 * DEPTH) ** 0.25

RWKV_GN_EPS = HEAD_DIM * 1e-5
GLA_DK = 32
GLA_TAU = 16.0
DSA_KV_RANK = 128
IDX_HEADS = 8
IDX_DIM = 32
TOPK_MAX = 256
N_BUCKETS = 32
MAX_DISTANCE = 128
CONV_W = 4
N_GROUPS = 4
EPG = 4
N_EXPERTS = 16
D_EXPERT = 256

INT_MIN = -(2 ** 31)
KEY_INF = 0x7F800000
VMEM_LIMIT = 56 * 1024 * 1024


def _cparams(*sem):
    return pltpu.CompilerParams(dimension_semantics=tuple(sem), vmem_limit_bytes=VMEM_LIMIT)


def _pick_tile(n, target):
    best = LANES
    t = LANES
    while t <= min(n, target):
        if n % t == 0:
            best = t
        t += LANES
    return best


def _bdot(a, b):
    return jnp.dot(a.astype(BF16), b.astype(BF16), preferred_element_type=F32)


def _bdot_nt(a, b):
    return lax.dot_general(a.astype(BF16), b.astype(BF16), (((1,), (1,)), ((), ())),
                           preferred_element_type=F32)


def _bdot_tn(a, b):
    return lax.dot_general(a.astype(BF16), b.astype(BF16), (((0,), (0,)), ((), ())),
                           preferred_element_type=F32)


def _split(a):
    hi = a.astype(BF16)
    lo = (a - hi.astype(F32)).astype(BF16)
    return hi, lo


_NN = (((1,), (0,)), ((), ()))
_NT = (((1,), (1,)), ((), ()))
_TN = (((0,), (0,)), ((), ()))


def _dot3(a, b, dims=_NN):
    ah, al = _split(a)
    bh, bl = _split(b)
    dg = lambda x, y: lax.dot_general(x, y, dims, preferred_element_type=F32)
    return dg(ah, bh) + (dg(ah, bl) + dg(al, bh))


def _dot_exact_lhs(a_bf16, b):
    bh, bl = _split(b)
    return (jnp.dot(a_bf16, bh, preferred_element_type=F32)
            + jnp.dot(a_bf16, bl, preferred_element_type=F32))


def _dot_exact_rhs(a, b_bf16):
    ah, al = _split(a)
    return (jnp.dot(ah, b_bf16, preferred_element_type=F32)
            + jnp.dot(al, b_bf16, preferred_element_type=F32))


def _sigmoid(x):
    return 1.0 / (1.0 + jnp.exp(-x))


def _log_sigmoid(x):
    return jnp.minimum(x, 0.0) - jnp.log(1.0 + jnp.exp(-jnp.abs(x)))


def _silu(x):
    return x * _sigmoid(x)


def _iota(shape, dim):
    return lax.broadcasted_iota(jnp.int32, shape, dim)


def _tri_incl(n):
    return (_iota((n, n), 1) <= _iota((n, n), 0))


def _head_ones():
    return ((_iota((MIX_W, MIX_W), 0) // HEAD_DIM) == (_iota((MIX_W, MIX_W), 1) // HEAD_DIM)).astype(BF16)


def _row_ids(rows):
    return pl.program_id(1) * ROW_TILE + _iota((rows, 1), 0)


def _embed_kernel(x_ref, meta_ref, g_ref, b_ref, h_ref, hb_ref):
    j = pl.program_id(1)
    src = jnp.where(j == 0, meta_ref[...], x_ref[0])
    mu = jnp.mean(src, -1, keepdims=True)
    xc = src - mu
    var = jnp.mean(xc * xc, -1, keepdims=True)
    y = xc * lax.rsqrt(var + LN_EPS) * g_ref[...] + b_ref[...]
    h_ref[0] = y
    hb_ref[0] = y.astype(BF16)


def _embed(x, meta, g, b):
    B, S, D = x.shape
    TP = S + FRONT
    meta_pad = jnp.concatenate([jnp.zeros((FP, D), F32), meta.astype(F32)], axis=0)
    return pl.pallas_call(
        _embed_kernel,
        grid=(B, TP // ROW_TILE),
        in_specs=[
            pl.BlockSpec((1, ROW_TILE, D), lambda b, j: (b, jnp.maximum(j - 1, 0), 0)),
            pl.BlockSpec((ROW_TILE, D), lambda b, j: (0, 0)),
            pl.BlockSpec((1, D), lambda b, j: (0, 0)),
            pl.BlockSpec((1, D), lambda b, j: (0, 0)),
        ],
        out_specs=[
            pl.BlockSpec((1, ROW_TILE, D), lambda b, j: (b, j, 0)),
            pl.BlockSpec((1, ROW_TILE, D), lambda b, j: (b, j, 0)),
        ],
        out_shape=[jax.ShapeDtypeStruct((B, TP, D), F32), jax.ShapeDtypeStruct((B, TP, D), BF16)],
        compiler_params=_cparams("parallel", "arbitrary"),
        name="embed_ln",
    )(x, meta_pad, g.reshape(1, D), b.reshape(1, D))


def _proj_kernel(h_ref, w_ref, o_ref, *, act):
    y = jnp.dot(h_ref[...], w_ref[...], preferred_element_type=F32)
    if act == "sigmoid":
        y = _sigmoid(y)
    o_ref[...] = y.astype(o_ref.dtype)


def _proj(hb, w, act=None, out_dtype=F32):
    N, D = hb.shape
    W = w.shape[1]
    tn = W if W <= 1152 else 1024
    tm = _pick_tile(N, 640)
    return pl.pallas_call(
        functools.partial(_proj_kernel, act=act),
        grid=(W // tn, N // tm),
        in_specs=[pl.BlockSpec((tm, D), lambda j, i: (i, 0)),
                  pl.BlockSpec((D, tn), lambda j, i: (0, j))],
        out_specs=pl.BlockSpec((tm, tn), lambda j, i: (i, j)),
        out_shape=jax.ShapeDtypeStruct((N, W), out_dtype),
        compiler_params=_cparams("arbitrary", "arbitrary"),
        name="in_proj",
    )(hb, w)


def _rwkv_kernel(p_ref, mu_ref, wup_ref, w0_ref, aup_ref, a0_ref, gup_ref, kk_ref, ka_ref, rk_ref,
                 gng_ref, gnb_ref, y_ref, carry_ref, s_ref):
    j = pl.program_id(0)
    nb = p_ref.shape[0]
    n_chunks = ROW_TILE // CHUNK

    @pl.when(j == 0)
    def _():
        carry_ref[...] = jnp.zeros_like(carry_ref)
        s_ref[...] = jnp.zeros_like(s_ref)

    valid = (j * ROW_TILE + _iota((ROW_TILE, 1), 0)) >= FP
    first_row = _iota((ROW_TILE, 1), 0) == 0
    ones_h = _head_ones()
    tri = _tri_incl(CHUNK)
    tri_b = tri.astype(BF16)
    strict = _iota((CHUNK, CHUNK), 1) < _iota((CHUNK, CHUNK), 0)
    eye = (_iota((CHUNK, CHUNK), 1) == _iota((CHUNK, CHUNK), 0)).astype(F32)
    heads = [slice(h * HEAD_DIM, (h + 1) * HEAD_DIM) for h in range(N_HEADS)]

    pro = []
    unit = {}
    for b in range(nb):
        p = jnp.where(valid, p_ref[b], 0.0)
        prev = jnp.where(first_row, carry_ref[b], pltpu.roll(p, 1, 0))
        carry_ref[b] = p[ROW_TILE - 1:ROW_TILE, :]
        ps = p + (prev - p) * mu_ref[...]
        r = ps[:, 0:256]
        k = ps[:, 256:512]
        v = ps[:, 512:768]
        lora_in = ps[:, 768:896]
        xg = ps[:, 896:1024]
        w_log = _log_sigmoid(w0_ref[...] + _bdot(jnp.tanh(lora_in), wup_ref[...])) - 0.5
        lw = jnp.where(valid, -jnp.exp(w_log), 0.0)
        alpha = _sigmoid(a0_ref[...] + _bdot(lora_in, aup_ref[...]))
        gate = _bdot(_sigmoid(xg), gup_ref[...])
        kk = k * kk_ref[...]
        kk = kk / jnp.maximum(jnp.sqrt(_dot_exact_rhs(kk * kk, ones_h)), 1e-12)
        k = k * (1.0 + (alpha - 1.0) * ka_ref[...])
        kka = kk * alpha
        pro.append((r, k, v, gate))
        for c in range(n_chunks):
            sl = slice(c * CHUNK, (c + 1) * CHUNK)
            lw_c = lw[sl]
            cum = _dot_exact_lhs(tri_b, lw_c)
            cum_last = cum[CHUNK - 1:CHUNK, :]
            p_inv = jnp.exp(-cum)
            p_tail = jnp.exp(cum_last - cum)
            unit[b, c] = dict(a=-kk[sl] * jnp.exp(cum - lw_c), b=kka[sl] * p_inv, k=k[sl] * p_inv,
                              r=r[sl] * jnp.exp(cum), kb=k[sl] * p_tail, bb=kka[sl] * p_tail,
                              pl=jnp.exp(cum_last), v=v[sl])

    keys = [(b, c, h) for b in range(nb) for c in range(n_chunks) for h in range(N_HEADS)]
    part = lambda name, key: unit[key[0], key[1]][name][:, heads[key[2]]]
    a_ab = {q: jnp.where(strict, _dot3(part("a", q), part("b", q), _NT), 0.0) for q in keys}
    a_ak = {q: jnp.where(strict, _dot3(part("a", q), part("k", q), _NT), 0.0) for q in keys}
    a_rb = {q: jnp.where(tri, _bdot_nt(part("r", q), part("b", q)), 0.0) for q in keys}
    a_rk = {q: jnp.where(tri, _bdot_nt(part("r", q), part("k", q)), 0.0) for q in keys}
    inv = {q: eye + a_ab[q] for q in keys}
    pw = a_ab
    for _ in range(5):
        pw = {q: _dot3(pw[q], pw[q]) for q in keys}
        inv = {q: inv[q] + _dot3(inv[q], pw[q]) for q in keys}
    ak_v = {q: _dot3(a_ak[q], part("v", q)) for q in keys}
    rk_v = {q: _bdot(a_rk[q], part("v", q)) for q in keys}
    kb_v = {q: _bdot_tn(part("v", q), part("kb", q)) for q in keys}

    bh = [(b, h) for b in range(nb) for h in range(N_HEADS)]
    state = {q: s_ref[q[0], q[1]] for q in bh}
    y_parts = {}
    for c in range(n_chunks):
        full = lambda q: (q[0], c, q[1])
        a_s = {q: _dot3(part("a", full(q)), state[q], _NT) for q in bh}
        r_s = {q: _bdot_nt(part("r", full(q)), state[q]) for q in bh}
        u = {q: _dot3(inv[full(q)], a_s[q] + ak_v[full(q)]) for q in bh}
        for q in bh:
            y_parts[full(q)] = r_s[q] + rk_v[full(q)] + _bdot(a_rb[full(q)], u[q])
        state = {q: (state[q] * part("pl", full(q)) + kb_v[full(q)] + _bdot_tn(u[q], part("bb", full(q))))
                 for q in bh}
    for q in bh:
        s_ref[q[0], q[1]] = state[q]

    for b in range(nb):
        r, k, v, gate = pro[b]
        y = jnp.concatenate([jnp.concatenate([y_parts[b, c, h] for h in range(N_HEADS)], axis=1)
                             for c in range(n_chunks)], axis=0)
        mean = _dot_exact_rhs(y, ones_h) * (1.0 / HEAD_DIM)
        yc = y - mean
        var = _dot_exact_rhs(yc * yc, ones_h) * (1.0 / HEAD_DIM)
        yn = yc * lax.rsqrt(var + RWKV_GN_EPS) * gng_ref[...] + gnb_ref[...]
        bonus = _dot_exact_rhs(r * k * rk_ref[...], ones_h) * v
        y_ref[b] = ((yn + bonus) * gate).astype(y_ref.dtype)


def _rwkv(p_a, B, TP, mu, w_up, w0, a_up, a0, g_up, k_k, k_a, r_k, gn_g, gn_b):
    W = MIX_W
    z64 = jnp.zeros((64, W), F32)
    wup_pad = jnp.concatenate([w_up, z64], axis=0).astype(BF16)
    aup_pad = jnp.concatenate([z64, a_up], axis=0).astype(BF16)
    row = lambda a: a.reshape(1, -1).astype(F32)
    full = lambda shape: pl.BlockSpec(shape, lambda j: (0,) * len(shape))
    return pl.pallas_call(
        _rwkv_kernel,
        grid=(TP // ROW_TILE,),
        in_specs=[pl.BlockSpec((B, ROW_TILE, 1024), lambda j: (0, j, 0)),
                  full((1, 1024)), full((128, W)), full((1, W)), full((128, W)), full((1, W)),
                  full((128, W)), full((1, W)), full((1, W)), full((1, W)), full((1, W)), full((1, W))],
        out_specs=pl.BlockSpec((B, ROW_TILE, W), lambda j: (0, j, 0)),
        out_shape=jax.ShapeDtypeStruct((B, TP, W), BF16),
        scratch_shapes=[pltpu.VMEM((B, 1, 1024), F32), pltpu.VMEM((B, N_HEADS, HEAD_DIM, HEAD_DIM), F32)],
        compiler_params=_cparams("arbitrary"),
        name="rwkv7",
    )(p_a.reshape(B, TP, 1024), row(mu), wup_pad, row(w0), aup_pad, row(a0), g_up.astype(BF16),
      row(k_k), row(k_a), row(r_k), row(gn_g), row(gn_b))


def _gla_kernel(p_ref, aup_ref, ab_ref, ng_ref, y_ref, s_ref):
    j = pl.program_id(0)
    nb = p_ref.shape[0]
    n_chunks = ROW_TILE // CHUNK

    @pl.when(j == 0)
    def _():
        s_ref[...] = jnp.zeros_like(s_ref)

    valid = (j * ROW_TILE + _iota((ROW_TILE, 1), 0)) >= FP
    tri = _tri_incl(CHUNK)
    tri_b = tri.astype(BF16)

    og_all = []
    pre = {}
    for b in range(nb):
        p = jnp.where(valid, p_ref[b], 0.0)
        la = _log_sigmoid(_bdot(p[:, 768:896], aup_ref[...]) + ab_ref[...]) * (1.0 / GLA_TAU)
        la = jnp.where(valid, la, 0.0)
        og_all.append(p[:, 512:768])
        for c in range(n_chunks):
            sl = slice(c * CHUNK, (c + 1) * CHUNK)
            pre[b, c] = dict(q=p[sl, 0:128] * (GLA_DK ** -0.5), k=p[sl, 128:256], v=p[sl, 256:512], la=la[sl])
    bc = [(b, c) for b in range(nb) for c in range(n_chunks)]
    keys = [(b, c, h) for (b, c) in bc for h in range(N_HEADS)]
    ks = [slice(h * GLA_DK, (h + 1) * GLA_DK) for h in range(N_HEADS)]
    vs = [slice(h * HEAD_DIM, (h + 1) * HEAD_DIM) for h in range(N_HEADS)]
    b_cum = {u: _dot_exact_lhs(tri_b, pre[u]["la"]) for u in bc}
    b_last = {u: b_cum[u][CHUNK - 1:CHUNK, :] for u in bc}
    q_g = {u: pre[u]["q"] * jnp.exp(b_cum[u]) for u in bc}
    k_g = {u: pre[u]["k"] * jnp.exp(-b_cum[u]) for u in bc}
    k_l = {u: pre[u]["k"] * jnp.exp(b_last[u] - b_cum[u]) for u in bc}
    dec = {u: jnp.exp(b_last[u]) for u in bc}
    att = {u: jnp.where(tri, _bdot_nt(q_g[u[0], u[1]][:, ks[u[2]]], k_g[u[0], u[1]][:, ks[u[2]]]), 0.0)
           for u in keys}
    att_v = {u: _bdot(att[u], pre[u[0], u[1]]["v"][:, vs[u[2]]]) for u in keys}
    kl_v = {u: _bdot_tn(pre[u[0], u[1]]["v"][:, vs[u[2]]], k_l[u[0], u[1]][:, ks[u[2]]]) for u in keys}

    bh = [(b, h) for b in range(nb) for h in range(N_HEADS)]
    state = {q: s_ref[q[0], q[1]] for q in bh}
    o_parts = {}
    for c in range(n_chunks):
        for q in bh:
            o_parts[q[0], c, q[1]] = att_v[q[0], c, q[1]] + _bdot_nt(q_g[q[0], c][:, ks[q[1]]], state[q])
        state = {q: state[q] * dec[q[0], c][:, ks[q[1]]] + kl_v[q[0], c, q[1]] for q in bh}
    for q in bh:
        s_ref[q[0], q[1]] = state[q]

    ones_h = _head_ones()
    for b in range(nb):
        o = jnp.concatenate([jnp.concatenate([o_parts[b, c, h] for h in range(N_HEADS)], axis=1)
                             for c in range(n_chunks)], axis=0)
        ms = _dot_exact_rhs(o * o, ones_h) * (1.0 / HEAD_DIM)
        y = o * lax.rsqrt(ms + 1e-6) * ng_ref[...] * _silu(og_all[b])
        y_ref[b] = y.astype(y_ref.dtype)


def _gla(p_b, B, TP, a_up, a_b, norm_g):
    aup_pad = jnp.zeros((128, 128), F32).at[:a_up.shape[0]].set(a_up).astype(BF16)
    full = lambda shape: pl.BlockSpec(shape, lambda j: (0,) * len(shape))
    return pl.pallas_call(
        _gla_kernel,
        grid=(TP // ROW_TILE,),
        in_specs=[pl.BlockSpec((B, ROW_TILE, 896), lambda j: (0, j, 0)),
                  full((128, 128)), full((1, 128)), full((1, MIX_W))],
        out_specs=pl.BlockSpec((B, ROW_TILE, MIX_W), lambda j: (0, j, 0)),
        out_shape=jax.ShapeDtypeStruct((B, TP, MIX_W), BF16),
        scratch_shapes=[pltpu.VMEM((B, N_HEADS, HEAD_DIM, GLA_DK), F32)],
        compiler_params=_cparams("arbitrary"),
        name="gla",
    )(p_b.reshape(B, TP, 896), aup_pad, a_b.reshape(1, 128).astype(F32),
      jnp.tile(norm_g.astype(F32), N_HEADS).reshape(1, MIX_W))


def _mlstm_kernel(p_ref, cw_ref, cb_ref, ib_ref, fb_ref, ng_ref, y_ref, carry_ref, c_ref, n_ref, m_ref):
    j = pl.program_id(0)
    nb = p_ref.shape[0]
    n_chunks = ROW_TILE // CHUNK

    @pl.when(j == 0)
    def _():
        carry_ref[...] = jnp.zeros_like(carry_ref)
        c_ref[...] = jnp.zeros_like(c_ref)
        n_ref[...] = jnp.zeros_like(n_ref)
        m_ref[...] = jnp.zeros_like(m_ref)

    valid = (j * ROW_TILE + _iota((ROW_TILE, 1), 0)) >= FP
    tri = _tri_incl(CHUNK)
    tri_b = tri.astype(BF16)
    ones_h = _head_ones()

    og_all = []
    pre = {}
    for b in range(nb):
        p = jnp.where(valid, p_ref[b], 0.0)
        a = p[:, 0:512]
        ext = jnp.concatenate([carry_ref[b], a], axis=0)
        carry_ref[b] = a[ROW_TILE - 8:ROW_TILE, :]
        conv = cb_ref[...] + a * cw_ref[CONV_W - 1:CONV_W, :]
        for s in range(1, CONV_W):
            conv = conv + pltpu.roll(ext, s, 0)[8:8 + ROW_TILE, :] * cw_ref[CONV_W - 1 - s:CONV_W - s, :]
        qk = _silu(conv)
        q = jnp.where(valid, qk[:, 0:MIX_W], 0.0)
        k = jnp.where(valid, qk[:, MIX_W:2 * MIX_W], 0.0) * (HEAD_DIM ** -0.5)
        v = p[:, 512:768]
        og_all.append(p[:, 768:1024])
        gates = p[:, 1024:1152]
        li_all = jnp.where(valid, gates + ib_ref[...], NEG)
        lf_all = jnp.where(valid, _log_sigmoid(gates + fb_ref[...]), 0.0)
        for c in range(n_chunks):
            sl = slice(c * CHUNK, (c + 1) * CHUNK)
            pre[b, c] = dict(q=q[sl], k=k[sl], v=v[sl], li=li_all[sl], lf=lf_all[sl])

    bc = [(b, c) for b in range(nb) for c in range(n_chunks)]
    keys = [(b, c, h) for (b, c) in bc for h in range(N_HEADS)]
    heads = [slice(h * HEAD_DIM, (h + 1) * HEAD_DIM) for h in range(N_HEADS)]
    part = lambda name, u: pre[u[0], u[1]][name][:, heads[u[2]]]
    b_cum = {u: _dot_exact_lhs(tri_b, pre[u]["lf"]) for u in bc}
    b_t = {u: b_cum[u].T for u in bc}
    li_t = {u: pre[u]["li"].T for u in bc}
    b_col = {u: b_cum[u[0], u[1]][:, N_HEADS + u[2]:N_HEADS + u[2] + 1] for u in keys}
    b_last = {u: b_col[u][CHUNK - 1:CHUNK, :] for u in keys}
    d_log = {u: jnp.where(tri, b_col[u] - b_t[u[0], u[1]][N_HEADS + u[2]:N_HEADS + u[2] + 1, :]
                          + li_t[u[0], u[1]][u[2]:u[2] + 1, :], -jnp.inf) for u in keys}
    dmax = {u: jnp.max(d_log[u], axis=1, keepdims=True) for u in keys}
    qk = {u: _bdot_nt(part("q", u), part("k", u)) for u in keys}
    s0 = {u: jnp.exp(d_log[u] - dmax[u]) * qk[u] for u in keys}
    sv = {u: _bdot(s0[u], part("v", u)) for u in keys}
    ssum = {u: jnp.sum(s0[u], axis=1, keepdims=True) for u in keys}
    g_loc = {u: b_last[u] - b_col[u] + pre[u[0], u[1]]["li"][:, u[2]:u[2] + 1] for u in keys}
    m_loc = {u: jnp.max(g_loc[u], axis=0, keepdims=True) for u in keys}
    kw = {u: part("k", u) * jnp.exp(g_loc[u] - m_loc[u]) for u in keys}
    kwv = {u: _bdot_tn(kw[u], part("v", u)) for u in keys}
    kwsum = {u: jnp.sum(kw[u], axis=0, keepdims=True) for u in keys}

    bh = [(b, h) for b in range(nb) for h in range(N_HEADS)]
    c_st = {q: c_ref[q[0], q[1]] for q in bh}
    n_st = {q: n_ref[q[0], q[1]] for q in bh}
    m_st = {q: m_ref[q[0], q[1]] for q in bh}
    h_parts = {}
    for c in range(n_chunks):
        full = lambda q: (q[0], c, q[1])
        qc = {q: _bdot(part("q", full(q)), c_st[q]) for q in bh}
        qn = {q: jnp.sum(part("q", full(q)) * n_st[q], axis=1, keepdims=True) for q in bh}
        inter = {q: b_col[full(q)] + m_st[q] for q in bh}
        m_t = {q: jnp.maximum(inter[q], dmax[full(q)]) for q in bh}
        e_loc = {q: jnp.exp(dmax[full(q)] - m_t[q]) for q in bh}
        w_int = {q: jnp.exp(inter[q] - m_t[q]) for q in bh}
        for q in bh:
            num = e_loc[q] * sv[full(q)] + w_int[q] * qc[q]
            den = e_loc[q] * ssum[full(q)] + w_int[q] * qn[q]
            h_parts[full(q)] = num / jnp.maximum(jnp.abs(den), jnp.exp(-m_t[q]))
        m_new = {q: jnp.maximum(b_last[full(q)] + m_st[q], m_loc[full(q)]) for q in bh}
        s_old = {q: jnp.exp(b_last[full(q)] + m_st[q] - m_new[q]) for q in bh}
        s_new = {q: jnp.exp(m_loc[full(q)] - m_new[q]) for q in bh}
        c_st = {q: s_old[q] * c_st[q] + s_new[q] * kwv[full(q)] for q in bh}
        n_st = {q: s_old[q] * n_st[q] + s_new[q] * kwsum[full(q)] for q in bh}
        m_st = m_new
    for q in bh:
        c_ref[q[0], q[1]], n_ref[q[0], q[1]], m_ref[q[0], q[1]] = c_st[q], n_st[q], m_st[q]

    for b in range(nb):
        hh = jnp.concatenate([jnp.concatenate([h_parts[b, c, h] for h in range(N_HEADS)], axis=1)
                              for c in range(n_chunks)], axis=0) * _sigmoid(og_all[b])
        mean = _dot_exact_rhs(hh, ones_h) * (1.0 / HEAD_DIM)
        hc = hh - mean
        var = _dot_exact_rhs(hc * hc, ones_h) * (1.0 / HEAD_DIM)
        y_ref[b] = (hc * lax.rsqrt(var + 1e-5) * ng_ref[...]).astype(y_ref.dtype)


def _mlstm(p_d, B, TP, conv_w, conv_b, i_b, f_b, norm_g):
    ib = jnp.zeros((1, LANES), F32).at[0, 0:N_HEADS].set(i_b)
    fb = jnp.zeros((1, LANES), F32).at[0, N_HEADS:2 * N_HEADS].set(f_b)
    full = lambda shape: pl.BlockSpec(shape, lambda j: (0,) * len(shape))
    return pl.pallas_call(
        _mlstm_kernel,
        grid=(TP // ROW_TILE,),
        in_specs=[pl.BlockSpec((B, ROW_TILE, 1152), lambda j: (0, j, 0)),
                  full((CONV_W, 512)), full((1, 512)), full((1, LANES)), full((1, LANES)), full((1, MIX_W))],
        out_specs=pl.BlockSpec((B, ROW_TILE, MIX_W), lambda j: (0, j, 0)),
        out_shape=jax.ShapeDtypeStruct((B, TP, MIX_W), BF16),
        scratch_shapes=[pltpu.VMEM((B, 8, 512), F32),
                        pltpu.VMEM((B, N_HEADS, HEAD_DIM, HEAD_DIM), F32),
                        pltpu.VMEM((B, N_HEADS, 1, HEAD_DIM), F32),
                        pltpu.VMEM((B, N_HEADS, 1, 1), F32)],
        compiler_params=_cparams("arbitrary"),
        name="mlstm",
    )(p_d.reshape(B, TP, 1152), conv_w.astype(F32), conv_b.reshape(1, 512).astype(F32), ib, fb,
      norm_g.reshape(1, MIX_W).astype(F32))


V_ROWS = 80
WT_ROWS = 528


def _dsa_prep_kernel(h_ref, wt_ref, wn_ref, kvg_ref, wuk_ref, wuvt_ref,
                     qt_ref, qit_ref, wit_ref, k_ref, ki_ref, vt_ref):
    hb = h_ref[...]
    tm = hb.shape[0]
    pt = lax.dot_general(wt_ref[...], hb, _NT, preferred_element_type=F32)
    pn = jnp.dot(hb, wn_ref[...], preferred_element_type=F32)
    ckv = pn[:, 0:DSA_KV_RANK]
    c = ckv * lax.rsqrt(jnp.mean(ckv * ckv, -1, keepdims=True) + 1e-6) * kvg_ref[...]
    cb = c.astype(BF16)
    k_ref[...] = jnp.dot(cb, wuk_ref[...], preferred_element_type=F32).astype(BF16)
    ki_ref[...] = pn[:, DSA_KV_RANK:DSA_KV_RANK + IDX_DIM].astype(BF16)
    vt = lax.dot_general(wuvt_ref[...], cb, _NT, preferred_element_type=F32)
    vt = jnp.where(_iota((V_ROWS, tm), 0) == HEAD_DIM, 1.0, vt)
    for t in range(tm // LANES):
        cs = slice(t * LANES, (t + 1) * LANES)
        for h in range(N_HEADS):
            qt_ref[t, :, h * LANES:(h + 1) * LANES] = (
                pt[h * HEAD_DIM:(h + 1) * HEAD_DIM, cs] * (HEAD_DIM ** -0.5)).astype(BF16)
        for h in range(IDX_HEADS):
            qit_ref[t, :, h * LANES:(h + 1) * LANES] = pt[MIX_W + h * IDX_DIM:MIX_W + (h + 1) * IDX_DIM, cs].astype(BF16)
        wit_ref[t] = pt[2 * MIX_W:2 * MIX_W + IDX_HEADS, cs] * ((IDX_HEADS * IDX_DIM) ** -0.5)
        vt_ref[t] = vt[:, cs].astype(BF16)


def _dsa_prep(hb, w_t, w_n, kv_norm_g, w_uk, w_uv):
    N, D = hb.shape
    tm = _pick_tile(N, 640)
    nt = tm // LANES
    full = lambda shape: pl.BlockSpec(shape, lambda i: (0,) * len(shape))
    wuvt = jnp.pad(w_uv.T, ((0, V_ROWS - HEAD_DIM), (0, 0))).astype(BF16)
    return pl.pallas_call(
        _dsa_prep_kernel,
        grid=(N // tm,),
        in_specs=[pl.BlockSpec((tm, D), lambda i: (i, 0)),
                  full((WT_ROWS, D)), full((D, 256)), full((1, DSA_KV_RANK)),
                  full((DSA_KV_RANK, HEAD_DIM)), full((V_ROWS, DSA_KV_RANK))],
        out_specs=[pl.BlockSpec((nt, HEAD_DIM, N_HEADS * LANES), lambda i: (i, 0, 0)),
                   pl.BlockSpec((nt, IDX_DIM, IDX_HEADS * LANES), lambda i: (i, 0, 0)),
                   pl.BlockSpec((nt, IDX_HEADS, LANES), lambda i: (i, 0, 0)),
                   pl.BlockSpec((tm, HEAD_DIM), lambda i: (i, 0)),
                   pl.BlockSpec((tm, IDX_DIM), lambda i: (i, 0)),
                   pl.BlockSpec((nt, V_ROWS, LANES), lambda i: (i, 0, 0))],
        out_shape=[jax.ShapeDtypeStruct((N // LANES, HEAD_DIM, N_HEADS * LANES), BF16),
                   jax.ShapeDtypeStruct((N // LANES, IDX_DIM, IDX_HEADS * LANES), BF16),
                   jax.ShapeDtypeStruct((N // LANES, IDX_HEADS, LANES), F32),
                   jax.ShapeDtypeStruct((N, HEAD_DIM), BF16),
                   jax.ShapeDtypeStruct((N, IDX_DIM), BF16),
                   jax.ShapeDtypeStruct((N // LANES, V_ROWS, LANES), BF16)],
        compiler_params=_cparams("arbitrary"),
        name="dsa_prep",
    )(hb, w_t, w_n, kv_norm_g.reshape(1, DSA_KV_RANK).astype(F32), w_uk.astype(BF16), wuvt)


def _loop_groups(lo, hi, fn):
    n = jnp.maximum(hi - lo, 0)
    n4 = lax.shift_right_logical(n, 2)

    def body(j, c):
        fn([lo + 4 * j + u for u in range(4)])
        return c

    lax.fori_loop(0, n4, body, 0)
    rest = lo + 4 * n4

    @pl.when((n & 2) == 2)
    def _():
        fn([rest, rest + 1])

    @pl.when((n & 1) == 1)
    def _():
        fn([hi - 1])


def _dsa_kernel(qt_ref, qit_ref, wit_ref, k_ref, ki_ref, vt_ref, bias_ref, y_ref,
                sk_ref, m_ref, acc_ref, *, topk):
    i = pl.program_id(1)
    nk = i + 1
    QT = ROW_TILE
    HQ = N_HEADS * QT
    t_lane = i * QT + _iota((LANES, QT), 1)
    key_pos = lambda kt: kt * LANES + _iota((LANES, QT), 0)
    rows = lambda kt: pl.ds(pl.multiple_of(kt * LANES, LANES), LANES)
    per_head = lambda fn: jnp.concatenate([fn(slice(h * QT, (h + 1) * QT)) for h in range(N_HEADS)], axis=1)

    qit = qit_ref[0]
    wit = wit_ref[0]

    def score_tile(kt, edge):
        rel = jnp.dot(ki_ref[rows(kt), :], qit, preferred_element_type=F32)
        score = jnp.maximum(rel[:, 0:QT], 0.0) * wit[0:1, :]
        for h in range(1, IDX_HEADS):
            score = score + jnp.maximum(rel[:, h * QT:(h + 1) * QT], 0.0) * wit[h:h + 1, :]
        score = jnp.where(score == 0.0, 0.0, score)
        bits = lax.bitcast_convert_type(score, jnp.int32)
        key = jnp.where(bits < 0, bits ^ jnp.int32(0x7FFFFFFF), bits)
        if edge:
            s_pos = key_pos(kt)
            key = jnp.where(s_pos < FP + N_META, jnp.int32(KEY_INF), key)
            key = jnp.where((s_pos >= FP) & (s_pos <= t_lane), key, jnp.int32(INT_MIN))
        sk_ref[kt] = key

    score_tile(0, True)
    _loop_groups(1, i, lambda kts: [score_tile(kt, False) for kt in kts])

    @pl.when(i > 0)
    def _():
        score_tile(i, True)

    def count(pred_fn):
        def body(kt, acc):
            return acc + jnp.where(pred_fn(sk_ref[kt], kt), 1, 0)

        def body4(j, acc):
            for u in range(4):
                acc = body(4 * j + u, acc)
            return acc

        n4 = lax.shift_right_logical(nk, 2)
        acc = lax.fori_loop(0, n4, body4, jnp.zeros((LANES, QT), jnp.int32))
        acc = lax.fori_loop(4 * n4, nk, body, acc)
        return jnp.sum(acc, axis=0, keepdims=True)

    def bit_body(it, tau):
        cand = tau + jnp.left_shift(jnp.int32(1), 31 - it)
        cnt = count(lambda sk, kt: sk >= cand)
        return jnp.where(cnt >= topk, cand, tau)

    tau = lax.fori_loop(0, 32, bit_body, jnp.full((1, QT), INT_MIN, jnp.int32))
    tau = jnp.maximum(tau, jnp.int32(INT_MIN + 1))
    n_gt = count(lambda sk, kt: sk > tau)
    n_ge = count(lambda sk, kt: sk >= tau)
    need = topk - n_gt

    @pl.when(jnp.max(n_ge - topk) > 0)
    def _():
        n_bits = max(1, int(math.ceil(math.log2(sk_ref.shape[0] * LANES + 1))))

        def pos_body(it, x):
            cand = x + jnp.left_shift(jnp.int32(1), n_bits - 1 - it)
            cnt = count(lambda sk, kt: (sk == tau) & (key_pos(kt) < cand))
            return jnp.where(cnt < need, cand, x)

        x = lax.fori_loop(0, n_bits, pos_body, jnp.zeros((1, QT), jnp.int32))
        jmax = jnp.where(n_ge > topk, x, jnp.int32(2 ** 30))

        def drop_body(kt, c):
            sk = sk_ref[kt]
            sk_ref[kt] = jnp.where((sk == tau) & (key_pos(kt) > jmax), jnp.int32(INT_MIN), sk)
            return c

        lax.fori_loop(0, nk, drop_body, 0)

    qt = qt_ref[0]
    m_ref[...] = jnp.full((1, HQ), NEG, F32)
    acc_ref[...] = jnp.zeros((V_ROWS, HQ), F32)

    def attend(kts, near):
        lgs = []
        for kt in kts:
            lg = jnp.dot(k_ref[rows(kt), :], qt, preferred_element_type=F32)
            if near is not None:
                lg = lg + bias_ref[near]
            sel = sk_ref[kt] >= tau
            lgs.append(per_head(lambda hs: jnp.where(sel, lg[:, hs], NEG)))
        tmax = lgs[0]
        for lg in lgs[1:]:
            tmax = jnp.maximum(tmax, lg)
        m_old = m_ref[...]
        m_new = jnp.maximum(m_old, jnp.max(tmax, axis=0, keepdims=True))
        pv = None
        for kt, lg in zip(kts, lgs):
            t = jnp.dot(vt_ref[kt], jnp.exp(lg - m_new).astype(BF16), preferred_element_type=F32)
            pv = t if pv is None else pv + t
        acc_ref[...] = acc_ref[...] * jnp.exp(m_old - m_new) + pv
        m_ref[...] = m_new

    attend([i], 0)

    @pl.when(i > 0)
    def _():
        attend([i - 1], 1)

    _loop_groups(0, i - 1, lambda kts: attend(kts, None))
    acc = acc_ref[...]
    out = acc[0:HEAD_DIM, :] / jnp.maximum(acc[HEAD_DIM:HEAD_DIM + 1, :], 1e-30)
    y_ref[...] = per_head(lambda hs: out[:, hs].T).astype(y_ref.dtype)


def _t5_bucket(dist):
    max_exact = N_BUCKETS // 2
    n = jnp.maximum(dist, 0)
    large = max_exact + (jnp.log(jnp.maximum(n, 1).astype(F32) / max_exact)
                         / math.log(MAX_DISTANCE / max_exact) * (N_BUCKETS - max_exact)).astype(jnp.int32)
    return jnp.where(n < max_exact, n, jnp.minimum(large, N_BUCKETS - 1))


def _bias_tables(rel_bias):
    per_dist = rel_bias[_t5_bucket(jnp.arange(2 * ROW_TILE, dtype=jnp.int32))]
    q_minus_s = np.arange(ROW_TILE)[None, :] - np.arange(ROW_TILE)[:, None]
    far = per_dist[2 * ROW_TILE - 1]
    tabs = [per_dist[np.clip(r * ROW_TILE + q_minus_s, 0, 2 * ROW_TILE - 1)] - far for r in (0, 1)]
    return jnp.stack(tabs).transpose(0, 1, 3, 2).reshape(2, ROW_TILE, N_HEADS * ROW_TILE).astype(F32)


def _dsa(qt, qit, wit, k, ki, vt, bias_tab, B, TP, topk):
    nq = TP // ROW_TILE
    return pl.pallas_call(
        functools.partial(_dsa_kernel, topk=topk),
        grid=(B, nq),
        in_specs=[pl.BlockSpec((1, HEAD_DIM, N_HEADS * LANES), lambda b, i: (b * nq + i, 0, 0)),
                  pl.BlockSpec((1, IDX_DIM, IDX_HEADS * LANES), lambda b, i: (b * nq + i, 0, 0)),
                  pl.BlockSpec((1, IDX_HEADS, LANES), lambda b, i: (b * nq + i, 0, 0)),
                  pl.BlockSpec((TP, HEAD_DIM), lambda b, i: (b, 0)),
                  pl.BlockSpec((TP, IDX_DIM), lambda b, i: (b, 0)),
                  pl.BlockSpec((nq, V_ROWS, LANES), lambda b, i: (b, 0, 0)),
                  pl.BlockSpec((2, ROW_TILE, N_HEADS * ROW_TILE), lambda b, i: (0, 0, 0))],
        out_specs=pl.BlockSpec((ROW_TILE, MIX_W), lambda b, i: (b * nq + i, 0)),
        out_shape=jax.ShapeDtypeStruct((B * TP, MIX_W), BF16),
        scratch_shapes=[pltpu.VMEM((nq, LANES, ROW_TILE), jnp.int32),
                        pltpu.VMEM((1, N_HEADS * ROW_TILE), F32),
                        pltpu.VMEM((V_ROWS, N_HEADS * ROW_TILE), F32)],
        compiler_params=_cparams("parallel", "arbitrary"),
        name="dsa_attend",
    )(qt, qit, wit, k, ki, vt, bias_tab)


def _layer_norm_rows(z, g, b):
    mu = jnp.mean(z, -1, keepdims=True)
    zc = z - mu
    var = jnp.mean(zc * zc, -1, keepdims=True)
    return zc * lax.rsqrt(var + LN_EPS) * g + b


def _merge_kernel(h_ref, g_ref, ya_ref, yb_ref, yc_ref, yd_ref, wb_ref, wo_ref, lg_ref, lb_ref,
                  h1_ref, h1b_ref):
    merged = None
    for i, y_ref in enumerate((ya_ref, yb_ref, yc_ref, yd_ref)):
        t = g_ref[:, i * D_MODEL:(i + 1) * D_MODEL] * jnp.dot(y_ref[...], wb_ref[i], preferred_element_type=F32)
        merged = t if merged is None else merged + t
    z = DN_ALPHA * h_ref[...] + jnp.dot(merged.astype(BF16), wo_ref[...], preferred_element_type=F32)
    y = _layer_norm_rows(z, lg_ref[...], lb_ref[...])
    h1_ref[...] = y
    h1b_ref[...] = y.astype(BF16)


def _merge(h, gates, ys, w_branch, w_out, ln_g, ln_b):
    N, D = h.shape
    tm = _pick_tile(N, 512)
    full = lambda shape: pl.BlockSpec(shape, lambda i: (0,) * len(shape))
    tok = lambda w: pl.BlockSpec((tm, w), lambda i: (i, 0))
    return pl.pallas_call(
        _merge_kernel,
        grid=(N // tm,),
        in_specs=[tok(D), tok(4 * D), tok(MIX_W), tok(MIX_W), tok(MIX_W), tok(MIX_W),
                  full((4, MIX_W, D)), full((D, D)), full((1, D)), full((1, D))],
        out_specs=[tok(D), tok(D)],
        out_shape=[jax.ShapeDtypeStruct((N, D), F32), jax.ShapeDtypeStruct((N, D), BF16)],
        compiler_params=_cparams("arbitrary"),
        name="merge_out_ln",
    )(h, gates, *ys, w_branch.astype(BF16), w_out.astype(BF16),
      ln_g.reshape(1, D).astype(F32), ln_b.reshape(1, D).astype(F32))


def _moe_kernel(h_ref, hb_ref, wr_ref, br_ref, wg_ref, wu_ref, wd_ref, lg_ref, lb_ref, o_ref, ob_ref,
                gate_ref, acc_ref):
    e = pl.program_id(1)
    xb = hb_ref[...]
    tm = xb.shape[0]
    lane = _iota((tm, LANES), 1)

    @pl.when(e == 0)
    def _():
        logit = jnp.dot(xb, wr_ref[...], preferred_element_type=F32) + br_ref[...]
        big = jnp.int32(LANES)
        gl = jnp.where(lane < N_GROUPS, logit, -jnp.inf)
        gmax = jnp.max(gl, axis=1, keepdims=True)
        g_sel = jnp.min(jnp.where(gl == gmax, lane, big), axis=1, keepdims=True)
        p_grp = 1.0 / jnp.sum(jnp.exp(gl - gmax), axis=1, keepdims=True)
        lo = N_GROUPS + g_sel * EPG
        el = jnp.where((lane >= lo) & (lane < lo + EPG), logit, -jnp.inf)
        v1 = jnp.max(el, axis=1, keepdims=True)
        i1 = jnp.min(jnp.where(el == v1, lane, big), axis=1, keepdims=True)
        el2 = jnp.where(lane == i1, -jnp.inf, el)
        v2 = jnp.max(el2, axis=1, keepdims=True)
        i2 = jnp.min(jnp.where(el2 == v2, lane, big), axis=1, keepdims=True)
        e2 = jnp.exp(v2 - v1)
        w1 = p_grp / (1.0 + e2)
        w2 = p_grp * e2 / (1.0 + e2)
        gate_ref[...] = jnp.where(lane == i1, w1, 0.0) + jnp.where(lane == i2, w2, 0.0)
        acc_ref[...] = jnp.zeros_like(acc_ref)

    g_e = jnp.sum(jnp.where(lane == e + N_GROUPS, gate_ref[...], 0.0), axis=1, keepdims=True)
    hid = _silu(jnp.dot(xb, wg_ref[0], preferred_element_type=F32)) * jnp.dot(xb, wu_ref[0], preferred_element_type=F32)
    acc_ref[...] += g_e * jnp.dot(hid.astype(BF16), wd_ref[0], preferred_element_type=F32)

    @pl.when(e == N_EXPERTS - 1)
    def _():
        y = _layer_norm_rows(DN_ALPHA * h_ref[...] + acc_ref[...], lg_ref[...], lb_ref[...])
        o_ref[...] = y
        ob_ref[...] = y.astype(BF16)


def _moe(h1, h1b, w_grp, b_grp, w_rt, b_rt, w_gate, w_up, w_down, ln_g, ln_b):
    N, D = h1.shape
    tm = _pick_tile(N, 640)
    w_r = jnp.zeros((D, LANES), F32).at[:, 0:N_GROUPS].set(w_grp).at[:, N_GROUPS:N_GROUPS + N_EXPERTS].set(w_rt)
    b_r = jnp.zeros((1, LANES), F32).at[0, 0:N_GROUPS].set(b_grp).at[0, N_GROUPS:N_GROUPS + N_EXPERTS].set(b_rt)
    full = lambda shape: pl.BlockSpec(shape, lambda i, e: (0,) * len(shape))
    tok = lambda w: pl.BlockSpec((tm, w), lambda i, e: (i, 0))
    return pl.pallas_call(
        _moe_kernel,
        grid=(N // tm, N_EXPERTS),
        in_specs=[tok(D), tok(D), full((D, LANES)), full((1, LANES)),
                  pl.BlockSpec((1, D, D_EXPERT), lambda i, e: (e, 0, 0)),
                  pl.BlockSpec((1, D, D_EXPERT), lambda i, e: (e, 0, 0)),
                  pl.BlockSpec((1, D_EXPERT, D), lambda i, e: (e, 0, 0)),
                  full((1, D)), full((1, D))],
        out_specs=[tok(D), tok(D)],
        out_shape=[jax.ShapeDtypeStruct((N, D), F32), jax.ShapeDtypeStruct((N, D), BF16)],
        scratch_shapes=[pltpu.VMEM((tm, LANES), F32), pltpu.VMEM((tm, D), F32)],
        compiler_params=_cparams("arbitrary", "arbitrary"),
        name="hier_moe_ln",
    )(h1, h1b, w_r.astype(BF16), b_r, w_gate.astype(BF16), w_up.astype(BF16), w_down.astype(BF16),
      ln_g.reshape(1, D).astype(F32), ln_b.reshape(1, D).astype(F32))


def _pad_cols(w, width):
    return jnp.pad(w, ((0, 0), (0, width - w.shape[1])))


def _split_w_in(w):
    o = 0
    w_a = w[:, o:o + 1024]; o += 1024
    gq, gk, gv, ga, gg = (w[:, o:o + 128], w[:, o + 128:o + 256], w[:, o + 256:o + 512],
                          w[:, o + 512:o + 528], w[:, o + 528:o + 784]); o += 784
    w_b = _pad_cols(jnp.concatenate([gq, gk, gv, gg, ga], axis=1), 896)
    cq, ckv, cqi, cki, cwi = (w[:, o:o + 256], w[:, o + 256:o + 384], w[:, o + 384:o + 640],
                              w[:, o + 640:o + 672], w[:, o + 672:o + 680]); o += 680
    w_t = jnp.pad(jnp.concatenate([cq.T, cqi.T, cwi.T], axis=0), ((0, WT_ROWS - 2 * MIX_W - IDX_HEADS), (0, 0)))
    w_n = _pad_cols(jnp.concatenate([ckv, cki], axis=1), 256)
    dq, dk, dv, di, df, do = (w[:, o:o + 256], w[:, o + 256:o + 512], w[:, o + 512:o + 768],
                              w[:, o + 768:o + 772], w[:, o + 772:o + 776], w[:, o + 776:o + 1032]); o += 1032
    w_d = _pad_cols(jnp.concatenate([dq, dk, dv, do, di, df], axis=1), 1152)
    w_g = w[:, o:o + 4096]
    bf = lambda a: a.astype(BF16)
    return bf(w_a), bf(w_b), bf(w_t), bf(w_n), bf(w_d), bf(w_g)


def kernel(x, meta, ln_in_g, ln_in_b, rel_bias, w_in, rwkv_mu, rwkv_w_up, rwkv_w0, rwkv_a_up, rwkv_a0, rwkv_g_up, rwkv_k_k, rwkv_k_a, rwkv_r_k, rwkv_gn_g, rwkv_gn_b, gla_a_up, gla_a_b, gla_norm_g, dsa_kv_norm_g, dsa_w_uk, dsa_w_uv, mlstm_conv_w, mlstm_conv_b, mlstm_i_b, mlstm_f_b, mlstm_norm_g, w_branch, w_out, ln1_g, ln1_b, moe_w_grp, moe_b_grp, moe_w_rt, moe_b_rt, moe_w_gate, moe_w_up, moe_w_down, ln2_g, ln2_b):
    B, S, D = x.shape
    assert D == D_MODEL and S % ROW_TILE == 0
    TP = S + FRONT
    N = B * TP
    topk = min(TOPK_MAX, S // 4)
    bias_tab = _bias_tables(rel_bias)

    h, hb = _embed(x, meta, ln_in_g, ln_in_b)
    h = h.reshape(N, D)
    hb = hb.reshape(N, D)
    for l in range(DEPTH):
        w_a, w_b, w_t, w_n, w_d, w_g = _split_w_in(w_in[l])
        p_a = _proj(hb, w_a)
        p_b = _proj(hb, w_b)
        p_d = _proj(hb, w_d)
        gates = _proj(hb, w_g, act="sigmoid")
        qt, qit, wit, k, ki, vt = _dsa_prep(hb, w_t, w_n, dsa_kv_norm_g[l], dsa_w_uk[l], dsa_w_uv[l])
        y_a = _rwkv(p_a, B, TP, rwkv_mu[l], rwkv_w_up[l], rwkv_w0[l], rwkv_a_up[l], rwkv_a0[l], rwkv_g_up[l],
                    rwkv_k_k[l], rwkv_k_a[l], rwkv_r_k[l], rwkv_gn_g[l], rwkv_gn_b[l])
        y_b = _gla(p_b, B, TP, gla_a_up[l], gla_a_b[l], gla_norm_g[l])
        y_c = _dsa(qt, qit, wit, k, ki, vt, bias_tab, B, TP, topk)
        y_d = _mlstm(p_d, B, TP, mlstm_conv_w[l], mlstm_conv_b[l], mlstm_i_b[l], mlstm_f_b[l], mlstm_norm_g[l])
        ys = (y_a.reshape(N, MIX_W), y_b.reshape(N, MIX_W), y_c, y_d.reshape(N, MIX_W))
        h1, h1b = _merge(h, gates, ys, w_branch[l], w_out[l], ln1_g[l], ln1_b[l])
        h, hb = _moe(h1, h1b, moe_w_grp[l], moe_b_grp[l], moe_w_rt[l], moe_b_rt[l],
                     moe_w_gate[l], moe_w_up[l], moe_w_down[l], ln2_g[l], ln2_b[l])
    return h.reshape(B, TP, D)[:, FRONT:]
```

```python
import functools
import math

import numpy as np
import jax
import jax.numpy as jnp
from jax import lax
from jax.experimental import pallas as pl
from jax.experimental.pallas import tpu as pltpu

F32 = jnp.float32
BF16 = jnp.bfloat16

D_MODEL = 1024
HEAD_DIM = 64
N_HEADS = 4
MIX_W = 256
N_META = 16
CHUNK = 64
LANES = 128
ROW_TILE = 128
FRONT = ROW_TILE
FP = FRONT - N_META
NEG = -1e30
LN_EPS = 1e-5
DEPTH = 2
DN_ALPHA = (2 * DEPTH) ** 0.25

RWKV_GN_EPS = HEAD_DIM * 1e-5
GLA_DK = 32
GLA_TAU = 16.0
DSA_KV_RANK = 128
IDX_HEADS = 8
IDX_DIM = 32
TOPK_MAX = 256
N_BUCKETS = 32
MAX_DISTANCE = 128
CONV_W = 4
N_GROUPS = 4
EPG = 4
N_EXPERTS = 16
D_EXPERT = 256

INT_MIN = -(2 ** 31)
KEY_INF = 0x7F800000
VMEM_LIMIT = 56 * 1024 * 1024


def _cparams(*sem):
    return pltpu.CompilerParams(dimension_semantics=tuple(sem), vmem_limit_bytes=VMEM_LIMIT)


def _pick_tile(n, target):
    best = LANES
    t = LANES
    while t <= min(n, target):
        if n % t == 0:
            best = t
        t += LANES
    return best


def _bdot(a, b):
    return jnp.dot(a.astype(BF16), b.astype(BF16), preferred_element_type=F32)


def _bdot_nt(a, b):
    return lax.dot_general(a.astype(BF16), b.astype(BF16), (((1,), (1,)), ((), ())),
                           preferred_element_type=F32)


def _bdot_tn(a, b):
    return lax.dot_general(a.astype(BF16), b.astype(BF16), (((0,), (0,)), ((), ())),
                           preferred_element_type=F32)


def _split(a):
    hi = a.astype(BF16)
    lo = (a - hi.astype(F32)).astype(BF16)
    return hi, lo


_NN = (((1,), (0,)), ((), ()))
_NT = (((1,), (1,)), ((), ()))
_TN = (((0,), (0,)), ((), ()))


def _dot3(a, b, dims=_NN):
    ah, al = _split(a)
    bh, bl = _split(b)
    dg = lambda x, y: lax.dot_general(x, y, dims, preferred_element_type=F32)
    return dg(ah, bh) + (dg(ah, bl) + dg(al, bh))


def _dot_exact_lhs(a_bf16, b):
    bh, bl = _split(b)
    return (jnp.dot(a_bf16, bh, preferred_element_type=F32)
            + jnp.dot(a_bf16, bl, preferred_element_type=F32))


def _dot_exact_rhs(a, b_bf16):
    ah, al = _split(a)
    return (jnp.dot(ah, b_bf16, preferred_element_type=F32)
            + jnp.dot(al, b_bf16, preferred_element_type=F32))


def _sigmoid(x):
    return 1.0 / (1.0 + jnp.exp(-x))


def _log_sigmoid(x):
    return jnp.minimum(x, 0.0) - jnp.log(1.0 + jnp.exp(-jnp.abs(x)))


def _silu(x):
    return x * _sigmoid(x)


def _iota(shape, dim):
    return lax.broadcasted_iota(jnp.int32, shape, dim)


def _tri_incl(n):
    return (_iota((n, n), 1) <= _iota((n, n), 0))


def _head_ones():
    return ((_iota((MIX_W, MIX_W), 0) // HEAD_DIM) == (_iota((MIX_W, MIX_W), 1) // HEAD_DIM)).astype(BF16)


def _row_ids(rows):
    return pl.program_id(1) * ROW_TILE + _iota((rows, 1), 0)


def _embed_kernel(x_ref, meta_ref, g_ref, b_ref, h_ref, hb_ref):
    j = pl.program_id(1)
    src = jnp.where(j == 0, meta_ref[...], x_ref[0])
    mu = jnp.mean(src, -1, keepdims=True)
    xc = src - mu
    var = jnp.mean(xc * xc, -1, keepdims=True)
    y = xc * lax.rsqrt(var + LN_EPS) * g_ref[...] + b_ref[...]
    h_ref[0] = y
    hb_ref[0] = y.astype(BF16)


def _embed(x, meta, g, b):
    B, S, D = x.shape
    TP = S + FRONT
    meta_pad = jnp.concatenate([jnp.zeros((FP, D), F32), meta.astype(F32)], axis=0)
    return pl.pallas_call(
        _embed_kernel,
        grid=(B, TP // ROW_TILE),
        in_specs=[
            pl.BlockSpec((1, ROW_TILE, D), lambda b, j: (b, jnp.maximum(j - 1, 0), 0)),
            pl.BlockSpec((ROW_TILE, D), lambda b, j: (0, 0)),
            pl.BlockSpec((1, D), lambda b, j: (0, 0)),
            pl.BlockSpec((1, D), lambda b, j: (0, 0)),
        ],
        out_specs=[
            pl.BlockSpec((1, ROW_TILE, D), lambda b, j: (b, j, 0)),
            pl.BlockSpec((1, ROW_TILE, D), lambda b, j: (b, j, 0)),
        ],
        out_shape=[jax.ShapeDtypeStruct((B, TP, D), F32), jax.ShapeDtypeStruct((B, TP, D), BF16)],
        compiler_params=_cparams("parallel", "arbitrary"),
        name="embed_ln",
    )(x, meta_pad, g.reshape(1, D), b.reshape(1, D))


def _proj_kernel(h_ref, w_ref, o_ref, *, act):
    y = jnp.dot(h_ref[...], w_ref[...], preferred_element_type=F32)
    if act == "sigmoid":
        y = _sigmoid(y)
    o_ref[...] = y.astype(o_ref.dtype)


def _proj(hb, w, act=None, out_dtype=F32):
    N, D = hb.shape
    W = w.shape[1]
    tn = W if W <= 1152 else 1024
    tm = _pick_tile(N, 640)
    return pl.pallas_call(
        functools.partial(_proj_kernel, act=act),
        grid=(W // tn, N // tm),
        in_specs=[pl.BlockSpec((tm, D), lambda j, i: (i, 0)),
                  pl.BlockSpec((D, tn), lambda j, i: (0, j))],
        out_specs=pl.BlockSpec((tm, tn), lambda j, i: (i, j)),
        out_shape=jax.ShapeDtypeStruct((N, W), out_dtype),
        compiler_params=_cparams("arbitrary", "arbitrary"),
        name="in_proj",
    )(hb, w)


def _rwkv_kernel(p_ref, mu_ref, wup_ref, w0_ref, aup_ref, a0_ref, gup_ref, kk_ref, ka_ref, rk_ref,
                 gng_ref, gnb_ref, y_ref, carry_ref, s_ref):
    j = pl.program_id(0)
    nb = p_ref.shape[0]
    n_chunks = ROW_TILE // CHUNK

    @pl.when(j == 0)
    def _():
        carry_ref[...] = jnp.zeros_like(carry_ref)
        s_ref[...] = jnp.zeros_like(s_ref)

    valid = (j * ROW_TILE + _iota((ROW_TILE, 1), 0)) >= FP
    first_row = _iota((ROW_TILE, 1), 0) == 0
    ones_h = _head_ones()
    tri = _tri_incl(CHUNK)
    tri_b = tri.astype(BF16)
    strict = _iota((CHUNK, CHUNK), 1) < _iota((CHUNK, CHUNK), 0)
    eye = (_iota((CHUNK, CHUNK), 1) == _iota((CHUNK, CHUNK), 0)).astype(F32)
    heads = [slice(h * HEAD_DIM, (h + 1) * HEAD_DIM) for h in range(N_HEADS)]

    pro = []
    unit = {}
    for b in range(nb):
        p = jnp.where(valid, p_ref[b], 0.0)
        prev = jnp.where(first_row, carry_ref[b], pltpu.roll(p, 1, 0))
        carry_ref[b] = p[ROW_TILE - 1:ROW_TILE, :]
        ps = p + (prev - p) * mu_ref[...]
        r = ps[:, 0:256]
        k = ps[:, 256:512]
        v = ps[:, 512:768]
        lora_in = ps[:, 768:896]
        xg = ps[:, 896:1024]
        w_log = _log_sigmoid(w0_ref[...] + _bdot(jnp.tanh(lora_in), wup_ref[...])) - 0.5
        lw = jnp.where(valid, -jnp.exp(w_log), 0.0)
        alpha = _sigmoid(a0_ref[...] + _bdot(lora_in, aup_ref[...]))
        gate = _bdot(_sigmoid(xg), gup_ref[...])
        kk = k * kk_ref[...]
        kk = kk / jnp.maximum(jnp.sqrt(_dot_exact_rhs(kk * kk, ones_h)), 1e-12)
        k = k * (1.0 + (alpha - 1.0) * ka_ref[...])
        kka = kk * alpha
        pro.append((r, k, v, gate))
        for c in range(n_chunks):
            sl = slice(c * CHUNK, (c + 1) * CHUNK)
            lw_c = lw[sl]
            cum = _dot_exact_lhs(tri_b, lw_c)
            cum_last = cum[CHUNK - 1:CHUNK, :]
            p_inv = jnp.exp(-cum)
            p_tail = jnp.exp(cum_last - cum)
            unit[b, c] = dict(a=-kk[sl] * jnp.exp(cum - lw_c), b=kka[sl] * p_inv, k=k[sl] * p_inv,
                              r=r[sl] * jnp.exp(cum), kb=k[sl] * p_tail, bb=kka[sl] * p_tail,
                              pl=jnp.exp(cum_last), v=v[sl])

    keys = [(b, c, h) for b in range(nb) for c in range(n_chunks) for h in range(N_HEADS)]
    part = lambda name, key: unit[key[0], key[1]][name][:, heads[key[2]]]
    a_ab = {q: jnp.where(strict, _dot3(part("a", q), part("b", q), _NT), 0.0) for q in keys}
    a_ak = {q: jnp.where(strict, _bdot_nt(part("a", q), part("k", q)), 0.0) for q in keys}
    a_rb = {q: jnp.where(tri, _bdot_nt(part("r", q), part("b", q)), 0.0) for q in keys}
    a_rk = {q: jnp.where(tri, _bdot_nt(part("r", q), part("k", q)), 0.0) for q in keys}
    inv = {q: eye + a_ab[q] for q in keys}
    pw = a_ab
    for _ in range(5):
        pw = {q: _bdot(pw[q], pw[q]) for q in keys}
        inv = {q: inv[q] + _bdot(inv[q], pw[q]) for q in keys}
    ak_v = {q: _bdot(a_ak[q], part("v", q)) for q in keys}
    rk_v = {q: _bdot(a_rk[q], part("v", q)) for q in keys}
    kb_v = {q: _bdot_tn(part("v", q), part("kb", q)) for q in keys}

    bh = [(b, h) for b in range(nb) for h in range(N_HEADS)]
    state = {q: s_ref[q[0], q[1]] for q in bh}
    y_parts = {}
    for c in range(n_chunks):
        full = lambda q: (q[0], c, q[1])
        a_s = {q: _bdot_nt(part("a", full(q)), state[q]) for q in bh}
        r_s = {q: _bdot_nt(part("r", full(q)), state[q]) for q in bh}
        u = {q: _bdot(inv[full(q)], a_s[q] + ak_v[full(q)]) for q in bh}
        for q in bh:
            y_parts[full(q)] = r_s[q] + rk_v[full(q)] + _bdot(a_rb[full(q)], u[q])
        state = {q: (state[q] * part("pl", full(q)) + kb_v[full(q)] + _bdot_tn(u[q], part("bb", full(q))))
                 for q in bh}
    for q in bh:
        s_ref[q[0], q[1]] = state[q]

    for b in range(nb):
        r, k, v, gate = pro[b]
        y = jnp.concatenate([jnp.concatenate([y_parts[b, c, h] for h in range(N_HEADS)], axis=1)
                             for c in range(n_chunks)], axis=0)
        mean = _dot_exact_rhs(y, ones_h) * (1.0 / HEAD_DIM)
        yc = y - mean
        var = _dot_exact_rhs(yc * yc, ones_h) * (1.0 / HEAD_DIM)
        yn = yc * lax.rsqrt(var + RWKV_GN_EPS) * gng_ref[...] + gnb_ref[...]
        bonus = _dot_exact_rhs(r * k * rk_ref[...], ones_h) * v
        y_ref[b] = ((yn + bonus) * gate).astype(y_ref.dtype)


def _rwkv(p_a, B, TP, mu, w_up, w0, a_up, a0, g_up, k_k, k_a, r_k, gn_g, gn_b):
    W = MIX_W
    z64 = jnp.zeros((64, W), F32)
    wup_pad = jnp.concatenate([w_up, z64], axis=0).astype(BF16)
    aup_pad = jnp.concatenate([z64, a_up], axis=0).astype(BF16)
    row = lambda a: a.reshape(1, -1).astype(F32)
    full = lambda shape: pl.BlockSpec(shape, lambda j: (0,) * len(shape))
    return pl.pallas_call(
        _rwkv_kernel,
        grid=(TP // ROW_TILE,),
        in_specs=[pl.BlockSpec((B, ROW_TILE, 1024), lambda j: (0, j, 0)),
                  full((1, 1024)), full((128, W)), full((1, W)), full((128, W)), full((1, W)),
                  full((128, W)), full((1, W)), full((1, W)), full((1, W)), full((1, W)), full((1, W))],
        out_specs=pl.BlockSpec((B, ROW_TILE, W), lambda j: (0, j, 0)),
        out_shape=jax.ShapeDtypeStruct((B, TP, W), BF16),
        scratch_shapes=[pltpu.VMEM((B, 1, 1024), F32), pltpu.VMEM((B, N_HEADS, HEAD_DIM, HEAD_DIM), F32)],
        compiler_params=_cparams("arbitrary"),
        name="rwkv7",
    )(p_a.reshape(B, TP, 1024), row(mu), wup_pad, row(w0), aup_pad, row(a0), g_up.astype(BF16),
      row(k_k), row(k_a), row(r_k), row(gn_g), row(gn_b))


def _gla_kernel(p_ref, aup_ref, ab_ref, ng_ref, y_ref, s_ref):
    j = pl.program_id(0)
    nb = p_ref.shape[0]
    n_chunks = ROW_TILE // CHUNK

    @pl.when(j == 0)
    def _():
        s_ref[...] = jnp.zeros_like(s_ref)

    valid = (j * ROW_TILE + _iota((ROW_TILE, 1), 0)) >= FP
    tri = _tri_incl(CHUNK)
    tri_b = tri.astype(BF16)

    og_all = []
    pre = {}
    for b in range(nb):
        p = jnp.where(valid, p_ref[b], 0.0)
        la = _log_sigmoid(_bdot(p[:, 768:896], aup_ref[...]) + ab_ref[...]) * (1.0 / GLA_TAU)
        la = jnp.where(valid, la, 0.0)
        og_all.append(p[:, 512:768])
        for c in range(n_chunks):
            sl = slice(c * CHUNK, (c + 1) * CHUNK)
            pre[b, c] = dict(q=p[sl, 0:128] * (GLA_DK ** -0.5), k=p[sl, 128:256], v=p[sl, 256:512], la=la[sl])
    bc = [(b, c) for b in range(nb) for c in range(n_chunks)]
    keys = [(b, c, h) for (b, c) in bc for h in range(N_HEADS)]
    ks = [slice(h * GLA_DK, (h + 1) * GLA_DK) for h in range(N_HEADS)]
    vs = [slice(h * HEAD_DIM, (h + 1) * HEAD_DIM) for h in range(N_HEADS)]
    b_cum = {u: _dot_exact_lhs(tri_b, pre[u]["la"]) for u in bc}
    b_last = {u: b_cum[u][CHUNK - 1:CHUNK, :] for u in bc}
    q_g = {u: pre[u]["q"] * jnp.exp(b_cum[u]) for u in bc}
    k_g = {u: pre[u]["k"] * jnp.exp(-b_cum[u]) for u in bc}
    k_l = {u: pre[u]["k"] * jnp.exp(b_last[u] - b_cum[u]) for u in bc}
    dec = {u: jnp.exp(b_last[u]) for u in bc}
    att = {u: jnp.where(tri, _bdot_nt(q_g[u[0], u[1]][:, ks[u[2]]], k_g[u[0], u[1]][:, ks[u[2]]]), 0.0)
           for u in keys}
    att_v = {u: _bdot(att[u], pre[u[0], u[1]]["v"][:, vs[u[2]]]) for u in keys}
    kl_v = {u: _bdot_tn(pre[u[0], u[1]]["v"][:, vs[u[2]]], k_l[u[0], u[1]][:, ks[u[2]]]) for u in keys}

    bh = [(b, h) for b in range(nb) for h in range(N_HEADS)]
    state = {q: s_ref[q[0], q[1]] for q in bh}
    o_parts = {}
    for c in range(n_chunks):
        for q in bh:
            o_parts[q[0], c, q[1]] = att_v[q[0], c, q[1]] + _bdot_nt(q_g[q[0], c][:, ks[q[1]]], state[q])
        state = {q: state[q] * dec[q[0], c][:, ks[q[1]]] + kl_v[q[0], c, q[1]] for q in bh}
    for q in bh:
        s_ref[q[0], q[1]] = state[q]

    ones_h = _head_ones()
    for b in range(nb):
        o = jnp.concatenate([jnp.concatenate([o_parts[b, c, h] for h in range(N_HEADS)], axis=1)
                             for c in range(n_chunks)], axis=0)
        ms = _dot_exact_rhs(o * o, ones_h) * (1.0 / HEAD_DIM)
        y = o * lax.rsqrt(ms + 1e-6) * ng_ref[...] * _silu(og_all[b])
        y_ref[b] = y.astype(y_ref.dtype)


def _gla(p_b, B, TP, a_up, a_b, norm_g):
    aup_pad = jnp.zeros((128, 128), F32).at[:a_up.shape[0]].set(a_up).astype(BF16)
    full = lambda shape: pl.BlockSpec(shape, lambda j: (0,) * len(shape))
    return pl.pallas_call(
        _gla_kernel,
        grid=(TP // ROW_TILE,),
        in_specs=[pl.BlockSpec((B, ROW_TILE, 896), lambda j: (0, j, 0)),
                  full((128, 128)), full((1, 128)), full((1, MIX_W))],
        out_specs=pl.BlockSpec((B, ROW_TILE, MIX_W), lambda j: (0, j, 0)),
        out_shape=jax.ShapeDtypeStruct((B, TP, MIX_W), BF16),
        scratch_shapes=[pltpu.VMEM((B, N_HEADS, HEAD_DIM, GLA_DK), F32)],
        compiler_params=_cparams("arbitrary"),
        name="gla",
    )(p_b.reshape(B, TP, 896), aup_pad, a_b.reshape(1, 128).astype(F32),
      jnp.tile(norm_g.astype(F32), N_HEADS).reshape(1, MIX_W))


def _mlstm_kernel(p_ref, cw_ref, cb_ref, ib_ref, fb_ref, ng_ref, y_ref, carry_ref, c_ref, n_ref, m_ref):
    j = pl.program_id(0)
    nb = p_ref.shape[0]
    n_chunks = ROW_TILE // CHUNK

    @pl.when(j == 0)
    def _():
        carry_ref[...] = jnp.zeros_like(carry_ref)
        c_ref[...] = jnp.zeros_like(c_ref)
        n_ref[...] = jnp.zeros_like(n_ref)
        m_ref[...] = jnp.zeros_like(m_ref)

    valid = (j * ROW_TILE + _iota((ROW_TILE, 1), 0)) >= FP
    tri = _tri_incl(CHUNK)
    tri_b = tri.astype(BF16)
    ones_h = _head_ones()

    og_all = []
    pre = {}
    for b in range(nb):
        p = jnp.where(valid, p_ref[b], 0.0)
        a = p[:, 0:512]
        ext = jnp.concatenate([carry_ref[b], a], axis=0)
        carry_ref[b] = a[ROW_TILE - 8:ROW_TILE, :]
        conv = cb_ref[...] + a * cw_ref[CONV_W - 1:CONV_W, :]
        for s in range(1, CONV_W):
            conv = conv + pltpu.roll(ext, s, 0)[8:8 + ROW_TILE, :] * cw_ref[CONV_W - 1 - s:CONV_W - s, :]
        qk = _silu(conv)
        q = jnp.where(valid, qk[:, 0:MIX_W], 0.0)
        k = jnp.where(valid, qk[:, MIX_W:2 * MIX_W], 0.0) * (HEAD_DIM ** -0.5)
        v = p[:, 512:768]
        og_all.append(p[:, 768:1024])
        gates = p[:, 1024:1152]
        li_all = jnp.where(valid, gates + ib_ref[...], NEG)
        lf_all = jnp.where(valid, _log_sigmoid(gates + fb_ref[...]), 0.0)
        for c in range(n_chunks):
            sl = slice(c * CHUNK, (c + 1) * CHUNK)
            pre[b, c] = dict(q=q[sl], k=k[sl], v=v[sl], li=li_all[sl], lf=lf_all[sl])

    bc = [(b, c) for b in range(nb) for c in range(n_chunks)]
    keys = [(b, c, h) for (b, c) in bc for h in range(N_HEADS)]
    heads = [slice(h * HEAD_DIM, (h + 1) * HEAD_DIM) for h in range(N_HEADS)]
    part = lambda name, u: pre[u[0], u[1]][name][:, heads[u[2]]]
    b_cum = {u: _dot_exact_lhs(tri_b, pre[u]["lf"]) for u in bc}
    b_t = {u: b_cum[u].T for u in bc}
    li_t = {u: pre[u]["li"].T for u in bc}
    b_col = {u: b_cum[u[0], u[1]][:, N_HEADS + u[2]:N_HEADS + u[2] + 1] for u in keys}
    b_last = {u: b_col[u][CHUNK - 1:CHUNK, :] for u in keys}
    d_log = {u: jnp.where(tri, b_col[u] - b_t[u[0], u[1]][N_HEADS + u[2]:N_HEADS + u[2] + 1, :]
                          + li_t[u[0], u[1]][u[2]:u[2] + 1, :], -jnp.inf) for u in keys}
    dmax = {u: jnp.max(d_log[u], axis=1, keepdims=True) for u in keys}
    qk = {u: _bdot_nt(part("q", u), part("k", u)) for u in keys}
    s0 = {u: jnp.exp(d_log[u] - dmax[u]) * qk[u] for u in keys}
    sv = {u: _bdot(s0[u], part("v", u)) for u in keys}
    ssum = {u: jnp.sum(s0[u], axis=1, keepdims=True) for u in keys}
    g_loc = {u: b_last[u] - b_col[u] + pre[u[0], u[1]]["li"][:, u[2]:u[2] + 1] for u in keys}
    m_loc = {u: jnp.max(g_loc[u], axis=0, keepdims=True) for u in keys}
    kw = {u: part("k", u) * jnp.exp(g_loc[u] - m_loc[u]) for u in keys}
    kwv = {u: _bdot_tn(kw[u], part("v", u)) for u in keys}
    kwsum = {u: jnp.sum(kw[u], axis=0, keepdims=True) for u in keys}

    bh = [(b, h) for b in range(nb) for h in range(N_HEADS)]
    c_st = {q: c_ref[q[0], q[1]] for q in bh}
    n_st = {q: n_ref[q[0], q[1]] for q in bh}
    m_st = {q: m_ref[q[0], q[1]] for q in bh}
    h_parts = {}
    for c in range(n_chunks):
        full = lambda q: (q[0], c, q[1])
        qc = {q: _bdot(part("q", full(q)), c_st[q]) for q in bh}
        qn = {q: jnp.sum(part("q", full(q)) * n_st[q], axis=1, keepdims=True) for q in bh}
        inter = {q: b_col[full(q)] + m_st[q] for q in bh}
        m_t = {q: jnp.maximum(inter[q], dmax[full(q)]) for q in bh}
        e_loc = {q: jnp.exp(dmax[full(q)] - m_t[q]) for q in bh}
        w_int = {q: jnp.exp(inter[q] - m_t[q]) for q in bh}
        for q in bh:
            num = e_loc[q] * sv[full(q)] + w_int[q] * qc[q]
            den = e_loc[q] * ssum[full(q)] + w_int[q] * qn[q]
            h_parts[full(q)] = num / jnp.maximum(jnp.abs(den), jnp.exp(-m_t[q]))
        m_new = {q: jnp.maximum(b_last[full(q)] + m_st[q], m_loc[full(q)]) for q in bh}
        s_old = {q: jnp.exp(b_last[full(q)] + m_st[q] - m_new[q]) for q in bh}
        s_new = {q: jnp.exp(m_loc[full(q)] - m_new[q]) for q in bh}
        c_st = {q: s_old[q] * c_st[q] + s_new[q] * kwv[full(q)] for q in bh}
        n_st = {q: s_old[q] * n_st[q] + s_new[q] * kwsum[full(q)] for q in bh}
        m_st = m_new
    for q in bh:
        c_ref[q[0], q[1]], n_ref[q[0], q[1]], m_ref[q[0], q[1]] = c_st[q], n_st[q], m_st[q]

    for b in range(nb):
        hh = jnp.concatenate([jnp.concatenate([h_parts[b, c, h] for h in range(N_HEADS)], axis=1)
                              for c in range(n_chunks)], axis=0) * _sigmoid(og_all[b])
        mean = _dot_exact_rhs(hh, ones_h) * (1.0 / HEAD_DIM)
        hc = hh - mean
        var = _dot_exact_rhs(hc * hc, ones_h) * (1.0 / HEAD_DIM)
        y_ref[b] = (hc * lax.rsqrt(var + 1e-5) * ng_ref[...]).astype(y_ref.dtype)


def _mlstm(p_d, B, TP, conv_w, conv_b, i_b, f_b, norm_g):
    ib = jnp.zeros((1, LANES), F32).at[0, 0:N_HEADS].set(i_b)
    fb = jnp.zeros((1, LANES), F32).at[0, N_HEADS:2 * N_HEADS].set(f_b)
    full = lambda shape: pl.BlockSpec(shape, lambda j: (0,) * len(shape))
    return pl.pallas_call(
        _mlstm_kernel,
        grid=(TP // ROW_TILE,),
        in_specs=[pl.BlockSpec((B, ROW_TILE, 1152), lambda j: (0, j, 0)),
                  full((CONV_W, 512)), full((1, 512)), full((1, LANES)), full((1, LANES)), full((1, MIX_W))],
        out_specs=pl.BlockSpec((B, ROW_TILE, MIX_W), lambda j: (0, j, 0)),
        out_shape=jax.ShapeDtypeStruct((B, TP, MIX_W), BF16),
        scratch_shapes=[pltpu.VMEM((B, 8, 512), F32),
                        pltpu.VMEM((B, N_HEADS, HEAD_DIM, HEAD_DIM), F32),
                        pltpu.VMEM((B, N_HEADS, 1, HEAD_DIM), F32),
                        pltpu.VMEM((B, N_HEADS, 1, 1), F32)],
        compiler_params=_cparams("arbitrary"),
        name="mlstm",
    )(p_d.reshape(B, TP, 1152), conv_w.astype(F32), conv_b.reshape(1, 512).astype(F32), ib, fb,
      norm_g.reshape(1, MIX_W).astype(F32))


V_ROWS = 80
WT_ROWS = 528


def _dsa_prep_kernel(h_ref, wt_ref, wn_ref, kvg_ref, wuk_ref, wuvt_ref,
                     qt_ref, qit_ref, wit_ref, k_ref, ki_ref, vt_ref):
    hb = h_ref[...]
    tm = hb.shape[0]
    pt = lax.dot_general(wt_ref[...], hb, _NT, preferred_element_type=F32)
    pn = jnp.dot(hb, wn_ref[...], preferred_element_type=F32)
    ckv = pn[:, 0:DSA_KV_RANK]
    c = ckv * lax.rsqrt(jnp.mean(ckv * ckv, -1, keepdims=True) + 1e-6) * kvg_ref[...]
    cb = c.astype(BF16)
    k_ref[...] = jnp.dot(cb, wuk_ref[...], preferred_element_type=F32).astype(BF16)
    ki_ref[...] = pn[:, DSA_KV_RANK:DSA_KV_RANK + IDX_DIM].astype(BF16)
    vt = lax.dot_general(wuvt_ref[...], cb, _NT, preferred_element_type=F32)
    vt = jnp.where(_iota((V_ROWS, tm), 0) == HEAD_DIM, 1.0, vt)
    for t in range(tm // LANES):
        cs = slice(t * LANES, (t + 1) * LANES)
        for h in range(N_HEADS):
            qt_ref[t, :, h * LANES:(h + 1) * LANES] = (
                pt[h * HEAD_DIM:(h + 1) * HEAD_DIM, cs] * (HEAD_DIM ** -0.5)).astype(BF16)
        for h in range(IDX_HEADS):
            qit_ref[t, :, h * LANES:(h + 1) * LANES] = pt[MIX_W + h * IDX_DIM:MIX_W + (h + 1) * IDX_DIM, cs].astype(BF16)
        wit_ref[t] = pt[2 * MIX_W:2 * MIX_W + IDX_HEADS, cs] * ((IDX_HEADS * IDX_DIM) ** -0.5)
        vt_ref[t] = vt[:, cs].astype(BF16)


def _dsa_prep(hb, w_t, w_n, kv_norm_g, w_uk, w_uv):
    N, D = hb.shape
    tm = _pick_tile(N, 640)
    nt = tm // LANES
    full = lambda shape: pl.BlockSpec(shape, lambda i: (0,) * len(shape))
    wuvt = jnp.pad(w_uv.T, ((0, V_ROWS - HEAD_DIM), (0, 0))).astype(BF16)
    return pl.pallas_call(
        _dsa_prep_kernel,
        grid=(N // tm,),
        in_specs=[pl.BlockSpec((tm, D), lambda i: (i, 0)),
                  full((WT_ROWS, D)), full((D, 256)), full((1, DSA_KV_RANK)),
                  full((DSA_KV_RANK, HEAD_DIM)), full((V_ROWS, DSA_KV_RANK))],
        out_specs=[pl.BlockSpec((nt, HEAD_DIM, N_HEADS * LANES), lambda i: (i, 0, 0)),
                   pl.BlockSpec((nt, IDX_DIM, IDX_HEADS * LANES), lambda i: (i, 0, 0)),
                   pl.BlockSpec((nt, IDX_HEADS, LANES), lambda i: (i, 0, 0)),
                   pl.BlockSpec((tm, HEAD_DIM), lambda i: (i, 0)),
                   pl.BlockSpec((tm, IDX_DIM), lambda i: (i, 0)),
                   pl.BlockSpec((nt, V_ROWS, LANES), lambda i: (i, 0, 0))],
        out_shape=[jax.ShapeDtypeStruct((N // LANES, HEAD_DIM, N_HEADS * LANES), BF16),
                   jax.ShapeDtypeStruct((N // LANES, IDX_DIM, IDX_HEADS * LANES), BF16),
                   jax.ShapeDtypeStruct((N // LANES, IDX_HEADS, LANES), F32),
                   jax.ShapeDtypeStruct((N, HEAD_DIM), BF16),
                   jax.ShapeDtypeStruct((N, IDX_DIM), BF16),
                   jax.ShapeDtypeStruct((N // LANES, V_ROWS, LANES), BF16)],
        compiler_params=_cparams("arbitrary"),
        name="dsa_prep",
    )(hb, w_t, w_n, kv_norm_g.reshape(1, DSA_KV_RANK).astype(F32), w_uk.astype(BF16), wuvt)


def _loop_groups(lo, hi, fn):
    n = jnp.maximum(hi - lo, 0)
    n4 = lax.shift_right_logical(n, 2)

    def body(j, c):
        fn([lo + 4 * j + u for u in range(4)])
        return c

    lax.fori_loop(0, n4, body, 0)
    rest = lo + 4 * n4

    @pl.when((n & 2) == 2)
    def _():
        fn([rest, rest + 1])

    @pl.when((n & 1) == 1)
    def _():
        fn([hi - 1])


def _dsa_kernel(qt_ref, qit_ref, wit_ref, k_ref, ki_ref, vt_ref, bias_ref, y_ref,
                sk_ref, m_ref, acc_ref, lg_ref, mg_ref, *, topk):
    i = pl.program_id(1)
    nk = i + 1
    QT = ROW_TILE
    HQ = N_HEADS * QT
    t_lane = i * QT + _iota((LANES, QT), 1)
    key_pos = lambda kt: kt * LANES + _iota((LANES, QT), 0)
    rows = lambda kt: pl.ds(pl.multiple_of(kt * LANES, LANES), LANES)
    per_head = lambda fn: jnp.concatenate([fn(slice(h * QT, (h + 1) * QT)) for h in range(N_HEADS)], axis=1)

    qit = qit_ref[0]
    wit = wit_ref[0]

    def score_tile(kt, edge):
        rel = jnp.dot(ki_ref[rows(kt), :], qit, preferred_element_type=F32)
        score = jnp.maximum(rel[:, 0:QT], 0.0) * wit[0:1, :]
        for h in range(1, IDX_HEADS):
            score = score + jnp.maximum(rel[:, h * QT:(h + 1) * QT], 0.0) * wit[h:h + 1, :]
        score = jnp.where(score == 0.0, 0.0, score)
        bits = lax.bitcast_convert_type(score, jnp.int32)
        key = jnp.where(bits < 0, bits ^ jnp.int32(0x7FFFFFFF), bits)
        if edge:
            s_pos = key_pos(kt)
            key = jnp.where(s_pos < FP + N_META, jnp.int32(KEY_INF), key)
            key = jnp.where((s_pos >= FP) & (s_pos <= t_lane), key, jnp.int32(INT_MIN))
        sk_ref[kt] = key

    score_tile(0, True)
    _loop_groups(1, i, lambda kts: [score_tile(kt, False) for kt in kts])

    @pl.when(i > 0)
    def _():
        score_tile(i, True)

    def count(pred_fn):
        def body(kt, acc):
            return acc + jnp.where(pred_fn(sk_ref[kt], kt), 1, 0)

        def body4(j, acc):
            for u in range(4):
                acc = body(4 * j + u, acc)
            return acc

        n4 = lax.shift_right_logical(nk, 2)
        acc = lax.fori_loop(0, n4, body4, jnp.zeros((LANES, QT), jnp.int32))
        acc = lax.fori_loop(4 * n4, nk, body, acc)
        return jnp.sum(acc, axis=0, keepdims=True)

    def bit_body(it, tau):
        cand = tau + jnp.left_shift(jnp.int32(1), 31 - it)
        cnt = count(lambda sk, kt: sk >= cand)
        return jnp.where(cnt >= topk, cand, tau)

    tau = lax.fori_loop(0, 32, bit_body, jnp.full((1, QT), INT_MIN, jnp.int32))
    tau = jnp.maximum(tau, jnp.int32(INT_MIN + 1))
    n_gt = count(lambda sk, kt: sk > tau)
    n_ge = count(lambda sk, kt: sk >= tau)
    need = topk - n_gt

    @pl.when(jnp.max(n_ge - topk) > 0)
    def _():
        n_bits = max(1, int(math.ceil(math.log2(sk_ref.shape[0] * LANES + 1))))

        def pos_body(it, x):
            cand = x + jnp.left_shift(jnp.int32(1), n_bits - 1 - it)
            cnt = count(lambda sk, kt: (sk == tau) & (key_pos(kt) < cand))
            return jnp.where(cnt < need, cand, x)

        x = lax.fori_loop(0, n_bits, pos_body, jnp.zeros((1, QT), jnp.int32))
        jmax = jnp.where(n_ge > topk, x, jnp.int32(2 ** 30))

        def drop_body(kt, c):
            sk = sk_ref[kt]
            sk_ref[kt] = jnp.where((sk == tau) & (key_pos(kt) > jmax), jnp.int32(INT_MIN), sk)
            return c

        lax.fori_loop(0, nk, drop_body, 0)

    qt = qt_ref[0]
    m_ref[...] = jnp.full((1, HQ), NEG, F32)
    acc_ref[...] = jnp.zeros((V_ROWS, HQ), F32)

    def attend(kts, near):
        lgs = []
        for kt in kts:
            lg = jnp.dot(k_ref[rows(kt), :], qt, preferred_element_type=F32)
            if near is not None:
                lg = lg + bias_ref[near]
            sel = sk_ref[kt] >= tau
            lgs.append(per_head(lambda hs: jnp.where(sel, lg[:, hs], NEG)))
        tmax = lgs[0]
        for lg in lgs[1:]:
            tmax = jnp.maximum(tmax, lg)
        m_old = m_ref[...]
        m_new = jnp.maximum(m_old, jnp.max(tmax, axis=0, keepdims=True))
        pv = None
        for kt, lg in zip(kts, lgs):
            t = jnp.dot(vt_ref[kt], jnp.exp(lg - m_new).astype(BF16), preferred_element_type=F32)
            pv = t if pv is None else pv + t
        acc_ref[...] = acc_ref[...] * jnp.exp(m_old - m_new) + pv
        m_ref[...] = m_new

    attend([i], 0)

    @pl.when(i > 0)
    def _():
        attend([i - 1], 1)

    n_far = jnp.maximum(i - 1, 0)
    n8 = lax.shift_right_logical(n_far, 3)
    GW = lg_ref.shape[1]

    def park(g, slot):
        tmax = None
        for u in range(GW):
            kt = jnp.minimum(GW * g + u, i)
            lg = jnp.dot(k_ref[rows(kt), :], qt, preferred_element_type=F32)
            sel = sk_ref[kt] >= tau
            lgm = per_head(lambda hs: jnp.where(sel, lg[:, hs], NEG))
            lg_ref[slot, u] = lgm
            tmax = lgm if tmax is None else jnp.maximum(tmax, lgm)
        mg_ref[slot] = jnp.max(tmax, axis=0, keepdims=True)

    def weights(slot):
        m_old = m_ref[...]
        m_new = jnp.maximum(m_old, mg_ref[slot])
        m_ref[...] = m_new
        return jnp.exp(m_old - m_new), [jnp.exp(lg_ref[slot, u] - m_new).astype(BF16) for u in range(GW)]

    def fold(g, corr, prs):
        pv = None
        for u in range(GW):
            t = jnp.dot(vt_ref[GW * g + u], prs[u], preferred_element_type=F32)
            pv = t if pv is None else pv + t
        acc_ref[...] = acc_ref[...] * corr + pv

    @pl.when(n8 > 0)
    def _():
        park(0, 0)

    def pipe_body(jj, c):
        corr, prs = weights(0)
        park(2 * jj + 1, 1)
        fold(2 * jj, corr, prs)
        corr, prs = weights(1)
        park(2 * jj + 2, 0)
        fold(2 * jj + 1, corr, prs)
        return c

    lax.fori_loop(0, n8, pipe_body, 0)
    _loop_groups(8 * n8, i - 1, lambda kts: attend(kts, None))
    acc = acc_ref[...]
    out = acc[0:HEAD_DIM, :] / jnp.maximum(acc[HEAD_DIM:HEAD_DIM + 1, :], 1e-30)
    y_ref[...] = per_head(lambda hs: out[:, hs].T).astype(y_ref.dtype)


def _t5_bucket(dist):
    max_exact = N_BUCKETS // 2
    n = jnp.maximum(dist, 0)
    large = max_exact + (jnp.log(jnp.maximum(n, 1).astype(F32) / max_exact)
                         / math.log(MAX_DISTANCE / max_exact) * (N_BUCKETS - max_exact)).astype(jnp.int32)
    return jnp.where(n < max_exact, n, jnp.minimum(large, N_BUCKETS - 1))


def _bias_tables(rel_bias):
    per_dist = rel_bias[_t5_bucket(jnp.arange(2 * ROW_TILE, dtype=jnp.int32))]
    q_minus_s = np.arange(ROW_TILE)[None, :] - np.arange(ROW_TILE)[:, None]
    far = per_dist[2 * ROW_TILE - 1]
    tabs = [per_dist[np.clip(r * ROW_TILE + q_minus_s, 0, 2 * ROW_TILE - 1)] - far for r in (0, 1)]
    return jnp.stack(tabs).transpose(0, 1, 3, 2).reshape(2, ROW_TILE, N_HEADS * ROW_TILE).astype(F32)


def _dsa(qt, qit, wit, k, ki, vt, bias_tab, B, TP, topk):
    nq = TP // ROW_TILE
    return pl.pallas_call(
        functools.partial(_dsa_kernel, topk=topk),
        grid=(B, nq),
        in_specs=[pl.BlockSpec((1, HEAD_DIM, N_HEADS * LANES), lambda b, i: (b * nq + i, 0, 0)),
                  pl.BlockSpec((1, IDX_DIM, IDX_HEADS * LANES), lambda b, i: (b * nq + i, 0, 0)),
                  pl.BlockSpec((1, IDX_HEADS, LANES), lambda b, i: (b * nq + i, 0, 0)),
                  pl.BlockSpec((TP, HEAD_DIM), lambda b, i: (b, 0)),
                  pl.BlockSpec((TP, IDX_DIM), lambda b, i: (b, 0)),
                  pl.BlockSpec((nq, V_ROWS, LANES), lambda b, i: (b, 0, 0)),
                  pl.BlockSpec((2, ROW_TILE, N_HEADS * ROW_TILE), lambda b, i: (0, 0, 0))],
        out_specs=pl.BlockSpec((ROW_TILE, MIX_W), lambda b, i: (b * nq + i, 0)),
        out_shape=jax.ShapeDtypeStruct((B * TP, MIX_W), BF16),
        scratch_shapes=[pltpu.VMEM((nq, LANES, ROW_TILE), jnp.int32),
                        pltpu.VMEM((1, N_HEADS * ROW_TILE), F32),
                        pltpu.VMEM((V_ROWS, N_HEADS * ROW_TILE), F32),
                        pltpu.VMEM((2, 4, LANES, N_HEADS * ROW_TILE), F32),
                        pltpu.VMEM((2, 1, N_HEADS * ROW_TILE), F32)],
        compiler_params=_cparams("parallel", "arbitrary"),
        name="dsa_attend",
    )(qt, qit, wit, k, ki, vt, bias_tab)


def _layer_norm_rows(z, g, b):
    mu = jnp.mean(z, -1, keepdims=True)
    zc = z - mu
    var = jnp.mean(zc * zc, -1, keepdims=True)
    return zc * lax.rsqrt(var + LN_EPS) * g + b


def _merge_kernel(h_ref, g_ref, ya_ref, yb_ref, yc_ref, yd_ref, wb_ref, wo_ref, lg_ref, lb_ref,
                  h1_ref, h1b_ref):
    merged = None
    for i, y_ref in enumerate((ya_ref, yb_ref, yc_ref, yd_ref)):
        t = g_ref[:, i * D_MODEL:(i + 1) * D_MODEL] * jnp.dot(y_ref[...], wb_ref[i], preferred_element_type=F32)
        merged = t if merged is None else merged + t
    z = DN_ALPHA * h_ref[...] + jnp.dot(merged.astype(BF16), wo_ref[...], preferred_element_type=F32)
    y = _layer_norm_rows(z, lg_ref[...], lb_ref[...])
    h1_ref[...] = y
    h1b_ref[...] = y.astype(BF16)


def _merge(h, gates, ys, w_branch, w_out, ln_g, ln_b):
    N, D = h.shape
    tm = _pick_tile(N, 512)
    full = lambda shape: pl.BlockSpec(shape, lambda i: (0,) * len(shape))
    tok = lambda w: pl.BlockSpec((tm, w), lambda i: (i, 0))
    return pl.pallas_call(
        _merge_kernel,
        grid=(N // tm,),
        in_specs=[tok(D), tok(4 * D), tok(MIX_W), tok(MIX_W), tok(MIX_W), tok(MIX_W),
                  full((4, MIX_W, D)), full((D, D)), full((1, D)), full((1, D))],
        out_specs=[tok(D), tok(D)],
        out_shape=[jax.ShapeDtypeStruct((N, D), F32), jax.ShapeDtypeStruct((N, D), BF16)],
        compiler_params=_cparams("arbitrary"),
        name="merge_out_ln",
    )(h, gates, *ys, w_branch.astype(BF16), w_out.astype(BF16),
      ln_g.reshape(1, D).astype(F32), ln_b.reshape(1, D).astype(F32))


def _moe_kernel(h_ref, hb_ref, wr_ref, br_ref, wg_ref, wu_ref, wd_ref, lg_ref, lb_ref, o_ref, ob_ref,
                gate_ref, acc_ref):
    e = pl.program_id(1)
    xb = hb_ref[...]
    tm = xb.shape[0]
    lane = _iota((tm, LANES), 1)

    @pl.when(e == 0)
    def _():
        logit = jnp.dot(xb, wr_ref[...], preferred_element_type=F32) + br_ref[...]
        big = jnp.int32(LANES)
        gl = jnp.where(lane < N_GROUPS, logit, -jnp.inf)
        gmax = jnp.max(gl, axis=1, keepdims=True)
        g_sel = jnp.min(jnp.where(gl == gmax, lane, big), axis=1, keepdims=True)
        p_grp = 1.0 / jnp.sum(jnp.exp(gl - gmax), axis=1, keepdims=True)
        lo = N_GROUPS + g_sel * EPG
        el = jnp.where((lane >= lo) & (lane < lo + EPG), logit, -jnp.inf)
        v1 = jnp.max(el, axis=1, keepdims=True)
        i1 = jnp.min(jnp.where(el == v1, lane, big), axis=1, keepdims=True)
        el2 = jnp.where(lane == i1, -jnp.inf, el)
        v2 = jnp.max(el2, axis=1, keepdims=True)
        i2 = jnp.min(jnp.where(el2 == v2, lane, big), axis=1, keepdims=True)
        e2 = jnp.exp(v2 - v1)
        w1 = p_grp / (1.0 + e2)
        w2 = p_grp * e2 / (1.0 + e2)
        gate_ref[...] = jnp.where(lane == i1, w1, 0.0) + jnp.where(lane == i2, w2, 0.0)
        acc_ref[...] = jnp.zeros_like(acc_ref)

    g_e = jnp.sum(jnp.where(lane == e + N_GROUPS, gate_ref[...], 0.0), axis=1, keepdims=True)
    hid = _silu(jnp.dot(xb, wg_ref[0], preferred_element_type=F32)) * jnp.dot(xb, wu_ref[0], preferred_element_type=F32)
    acc_ref[...] += g_e * jnp.dot(hid.astype(BF16), wd_ref[0], preferred_element_type=F32)

    @pl.when(e == N_EXPERTS - 1)
    def _():
        y = _layer_norm_rows(DN_ALPHA * h_ref[...] + acc_ref[...], lg_ref[...], lb_ref[...])
        o_ref[...] = y
        ob_ref[...] = y.astype(BF16)


def _moe(h1, h1b, w_grp, b_grp, w_rt, b_rt, w_gate, w_up, w_down, ln_g, ln_b):
    N, D = h1.shape
    tm = _pick_tile(N, 640)
    w_r = jnp.zeros((D, LANES), F32).at[:, 0:N_GROUPS].set(w_grp).at[:, N_GROUPS:N_GROUPS + N_EXPERTS].set(w_rt)
    b_r = jnp.zeros((1, LANES), F32).at[0, 0:N_GROUPS].set(b_grp).at[0, N_GROUPS:N_GROUPS + N_EXPERTS].set(b_rt)
    full = lambda shape: pl.BlockSpec(shape, lambda i, e: (0,) * len(shape))
    tok = lambda w: pl.BlockSpec((tm, w), lambda i, e: (i, 0))
    return pl.pallas_call(
        _moe_kernel,
        grid=(N // tm, N_EXPERTS),
        in_specs=[tok(D), tok(D), full((D, LANES)), full((1, LANES)),
                  pl.BlockSpec((1, D, D_EXPERT), lambda i, e: (e, 0, 0)),
                  pl.BlockSpec((1, D, D_EXPERT), lambda i, e: (e, 0, 0)),
                  pl.BlockSpec((1, D_EXPERT, D), lambda i, e: (e, 0, 0)),
                  full((1, D)), full((1, D))],
        out_specs=[tok(D), tok(D)],
        out_shape=[jax.ShapeDtypeStruct((N, D), F32), jax.ShapeDtypeStruct((N, D), BF16)],
        scratch_shapes=[pltpu.VMEM((tm, LANES), F32), pltpu.VMEM((tm, D), F32)],
        compiler_params=_cparams("arbitrary", "arbitrary"),
        name="hier_moe_ln",
    )(h1, h1b, w_r.astype(BF16), b_r, w_gate.astype(BF16), w_up.astype(BF16), w_down.astype(BF16),
      ln_g.reshape(1, D).astype(F32), ln_b.reshape(1, D).astype(F32))


def _pad_cols(w, width):
    return jnp.pad(w, ((0, 0), (0, width - w.shape[1])))


def _split_w_in(w):
    o = 0
    w_a = w[:, o:o + 1024]; o += 1024
    gq, gk, gv, ga, gg = (w[:, o:o + 128], w[:, o + 128:o + 256], w[:, o + 256:o + 512],
                          w[:, o + 512:o + 528], w[:, o + 528:o + 784]); o += 784
    w_b = _pad_cols(jnp.concatenate([gq, gk, gv, gg, ga], axis=1), 896)
    cq, ckv, cqi, cki, cwi = (w[:, o:o + 256], w[:, o + 256:o + 384], w[:, o + 384:o + 640],
                              w[:, o + 640:o + 672], w[:, o + 672:o + 680]); o += 680
    w_t = jnp.pad(jnp.concatenate([cq.T, cqi.T, cwi.T], axis=0), ((0, WT_ROWS - 2 * MIX_W - IDX_HEADS), (0, 0)))
    w_n = _pad_cols(jnp.concatenate([ckv, cki], axis=1), 256)
    dq, dk, dv, di, df, do = (w[:, o:o + 256], w[:, o + 256:o + 512], w[:, o + 512:o + 768],
                              w[:, o + 768:o + 772], w[:, o + 772:o + 776], w[:, o + 776:o + 1032]); o += 1032
    w_d = _pad_cols(jnp.concatenate([dq, dk, dv, do, di, df], axis=1), 1152)
    w_g = w[:, o:o + 4096]
    bf = lambda a: a.astype(BF16)
    return bf(w_a), bf(w_b), bf(w_t), bf(w_n), bf(w_d), bf(w_g)


def kernel(x, meta, ln_in_g, ln_in_b, rel_bias, w_in, rwkv_mu, rwkv_w_up, rwkv_w0, rwkv_a_up, rwkv_a0, rwkv_g_up, rwkv_k_k, rwkv_k_a, rwkv_r_k, rwkv_gn_g, rwkv_gn_b, gla_a_up, gla_a_b, gla_norm_g, dsa_kv_norm_g, dsa_w_uk, dsa_w_uv, mlstm_conv_w, mlstm_conv_b, mlstm_i_b, mlstm_f_b, mlstm_norm_g, w_branch, w_out, ln1_g, ln1_b, moe_w_grp, moe_b_grp, moe_w_rt, moe_b_rt, moe_w_gate, moe_w_up, moe_w_down, ln2_g, ln2_b):
    B, S, D = x.shape
    assert D == D_MODEL and S % ROW_TILE == 0
    TP = S + FRONT
    N = B * TP
    topk = min(TOPK_MAX, S // 4)
    bias_tab = _bias_tables(rel_bias)

    h, hb = _embed(x, meta, ln_in_g, ln_in_b)
    h = h.reshape(N, D)
    hb = hb.reshape(N, D)
    for l in range(DEPTH):
        w_a, w_b, w_t, w_n, w_d, w_g = _split_w_in(w_in[l])
        p_a = _proj(hb, w_a)
        p_b = _proj(hb, w_b)
        p_d = _proj(hb, w_d)
        gates = _proj(hb, w_g, act="sigmoid")
        qt, qit, wit, k, ki, vt = _dsa_prep(hb, w_t, w_n, dsa_kv_norm_g[l], dsa_w_uk[l], dsa_w_uv[l])
        y_a = _rwkv(p_a, B, TP, rwkv_mu[l], rwkv_w_up[l], rwkv_w0[l], rwkv_a_up[l], rwkv_a0[l], rwkv_g_up[l],
                    rwkv_k_k[l], rwkv_k_a[l], rwkv_r_k[l], rwkv_gn_g[l], rwkv_gn_b[l])
        y_b = _gla(p_b, B, TP, gla_a_up[l], gla_a_b[l], gla_norm_g[l])
        y_c = _dsa(qt, qit, wit, k, ki, vt, bias_tab, B, TP, topk)
        y_d = _mlstm(p_d, B, TP, mlstm_conv_w[l], mlstm_conv_b[l], mlstm_i_b[l], mlstm_f_b[l], mlstm_norm_g[l])
        ys = (y_a.reshape(N, MIX_W), y_b.reshape(N, MIX_W), y_c, y_d.reshape(N, MIX_W))
        h1, h1b = _merge(h, gates, ys, w_branch[l], w_out[l], ln1_g[l], ln1_b[l])
        h, hb = _moe(h1, h1b, moe_w_grp[l], moe_b_grp[l], moe_w_rt[l], moe_b_rt[l],
                     moe_w_gate[l], moe_w_up[l], moe_w_down[l], ln2_g[l], ln2_b[l])
    return h.reshape(B, TP, D)[:, FRONT:]
```

```python
import functools
import math

import numpy as np
import jax
import jax.numpy as jnp
from jax import lax
from jax.experimental import pallas as pl
from jax.experimental.pallas import tpu as pltpu

F32 = jnp.float32
BF16 = jnp.bfloat16

D_MODEL = 1024
HEAD_DIM = 64
N_HEADS = 4
MIX_W = 256
N_META = 16
CHUNK = 64
LANES = 128
ROW_TILE = 128
FRONT = ROW_TILE
FP = FRONT - N_META
NEG = -1e30
LN_EPS = 1e-5
DEPTH = 2
DN_ALPHA = (2 * DEPTH) ** 0.25

RWKV_GN_EPS = HEAD_DIM * 1e-5
GLA_DK = 32
GLA_TAU = 16.0
DSA_KV_RANK = 128
IDX_HEADS = 8
IDX_DIM = 32
TOPK_MAX = 256
N_BUCKETS = 32
MAX_DISTANCE = 128
CONV_W = 4
N_GROUPS = 4
EPG = 4
N_EXPERTS = 16
D_EXPERT = 256

INT_MIN = -(2 ** 31)
KEY_INF = 0x7F800000
VMEM_LIMIT = 56 * 1024 * 1024


def _cparams(*sem):
    return pltpu.CompilerParams(dimension_semantics=tuple(sem), vmem_limit_bytes=VMEM_LIMIT)


def _pick_tile(n, target):
    best = LANES
    t = LANES
    while t <= min(n, target):
        if n % t == 0:
            best = t
        t += LANES
    return best


def _bdot(a, b):
    return jnp.dot(a.astype(BF16), b.astype(BF16), preferred_element_type=F32)


def _bdot_nt(a, b):
    return lax.dot_general(a.astype(BF16), b.astype(BF16), (((1,), (1,)), ((), ())),
                           preferred_element_type=F32)


def _bdot_tn(a, b):
    return lax.dot_general(a.astype(BF16), b.astype(BF16), (((0,), (0,)), ((), ())),
                           preferred_element_type=F32)


def _split(a):
    hi = a.astype(BF16)
    lo = (a - hi.astype(F32)).astype(BF16)
    return hi, lo


_NN = (((1,), (0,)), ((), ()))
_NT = (((1,), (1,)), ((), ()))
_TN = (((0,), (0,)), ((), ()))


def _dot3(a, b, dims=_NN):
    ah, al = _split(a)
    bh, bl = _split(b)
    dg = lambda x, y: lax.dot_general(x, y, dims, preferred_element_type=F32)
    return dg(ah, bh) + (dg(ah, bl) + dg(al, bh))


def _dot_exact_lhs(a_bf16, b):
    bh, bl = _split(b)
    return (jnp.dot(a_bf16, bh, preferred_element_type=F32)
            + jnp.dot(a_bf16, bl, preferred_element_type=F32))


def _dot_exact_rhs(a, b_bf16):
    ah, al = _split(a)
    return (jnp.dot(ah, b_bf16, preferred_element_type=F32)
            + jnp.dot(al, b_bf16, preferred_element_type=F32))


def _sigmoid(x):
    return 1.0 / (1.0 + jnp.exp(-x))


def _log_sigmoid(x):
    return jnp.minimum(x, 0.0) - jnp.log(1.0 + jnp.exp(-jnp.abs(x)))


def _silu(x):
    return x * _sigmoid(x)


def _iota(shape, dim):
    return lax.broadcasted_iota(jnp.int32, shape, dim)


def _tri_incl(n):
    return (_iota((n, n), 1) <= _iota((n, n), 0))


def _head_ones():
    return ((_iota((MIX_W, MIX_W), 0) // HEAD_DIM) == (_iota((MIX_W, MIX_W), 1) // HEAD_DIM)).astype(BF16)


def _row_ids(rows):
    return pl.program_id(1) * ROW_TILE + _iota((rows, 1), 0)


def _embed_kernel(x_ref, meta_ref, g_ref, b_ref, h_ref, hb_ref):
    j = pl.program_id(1)
    src = jnp.where(j == 0, meta_ref[...], x_ref[0])
    mu = jnp.mean(src, -1, keepdims=True)
    xc = src - mu
    var = jnp.mean(xc * xc, -1, keepdims=True)
    y = xc * lax.rsqrt(var + LN_EPS) * g_ref[...] + b_ref[...]
    h_ref[0] = y
    hb_ref[0] = y.astype(BF16)


def _embed(x, meta, g, b):
    B, S, D = x.shape
    TP = S + FRONT
    meta_pad = jnp.concatenate([jnp.zeros((FP, D), F32), meta.astype(F32)], axis=0)
    return pl.pallas_call(
        _embed_kernel,
        grid=(B, TP // ROW_TILE),
        in_specs=[
            pl.BlockSpec((1, ROW_TILE, D), lambda b, j: (b, jnp.maximum(j - 1, 0), 0)),
            pl.BlockSpec((ROW_TILE, D), lambda b, j: (0, 0)),
            pl.BlockSpec((1, D), lambda b, j: (0, 0)),
            pl.BlockSpec((1, D), lambda b, j: (0, 0)),
        ],
        out_specs=[
            pl.BlockSpec((1, ROW_TILE, D), lambda b, j: (b, j, 0)),
            pl.BlockSpec((1, ROW_TILE, D), lambda b, j: (b, j, 0)),
        ],
        out_shape=[jax.ShapeDtypeStruct((B, TP, D), F32), jax.ShapeDtypeStruct((B, TP, D), BF16)],
        compiler_params=_cparams("parallel", "arbitrary"),
        name="embed_ln",
    )(x, meta_pad, g.reshape(1, D), b.reshape(1, D))


def _proj_kernel(h_ref, w_ref, o_ref, *, act):
    y = jnp.dot(h_ref[...], w_ref[...], preferred_element_type=F32)
    if act == "sigmoid":
        y = _sigmoid(y)
    o_ref[...] = y.astype(o_ref.dtype)


def _proj(hb, w, act=None, out_dtype=F32):
    N, D = hb.shape
    W = w.shape[1]
    tn = W if W <= 1152 else 1024
    tm = _pick_tile(N, 640)
    return pl.pallas_call(
        functools.partial(_proj_kernel, act=act),
        grid=(W // tn, N // tm),
        in_specs=[pl.BlockSpec((tm, D), lambda j, i: (i, 0)),
                  pl.BlockSpec((D, tn), lambda j, i: (0, j))],
        out_specs=pl.BlockSpec((tm, tn), lambda j, i: (i, j)),
        out_shape=jax.ShapeDtypeStruct((N, W), out_dtype),
        compiler_params=_cparams("arbitrary", "arbitrary"),
        name="in_proj",
    )(hb, w)


def _rwkv_kernel(p_ref, mu_ref, wup_ref, w0_ref, aup_ref, a0_ref, gup_ref, kk_ref, ka_ref, rk_ref,
                 gng_ref, gnb_ref, y_ref, carry_ref, s_ref):
    j = pl.program_id(0)
    nb = p_ref.shape[0]
    n_chunks = ROW_TILE // CHUNK

    @pl.when(j == 0)
    def _():
        carry_ref[...] = jnp.zeros_like(carry_ref)
        s_ref[...] = jnp.zeros_like(s_ref)

    valid = (j * ROW_TILE + _iota((ROW_TILE, 1), 0)) >= FP
    first_row = _iota((ROW_TILE, 1), 0) == 0
    ones_h = _head_ones()
    tri = _tri_incl(CHUNK)
    tri_b = tri.astype(BF16)
    strict = _iota((CHUNK, CHUNK), 1) < _iota((CHUNK, CHUNK), 0)
    eye = (_iota((CHUNK, CHUNK), 1) == _iota((CHUNK, CHUNK), 0)).astype(F32)
    heads = [slice(h * HEAD_DIM, (h + 1) * HEAD_DIM) for h in range(N_HEADS)]

    pro = []
    unit = {}
    for b in range(nb):
        p = jnp.where(valid, p_ref[b], 0.0)
        prev = jnp.where(first_row, carry_ref[b], pltpu.roll(p, 1, 0))
        carry_ref[b] = p[ROW_TILE - 1:ROW_TILE, :]
        ps = p + (prev - p) * mu_ref[...]
        r = ps[:, 0:256]
        k = ps[:, 256:512]
        v = ps[:, 512:768]
        lora_in = ps[:, 768:896]
        xg = ps[:, 896:1024]
        w_log = _log_sigmoid(w0_ref[...] + _bdot(jnp.tanh(lora_in), wup_ref[...])) - 0.5
        lw = jnp.where(valid, -jnp.exp(w_log), 0.0)
        alpha = _sigmoid(a0_ref[...] + _bdot(lora_in, aup_ref[...]))
        gate = _bdot(_sigmoid(xg), gup_ref[...])
        kk = k * kk_ref[...]
        kk = kk / jnp.maximum(jnp.sqrt(_dot_exact_rhs(kk * kk, ones_h)), 1e-12)
        k = k * (1.0 + (alpha - 1.0) * ka_ref[...])
        kka = kk * alpha
        pro.append((r, k, v, gate))
        for c in range(n_chunks):
            sl = slice(c * CHUNK, (c + 1) * CHUNK)
            lw_c = lw[sl]
            cum = _dot_exact_lhs(tri_b, lw_c)
            cum_last = cum[CHUNK - 1:CHUNK, :]
            p_inv = jnp.exp(-cum)
            p_tail = jnp.exp(cum_last - cum)
            unit[b, c] = dict(a=-kk[sl] * jnp.exp(cum - lw_c), b=kka[sl] * p_inv, k=k[sl] * p_inv,
                              r=r[sl] * jnp.exp(cum), kb=k[sl] * p_tail, bb=kka[sl] * p_tail,
                              pl=jnp.exp(cum_last), v=v[sl])

    keys = [(b, c, h) for b in range(nb) for c in range(n_chunks) for h in range(N_HEADS)]
    part = lambda name, key: unit[key[0], key[1]][name][:, heads[key[2]]]
    a_ab = {q: jnp.where(strict, _dot3(part("a", q), part("b", q), _NT), 0.0) for q in keys}
    a_ak = {q: jnp.where(strict, _bdot_nt(part("a", q), part("k", q)), 0.0) for q in keys}
    a_rb = {q: jnp.where(tri, _bdot_nt(part("r", q), part("b", q)), 0.0) for q in keys}
    a_rk = {q: jnp.where(tri, _bdot_nt(part("r", q), part("k", q)), 0.0) for q in keys}
    inv = {q: eye + a_ab[q] for q in keys}
    pw = a_ab
    for _ in range(5):
        pw = {q: _bdot(pw[q], pw[q]) for q in keys}
        inv = {q: inv[q] + _bdot(inv[q], pw[q]) for q in keys}
    ak_v = {q: _bdot(a_ak[q], part("v", q)) for q in keys}
    rk_v = {q: _bdot(a_rk[q], part("v", q)) for q in keys}
    kb_v = {q: _bdot_tn(part("v", q), part("kb", q)) for q in keys}

    bh = [(b, h) for b in range(nb) for h in range(N_HEADS)]
    state = {q: s_ref[q[0], q[1]] for q in bh}
    y_parts = {}
    for c in range(n_chunks):
        full = lambda q: (q[0], c, q[1])
        a_s = {q: _bdot_nt(part("a", full(q)), state[q]) for q in bh}
        r_s = {q: _bdot_nt(part("r", full(q)), state[q]) for q in bh}
        u = {q: _bdot(inv[full(q)], a_s[q] + ak_v[full(q)]) for q in bh}
        for q in bh:
            y_parts[full(q)] = r_s[q] + rk_v[full(q)] + _bdot(a_rb[full(q)], u[q])
        state = {q: (state[q] * part("pl", full(q)) + kb_v[full(q)] + _bdot_tn(u[q], part("bb", full(q))))
                 for q in bh}
    for q in bh:
        s_ref[q[0], q[1]] = state[q]

    for b in range(nb):
        r, k, v, gate = pro[b]
        y = jnp.concatenate([jnp.concatenate([y_parts[b, c, h] for h in range(N_HEADS)], axis=1)
                             for c in range(n_chunks)], axis=0)
        mean = _dot_exact_rhs(y, ones_h) * (1.0 / HEAD_DIM)
        yc = y - mean
        var = _dot_exact_rhs(yc * yc, ones_h) * (1.0 / HEAD_DIM)
        yn = yc * lax.rsqrt(var + RWKV_GN_EPS) * gng_ref[...] + gnb_ref[...]
        bonus = _dot_exact_rhs(r * k * rk_ref[...], ones_h) * v
        y_ref[b] = ((yn + bonus) * gate).astype(y_ref.dtype)


def _rwkv(p_a, B, TP, mu, w_up, w0, a_up, a0, g_up, k_k, k_a, r_k, gn_g, gn_b):
    W = MIX_W
    z64 = jnp.zeros((64, W), F32)
    wup_pad = jnp.concatenate([w_up, z64], axis=0).astype(BF16)
    aup_pad = jnp.concatenate([z64, a_up], axis=0).astype(BF16)
    row = lambda a: a.reshape(1, -1).astype(F32)
    full = lambda shape: pl.BlockSpec(shape, lambda j: (0,) * len(shape))
    return pl.pallas_call(
        _rwkv_kernel,
        grid=(TP // ROW_TILE,),
        in_specs=[pl.BlockSpec((B, ROW_TILE, 1024), lambda j: (0, j, 0)),
                  full((1, 1024)), full((128, W)), full((1, W)), full((128, W)), full((1, W)),
                  full((128, W)), full((1, W)), full((1, W)), full((1, W)), full((1, W)), full((1, W))],
        out_specs=pl.BlockSpec((B, ROW_TILE, W), lambda j: (0, j, 0)),
        out_shape=jax.ShapeDtypeStruct((B, TP, W), BF16),
        scratch_shapes=[pltpu.VMEM((B, 1, 1024), F32), pltpu.VMEM((B, N_HEADS, HEAD_DIM, HEAD_DIM), F32)],
        compiler_params=_cparams("arbitrary"),
        name="rwkv7",
    )(p_a.reshape(B, TP, 1024), row(mu), wup_pad, row(w0), aup_pad, row(a0), g_up.astype(BF16),
      row(k_k), row(k_a), row(r_k), row(gn_g), row(gn_b))


def _gla_kernel(p_ref, aup_ref, ab_ref, ng_ref, y_ref, s_ref):
    j = pl.program_id(0)
    nb = p_ref.shape[0]
    n_chunks = ROW_TILE // CHUNK

    @pl.when(j == 0)
    def _():
        s_ref[...] = jnp.zeros_like(s_ref)

    valid = (j * ROW_TILE + _iota((ROW_TILE, 1), 0)) >= FP
    tri = _tri_incl(CHUNK)
    tri_b = tri.astype(BF16)

    og_all = []
    pre = {}
    for b in range(nb):
        p = jnp.where(valid, p_ref[b], 0.0)
        la = _log_sigmoid(_bdot(p[:, 768:896], aup_ref[...]) + ab_ref[...]) * (1.0 / GLA_TAU)
        la = jnp.where(valid, la, 0.0)
        og_all.append(p[:, 512:768])
        for c in range(n_chunks):
            sl = slice(c * CHUNK, (c + 1) * CHUNK)
            pre[b, c] = dict(q=p[sl, 0:128] * (GLA_DK ** -0.5), k=p[sl, 128:256], v=p[sl, 256:512], la=la[sl])
    bc = [(b, c) for b in range(nb) for c in range(n_chunks)]
    keys = [(b, c, h) for (b, c) in bc for h in range(N_HEADS)]
    ks = [slice(h * GLA_DK, (h + 1) * GLA_DK) for h in range(N_HEADS)]
    vs = [slice(h * HEAD_DIM, (h + 1) * HEAD_DIM) for h in range(N_HEADS)]
    b_cum = {u: _dot_exact_lhs(tri_b, pre[u]["la"]) for u in bc}
    b_last = {u: b_cum[u][CHUNK - 1:CHUNK, :] for u in bc}
    q_g = {u: pre[u]["q"] * jnp.exp(b_cum[u]) for u in bc}
    k_g = {u: pre[u]["k"] * jnp.exp(-b_cum[u]) for u in bc}
    k_l = {u: pre[u]["k"] * jnp.exp(b_last[u] - b_cum[u]) for u in bc}
    dec = {u: jnp.exp(b_last[u]) for u in bc}
    att = {u: jnp.where(tri, _bdot_nt(q_g[u[0], u[1]][:, ks[u[2]]], k_g[u[0], u[1]][:, ks[u[2]]]), 0.0)
           for u in keys}
    att_v = {u: _bdot(att[u], pre[u[0], u[1]]["v"][:, vs[u[2]]]) for u in keys}
    kl_v = {u: _bdot_tn(pre[u[0], u[1]]["v"][:, vs[u[2]]], k_l[u[0], u[1]][:, ks[u[2]]]) for u in keys}

    bh = [(b, h) for b in range(nb) for h in range(N_HEADS)]
    state = {q: s_ref[q[0], q[1]] for q in bh}
    o_parts = {}
    for c in range(n_chunks):
        for q in bh:
            o_parts[q[0], c, q[1]] = att_v[q[0], c, q[1]] + _bdot_nt(q_g[q[0], c][:, ks[q[1]]], state[q])
        state = {q: state[q] * dec[q[0], c][:, ks[q[1]]] + kl_v[q[0], c, q[1]] for q in bh}
    for q in bh:
        s_ref[q[0], q[1]] = state[q]

    ones_h = _head_ones()
    for b in range(nb):
        o = jnp.concatenate([jnp.concatenate([o_parts[b, c, h] for h in range(N_HEADS)], axis=1)
                             for c in range(n_chunks)], axis=0)
        ms = _dot_exact_rhs(o * o, ones_h) * (1.0 / HEAD_DIM)
        y = o * lax.rsqrt(ms + 1e-6) * ng_ref[...] * _silu(og_all[b])
        y_ref[b] = y.astype(y_ref.dtype)


def _gla(p_b, B, TP, a_up, a_b, norm_g):
    aup_pad = jnp.zeros((128, 128), F32).at[:a_up.shape[0]].set(a_up).astype(BF16)
    full = lambda shape: pl.BlockSpec(shape, lambda j: (0,) * len(shape))
    return pl.pallas_call(
        _gla_kernel,
        grid=(TP // ROW_TILE,),
        in_specs=[pl.BlockSpec((B, ROW_TILE, 896), lambda j: (0, j, 0)),
                  full((128, 128)), full((1, 128)), full((1, MIX_W))],
        out_specs=pl.BlockSpec((B, ROW_TILE, MIX_W), lambda j: (0, j, 0)),
        out_shape=jax.ShapeDtypeStruct((B, TP, MIX_W), BF16),
        scratch_shapes=[pltpu.VMEM((B, N_HEADS, HEAD_DIM, GLA_DK), F32)],
        compiler_params=_cparams("arbitrary"),
        name="gla",
    )(p_b.reshape(B, TP, 896), aup_pad, a_b.reshape(1, 128).astype(F32),
      jnp.tile(norm_g.astype(F32), N_HEADS).reshape(1, MIX_W))


def _mlstm_kernel(p_ref, cw_ref, cb_ref, ib_ref, fb_ref, ng_ref, y_ref, carry_ref, c_ref, n_ref, m_ref):
    j = pl.program_id(0)
    nb = p_ref.shape[0]
    n_chunks = ROW_TILE // CHUNK

    @pl.when(j == 0)
    def _():
        carry_ref[...] = jnp.zeros_like(carry_ref)
        c_ref[...] = jnp.zeros_like(c_ref)
        n_ref[...] = jnp.zeros_like(n_ref)
        m_ref[...] = jnp.zeros_like(m_ref)

    valid = (j * ROW_TILE + _iota((ROW_TILE, 1), 0)) >= FP
    tri = _tri_incl(CHUNK)
    tri_b = tri.astype(BF16)
    ones_h = _head_ones()

    og_all = []
    pre = {}
    for b in range(nb):
        p = jnp.where(valid, p_ref[b], 0.0)
        a = p[:, 0:512]
        ext = jnp.concatenate([carry_ref[b], a], axis=0)
        carry_ref[b] = a[ROW_TILE - 8:ROW_TILE, :]
        conv = cb_ref[...] + a * cw_ref[CONV_W - 1:CONV_W, :]
        for s in range(1, CONV_W):
            conv = conv + pltpu.roll(ext, s, 0)[8:8 + ROW_TILE, :] * cw_ref[CONV_W - 1 - s:CONV_W - s, :]
        qk = _silu(conv)
        q = jnp.where(valid, qk[:, 0:MIX_W], 0.0)
        k = jnp.where(valid, qk[:, MIX_W:2 * MIX_W], 0.0) * (HEAD_DIM ** -0.5)
        v = p[:, 512:768]
        og_all.append(p[:, 768:1024])
        gates = p[:, 1024:1152]
        li_all = jnp.where(valid, gates + ib_ref[...], NEG)
        lf_all = jnp.where(valid, _log_sigmoid(gates + fb_ref[...]), 0.0)
        for c in range(n_chunks):
            sl = slice(c * CHUNK, (c + 1) * CHUNK)
            pre[b, c] = dict(q=q[sl], k=k[sl], v=v[sl], li=li_all[sl], lf=lf_all[sl])

    bc = [(b, c) for b in range(nb) for c in range(n_chunks)]
    keys = [(b, c, h) for (b, c) in bc for h in range(N_HEADS)]
    heads = [slice(h * HEAD_DIM, (h + 1) * HEAD_DIM) for h in range(N_HEADS)]
    part = lambda name, u: pre[u[0], u[1]][name][:, heads[u[2]]]
    b_cum = {u: _dot_exact_lhs(tri_b, pre[u]["lf"]) for u in bc}
    b_t = {u: b_cum[u].T for u in bc}
    li_t = {u: pre[u]["li"].T for u in bc}
    b_col = {u: b_cum[u[0], u[1]][:, N_HEADS + u[2]:N_HEADS + u[2] + 1] for u in keys}
    b_last = {u: b_col[u][CHUNK - 1:CHUNK, :] for u in keys}
    d_log = {u: jnp.where(tri, b_col[u] - b_t[u[0], u[1]][N_HEADS + u[2]:N_HEADS + u[2] + 1, :]
                          + li_t[u[0], u[1]][u[2]:u[2] + 1, :], -jnp.inf) for u in keys}
    dmax = {u: jnp.max(d_log[u], axis=1, keepdims=True) for u in keys}
    qk = {u: _bdot_nt(part("q", u), part("k", u)) for u in keys}
    s0 = {u: jnp.exp(d_log[u] - dmax[u]) * qk[u] for u in keys}
    sv = {u: _bdot(s0[u], part("v", u)) for u in keys}
    ssum = {u: jnp.sum(s0[u], axis=1, keepdims=True) for u in keys}
    g_loc = {u: b_last[u] - b_col[u] + pre[u[0], u[1]]["li"][:, u[2]:u[2] + 1] for u in keys}
    m_loc = {u: jnp.max(g_loc[u], axis=0, keepdims=True) for u in keys}
    kw = {u: part("k", u) * jnp.exp(g_loc[u] - m_loc[u]) for u in keys}
    kwv = {u: _bdot_tn(kw[u], part("v", u)) for u in keys}
    kwsum = {u: jnp.sum(kw[u], axis=0, keepdims=True) for u in keys}

    bh = [(b, h) for b in range(nb) for h in range(N_HEADS)]
    c_st = {q: c_ref[q[0], q[1]] for q in bh}
    n_st = {q: n_ref[q[0], q[1]] for q in bh}
    m_st = {q: m_ref[q[0], q[1]] for q in bh}
    h_parts = {}
    for c in range(n_chunks):
        full = lambda q: (q[0], c, q[1])
        qc = {q: _bdot(part("q", full(q)), c_st[q]) for q in bh}
        qn = {q: jnp.sum(part("q", full(q)) * n_st[q], axis=1, keepdims=True) for q in bh}
        inter = {q: b_col[full(q)] + m_st[q] for q in bh}
        m_t = {q: jnp.maximum(inter[q], dmax[full(q)]) for q in bh}
        e_loc = {q: jnp.exp(dmax[full(q)] - m_t[q]) for q in bh}
        w_int = {q: jnp.exp(inter[q] - m_t[q]) for q in bh}
        for q in bh:
            num = e_loc[q] * sv[full(q)] + w_int[q] * qc[q]
            den = e_loc[q] * ssum[full(q)] + w_int[q] * qn[q]
            h_parts[full(q)] = num / jnp.maximum(jnp.abs(den), jnp.exp(-m_t[q]))
        m_new = {q: jnp.maximum(b_last[full(q)] + m_st[q], m_loc[full(q)]) for q in bh}
        s_old = {q: jnp.exp(b_last[full(q)] + m_st[q] - m_new[q]) for q in bh}
        s_new = {q: jnp.exp(m_loc[full(q)] - m_new[q]) for q in bh}
        c_st = {q: s_old[q] * c_st[q] + s_new[q] * kwv[full(q)] for q in bh}
        n_st = {q: s_old[q] * n_st[q] + s_new[q] * kwsum[full(q)] for q in bh}
        m_st = m_new
    for q in bh:
        c_ref[q[0], q[1]], n_ref[q[0], q[1]], m_ref[q[0], q[1]] = c_st[q], n_st[q], m_st[q]

    for b in range(nb):
        hh = jnp.concatenate([jnp.concatenate([h_parts[b, c, h] for h in range(N_HEADS)], axis=1)
                              for c in range(n_chunks)], axis=0) * _sigmoid(og_all[b])
        mean = _dot_exact_rhs(hh, ones_h) * (1.0 / HEAD_DIM)
        hc = hh - mean
        var = _dot_exact_rhs(hc * hc, ones_h) * (1.0 / HEAD_DIM)
        y_ref[b] = (hc * lax.rsqrt(var + 1e-5) * ng_ref[...]).astype(y_ref.dtype)


def _mlstm(p_d, B, TP, conv_w, conv_b, i_b, f_b, norm_g):
    ib = jnp.zeros((1, LANES), F32).at[0, 0:N_HEADS].set(i_b)
    fb = jnp.zeros((1, LANES), F32).at[0, N_HEADS:2 * N_HEADS].set(f_b)
    full = lambda shape: pl.BlockSpec(shape, lambda j: (0,) * len(shape))
    return pl.pallas_call(
        _mlstm_kernel,
        grid=(TP // ROW_TILE,),
        in_specs=[pl.BlockSpec((B, ROW_TILE, 1152), lambda j: (0, j, 0)),
                  full((CONV_W, 512)), full((1, 512)), full((1, LANES)), full((1, LANES)), full((1, MIX_W))],
        out_specs=pl.BlockSpec((B, ROW_TILE, MIX_W), lambda j: (0, j, 0)),
        out_shape=jax.ShapeDtypeStruct((B, TP, MIX_W), BF16),
        scratch_shapes=[pltpu.VMEM((B, 8, 512), F32),
                        pltpu.VMEM((B, N_HEADS, HEAD_DIM, HEAD_DIM), F32),
                        pltpu.VMEM((B, N_HEADS, 1, HEAD_DIM), F32),
                        pltpu.VMEM((B, N_HEADS, 1, 1), F32)],
        compiler_params=_cparams("arbitrary"),
        name="mlstm",
    )(p_d.reshape(B, TP, 1152), conv_w.astype(F32), conv_b.reshape(1, 512).astype(F32), ib, fb,
      norm_g.reshape(1, MIX_W).astype(F32))


V_ROWS = 80
WT_ROWS = 528


def _dsa_prep_kernel(h_ref, wt_ref, wn_ref, kvg_ref, wuk_ref, wuvt_ref,
                     qt_ref, qit_ref, wit_ref, k_ref, ki_ref, vt_ref):
    hb = h_ref[...]
    tm = hb.shape[0]
    pt = lax.dot_general(wt_ref[...], hb, _NT, preferred_element_type=F32)
    pn = jnp.dot(hb, wn_ref[...], preferred_element_type=F32)
    ckv = pn[:, 0:DSA_KV_RANK]
    c = ckv * lax.rsqrt(jnp.mean(ckv * ckv, -1, keepdims=True) + 1e-6) * kvg_ref[...]
    cb = c.astype(BF16)
    k_ref[...] = jnp.dot(cb, wuk_ref[...], preferred_element_type=F32).astype(BF16)
    ki_ref[...] = pn[:, DSA_KV_RANK:DSA_KV_RANK + IDX_DIM].astype(BF16)
    vt = lax.dot_general(wuvt_ref[...], cb, _NT, preferred_element_type=F32)
    vt = jnp.where(_iota((V_ROWS, tm), 0) == HEAD_DIM, 1.0, vt)
    for t in range(tm // LANES):
        cs = slice(t * LANES, (t + 1) * LANES)
        for h in range(N_HEADS):
            qt_ref[t, :, h * LANES:(h + 1) * LANES] = (
                pt[h * HEAD_DIM:(h + 1) * HEAD_DIM, cs] * (HEAD_DIM ** -0.5)).astype(BF16)
        for h in range(IDX_HEADS):
            qit_ref[t, :, h * LANES:(h + 1) * LANES] = pt[MIX_W + h * IDX_DIM:MIX_W + (h + 1) * IDX_DIM, cs].astype(BF16)
        wit_ref[t] = pt[2 * MIX_W:2 * MIX_W + IDX_HEADS, cs] * ((IDX_HEADS * IDX_DIM) ** -0.5)
        vt_ref[t] = vt[:, cs].astype(BF16)


def _dsa_prep(hb, w_t, w_n, kv_norm_g, w_uk, w_uv):
    N, D = hb.shape
    tm = _pick_tile(N, 640)
    nt = tm // LANES
    full = lambda shape: pl.BlockSpec(shape, lambda i: (0,) * len(shape))
    wuvt = jnp.pad(w_uv.T, ((0, V_ROWS - HEAD_DIM), (0, 0))).astype(BF16)
    return pl.pallas_call(
        _dsa_prep_kernel,
        grid=(N // tm,),
        in_specs=[pl.BlockSpec((tm, D), lambda i: (i, 0)),
                  full((WT_ROWS, D)), full((D, 256)), full((1, DSA_KV_RANK)),
                  full((DSA_KV_RANK, HEAD_DIM)), full((V_ROWS, DSA_KV_RANK))],
        out_specs=[pl.BlockSpec((nt, HEAD_DIM, N_HEADS * LANES), lambda i: (i, 0, 0)),
                   pl.BlockSpec((nt, IDX_DIM, IDX_HEADS * LANES), lambda i: (i, 0, 0)),
                   pl.BlockSpec((nt, IDX_HEADS, LANES), lambda i: (i, 0, 0)),
                   pl.BlockSpec((tm, HEAD_DIM), lambda i: (i, 0)),
                   pl.BlockSpec((tm, IDX_DIM), lambda i: (i, 0)),
                   pl.BlockSpec((nt, V_ROWS, LANES), lambda i: (i, 0, 0))],
        out_shape=[jax.ShapeDtypeStruct((N // LANES, HEAD_DIM, N_HEADS * LANES), BF16),
                   jax.ShapeDtypeStruct((N // LANES, IDX_DIM, IDX_HEADS * LANES), BF16),
                   jax.ShapeDtypeStruct((N // LANES, IDX_HEADS, LANES), F32),
                   jax.ShapeDtypeStruct((N, HEAD_DIM), BF16),
                   jax.ShapeDtypeStruct((N, IDX_DIM), BF16),
                   jax.ShapeDtypeStruct((N // LANES, V_ROWS, LANES), BF16)],
        compiler_params=_cparams("arbitrary"),
        name="dsa_prep",
    )(hb, w_t, w_n, kv_norm_g.reshape(1, DSA_KV_RANK).astype(F32), w_uk.astype(BF16), wuvt)


def _loop_groups(lo, hi, fn):
    n = jnp.maximum(hi - lo, 0)
    n4 = lax.shift_right_logical(n, 2)

    def body(j, c):
        fn([lo + 4 * j + u for u in range(4)])
        return c

    lax.fori_loop(0, n4, body, 0)
    rest = lo + 4 * n4

    @pl.when((n & 2) == 2)
    def _():
        fn([rest, rest + 1])

    @pl.when((n & 1) == 1)
    def _():
        fn([hi - 1])


def _dsa_kernel(qt_ref, qit_ref, wit_ref, k_ref, ki_ref, vt_ref, bias_ref, y_ref,
                sk_ref, hi_ref, lo_ref, m_ref, acc_ref, lg_ref, mg_ref, *, topk):
    i = pl.program_id(1)
    nk = i + 1
    QT = ROW_TILE
    HQ = N_HEADS * QT
    t_lane = i * QT + _iota((LANES, QT), 1)
    key_pos = lambda kt: kt * LANES + _iota((LANES, QT), 0)
    rows = lambda kt: pl.ds(pl.multiple_of(kt * LANES, LANES), LANES)
    per_head = lambda fn: jnp.concatenate([fn(slice(h * QT, (h + 1) * QT)) for h in range(N_HEADS)], axis=1)

    qit = qit_ref[0]
    wit = wit_ref[0]

    def score_tile(kt, edge):
        rel = jnp.dot(ki_ref[rows(kt), :], qit, preferred_element_type=F32)
        score = jnp.maximum(rel[:, 0:QT], 0.0) * wit[0:1, :]
        for h in range(1, IDX_HEADS):
            score = score + jnp.maximum(rel[:, h * QT:(h + 1) * QT], 0.0) * wit[h:h + 1, :]
        score = jnp.where(score == 0.0, 0.0, score)
        bits = lax.bitcast_convert_type(score, jnp.int32)
        key = jnp.where(bits < 0, bits ^ jnp.int32(0x7FFFFFFF), bits)
        if edge:
            s_pos = key_pos(kt)
            key = jnp.where(s_pos < FP + N_META, jnp.int32(KEY_INF), key)
            key = jnp.where((s_pos >= FP) & (s_pos <= t_lane), key, jnp.int32(INT_MIN))
        sk_ref[kt] = key
        hi_ref[kt] = jnp.right_shift(key, 16).astype(jnp.int16)

    score_tile(0, True)
    _loop_groups(1, i, lambda kts: [score_tile(kt, False) for kt in kts])

    @pl.when(i > 0)
    def _():
        score_tile(i, True)

    def count(pred_fn):
        def body(kt, acc):
            return acc + jnp.where(pred_fn(sk_ref[kt], kt), 1, 0)

        def body4(j, acc):
            for u in range(4):
                acc = body(4 * j + u, acc)
            return acc

        n4 = lax.shift_right_logical(nk, 2)
        acc = lax.fori_loop(0, n4, body4, jnp.zeros((LANES, QT), jnp.int32))
        acc = lax.fori_loop(4 * n4, nk, body, acc)
        return jnp.sum(acc, axis=0, keepdims=True)

    I16_MIN = -(2 ** 15)
    one16 = jnp.ones((LANES, QT), jnp.int16)
    zero16 = jnp.zeros((LANES, QT), jnp.int16)

    def count16(src_ref, cand):
        cand_b = jnp.broadcast_to(cand.astype(jnp.int16), (LANES, QT))

        def body(kt, acc):
            return acc + jnp.where(src_ref[kt] >= cand_b, one16, zero16)

        def body4(j, acc):
            for u in range(4):
                acc = body(4 * j + u, acc)
            return acc

        n4 = lax.shift_right_logical(nk, 2)
        acc = lax.fori_loop(0, n4, body4, zero16)
        acc = lax.fori_loop(4 * n4, nk, body, acc)
        return jnp.sum(acc.astype(jnp.int32), axis=0, keepdims=True)

    def half_search(src_ref, rank):
        def body(it, t):
            cand = t + jnp.left_shift(jnp.int32(1), 15 - it)
            return jnp.where(count16(src_ref, cand) >= rank, cand, t)
        return lax.fori_loop(0, 16, body, jnp.full((1, QT), I16_MIN, jnp.int32))

    tau_hi = half_search(hi_ref, topk)
    n_above = jnp.where(tau_hi < 2 ** 15 - 1, count16(hi_ref, jnp.minimum(tau_hi + 1, 2 ** 15 - 1)), 0)
    tau_hi_b = jnp.broadcast_to(tau_hi.astype(jnp.int16), (LANES, QT))

    def lo_fill(kt, c):
        lo = ((sk_ref[kt] & 0xFFFF) + I16_MIN).astype(jnp.int16)
        lo_ref[kt] = jnp.where(hi_ref[kt] == tau_hi_b, lo, jnp.int16(I16_MIN))
        return c

    lax.fori_loop(0, nk, lo_fill, 0)
    tau_lo = half_search(lo_ref, topk - n_above)
    tau = jnp.left_shift(tau_hi, 16) + (tau_lo - I16_MIN)
    tau = jnp.maximum(tau, jnp.int32(INT_MIN + 1))
    n_gt = count(lambda sk, kt: sk > tau)
    n_ge = count(lambda sk, kt: sk >= tau)
    need = topk - n_gt

    @pl.when(jnp.max(n_ge - topk) > 0)
    def _():
        n_bits = max(1, int(math.ceil(math.log2(sk_ref.shape[0] * LANES + 1))))

        def pos_body(it, x):
            cand = x + jnp.left_shift(jnp.int32(1), n_bits - 1 - it)
            cnt = count(lambda sk, kt: (sk == tau) & (key_pos(kt) < cand))
            return jnp.where(cnt < need, cand, x)

        x = lax.fori_loop(0, n_bits, pos_body, jnp.zeros((1, QT), jnp.int32))
        jmax = jnp.where(n_ge > topk, x, jnp.int32(2 ** 30))

        def drop_body(kt, c):
            sk = sk_ref[kt]
            sk_ref[kt] = jnp.where((sk == tau) & (key_pos(kt) > jmax), jnp.int32(INT_MIN), sk)
            return c

        lax.fori_loop(0, nk, drop_body, 0)

    qt = qt_ref[0]
    m_ref[...] = jnp.full((1, HQ), NEG, F32)
    acc_ref[...] = jnp.zeros((V_ROWS, HQ), F32)

    def attend(kts, near):
        lgs = []
        for kt in kts:
            lg = jnp.dot(k_ref[rows(kt), :], qt, preferred_element_type=F32)
            if near is not None:
                lg = lg + bias_ref[near]
            sel = sk_ref[kt] >= tau
            lgs.append(per_head(lambda hs: jnp.where(sel, lg[:, hs], NEG)))
        tmax = lgs[0]
        for lg in lgs[1:]:
            tmax = jnp.maximum(tmax, lg)
        m_old = m_ref[...]
        m_new = jnp.maximum(m_old, jnp.max(tmax, axis=0, keepdims=True))
        pv = None
        for kt, lg in zip(kts, lgs):
            t = jnp.dot(vt_ref[kt], jnp.exp(lg - m_new).astype(BF16), preferred_element_type=F32)
            pv = t if pv is None else pv + t
        acc_ref[...] = acc_ref[...] * jnp.exp(m_old - m_new) + pv
        m_ref[...] = m_new

    attend([i], 0)

    @pl.when(i > 0)
    def _():
        attend([i - 1], 1)

    n_far = jnp.maximum(i - 1, 0)
    n8 = lax.shift_right_logical(n_far, 3)
    GW = lg_ref.shape[1]

    def park(g, slot):
        tmax = None
        for u in range(GW):
            kt = jnp.minimum(GW * g + u, i)
            lg = jnp.dot(k_ref[rows(kt), :], qt, preferred_element_type=F32)
            sel = sk_ref[kt] >= tau
            lgm = per_head(lambda hs: jnp.where(sel, lg[:, hs], NEG))
            lg_ref[slot, u] = lgm
            tmax = lgm if tmax is None else jnp.maximum(tmax, lgm)
        mg_ref[slot] = jnp.max(tmax, axis=0, keepdims=True)

    def weights(slot):
        m_old = m_ref[...]
        m_new = jnp.maximum(m_old, mg_ref[slot])
        m_ref[...] = m_new
        return jnp.exp(m_old - m_new), [jnp.exp(lg_ref[slot, u] - m_new).astype(BF16) for u in range(GW)]

    def fold(g, corr, prs):
        pv = None
        for u in range(GW):
            t = jnp.dot(vt_ref[GW * g + u], prs[u], preferred_element_type=F32)
            pv = t if pv is None else pv + t
        acc_ref[...] = acc_ref[...] * corr + pv

    @pl.when(n8 > 0)
    def _():
        park(0, 0)

    def pipe_body(jj, c):
        corr, prs = weights(0)
        park(2 * jj + 1, 1)
        fold(2 * jj, corr, prs)
        corr, prs = weights(1)
        park(2 * jj + 2, 0)
        fold(2 * jj + 1, corr, prs)
        return c

    lax.fori_loop(0, n8, pipe_body, 0)
    _loop_groups(8 * n8, i - 1, lambda kts: attend(kts, None))
    acc = acc_ref[...]
    out = acc[0:HEAD_DIM, :] / jnp.maximum(acc[HEAD_DIM:HEAD_DIM + 1, :], 1e-30)
    y_ref[...] = per_head(lambda hs: out[:, hs].T).astype(y_ref.dtype)


def _t5_bucket(dist):
    max_exact = N_BUCKETS // 2
    n = jnp.maximum(dist, 0)
    large = max_exact + (jnp.log(jnp.maximum(n, 1).astype(F32) / max_exact)
                         / math.log(MAX_DISTANCE / max_exact) * (N_BUCKETS - max_exact)).astype(jnp.int32)
    return jnp.where(n < max_exact, n, jnp.minimum(large, N_BUCKETS - 1))


def _bias_tables(rel_bias):
    per_dist = rel_bias[_t5_bucket(jnp.arange(2 * ROW_TILE, dtype=jnp.int32))]
    q_minus_s = np.arange(ROW_TILE)[None, :] - np.arange(ROW_TILE)[:, None]
    far = per_dist[2 * ROW_TILE - 1]
    tabs = [per_dist[np.clip(r * ROW_TILE + q_minus_s, 0, 2 * ROW_TILE - 1)] - far for r in (0, 1)]
    return jnp.stack(tabs).transpose(0, 1, 3, 2).reshape(2, ROW_TILE, N_HEADS * ROW_TILE).astype(F32)


def _dsa(qt, qit, wit, k, ki, vt, bias_tab, B, TP, topk):
    nq = TP // ROW_TILE
    return pl.pallas_call(
        functools.partial(_dsa_kernel, topk=topk),
        grid=(B, nq),
        in_specs=[pl.BlockSpec((1, HEAD_DIM, N_HEADS * LANES), lambda b, i: (b * nq + i, 0, 0)),
                  pl.BlockSpec((1, IDX_DIM, IDX_HEADS * LANES), lambda b, i: (b * nq + i, 0, 0)),
                  pl.BlockSpec((1, IDX_HEADS, LANES), lambda b, i: (b * nq + i, 0, 0)),
                  pl.BlockSpec((TP, HEAD_DIM), lambda b, i: (b, 0)),
                  pl.BlockSpec((TP, IDX_DIM), lambda b, i: (b, 0)),
                  pl.BlockSpec((nq, V_ROWS, LANES), lambda b, i: (b, 0, 0)),
                  pl.BlockSpec((2, ROW_TILE, N_HEADS * ROW_TILE), lambda b, i: (0, 0, 0))],
        out_specs=pl.BlockSpec((ROW_TILE, MIX_W), lambda b, i: (b * nq + i, 0)),
        out_shape=jax.ShapeDtypeStruct((B * TP, MIX_W), BF16),
        scratch_shapes=[pltpu.VMEM((nq, LANES, ROW_TILE), jnp.int32),
                        pltpu.VMEM((nq, LANES, ROW_TILE), jnp.int16),
                        pltpu.VMEM((nq, LANES, ROW_TILE), jnp.int16),
                        pltpu.VMEM((1, N_HEADS * ROW_TILE), F32),
                        pltpu.VMEM((V_ROWS, N_HEADS * ROW_TILE), F32),
                        pltpu.VMEM((2, 4, LANES, N_HEADS * ROW_TILE), F32),
                        pltpu.VMEM((2, 1, N_HEADS * ROW_TILE), F32)],
        compiler_params=_cparams("parallel", "arbitrary"),
        name="dsa_attend",
    )(qt, qit, wit, k, ki, vt, bias_tab)


def _layer_norm_rows(z, g, b):
    mu = jnp.mean(z, -1, keepdims=True)
    zc = z - mu
    var = jnp.mean(zc * zc, -1, keepdims=True)
    return zc * lax.rsqrt(var + LN_EPS) * g + b


def _merge_kernel(h_ref, g_ref, ya_ref, yb_ref, yc_ref, yd_ref, wb_ref, wo_ref, lg_ref, lb_ref,
                  h1_ref, h1b_ref):
    merged = None
    for i, y_ref in enumerate((ya_ref, yb_ref, yc_ref, yd_ref)):
        t = g_ref[:, i * D_MODEL:(i + 1) * D_MODEL] * jnp.dot(y_ref[...], wb_ref[i], preferred_element_type=F32)
        merged = t if merged is None else merged + t
    z = DN_ALPHA * h_ref[...] + jnp.dot(merged.astype(BF16), wo_ref[...], preferred_element_type=F32)
    y = _layer_norm_rows(z, lg_ref[...], lb_ref[...])
    h1_ref[...] = y
    h1b_ref[...] = y.astype(BF16)


def _merge(h, gates, ys, w_branch, w_out, ln_g, ln_b):
    N, D = h.shape
    tm = _pick_tile(N, 512)
    full = lambda shape: pl.BlockSpec(shape, lambda i: (0,) * len(shape))
    tok = lambda w: pl.BlockSpec((tm, w), lambda i: (i, 0))
    return pl.pallas_call(
        _merge_kernel,
        grid=(N // tm,),
        in_specs=[tok(D), tok(4 * D), tok(MIX_W), tok(MIX_W), tok(MIX_W), tok(MIX_W),
                  full((4, MIX_W, D)), full((D, D)), full((1, D)), full((1, D))],
        out_specs=[tok(D), tok(D)],
        out_shape=[jax.ShapeDtypeStruct((N, D), F32), jax.ShapeDtypeStruct((N, D), BF16)],
        compiler_params=_cparams("arbitrary"),
        name="merge_out_ln",
    )(h, gates, *ys, w_branch.astype(BF16), w_out.astype(BF16),
      ln_g.reshape(1, D).astype(F32), ln_b.reshape(1, D).astype(F32))


def _moe_kernel(h_ref, hb_ref, wr_ref, br_ref, wg_ref, wu_ref, wd_ref, lg_ref, lb_ref, o_ref, ob_ref,
                gate_ref, acc_ref):
    e = pl.program_id(1)
    xb = hb_ref[...]
    tm = xb.shape[0]
    lane = _iota((tm, LANES), 1)

    @pl.when(e == 0)
    def _():
        logit = jnp.dot(xb, wr_ref[...], preferred_element_type=F32) + br_ref[...]
        big = jnp.int32(LANES)
        gl = jnp.where(lane < N_GROUPS, logit, -jnp.inf)
        gmax = jnp.max(gl, axis=1, keepdims=True)
        g_sel = jnp.min(jnp.where(gl == gmax, lane, big), axis=1, keepdims=True)
        p_grp = 1.0 / jnp.sum(jnp.exp(gl - gmax), axis=1, keepdims=True)
        lo = N_GROUPS + g_sel * EPG
        el = jnp.where((lane >= lo) & (lane < lo + EPG), logit, -jnp.inf)
        v1 = jnp.max(el, axis=1, keepdims=True)
        i1 = jnp.min(jnp.where(el == v1, lane, big), axis=1, keepdims=True)
        el2 = jnp.where(lane == i1, -jnp.inf, el)
        v2 = jnp.max(el2, axis=1, keepdims=True)
        i2 = jnp.min(jnp.where(el2 == v2, lane, big), axis=1, keepdims=True)
        e2 = jnp.exp(v2 - v1)
        w1 = p_grp / (1.0 + e2)
        w2 = p_grp * e2 / (1.0 + e2)
        gate_ref[...] = jnp.where(lane == i1, w1, 0.0) + jnp.where(lane == i2, w2, 0.0)
        acc_ref[...] = jnp.zeros_like(acc_ref)

    g_e = jnp.sum(jnp.where(lane == e + N_GROUPS, gate_ref[...], 0.0), axis=1, keepdims=True)
    hid = _silu(jnp.dot(xb, wg_ref[0], preferred_element_type=F32)) * jnp.dot(xb, wu_ref[0], preferred_element_type=F32)
    acc_ref[...] += g_e * jnp.dot(hid.astype(BF16), wd_ref[0], preferred_element_type=F32)

    @pl.when(e == N_EXPERTS - 1)
    def _():
        y = _layer_norm_rows(DN_ALPHA * h_ref[...] + acc_ref[...], lg_ref[...], lb_ref[...])
        o_ref[...] = y
        ob_ref[...] = y.astype(BF16)


def _moe(h1, h1b, w_grp, b_grp, w_rt, b_rt, w_gate, w_up, w_down, ln_g, ln_b):
    N, D = h1.shape
    tm = _pick_tile(N, 640)
    w_r = jnp.zeros((D, LANES), F32).at[:, 0:N_GROUPS].set(w_grp).at[:, N_GROUPS:N_GROUPS + N_EXPERTS].set(w_rt)
    b_r = jnp.zeros((1, LANES), F32).at[0, 0:N_GROUPS].set(b_grp).at[0, N_GROUPS:N_GROUPS + N_EXPERTS].set(b_rt)
    full = lambda shape: pl.BlockSpec(shape, lambda i, e: (0,) * len(shape))
    tok = lambda w: pl.BlockSpec((tm, w), lambda i, e: (i, 0))
    return pl.pallas_call(
        _moe_kernel,
        grid=(N // tm, N_EXPERTS),
        in_specs=[tok(D), tok(D), full((D, LANES)), full((1, LANES)),
                  pl.BlockSpec((1, D, D_EXPERT), lambda i, e: (e, 0, 0)),
                  pl.BlockSpec((1, D, D_EXPERT), lambda i, e: (e, 0, 0)),
                  pl.BlockSpec((1, D_EXPERT, D), lambda i, e: (e, 0, 0)),
                  full((1, D)), full((1, D))],
        out_specs=[tok(D), tok(D)],
        out_shape=[jax.ShapeDtypeStruct((N, D), F32), jax.ShapeDtypeStruct((N, D), BF16)],
        scratch_shapes=[pltpu.VMEM((tm, LANES), F32), pltpu.VMEM((tm, D), F32)],
        compiler_params=_cparams("arbitrary", "arbitrary"),
        name="hier_moe_ln",
    )(h1, h1b, w_r.astype(BF16), b_r, w_gate.astype(BF16), w_up.astype(BF16), w_down.astype(BF16),
      ln_g.reshape(1, D).astype(F32), ln_b.reshape(1, D).astype(F32))


def _pad_cols(w, width):
    return jnp.pad(w, ((0, 0), (0, width - w.shape[1])))


def _split_w_in(w):
    o = 0
    w_a = w[:, o:o + 1024]; o += 1024
    gq, gk, gv, ga, gg = (w[:, o:o + 128], w[:, o + 128:o + 256], w[:, o + 256:o + 512],
                          w[:, o + 512:o + 528], w[:, o + 528:o + 784]); o += 784
    w_b = _pad_cols(jnp.concatenate([gq, gk, gv, gg, ga], axis=1), 896)
    cq, ckv, cqi, cki, cwi = (w[:, o:o + 256], w[:, o + 256:o + 384], w[:, o + 384:o + 640],
                              w[:, o + 640:o + 672], w[:, o + 672:o + 680]); o += 680
    w_t = jnp.pad(jnp.concatenate([cq.T, cqi.T, cwi.T], axis=0), ((0, WT_ROWS - 2 * MIX_W - IDX_HEADS), (0, 0)))
    w_n = _pad_cols(jnp.concatenate([ckv, cki], axis=1), 256)
    dq, dk, dv, di, df, do = (w[:, o:o + 256], w[:, o + 256:o + 512], w[:, o + 512:o + 768],
                              w[:, o + 768:o + 772], w[:, o + 772:o + 776], w[:, o + 776:o + 1032]); o += 1032
    w_d = _pad_cols(jnp.concatenate([dq, dk, dv, do, di, df], axis=1), 1152)
    w_g = w[:, o:o + 4096]
    bf = lambda a: a.astype(BF16)
    return bf(w_a), bf(w_b), bf(w_t), bf(w_n), bf(w_d), bf(w_g)


def kernel(x, meta, ln_in_g, ln_in_b, rel_bias, w_in, rwkv_mu, rwkv_w_up, rwkv_w0, rwkv_a_up, rwkv_a0, rwkv_g_up, rwkv_k_k, rwkv_k_a, rwkv_r_k, rwkv_gn_g, rwkv_gn_b, gla_a_up, gla_a_b, gla_norm_g, dsa_kv_norm_g, dsa_w_uk, dsa_w_uv, mlstm_conv_w, mlstm_conv_b, mlstm_i_b, mlstm_f_b, mlstm_norm_g, w_branch, w_out, ln1_g, ln1_b, moe_w_grp, moe_b_grp, moe_w_rt, moe_b_rt, moe_w_gate, moe_w_up, moe_w_down, ln2_g, ln2_b):
    B, S, D = x.shape
    assert D == D_MODEL and S % ROW_TILE == 0
    TP = S + FRONT
    N = B * TP
    topk = min(TOPK_MAX, S // 4)
    bias_tab = _bias_tables(rel_bias)

    h, hb = _embed(x, meta, ln_in_g, ln_in_b)
    h = h.reshape(N, D)
    hb = hb.reshape(N, D)
    for l in range(DEPTH):
        w_a, w_b, w_t, w_n, w_d, w_g = _split_w_in(w_in[l])
        p_a = _proj(hb, w_a)
        p_b = _proj(hb, w_b)
        p_d = _proj(hb, w_d)
        gates = _proj(hb, w_g, act="sigmoid")
        qt, qit, wit, k, ki, vt = _dsa_prep(hb, w_t, w_n, dsa_kv_norm_g[l], dsa_w_uk[l], dsa_w_uv[l])
        y_a = _rwkv(p_a, B, TP, rwkv_mu[l], rwkv_w_up[l], rwkv_w0[l], rwkv_a_up[l], rwkv_a0[l], rwkv_g_up[l],
                    rwkv_k_k[l], rwkv_k_a[l], rwkv_r_k[l], rwkv_gn_g[l], rwkv_gn_b[l])
        y_b = _gla(p_b, B, TP, gla_a_up[l], gla_a_b[l], gla_norm_g[l])
        y_c = _dsa(qt, qit, wit, k, ki, vt, bias_tab, B, TP, topk)
        y_d = _mlstm(p_d, B, TP, mlstm_conv_w[l], mlstm_conv_b[l], mlstm_i_b[l], mlstm_f_b[l], mlstm_norm_g[l])
        ys = (y_a.reshape(N, MIX_W), y_b.reshape(N, MIX_W), y_c, y_d.reshape(N, MIX_W))
        h1, h1b = _merge(h, gates, ys, w_branch[l], w_out[l], ln1_g[l], ln1_b[l])
        h, hb = _moe(h1, h1b, moe_w_grp[l], moe_b_grp[l], moe_w_rt[l], moe_b_rt[l],
                     moe_w_gate[l], moe_w_up[l], moe_w_down[l], ln2_g[l], ln2_b[l])
    return h.reshape(B, TP, D)[:, FRONT:]
```

```python
import functools
import math

import numpy as np
import jax
import jax.numpy as jnp
from jax import lax
from jax.experimental import pallas as pl
from jax.experimental.pallas import tpu as pltpu

F32 = jnp.float32
BF16 = jnp.bfloat16

D_MODEL = 1024
HEAD_DIM = 64
N_HEADS = 4
MIX_W = 256
N_META = 16
CHUNK = 64
LANES = 128
ROW_TILE = 128
FRONT = ROW_TILE
FP = FRONT - N_META
NEG = -1e30
LN_EPS = 1e-5
DEPTH = 2
DN_ALPHA = (2 * DEPTH) ** 0.25

RWKV_GN_EPS = HEAD_DIM * 1e-5
GLA_DK = 32
GLA_TAU = 16.0
DSA_KV_RANK = 128
IDX_HEADS = 8
IDX_DIM = 32
TOPK_MAX = 256
N_BUCKETS = 32
MAX_DISTANCE = 128
CONV_W = 4
N_GROUPS = 4
EPG = 4
N_EXPERTS = 16
D_EXPERT = 256

INT_MIN = -(2 ** 31)
KEY_INF = 0x7F800000
VMEM_LIMIT = 56 * 1024 * 1024


def _cparams(*sem):
    return pltpu.CompilerParams(dimension_semantics=tuple(sem), vmem_limit_bytes=VMEM_LIMIT)


def _pick_tile(n, target):
    best = LANES
    t = LANES
    while t <= min(n, target):
        if n % t == 0:
            best = t
        t += LANES
    return best


def _bdot(a, b):
    return jnp.dot(a.astype(BF16), b.astype(BF16), preferred_element_type=F32)


def _bdot_nt(a, b):
    return lax.dot_general(a.astype(BF16), b.astype(BF16), (((1,), (1,)), ((), ())),
                           preferred_element_type=F32)


def _bdot_tn(a, b):
    return lax.dot_general(a.astype(BF16), b.astype(BF16), (((0,), (0,)), ((), ())),
                           preferred_element_type=F32)


def _split(a):
    hi = a.astype(BF16)
    lo = (a - hi.astype(F32)).astype(BF16)
    return hi, lo


_NN = (((1,), (0,)), ((), ()))
_NT = (((1,), (1,)), ((), ()))
_TN = (((0,), (0,)), ((), ()))


def _dot3(a, b, dims=_NN):
    ah, al = _split(a)
    bh, bl = _split(b)
    dg = lambda x, y: lax.dot_general(x, y, dims, preferred_element_type=F32)
    return dg(ah, bh) + (dg(ah, bl) + dg(al, bh))


def _dot_exact_lhs(a_bf16, b):
    bh, bl = _split(b)
    return (jnp.dot(a_bf16, bh, preferred_element_type=F32)
            + jnp.dot(a_bf16, bl, preferred_element_type=F32))


def _dot_exact_rhs(a, b_bf16):
    ah, al = _split(a)
    return (jnp.dot(ah, b_bf16, preferred_element_type=F32)
            + jnp.dot(al, b_bf16, preferred_element_type=F32))


def _sigmoid(x):
    return 1.0 / (1.0 + jnp.exp(-x))


def _log_sigmoid(x):
    return jnp.minimum(x, 0.0) - jnp.log(1.0 + jnp.exp(-jnp.abs(x)))


def _silu(x):
    return x * _sigmoid(x)


def _iota(shape, dim):
    return lax.broadcasted_iota(jnp.int32, shape, dim)


def _tri_incl(n):
    return (_iota((n, n), 1) <= _iota((n, n), 0))


def _head_ones():
    return ((_iota((MIX_W, MIX_W), 0) // HEAD_DIM) == (_iota((MIX_W, MIX_W), 1) // HEAD_DIM)).astype(BF16)


def _row_ids(rows):
    return pl.program_id(1) * ROW_TILE + _iota((rows, 1), 0)


def _embed_kernel(x_ref, meta_ref, g_ref, b_ref, h_ref, hb_ref):
    j = pl.program_id(1)
    src = jnp.where(j == 0, meta_ref[...], x_ref[0])
    mu = jnp.mean(src, -1, keepdims=True)
    xc = src - mu
    var = jnp.mean(xc * xc, -1, keepdims=True)
    y = xc * lax.rsqrt(var + LN_EPS) * g_ref[...] + b_ref[...]
    h_ref[0] = y
    hb_ref[0] = y.astype(BF16)


def _embed(x, meta, g, b):
    B, S, D = x.shape
    TP = S + FRONT
    meta_pad = jnp.concatenate([jnp.zeros((FP, D), F32), meta.astype(F32)], axis=0)
    return pl.pallas_call(
        _embed_kernel,
        grid=(B, TP // ROW_TILE),
        in_specs=[
            pl.BlockSpec((1, ROW_TILE, D), lambda b, j: (b, jnp.maximum(j - 1, 0), 0)),
            pl.BlockSpec((ROW_TILE, D), lambda b, j: (0, 0)),
            pl.BlockSpec((1, D), lambda b, j: (0, 0)),
            pl.BlockSpec((1, D), lambda b, j: (0, 0)),
        ],
        out_specs=[
            pl.BlockSpec((1, ROW_TILE, D), lambda b, j: (b, j, 0)),
            pl.BlockSpec((1, ROW_TILE, D), lambda b, j: (b, j, 0)),
        ],
        out_shape=[jax.ShapeDtypeStruct((B, TP, D), F32), jax.ShapeDtypeStruct((B, TP, D), BF16)],
        compiler_params=_cparams("parallel", "arbitrary"),
        name="embed_ln",
    )(x, meta_pad, g.reshape(1, D), b.reshape(1, D))


def _proj_kernel(h_ref, w_ref, o_ref, *, act):
    y = jnp.dot(h_ref[...], w_ref[...], preferred_element_type=F32)
    if act == "sigmoid":
        y = _sigmoid(y)
    o_ref[...] = y.astype(o_ref.dtype)


def _proj(hb, w, act=None, out_dtype=F32):
    N, D = hb.shape
    W = w.shape[1]
    tn = W if W <= 1152 else 1024
    tm = _pick_tile(N, 1280)
    return pl.pallas_call(
        functools.partial(_proj_kernel, act=act),
        grid=(W // tn, N // tm),
        in_specs=[pl.BlockSpec((tm, D), lambda j, i: (i, 0)),
                  pl.BlockSpec((D, tn), lambda j, i: (0, j))],
        out_specs=pl.BlockSpec((tm, tn), lambda j, i: (i, j)),
        out_shape=jax.ShapeDtypeStruct((N, W), out_dtype),
        compiler_params=_cparams("arbitrary", "arbitrary"),
        name="in_proj",
    )(hb, w)


def _rwkv_kernel(p_ref, mu_ref, wup_ref, w0_ref, aup_ref, a0_ref, gup_ref, kk_ref, ka_ref, rk_ref,
                 gng_ref, gnb_ref, y_ref, carry_ref, s_ref):
    j = pl.program_id(0)
    nb = p_ref.shape[0]
    n_chunks = ROW_TILE // CHUNK

    @pl.when(j == 0)
    def _():
        carry_ref[...] = jnp.zeros_like(carry_ref)
        s_ref[...] = jnp.zeros_like(s_ref)

    valid = (j * ROW_TILE + _iota((ROW_TILE, 1), 0)) >= FP
    first_row = _iota((ROW_TILE, 1), 0) == 0
    ones_h = _head_ones()
    tri = _tri_incl(CHUNK)
    tri_b = tri.astype(BF16)
    strict = _iota((CHUNK, CHUNK), 1) < _iota((CHUNK, CHUNK), 0)
    eye = (_iota((CHUNK, CHUNK), 1) == _iota((CHUNK, CHUNK), 0)).astype(F32)
    heads = [slice(h * HEAD_DIM, (h + 1) * HEAD_DIM) for h in range(N_HEADS)]

    pro = []
    unit = {}
    for b in range(nb):
        p = jnp.where(valid, p_ref[b], 0.0)
        prev = jnp.where(first_row, carry_ref[b], pltpu.roll(p, 1, 0))
        carry_ref[b] = p[ROW_TILE - 1:ROW_TILE, :]
        ps = p + (prev - p) * mu_ref[...]
        r = ps[:, 0:256]
        k = ps[:, 256:512]
        v = ps[:, 512:768]
        lora_in = ps[:, 768:896]
        xg = ps[:, 896:1024]
        w_log = _log_sigmoid(w0_ref[...] + _bdot(jnp.tanh(lora_in), wup_ref[...])) - 0.5
        lw = jnp.where(valid, -jnp.exp(w_log), 0.0)
        alpha = _sigmoid(a0_ref[...] + _bdot(lora_in, aup_ref[...]))
        gate = _bdot(_sigmoid(xg), gup_ref[...])
        kk = k * kk_ref[...]
        kk = kk / jnp.maximum(jnp.sqrt(_dot_exact_rhs(kk * kk, ones_h)), 1e-12)
        k = k * (1.0 + (alpha - 1.0) * ka_ref[...])
        kka = kk * alpha
        pro.append((r, k, v, gate))
        for c in range(n_chunks):
            sl = slice(c * CHUNK, (c + 1) * CHUNK)
            lw_c = lw[sl]
            cum = _dot_exact_lhs(tri_b, lw_c)
            cum_last = cum[CHUNK - 1:CHUNK, :]
            p_inv = jnp.exp(-cum)
            p_tail = jnp.exp(cum_last - cum)
            unit[b, c] = dict(a=-kk[sl] * jnp.exp(cum - lw_c), b=kka[sl] * p_inv, k=k[sl] * p_inv,
                              r=r[sl] * jnp.exp(cum), kb=k[sl] * p_tail, bb=kka[sl] * p_tail,
                              pl=jnp.exp(cum_last), v=v[sl])

    keys = [(b, c, h) for b in range(nb) for c in range(n_chunks) for h in range(N_HEADS)]
    part = lambda name, key: unit[key[0], key[1]][name][:, heads[key[2]]]
    a_ab = {q: jnp.where(strict, _dot3(part("a", q), part("b", q), _NT), 0.0) for q in keys}
    a_ak = {q: jnp.where(strict, _bdot_nt(part("a", q), part("k", q)), 0.0) for q in keys}
    a_rb = {q: jnp.where(tri, _bdot_nt(part("r", q), part("b", q)), 0.0) for q in keys}
    a_rk = {q: jnp.where(tri, _bdot_nt(part("r", q), part("k", q)), 0.0) for q in keys}
    inv = {q: eye + a_ab[q] for q in keys}
    pw = a_ab
    for _ in range(5):
        pw = {q: _bdot(pw[q], pw[q]) for q in keys}
        inv = {q: inv[q] + _bdot(inv[q], pw[q]) for q in keys}
    ak_v = {q: _bdot(a_ak[q], part("v", q)) for q in keys}
    rk_v = {q: _bdot(a_rk[q], part("v", q)) for q in keys}
    kb_v = {q: _bdot_tn(part("v", q), part("kb", q)) for q in keys}

    bh = [(b, h) for b in range(nb) for h in range(N_HEADS)]
    state = {q: s_ref[q[0], q[1]] for q in bh}
    y_parts = {}
    for c in range(n_chunks):
        full = lambda q: (q[0], c, q[1])
        a_s = {q: _bdot_nt(part("a", full(q)), state[q]) for q in bh}
        r_s = {q: _bdot_nt(part("r", full(q)), state[q]) for q in bh}
        u = {q: _bdot(inv[full(q)], a_s[q] + ak_v[full(q)]) for q in bh}
        for q in bh:
            y_parts[full(q)] = r_s[q] + rk_v[full(q)] + _bdot(a_rb[full(q)], u[q])
        state = {q: (state[q] * part("pl", full(q)) + kb_v[full(q)] + _bdot_tn(u[q], part("bb", full(q))))
                 for q in bh}
    for q in bh:
        s_ref[q[0], q[1]] = state[q]

    for b in range(nb):
        r, k, v, gate = pro[b]
        y = jnp.concatenate([jnp.concatenate([y_parts[b, c, h] for h in range(N_HEADS)], axis=1)
                             for c in range(n_chunks)], axis=0)
        mean = _dot_exact_rhs(y, ones_h) * (1.0 / HEAD_DIM)
        yc = y - mean
        var = _dot_exact_rhs(yc * yc, ones_h) * (1.0 / HEAD_DIM)
        yn = yc * lax.rsqrt(var + RWKV_GN_EPS) * gng_ref[...] + gnb_ref[...]
        bonus = _dot_exact_rhs(r * k * rk_ref[...], ones_h) * v
        y_ref[b] = ((yn + bonus) * gate).astype(y_ref.dtype)


def _rwkv(p_a, B, TP, mu, w_up, w0, a_up, a0, g_up, k_k, k_a, r_k, gn_g, gn_b):
    W = MIX_W
    z64 = jnp.zeros((64, W), F32)
    wup_pad = jnp.concatenate([w_up, z64], axis=0).astype(BF16)
    aup_pad = jnp.concatenate([z64, a_up], axis=0).astype(BF16)
    row = lambda a: a.reshape(1, -1).astype(F32)
    full = lambda shape: pl.BlockSpec(shape, lambda j: (0,) * len(shape))
    return pl.pallas_call(
        _rwkv_kernel,
        grid=(TP // ROW_TILE,),
        in_specs=[pl.BlockSpec((B, ROW_TILE, 1024), lambda j: (0, j, 0)),
                  full((1, 1024)), full((128, W)), full((1, W)), full((128, W)), full((1, W)),
                  full((128, W)), full((1, W)), full((1, W)), full((1, W)), full((1, W)), full((1, W))],
        out_specs=pl.BlockSpec((B, ROW_TILE, W), lambda j: (0, j, 0)),
        out_shape=jax.ShapeDtypeStruct((B, TP, W), BF16),
        scratch_shapes=[pltpu.VMEM((B, 1, 1024), F32), pltpu.VMEM((B, N_HEADS, HEAD_DIM, HEAD_DIM), F32)],
        compiler_params=_cparams("arbitrary"),
        name="rwkv7",
    )(p_a.reshape(B, TP, 1024), row(mu), wup_pad, row(w0), aup_pad, row(a0), g_up.astype(BF16),
      row(k_k), row(k_a), row(r_k), row(gn_g), row(gn_b))


def _gla_kernel(p_ref, aup_ref, ab_ref, ng_ref, y_ref, s_ref):
    j = pl.program_id(0)
    nb = p_ref.shape[0]
    n_chunks = ROW_TILE // CHUNK

    @pl.when(j == 0)
    def _():
        s_ref[...] = jnp.zeros_like(s_ref)

    valid = (j * ROW_TILE + _iota((ROW_TILE, 1), 0)) >= FP
    tri = _tri_incl(CHUNK)
    tri_b = tri.astype(BF16)

    og_all = []
    pre = {}
    for b in range(nb):
        p = jnp.where(valid, p_ref[b], 0.0)
        la = _log_sigmoid(_bdot(p[:, 768:896], aup_ref[...]) + ab_ref[...]) * (1.0 / GLA_TAU)
        la = jnp.where(valid, la, 0.0)
        og_all.append(p[:, 512:768])
        for c in range(n_chunks):
            sl = slice(c * CHUNK, (c + 1) * CHUNK)
            pre[b, c] = dict(q=p[sl, 0:128] * (GLA_DK ** -0.5), k=p[sl, 128:256], v=p[sl, 256:512], la=la[sl])
    bc = [(b, c) for b in range(nb) for c in range(n_chunks)]
    keys = [(b, c, h) for (b, c) in bc for h in range(N_HEADS)]
    ks = [slice(h * GLA_DK, (h + 1) * GLA_DK) for h in range(N_HEADS)]
    vs = [slice(h * HEAD_DIM, (h + 1) * HEAD_DIM) for h in range(N_HEADS)]
    b_cum = {u: _dot_exact_lhs(tri_b, pre[u]["la"]) for u in bc}
    b_last = {u: b_cum[u][CHUNK - 1:CHUNK, :] for u in bc}
    q_g = {u: pre[u]["q"] * jnp.exp(b_cum[u]) for u in bc}
    k_g = {u: pre[u]["k"] * jnp.exp(-b_cum[u]) for u in bc}
    k_l = {u: pre[u]["k"] * jnp.exp(b_last[u] - b_cum[u]) for u in bc}
    dec = {u: jnp.exp(b_last[u]) for u in bc}
    att = {u: jnp.where(tri, _bdot_nt(q_g[u[0], u[1]][:, ks[u[2]]], k_g[u[0], u[1]][:, ks[u[2]]]), 0.0)
           for u in keys}
    att_v = {u: _bdot(att[u], pre[u[0], u[1]]["v"][:, vs[u[2]]]) for u in keys}
    kl_v = {u: _bdot_tn(pre[u[0], u[1]]["v"][:, vs[u[2]]], k_l[u[0], u[1]][:, ks[u[2]]]) for u in keys}

    bh = [(b, h) for b in range(nb) for h in range(N_HEADS)]
    state = {q: s_ref[q[0], q[1]] for q in bh}
    o_parts = {}
    for c in range(n_chunks):
        for q in bh:
            o_parts[q[0], c, q[1]] = att_v[q[0], c, q[1]] + _bdot_nt(q_g[q[0], c][:, ks[q[1]]], state[q])
        state = {q: state[q] * dec[q[0], c][:, ks[q[1]]] + kl_v[q[0], c, q[1]] for q in bh}
    for q in bh:
        s_ref[q[0], q[1]] = state[q]

    ones_h = _head_ones()
    for b in range(nb):
        o = jnp.concatenate([jnp.concatenate([o_parts[b, c, h] for h in range(N_HEADS)], axis=1)
                             for c in range(n_chunks)], axis=0)
        ms = _dot_exact_rhs(o * o, ones_h) * (1.0 / HEAD_DIM)
        y = o * lax.rsqrt(ms + 1e-6) * ng_ref[...] * _silu(og_all[b])
        y_ref[b] = y.astype(y_ref.dtype)


def _gla(p_b, B, TP, a_up, a_b, norm_g):
    aup_pad = jnp.zeros((128, 128), F32).at[:a_up.shape[0]].set(a_up).astype(BF16)
    full = lambda shape: pl.BlockSpec(shape, lambda j: (0,) * len(shape))
    return pl.pallas_call(
        _gla_kernel,
        grid=(TP // ROW_TILE,),
        in_specs=[pl.BlockSpec((B, ROW_TILE, 896), lambda j: (0, j, 0)),
                  full((128, 128)), full((1, 128)), full((1, MIX_W))],
        out_specs=pl.BlockSpec((B, ROW_TILE, MIX_W), lambda j: (0, j, 0)),
        out_shape=jax.ShapeDtypeStruct((B, TP, MIX_W), BF16),
        scratch_shapes=[pltpu.VMEM((B, N_HEADS, HEAD_DIM, GLA_DK), F32)],
        compiler_params=_cparams("arbitrary"),
        name="gla",
    )(p_b.reshape(B, TP, 896), aup_pad, a_b.reshape(1, 128).astype(F32),
      jnp.tile(norm_g.astype(F32), N_HEADS).reshape(1, MIX_W))


def _mlstm_kernel(p_ref, cw_ref, cb_ref, ib_ref, fb_ref, ng_ref, y_ref, carry_ref, c_ref, n_ref, m_ref):
    j = pl.program_id(0)
    nb = p_ref.shape[0]
    n_chunks = ROW_TILE // CHUNK

    @pl.when(j == 0)
    def _():
        carry_ref[...] = jnp.zeros_like(carry_ref)
        c_ref[...] = jnp.zeros_like(c_ref)
        n_ref[...] = jnp.zeros_like(n_ref)
        m_ref[...] = jnp.zeros_like(m_ref)

    valid = (j * ROW_TILE + _iota((ROW_TILE, 1), 0)) >= FP
    tri = _tri_incl(CHUNK)
    tri_b = tri.astype(BF16)
    ones_h = _head_ones()

    og_all = []
    pre = {}
    for b in range(nb):
        p = jnp.where(valid, p_ref[b], 0.0)
        a = p[:, 0:512]
        ext = jnp.concatenate([carry_ref[b], a], axis=0)
        carry_ref[b] = a[ROW_TILE - 8:ROW_TILE, :]
        conv = cb_ref[...] + a * cw_ref[CONV_W - 1:CONV_W, :]
        for s in range(1, CONV_W):
            conv = conv + pltpu.roll(ext, s, 0)[8:8 + ROW_TILE, :] * cw_ref[CONV_W - 1 - s:CONV_W - s, :]
        qk = _silu(conv)
        q = jnp.where(valid, qk[:, 0:MIX_W], 0.0)
        k = jnp.where(valid, qk[:, MIX_W:2 * MIX_W], 0.0) * (HEAD_DIM ** -0.5)
        v = p[:, 512:768]
        og_all.append(p[:, 768:1024])
        gates = p[:, 1024:1152]
        li_all = jnp.where(valid, gates + ib_ref[...], NEG)
        lf_all = jnp.where(valid, _log_sigmoid(gates + fb_ref[...]), 0.0)
        for c in range(n_chunks):
            sl = slice(c * CHUNK, (c + 1) * CHUNK)
            pre[b, c] = dict(q=q[sl], k=k[sl], v=v[sl], li=li_all[sl], lf=lf_all[sl])

    bc = [(b, c) for b in range(nb) for c in range(n_chunks)]
    keys = [(b, c, h) for (b, c) in bc for h in range(N_HEADS)]
    heads = [slice(h * HEAD_DIM, (h + 1) * HEAD_DIM) for h in range(N_HEADS)]
    part = lambda name, u: pre[u[0], u[1]][name][:, heads[u[2]]]
    b_cum = {u: _dot_exact_lhs(tri_b, pre[u]["lf"]) for u in bc}
    b_t = {u: b_cum[u].T for u in bc}
    li_t = {u: pre[u]["li"].T for u in bc}
    b_col = {u: b_cum[u[0], u[1]][:, N_HEADS + u[2]:N_HEADS + u[2] + 1] for u in keys}
    b_last = {u: b_col[u][CHUNK - 1:CHUNK, :] for u in keys}
    d_log = {u: jnp.where(tri, b_col[u] - b_t[u[0], u[1]][N_HEADS + u[2]:N_HEADS + u[2] + 1, :]
                          + li_t[u[0], u[1]][u[2]:u[2] + 1, :], -jnp.inf) for u in keys}
    dmax = {u: jnp.max(d_log[u], axis=1, keepdims=True) for u in keys}
    qk = {u: _bdot_nt(part("q", u), part("k", u)) for u in keys}
    s0 = {u: jnp.exp(d_log[u] - dmax[u]) * qk[u] for u in keys}
    sv = {u: _bdot(s0[u], part("v", u)) for u in keys}
    ssum = {u: jnp.sum(s0[u], axis=1, keepdims=True) for u in keys}
    g_loc = {u: b_last[u] - b_col[u] + pre[u[0], u[1]]["li"][:, u[2]:u[2] + 1] for u in keys}
    m_loc = {u: jnp.max(g_loc[u], axis=0, keepdims=True) for u in keys}
    kw = {u: part("k", u) * jnp.exp(g_loc[u] - m_loc[u]) for u in keys}
    kwv = {u: _bdot_tn(kw[u], part("v", u)) for u in keys}
    kwsum = {u: jnp.sum(kw[u], axis=0, keepdims=True) for u in keys}

    bh = [(b, h) for b in range(nb) for h in range(N_HEADS)]
    c_st = {q: c_ref[q[0], q[1]] for q in bh}
    n_st = {q: n_ref[q[0], q[1]] for q in bh}
    m_st = {q: m_ref[q[0], q[1]] for q in bh}
    h_parts = {}
    for c in range(n_chunks):
        full = lambda q: (q[0], c, q[1])
        qc = {q: _bdot(part("q", full(q)), c_st[q]) for q in bh}
        qn = {q: jnp.sum(part("q", full(q)) * n_st[q], axis=1, keepdims=True) for q in bh}
        inter = {q: b_col[full(q)] + m_st[q] for q in bh}
        m_t = {q: jnp.maximum(inter[q], dmax[full(q)]) for q in bh}
        e_loc = {q: jnp.exp(dmax[full(q)] - m_t[q]) for q in bh}
        w_int = {q: jnp.exp(inter[q] - m_t[q]) for q in bh}
        for q in bh:
            num = e_loc[q] * sv[full(q)] + w_int[q] * qc[q]
            den = e_loc[q] * ssum[full(q)] + w_int[q] * qn[q]
            h_parts[full(q)] = num / jnp.maximum(jnp.abs(den), jnp.exp(-m_t[q]))
        m_new = {q: jnp.maximum(b_last[full(q)] + m_st[q], m_loc[full(q)]) for q in bh}
        s_old = {q: jnp.exp(b_last[full(q)] + m_st[q] - m_new[q]) for q in bh}
        s_new = {q: jnp.exp(m_loc[full(q)] - m_new[q]) for q in bh}
        c_st = {q: s_old[q] * c_st[q] + s_new[q] * kwv[full(q)] for q in bh}
        n_st = {q: s_old[q] * n_st[q] + s_new[q] * kwsum[full(q)] for q in bh}
        m_st = m_new
    for q in bh:
        c_ref[q[0], q[1]], n_ref[q[0], q[1]], m_ref[q[0], q[1]] = c_st[q], n_st[q], m_st[q]

    for b in range(nb):
        hh = jnp.concatenate([jnp.concatenate([h_parts[b, c, h] for h in range(N_HEADS)], axis=1)
                              for c in range(n_chunks)], axis=0) * _sigmoid(og_all[b])
        mean = _dot_exact_rhs(hh, ones_h) * (1.0 / HEAD_DIM)
        hc = hh - mean
        var = _dot_exact_rhs(hc * hc, ones_h) * (1.0 / HEAD_DIM)
        y_ref[b] = (hc * lax.rsqrt(var + 1e-5) * ng_ref[...]).astype(y_ref.dtype)


def _mlstm(p_d, B, TP, conv_w, conv_b, i_b, f_b, norm_g):
    ib = jnp.zeros((1, LANES), F32).at[0, 0:N_HEADS].set(i_b)
    fb = jnp.zeros((1, LANES), F32).at[0, N_HEADS:2 * N_HEADS].set(f_b)
    full = lambda shape: pl.BlockSpec(shape, lambda j: (0,) * len(shape))
    return pl.pallas_call(
        _mlstm_kernel,
        grid=(TP // ROW_TILE,),
        in_specs=[pl.BlockSpec((B, ROW_TILE, 1152), lambda j: (0, j, 0)),
                  full((CONV_W, 512)), full((1, 512)), full((1, LANES)), full((1, LANES)), full((1, MIX_W))],
        out_specs=pl.BlockSpec((B, ROW_TILE, MIX_W), lambda j: (0, j, 0)),
        out_shape=jax.ShapeDtypeStruct((B, TP, MIX_W), BF16),
        scratch_shapes=[pltpu.VMEM((B, 8, 512), F32),
                        pltpu.VMEM((B, N_HEADS, HEAD_DIM, HEAD_DIM), F32),
                        pltpu.VMEM((B, N_HEADS, 1, HEAD_DIM), F32),
                        pltpu.VMEM((B, N_HEADS, 1, 1), F32)],
        compiler_params=_cparams("arbitrary"),
        name="mlstm",
    )(p_d.reshape(B, TP, 1152), conv_w.astype(F32), conv_b.reshape(1, 512).astype(F32), ib, fb,
      norm_g.reshape(1, MIX_W).astype(F32))


V_ROWS = 80
WT_ROWS = 528


def _dsa_prep_kernel(h_ref, wt_ref, wn_ref, kvg_ref, wuk_ref, wuvt_ref,
                     qt_ref, qit_ref, wit_ref, k_ref, ki_ref, vt_ref):
    hb = h_ref[...]
    tm = hb.shape[0]
    pt = lax.dot_general(wt_ref[...], hb, _NT, preferred_element_type=F32)
    pn = jnp.dot(hb, wn_ref[...], preferred_element_type=F32)
    ckv = pn[:, 0:DSA_KV_RANK]
    c = ckv * lax.rsqrt(jnp.mean(ckv * ckv, -1, keepdims=True) + 1e-6) * kvg_ref[...]
    cb = c.astype(BF16)
    k_ref[...] = jnp.dot(cb, wuk_ref[...], preferred_element_type=F32).astype(BF16)
    ki_ref[...] = pn[:, DSA_KV_RANK:DSA_KV_RANK + IDX_DIM].astype(BF16)
    vt = lax.dot_general(wuvt_ref[...], cb, _NT, preferred_element_type=F32)
    vt = jnp.where(_iota((V_ROWS, tm), 0) == HEAD_DIM, 1.0, vt)
    for t in range(tm // LANES):
        cs = slice(t * LANES, (t + 1) * LANES)
        for h in range(N_HEADS):
            qt_ref[t, :, h * LANES:(h + 1) * LANES] = (
                pt[h * HEAD_DIM:(h + 1) * HEAD_DIM, cs] * (HEAD_DIM ** -0.5)).astype(BF16)
        for h in range(IDX_HEADS):
            qit_ref[t, :, h * LANES:(h + 1) * LANES] = pt[MIX_W + h * IDX_DIM:MIX_W + (h + 1) * IDX_DIM, cs].astype(BF16)
        wit_ref[t] = pt[2 * MIX_W:2 * MIX_W + IDX_HEADS, cs] * ((IDX_HEADS * IDX_DIM) ** -0.5)
        vt_ref[t] = vt[:, cs].astype(BF16)


def _dsa_prep(hb, w_t, w_n, kv_norm_g, w_uk, w_uv):
    N, D = hb.shape
    tm = _pick_tile(N, 640)
    nt = tm // LANES
    full = lambda shape: pl.BlockSpec(shape, lambda i: (0,) * len(shape))
    wuvt = jnp.pad(w_uv.T, ((0, V_ROWS - HEAD_DIM), (0, 0))).astype(BF16)
    return pl.pallas_call(
        _dsa_prep_kernel,
        grid=(N // tm,),
        in_specs=[pl.BlockSpec((tm, D), lambda i: (i, 0)),
                  full((WT_ROWS, D)), full((D, 256)), full((1, DSA_KV_RANK)),
                  full((DSA_KV_RANK, HEAD_DIM)), full((V_ROWS, DSA_KV_RANK))],
        out_specs=[pl.BlockSpec((nt, HEAD_DIM, N_HEADS * LANES), lambda i: (i, 0, 0)),
                   pl.BlockSpec((nt, IDX_DIM, IDX_HEADS * LANES), lambda i: (i, 0, 0)),
                   pl.BlockSpec((nt, IDX_HEADS, LANES), lambda i: (i, 0, 0)),
                   pl.BlockSpec((tm, HEAD_DIM), lambda i: (i, 0)),
                   pl.BlockSpec((tm, IDX_DIM), lambda i: (i, 0)),
                   pl.BlockSpec((nt, V_ROWS, LANES), lambda i: (i, 0, 0))],
        out_shape=[jax.ShapeDtypeStruct((N // LANES, HEAD_DIM, N_HEADS * LANES), BF16),
                   jax.ShapeDtypeStruct((N // LANES, IDX_DIM, IDX_HEADS * LANES), BF16),
                   jax.ShapeDtypeStruct((N // LANES, IDX_HEADS, LANES), F32),
                   jax.ShapeDtypeStruct((N, HEAD_DIM), BF16),
                   jax.ShapeDtypeStruct((N, IDX_DIM), BF16),
                   jax.ShapeDtypeStruct((N // LANES, V_ROWS, LANES), BF16)],
        compiler_params=_cparams("arbitrary"),
        name="dsa_prep",
    )(hb, w_t, w_n, kv_norm_g.reshape(1, DSA_KV_RANK).astype(F32), w_uk.astype(BF16), wuvt)


def _loop_groups(lo, hi, fn):
    n = jnp.maximum(hi - lo, 0)
    n4 = lax.shift_right_logical(n, 2)

    def body(j, c):
        fn([lo + 4 * j + u for u in range(4)])
        return c

    lax.fori_loop(0, n4, body, 0)
    rest = lo + 4 * n4

    @pl.when((n & 2) == 2)
    def _():
        fn([rest, rest + 1])

    @pl.when((n & 1) == 1)
    def _():
        fn([hi - 1])


def _dsa_kernel(qt_ref, qit_ref, wit_ref, k_ref, ki_ref, vt_ref, bias_ref, y_ref,
                sk_ref, m_ref, acc_ref, lg_ref, mg_ref, *, topk):
    i = pl.program_id(1)
    nk = i + 1
    QT = ROW_TILE
    HQ = N_HEADS * QT
    t_lane = i * QT + _iota((LANES, QT), 1)
    key_pos = lambda kt: kt * LANES + _iota((LANES, QT), 0)
    rows = lambda kt: pl.ds(pl.multiple_of(kt * LANES, LANES), LANES)
    per_head = lambda fn: jnp.concatenate([fn(slice(h * QT, (h + 1) * QT)) for h in range(N_HEADS)], axis=1)

    qit = qit_ref[0]
    wit = wit_ref[0]

    def score_tile(kt, edge):
        rel = jnp.dot(ki_ref[rows(kt), :], qit, preferred_element_type=F32)
        score = jnp.maximum(rel[:, 0:QT], 0.0) * wit[0:1, :]
        for h in range(1, IDX_HEADS):
            score = score + jnp.maximum(rel[:, h * QT:(h + 1) * QT], 0.0) * wit[h:h + 1, :]
        score = jnp.where(score == 0.0, 0.0, score)
        bits = lax.bitcast_convert_type(score, jnp.int32)
        key = jnp.where(bits < 0, bits ^ jnp.int32(0x7FFFFFFF), bits)
        if edge:
            s_pos = key_pos(kt)
            key = jnp.where(s_pos < FP + N_META, jnp.int32(KEY_INF), key)
            key = jnp.where((s_pos >= FP) & (s_pos <= t_lane), key, jnp.int32(INT_MIN))
        sk_ref[kt] = key

    score_tile(0, True)
    _loop_groups(1, i, lambda kts: [score_tile(kt, False) for kt in kts])

    @pl.when(i > 0)
    def _():
        score_tile(i, True)

    def count(pred_fn):
        def body(kt, acc):
            return acc + jnp.where(pred_fn(sk_ref[kt], kt), 1, 0)

        def body4(j, acc):
            for u in range(4):
                acc = body(4 * j + u, acc)
            return acc

        n4 = lax.shift_right_logical(nk, 2)
        acc = lax.fori_loop(0, n4, body4, jnp.zeros((LANES, QT), jnp.int32))
        acc = lax.fori_loop(4 * n4, nk, body, acc)
        return jnp.sum(acc, axis=0, keepdims=True)

    def bit_body(it, tau):
        cand = tau + jnp.left_shift(jnp.int32(1), 31 - it)
        cnt = count(lambda sk, kt: sk >= cand)
        return jnp.where(cnt >= topk, cand, tau)

    tau = lax.fori_loop(0, 32, bit_body, jnp.full((1, QT), INT_MIN, jnp.int32))
    tau = jnp.maximum(tau, jnp.int32(INT_MIN + 1))
    n_gt = count(lambda sk, kt: sk > tau)
    n_ge = count(lambda sk, kt: sk >= tau)
    need = topk - n_gt

    @pl.when(jnp.max(n_ge - topk) > 0)
    def _():
        n_bits = max(1, int(math.ceil(math.log2(sk_ref.shape[0] * LANES + 1))))

        def pos_body(it, x):
            cand = x + jnp.left_shift(jnp.int32(1), n_bits - 1 - it)
            cnt = count(lambda sk, kt: (sk == tau) & (key_pos(kt) < cand))
            return jnp.where(cnt < need, cand, x)

        x = lax.fori_loop(0, n_bits, pos_body, jnp.zeros((1, QT), jnp.int32))
        jmax = jnp.where(n_ge > topk, x, jnp.int32(2 ** 30))

        def drop_body(kt, c):
            sk = sk_ref[kt]
            sk_ref[kt] = jnp.where((sk == tau) & (key_pos(kt) > jmax), jnp.int32(INT_MIN), sk)
            return c

        lax.fori_loop(0, nk, drop_body, 0)

    qt = qt_ref[0]
    m_ref[...] = jnp.full((1, HQ), NEG, F32)
    acc_ref[...] = jnp.zeros((V_ROWS, HQ), F32)
    GW = lg_ref.shape[1]
    int_max = jnp.int32(2 ** 31 - 1)

    def park(g, slot):
        tmax = None
        for u in range(GW):
            kt = jnp.minimum(GW * g + u, i)
            tau_u = jnp.where(GW * g + u <= i, tau, int_max)
            lg = jnp.dot(k_ref[rows(kt), :], qt, preferred_element_type=F32) + bias_ref[jnp.minimum(i - kt, 2)]
            sel = sk_ref[kt] >= tau_u
            lgm = per_head(lambda hs: jnp.where(sel, lg[:, hs], NEG))
            lg_ref[slot, u] = lgm
            tmax = lgm if tmax is None else jnp.maximum(tmax, lgm)
        mg_ref[slot] = jnp.max(tmax, axis=0, keepdims=True)

    def weights(slot):
        m_old = m_ref[...]
        m_new = jnp.maximum(m_old, mg_ref[slot])
        m_ref[...] = m_new
        return jnp.exp(m_old - m_new), [jnp.exp(lg_ref[slot, u] - m_new).astype(BF16) for u in range(GW)]

    def fold(g, corr, prs):
        pv = None
        for u in range(GW):
            t = jnp.dot(vt_ref[jnp.minimum(GW * g + u, i)], prs[u], preferred_element_type=F32)
            pv = t if pv is None else pv + t
        acc_ref[...] = acc_ref[...] * corr + pv

    park(0, 0)

    def pipe_body(jj, c):
        corr, prs = weights(0)
        park(2 * jj + 1, 1)
        fold(2 * jj, corr, prs)
        corr, prs = weights(1)
        park(2 * jj + 2, 0)
        fold(2 * jj + 1, corr, prs)
        return c

    lax.fori_loop(0, lax.shift_right_logical(i + 2 * GW, 3), pipe_body, 0)
    acc = acc_ref[...]
    out = acc[0:HEAD_DIM, :] / jnp.maximum(acc[HEAD_DIM:HEAD_DIM + 1, :], 1e-30)
    y_ref[...] = per_head(lambda hs: out[:, hs].T).astype(y_ref.dtype)


def _t5_bucket(dist):
    max_exact = N_BUCKETS // 2
    n = jnp.maximum(dist, 0)
    large = max_exact + (jnp.log(jnp.maximum(n, 1).astype(F32) / max_exact)
                         / math.log(MAX_DISTANCE / max_exact) * (N_BUCKETS - max_exact)).astype(jnp.int32)
    return jnp.where(n < max_exact, n, jnp.minimum(large, N_BUCKETS - 1))


def _bias_tables(rel_bias):
    per_dist = rel_bias[_t5_bucket(jnp.arange(2 * ROW_TILE, dtype=jnp.int32))]
    q_minus_s = np.arange(ROW_TILE)[None, :] - np.arange(ROW_TILE)[:, None]
    far = per_dist[2 * ROW_TILE - 1]
    tabs = [per_dist[np.clip(r * ROW_TILE + q_minus_s, 0, 2 * ROW_TILE - 1)] - far for r in (0, 1)]
    tabs.append(jnp.zeros_like(tabs[0]))
    return jnp.stack(tabs).transpose(0, 1, 3, 2).reshape(3, ROW_TILE, N_HEADS * ROW_TILE).astype(F32)


def _dsa(qt, qit, wit, k, ki, vt, bias_tab, B, TP, topk):
    nq = TP // ROW_TILE
    return pl.pallas_call(
        functools.partial(_dsa_kernel, topk=topk),
        grid=(B, nq),
        in_specs=[pl.BlockSpec((1, HEAD_DIM, N_HEADS * LANES), lambda b, i: (b * nq + i, 0, 0)),
                  pl.BlockSpec((1, IDX_DIM, IDX_HEADS * LANES), lambda b, i: (b * nq + i, 0, 0)),
                  pl.BlockSpec((1, IDX_HEADS, LANES), lambda b, i: (b * nq + i, 0, 0)),
                  pl.BlockSpec((TP, HEAD_DIM), lambda b, i: (b, 0)),
                  pl.BlockSpec((TP, IDX_DIM), lambda b, i: (b, 0)),
                  pl.BlockSpec((nq, V_ROWS, LANES), lambda b, i: (b, 0, 0)),
                  pl.BlockSpec((3, ROW_TILE, N_HEADS * ROW_TILE), lambda b, i: (0, 0, 0))],
        out_specs=pl.BlockSpec((ROW_TILE, MIX_W), lambda b, i: (b * nq + i, 0)),
        out_shape=jax.ShapeDtypeStruct((B * TP, MIX_W), BF16),
        scratch_shapes=[pltpu.VMEM((nq, LANES, ROW_TILE), jnp.int32),
                        pltpu.VMEM((1, N_HEADS * ROW_TILE), F32),
                        pltpu.VMEM((V_ROWS, N_HEADS * ROW_TILE), F32),
                        pltpu.VMEM((2, 4, LANES, N_HEADS * ROW_TILE), F32),
                        pltpu.VMEM((2, 1, N_HEADS * ROW_TILE), F32)],
        compiler_params=_cparams("parallel", "arbitrary"),
        name="dsa_attend",
    )(qt, qit, wit, k, ki, vt, bias_tab)


def _layer_norm_rows(z, g, b):
    mu = jnp.mean(z, -1, keepdims=True)
    zc = z - mu
    var = jnp.mean(zc * zc, -1, keepdims=True)
    return zc * lax.rsqrt(var + LN_EPS) * g + b


def _merge_kernel(h_ref, g_ref, ya_ref, yb_ref, yc_ref, yd_ref, wb_ref, wo_ref, lg_ref, lb_ref,
                  h1_ref, h1b_ref):
    merged = None
    for i, y_ref in enumerate((ya_ref, yb_ref, yc_ref, yd_ref)):
        t = g_ref[:, i * D_MODEL:(i + 1) * D_MODEL] * jnp.dot(y_ref[...], wb_ref[i], preferred_element_type=F32)
        merged = t if merged is None else merged + t
    z = DN_ALPHA * h_ref[...] + jnp.dot(merged.astype(BF16), wo_ref[...], preferred_element_type=F32)
    y = _layer_norm_rows(z, lg_ref[...], lb_ref[...])
    h1_ref[...] = y
    h1b_ref[...] = y.astype(BF16)


def _merge(h, gates, ys, w_branch, w_out, ln_g, ln_b):
    N, D = h.shape
    tm = _pick_tile(N, 640)
    full = lambda shape: pl.BlockSpec(shape, lambda i: (0,) * len(shape))
    tok = lambda w: pl.BlockSpec((tm, w), lambda i: (i, 0))
    return pl.pallas_call(
        _merge_kernel,
        grid=(N // tm,),
        in_specs=[tok(D), tok(4 * D), tok(MIX_W), tok(MIX_W), tok(MIX_W), tok(MIX_W),
                  full((4, MIX_W, D)), full((D, D)), full((1, D)), full((1, D))],
        out_specs=[tok(D), tok(D)],
        out_shape=[jax.ShapeDtypeStruct((N, D), F32), jax.ShapeDtypeStruct((N, D), BF16)],
        compiler_params=_cparams("arbitrary"),
        name="merge_out_ln",
    )(h, gates, *ys, w_branch.astype(BF16), w_out.astype(BF16),
      ln_g.reshape(1, D).astype(F32), ln_b.reshape(1, D).astype(F32))


def _moe_kernel(h_ref, hb_ref, wr_ref, br_ref, wg_ref, wu_ref, wd_ref, lg_ref, lb_ref, o_ref, ob_ref,
                gate_ref, acc_ref):
    e = pl.program_id(1)
    xb = hb_ref[...]
    tm = xb.shape[0]
    lane = _iota((tm, LANES), 1)

    @pl.when(e == 0)
    def _():
        logit = jnp.dot(xb, wr_ref[...], preferred_element_type=F32) + br_ref[...]
        big = jnp.int32(LANES)
        gl = jnp.where(lane < N_GROUPS, logit, -jnp.inf)
        gmax = jnp.max(gl, axis=1, keepdims=True)
        g_sel = jnp.min(jnp.where(gl == gmax, lane, big), axis=1, keepdims=True)
        p_grp = 1.0 / jnp.sum(jnp.exp(gl - gmax), axis=1, keepdims=True)
        lo = N_GROUPS + g_sel * EPG
        el = jnp.where((lane >= lo) & (lane < lo + EPG), logit, -jnp.inf)
        v1 = jnp.max(el, axis=1, keepdims=True)
        i1 = jnp.min(jnp.where(el == v1, lane, big), axis=1, keepdims=True)
        el2 = jnp.where(lane == i1, -jnp.inf, el)
        v2 = jnp.max(el2, axis=1, keepdims=True)
        i2 = jnp.min(jnp.where(el2 == v2, lane, big), axis=1, keepdims=True)
        e2 = jnp.exp(v2 - v1)
        w1 = p_grp / (1.0 + e2)
        w2 = p_grp * e2 / (1.0 + e2)
        gate_ref[...] = jnp.where(lane == i1, w1, 0.0) + jnp.where(lane == i2, w2, 0.0)
        acc_ref[...] = jnp.zeros_like(acc_ref)

    g_e = jnp.sum(jnp.where(lane == e + N_GROUPS, gate_ref[...], 0.0), axis=1, keepdims=True)
    hid = _silu(jnp.dot(xb, wg_ref[0], preferred_element_type=F32)) * jnp.dot(xb, wu_ref[0], preferred_element_type=F32)
    acc_ref[...] += g_e * jnp.dot(hid.astype(BF16), wd_ref[0], preferred_element_type=F32)

    @pl.when(e == N_EXPERTS - 1)
    def _():
        y = _layer_norm_rows(DN_ALPHA * h_ref[...] + acc_ref[...], lg_ref[...], lb_ref[...])
        o_ref[...] = y
        ob_ref[...] = y.astype(BF16)


def _moe(h1, h1b, w_grp, b_grp, w_rt, b_rt, w_gate, w_up, w_down, ln_g, ln_b):
    N, D = h1.shape
    tm = _pick_tile(N, 1280)
    w_r = jnp.zeros((D, LANES), F32).at[:, 0:N_GROUPS].set(w_grp).at[:, N_GROUPS:N_GROUPS + N_EXPERTS].set(w_rt)
    b_r = jnp.zeros((1, LANES), F32).at[0, 0:N_GROUPS].set(b_grp).at[0, N_GROUPS:N_GROUPS + N_EXPERTS].set(b_rt)
    full = lambda shape: pl.BlockSpec(shape, lambda i, e: (0,) * len(shape))
    tok = lambda w: pl.BlockSpec((tm, w), lambda i, e: (i, 0))
    return pl.pallas_call(
        _moe_kernel,
        grid=(N // tm, N_EXPERTS),
        in_specs=[tok(D), tok(D), full((D, LANES)), full((1, LANES)),
                  pl.BlockSpec((1, D, D_EXPERT), lambda i, e: (e, 0, 0)),
                  pl.BlockSpec((1, D, D_EXPERT), lambda i, e: (e, 0, 0)),
                  pl.BlockSpec((1, D_EXPERT, D), lambda i, e: (e, 0, 0)),
                  full((1, D)), full((1, D))],
        out_specs=[tok(D), tok(D)],
        out_shape=[jax.ShapeDtypeStruct((N, D), F32), jax.ShapeDtypeStruct((N, D), BF16)],
        scratch_shapes=[pltpu.VMEM((tm, LANES), F32), pltpu.VMEM((tm, D), F32)],
        compiler_params=_cparams("arbitrary", "arbitrary"),
        name="hier_moe_ln",
    )(h1, h1b, w_r.astype(BF16), b_r, w_gate.astype(BF16), w_up.astype(BF16), w_down.astype(BF16),
      ln_g.reshape(1, D).astype(F32), ln_b.reshape(1, D).astype(F32))


def _pad_cols(w, width):
    return jnp.pad(w, ((0, 0), (0, width - w.shape[1])))


def _split_w_in(w):
    o = 0
    w_a = w[:, o:o + 1024]; o += 1024
    gq, gk, gv, ga, gg = (w[:, o:o + 128], w[:, o + 128:o + 256], w[:, o + 256:o + 512],
                          w[:, o + 512:o + 528], w[:, o + 528:o + 784]); o += 784
    w_b = _pad_cols(jnp.concatenate([gq, gk, gv, gg, ga], axis=1), 896)
    cq, ckv, cqi, cki, cwi = (w[:, o:o + 256], w[:, o + 256:o + 384], w[:, o + 384:o + 640],
                              w[:, o + 640:o + 672], w[:, o + 672:o + 680]); o += 680
    w_t = jnp.pad(jnp.concatenate([cq.T, cqi.T, cwi.T], axis=0), ((0, WT_ROWS - 2 * MIX_W - IDX_HEADS), (0, 0)))
    w_n = _pad_cols(jnp.concatenate([ckv, cki], axis=1), 256)
    dq, dk, dv, di, df, do = (w[:, o:o + 256], w[:, o + 256:o + 512], w[:, o + 512:o + 768],
                              w[:, o + 768:o + 772], w[:, o + 772:o + 776], w[:, o + 776:o + 1032]); o += 1032
    w_d = _pad_cols(jnp.concatenate([dq, dk, dv, do, di, df], axis=1), 1152)
    w_g = w[:, o:o + 4096]
    bf = lambda a: a.astype(BF16)
    return bf(w_a), bf(w_b), bf(w_t), bf(w_n), bf(w_d), bf(w_g)


def kernel(x, meta, ln_in_g, ln_in_b, rel_bias, w_in, rwkv_mu, rwkv_w_up, rwkv_w0, rwkv_a_up, rwkv_a0, rwkv_g_up, rwkv_k_k, rwkv_k_a, rwkv_r_k, rwkv_gn_g, rwkv_gn_b, gla_a_up, gla_a_b, gla_norm_g, dsa_kv_norm_g, dsa_w_uk, dsa_w_uv, mlstm_conv_w, mlstm_conv_b, mlstm_i_b, mlstm_f_b, mlstm_norm_g, w_branch, w_out, ln1_g, ln1_b, moe_w_grp, moe_b_grp, moe_w_rt, moe_b_rt, moe_w_gate, moe_w_up, moe_w_down, ln2_g, ln2_b):
    B, S, D = x.shape
    assert D == D_MODEL and S % ROW_TILE == 0
    TP = S + FRONT
    N = B * TP
    topk = min(TOPK_MAX, S // 4)
    bias_tab = _bias_tables(rel_bias)

    h, hb = _embed(x, meta, ln_in_g, ln_in_b)
    h = h.reshape(N, D)
    hb = hb.reshape(N, D)
    for l in range(DEPTH):
        w_a, w_b, w_t, w_n, w_d, w_g = _split_w_in(w_in[l])
        p_a = _proj(hb, w_a)
        p_b = _proj(hb, w_b)
        p_d = _proj(hb, w_d)
        gates = _proj(hb, w_g, act="sigmoid")
        qt, qit, wit, k, ki, vt = _dsa_prep(hb, w_t, w_n, dsa_kv_norm_g[l], dsa_w_uk[l], dsa_w_uv[l])
        y_a = _rwkv(p_a, B, TP, rwkv_mu[l], rwkv_w_up[l], rwkv_w0[l], rwkv_a_up[l], rwkv_a0[l], rwkv_g_up[l],
                    rwkv_k_k[l], rwkv_k_a[l], rwkv_r_k[l], rwkv_gn_g[l], rwkv_gn_b[l])
        y_b = _gla(p_b, B, TP, gla_a_up[l], gla_a_b[l], gla_norm_g[l])
        y_c = _dsa(qt, qit, wit, k, ki, vt, bias_tab, B, TP, topk)
        y_d = _mlstm(p_d, B, TP, mlstm_conv_w[l], mlstm_conv_b[l], mlstm_i_b[l], mlstm_f_b[l], mlstm_norm_g[l])
        ys = (y_a.reshape(N, MIX_W), y_b.reshape(N, MIX_W), y_c, y_d.reshape(N, MIX_W))
        h1, h1b = _merge(h, gates, ys, w_branch[l], w_out[l], ln1_g[l], ln1_b[l])
        h, hb = _moe(h1, h1b, moe_w_grp[l], moe_b_grp[l], moe_w_rt[l], moe_b_rt[l],
                     moe_w_gate[l], moe_w_up[l], moe_w_down[l], ln2_g[l], ln2_b[l])
    return h.reshape(B, TP, D)[:, FRONT:]
```

```python
import functools
import math

import numpy as np
import jax
import jax.numpy as jnp
from jax import lax
from jax.experimental import pallas as pl
from jax.experimental.pallas import tpu as pltpu

F32 = jnp.float32
BF16 = jnp.bfloat16

D_MODEL = 1024
HEAD_DIM = 64
N_HEADS = 4
MIX_W = 256
N_META = 16
CHUNK = 64
LANES = 128
ROW_TILE = 128
FRONT = ROW_TILE
FP = FRONT - N_META
NEG = -1e30
LN_EPS = 1e-5
DEPTH = 2
DN_ALPHA = (2 * DEPTH) ** 0.25

RWKV_GN_EPS = HEAD_DIM * 1e-5
GLA_DK = 32
GLA_TAU = 16.0
DSA_KV_RANK = 128
IDX_HEADS = 8
IDX_DIM = 32
TOPK_MAX = 256
N_BUCKETS = 32
MAX_DISTANCE = 128
CONV_W = 4
N_GROUPS = 4
EPG = 4
N_EXPERTS = 16
D_EXPERT = 256

INT_MIN = -(2 ** 31)
KEY_INF = 0x7F800000
VMEM_LIMIT = 56 * 1024 * 1024


def _cparams(*sem):
    return pltpu.CompilerParams(dimension_semantics=tuple(sem), vmem_limit_bytes=VMEM_LIMIT)


def _pick_tile(n, target):
    best = LANES
    t = LANES
    while t <= min(n, target):
        if n % t == 0:
            best = t
        t += LANES
    return best


def _bdot(a, b):
    return jnp.dot(a.astype(BF16), b.astype(BF16), preferred_element_type=F32)


def _bdot_nt(a, b):
    return lax.dot_general(a.astype(BF16), b.astype(BF16), (((1,), (1,)), ((), ())),
                           preferred_element_type=F32)


def _bdot_tn(a, b):
    return lax.dot_general(a.astype(BF16), b.astype(BF16), (((0,), (0,)), ((), ())),
                           preferred_element_type=F32)


def _split(a):
    hi = a.astype(BF16)
    lo = (a - hi.astype(F32)).astype(BF16)
    return hi, lo


_NN = (((1,), (0,)), ((), ()))
_NT = (((1,), (1,)), ((), ()))
_TN = (((0,), (0,)), ((), ()))


def _dot3(a, b, dims=_NN):
    ah, al = _split(a)
    bh, bl = _split(b)
    dg = lambda x, y: lax.dot_general(x, y, dims, preferred_element_type=F32)
    return dg(ah, bh) + (dg(ah, bl) + dg(al, bh))


def _dot_exact_lhs(a_bf16, b):
    bh, bl = _split(b)
    return (jnp.dot(a_bf16, bh, preferred_element_type=F32)
            + jnp.dot(a_bf16, bl, preferred_element_type=F32))


def _dot_exact_rhs(a, b_bf16):
    ah, al = _split(a)
    return (jnp.dot(ah, b_bf16, preferred_element_type=F32)
            + jnp.dot(al, b_bf16, preferred_element_type=F32))


def _sigmoid(x):
    return 1.0 / (1.0 + jnp.exp(-x))


def _log_sigmoid(x):
    return jnp.minimum(x, 0.0) - jnp.log(1.0 + jnp.exp(-jnp.abs(x)))


def _silu(x):
    return x * _sigmoid(x)


def _iota(shape, dim):
    return lax.broadcasted_iota(jnp.int32, shape, dim)


def _tri_incl(n):
    return (_iota((n, n), 1) <= _iota((n, n), 0))


def _head_ones():
    return ((_iota((MIX_W, MIX_W), 0) // HEAD_DIM) == (_iota((MIX_W, MIX_W), 1) // HEAD_DIM)).astype(BF16)


def _row_ids(rows):
    return pl.program_id(1) * ROW_TILE + _iota((rows, 1), 0)


def _embed_kernel(x_ref, meta_ref, g_ref, b_ref, h_ref, hb_ref):
    j = pl.program_id(1)
    src = jnp.where(j == 0, meta_ref[...], x_ref[0])
    mu = jnp.mean(src, -1, keepdims=True)
    xc = src - mu
    var = jnp.mean(xc * xc, -1, keepdims=True)
    y = xc * lax.rsqrt(var + LN_EPS) * g_ref[...] + b_ref[...]
    h_ref[0] = y
    hb_ref[0] = y.astype(BF16)


def _embed(x, meta, g, b):
    B, S, D = x.shape
    TP = S + FRONT
    meta_pad = jnp.concatenate([jnp.zeros((FP, D), F32), meta.astype(F32)], axis=0)
    return pl.pallas_call(
        _embed_kernel,
        grid=(B, TP // ROW_TILE),
        in_specs=[
            pl.BlockSpec((1, ROW_TILE, D), lambda b, j: (b, jnp.maximum(j - 1, 0), 0)),
            pl.BlockSpec((ROW_TILE, D), lambda b, j: (0, 0)),
            pl.BlockSpec((1, D), lambda b, j: (0, 0)),
            pl.BlockSpec((1, D), lambda b, j: (0, 0)),
        ],
        out_specs=[
            pl.BlockSpec((1, ROW_TILE, D), lambda b, j: (b, j, 0)),
            pl.BlockSpec((1, ROW_TILE, D), lambda b, j: (b, j, 0)),
        ],
        out_shape=[jax.ShapeDtypeStruct((B, TP, D), F32), jax.ShapeDtypeStruct((B, TP, D), BF16)],
        compiler_params=_cparams("parallel", "arbitrary"),
        name="embed_ln",
    )(x, meta_pad, g.reshape(1, D), b.reshape(1, D))


def _proj_kernel(h_ref, w_ref, o_ref, *, act):
    y = jnp.dot(h_ref[...], w_ref[...], preferred_element_type=F32)
    if act == "sigmoid":
        y = _sigmoid(y)
    o_ref[...] = y.astype(o_ref.dtype)


def _proj(hb, w, act=None, out_dtype=F32):
    N, D = hb.shape
    W = w.shape[1]
    tn = W if W <= 1152 else 1024
    tm = _pick_tile(N, 1280)
    return pl.pallas_call(
        functools.partial(_proj_kernel, act=act),
        grid=(W // tn, N // tm),
        in_specs=[pl.BlockSpec((tm, D), lambda j, i: (i, 0)),
                  pl.BlockSpec((D, tn), lambda j, i: (0, j))],
        out_specs=pl.BlockSpec((tm, tn), lambda j, i: (i, j)),
        out_shape=jax.ShapeDtypeStruct((N, W), out_dtype),
        compiler_params=_cparams("arbitrary", "arbitrary"),
        name="in_proj",
    )(hb, w)


def _rwkv_kernel(p_ref, mu_ref, wup_ref, w0_ref, aup_ref, a0_ref, gup_ref, kk_ref, ka_ref, rk_ref,
                 gng_ref, gnb_ref, y_ref, carry_ref, s_ref):
    j = pl.program_id(0)
    nb = p_ref.shape[0]
    n_chunks = ROW_TILE // CHUNK

    @pl.when(j == 0)
    def _():
        carry_ref[...] = jnp.zeros_like(carry_ref)
        s_ref[...] = jnp.zeros_like(s_ref)

    valid = (j * ROW_TILE + _iota((ROW_TILE, 1), 0)) >= FP
    first_row = _iota((ROW_TILE, 1), 0) == 0
    ones_h = _head_ones()
    tri = _tri_incl(CHUNK)
    tri_b = tri.astype(BF16)
    strict = _iota((CHUNK, CHUNK), 1) < _iota((CHUNK, CHUNK), 0)
    eye = (_iota((CHUNK, CHUNK), 1) == _iota((CHUNK, CHUNK), 0)).astype(F32)
    heads = [slice(h * HEAD_DIM, (h + 1) * HEAD_DIM) for h in range(N_HEADS)]

    pro = []
    unit = {}
    for b in range(nb):
        p = jnp.where(valid, p_ref[b], 0.0)
        prev = jnp.where(first_row, carry_ref[b], pltpu.roll(p, 1, 0))
        carry_ref[b] = p[ROW_TILE - 1:ROW_TILE, :]
        ps = p + (prev - p) * mu_ref[...]
        r = ps[:, 0:256]
        k = ps[:, 256:512]
        v = ps[:, 512:768]
        lora_in = ps[:, 768:896]
        xg = ps[:, 896:1024]
        w_log = _log_sigmoid(w0_ref[...] + _bdot(jnp.tanh(lora_in), wup_ref[...])) - 0.5
        lw = jnp.where(valid, -jnp.exp(w_log), 0.0)
        alpha = _sigmoid(a0_ref[...] + _bdot(lora_in, aup_ref[...]))
        gate = _bdot(_sigmoid(xg), gup_ref[...])
        kk = k * kk_ref[...]
        kk = kk / jnp.maximum(jnp.sqrt(_dot_exact_rhs(kk * kk, ones_h)), 1e-12)
        k = k * (1.0 + (alpha - 1.0) * ka_ref[...])
        kka = kk * alpha
        pro.append((r, k, v, gate))
        for c in range(n_chunks):
            sl = slice(c * CHUNK, (c + 1) * CHUNK)
            lw_c = lw[sl]
            cum = _dot_exact_lhs(tri_b, lw_c)
            cum_last = cum[CHUNK - 1:CHUNK, :]
            p_inv = jnp.exp(-cum)
            p_tail = jnp.exp(cum_last - cum)
            unit[b, c] = dict(a=-kk[sl] * jnp.exp(cum - lw_c), b=kka[sl] * p_inv, k=k[sl] * p_inv,
                              r=r[sl] * jnp.exp(cum), kb=k[sl] * p_tail, bb=kka[sl] * p_tail,
                              pl=jnp.exp(cum_last), v=v[sl])

    keys = [(b, c, h) for b in range(nb) for c in range(n_chunks) for h in range(N_HEADS)]
    part = lambda name, key: unit[key[0], key[1]][name][:, heads[key[2]]]
    a_ab = {q: jnp.where(strict, _dot3(part("a", q), part("b", q), _NT), 0.0) for q in keys}
    a_ak = {q: jnp.where(strict, _bdot_nt(part("a", q), part("k", q)), 0.0) for q in keys}
    a_rb = {q: jnp.where(tri, _bdot_nt(part("r", q), part("b", q)), 0.0) for q in keys}
    a_rk = {q: jnp.where(tri, _bdot_nt(part("r", q), part("k", q)), 0.0) for q in keys}
    inv = {q: eye + a_ab[q] for q in keys}
    pw = a_ab
    for _ in range(5):
        pw = {q: _bdot(pw[q], pw[q]) for q in keys}
        inv = {q: inv[q] + _bdot(inv[q], pw[q]) for q in keys}
    ak_v = {q: _bdot(a_ak[q], part("v", q)) for q in keys}
    rk_v = {q: _bdot(a_rk[q], part("v", q)) for q in keys}
    kb_v = {q: _bdot_tn(part("v", q), part("kb", q)) for q in keys}

    bh = [(b, h) for b in range(nb) for h in range(N_HEADS)]
    state = {q: s_ref[q[0], q[1]] for q in bh}
    y_parts = {}
    for c in range(n_chunks):
        full = lambda q: (q[0], c, q[1])
        a_s = {q: _bdot_nt(part("a", full(q)), state[q]) for q in bh}
        r_s = {q: _bdot_nt(part("r", full(q)), state[q]) for q in bh}
        u = {q: _bdot(inv[full(q)], a_s[q] + ak_v[full(q)]) for q in bh}
        for q in bh:
            y_parts[full(q)] = r_s[q] + rk_v[full(q)] + _bdot(a_rb[full(q)], u[q])
        state = {q: (state[q] * part("pl", full(q)) + kb_v[full(q)] + _bdot_tn(u[q], part("bb", full(q))))
                 for q in bh}
    for q in bh:
        s_ref[q[0], q[1]] = state[q]

    for b in range(nb):
        r, k, v, gate = pro[b]
        y = jnp.concatenate([jnp.concatenate([y_parts[b, c, h] for h in range(N_HEADS)], axis=1)
                             for c in range(n_chunks)], axis=0)
        mean = _dot_exact_rhs(y, ones_h) * (1.0 / HEAD_DIM)
        yc = y - mean
        var = _dot_exact_rhs(yc * yc, ones_h) * (1.0 / HEAD_DIM)
        yn = yc * lax.rsqrt(var + RWKV_GN_EPS) * gng_ref[...] + gnb_ref[...]
        bonus = _dot_exact_rhs(r * k * rk_ref[...], ones_h) * v
        y_ref[b] = ((yn + bonus) * gate).astype(y_ref.dtype)


def _rwkv(p_a, B, TP, mu, w_up, w0, a_up, a0, g_up, k_k, k_a, r_k, gn_g, gn_b):
    W = MIX_W
    z64 = jnp.zeros((64, W), F32)
    wup_pad = jnp.concatenate([w_up, z64], axis=0).astype(BF16)
    aup_pad = jnp.concatenate([z64, a_up], axis=0).astype(BF16)
    row = lambda a: a.reshape(1, -1).astype(F32)
    full = lambda shape: pl.BlockSpec(shape, lambda j: (0,) * len(shape))
    return pl.pallas_call(
        _rwkv_kernel,
        grid=(TP // ROW_TILE,),
        in_specs=[pl.BlockSpec((B, ROW_TILE, 1024), lambda j: (0, j, 0)),
                  full((1, 1024)), full((128, W)), full((1, W)), full((128, W)), full((1, W)),
                  full((128, W)), full((1, W)), full((1, W)), full((1, W)), full((1, W)), full((1, W))],
        out_specs=pl.BlockSpec((B, ROW_TILE, W), lambda j: (0, j, 0)),
        out_shape=jax.ShapeDtypeStruct((B, TP, W), BF16),
        scratch_shapes=[pltpu.VMEM((B, 1, 1024), F32), pltpu.VMEM((B, N_HEADS, HEAD_DIM, HEAD_DIM), F32)],
        compiler_params=_cparams("arbitrary"),
        name="rwkv7",
    )(p_a.reshape(B, TP, 1024), row(mu), wup_pad, row(w0), aup_pad, row(a0), g_up.astype(BF16),
      row(k_k), row(k_a), row(r_k), row(gn_g), row(gn_b))


def _gla_kernel(p_ref, aup_ref, ab_ref, ng_ref, y_ref, s_ref):
    j = pl.program_id(0)
    nb = p_ref.shape[0]
    n_chunks = ROW_TILE // CHUNK

    @pl.when(j == 0)
    def _():
        s_ref[...] = jnp.zeros_like(s_ref)

    valid = (j * ROW_TILE + _iota((ROW_TILE, 1), 0)) >= FP
    tri = _tri_incl(CHUNK)
    tri_b = tri.astype(BF16)

    og_all = []
    pre = {}
    for b in range(nb):
        p = jnp.where(valid, p_ref[b], 0.0)
        la = _log_sigmoid(_bdot(p[:, 768:896], aup_ref[...]) + ab_ref[...]) * (1.0 / GLA_TAU)
        la = jnp.where(valid, la, 0.0)
        og_all.append(p[:, 512:768])
        for c in range(n_chunks):
            sl = slice(c * CHUNK, (c + 1) * CHUNK)
            pre[b, c] = dict(q=p[sl, 0:128] * (GLA_DK ** -0.5), k=p[sl, 128:256], v=p[sl, 256:512], la=la[sl])
    bc = [(b, c) for b in range(nb) for c in range(n_chunks)]
    keys = [(b, c, h) for (b, c) in bc for h in range(N_HEADS)]
    ks = [slice(h * GLA_DK, (h + 1) * GLA_DK) for h in range(N_HEADS)]
    vs = [slice(h * HEAD_DIM, (h + 1) * HEAD_DIM) for h in range(N_HEADS)]
    b_cum = {u: _dot_exact_lhs(tri_b, pre[u]["la"]) for u in bc}
    b_last = {u: b_cum[u][CHUNK - 1:CHUNK, :] for u in bc}
    q_g = {u: pre[u]["q"] * jnp.exp(b_cum[u]) for u in bc}
    k_g = {u: pre[u]["k"] * jnp.exp(-b_cum[u]) for u in bc}
    k_l = {u: pre[u]["k"] * jnp.exp(b_last[u] - b_cum[u]) for u in bc}
    dec = {u: jnp.exp(b_last[u]) for u in bc}
    att = {u: jnp.where(tri, _bdot_nt(q_g[u[0], u[1]][:, ks[u[2]]], k_g[u[0], u[1]][:, ks[u[2]]]), 0.0)
           for u in keys}
    att_v = {u: _bdot(att[u], pre[u[0], u[1]]["v"][:, vs[u[2]]]) for u in keys}
    kl_v = {u: _bdot_tn(pre[u[0], u[1]]["v"][:, vs[u[2]]], k_l[u[0], u[1]][:, ks[u[2]]]) for u in keys}

    bh = [(b, h) for b in range(nb) for h in range(N_HEADS)]
    state = {q: s_ref[q[0], q[1]] for q in bh}
    o_parts = {}
    for c in range(n_chunks):
        for q in bh:
            o_parts[q[0], c, q[1]] = att_v[q[0], c, q[1]] + _bdot_nt(q_g[q[0], c][:, ks[q[1]]], state[q])
        state = {q: state[q] * dec[q[0], c][:, ks[q[1]]] + kl_v[q[0], c, q[1]] for q in bh}
    for q in bh:
        s_ref[q[0], q[1]] = state[q]

    ones_h = _head_ones()
    for b in range(nb):
        o = jnp.concatenate([jnp.concatenate([o_parts[b, c, h] for h in range(N_HEADS)], axis=1)
                             for c in range(n_chunks)], axis=0)
        ms = _dot_exact_rhs(o * o, ones_h) * (1.0 / HEAD_DIM)
        y = o * lax.rsqrt(ms + 1e-6) * ng_ref[...] * _silu(og_all[b])
        y_ref[b] = y.astype(y_ref.dtype)


def _gla(p_b, B, TP, a_up, a_b, norm_g):
    aup_pad = jnp.zeros((128, 128), F32).at[:a_up.shape[0]].set(a_up).astype(BF16)
    full = lambda shape: pl.BlockSpec(shape, lambda j: (0,) * len(shape))
    return pl.pallas_call(
        _gla_kernel,
        grid=(TP // ROW_TILE,),
        in_specs=[pl.BlockSpec((B, ROW_TILE, 896), lambda j: (0, j, 0)),
                  full((128, 128)), full((1, 128)), full((1, MIX_W))],
        out_specs=pl.BlockSpec((B, ROW_TILE, MIX_W), lambda j: (0, j, 0)),
        out_shape=jax.ShapeDtypeStruct((B, TP, MIX_W), BF16),
        scratch_shapes=[pltpu.VMEM((B, N_HEADS, HEAD_DIM, GLA_DK), F32)],
        compiler_params=_cparams("arbitrary"),
        name="gla",
    )(p_b.reshape(B, TP, 896), aup_pad, a_b.reshape(1, 128).astype(F32),
      jnp.tile(norm_g.astype(F32), N_HEADS).reshape(1, MIX_W))


def _mlstm_kernel(p_ref, cw_ref, cb_ref, ib_ref, fb_ref, ng_ref, y_ref, carry_ref, c_ref, n_ref, m_ref):
    j = pl.program_id(0)
    nb = p_ref.shape[0]
    n_chunks = ROW_TILE // CHUNK

    @pl.when(j == 0)
    def _():
        carry_ref[...] = jnp.zeros_like(carry_ref)
        c_ref[...] = jnp.zeros_like(c_ref)
        n_ref[...] = jnp.zeros_like(n_ref)
        m_ref[...] = jnp.zeros_like(m_ref)

    valid = (j * ROW_TILE + _iota((ROW_TILE, 1), 0)) >= FP
    tri = _tri_incl(CHUNK)
    tri_b = tri.astype(BF16)
    ones_h = _head_ones()

    og_all = []
    pre = {}
    for b in range(nb):
        p = jnp.where(valid, p_ref[b], 0.0)
        a = p[:, 0:512]
        ext = jnp.concatenate([carry_ref[b], a], axis=0)
        carry_ref[b] = a[ROW_TILE - 8:ROW_TILE, :]
        conv = cb_ref[...] + a * cw_ref[CONV_W - 1:CONV_W, :]
        for s in range(1, CONV_W):
            conv = conv + pltpu.roll(ext, s, 0)[8:8 + ROW_TILE, :] * cw_ref[CONV_W - 1 - s:CONV_W - s, :]
        qk = _silu(conv)
        q = jnp.where(valid, qk[:, 0:MIX_W], 0.0)
        k = jnp.where(valid, qk[:, MIX_W:2 * MIX_W], 0.0) * (HEAD_DIM ** -0.5)
        v = p[:, 512:768]
        og_all.append(p[:, 768:1024])
        gates = p[:, 1024:1152]
        li_all = jnp.where(valid, gates + ib_ref[...], NEG)
        lf_all = jnp.where(valid, _log_sigmoid(gates + fb_ref[...]), 0.0)
        for c in range(n_chunks):
            sl = slice(c * CHUNK, (c + 1) * CHUNK)
            pre[b, c] = dict(q=q[sl], k=k[sl], v=v[sl], li=li_all[sl], lf=lf_all[sl])

    bc = [(b, c) for b in range(nb) for c in range(n_chunks)]
    keys = [(b, c, h) for (b, c) in bc for h in range(N_HEADS)]
    heads = [slice(h * HEAD_DIM, (h + 1) * HEAD_DIM) for h in range(N_HEADS)]
    part = lambda name, u: pre[u[0], u[1]][name][:, heads[u[2]]]
    b_cum = {u: _dot_exact_lhs(tri_b, pre[u]["lf"]) for u in bc}
    b_t = {u: b_cum[u].T for u in bc}
    li_t = {u: pre[u]["li"].T for u in bc}
    b_col = {u: b_cum[u[0], u[1]][:, N_HEADS + u[2]:N_HEADS + u[2] + 1] for u in keys}
    b_last = {u: b_col[u][CHUNK - 1:CHUNK, :] for u in keys}
    d_log = {u: jnp.where(tri, b_col[u] - b_t[u[0], u[1]][N_HEADS + u[2]:N_HEADS + u[2] + 1, :]
                          + li_t[u[0], u[1]][u[2]:u[2] + 1, :], -jnp.inf) for u in keys}
    dmax = {u: jnp.max(d_log[u], axis=1, keepdims=True) for u in keys}
    qk = {u: _bdot_nt(part("q", u), part("k", u)) for u in keys}
    s0 = {u: jnp.exp(d_log[u] - dmax[u]) * qk[u] for u in keys}
    sv = {u: _bdot(s0[u], part("v", u)) for u in keys}
    ssum = {u: jnp.sum(s0[u], axis=1, keepdims=True) for u in keys}
    g_loc = {u: b_last[u] - b_col[u] + pre[u[0], u[1]]["li"][:, u[2]:u[2] + 1] for u in keys}
    m_loc = {u: jnp.max(g_loc[u], axis=0, keepdims=True) for u in keys}
    kw = {u: part("k", u) * jnp.exp(g_loc[u] - m_loc[u]) for u in keys}
    kwv = {u: _bdot_tn(kw[u], part("v", u)) for u in keys}
    kwsum = {u: jnp.sum(kw[u], axis=0, keepdims=True) for u in keys}

    bh = [(b, h) for b in range(nb) for h in range(N_HEADS)]
    c_st = {q: c_ref[q[0], q[1]] for q in bh}
    n_st = {q: n_ref[q[0], q[1]] for q in bh}
    m_st = {q: m_ref[q[0], q[1]] for q in bh}
    h_parts = {}
    for c in range(n_chunks):
        full = lambda q: (q[0], c, q[1])
        qc = {q: _bdot(part("q", full(q)), c_st[q]) for q in bh}
        qn = {q: jnp.sum(part("q", full(q)) * n_st[q], axis=1, keepdims=True) for q in bh}
        inter = {q: b_col[full(q)] + m_st[q] for q in bh}
        m_t = {q: jnp.maximum(inter[q], dmax[full(q)]) for q in bh}
        e_loc = {q: jnp.exp(dmax[full(q)] - m_t[q]) for q in bh}
        w_int = {q: jnp.exp(inter[q] - m_t[q]) for q in bh}
        for q in bh:
            num = e_loc[q] * sv[full(q)] + w_int[q] * qc[q]
            den = e_loc[q] * ssum[full(q)] + w_int[q] * qn[q]
            h_parts[full(q)] = num / jnp.maximum(jnp.abs(den), jnp.exp(-m_t[q]))
        m_new = {q: jnp.maximum(b_last[full(q)] + m_st[q], m_loc[full(q)]) for q in bh}
        s_old = {q: jnp.exp(b_last[full(q)] + m_st[q] - m_new[q]) for q in bh}
        s_new = {q: jnp.exp(m_loc[full(q)] - m_new[q]) for q in bh}
        c_st = {q: s_old[q] * c_st[q] + s_new[q] * kwv[full(q)] for q in bh}
        n_st = {q: s_old[q] * n_st[q] + s_new[q] * kwsum[full(q)] for q in bh}
        m_st = m_new
    for q in bh:
        c_ref[q[0], q[1]], n_ref[q[0], q[1]], m_ref[q[0], q[1]] = c_st[q], n_st[q], m_st[q]

    for b in range(nb):
        hh = jnp.concatenate([jnp.concatenate([h_parts[b, c, h] for h in range(N_HEADS)], axis=1)
                              for c in range(n_chunks)], axis=0) * _sigmoid(og_all[b])
        mean = _dot_exact_rhs(hh, ones_h) * (1.0 / HEAD_DIM)
        hc = hh - mean
        var = _dot_exact_rhs(hc * hc, ones_h) * (1.0 / HEAD_DIM)
        y_ref[b] = (hc * lax.rsqrt(var + 1e-5) * ng_ref[...]).astype(y_ref.dtype)


def _mlstm(p_d, B, TP, conv_w, conv_b, i_b, f_b, norm_g):
    ib = jnp.zeros((1, LANES), F32).at[0, 0:N_HEADS].set(i_b)
    fb = jnp.zeros((1, LANES), F32).at[0, N_HEADS:2 * N_HEADS].set(f_b)
    full = lambda shape: pl.BlockSpec(shape, lambda j: (0,) * len(shape))
    return pl.pallas_call(
        _mlstm_kernel,
        grid=(TP // ROW_TILE,),
        in_specs=[pl.BlockSpec((B, ROW_TILE, 1152), lambda j: (0, j, 0)),
                  full((CONV_W, 512)), full((1, 512)), full((1, LANES)), full((1, LANES)), full((1, MIX_W))],
        out_specs=pl.BlockSpec((B, ROW_TILE, MIX_W), lambda j: (0, j, 0)),
        out_shape=jax.ShapeDtypeStruct((B, TP, MIX_W), BF16),
        scratch_shapes=[pltpu.VMEM((B, 8, 512), F32),
                        pltpu.VMEM((B, N_HEADS, HEAD_DIM, HEAD_DIM), F32),
                        pltpu.VMEM((B, N_HEADS, 1, HEAD_DIM), F32),
                        pltpu.VMEM((B, N_HEADS, 1, 1), F32)],
        compiler_params=_cparams("arbitrary"),
        name="mlstm",
    )(p_d.reshape(B, TP, 1152), conv_w.astype(F32), conv_b.reshape(1, 512).astype(F32), ib, fb,
      norm_g.reshape(1, MIX_W).astype(F32))


V_ROWS = 80
WT_ROWS = 528


def _dsa_prep_kernel(h_ref, wt_ref, wn_ref, kvg_ref, wuk_ref, wuvt_ref,
                     qt_ref, qit_ref, wit_ref, k_ref, ki_ref, vt_ref):
    hb = h_ref[...]
    tm = hb.shape[0]
    pt = lax.dot_general(wt_ref[...], hb, _NT, preferred_element_type=F32)
    pn = jnp.dot(hb, wn_ref[...], preferred_element_type=F32)
    ckv = pn[:, 0:DSA_KV_RANK]
    c = ckv * lax.rsqrt(jnp.mean(ckv * ckv, -1, keepdims=True) + 1e-6) * kvg_ref[...]
    cb = c.astype(BF16)
    k_ref[...] = jnp.dot(cb, wuk_ref[...], preferred_element_type=F32).astype(BF16)
    ki_ref[...] = pn[:, DSA_KV_RANK:DSA_KV_RANK + IDX_DIM].astype(BF16)
    vt = lax.dot_general(wuvt_ref[...], cb, _NT, preferred_element_type=F32)
    vt = jnp.where(_iota((V_ROWS, tm), 0) == HEAD_DIM, 1.0, vt)
    for t in range(tm // LANES):
        cs = slice(t * LANES, (t + 1) * LANES)
        for h in range(N_HEADS):
            qt_ref[t, :, h * LANES:(h + 1) * LANES] = (
                pt[h * HEAD_DIM:(h + 1) * HEAD_DIM, cs] * (HEAD_DIM ** -0.5)).astype(BF16)
        for h in range(IDX_HEADS):
            qit_ref[t, :, h * LANES:(h + 1) * LANES] = pt[MIX_W + h * IDX_DIM:MIX_W + (h + 1) * IDX_DIM, cs].astype(BF16)
        wit_ref[t] = pt[2 * MIX_W:2 * MIX_W + IDX_HEADS, cs] * ((IDX_HEADS * IDX_DIM) ** -0.5)
        vt_ref[t] = vt[:, cs].astype(BF16)


def _dsa_prep(hb, w_t, w_n, kv_norm_g, w_uk, w_uv):
    N, D = hb.shape
    tm = _pick_tile(N, 640)
    nt = tm // LANES
    full = lambda shape: pl.BlockSpec(shape, lambda i: (0,) * len(shape))
    wuvt = jnp.pad(w_uv.T, ((0, V_ROWS - HEAD_DIM), (0, 0))).astype(BF16)
    return pl.pallas_call(
        _dsa_prep_kernel,
        grid=(N // tm,),
        in_specs=[pl.BlockSpec((tm, D), lambda i: (i, 0)),
                  full((WT_ROWS, D)), full((D, 256)), full((1, DSA_KV_RANK)),
                  full((DSA_KV_RANK, HEAD_DIM)), full((V_ROWS, DSA_KV_RANK))],
        out_specs=[pl.BlockSpec((nt, HEAD_DIM, N_HEADS * LANES), lambda i: (i, 0, 0)),
                   pl.BlockSpec((nt, IDX_DIM, IDX_HEADS * LANES), lambda i: (i, 0, 0)),
                   pl.BlockSpec((nt, IDX_HEADS, LANES), lambda i: (i, 0, 0)),
                   pl.BlockSpec((tm, HEAD_DIM), lambda i: (i, 0)),
                   pl.BlockSpec((tm, IDX_DIM), lambda i: (i, 0)),
                   pl.BlockSpec((nt, V_ROWS, LANES), lambda i: (i, 0, 0))],
        out_shape=[jax.ShapeDtypeStruct((N // LANES, HEAD_DIM, N_HEADS * LANES), BF16),
                   jax.ShapeDtypeStruct((N // LANES, IDX_DIM, IDX_HEADS * LANES), BF16),
                   jax.ShapeDtypeStruct((N // LANES, IDX_HEADS, LANES), F32),
                   jax.ShapeDtypeStruct((N, HEAD_DIM), BF16),
                   jax.ShapeDtypeStruct((N, IDX_DIM), BF16),
                   jax.ShapeDtypeStruct((N // LANES, V_ROWS, LANES), BF16)],
        compiler_params=_cparams("arbitrary"),
        name="dsa_prep",
    )(hb, w_t, w_n, kv_norm_g.reshape(1, DSA_KV_RANK).astype(F32), w_uk.astype(BF16), wuvt)


def _loop_groups(lo, hi, fn):
    n = jnp.maximum(hi - lo, 0)
    n4 = lax.shift_right_logical(n, 2)

    def body(j, c):
        fn([lo + 4 * j + u for u in range(4)])
        return c

    lax.fori_loop(0, n4, body, 0)
    rest = lo + 4 * n4

    @pl.when((n & 2) == 2)
    def _():
        fn([rest, rest + 1])

    @pl.when((n & 1) == 1)
    def _():
        fn([hi - 1])


def _dsa_kernel(qt_ref, qit_ref, wit_ref, k_ref, ki_ref, vt_ref, bias_ref, y_ref,
                sk_ref, m_ref, acc_ref, lg_ref, mg_ref, *, topk):
    i = pl.program_id(1)
    nk = i + 1
    QT = ROW_TILE
    HQ = N_HEADS * QT
    t_lane = i * QT + _iota((LANES, QT), 1)
    key_pos = lambda kt: kt * LANES + _iota((LANES, QT), 0)
    rows = lambda kt: pl.ds(pl.multiple_of(kt * LANES, LANES), LANES)
    per_head = lambda fn: jnp.concatenate([fn(slice(h * QT, (h + 1) * QT)) for h in range(N_HEADS)], axis=1)

    qit = qit_ref[0]
    wit = wit_ref[0]

    def score_tile(kt, edge):
        rel = jnp.dot(ki_ref[rows(kt), :], qit, preferred_element_type=F32)
        score = jnp.maximum(rel[:, 0:QT], 0.0) * wit[0:1, :]
        for h in range(1, IDX_HEADS):
            score = score + jnp.maximum(rel[:, h * QT:(h + 1) * QT], 0.0) * wit[h:h + 1, :]
        score = jnp.where(score == 0.0, 0.0, score)
        bits = lax.bitcast_convert_type(score, jnp.int32)
        key = jnp.where(bits < 0, bits ^ jnp.int32(0x7FFFFFFF), bits)
        if edge:
            s_pos = key_pos(kt)
            key = jnp.where(s_pos < FP + N_META, jnp.int32(KEY_INF), key)
            key = jnp.where((s_pos >= FP) & (s_pos <= t_lane), key, jnp.int32(INT_MIN))
        sk_ref[kt] = key

    score_tile(0, True)
    _loop_groups(1, i, lambda kts: [score_tile(kt, False) for kt in kts])

    @pl.when(i > 0)
    def _():
        score_tile(i, True)

    def count(pred_fn):
        def body(kt, acc):
            return acc + jnp.where(pred_fn(sk_ref[kt], kt), 1, 0)

        def body4(j, acc):
            for u in range(4):
                acc = body(4 * j + u, acc)
            return acc

        n4 = lax.shift_right_logical(nk, 2)
        acc = lax.fori_loop(0, n4, body4, jnp.zeros((LANES, QT), jnp.int32))
        acc = lax.fori_loop(4 * n4, nk, body, acc)
        return jnp.sum(acc, axis=0, keepdims=True)

    def bit_body(it, carry):
        tau, n_ge = carry
        cand = tau + jnp.left_shift(jnp.int32(1), 31 - it)
        cnt = count(lambda sk, kt: sk >= cand)
        return jnp.where(cnt >= topk, cand, tau), jnp.where(cnt >= topk, cnt, n_ge)

    tau, n_ge = lax.fori_loop(0, 32, bit_body, (jnp.full((1, QT), INT_MIN, jnp.int32),
                                                jnp.zeros((1, QT), jnp.int32)))
    tau = jnp.maximum(tau, jnp.int32(INT_MIN + 1))

    @pl.when(jnp.max(n_ge - topk) > 0)
    def _():
        n_bits = max(1, int(math.ceil(math.log2(sk_ref.shape[0] * LANES + 1))))
        need = topk - count(lambda sk, kt: sk > tau)

        def pos_body(it, x):
            cand = x + jnp.left_shift(jnp.int32(1), n_bits - 1 - it)
            cnt = count(lambda sk, kt: (sk == tau) & (key_pos(kt) < cand))
            return jnp.where(cnt < need, cand, x)

        x = lax.fori_loop(0, n_bits, pos_body, jnp.zeros((1, QT), jnp.int32))
        jmax = jnp.where(n_ge > topk, x, jnp.int32(2 ** 30))

        def drop_body(kt, c):
            sk = sk_ref[kt]
            sk_ref[kt] = jnp.where((sk == tau) & (key_pos(kt) > jmax), jnp.int32(INT_MIN), sk)
            return c

        lax.fori_loop(0, nk, drop_body, 0)

    qt = qt_ref[0]
    m_ref[...] = jnp.full((1, HQ), NEG, F32)
    acc_ref[...] = jnp.zeros((V_ROWS, HQ), F32)
    GW = lg_ref.shape[1]
    int_max = jnp.int32(2 ** 31 - 1)

    def park(g, slot):
        tmax = None
        for u in range(GW):
            kt = jnp.minimum(GW * g + u, i)
            tau_u = jnp.where(GW * g + u <= i, tau, int_max)
            lg = jnp.dot(k_ref[rows(kt), :], qt, preferred_element_type=F32) + bias_ref[jnp.minimum(i - kt, 2)]
            sel = sk_ref[kt] >= tau_u
            lgm = per_head(lambda hs: jnp.where(sel, lg[:, hs], NEG))
            lg_ref[slot, u] = lgm
            tmax = lgm if tmax is None else jnp.maximum(tmax, lgm)
        mg_ref[slot] = jnp.max(tmax, axis=0, keepdims=True)

    def weights(slot):
        m_old = m_ref[...]
        m_new = jnp.maximum(m_old, mg_ref[slot])
        m_ref[...] = m_new
        return jnp.exp(m_old - m_new), [jnp.exp(lg_ref[slot, u] - m_new).astype(BF16) for u in range(GW)]

    def fold(g, corr, prs):
        pv = None
        for u in range(GW):
            t = jnp.dot(vt_ref[jnp.minimum(GW * g + u, i)], prs[u], preferred_element_type=F32)
            pv = t if pv is None else pv + t
        acc_ref[...] = acc_ref[...] * corr + pv

    park(0, 0)

    def pipe_body(jj, c):
        corr, prs = weights(0)
        park(2 * jj + 1, 1)
        fold(2 * jj, corr, prs)
        corr, prs = weights(1)
        park(2 * jj + 2, 0)
        fold(2 * jj + 1, corr, prs)
        return c

    lax.fori_loop(0, lax.shift_right_logical(i + 2 * GW, 3), pipe_body, 0)
    acc = acc_ref[...]
    out = acc[0:HEAD_DIM, :] / jnp.maximum(acc[HEAD_DIM:HEAD_DIM + 1, :], 1e-30)
    y_ref[...] = per_head(lambda hs: out[:, hs].T).astype(y_ref.dtype)


def _t5_bucket(dist):
    max_exact = N_BUCKETS // 2
    n = jnp.maximum(dist, 0)
    large = max_exact + (jnp.log(jnp.maximum(n, 1).astype(F32) / max_exact)
                         / math.log(MAX_DISTANCE / max_exact) * (N_BUCKETS - max_exact)).astype(jnp.int32)
    return jnp.where(n < max_exact, n, jnp.minimum(large, N_BUCKETS - 1))


def _bias_tables(rel_bias):
    per_dist = rel_bias[_t5_bucket(jnp.arange(2 * ROW_TILE, dtype=jnp.int32))]
    q_minus_s = np.arange(ROW_TILE)[None, :] - np.arange(ROW_TILE)[:, None]
    far = per_dist[2 * ROW_TILE - 1]
    tabs = [per_dist[np.clip(r * ROW_TILE + q_minus_s, 0, 2 * ROW_TILE - 1)] - far for r in (0, 1)]
    tabs.append(jnp.zeros_like(tabs[0]))
    return jnp.stack(tabs).transpose(0, 1, 3, 2).reshape(3, ROW_TILE, N_HEADS * ROW_TILE).astype(F32)


def _dsa(qt, qit, wit, k, ki, vt, bias_tab, B, TP, topk):
    nq = TP // ROW_TILE
    return pl.pallas_call(
        functools.partial(_dsa_kernel, topk=topk),
        grid=(B, nq),
        in_specs=[pl.BlockSpec((1, HEAD_DIM, N_HEADS * LANES), lambda b, i: (b * nq + i, 0, 0)),
                  pl.BlockSpec((1, IDX_DIM, IDX_HEADS * LANES), lambda b, i: (b * nq + i, 0, 0)),
                  pl.BlockSpec((1, IDX_HEADS, LANES), lambda b, i: (b * nq + i, 0, 0)),
                  pl.BlockSpec((TP, HEAD_DIM), lambda b, i: (b, 0)),
                  pl.BlockSpec((TP, IDX_DIM), lambda b, i: (b, 0)),
                  pl.BlockSpec((nq, V_ROWS, LANES), lambda b, i: (b, 0, 0)),
                  pl.BlockSpec((3, ROW_TILE, N_HEADS * ROW_TILE), lambda b, i: (0, 0, 0))],
        out_specs=pl.BlockSpec((ROW_TILE, MIX_W), lambda b, i: (b * nq + i, 0)),
        out_shape=jax.ShapeDtypeStruct((B * TP, MIX_W), BF16),
        scratch_shapes=[pltpu.VMEM((nq, LANES, ROW_TILE), jnp.int32),
                        pltpu.VMEM((1, N_HEADS * ROW_TILE), F32),
                        pltpu.VMEM((V_ROWS, N_HEADS * ROW_TILE), F32),
                        pltpu.VMEM((2, 4, LANES, N_HEADS * ROW_TILE), F32),
                        pltpu.VMEM((2, 1, N_HEADS * ROW_TILE), F32)],
        compiler_params=_cparams("parallel", "arbitrary"),
        name="dsa_attend",
    )(qt, qit, wit, k, ki, vt, bias_tab)


def _layer_norm_rows(z, g, b):
    mu = jnp.mean(z, -1, keepdims=True)
    zc = z - mu
    var = jnp.mean(zc * zc, -1, keepdims=True)
    return zc * lax.rsqrt(var + LN_EPS) * g + b


def _merge_kernel(h_ref, g_ref, ya_ref, yb_ref, yc_ref, yd_ref, wb_ref, wo_ref, lg_ref, lb_ref,
                  h1_ref, h1b_ref):
    merged = None
    for i, y_ref in enumerate((ya_ref, yb_ref, yc_ref, yd_ref)):
        t = g_ref[:, i * D_MODEL:(i + 1) * D_MODEL] * jnp.dot(y_ref[...], wb_ref[i], preferred_element_type=F32)
        merged = t if merged is None else merged + t
    z = DN_ALPHA * h_ref[...] + jnp.dot(merged.astype(BF16), wo_ref[...], preferred_element_type=F32)
    y = _layer_norm_rows(z, lg_ref[...], lb_ref[...])
    h1_ref[...] = y
    h1b_ref[...] = y.astype(BF16)


def _merge(h, gates, ys, w_branch, w_out, ln_g, ln_b):
    N, D = h.shape
    tm = _pick_tile(N, 640)
    full = lambda shape: pl.BlockSpec(shape, lambda i: (0,) * len(shape))
    tok = lambda w: pl.BlockSpec((tm, w), lambda i: (i, 0))
    return pl.pallas_call(
        _merge_kernel,
        grid=(N // tm,),
        in_specs=[tok(D), tok(4 * D), tok(MIX_W), tok(MIX_W), tok(MIX_W), tok(MIX_W),
                  full((4, MIX_W, D)), full((D, D)), full((1, D)), full((1, D))],
        out_specs=[tok(D), tok(D)],
        out_shape=[jax.ShapeDtypeStruct((N, D), F32), jax.ShapeDtypeStruct((N, D), BF16)],
        compiler_params=_cparams("arbitrary"),
        name="merge_out_ln",
    )(h, gates, *ys, w_branch.astype(BF16), w_out.astype(BF16),
      ln_g.reshape(1, D).astype(F32), ln_b.reshape(1, D).astype(F32))


def _moe_kernel(h_ref, hb_ref, wr_ref, br_ref, wg_ref, wu_ref, wd_ref, lg_ref, lb_ref, o_ref, ob_ref,
                gate_ref, acc_ref):
    e = pl.program_id(1)
    xb = hb_ref[...]
    tm = xb.shape[0]
    lane = _iota((tm, LANES), 1)

    @pl.when(e == 0)
    def _():
        logit = jnp.dot(xb, wr_ref[...], preferred_element_type=F32) + br_ref[...]
        big = jnp.int32(LANES)
        gl = jnp.where(lane < N_GROUPS, logit, -jnp.inf)
        gmax = jnp.max(gl, axis=1, keepdims=True)
        g_sel = jnp.min(jnp.where(gl == gmax, lane, big), axis=1, keepdims=True)
        p_grp = 1.0 / jnp.sum(jnp.exp(gl - gmax), axis=1, keepdims=True)
        lo = N_GROUPS + g_sel * EPG
        el = jnp.where((lane >= lo) & (lane < lo + EPG), logit, -jnp.inf)
        v1 = jnp.max(el, axis=1, keepdims=True)
        i1 = jnp.min(jnp.where(el == v1, lane, big), axis=1, keepdims=True)
        el2 = jnp.where(lane == i1, -jnp.inf, el)
        v2 = jnp.max(el2, axis=1, keepdims=True)
        i2 = jnp.min(jnp.where(el2 == v2, lane, big), axis=1, keepdims=True)
        e2 = jnp.exp(v2 - v1)
        w1 = p_grp / (1.0 + e2)
        w2 = p_grp * e2 / (1.0 + e2)
        gate_ref[...] = jnp.where(lane == i1, w1, 0.0) + jnp.where(lane == i2, w2, 0.0)
        acc_ref[...] = jnp.zeros_like(acc_ref)

    g_e = jnp.sum(jnp.where(lane == e + N_GROUPS, gate_ref[...], 0.0), axis=1, keepdims=True)
    hid = _silu(jnp.dot(xb, wg_ref[0], preferred_element_type=F32)) * jnp.dot(xb, wu_ref[0], preferred_element_type=F32)
    acc_ref[...] += g_e * jnp.dot(hid.astype(BF16), wd_ref[0], preferred_element_type=F32)

    @pl.when(e == N_EXPERTS - 1)
    def _():
        y = _layer_norm_rows(DN_ALPHA * h_ref[...] + acc_ref[...], lg_ref[...], lb_ref[...])
        o_ref[...] = y
        ob_ref[...] = y.astype(BF16)


def _moe(h1, h1b, w_grp, b_grp, w_rt, b_rt, w_gate, w_up, w_down, ln_g, ln_b):
    N, D = h1.shape
    tm = _pick_tile(N, 1280)
    w_r = jnp.zeros((D, LANES), F32).at[:, 0:N_GROUPS].set(w_grp).at[:, N_GROUPS:N_GROUPS + N_EXPERTS].set(w_rt)
    b_r = jnp.zeros((1, LANES), F32).at[0, 0:N_GROUPS].set(b_grp).at[0, N_GROUPS:N_GROUPS + N_EXPERTS].set(b_rt)
    full = lambda shape: pl.BlockSpec(shape, lambda i, e: (0,) * len(shape))
    tok = lambda w: pl.BlockSpec((tm, w), lambda i, e: (i, 0))
    return pl.pallas_call(
        _moe_kernel,
        grid=(N // tm, N_EXPERTS),
        in_specs=[tok(D), tok(D), full((D, LANES)), full((1, LANES)),
                  pl.BlockSpec((1, D, D_EXPERT), lambda i, e: (e, 0, 0)),
                  pl.BlockSpec((1, D, D_EXPERT), lambda i, e: (e, 0, 0)),
                  pl.BlockSpec((1, D_EXPERT, D), lambda i, e: (e, 0, 0)),
                  full((1, D)), full((1, D))],
        out_specs=[tok(D), tok(D)],
        out_shape=[jax.ShapeDtypeStruct((N, D), F32), jax.ShapeDtypeStruct((N, D), BF16)],
        scratch_shapes=[pltpu.VMEM((tm, LANES), F32), pltpu.VMEM((tm, D), F32)],
        compiler_params=_cparams("arbitrary", "arbitrary"),
        name="hier_moe_ln",
    )(h1, h1b, w_r.astype(BF16), b_r, w_gate.astype(BF16), w_up.astype(BF16), w_down.astype(BF16),
      ln_g.reshape(1, D).astype(F32), ln_b.reshape(1, D).astype(F32))


def _pad_cols(w, width):
    return jnp.pad(w, ((0, 0), (0, width - w.shape[1])))


def _split_w_in(w):
    o = 0
    w_a = w[:, o:o + 1024]; o += 1024
    gq, gk, gv, ga, gg = (w[:, o:o + 128], w[:, o + 128:o + 256], w[:, o + 256:o + 512],
                          w[:, o + 512:o + 528], w[:, o + 528:o + 784]); o += 784
    w_b = _pad_cols(jnp.concatenate([gq, gk, gv, gg, ga], axis=1), 896)
    cq, ckv, cqi, cki, cwi = (w[:, o:o + 256], w[:, o + 256:o + 384], w[:, o + 384:o + 640],
                              w[:, o + 640:o + 672], w[:, o + 672:o + 680]); o += 680
    w_t = jnp.pad(jnp.concatenate([cq.T, cqi.T, cwi.T], axis=0), ((0, WT_ROWS - 2 * MIX_W - IDX_HEADS), (0, 0)))
    w_n = _pad_cols(jnp.concatenate([ckv, cki], axis=1), 256)
    dq, dk, dv, di, df, do = (w[:, o:o + 256], w[:, o + 256:o + 512], w[:, o + 512:o + 768],
                              w[:, o + 768:o + 772], w[:, o + 772:o + 776], w[:, o + 776:o + 1032]); o += 1032
    w_d = _pad_cols(jnp.concatenate([dq, dk, dv, do, di, df], axis=1), 1152)
    w_g = w[:, o:o + 4096]
    bf = lambda a: a.astype(BF16)
    return bf(w_a), bf(w_b), bf(w_t), bf(w_n), bf(w_d), bf(w_g)


def kernel(x, meta, ln_in_g, ln_in_b, rel_bias, w_in, rwkv_mu, rwkv_w_up, rwkv_w0, rwkv_a_up, rwkv_a0, rwkv_g_up, rwkv_k_k, rwkv_k_a, rwkv_r_k, rwkv_gn_g, rwkv_gn_b, gla_a_up, gla_a_b, gla_norm_g, dsa_kv_norm_g, dsa_w_uk, dsa_w_uv, mlstm_conv_w, mlstm_conv_b, mlstm_i_b, mlstm_f_b, mlstm_norm_g, w_branch, w_out, ln1_g, ln1_b, moe_w_grp, moe_b_grp, moe_w_rt, moe_b_rt, moe_w_gate, moe_w_up, moe_w_down, ln2_g, ln2_b):
    B, S, D = x.shape
    assert D == D_MODEL and S % ROW_TILE == 0
    TP = S + FRONT
    N = B * TP
    topk = min(TOPK_MAX, S // 4)
    bias_tab = _bias_tables(rel_bias)

    h, hb = _embed(x, meta, ln_in_g, ln_in_b)
    h = h.reshape(N, D)
    hb = hb.reshape(N, D)
    for l in range(DEPTH):
        w_a, w_b, w_t, w_n, w_d, w_g = _split_w_in(w_in[l])
        p_a = _proj(hb, w_a)
        p_b = _proj(hb, w_b)
        p_d = _proj(hb, w_d)
        gates = _proj(hb, w_g, act="sigmoid")
        qt, qit, wit, k, ki, vt = _dsa_prep(hb, w_t, w_n, dsa_kv_norm_g[l], dsa_w_uk[l], dsa_w_uv[l])
        y_a = _rwkv(p_a, B, TP, rwkv_mu[l], rwkv_w_up[l], rwkv_w0[l], rwkv_a_up[l], rwkv_a0[l], rwkv_g_up[l],
                    rwkv_k_k[l], rwkv_k_a[l], rwkv_r_k[l], rwkv_gn_g[l], rwkv_gn_b[l])
        y_b = _gla(p_b, B, TP, gla_a_up[l], gla_a_b[l], gla_norm_g[l])
        y_c = _dsa(qt, qit, wit, k, ki, vt, bias_tab, B, TP, topk)
        y_d = _mlstm(p_d, B, TP, mlstm_conv_w[l], mlstm_conv_b[l], mlstm_i_b[l], mlstm_f_b[l], mlstm_norm_g[l])
        ys = (y_a.reshape(N, MIX_W), y_b.reshape(N, MIX_W), y_c, y_d.reshape(N, MIX_W))
        h1, h1b = _merge(h, gates, ys, w_branch[l], w_out[l], ln1_g[l], ln1_b[l])
        h, hb = _moe(h1, h1b, moe_w_grp[l], moe_b_grp[l], moe_w_rt[l], moe_b_rt[l],
                     moe_w_gate[l], moe_w_up[l], moe_w_down[l], ln2_g[l], ln2_b[l])
    return h.reshape(B, TP, D)[:, FRONT:]
```

```python
import functools
import math

import numpy as np
import jax
import jax.numpy as jnp
from jax import lax
from jax.experimental import pallas as pl
from jax.experimental.pallas import tpu as pltpu

F32 = jnp.float32
BF16 = jnp.bfloat16

D_MODEL = 1024
HEAD_DIM = 64
N_HEADS = 4
MIX_W = 256
N_META = 16
CHUNK = 64
LANES = 128
ROW_TILE = 128
FRONT = ROW_TILE
FP = FRONT - N_META
NEG = -1e30
LN_EPS = 1e-5
DEPTH = 2
DN_ALPHA = (2 * DEPTH) ** 0.25

RWKV_GN_EPS = HEAD_DIM * 1e-5
GLA_DK = 32
GLA_TAU = 16.0
DSA_KV_RANK = 128
IDX_HEADS = 8
IDX_DIM = 32
TOPK_MAX = 256
N_BUCKETS = 32
MAX_DISTANCE = 128
CONV_W = 4
N_GROUPS = 4
EPG = 4
N_EXPERTS = 16
D_EXPERT = 256

INT_MIN = -(2 ** 31)
KEY_INF = 0x7F800000
VMEM_LIMIT = 56 * 1024 * 1024


def _cparams(*sem):
    return pltpu.CompilerParams(dimension_semantics=tuple(sem), vmem_limit_bytes=VMEM_LIMIT)


def _pick_tile(n, target):
    best = LANES
    t = LANES
    while t <= min(n, target):
        if n % t == 0:
            best = t
        t += LANES
    return best


def _bdot(a, b):
    return jnp.dot(a.astype(BF16), b.astype(BF16), preferred_element_type=F32)


def _bdot_nt(a, b):
    return lax.dot_general(a.astype(BF16), b.astype(BF16), (((1,), (1,)), ((), ())),
                           preferred_element_type=F32)


def _bdot_tn(a, b):
    return lax.dot_general(a.astype(BF16), b.astype(BF16), (((0,), (0,)), ((), ())),
                           preferred_element_type=F32)


def _split(a):
    hi = a.astype(BF16)
    lo = (a - hi.astype(F32)).astype(BF16)
    return hi, lo


_NN = (((1,), (0,)), ((), ()))
_NT = (((1,), (1,)), ((), ()))
_TN = (((0,), (0,)), ((), ()))


def _dot3(a, b, dims=_NN):
    ah, al = _split(a)
    bh, bl = _split(b)
    dg = lambda x, y: lax.dot_general(x, y, dims, preferred_element_type=F32)
    return dg(ah, bh) + (dg(ah, bl) + dg(al, bh))


def _dot_exact_lhs(a_bf16, b):
    bh, bl = _split(b)
    return (jnp.dot(a_bf16, bh, preferred_element_type=F32)
            + jnp.dot(a_bf16, bl, preferred_element_type=F32))


def _dot_exact_rhs(a, b_bf16):
    ah, al = _split(a)
    return (jnp.dot(ah, b_bf16, preferred_element_type=F32)
            + jnp.dot(al, b_bf16, preferred_element_type=F32))


def _sigmoid(x):
    return 1.0 / (1.0 + jnp.exp(-x))


def _log_sigmoid(x):
    return jnp.minimum(x, 0.0) - jnp.log(1.0 + jnp.exp(-jnp.abs(x)))


def _silu(x):
    return x * _sigmoid(x)


def _iota(shape, dim):
    return lax.broadcasted_iota(jnp.int32, shape, dim)


def _tri_incl(n):
    return (_iota((n, n), 1) <= _iota((n, n), 0))


def _head_ones():
    return ((_iota((MIX_W, MIX_W), 0) // HEAD_DIM) == (_iota((MIX_W, MIX_W), 1) // HEAD_DIM)).astype(BF16)


def _row_ids(rows):
    return pl.program_id(1) * ROW_TILE + _iota((rows, 1), 0)


def _embed_kernel(x_ref, meta_ref, g_ref, b_ref, h_ref, hb_ref):
    j = pl.program_id(1)
    src = jnp.where(j == 0, meta_ref[...], x_ref[0])
    mu = jnp.mean(src, -1, keepdims=True)
    xc = src - mu
    var = jnp.mean(xc * xc, -1, keepdims=True)
    y = xc * lax.rsqrt(var + LN_EPS) * g_ref[...] + b_ref[...]
    h_ref[0] = y
    hb_ref[0] = y.astype(BF16)


def _embed(x, meta, g, b):
    B, S, D = x.shape
    TP = S + FRONT
    meta_pad = jnp.concatenate([jnp.zeros((FP, D), F32), meta.astype(F32)], axis=0)
    return pl.pallas_call(
        _embed_kernel,
        grid=(B, TP // ROW_TILE),
        in_specs=[
            pl.BlockSpec((1, ROW_TILE, D), lambda b, j: (b, jnp.maximum(j - 1, 0), 0)),
            pl.BlockSpec((ROW_TILE, D), lambda b, j: (0, 0)),
            pl.BlockSpec((1, D), lambda b, j: (0, 0)),
            pl.BlockSpec((1, D), lambda b, j: (0, 0)),
        ],
        out_specs=[
            pl.BlockSpec((1, ROW_TILE, D), lambda b, j: (b, j, 0)),
            pl.BlockSpec((1, ROW_TILE, D), lambda b, j: (b, j, 0)),
        ],
        out_shape=[jax.ShapeDtypeStruct((B, TP, D), F32), jax.ShapeDtypeStruct((B, TP, D), BF16)],
        compiler_params=_cparams("parallel", "arbitrary"),
        name="embed_ln",
    )(x, meta_pad, g.reshape(1, D), b.reshape(1, D))


def _proj_kernel(h_ref, w_ref, o_ref, *, act):
    y = jnp.dot(h_ref[...], w_ref[...], preferred_element_type=F32)
    if act == "sigmoid":
        y = _sigmoid(y)
    o_ref[...] = y.astype(o_ref.dtype)


def _proj(hb, w, act=None, out_dtype=F32):
    N, D = hb.shape
    W = w.shape[1]
    tn = W if W <= 1152 else 1024
    tm = _pick_tile(N, 1280)
    return pl.pallas_call(
        functools.partial(_proj_kernel, act=act),
        grid=(W // tn, N // tm),
        in_specs=[pl.BlockSpec((tm, D), lambda j, i: (i, 0)),
                  pl.BlockSpec((D, tn), lambda j, i: (0, j))],
        out_specs=pl.BlockSpec((tm, tn), lambda j, i: (i, j)),
        out_shape=jax.ShapeDtypeStruct((N, W), out_dtype),
        compiler_params=_cparams("arbitrary", "arbitrary"),
        name="in_proj",
    )(hb, w)


def _rwkv_kernel(p_ref, mu_ref, wup_ref, w0_ref, aup_ref, a0_ref, gup_ref, kk_ref, ka_ref, rk_ref,
                 gng_ref, gnb_ref, y_ref, carry_ref, s_ref):
    j = pl.program_id(0)
    nb = p_ref.shape[0]
    n_chunks = ROW_TILE // CHUNK

    @pl.when(j == 0)
    def _():
        carry_ref[...] = jnp.zeros_like(carry_ref)
        s_ref[...] = jnp.zeros_like(s_ref)

    valid = (j * ROW_TILE + _iota((ROW_TILE, 1), 0)) >= FP
    first_row = _iota((ROW_TILE, 1), 0) == 0
    ones_h = _head_ones()
    tri = _tri_incl(CHUNK)
    tri_b = tri.astype(BF16)
    strict = _iota((CHUNK, CHUNK), 1) < _iota((CHUNK, CHUNK), 0)
    eye = (_iota((CHUNK, CHUNK), 1) == _iota((CHUNK, CHUNK), 0)).astype(F32)
    heads = [slice(h * HEAD_DIM, (h + 1) * HEAD_DIM) for h in range(N_HEADS)]

    pro = []
    unit = {}
    for b in range(nb):
        p = jnp.where(valid, p_ref[b], 0.0)
        prev = jnp.where(first_row, carry_ref[b], pltpu.roll(p, 1, 0))
        carry_ref[b] = p[ROW_TILE - 1:ROW_TILE, :]
        ps = p + (prev - p) * mu_ref[...]
        r = ps[:, 0:256]
        k = ps[:, 256:512]
        v = ps[:, 512:768]
        lora_in = ps[:, 768:896]
        xg = ps[:, 896:1024]
        w_log = _log_sigmoid(w0_ref[...] + _bdot(jnp.tanh(lora_in), wup_ref[...])) - 0.5
        lw = jnp.where(valid, -jnp.exp(w_log), 0.0)
        alpha = _sigmoid(a0_ref[...] + _bdot(lora_in, aup_ref[...]))
        gate = _bdot(_sigmoid(xg), gup_ref[...])
        kk = k * kk_ref[...]
        kk = kk / jnp.maximum(jnp.sqrt(_dot_exact_rhs(kk * kk, ones_h)), 1e-12)
        k = k * (1.0 + (alpha - 1.0) * ka_ref[...])
        kka = kk * alpha
        pro.append((r, k, v, gate))
        for c in range(n_chunks):
            sl = slice(c * CHUNK, (c + 1) * CHUNK)
            lw_c = lw[sl]
            cum = _dot_exact_lhs(tri_b, lw_c)
            cum_last = cum[CHUNK - 1:CHUNK, :]
            p_inv = jnp.exp(-cum)
            p_tail = jnp.exp(cum_last - cum)
            unit[b, c] = dict(a=-kk[sl] * jnp.exp(cum - lw_c), b=kka[sl] * p_inv, k=k[sl] * p_inv,
                              r=r[sl] * jnp.exp(cum), kb=k[sl] * p_tail, bb=kka[sl] * p_tail,
                              pl=jnp.exp(cum_last), v=v[sl])

    keys = [(b, c, h) for b in range(nb) for c in range(n_chunks) for h in range(N_HEADS)]
    part = lambda name, key: unit[key[0], key[1]][name][:, heads[key[2]]]
    a_ab = {q: jnp.where(strict, _dot3(part("a", q), part("b", q), _NT), 0.0) for q in keys}
    a_ak = {q: jnp.where(strict, _bdot_nt(part("a", q), part("k", q)), 0.0) for q in keys}
    a_rb = {q: jnp.where(tri, _bdot_nt(part("r", q), part("b", q)), 0.0) for q in keys}
    a_rk = {q: jnp.where(tri, _bdot_nt(part("r", q), part("k", q)), 0.0) for q in keys}
    inv = {q: eye + a_ab[q] for q in keys}
    pw = a_ab
    for _ in range(5):
        pw = {q: _bdot(pw[q], pw[q]) for q in keys}
        inv = {q: inv[q] + _bdot(inv[q], pw[q]) for q in keys}
    ak_v = {q: _bdot(a_ak[q], part("v", q)) for q in keys}
    rk_v = {q: _bdot(a_rk[q], part("v", q)) for q in keys}
    kb_v = {q: _bdot_tn(part("v", q), part("kb", q)) for q in keys}

    bh = [(b, h) for b in range(nb) for h in range(N_HEADS)]
    state = {q: s_ref[q[0], q[1]] for q in bh}
    y_parts = {}
    for c in range(n_chunks):
        full = lambda q: (q[0], c, q[1])
        a_s = {q: _bdot_nt(part("a", full(q)), state[q]) for q in bh}
        r_s = {q: _bdot_nt(part("r", full(q)), state[q]) for q in bh}
        u = {q: _bdot(inv[full(q)], a_s[q] + ak_v[full(q)]) for q in bh}
        for q in bh:
            y_parts[full(q)] = r_s[q] + rk_v[full(q)] + _bdot(a_rb[full(q)], u[q])
        state = {q: (state[q] * part("pl", full(q)) + kb_v[full(q)] + _bdot_tn(u[q], part("bb", full(q))))
                 for q in bh}
    for q in bh:
        s_ref[q[0], q[1]] = state[q]

    for b in range(nb):
        r, k, v, gate = pro[b]
        y = jnp.concatenate([jnp.concatenate([y_parts[b, c, h] for h in range(N_HEADS)], axis=1)
                             for c in range(n_chunks)], axis=0)
        mean = _dot_exact_rhs(y, ones_h) * (1.0 / HEAD_DIM)
        yc = y - mean
        var = _dot_exact_rhs(yc * yc, ones_h) * (1.0 / HEAD_DIM)
        yn = yc * lax.rsqrt(var + RWKV_GN_EPS) * gng_ref[...] + gnb_ref[...]
        bonus = _dot_exact_rhs(r * k * rk_ref[...], ones_h) * v
        y_ref[b] = ((yn + bonus) * gate).astype(y_ref.dtype)


def _rwkv(p_a, B, TP, mu, w_up, w0, a_up, a0, g_up, k_k, k_a, r_k, gn_g, gn_b):
    W = MIX_W
    z64 = jnp.zeros((64, W), F32)
    wup_pad = jnp.concatenate([w_up, z64], axis=0).astype(BF16)
    aup_pad = jnp.concatenate([z64, a_up], axis=0).astype(BF16)
    row = lambda a: a.reshape(1, -1).astype(F32)
    full = lambda shape: pl.BlockSpec(shape, lambda j: (0,) * len(shape))
    return pl.pallas_call(
        _rwkv_kernel,
        grid=(TP // ROW_TILE,),
        in_specs=[pl.BlockSpec((B, ROW_TILE, 1024), lambda j: (0, j, 0)),
                  full((1, 1024)), full((128, W)), full((1, W)), full((128, W)), full((1, W)),
                  full((128, W)), full((1, W)), full((1, W)), full((1, W)), full((1, W)), full((1, W))],
        out_specs=pl.BlockSpec((B, ROW_TILE, W), lambda j: (0, j, 0)),
        out_shape=jax.ShapeDtypeStruct((B, TP, W), BF16),
        scratch_shapes=[pltpu.VMEM((B, 1, 1024), F32), pltpu.VMEM((B, N_HEADS, HEAD_DIM, HEAD_DIM), F32)],
        compiler_params=_cparams("arbitrary"),
        name="rwkv7",
    )(p_a.reshape(B, TP, 1024), row(mu), wup_pad, row(w0), aup_pad, row(a0), g_up.astype(BF16),
      row(k_k), row(k_a), row(r_k), row(gn_g), row(gn_b))


def _gla_kernel(p_ref, aup_ref, ab_ref, ng_ref, y_ref, s_ref):
    j = pl.program_id(0)
    nb = p_ref.shape[0]
    n_chunks = ROW_TILE // CHUNK

    @pl.when(j == 0)
    def _():
        s_ref[...] = jnp.zeros_like(s_ref)

    valid = (j * ROW_TILE + _iota((ROW_TILE, 1), 0)) >= FP
    tri = _tri_incl(CHUNK)
    tri_b = tri.astype(BF16)

    og_all = []
    pre = {}
    for b in range(nb):
        p = jnp.where(valid, p_ref[b], 0.0)
        la = _log_sigmoid(_bdot(p[:, 768:896], aup_ref[...]) + ab_ref[...]) * (1.0 / GLA_TAU)
        la = jnp.where(valid, la, 0.0)
        og_all.append(p[:, 512:768])
        for c in range(n_chunks):
            sl = slice(c * CHUNK, (c + 1) * CHUNK)
            pre[b, c] = dict(q=p[sl, 0:128] * (GLA_DK ** -0.5), k=p[sl, 128:256], v=p[sl, 256:512], la=la[sl])
    bc = [(b, c) for b in range(nb) for c in range(n_chunks)]
    keys = [(b, c, h) for (b, c) in bc for h in range(N_HEADS)]
    ks = [slice(h * GLA_DK, (h + 1) * GLA_DK) for h in range(N_HEADS)]
    vs = [slice(h * HEAD_DIM, (h + 1) * HEAD_DIM) for h in range(N_HEADS)]
    b_cum = {u: _dot_exact_lhs(tri_b, pre[u]["la"]) for u in bc}
    b_last = {u: b_cum[u][CHUNK - 1:CHUNK, :] for u in bc}
    q_g = {u: pre[u]["q"] * jnp.exp(b_cum[u]) for u in bc}
    k_g = {u: pre[u]["k"] * jnp.exp(-b_cum[u]) for u in bc}
    k_l = {u: pre[u]["k"] * jnp.exp(b_last[u] - b_cum[u]) for u in bc}
    dec = {u: jnp.exp(b_last[u]) for u in bc}
    att = {u: jnp.where(tri, _bdot_nt(q_g[u[0], u[1]][:, ks[u[2]]], k_g[u[0], u[1]][:, ks[u[2]]]), 0.0)
           for u in keys}
    att_v = {u: _bdot(att[u], pre[u[0], u[1]]["v"][:, vs[u[2]]]) for u in keys}
    kl_v = {u: _bdot_tn(pre[u[0], u[1]]["v"][:, vs[u[2]]], k_l[u[0], u[1]][:, ks[u[2]]]) for u in keys}

    bh = [(b, h) for b in range(nb) for h in range(N_HEADS)]
    state = {q: s_ref[q[0], q[1]] for q in bh}
    o_parts = {}
    for c in range(n_chunks):
        for q in bh:
            o_parts[q[0], c, q[1]] = att_v[q[0], c, q[1]] + _bdot_nt(q_g[q[0], c][:, ks[q[1]]], state[q])
        state = {q: state[q] * dec[q[0], c][:, ks[q[1]]] + kl_v[q[0], c, q[1]] for q in bh}
    for q in bh:
        s_ref[q[0], q[1]] = state[q]

    ones_h = _head_ones()
    for b in range(nb):
        o = jnp.concatenate([jnp.concatenate([o_parts[b, c, h] for h in range(N_HEADS)], axis=1)
                             for c in range(n_chunks)], axis=0)
        ms = _dot_exact_rhs(o * o, ones_h) * (1.0 / HEAD_DIM)
        y = o * lax.rsqrt(ms + 1e-6) * ng_ref[...] * _silu(og_all[b])
        y_ref[b] = y.astype(y_ref.dtype)


def _gla(p_b, B, TP, a_up, a_b, norm_g):
    aup_pad = jnp.zeros((128, 128), F32).at[:a_up.shape[0]].set(a_up).astype(BF16)
    full = lambda shape: pl.BlockSpec(shape, lambda j: (0,) * len(shape))
    return pl.pallas_call(
        _gla_kernel,
        grid=(TP // ROW_TILE,),
        in_specs=[pl.BlockSpec((B, ROW_TILE, 896), lambda j: (0, j, 0)),
                  full((128, 128)), full((1, 128)), full((1, MIX_W))],
        out_specs=pl.BlockSpec((B, ROW_TILE, MIX_W), lambda j: (0, j, 0)),
        out_shape=jax.ShapeDtypeStruct((B, TP, MIX_W), BF16),
        scratch_shapes=[pltpu.VMEM((B, N_HEADS, HEAD_DIM, GLA_DK), F32)],
        compiler_params=_cparams("arbitrary"),
        name="gla",
    )(p_b.reshape(B, TP, 896), aup_pad, a_b.reshape(1, 128).astype(F32),
      jnp.tile(norm_g.astype(F32), N_HEADS).reshape(1, MIX_W))


def _mlstm_kernel(p_ref, cw_ref, cb_ref, ib_ref, fb_ref, ng_ref, y_ref, carry_ref, c_ref, n_ref, m_ref):
    j = pl.program_id(0)
    nb = p_ref.shape[0]
    n_chunks = ROW_TILE // CHUNK

    @pl.when(j == 0)
    def _():
        carry_ref[...] = jnp.zeros_like(carry_ref)
        c_ref[...] = jnp.zeros_like(c_ref)
        n_ref[...] = jnp.zeros_like(n_ref)
        m_ref[...] = jnp.zeros_like(m_ref)

    valid = (j * ROW_TILE + _iota((ROW_TILE, 1), 0)) >= FP
    tri = _tri_incl(CHUNK)
    tri_b = tri.astype(BF16)
    ones_h = _head_ones()

    og_all = []
    pre = {}
    for b in range(nb):
        p = jnp.where(valid, p_ref[b], 0.0)
        a = p[:, 0:512]
        ext = jnp.concatenate([carry_ref[b], a], axis=0)
        carry_ref[b] = a[ROW_TILE - 8:ROW_TILE, :]
        conv = cb_ref[...] + a * cw_ref[CONV_W - 1:CONV_W, :]
        for s in range(1, CONV_W):
            conv = conv + pltpu.roll(ext, s, 0)[8:8 + ROW_TILE, :] * cw_ref[CONV_W - 1 - s:CONV_W - s, :]
        qk = _silu(conv)
        q = jnp.where(valid, qk[:, 0:MIX_W], 0.0)
        k = jnp.where(valid, qk[:, MIX_W:2 * MIX_W], 0.0) * (HEAD_DIM ** -0.5)
        v = p[:, 512:768]
        og_all.append(p[:, 768:1024])
        gates = p[:, 1024:1152]
        li_all = jnp.where(valid, gates + ib_ref[...], NEG)
        lf_all = jnp.where(valid, _log_sigmoid(gates + fb_ref[...]), 0.0)
        for c in range(n_chunks):
            sl = slice(c * CHUNK, (c + 1) * CHUNK)
            pre[b, c] = dict(q=q[sl], k=k[sl], v=v[sl], li=li_all[sl], lf=lf_all[sl])

    bc = [(b, c) for b in range(nb) for c in range(n_chunks)]
    keys = [(b, c, h) for (b, c) in bc for h in range(N_HEADS)]
    heads = [slice(h * HEAD_DIM, (h + 1) * HEAD_DIM) for h in range(N_HEADS)]
    part = lambda name, u: pre[u[0], u[1]][name][:, heads[u[2]]]
    b_cum = {u: _dot_exact_lhs(tri_b, pre[u]["lf"]) for u in bc}
    b_t = {u: b_cum[u].T for u in bc}
    li_t = {u: pre[u]["li"].T for u in bc}
    b_col = {u: b_cum[u[0], u[1]][:, N_HEADS + u[2]:N_HEADS + u[2] + 1] for u in keys}
    b_last = {u: b_col[u][CHUNK - 1:CHUNK, :] for u in keys}
    d_log = {u: jnp.where(tri, b_col[u] - b_t[u[0], u[1]][N_HEADS + u[2]:N_HEADS + u[2] + 1, :]
                          + li_t[u[0], u[1]][u[2]:u[2] + 1, :], -jnp.inf) for u in keys}
    dmax = {u: jnp.max(d_log[u], axis=1, keepdims=True) for u in keys}
    qk = {u: _bdot_nt(part("q", u), part("k", u)) for u in keys}
    s0 = {u: jnp.exp(d_log[u] - dmax[u]) * qk[u] for u in keys}
    sv = {u: _bdot(s0[u], part("v", u)) for u in keys}
    ssum = {u: jnp.sum(s0[u], axis=1, keepdims=True) for u in keys}
    g_loc = {u: b_last[u] - b_col[u] + pre[u[0], u[1]]["li"][:, u[2]:u[2] + 1] for u in keys}
    m_loc = {u: jnp.max(g_loc[u], axis=0, keepdims=True) for u in keys}
    kw = {u: part("k", u) * jnp.exp(g_loc[u] - m_loc[u]) for u in keys}
    kwv = {u: _bdot_tn(kw[u], part("v", u)) for u in keys}
    kwsum = {u: jnp.sum(kw[u], axis=0, keepdims=True) for u in keys}

    bh = [(b, h) for b in range(nb) for h in range(N_HEADS)]
    c_st = {q: c_ref[q[0], q[1]] for q in bh}
    n_st = {q: n_ref[q[0], q[1]] for q in bh}
    m_st = {q: m_ref[q[0], q[1]] for q in bh}
    h_parts = {}
    for c in range(n_chunks):
        full = lambda q: (q[0], c, q[1])
        qc = {q: _bdot(part("q", full(q)), c_st[q]) for q in bh}
        qn = {q: jnp.sum(part("q", full(q)) * n_st[q], axis=1, keepdims=True) for q in bh}
        inter = {q: b_col[full(q)] + m_st[q] for q in bh}
        m_t = {q: jnp.maximum(inter[q], dmax[full(q)]) for q in bh}
        e_loc = {q: jnp.exp(dmax[full(q)] - m_t[q]) for q in bh}
        w_int = {q: jnp.exp(inter[q] - m_t[q]) for q in bh}
        for q in bh:
            num = e_loc[q] * sv[full(q)] + w_int[q] * qc[q]
            den = e_loc[q] * ssum[full(q)] + w_int[q] * qn[q]
            h_parts[full(q)] = num / jnp.maximum(jnp.abs(den), jnp.exp(-m_t[q]))
        m_new = {q: jnp.maximum(b_last[full(q)] + m_st[q], m_loc[full(q)]) for q in bh}
        s_old = {q: jnp.exp(b_last[full(q)] + m_st[q] - m_new[q]) for q in bh}
        s_new = {q: jnp.exp(m_loc[full(q)] - m_new[q]) for q in bh}
        c_st = {q: s_old[q] * c_st[q] + s_new[q] * kwv[full(q)] for q in bh}
        n_st = {q: s_old[q] * n_st[q] + s_new[q] * kwsum[full(q)] for q in bh}
        m_st = m_new
    for q in bh:
        c_ref[q[0], q[1]], n_ref[q[0], q[1]], m_ref[q[0], q[1]] = c_st[q], n_st[q], m_st[q]

    for b in range(nb):
        hh = jnp.concatenate([jnp.concatenate([h_parts[b, c, h] for h in range(N_HEADS)], axis=1)
                              for c in range(n_chunks)], axis=0) * _sigmoid(og_all[b])
        mean = _dot_exact_rhs(hh, ones_h) * (1.0 / HEAD_DIM)
        hc = hh - mean
        var = _dot_exact_rhs(hc * hc, ones_h) * (1.0 / HEAD_DIM)
        y_ref[b] = (hc * lax.rsqrt(var + 1e-5) * ng_ref[...]).astype(y_ref.dtype)


def _mlstm(p_d, B, TP, conv_w, conv_b, i_b, f_b, norm_g):
    ib = jnp.zeros((1, LANES), F32).at[0, 0:N_HEADS].set(i_b)
    fb = jnp.zeros((1, LANES), F32).at[0, N_HEADS:2 * N_HEADS].set(f_b)
    full = lambda shape: pl.BlockSpec(shape, lambda j: (0,) * len(shape))
    return pl.pallas_call(
        _mlstm_kernel,
        grid=(TP // ROW_TILE,),
        in_specs=[pl.BlockSpec((B, ROW_TILE, 1152), lambda j: (0, j, 0)),
                  full((CONV_W, 512)), full((1, 512)), full((1, LANES)), full((1, LANES)), full((1, MIX_W))],
        out_specs=pl.BlockSpec((B, ROW_TILE, MIX_W), lambda j: (0, j, 0)),
        out_shape=jax.ShapeDtypeStruct((B, TP, MIX_W), BF16),
        scratch_shapes=[pltpu.VMEM((B, 8, 512), F32),
                        pltpu.VMEM((B, N_HEADS, HEAD_DIM, HEAD_DIM), F32),
                        pltpu.VMEM((B, N_HEADS, 1, HEAD_DIM), F32),
                        pltpu.VMEM((B, N_HEADS, 1, 1), F32)],
        compiler_params=_cparams("arbitrary"),
        name="mlstm",
    )(p_d.reshape(B, TP, 1152), conv_w.astype(F32), conv_b.reshape(1, 512).astype(F32), ib, fb,
      norm_g.reshape(1, MIX_W).astype(F32))


V_ROWS = 80
WT_ROWS = 528


def _dsa_prep_kernel(h_ref, wt_ref, wn_ref, kvg_ref, wuk_ref, wuvt_ref,
                     qt_ref, qit_ref, wit_ref, k_ref, ki_ref, vt_ref):
    hb = h_ref[...]
    tm = hb.shape[0]
    pt = lax.dot_general(wt_ref[...], hb, _NT, preferred_element_type=F32)
    pn = jnp.dot(hb, wn_ref[...], preferred_element_type=F32)
    ckv = pn[:, 0:DSA_KV_RANK]
    c = ckv * lax.rsqrt(jnp.mean(ckv * ckv, -1, keepdims=True) + 1e-6) * kvg_ref[...]
    cb = c.astype(BF16)
    k_ref[...] = jnp.dot(cb, wuk_ref[...], preferred_element_type=F32).astype(BF16)
    ki_ref[...] = pn[:, DSA_KV_RANK:DSA_KV_RANK + IDX_DIM].astype(BF16)
    vt = lax.dot_general(wuvt_ref[...], cb, _NT, preferred_element_type=F32)
    vt = jnp.where(_iota((V_ROWS, tm), 0) == HEAD_DIM, 1.0, vt)
    for t in range(tm // LANES):
        cs = slice(t * LANES, (t + 1) * LANES)
        for h in range(N_HEADS):
            qt_ref[t, :, h * LANES:(h + 1) * LANES] = (
                pt[h * HEAD_DIM:(h + 1) * HEAD_DIM, cs] * (HEAD_DIM ** -0.5)).astype(BF16)
        for h in range(IDX_HEADS):
            qit_ref[t, :, h * LANES:(h + 1) * LANES] = pt[MIX_W + h * IDX_DIM:MIX_W + (h + 1) * IDX_DIM, cs].astype(BF16)
        wit_ref[t] = pt[2 * MIX_W:2 * MIX_W + IDX_HEADS, cs] * ((IDX_HEADS * IDX_DIM) ** -0.5)
        vt_ref[t] = vt[:, cs].astype(BF16)


def _dsa_prep(hb, w_t, w_n, kv_norm_g, w_uk, w_uv):
    N, D = hb.shape
    tm = _pick_tile(N, 640)
    nt = tm // LANES
    full = lambda shape: pl.BlockSpec(shape, lambda i: (0,) * len(shape))
    wuvt = jnp.pad(w_uv.T, ((0, V_ROWS - HEAD_DIM), (0, 0))).astype(BF16)
    return pl.pallas_call(
        _dsa_prep_kernel,
        grid=(N // tm,),
        in_specs=[pl.BlockSpec((tm, D), lambda i: (i, 0)),
                  full((WT_ROWS, D)), full((D, 256)), full((1, DSA_KV_RANK)),
                  full((DSA_KV_RANK, HEAD_DIM)), full((V_ROWS, DSA_KV_RANK))],
        out_specs=[pl.BlockSpec((nt, HEAD_DIM, N_HEADS * LANES), lambda i: (i, 0, 0)),
                   pl.BlockSpec((nt, IDX_DIM, IDX_HEADS * LANES), lambda i: (i, 0, 0)),
                   pl.BlockSpec((nt, IDX_HEADS, LANES), lambda i: (i, 0, 0)),
                   pl.BlockSpec((tm, HEAD_DIM), lambda i: (i, 0)),
                   pl.BlockSpec((tm, IDX_DIM), lambda i: (i, 0)),
                   pl.BlockSpec((nt, V_ROWS, LANES), lambda i: (i, 0, 0))],
        out_shape=[jax.ShapeDtypeStruct((N // LANES, HEAD_DIM, N_HEADS * LANES), BF16),
                   jax.ShapeDtypeStruct((N // LANES, IDX_DIM, IDX_HEADS * LANES), BF16),
                   jax.ShapeDtypeStruct((N // LANES, IDX_HEADS, LANES), F32),
                   jax.ShapeDtypeStruct((N, HEAD_DIM), BF16),
                   jax.ShapeDtypeStruct((N, IDX_DIM), BF16),
                   jax.ShapeDtypeStruct((N // LANES, V_ROWS, LANES), BF16)],
        compiler_params=_cparams("arbitrary"),
        name="dsa_prep",
    )(hb, w_t, w_n, kv_norm_g.reshape(1, DSA_KV_RANK).astype(F32), w_uk.astype(BF16), wuvt)


def _dsa_kernel(qt_ref, qit_ref, wit_ref, k_ref, ki_ref, vt_ref, bias_ref, y_ref,
                sk_ref, rel_ref, m_ref, acc_ref, lg_ref, mg_ref, *, topk):
    i = pl.program_id(1)
    nk = i + 1
    QT = ROW_TILE
    HQ = N_HEADS * QT
    t_lane = i * QT + _iota((LANES, QT), 1)
    key_pos = lambda kt: kt * LANES + _iota((LANES, QT), 0)
    rows = lambda kt: pl.ds(pl.multiple_of(kt * LANES, LANES), LANES)
    per_head = lambda fn: jnp.concatenate([fn(slice(h * QT, (h + 1) * QT)) for h in range(N_HEADS)], axis=1)

    qit = qit_ref[0]
    wit = wit_ref[0]

    GW = rel_ref.shape[1] // LANES
    n_tiles = sk_ref.shape[0] - 1
    n_trips = (i + 2 * GW) // (2 * GW)

    def group_base(g):
        return jnp.clip(GW * g, 0, n_tiles - GW)

    def issue(g, slot):
        span = pl.ds(pl.multiple_of(group_base(g) * LANES, LANES), GW * LANES)
        rel_ref[slot] = jnp.dot(ki_ref[span, :], qit, preferred_element_type=F32)

    def reduce(g, slot):
        for u in range(GW):
            kt = group_base(g) + u
            rows_u = slice(u * LANES, (u + 1) * LANES)
            score = jnp.maximum(rel_ref[slot, rows_u, 0:QT], 0.0) * wit[0:1, :]
            for h in range(1, IDX_HEADS):
                score = score + jnp.maximum(rel_ref[slot, rows_u, h * QT:(h + 1) * QT], 0.0) * wit[h:h + 1, :]
            score = jnp.where(score == 0.0, 0.0, score)
            bits = lax.bitcast_convert_type(score, jnp.int32)
            key = jnp.where(bits < 0, bits ^ jnp.int32(0x7FFFFFFF), bits)
            s_pos = key_pos(kt)
            key = jnp.where(s_pos < FP + N_META, jnp.int32(KEY_INF), key)
            key = jnp.where((s_pos >= FP) & (s_pos <= t_lane), key, jnp.int32(INT_MIN))
            mine = (kt >= GW * g) & (kt <= i)
            sk_ref[jnp.where(mine, kt, n_tiles)] = key

    issue(0, 0)

    def score_body(jj, c):
        issue(2 * jj + 1, 1)
        reduce(2 * jj, 0)
        issue(2 * jj + 2, 0)
        reduce(2 * jj + 1, 1)
        return c

    lax.fori_loop(0, n_trips, score_body, 0)

    def count(pred_fn):
        def body(kt, acc):
            return acc + jnp.where(pred_fn(sk_ref[kt], kt), 1, 0)

        def body4(j, acc):
            for u in range(4):
                acc = body(4 * j + u, acc)
            return acc

        n4 = lax.shift_right_logical(nk, 2)
        acc = lax.fori_loop(0, n4, body4, jnp.zeros((LANES, QT), jnp.int32))
        acc = lax.fori_loop(4 * n4, nk, body, acc)
        return jnp.sum(acc, axis=0, keepdims=True)

    def bit_body(it, carry):
        tau, n_ge = carry
        cand = tau + jnp.left_shift(jnp.int32(1), 31 - it)
        cnt = count(lambda sk, kt: sk >= cand)
        return jnp.where(cnt >= topk, cand, tau), jnp.where(cnt >= topk, cnt, n_ge)

    tau, n_ge = lax.fori_loop(0, 32, bit_body, (jnp.full((1, QT), INT_MIN, jnp.int32),
                                                jnp.zeros((1, QT), jnp.int32)))
    tau = jnp.maximum(tau, jnp.int32(INT_MIN + 1))

    @pl.when(jnp.max(n_ge - topk) > 0)
    def _():
        n_bits = max(1, int(math.ceil(math.log2(sk_ref.shape[0] * LANES + 1))))
        need = topk - count(lambda sk, kt: sk > tau)

        def pos_body(it, x):
            cand = x + jnp.left_shift(jnp.int32(1), n_bits - 1 - it)
            cnt = count(lambda sk, kt: (sk == tau) & (key_pos(kt) < cand))
            return jnp.where(cnt < need, cand, x)

        x = lax.fori_loop(0, n_bits, pos_body, jnp.zeros((1, QT), jnp.int32))
        jmax = jnp.where(n_ge > topk, x, jnp.int32(2 ** 30))

        def drop_body(kt, c):
            sk = sk_ref[kt]
            sk_ref[kt] = jnp.where((sk == tau) & (key_pos(kt) > jmax), jnp.int32(INT_MIN), sk)
            return c

        lax.fori_loop(0, nk, drop_body, 0)

    qt = qt_ref[0]
    m_ref[...] = jnp.full((1, HQ), NEG, F32)
    acc_ref[...] = jnp.zeros((V_ROWS, HQ), F32)
    int_max = jnp.int32(2 ** 31 - 1)

    def park(g, slot):
        base = group_base(g)
        span = pl.ds(pl.multiple_of(base * LANES, LANES), GW * LANES)
        lg_all = jnp.dot(k_ref[span, :], qt, preferred_element_type=F32)
        tmax = None
        for u in range(GW):
            t = base + u
            mine = (t >= GW * g) & (t <= i)
            tau_u = jnp.where(mine, tau, int_max)
            kt = jnp.minimum(t, i)
            lg = lg_all[u * LANES:(u + 1) * LANES, :] + bias_ref[jnp.clip(i - t, 0, 2)]
            sel = sk_ref[kt] >= tau_u
            lgm = per_head(lambda hs: jnp.where(sel, lg[:, hs], NEG))
            lg_ref[slot, u] = lgm
            tmax = lgm if tmax is None else jnp.maximum(tmax, lgm)
        mg_ref[slot] = jnp.max(tmax, axis=0, keepdims=True)

    def weights(slot):
        m_old = m_ref[...]
        m_new = jnp.maximum(m_old, mg_ref[slot])
        m_ref[...] = m_new
        return jnp.exp(m_old - m_new), [jnp.exp(lg_ref[slot, u] - m_new).astype(BF16) for u in range(GW)]

    def fold(g, corr, prs):
        vt_all = jnp.concatenate([vt_ref[group_base(g) + u] for u in range(GW)], axis=1)
        pv = jnp.dot(vt_all, jnp.concatenate(prs, axis=0), preferred_element_type=F32)
        acc_ref[...] = acc_ref[...] * corr + pv

    park(0, 0)

    def pipe_body(jj, c):
        corr, prs = weights(0)
        park(2 * jj + 1, 1)
        fold(2 * jj, corr, prs)
        corr, prs = weights(1)
        park(2 * jj + 2, 0)
        fold(2 * jj + 1, corr, prs)
        return c

    lax.fori_loop(0, n_trips, pipe_body, 0)
    acc = acc_ref[...]
    out = acc[0:HEAD_DIM, :] / jnp.maximum(acc[HEAD_DIM:HEAD_DIM + 1, :], 1e-30)
    y_ref[...] = per_head(lambda hs: out[:, hs].T).astype(y_ref.dtype)


def _t5_bucket(dist):
    max_exact = N_BUCKETS // 2
    n = jnp.maximum(dist, 0)
    large = max_exact + (jnp.log(jnp.maximum(n, 1).astype(F32) / max_exact)
                         / math.log(MAX_DISTANCE / max_exact) * (N_BUCKETS - max_exact)).astype(jnp.int32)
    return jnp.where(n < max_exact, n, jnp.minimum(large, N_BUCKETS - 1))


def _bias_tables(rel_bias):
    per_dist = rel_bias[_t5_bucket(jnp.arange(2 * ROW_TILE, dtype=jnp.int32))]
    q_minus_s = np.arange(ROW_TILE)[None, :] - np.arange(ROW_TILE)[:, None]
    far = per_dist[2 * ROW_TILE - 1]
    tabs = [per_dist[np.clip(r * ROW_TILE + q_minus_s, 0, 2 * ROW_TILE - 1)] - far for r in (0, 1)]
    tabs.append(jnp.zeros_like(tabs[0]))
    return jnp.stack(tabs).transpose(0, 1, 3, 2).reshape(3, ROW_TILE, N_HEADS * ROW_TILE).astype(F32)


def _dsa(qt, qit, wit, k, ki, vt, bias_tab, B, TP, topk):
    nq = TP // ROW_TILE
    return pl.pallas_call(
        functools.partial(_dsa_kernel, topk=topk),
        grid=(B, nq),
        in_specs=[pl.BlockSpec((1, HEAD_DIM, N_HEADS * LANES), lambda b, i: (b * nq + i, 0, 0)),
                  pl.BlockSpec((1, IDX_DIM, IDX_HEADS * LANES), lambda b, i: (b * nq + i, 0, 0)),
                  pl.BlockSpec((1, IDX_HEADS, LANES), lambda b, i: (b * nq + i, 0, 0)),
                  pl.BlockSpec((TP, HEAD_DIM), lambda b, i: (b, 0)),
                  pl.BlockSpec((TP, IDX_DIM), lambda b, i: (b, 0)),
                  pl.BlockSpec((nq, V_ROWS, LANES), lambda b, i: (b, 0, 0)),
                  pl.BlockSpec((3, ROW_TILE, N_HEADS * ROW_TILE), lambda b, i: (0, 0, 0))],
        out_specs=pl.BlockSpec((ROW_TILE, MIX_W), lambda b, i: (b * nq + i, 0)),
        out_shape=jax.ShapeDtypeStruct((B * TP, MIX_W), BF16),
        scratch_shapes=[pltpu.VMEM((nq + 1, LANES, ROW_TILE), jnp.int32),
                        pltpu.VMEM((2, min(4, nq) * LANES, IDX_HEADS * ROW_TILE), F32),
                        pltpu.VMEM((1, N_HEADS * ROW_TILE), F32),
                        pltpu.VMEM((V_ROWS, N_HEADS * ROW_TILE), F32),
                        pltpu.VMEM((2, min(4, nq), LANES, N_HEADS * ROW_TILE), F32),
                        pltpu.VMEM((2, 1, N_HEADS * ROW_TILE), F32)],
        compiler_params=_cparams("parallel", "arbitrary"),
        name="dsa_attend",
    )(qt, qit, wit, k, ki, vt, bias_tab)


def _layer_norm_rows(z, g, b):
    mu = jnp.mean(z, -1, keepdims=True)
    zc = z - mu
    var = jnp.mean(zc * zc, -1, keepdims=True)
    return zc * lax.rsqrt(var + LN_EPS) * g + b


def _merge_kernel(h_ref, g_ref, ya_ref, yb_ref, yc_ref, yd_ref, wb_ref, wo_ref, lg_ref, lb_ref,
                  h1_ref, h1b_ref):
    merged = None
    for i, y_ref in enumerate((ya_ref, yb_ref, yc_ref, yd_ref)):
        t = g_ref[:, i * D_MODEL:(i + 1) * D_MODEL] * jnp.dot(y_ref[...], wb_ref[i], preferred_element_type=F32)
        merged = t if merged is None else merged + t
    z = DN_ALPHA * h_ref[...] + jnp.dot(merged.astype(BF16), wo_ref[...], preferred_element_type=F32)
    y = _layer_norm_rows(z, lg_ref[...], lb_ref[...])
    h1_ref[...] = y
    h1b_ref[...] = y.astype(BF16)


def _merge(h, gates, ys, w_branch, w_out, ln_g, ln_b):
    N, D = h.shape
    tm = _pick_tile(N, 640)
    full = lambda shape: pl.BlockSpec(shape, lambda i: (0,) * len(shape))
    tok = lambda w: pl.BlockSpec((tm, w), lambda i: (i, 0))
    return pl.pallas_call(
        _merge_kernel,
        grid=(N // tm,),
        in_specs=[tok(D), tok(4 * D), tok(MIX_W), tok(MIX_W), tok(MIX_W), tok(MIX_W),
                  full((4, MIX_W, D)), full((D, D)), full((1, D)), full((1, D))],
        out_specs=[tok(D), tok(D)],
        out_shape=[jax.ShapeDtypeStruct((N, D), F32), jax.ShapeDtypeStruct((N, D), BF16)],
        compiler_params=_cparams("arbitrary"),
        name="merge_out_ln",
    )(h, gates, *ys, w_branch.astype(BF16), w_out.astype(BF16),
      ln_g.reshape(1, D).astype(F32), ln_b.reshape(1, D).astype(F32))


def _moe_kernel(h_ref, hb_ref, wr_ref, br_ref, wg_ref, wu_ref, wd_ref, lg_ref, lb_ref, o_ref, ob_ref,
                gate_ref, acc_ref):
    e = pl.program_id(1)
    xb = hb_ref[...]
    tm = xb.shape[0]
    lane = _iota((tm, LANES), 1)

    @pl.when(e == 0)
    def _():
        logit = jnp.dot(xb, wr_ref[...], preferred_element_type=F32) + br_ref[...]
        big = jnp.int32(LANES)
        gl = jnp.where(lane < N_GROUPS, logit, -jnp.inf)
        gmax = jnp.max(gl, axis=1, keepdims=True)
        g_sel = jnp.min(jnp.where(gl == gmax, lane, big), axis=1, keepdims=True)
        p_grp = 1.0 / jnp.sum(jnp.exp(gl - gmax), axis=1, keepdims=True)
        lo = N_GROUPS + g_sel * EPG
        el = jnp.where((lane >= lo) & (lane < lo + EPG), logit, -jnp.inf)
        v1 = jnp.max(el, axis=1, keepdims=True)
        i1 = jnp.min(jnp.where(el == v1, lane, big), axis=1, keepdims=True)
        el2 = jnp.where(lane == i1, -jnp.inf, el)
        v2 = jnp.max(el2, axis=1, keepdims=True)
        i2 = jnp.min(jnp.where(el2 == v2, lane, big), axis=1, keepdims=True)
        e2 = jnp.exp(v2 - v1)
        w1 = p_grp / (1.0 + e2)
        w2 = p_grp * e2 / (1.0 + e2)
        gate_ref[...] = jnp.where(lane == i1, w1, 0.0) + jnp.where(lane == i2, w2, 0.0)
        acc_ref[...] = jnp.zeros_like(acc_ref)

    g_e = jnp.sum(jnp.where(lane == e + N_GROUPS, gate_ref[...], 0.0), axis=1, keepdims=True)
    hid = _silu(jnp.dot(xb, wg_ref[0], preferred_element_type=F32)) * jnp.dot(xb, wu_ref[0], preferred_element_type=F32)
    acc_ref[...] += g_e * jnp.dot(hid.astype(BF16), wd_ref[0], preferred_element_type=F32)

    @pl.when(e == N_EXPERTS - 1)
    def _():
        y = _layer_norm_rows(DN_ALPHA * h_ref[...] + acc_ref[...], lg_ref[...], lb_ref[...])
        o_ref[...] = y
        ob_ref[...] = y.astype(BF16)


def _moe(h1, h1b, w_grp, b_grp, w_rt, b_rt, w_gate, w_up, w_down, ln_g, ln_b):
    N, D = h1.shape
    tm = _pick_tile(N, 1280)
    w_r = jnp.zeros((D, LANES), F32).at[:, 0:N_GROUPS].set(w_grp).at[:, N_GROUPS:N_GROUPS + N_EXPERTS].set(w_rt)
    b_r = jnp.zeros((1, LANES), F32).at[0, 0:N_GROUPS].set(b_grp).at[0, N_GROUPS:N_GROUPS + N_EXPERTS].set(b_rt)
    full = lambda shape: pl.BlockSpec(shape, lambda i, e: (0,) * len(shape))
    tok = lambda w: pl.BlockSpec((tm, w), lambda i, e: (i, 0))
    return pl.pallas_call(
        _moe_kernel,
        grid=(N // tm, N_EXPERTS),
        in_specs=[tok(D), tok(D), full((D, LANES)), full((1, LANES)),
                  pl.BlockSpec((1, D, D_EXPERT), lambda i, e: (e, 0, 0)),
                  pl.BlockSpec((1, D, D_EXPERT), lambda i, e: (e, 0, 0)),
                  pl.BlockSpec((1, D_EXPERT, D), lambda i, e: (e, 0, 0)),
                  full((1, D)), full((1, D))],
        out_specs=[tok(D), tok(D)],
        out_shape=[jax.ShapeDtypeStruct((N, D), F32), jax.ShapeDtypeStruct((N, D), BF16)],
        scratch_shapes=[pltpu.VMEM((tm, LANES), F32), pltpu.VMEM((tm, D), F32)],
        compiler_params=_cparams("arbitrary", "arbitrary"),
        name="hier_moe_ln",
    )(h1, h1b, w_r.astype(BF16), b_r, w_gate.astype(BF16), w_up.astype(BF16), w_down.astype(BF16),
      ln_g.reshape(1, D).astype(F32), ln_b.reshape(1, D).astype(F32))


def _pad_cols(w, width):
    return jnp.pad(w, ((0, 0), (0, width - w.shape[1])))


def _split_w_in(w):
    o = 0
    w_a = w[:, o:o + 1024]; o += 1024
    gq, gk, gv, ga, gg = (w[:, o:o + 128], w[:, o + 128:o + 256], w[:, o + 256:o + 512],
                          w[:, o + 512:o + 528], w[:, o + 528:o + 784]); o += 784
    w_b = _pad_cols(jnp.concatenate([gq, gk, gv, gg, ga], axis=1), 896)
    cq, ckv, cqi, cki, cwi = (w[:, o:o + 256], w[:, o + 256:o + 384], w[:, o + 384:o + 640],
                              w[:, o + 640:o + 672], w[:, o + 672:o + 680]); o += 680
    w_t = jnp.pad(jnp.concatenate([cq.T, cqi.T, cwi.T], axis=0), ((0, WT_ROWS - 2 * MIX_W - IDX_HEADS), (0, 0)))
    w_n = _pad_cols(jnp.concatenate([ckv, cki], axis=1), 256)
    dq, dk, dv, di, df, do = (w[:, o:o + 256], w[:, o + 256:o + 512], w[:, o + 512:o + 768],
                              w[:, o + 768:o + 772], w[:, o + 772:o + 776], w[:, o + 776:o + 1032]); o += 1032
    w_d = _pad_cols(jnp.concatenate([dq, dk, dv, do, di, df], axis=1), 1152)
    w_g = w[:, o:o + 4096]
    bf = lambda a: a.astype(BF16)
    return bf(w_a), bf(w_b), bf(w_t), bf(w_n), bf(w_d), bf(w_g)


def kernel(x, meta, ln_in_g, ln_in_b, rel_bias, w_in, rwkv_mu, rwkv_w_up, rwkv_w0, rwkv_a_up, rwkv_a0, rwkv_g_up, rwkv_k_k, rwkv_k_a, rwkv_r_k, rwkv_gn_g, rwkv_gn_b, gla_a_up, gla_a_b, gla_norm_g, dsa_kv_norm_g, dsa_w_uk, dsa_w_uv, mlstm_conv_w, mlstm_conv_b, mlstm_i_b, mlstm_f_b, mlstm_norm_g, w_branch, w_out, ln1_g, ln1_b, moe_w_grp, moe_b_grp, moe_w_rt, moe_b_rt, moe_w_gate, moe_w_up, moe_w_down, ln2_g, ln2_b):
    B, S, D = x.shape
    assert D == D_MODEL and S % ROW_TILE == 0
    TP = S + FRONT
    N = B * TP
    topk = min(TOPK_MAX, S // 4)
    bias_tab = _bias_tables(rel_bias)

    h, hb = _embed(x, meta, ln_in_g, ln_in_b)
    h = h.reshape(N, D)
    hb = hb.reshape(N, D)
    for l in range(DEPTH):
        w_a, w_b, w_t, w_n, w_d, w_g = _split_w_in(w_in[l])
        p_a = _proj(hb, w_a)
        p_b = _proj(hb, w_b)
        p_d = _proj(hb, w_d)
        gates = _proj(hb, w_g, act="sigmoid")
        qt, qit, wit, k, ki, vt = _dsa_prep(hb, w_t, w_n, dsa_kv_norm_g[l], dsa_w_uk[l], dsa_w_uv[l])
        y_a = _rwkv(p_a, B, TP, rwkv_mu[l], rwkv_w_up[l], rwkv_w0[l], rwkv_a_up[l], rwkv_a0[l], rwkv_g_up[l],
                    rwkv_k_k[l], rwkv_k_a[l], rwkv_r_k[l], rwkv_gn_g[l], rwkv_gn_b[l])
        y_b = _gla(p_b, B, TP, gla_a_up[l], gla_a_b[l], gla_norm_g[l])
        y_c = _dsa(qt, qit, wit, k, ki, vt, bias_tab, B, TP, topk)
        y_d = _mlstm(p_d, B, TP, mlstm_conv_w[l], mlstm_conv_b[l], mlstm_i_b[l], mlstm_f_b[l], mlstm_norm_g[l])
        ys = (y_a.reshape(N, MIX_W), y_b.reshape(N, MIX_W), y_c, y_d.reshape(N, MIX_W))
        h1, h1b = _merge(h, gates, ys, w_branch[l], w_out[l], ln1_g[l], ln1_b[l])
        h, hb = _moe(h1, h1b, moe_w_grp[l], moe_b_grp[l], moe_w_rt[l], moe_b_rt[l],
                     moe_w_gate[l], moe_w_up[l], moe_w_down[l], ln2_g[l], ln2_b[l])
    return h.reshape(B, TP, D)[:, FRONT:]
```

```python
import functools
import math

import numpy as np
import jax
import jax.numpy as jnp
from jax import lax
from jax.experimental import pallas as pl
from jax.experimental.pallas import tpu as pltpu

F32 = jnp.float32
BF16 = jnp.bfloat16

D_MODEL = 1024
HEAD_DIM = 64
N_HEADS = 4
MIX_W = 256
N_META = 16
CHUNK = 64
LANES = 128
ROW_TILE = 128
FRONT = ROW_TILE
FP = FRONT - N_META
NEG = -1e30
LN_EPS = 1e-5
DEPTH = 2
DN_ALPHA = (2 * DEPTH) ** 0.25

RWKV_GN_EPS = HEAD_DIM * 1e-5
GLA_DK = 32
GLA_TAU = 16.0
DSA_KV_RANK = 128
IDX_HEADS = 8
IDX_DIM = 32
TOPK_MAX = 256
N_BUCKETS = 32
MAX_DISTANCE = 128
CONV_W = 4
N_GROUPS = 4
EPG = 4
N_EXPERTS = 16
D_EXPERT = 256

INT_MIN = -(2 ** 31)
KEY_INF = 0x7F800000
VMEM_LIMIT = 56 * 1024 * 1024


def _cparams(*sem):
    return pltpu.CompilerParams(dimension_semantics=tuple(sem), vmem_limit_bytes=VMEM_LIMIT)


def _pick_tile(n, target):
    best = LANES
    t = LANES
    while t <= min(n, target):
        if n % t == 0:
            best = t
        t += LANES
    return best


def _bdot(a, b):
    return jnp.dot(a.astype(BF16), b.astype(BF16), preferred_element_type=F32)


def _bdot_nt(a, b):
    return lax.dot_general(a.astype(BF16), b.astype(BF16), (((1,), (1,)), ((), ())),
                           preferred_element_type=F32)


def _bdot_tn(a, b):
    return lax.dot_general(a.astype(BF16), b.astype(BF16), (((0,), (0,)), ((), ())),
                           preferred_element_type=F32)


def _split(a):
    hi = a.astype(BF16)
    lo = (a - hi.astype(F32)).astype(BF16)
    return hi, lo


_NN = (((1,), (0,)), ((), ()))
_NT = (((1,), (1,)), ((), ()))
_TN = (((0,), (0,)), ((), ()))


def _dot3(a, b, dims=_NN):
    ah, al = _split(a)
    bh, bl = _split(b)
    dg = lambda x, y: lax.dot_general(x, y, dims, preferred_element_type=F32)
    return dg(ah, bh) + (dg(ah, bl) + dg(al, bh))


def _dot_exact_lhs(a_bf16, b):
    bh, bl = _split(b)
    return (jnp.dot(a_bf16, bh, preferred_element_type=F32)
            + jnp.dot(a_bf16, bl, preferred_element_type=F32))


def _dot_exact_rhs(a, b_bf16):
    ah, al = _split(a)
    return (jnp.dot(ah, b_bf16, preferred_element_type=F32)
            + jnp.dot(al, b_bf16, preferred_element_type=F32))


def _sigmoid(x):
    return 1.0 / (1.0 + jnp.exp(-x))


def _log_sigmoid(x):
    return jnp.minimum(x, 0.0) - jnp.log(1.0 + jnp.exp(-jnp.abs(x)))


def _silu(x):
    return x * _sigmoid(x)


def _iota(shape, dim):
    return lax.broadcasted_iota(jnp.int32, shape, dim)


def _tri_incl(n):
    return (_iota((n, n), 1) <= _iota((n, n), 0))


def _head_ones():
    return ((_iota((MIX_W, MIX_W), 0) // HEAD_DIM) == (_iota((MIX_W, MIX_W), 1) // HEAD_DIM)).astype(BF16)


def _row_ids(rows):
    return pl.program_id(1) * ROW_TILE + _iota((rows, 1), 0)


def _embed_kernel(x_ref, meta_ref, g_ref, b_ref, h_ref, hb_ref):
    j = pl.program_id(1)
    src = jnp.where(j == 0, meta_ref[...], x_ref[0])
    mu = jnp.mean(src, -1, keepdims=True)
    xc = src - mu
    var = jnp.mean(xc * xc, -1, keepdims=True)
    y = xc * lax.rsqrt(var + LN_EPS) * g_ref[...] + b_ref[...]
    h_ref[0] = y
    hb_ref[0] = y.astype(BF16)


def _embed(x, meta, g, b):
    B, S, D = x.shape
    TP = S + FRONT
    meta_pad = jnp.concatenate([jnp.zeros((FP, D), F32), meta.astype(F32)], axis=0)
    return pl.pallas_call(
        _embed_kernel,
        grid=(B, TP // ROW_TILE),
        in_specs=[
            pl.BlockSpec((1, ROW_TILE, D), lambda b, j: (b, jnp.maximum(j - 1, 0), 0)),
            pl.BlockSpec((ROW_TILE, D), lambda b, j: (0, 0)),
            pl.BlockSpec((1, D), lambda b, j: (0, 0)),
            pl.BlockSpec((1, D), lambda b, j: (0, 0)),
        ],
        out_specs=[
            pl.BlockSpec((1, ROW_TILE, D), lambda b, j: (b, j, 0)),
            pl.BlockSpec((1, ROW_TILE, D), lambda b, j: (b, j, 0)),
        ],
        out_shape=[jax.ShapeDtypeStruct((B, TP, D), F32), jax.ShapeDtypeStruct((B, TP, D), BF16)],
        compiler_params=_cparams("parallel", "arbitrary"),
        name="embed_ln",
    )(x, meta_pad, g.reshape(1, D), b.reshape(1, D))


def _proj_kernel(h_ref, w_ref, o_ref, *, act):
    y = jnp.dot(h_ref[...], w_ref[...], preferred_element_type=F32)
    if act == "sigmoid":
        y = _sigmoid(y)
    o_ref[...] = y.astype(o_ref.dtype)


def _proj(hb, w, act=None, out_dtype=F32):
    N, D = hb.shape
    W = w.shape[1]
    tn = W if W <= 1152 else 1024
    tm = _pick_tile(N, 1280)
    return pl.pallas_call(
        functools.partial(_proj_kernel, act=act),
        grid=(W // tn, N // tm),
        in_specs=[pl.BlockSpec((tm, D), lambda j, i: (i, 0)),
                  pl.BlockSpec((D, tn), lambda j, i: (0, j))],
        out_specs=pl.BlockSpec((tm, tn), lambda j, i: (i, j)),
        out_shape=jax.ShapeDtypeStruct((N, W), out_dtype),
        compiler_params=_cparams("arbitrary", "arbitrary"),
        name="in_proj",
    )(hb, w)


def _rwkv_kernel(p_ref, mu_ref, wup_ref, w0_ref, aup_ref, a0_ref, gup_ref, kk_ref, ka_ref, rk_ref,
                 gng_ref, gnb_ref, y_ref, carry_ref, s_ref):
    j = pl.program_id(0)
    nb = p_ref.shape[0]
    n_chunks = ROW_TILE // CHUNK

    @pl.when(j == 0)
    def _():
        carry_ref[...] = jnp.zeros_like(carry_ref)
        s_ref[...] = jnp.zeros_like(s_ref)

    valid = (j * ROW_TILE + _iota((ROW_TILE, 1), 0)) >= FP
    first_row = _iota((ROW_TILE, 1), 0) == 0
    ones_h = _head_ones()
    tri = _tri_incl(CHUNK)
    tri_b = tri.astype(BF16)
    strict = _iota((CHUNK, CHUNK), 1) < _iota((CHUNK, CHUNK), 0)
    eye = (_iota((CHUNK, CHUNK), 1) == _iota((CHUNK, CHUNK), 0)).astype(F32)
    heads = [slice(h * HEAD_DIM, (h + 1) * HEAD_DIM) for h in range(N_HEADS)]

    pro = []
    unit = {}
    for b in range(nb):
        p = jnp.where(valid, p_ref[b], 0.0)
        prev = jnp.where(first_row, carry_ref[b], pltpu.roll(p, 1, 0))
        carry_ref[b] = p[ROW_TILE - 1:ROW_TILE, :]
        ps = p + (prev - p) * mu_ref[...]
        r = ps[:, 0:256]
        k = ps[:, 256:512]
        v = ps[:, 512:768]
        lora_in = ps[:, 768:896]
        xg = ps[:, 896:1024]
        w_log = _log_sigmoid(w0_ref[...] + _bdot(jnp.tanh(lora_in), wup_ref[...])) - 0.5
        lw = jnp.where(valid, -jnp.exp(w_log), 0.0)
        alpha = _sigmoid(a0_ref[...] + _bdot(lora_in, aup_ref[...]))
        gate = _bdot(_sigmoid(xg), gup_ref[...])
        kk = k * kk_ref[...]
        kk = kk / jnp.maximum(jnp.sqrt(_dot_exact_rhs(kk * kk, ones_h)), 1e-12)
        k = k * (1.0 + (alpha - 1.0) * ka_ref[...])
        kka = kk * alpha
        pro.append((r, k, v, gate))
        for c in range(n_chunks):
            sl = slice(c * CHUNK, (c + 1) * CHUNK)
            lw_c = lw[sl]
            cum = _dot_exact_lhs(tri_b, lw_c)
            cum_last = cum[CHUNK - 1:CHUNK, :]
            p_inv = jnp.exp(-cum)
            p_tail = jnp.exp(cum_last - cum)
            unit[b, c] = dict(a=-kk[sl] * jnp.exp(cum - lw_c), b=kka[sl] * p_inv, k=k[sl] * p_inv,
                              r=r[sl] * jnp.exp(cum), kb=k[sl] * p_tail, bb=kka[sl] * p_tail,
                              pl=jnp.exp(cum_last), v=v[sl])

    keys = [(b, c, h) for b in range(nb) for c in range(n_chunks) for h in range(N_HEADS)]
    part = lambda name, key: unit[key[0], key[1]][name][:, heads[key[2]]]
    a_ab = {q: jnp.where(strict, _dot3(part("a", q), part("b", q), _NT), 0.0) for q in keys}
    a_ak = {q: jnp.where(strict, _bdot_nt(part("a", q), part("k", q)), 0.0) for q in keys}
    a_rb = {q: jnp.where(tri, _bdot_nt(part("r", q), part("b", q)), 0.0) for q in keys}
    a_rk = {q: jnp.where(tri, _bdot_nt(part("r", q), part("k", q)), 0.0) for q in keys}
    inv = {q: eye + a_ab[q] for q in keys}
    pw = a_ab
    for _ in range(5):
        pw = {q: _bdot(pw[q], pw[q]) for q in keys}
        inv = {q: inv[q] + _bdot(inv[q], pw[q]) for q in keys}
    ak_v = {q: _bdot(a_ak[q], part("v", q)) for q in keys}
    rk_v = {q: _bdot(a_rk[q], part("v", q)) for q in keys}
    kb_v = {q: _bdot_tn(part("v", q), part("kb", q)) for q in keys}

    bh = [(b, h) for b in range(nb) for h in range(N_HEADS)]
    state = {q: s_ref[q[0], q[1]] for q in bh}
    y_parts = {}
    for c in range(n_chunks):
        full = lambda q: (q[0], c, q[1])
        a_s = {q: _bdot_nt(part("a", full(q)), state[q]) for q in bh}
        r_s = {q: _bdot_nt(part("r", full(q)), state[q]) for q in bh}
        u = {q: _bdot(inv[full(q)], a_s[q] + ak_v[full(q)]) for q in bh}
        for q in bh:
            y_parts[full(q)] = r_s[q] + rk_v[full(q)] + _bdot(a_rb[full(q)], u[q])
        state = {q: (state[q] * part("pl", full(q)) + kb_v[full(q)] + _bdot_tn(u[q], part("bb", full(q))))
                 for q in bh}
    for q in bh:
        s_ref[q[0], q[1]] = state[q]

    for b in range(nb):
        r, k, v, gate = pro[b]
        y = jnp.concatenate([jnp.concatenate([y_parts[b, c, h] for h in range(N_HEADS)], axis=1)
                             for c in range(n_chunks)], axis=0)
        mean = _dot_exact_rhs(y, ones_h) * (1.0 / HEAD_DIM)
        yc = y - mean
        var = _dot_exact_rhs(yc * yc, ones_h) * (1.0 / HEAD_DIM)
        yn = yc * lax.rsqrt(var + RWKV_GN_EPS) * gng_ref[...] + gnb_ref[...]
        bonus = _dot_exact_rhs(r * k * rk_ref[...], ones_h) * v
        y_ref[b] = ((yn + bonus) * gate).astype(y_ref.dtype)


def _rwkv(p_a, B, TP, mu, w_up, w0, a_up, a0, g_up, k_k, k_a, r_k, gn_g, gn_b):
    W = MIX_W
    z64 = jnp.zeros((64, W), F32)
    wup_pad = jnp.concatenate([w_up, z64], axis=0).astype(BF16)
    aup_pad = jnp.concatenate([z64, a_up], axis=0).astype(BF16)
    row = lambda a: a.reshape(1, -1).astype(F32)
    full = lambda shape: pl.BlockSpec(shape, lambda j: (0,) * len(shape))
    return pl.pallas_call(
        _rwkv_kernel,
        grid=(TP // ROW_TILE,),
        in_specs=[pl.BlockSpec((B, ROW_TILE, 1024), lambda j: (0, j, 0)),
                  full((1, 1024)), full((128, W)), full((1, W)), full((128, W)), full((1, W)),
                  full((128, W)), full((1, W)), full((1, W)), full((1, W)), full((1, W)), full((1, W))],
        out_specs=pl.BlockSpec((B, ROW_TILE, W), lambda j: (0, j, 0)),
        out_shape=jax.ShapeDtypeStruct((B, TP, W), BF16),
        scratch_shapes=[pltpu.VMEM((B, 1, 1024), F32), pltpu.VMEM((B, N_HEADS, HEAD_DIM, HEAD_DIM), F32)],
        compiler_params=_cparams("arbitrary"),
        name="rwkv7",
    )(p_a.reshape(B, TP, 1024), row(mu), wup_pad, row(w0), aup_pad, row(a0), g_up.astype(BF16),
      row(k_k), row(k_a), row(r_k), row(gn_g), row(gn_b))


def _gla_kernel(p_ref, aup_ref, ab_ref, ng_ref, y_ref, s_ref):
    j = pl.program_id(0)
    nb = p_ref.shape[0]
    n_chunks = ROW_TILE // CHUNK

    @pl.when(j == 0)
    def _():
        s_ref[...] = jnp.zeros_like(s_ref)

    valid = (j * ROW_TILE + _iota((ROW_TILE, 1), 0)) >= FP
    tri = _tri_incl(CHUNK)
    tri_b = tri.astype(BF16)

    og_all = []
    pre = {}
    for b in range(nb):
        p = jnp.where(valid, p_ref[b], 0.0)
        la = _log_sigmoid(_bdot(p[:, 768:896], aup_ref[...]) + ab_ref[...]) * (1.0 / GLA_TAU)
        la = jnp.where(valid, la, 0.0)
        og_all.append(p[:, 512:768])
        for c in range(n_chunks):
            sl = slice(c * CHUNK, (c + 1) * CHUNK)
            pre[b, c] = dict(q=p[sl, 0:128] * (GLA_DK ** -0.5), k=p[sl, 128:256], v=p[sl, 256:512], la=la[sl])
    bc = [(b, c) for b in range(nb) for c in range(n_chunks)]
    keys = [(b, c, h) for (b, c) in bc for h in range(N_HEADS)]
    ks = [slice(h * GLA_DK, (h + 1) * GLA_DK) for h in range(N_HEADS)]
    vs = [slice(h * HEAD_DIM, (h + 1) * HEAD_DIM) for h in range(N_HEADS)]
    b_cum = {u: _dot_exact_lhs(tri_b, pre[u]["la"]) for u in bc}
    b_last = {u: b_cum[u][CHUNK - 1:CHUNK, :] for u in bc}
    q_g = {u: pre[u]["q"] * jnp.exp(b_cum[u]) for u in bc}
    k_g = {u: pre[u]["k"] * jnp.exp(-b_cum[u]) for u in bc}
    k_l = {u: pre[u]["k"] * jnp.exp(b_last[u] - b_cum[u]) for u in bc}
    dec = {u: jnp.exp(b_last[u]) for u in bc}
    att = {u: jnp.where(tri, _bdot_nt(q_g[u[0], u[1]][:, ks[u[2]]], k_g[u[0], u[1]][:, ks[u[2]]]), 0.0)
           for u in keys}
    att_v = {u: _bdot(att[u], pre[u[0], u[1]]["v"][:, vs[u[2]]]) for u in keys}
    kl_v = {u: _bdot_tn(pre[u[0], u[1]]["v"][:, vs[u[2]]], k_l[u[0], u[1]][:, ks[u[2]]]) for u in keys}

    bh = [(b, h) for b in range(nb) for h in range(N_HEADS)]
    state = {q: s_ref[q[0], q[1]] for q in bh}
    o_parts = {}
    for c in range(n_chunks):
        for q in bh:
            o_parts[q[0], c, q[1]] = att_v[q[0], c, q[1]] + _bdot_nt(q_g[q[0], c][:, ks[q[1]]], state[q])
        state = {q: state[q] * dec[q[0], c][:, ks[q[1]]] + kl_v[q[0], c, q[1]] for q in bh}
    for q in bh:
        s_ref[q[0], q[1]] = state[q]

    ones_h = _head_ones()
    for b in range(nb):
        o = jnp.concatenate([jnp.concatenate([o_parts[b, c, h] for h in range(N_HEADS)], axis=1)
                             for c in range(n_chunks)], axis=0)
        ms = _dot_exact_rhs(o * o, ones_h) * (1.0 / HEAD_DIM)
        y = o * lax.rsqrt(ms + 1e-6) * ng_ref[...] * _silu(og_all[b])
        y_ref[b] = y.astype(y_ref.dtype)


def _gla(p_b, B, TP, a_up, a_b, norm_g):
    aup_pad = jnp.zeros((128, 128), F32).at[:a_up.shape[0]].set(a_up).astype(BF16)
    full = lambda shape: pl.BlockSpec(shape, lambda j: (0,) * len(shape))
    return pl.pallas_call(
        _gla_kernel,
        grid=(TP // ROW_TILE,),
        in_specs=[pl.BlockSpec((B, ROW_TILE, 896), lambda j: (0, j, 0)),
                  full((128, 128)), full((1, 128)), full((1, MIX_W))],
        out_specs=pl.BlockSpec((B, ROW_TILE, MIX_W), lambda j: (0, j, 0)),
        out_shape=jax.ShapeDtypeStruct((B, TP, MIX_W), BF16),
        scratch_shapes=[pltpu.VMEM((B, N_HEADS, HEAD_DIM, GLA_DK), F32)],
        compiler_params=_cparams("arbitrary"),
        name="gla",
    )(p_b.reshape(B, TP, 896), aup_pad, a_b.reshape(1, 128).astype(F32),
      jnp.tile(norm_g.astype(F32), N_HEADS).reshape(1, MIX_W))


def _mlstm_kernel(p_ref, cw_ref, cb_ref, ib_ref, fb_ref, ng_ref, y_ref, carry_ref, c_ref, n_ref, m_ref):
    j = pl.program_id(0)
    nb = p_ref.shape[0]
    n_chunks = ROW_TILE // CHUNK

    @pl.when(j == 0)
    def _():
        carry_ref[...] = jnp.zeros_like(carry_ref)
        c_ref[...] = jnp.zeros_like(c_ref)
        n_ref[...] = jnp.zeros_like(n_ref)
        m_ref[...] = jnp.zeros_like(m_ref)

    valid = (j * ROW_TILE + _iota((ROW_TILE, 1), 0)) >= FP
    tri = _tri_incl(CHUNK)
    tri_b = tri.astype(BF16)
    ones_h = _head_ones()

    og_all = []
    pre = {}
    for b in range(nb):
        p = jnp.where(valid, p_ref[b], 0.0)
        a = p[:, 0:512]
        ext = jnp.concatenate([carry_ref[b], a], axis=0)
        carry_ref[b] = a[ROW_TILE - 8:ROW_TILE, :]
        conv = cb_ref[...] + a * cw_ref[CONV_W - 1:CONV_W, :]
        for s in range(1, CONV_W):
            conv = conv + pltpu.roll(ext, s, 0)[8:8 + ROW_TILE, :] * cw_ref[CONV_W - 1 - s:CONV_W - s, :]
        qk = _silu(conv)
        q = jnp.where(valid, qk[:, 0:MIX_W], 0.0)
        k = jnp.where(valid, qk[:, MIX_W:2 * MIX_W], 0.0) * (HEAD_DIM ** -0.5)
        v = p[:, 512:768]
        og_all.append(p[:, 768:1024])
        gates = p[:, 1024:1152]
        li_all = jnp.where(valid, gates + ib_ref[...], NEG)
        lf_all = jnp.where(valid, _log_sigmoid(gates + fb_ref[...]), 0.0)
        for c in range(n_chunks):
            sl = slice(c * CHUNK, (c + 1) * CHUNK)
            pre[b, c] = dict(q=q[sl], k=k[sl], v=v[sl], li=li_all[sl], lf=lf_all[sl])

    bc = [(b, c) for b in range(nb) for c in range(n_chunks)]
    keys = [(b, c, h) for (b, c) in bc for h in range(N_HEADS)]
    heads = [slice(h * HEAD_DIM, (h + 1) * HEAD_DIM) for h in range(N_HEADS)]
    part = lambda name, u: pre[u[0], u[1]][name][:, heads[u[2]]]
    b_cum = {u: _dot_exact_lhs(tri_b, pre[u]["lf"]) for u in bc}
    b_t = {u: b_cum[u].T for u in bc}
    li_t = {u: pre[u]["li"].T for u in bc}
    b_col = {u: b_cum[u[0], u[1]][:, N_HEADS + u[2]:N_HEADS + u[2] + 1] for u in keys}
    b_last = {u: b_col[u][CHUNK - 1:CHUNK, :] for u in keys}
    d_log = {u: jnp.where(tri, b_col[u] - b_t[u[0], u[1]][N_HEADS + u[2]:N_HEADS + u[2] + 1, :]
                          + li_t[u[0], u[1]][u[2]:u[2] + 1, :], -jnp.inf) for u in keys}
    dmax = {u: jnp.max(d_log[u], axis=1, keepdims=True) for u in keys}
    qk = {u: _bdot_nt(part("q", u), part("k", u)) for u in keys}
    s0 = {u: jnp.exp(d_log[u] - dmax[u]) * qk[u] for u in keys}
    sv = {u: _bdot(s0[u], part("v", u)) for u in keys}
    ssum = {u: jnp.sum(s0[u], axis=1, keepdims=True) for u in keys}
    g_loc = {u: b_last[u] - b_col[u] + pre[u[0], u[1]]["li"][:, u[2]:u[2] + 1] for u in keys}
    m_loc = {u: jnp.max(g_loc[u], axis=0, keepdims=True) for u in keys}
    kw = {u: part("k", u) * jnp.exp(g_loc[u] - m_loc[u]) for u in keys}
    kwv = {u: _bdot_tn(kw[u], part("v", u)) for u in keys}
    kwsum = {u: jnp.sum(kw[u], axis=0, keepdims=True) for u in keys}

    bh = [(b, h) for b in range(nb) for h in range(N_HEADS)]
    c_st = {q: c_ref[q[0], q[1]] for q in bh}
    n_st = {q: n_ref[q[0], q[1]] for q in bh}
    m_st = {q: m_ref[q[0], q[1]] for q in bh}
    h_parts = {}
    for c in range(n_chunks):
        full = lambda q: (q[0], c, q[1])
        qc = {q: _bdot(part("q", full(q)), c_st[q]) for q in bh}
        qn = {q: jnp.sum(part("q", full(q)) * n_st[q], axis=1, keepdims=True) for q in bh}
        inter = {q: b_col[full(q)] + m_st[q] for q in bh}
        m_t = {q: jnp.maximum(inter[q], dmax[full(q)]) for q in bh}
        e_loc = {q: jnp.exp(dmax[full(q)] - m_t[q]) for q in bh}
        w_int = {q: jnp.exp(inter[q] - m_t[q]) for q in bh}
        for q in bh:
            num = e_loc[q] * sv[full(q)] + w_int[q] * qc[q]
            den = e_loc[q] * ssum[full(q)] + w_int[q] * qn[q]
            h_parts[full(q)] = num / jnp.maximum(jnp.abs(den), jnp.exp(-m_t[q]))
        m_new = {q: jnp.maximum(b_last[full(q)] + m_st[q], m_loc[full(q)]) for q in bh}
        s_old = {q: jnp.exp(b_last[full(q)] + m_st[q] - m_new[q]) for q in bh}
        s_new = {q: jnp.exp(m_loc[full(q)] - m_new[q]) for q in bh}
        c_st = {q: s_old[q] * c_st[q] + s_new[q] * kwv[full(q)] for q in bh}
        n_st = {q: s_old[q] * n_st[q] + s_new[q] * kwsum[full(q)] for q in bh}
        m_st = m_new
    for q in bh:
        c_ref[q[0], q[1]], n_ref[q[0], q[1]], m_ref[q[0], q[1]] = c_st[q], n_st[q], m_st[q]

    for b in range(nb):
        hh = jnp.concatenate([jnp.concatenate([h_parts[b, c, h] for h in range(N_HEADS)], axis=1)
                              for c in range(n_chunks)], axis=0) * _sigmoid(og_all[b])
        mean = _dot_exact_rhs(hh, ones_h) * (1.0 / HEAD_DIM)
        hc = hh - mean
        var = _dot_exact_rhs(hc * hc, ones_h) * (1.0 / HEAD_DIM)
        y_ref[b] = (hc * lax.rsqrt(var + 1e-5) * ng_ref[...]).astype(y_ref.dtype)


def _mlstm(p_d, B, TP, conv_w, conv_b, i_b, f_b, norm_g):
    ib = jnp.zeros((1, LANES), F32).at[0, 0:N_HEADS].set(i_b)
    fb = jnp.zeros((1, LANES), F32).at[0, N_HEADS:2 * N_HEADS].set(f_b)
    full = lambda shape: pl.BlockSpec(shape, lambda j: (0,) * len(shape))
    return pl.pallas_call(
        _mlstm_kernel,
        grid=(TP // ROW_TILE,),
        in_specs=[pl.BlockSpec((B, ROW_TILE, 1152), lambda j: (0, j, 0)),
                  full((CONV_W, 512)), full((1, 512)), full((1, LANES)), full((1, LANES)), full((1, MIX_W))],
        out_specs=pl.BlockSpec((B, ROW_TILE, MIX_W), lambda j: (0, j, 0)),
        out_shape=jax.ShapeDtypeStruct((B, TP, MIX_W), BF16),
        scratch_shapes=[pltpu.VMEM((B, 8, 512), F32),
                        pltpu.VMEM((B, N_HEADS, HEAD_DIM, HEAD_DIM), F32),
                        pltpu.VMEM((B, N_HEADS, 1, HEAD_DIM), F32),
                        pltpu.VMEM((B, N_HEADS, 1, 1), F32)],
        compiler_params=_cparams("arbitrary"),
        name="mlstm",
    )(p_d.reshape(B, TP, 1152), conv_w.astype(F32), conv_b.reshape(1, 512).astype(F32), ib, fb,
      norm_g.reshape(1, MIX_W).astype(F32))


V_ROWS = 80
WT_ROWS = 528


def _dsa_prep_kernel(h_ref, wt_ref, wn_ref, kvg_ref, wuk_ref, wuvt_ref,
                     qt_ref, qit_ref, wit_ref, k_ref, ki_ref, vt_ref):
    hb = h_ref[...]
    tm = hb.shape[0]
    pt = lax.dot_general(wt_ref[...], hb, _NT, preferred_element_type=F32)
    pn = jnp.dot(hb, wn_ref[...], preferred_element_type=F32)
    ckv = pn[:, 0:DSA_KV_RANK]
    c = ckv * lax.rsqrt(jnp.mean(ckv * ckv, -1, keepdims=True) + 1e-6) * kvg_ref[...]
    cb = c.astype(BF16)
    k_ref[...] = jnp.dot(cb, wuk_ref[...], preferred_element_type=F32).astype(BF16)
    ki_ref[...] = pn[:, DSA_KV_RANK:DSA_KV_RANK + IDX_DIM].astype(BF16)
    vt = lax.dot_general(wuvt_ref[...], cb, _NT, preferred_element_type=F32)
    vt = jnp.where(_iota((V_ROWS, tm), 0) == HEAD_DIM, 1.0, vt)
    for t in range(tm // LANES):
        cs = slice(t * LANES, (t + 1) * LANES)
        for h in range(N_HEADS):
            qt_ref[t, :, h * LANES:(h + 1) * LANES] = (
                pt[h * HEAD_DIM:(h + 1) * HEAD_DIM, cs] * (HEAD_DIM ** -0.5)).astype(BF16)
        for h in range(IDX_HEADS):
            qit_ref[t, :, h * LANES:(h + 1) * LANES] = pt[MIX_W + h * IDX_DIM:MIX_W + (h + 1) * IDX_DIM, cs].astype(BF16)
        wit_ref[t] = pt[2 * MIX_W:2 * MIX_W + IDX_HEADS, cs] * ((IDX_HEADS * IDX_DIM) ** -0.5)
        vt_ref[t] = vt[:, cs].astype(BF16)


def _dsa_prep(hb, w_t, w_n, kv_norm_g, w_uk, w_uv):
    N, D = hb.shape
    tm = _pick_tile(N, 640)
    nt = tm // LANES
    full = lambda shape: pl.BlockSpec(shape, lambda i: (0,) * len(shape))
    wuvt = jnp.pad(w_uv.T, ((0, V_ROWS - HEAD_DIM), (0, 0))).astype(BF16)
    return pl.pallas_call(
        _dsa_prep_kernel,
        grid=(N // tm,),
        in_specs=[pl.BlockSpec((tm, D), lambda i: (i, 0)),
                  full((WT_ROWS, D)), full((D, 256)), full((1, DSA_KV_RANK)),
                  full((DSA_KV_RANK, HEAD_DIM)), full((V_ROWS, DSA_KV_RANK))],
        out_specs=[pl.BlockSpec((nt, HEAD_DIM, N_HEADS * LANES), lambda i: (i, 0, 0)),
                   pl.BlockSpec((nt, IDX_DIM, IDX_HEADS * LANES), lambda i: (i, 0, 0)),
                   pl.BlockSpec((nt, IDX_HEADS, LANES), lambda i: (i, 0, 0)),
                   pl.BlockSpec((tm, HEAD_DIM), lambda i: (i, 0)),
                   pl.BlockSpec((tm, IDX_DIM), lambda i: (i, 0)),
                   pl.BlockSpec((nt, V_ROWS, LANES), lambda i: (i, 0, 0))],
        out_shape=[jax.ShapeDtypeStruct((N // LANES, HEAD_DIM, N_HEADS * LANES), BF16),
                   jax.ShapeDtypeStruct((N // LANES, IDX_DIM, IDX_HEADS * LANES), BF16),
                   jax.ShapeDtypeStruct((N // LANES, IDX_HEADS, LANES), F32),
                   jax.ShapeDtypeStruct((N, HEAD_DIM), BF16),
                   jax.ShapeDtypeStruct((N, IDX_DIM), BF16),
                   jax.ShapeDtypeStruct((N // LANES, V_ROWS, LANES), BF16)],
        compiler_params=_cparams("arbitrary"),
        name="dsa_prep",
    )(hb, w_t, w_n, kv_norm_g.reshape(1, DSA_KV_RANK).astype(F32), w_uk.astype(BF16), wuvt)


def _dsa_kernel(qt_ref, qit_ref, wit_ref, k_ref, ki_ref, vt_ref, bias_ref, y_ref,
                sk_ref, rel_ref, m_ref, acc_ref, lg_ref, mg_ref, *, topk):
    i = pl.program_id(1)
    nk = i + 1
    QT = ROW_TILE
    HQ = N_HEADS * QT
    t_lane = i * QT + _iota((LANES, QT), 1)
    key_pos = lambda kt: kt * LANES + _iota((LANES, QT), 0)
    per_head = lambda fn: jnp.concatenate([fn(slice(h * QT, (h + 1) * QT)) for h in range(N_HEADS)], axis=1)

    qit = qit_ref[0]
    wit = wit_ref[0]

    GW = rel_ref.shape[1] // LANES
    n_tiles = sk_ref.shape[0] - 1
    n_trips = (i + 2 * GW) // (2 * GW)

    def group_base(g):
        return jnp.clip(GW * g, 0, n_tiles - GW)

    def issue(g, slot):
        span = pl.ds(pl.multiple_of(group_base(g) * LANES, LANES), GW * LANES)
        rel_ref[slot] = jnp.dot(ki_ref[span, :], qit, preferred_element_type=F32)

    def reduce(g, slot):
        for u in range(GW):
            kt = group_base(g) + u
            rows_u = slice(u * LANES, (u + 1) * LANES)
            score = jnp.maximum(rel_ref[slot, rows_u, 0:QT], 0.0) * wit[0:1, :]
            for h in range(1, IDX_HEADS):
                score = score + jnp.maximum(rel_ref[slot, rows_u, h * QT:(h + 1) * QT], 0.0) * wit[h:h + 1, :]
            score = jnp.where(score == 0.0, 0.0, score)
            bits = lax.bitcast_convert_type(score, jnp.int32)
            key = jnp.where(bits < 0, bits ^ jnp.int32(0x7FFFFFFF), bits)
            mine = (kt >= GW * g) & (kt <= i)
            sk_ref[jnp.where(mine, kt, n_tiles)] = key

    issue(0, 0)

    def score_body(jj, c):
        issue(2 * jj + 1, 1)
        reduce(2 * jj, 0)
        issue(2 * jj + 2, 0)
        reduce(2 * jj + 1, 1)
        return c

    lax.fori_loop(0, n_trips, score_body, 0)
    first = sk_ref[0]
    first = jnp.where(key_pos(0) < FP + N_META, jnp.int32(KEY_INF), first)
    sk_ref[0] = jnp.where(key_pos(0) >= FP, first, jnp.int32(INT_MIN))
    sk_ref[i] = jnp.where(key_pos(i) <= t_lane, sk_ref[i], jnp.int32(INT_MIN))

    def count(pred_fn):
        def body(kt, acc):
            return acc + jnp.where(pred_fn(sk_ref[kt], kt), 1, 0)

        def body4(j, acc):
            for u in range(4):
                acc = body(4 * j + u, acc)
            return acc

        n4 = lax.shift_right_logical(nk, 2)
        acc = lax.fori_loop(0, n4, body4, jnp.zeros((LANES, QT), jnp.int32))
        acc = lax.fori_loop(4 * n4, nk, body, acc)
        return jnp.sum(acc, axis=0, keepdims=True)

    def bit_body(it, carry):
        tau, n_ge = carry
        cand = tau + jnp.left_shift(jnp.int32(1), 31 - it)
        cnt = count(lambda sk, kt: sk >= cand)
        return jnp.where(cnt >= topk, cand, tau), jnp.where(cnt >= topk, cnt, n_ge)

    tau, n_ge = lax.fori_loop(0, 32, bit_body, (jnp.full((1, QT), INT_MIN, jnp.int32),
                                                jnp.zeros((1, QT), jnp.int32)))
    tau = jnp.maximum(tau, jnp.int32(INT_MIN + 1))

    @pl.when(jnp.max(n_ge - topk) > 0)
    def _():
        n_bits = max(1, int(math.ceil(math.log2(sk_ref.shape[0] * LANES + 1))))
        need = topk - count(lambda sk, kt: sk > tau)

        def pos_body(it, x):
            cand = x + jnp.left_shift(jnp.int32(1), n_bits - 1 - it)
            cnt = count(lambda sk, kt: (sk == tau) & (key_pos(kt) < cand))
            return jnp.where(cnt < need, cand, x)

        x = lax.fori_loop(0, n_bits, pos_body, jnp.zeros((1, QT), jnp.int32))
        jmax = jnp.where(n_ge > topk, x, jnp.int32(2 ** 30))

        def drop_body(kt, c):
            sk = sk_ref[kt]
            sk_ref[kt] = jnp.where((sk == tau) & (key_pos(kt) > jmax), jnp.int32(INT_MIN), sk)
            return c

        lax.fori_loop(0, nk, drop_body, 0)

    qt = qt_ref[0]
    m_ref[...] = jnp.full((1, HQ), NEG, F32)
    acc_ref[...] = jnp.zeros((V_ROWS, HQ), F32)
    int_max = jnp.int32(2 ** 31 - 1)

    def park(g, slot):
        base = group_base(g)
        span = pl.ds(pl.multiple_of(base * LANES, LANES), GW * LANES)
        lg_all = jnp.dot(k_ref[span, :], qt, preferred_element_type=F32)
        tmax = None
        for u in range(GW):
            t = base + u
            mine = (t >= GW * g) & (t <= i)
            tau_u = jnp.where(mine, tau, int_max)
            kt = jnp.minimum(t, i)
            lg = lg_all[u * LANES:(u + 1) * LANES, :] + bias_ref[jnp.clip(i - t, 0, 2)]
            sel = sk_ref[kt] >= tau_u
            lgm = per_head(lambda hs: jnp.where(sel, lg[:, hs], NEG))
            lg_ref[slot, u] = lgm
            tmax = lgm if tmax is None else jnp.maximum(tmax, lgm)
        mg_ref[slot] = jnp.max(tmax, axis=0, keepdims=True)

    def weights(slot):
        m_old = m_ref[...]
        m_new = jnp.maximum(m_old, mg_ref[slot])
        m_ref[...] = m_new
        return jnp.exp(m_old - m_new), [jnp.exp(lg_ref[slot, u] - m_new).astype(BF16) for u in range(GW)]

    def fold(g, corr, prs):
        vt_all = jnp.concatenate([vt_ref[group_base(g) + u] for u in range(GW)], axis=1)
        pv = jnp.dot(vt_all, jnp.concatenate(prs, axis=0), preferred_element_type=F32)
        acc_ref[...] = acc_ref[...] * corr + pv

    park(0, 0)

    def pipe_body(jj, c):
        corr, prs = weights(0)
        park(2 * jj + 1, 1)
        fold(2 * jj, corr, prs)
        corr, prs = weights(1)
        park(2 * jj + 2, 0)
        fold(2 * jj + 1, corr, prs)
        return c

    lax.fori_loop(0, n_trips, pipe_body, 0)
    acc = acc_ref[...]
    out = acc[0:HEAD_DIM, :] / jnp.maximum(acc[HEAD_DIM:HEAD_DIM + 1, :], 1e-30)
    y_ref[...] = per_head(lambda hs: out[:, hs].T).astype(y_ref.dtype)


def _t5_bucket(dist):
    max_exact = N_BUCKETS // 2
    n = jnp.maximum(dist, 0)
    large = max_exact + (jnp.log(jnp.maximum(n, 1).astype(F32) / max_exact)
                         / math.log(MAX_DISTANCE / max_exact) * (N_BUCKETS - max_exact)).astype(jnp.int32)
    return jnp.where(n < max_exact, n, jnp.minimum(large, N_BUCKETS - 1))


def _bias_tables(rel_bias):
    per_dist = rel_bias[_t5_bucket(jnp.arange(2 * ROW_TILE, dtype=jnp.int32))]
    q_minus_s = np.arange(ROW_TILE)[None, :] - np.arange(ROW_TILE)[:, None]
    far = per_dist[2 * ROW_TILE - 1]
    tabs = [per_dist[np.clip(r * ROW_TILE + q_minus_s, 0, 2 * ROW_TILE - 1)] - far for r in (0, 1)]
    tabs.append(jnp.zeros_like(tabs[0]))
    return jnp.stack(tabs).transpose(0, 1, 3, 2).reshape(3, ROW_TILE, N_HEADS * ROW_TILE).astype(F32)


def _dsa(qt, qit, wit, k, ki, vt, bias_tab, B, TP, topk):
    nq = TP // ROW_TILE
    return pl.pallas_call(
        functools.partial(_dsa_kernel, topk=topk),
        grid=(B, nq),
        in_specs=[pl.BlockSpec((1, HEAD_DIM, N_HEADS * LANES), lambda b, i: (b * nq + i, 0, 0)),
                  pl.BlockSpec((1, IDX_DIM, IDX_HEADS * LANES), lambda b, i: (b * nq + i, 0, 0)),
                  pl.BlockSpec((1, IDX_HEADS, LANES), lambda b, i: (b * nq + i, 0, 0)),
                  pl.BlockSpec((TP, HEAD_DIM), lambda b, i: (b, 0)),
                  pl.BlockSpec((TP, IDX_DIM), lambda b, i: (b, 0)),
                  pl.BlockSpec((nq, V_ROWS, LANES), lambda b, i: (b, 0, 0)),
                  pl.BlockSpec((3, ROW_TILE, N_HEADS * ROW_TILE), lambda b, i: (0, 0, 0))],
        out_specs=pl.BlockSpec((ROW_TILE, MIX_W), lambda b, i: (b * nq + i, 0)),
        out_shape=jax.ShapeDtypeStruct((B * TP, MIX_W), BF16),
        scratch_shapes=[pltpu.VMEM((nq + 1, LANES, ROW_TILE), jnp.int32),
                        pltpu.VMEM((2, min(4, nq) * LANES, IDX_HEADS * ROW_TILE), F32),
                        pltpu.VMEM((1, N_HEADS * ROW_TILE), F32),
                        pltpu.VMEM((V_ROWS, N_HEADS * ROW_TILE), F32),
                        pltpu.VMEM((2, min(4, nq), LANES, N_HEADS * ROW_TILE), F32),
                        pltpu.VMEM((2, 1, N_HEADS * ROW_TILE), F32)],
        compiler_params=_cparams("parallel", "arbitrary"),
        name="dsa_attend",
    )(qt, qit, wit, k, ki, vt, bias_tab)


def _layer_norm_rows(z, g, b):
    mu = jnp.mean(z, -1, keepdims=True)
    zc = z - mu
    var = jnp.mean(zc * zc, -1, keepdims=True)
    return zc * lax.rsqrt(var + LN_EPS) * g + b


def _merge_kernel(h_ref, g_ref, ya_ref, yb_ref, yc_ref, yd_ref, wb_ref, wo_ref, lg_ref, lb_ref,
                  h1_ref, h1b_ref):
    merged = None
    for i, y_ref in enumerate((ya_ref, yb_ref, yc_ref, yd_ref)):
        t = g_ref[:, i * D_MODEL:(i + 1) * D_MODEL] * jnp.dot(y_ref[...], wb_ref[i], preferred_element_type=F32)
        merged = t if merged is None else merged + t
    z = DN_ALPHA * h_ref[...] + jnp.dot(merged.astype(BF16), wo_ref[...], preferred_element_type=F32)
    y = _layer_norm_rows(z, lg_ref[...], lb_ref[...])
    h1_ref[...] = y
    h1b_ref[...] = y.astype(BF16)


def _merge(h, gates, ys, w_branch, w_out, ln_g, ln_b):
    N, D = h.shape
    tm = _pick_tile(N, 640)
    full = lambda shape: pl.BlockSpec(shape, lambda i: (0,) * len(shape))
    tok = lambda w: pl.BlockSpec((tm, w), lambda i: (i, 0))
    return pl.pallas_call(
        _merge_kernel,
        grid=(N // tm,),
        in_specs=[tok(D), tok(4 * D), tok(MIX_W), tok(MIX_W), tok(MIX_W), tok(MIX_W),
                  full((4, MIX_W, D)), full((D, D)), full((1, D)), full((1, D))],
        out_specs=[tok(D), tok(D)],
        out_shape=[jax.ShapeDtypeStruct((N, D), F32), jax.ShapeDtypeStruct((N, D), BF16)],
        compiler_params=_cparams("arbitrary"),
        name="merge_out_ln",
    )(h, gates, *ys, w_branch.astype(BF16), w_out.astype(BF16),
      ln_g.reshape(1, D).astype(F32), ln_b.reshape(1, D).astype(F32))


def _moe_kernel(h_ref, hb_ref, wr_ref, br_ref, wg_ref, wu_ref, wd_ref, lg_ref, lb_ref, o_ref, ob_ref,
                gate_ref, acc_ref):
    e = pl.program_id(1)
    xb = hb_ref[...]
    tm = xb.shape[0]
    lane = _iota((tm, LANES), 1)

    @pl.when(e == 0)
    def _():
        logit = jnp.dot(xb, wr_ref[...], preferred_element_type=F32) + br_ref[...]
        big = jnp.int32(LANES)
        gl = jnp.where(lane < N_GROUPS, logit, -jnp.inf)
        gmax = jnp.max(gl, axis=1, keepdims=True)
        g_sel = jnp.min(jnp.where(gl == gmax, lane, big), axis=1, keepdims=True)
        p_grp = 1.0 / jnp.sum(jnp.exp(gl - gmax), axis=1, keepdims=True)
        lo = N_GROUPS + g_sel * EPG
        el = jnp.where((lane >= lo) & (lane < lo + EPG), logit, -jnp.inf)
        v1 = jnp.max(el, axis=1, keepdims=True)
        i1 = jnp.min(jnp.where(el == v1, lane, big), axis=1, keepdims=True)
        el2 = jnp.where(lane == i1, -jnp.inf, el)
        v2 = jnp.max(el2, axis=1, keepdims=True)
        i2 = jnp.min(jnp.where(el2 == v2, lane, big), axis=1, keepdims=True)
        e2 = jnp.exp(v2 - v1)
        w1 = p_grp / (1.0 + e2)
        w2 = p_grp * e2 / (1.0 + e2)
        gate_ref[...] = jnp.where(lane == i1, w1, 0.0) + jnp.where(lane == i2, w2, 0.0)
        acc_ref[...] = jnp.zeros_like(acc_ref)

    g_e = jnp.sum(jnp.where(lane == e + N_GROUPS, gate_ref[...], 0.0), axis=1, keepdims=True)
    hid = _silu(jnp.dot(xb, wg_ref[0], preferred_element_type=F32)) * jnp.dot(xb, wu_ref[0], preferred_element_type=F32)
    acc_ref[...] += g_e * jnp.dot(hid.astype(BF16), wd_ref[0], preferred_element_type=F32)

    @pl.when(e == N_EXPERTS - 1)
    def _():
        y = _layer_norm_rows(DN_ALPHA * h_ref[...] + acc_ref[...], lg_ref[...], lb_ref[...])
        o_ref[...] = y
        ob_ref[...] = y.astype(BF16)


def _moe(h1, h1b, w_grp, b_grp, w_rt, b_rt, w_gate, w_up, w_down, ln_g, ln_b):
    N, D = h1.shape
    tm = _pick_tile(N, 1280)
    w_r = jnp.zeros((D, LANES), F32).at[:, 0:N_GROUPS].set(w_grp).at[:, N_GROUPS:N_GROUPS + N_EXPERTS].set(w_rt)
    b_r = jnp.zeros((1, LANES), F32).at[0, 0:N_GROUPS].set(b_grp).at[0, N_GROUPS:N_GROUPS + N_EXPERTS].set(b_rt)
    full = lambda shape: pl.BlockSpec(shape, lambda i, e: (0,) * len(shape))
    tok = lambda w: pl.BlockSpec((tm, w), lambda i, e: (i, 0))
    return pl.pallas_call(
        _moe_kernel,
        grid=(N // tm, N_EXPERTS),
        in_specs=[tok(D), tok(D), full((D, LANES)), full((1, LANES)),
                  pl.BlockSpec((1, D, D_EXPERT), lambda i, e: (e, 0, 0)),
                  pl.BlockSpec((1, D, D_EXPERT), lambda i, e: (e, 0, 0)),
                  pl.BlockSpec((1, D_EXPERT, D), lambda i, e: (e, 0, 0)),
                  full((1, D)), full((1, D))],
        out_specs=[tok(D), tok(D)],
        out_shape=[jax.ShapeDtypeStruct((N, D), F32), jax.ShapeDtypeStruct((N, D), BF16)],
        scratch_shapes=[pltpu.VMEM((tm, LANES), F32), pltpu.VMEM((tm, D), F32)],
        compiler_params=_cparams("arbitrary", "arbitrary"),
        name="hier_moe_ln",
    )(h1, h1b, w_r.astype(BF16), b_r, w_gate.astype(BF16), w_up.astype(BF16), w_down.astype(BF16),
      ln_g.reshape(1, D).astype(F32), ln_b.reshape(1, D).astype(F32))


def _pad_cols(w, width):
    return jnp.pad(w, ((0, 0), (0, width - w.shape[1])))


def _split_w_in(w):
    o = 0
    w_a = w[:, o:o + 1024]; o += 1024
    gq, gk, gv, ga, gg = (w[:, o:o + 128], w[:, o + 128:o + 256], w[:, o + 256:o + 512],
                          w[:, o + 512:o + 528], w[:, o + 528:o + 784]); o += 784
    w_b = _pad_cols(jnp.concatenate([gq, gk, gv, gg, ga], axis=1), 896)
    cq, ckv, cqi, cki, cwi = (w[:, o:o + 256], w[:, o + 256:o + 384], w[:, o + 384:o + 640],
                              w[:, o + 640:o + 672], w[:, o + 672:o + 680]); o += 680
    w_t = jnp.pad(jnp.concatenate([cq.T, cqi.T, cwi.T], axis=0), ((0, WT_ROWS - 2 * MIX_W - IDX_HEADS), (0, 0)))
    w_n = _pad_cols(jnp.concatenate([ckv, cki], axis=1), 256)
    dq, dk, dv, di, df, do = (w[:, o:o + 256], w[:, o + 256:o + 512], w[:, o + 512:o + 768],
                              w[:, o + 768:o + 772], w[:, o + 772:o + 776], w[:, o + 776:o + 1032]); o += 1032
    w_d = _pad_cols(jnp.concatenate([dq, dk, dv, do, di, df], axis=1), 1152)
    w_g = w[:, o:o + 4096]
    bf = lambda a: a.astype(BF16)
    return bf(w_a), bf(w_b), bf(w_t), bf(w_n), bf(w_d), bf(w_g)


def kernel(x, meta, ln_in_g, ln_in_b, rel_bias, w_in, rwkv_mu, rwkv_w_up, rwkv_w0, rwkv_a_up, rwkv_a0, rwkv_g_up, rwkv_k_k, rwkv_k_a, rwkv_r_k, rwkv_gn_g, rwkv_gn_b, gla_a_up, gla_a_b, gla_norm_g, dsa_kv_norm_g, dsa_w_uk, dsa_w_uv, mlstm_conv_w, mlstm_conv_b, mlstm_i_b, mlstm_f_b, mlstm_norm_g, w_branch, w_out, ln1_g, ln1_b, moe_w_grp, moe_b_grp, moe_w_rt, moe_b_rt, moe_w_gate, moe_w_up, moe_w_down, ln2_g, ln2_b):
    B, S, D = x.shape
    assert D == D_MODEL and S % ROW_TILE == 0
    TP = S + FRONT
    N = B * TP
    topk = min(TOPK_MAX, S // 4)
    bias_tab = _bias_tables(rel_bias)

    h, hb = _embed(x, meta, ln_in_g, ln_in_b)
    h = h.reshape(N, D)
    hb = hb.reshape(N, D)
    for l in range(DEPTH):
        w_a, w_b, w_t, w_n, w_d, w_g = _split_w_in(w_in[l])
        p_a = _proj(hb, w_a)
        p_b = _proj(hb, w_b)
        p_d = _proj(hb, w_d)
        gates = _proj(hb, w_g, act="sigmoid")
        qt, qit, wit, k, ki, vt = _dsa_prep(hb, w_t, w_n, dsa_kv_norm_g[l], dsa_w_uk[l], dsa_w_uv[l])
        y_a = _rwkv(p_a, B, TP, rwkv_mu[l], rwkv_w_up[l], rwkv_w0[l], rwkv_a_up[l], rwkv_a0[l], rwkv_g_up[l],
                    rwkv_k_k[l], rwkv_k_a[l], rwkv_r_k[l], rwkv_gn_g[l], rwkv_gn_b[l])
        y_b = _gla(p_b, B, TP, gla_a_up[l], gla_a_b[l], gla_norm_g[l])
        y_c = _dsa(qt, qit, wit, k, ki, vt, bias_tab, B, TP, topk)
        y_d = _mlstm(p_d, B, TP, mlstm_conv_w[l], mlstm_conv_b[l], mlstm_i_b[l], mlstm_f_b[l], mlstm_norm_g[l])
        ys = (y_a.reshape(N, MIX_W), y_b.reshape(N, MIX_W), y_c, y_d.reshape(N, MIX_W))
        h1, h1b = _merge(h, gates, ys, w_branch[l], w_out[l], ln1_g[l], ln1_b[l])
        h, hb = _moe(h1, h1b, moe_w_grp[l], moe_b_grp[l], moe_w_rt[l], moe_b_rt[l],
                     moe_w_gate[l], moe_w_up[l], moe_w_down[l], ln2_g[l], ln2_b[l])
    return h.reshape(B, TP, D)[:, FRONT:]
```

```python
import functools
import math

import numpy as np
import jax
import jax.numpy as jnp
from jax import lax
from jax.experimental import pallas as pl
from jax.experimental.pallas import tpu as pltpu

F32 = jnp.float32
BF16 = jnp.bfloat16

D_MODEL = 1024
HEAD_DIM = 64
N_HEADS = 4
MIX_W = 256
N_META = 16
CHUNK = 64
LANES = 128
ROW_TILE = 128
FRONT = ROW_TILE
FP = FRONT - N_META
NEG = -1e30
LN_EPS = 1e-5
DEPTH = 2
DN_ALPHA = (2 * DEPTH) ** 0.25

RWKV_GN_EPS = HEAD_DIM * 1e-5
GLA_DK = 32
GLA_TAU = 16.0
DSA_KV_RANK = 128
IDX_HEADS = 8
IDX_DIM = 32
TOPK_MAX = 256
N_BUCKETS = 32
MAX_DISTANCE = 128
CONV_W = 4
N_GROUPS = 4
EPG = 4
N_EXPERTS = 16
D_EXPERT = 256

INT_MIN = -(2 ** 31)
FLT_LOWEST = float(np.finfo(np.float32).min)
KEY_LOWEST = -(2 ** 31) + 0x00800000
VMEM_LIMIT = 56 * 1024 * 1024


def _cparams(*sem):
    return pltpu.CompilerParams(dimension_semantics=tuple(sem), vmem_limit_bytes=VMEM_LIMIT)


def _pick_tile(n, target):
    best = LANES
    t = LANES
    while t <= min(n, target):
        if n % t == 0:
            best = t
        t += LANES
    return best


def _bdot(a, b):
    return jnp.dot(a.astype(BF16), b.astype(BF16), preferred_element_type=F32)


def _bdot_nt(a, b):
    return lax.dot_general(a.astype(BF16), b.astype(BF16), (((1,), (1,)), ((), ())),
                           preferred_element_type=F32)


def _bdot_tn(a, b):
    return lax.dot_general(a.astype(BF16), b.astype(BF16), (((0,), (0,)), ((), ())),
                           preferred_element_type=F32)


def _split(a):
    hi = a.astype(BF16)
    lo = (a - hi.astype(F32)).astype(BF16)
    return hi, lo


_NN = (((1,), (0,)), ((), ()))
_NT = (((1,), (1,)), ((), ()))
_TN = (((0,), (0,)), ((), ()))


def _dot3(a, b, dims=_NN):
    ah, al = _split(a)
    bh, bl = _split(b)
    dg = lambda x, y: lax.dot_general(x, y, dims, preferred_element_type=F32)
    return dg(ah, bh) + (dg(ah, bl) + dg(al, bh))


def _dot_exact_lhs(a_bf16, b):
    bh, bl = _split(b)
    return (jnp.dot(a_bf16, bh, preferred_element_type=F32)
            + jnp.dot(a_bf16, bl, preferred_element_type=F32))


def _dot_exact_rhs(a, b_bf16):
    ah, al = _split(a)
    return (jnp.dot(ah, b_bf16, preferred_element_type=F32)
            + jnp.dot(al, b_bf16, preferred_element_type=F32))


def _sigmoid(x):
    return 1.0 / (1.0 + jnp.exp(-x))


def _log_sigmoid(x):
    return jnp.minimum(x, 0.0) - jnp.log(1.0 + jnp.exp(-jnp.abs(x)))


def _silu(x):
    return x * _sigmoid(x)


def _iota(shape, dim):
    return lax.broadcasted_iota(jnp.int32, shape, dim)


def _tri_incl(n):
    return (_iota((n, n), 1) <= _iota((n, n), 0))


def _head_ones():
    return ((_iota((MIX_W, MIX_W), 0) // HEAD_DIM) == (_iota((MIX_W, MIX_W), 1) // HEAD_DIM)).astype(BF16)


def _row_ids(rows):
    return pl.program_id(1) * ROW_TILE + _iota((rows, 1), 0)


def _embed_kernel(x_ref, meta_ref, g_ref, b_ref, h_ref, hb_ref):
    j = pl.program_id(1)
    src = jnp.where(j == 0, meta_ref[...], x_ref[0])
    mu = jnp.mean(src, -1, keepdims=True)
    xc = src - mu
    var = jnp.mean(xc * xc, -1, keepdims=True)
    y = xc * lax.rsqrt(var + LN_EPS) * g_ref[...] + b_ref[...]
    h_ref[0] = y
    hb_ref[0] = y.astype(BF16)


def _embed(x, meta, g, b):
    B, S, D = x.shape
    TP = S + FRONT
    meta_pad = jnp.concatenate([jnp.zeros((FP, D), F32), meta.astype(F32)], axis=0)
    return pl.pallas_call(
        _embed_kernel,
        grid=(B, TP // ROW_TILE),
        in_specs=[
            pl.BlockSpec((1, ROW_TILE, D), lambda b, j: (b, jnp.maximum(j - 1, 0), 0)),
            pl.BlockSpec((ROW_TILE, D), lambda b, j: (0, 0)),
            pl.BlockSpec((1, D), lambda b, j: (0, 0)),
            pl.BlockSpec((1, D), lambda b, j: (0, 0)),
        ],
        out_specs=[
            pl.BlockSpec((1, ROW_TILE, D), lambda b, j: (b, j, 0)),
            pl.BlockSpec((1, ROW_TILE, D), lambda b, j: (b, j, 0)),
        ],
        out_shape=[jax.ShapeDtypeStruct((B, TP, D), F32), jax.ShapeDtypeStruct((B, TP, D), BF16)],
        compiler_params=_cparams("parallel", "arbitrary"),
        name="embed_ln",
    )(x, meta_pad, g.reshape(1, D), b.reshape(1, D))


def _proj_kernel(h_ref, w_ref, o_ref, *, act):
    y = jnp.dot(h_ref[...], w_ref[...], preferred_element_type=F32)
    if act == "sigmoid":
        y = _sigmoid(y)
    o_ref[...] = y.astype(o_ref.dtype)


def _proj(hb, w, act=None, out_dtype=F32):
    N, D = hb.shape
    W = w.shape[1]
    tn = W if W <= 1152 else 1024
    tm = _pick_tile(N, 1280)
    return pl.pallas_call(
        functools.partial(_proj_kernel, act=act),
        grid=(W // tn, N // tm),
        in_specs=[pl.BlockSpec((tm, D), lambda j, i: (i, 0)),
                  pl.BlockSpec((D, tn), lambda j, i: (0, j))],
        out_specs=pl.BlockSpec((tm, tn), lambda j, i: (i, j)),
        out_shape=jax.ShapeDtypeStruct((N, W), out_dtype),
        compiler_params=_cparams("arbitrary", "arbitrary"),
        name="in_proj",
    )(hb, w)


def _rwkv_kernel(p_ref, mu_ref, wup_ref, w0_ref, aup_ref, a0_ref, gup_ref, kk_ref, ka_ref, rk_ref,
                 gng_ref, gnb_ref, y_ref, carry_ref, s_ref):
    j = pl.program_id(0)
    nb = p_ref.shape[0]
    n_chunks = ROW_TILE // CHUNK

    @pl.when(j == 0)
    def _():
        carry_ref[...] = jnp.zeros_like(carry_ref)
        s_ref[...] = jnp.zeros_like(s_ref)

    valid = (j * ROW_TILE + _iota((ROW_TILE, 1), 0)) >= FP
    first_row = _iota((ROW_TILE, 1), 0) == 0
    ones_h = _head_ones()
    tri = _tri_incl(CHUNK)
    tri_b = tri.astype(BF16)
    strict = _iota((CHUNK, CHUNK), 1) < _iota((CHUNK, CHUNK), 0)
    eye = (_iota((CHUNK, CHUNK), 1) == _iota((CHUNK, CHUNK), 0)).astype(F32)
    heads = [slice(h * HEAD_DIM, (h + 1) * HEAD_DIM) for h in range(N_HEADS)]

    pro = []
    unit = {}
    for b in range(nb):
        p = jnp.where(valid, p_ref[b], 0.0)
        prev = jnp.where(first_row, carry_ref[b], pltpu.roll(p, 1, 0))
        carry_ref[b] = p[ROW_TILE - 1:ROW_TILE, :]
        ps = p + (prev - p) * mu_ref[...]
        r = ps[:, 0:256]
        k = ps[:, 256:512]
        v = ps[:, 512:768]
        lora_in = ps[:, 768:896]
        xg = ps[:, 896:1024]
        w_log = _log_sigmoid(w0_ref[...] + _bdot(jnp.tanh(lora_in), wup_ref[...])) - 0.5
        lw = jnp.where(valid, -jnp.exp(w_log), 0.0)
        alpha = _sigmoid(a0_ref[...] + _bdot(lora_in, aup_ref[...]))
        gate = _bdot(_sigmoid(xg), gup_ref[...])
        kk = k * kk_ref[...]
        kk = kk / jnp.maximum(jnp.sqrt(_dot_exact_rhs(kk * kk, ones_h)), 1e-12)
        k = k * (1.0 + (alpha - 1.0) * ka_ref[...])
        kka = kk * alpha
        pro.append((r, k, v, gate))
        for c in range(n_chunks):
            sl = slice(c * CHUNK, (c + 1) * CHUNK)
            lw_c = lw[sl]
            cum = _dot_exact_lhs(tri_b, lw_c)
            cum_last = cum[CHUNK - 1:CHUNK, :]
            p_inv = jnp.exp(-cum)
            p_tail = jnp.exp(cum_last - cum)
            unit[b, c] = dict(a=-kk[sl] * jnp.exp(cum - lw_c), b=kka[sl] * p_inv, k=k[sl] * p_inv,
                              r=r[sl] * jnp.exp(cum), kb=k[sl] * p_tail, bb=kka[sl] * p_tail,
                              pl=jnp.exp(cum_last), v=v[sl])

    keys = [(b, c, h) for b in range(nb) for c in range(n_chunks) for h in range(N_HEADS)]
    part = lambda name, key: unit[key[0], key[1]][name][:, heads[key[2]]]
    a_ab = {q: jnp.where(strict, _dot3(part("a", q), part("b", q), _NT), 0.0) for q in keys}
    a_ak = {q: jnp.where(strict, _bdot_nt(part("a", q), part("k", q)), 0.0) for q in keys}
    a_rb = {q: jnp.where(tri, _bdot_nt(part("r", q), part("b", q)), 0.0) for q in keys}
    a_rk = {q: jnp.where(tri, _bdot_nt(part("r", q), part("k", q)), 0.0) for q in keys}
    inv = {q: eye + a_ab[q] for q in keys}
    pw = a_ab
    for _ in range(5):
        pw = {q: _bdot(pw[q], pw[q]) for q in keys}
        inv = {q: inv[q] + _bdot(inv[q], pw[q]) for q in keys}
    ak_v = {q: _bdot(a_ak[q], part("v", q)) for q in keys}
    rk_v = {q: _bdot(a_rk[q], part("v", q)) for q in keys}
    kb_v = {q: _bdot_tn(part("v", q), part("kb", q)) for q in keys}

    bh = [(b, h) for b in range(nb) for h in range(N_HEADS)]
    state = {q: s_ref[q[0], q[1]] for q in bh}
    y_parts = {}
    for c in range(n_chunks):
        full = lambda q: (q[0], c, q[1])
        a_s = {q: _bdot_nt(part("a", full(q)), state[q]) for q in bh}
        r_s = {q: _bdot_nt(part("r", full(q)), state[q]) for q in bh}
        u = {q: _bdot(inv[full(q)], a_s[q] + ak_v[full(q)]) for q in bh}
        for q in bh:
            y_parts[full(q)] = r_s[q] + rk_v[full(q)] + _bdot(a_rb[full(q)], u[q])
        state = {q: (state[q] * part("pl", full(q)) + kb_v[full(q)] + _bdot_tn(u[q], part("bb", full(q))))
                 for q in bh}
    for q in bh:
        s_ref[q[0], q[1]] = state[q]

    for b in range(nb):
        r, k, v, gate = pro[b]
        y = jnp.concatenate([jnp.concatenate([y_parts[b, c, h] for h in range(N_HEADS)], axis=1)
                             for c in range(n_chunks)], axis=0)
        mean = _dot_exact_rhs(y, ones_h) * (1.0 / HEAD_DIM)
        yc = y - mean
        var = _dot_exact_rhs(yc * yc, ones_h) * (1.0 / HEAD_DIM)
        yn = yc * lax.rsqrt(var + RWKV_GN_EPS) * gng_ref[...] + gnb_ref[...]
        bonus = _dot_exact_rhs(r * k * rk_ref[...], ones_h) * v
        y_ref[b] = ((yn + bonus) * gate).astype(y_ref.dtype)


def _rwkv(p_a, B, TP, mu, w_up, w0, a_up, a0, g_up, k_k, k_a, r_k, gn_g, gn_b):
    W = MIX_W
    z64 = jnp.zeros((64, W), F32)
    wup_pad = jnp.concatenate([w_up, z64], axis=0).astype(BF16)
    aup_pad = jnp.concatenate([z64, a_up], axis=0).astype(BF16)
    row = lambda a: a.reshape(1, -1).astype(F32)
    full = lambda shape: pl.BlockSpec(shape, lambda j: (0,) * len(shape))
    return pl.pallas_call(
        _rwkv_kernel,
        grid=(TP // ROW_TILE,),
        in_specs=[pl.BlockSpec((B, ROW_TILE, 1024), lambda j: (0, j, 0)),
                  full((1, 1024)), full((128, W)), full((1, W)), full((128, W)), full((1, W)),
                  full((128, W)), full((1, W)), full((1, W)), full((1, W)), full((1, W)), full((1, W))],
        out_specs=pl.BlockSpec((B, ROW_TILE, W), lambda j: (0, j, 0)),
        out_shape=jax.ShapeDtypeStruct((B, TP, W), BF16),
        scratch_shapes=[pltpu.VMEM((B, 1, 1024), F32), pltpu.VMEM((B, N_HEADS, HEAD_DIM, HEAD_DIM), F32)],
        compiler_params=_cparams("arbitrary"),
        name="rwkv7",
    )(p_a.reshape(B, TP, 1024), row(mu), wup_pad, row(w0), aup_pad, row(a0), g_up.astype(BF16),
      row(k_k), row(k_a), row(r_k), row(gn_g), row(gn_b))


def _gla_kernel(p_ref, aup_ref, ab_ref, ng_ref, y_ref, s_ref):
    j = pl.program_id(0)
    nb = p_ref.shape[0]
    n_chunks = ROW_TILE // CHUNK

    @pl.when(j == 0)
    def _():
        s_ref[...] = jnp.zeros_like(s_ref)

    valid = (j * ROW_TILE + _iota((ROW_TILE, 1), 0)) >= FP
    tri = _tri_incl(CHUNK)
    tri_b = tri.astype(BF16)

    og_all = []
    pre = {}
    for b in range(nb):
        p = jnp.where(valid, p_ref[b], 0.0)
        la = _log_sigmoid(_bdot(p[:, 768:896], aup_ref[...]) + ab_ref[...]) * (1.0 / GLA_TAU)
        la = jnp.where(valid, la, 0.0)
        og_all.append(p[:, 512:768])
        for c in range(n_chunks):
            sl = slice(c * CHUNK, (c + 1) * CHUNK)
            pre[b, c] = dict(q=p[sl, 0:128] * (GLA_DK ** -0.5), k=p[sl, 128:256], v=p[sl, 256:512], la=la[sl])
    bc = [(b, c) for b in range(nb) for c in range(n_chunks)]
    keys = [(b, c, h) for (b, c) in bc for h in range(N_HEADS)]
    ks = [slice(h * GLA_DK, (h + 1) * GLA_DK) for h in range(N_HEADS)]
    vs = [slice(h * HEAD_DIM, (h + 1) * HEAD_DIM) for h in range(N_HEADS)]
    b_cum = {u: _dot_exact_lhs(tri_b, pre[u]["la"]) for u in bc}
    b_last = {u: b_cum[u][CHUNK - 1:CHUNK, :] for u in bc}
    q_g = {u: pre[u]["q"] * jnp.exp(b_cum[u]) for u in bc}
    k_g = {u: pre[u]["k"] * jnp.exp(-b_cum[u]) for u in bc}
    k_l = {u: pre[u]["k"] * jnp.exp(b_last[u] - b_cum[u]) for u in bc}
    dec = {u: jnp.exp(b_last[u]) for u in bc}
    att = {u: jnp.where(tri, _bdot_nt(q_g[u[0], u[1]][:, ks[u[2]]], k_g[u[0], u[1]][:, ks[u[2]]]), 0.0)
           for u in keys}
    att_v = {u: _bdot(att[u], pre[u[0], u[1]]["v"][:, vs[u[2]]]) for u in keys}
    kl_v = {u: _bdot_tn(pre[u[0], u[1]]["v"][:, vs[u[2]]], k_l[u[0], u[1]][:, ks[u[2]]]) for u in keys}

    bh = [(b, h) for b in range(nb) for h in range(N_HEADS)]
    state = {q: s_ref[q[0], q[1]] for q in bh}
    o_parts = {}
    for c in range(n_chunks):
        for q in bh:
            o_parts[q[0], c, q[1]] = att_v[q[0], c, q[1]] + _bdot_nt(q_g[q[0], c][:, ks[q[1]]], state[q])
        state = {q: state[q] * dec[q[0], c][:, ks[q[1]]] + kl_v[q[0], c, q[1]] for q in bh}
    for q in bh:
        s_ref[q[0], q[1]] = state[q]

    ones_h = _head_ones()
    for b in range(nb):
        o = jnp.concatenate([jnp.concatenate([o_parts[b, c, h] for h in range(N_HEADS)], axis=1)
                             for c in range(n_chunks)], axis=0)
        ms = _dot_exact_rhs(o * o, ones_h) * (1.0 / HEAD_DIM)
        y = o * lax.rsqrt(ms + 1e-6) * ng_ref[...] * _silu(og_all[b])
        y_ref[b] = y.astype(y_ref.dtype)


def _gla(p_b, B, TP, a_up, a_b, norm_g):
    aup_pad = jnp.zeros((128, 128), F32).at[:a_up.shape[0]].set(a_up).astype(BF16)
    full = lambda shape: pl.BlockSpec(shape, lambda j: (0,) * len(shape))
    return pl.pallas_call(
        _gla_kernel,
        grid=(TP // ROW_TILE,),
        in_specs=[pl.BlockSpec((B, ROW_TILE, 896), lambda j: (0, j, 0)),
                  full((128, 128)), full((1, 128)), full((1, MIX_W))],
        out_specs=pl.BlockSpec((B, ROW_TILE, MIX_W), lambda j: (0, j, 0)),
        out_shape=jax.ShapeDtypeStruct((B, TP, MIX_W), BF16),
        scratch_shapes=[pltpu.VMEM((B, N_HEADS, HEAD_DIM, GLA_DK), F32)],
        compiler_params=_cparams("arbitrary"),
        name="gla",
    )(p_b.reshape(B, TP, 896), aup_pad, a_b.reshape(1, 128).astype(F32),
      jnp.tile(norm_g.astype(F32), N_HEADS).reshape(1, MIX_W))


def _mlstm_kernel(p_ref, cw_ref, cb_ref, ib_ref, fb_ref, ng_ref, y_ref, carry_ref, c_ref, n_ref, m_ref):
    j = pl.program_id(0)
    nb = p_ref.shape[0]
    n_chunks = ROW_TILE // CHUNK

    @pl.when(j == 0)
    def _():
        carry_ref[...] = jnp.zeros_like(carry_ref)
        c_ref[...] = jnp.zeros_like(c_ref)
        n_ref[...] = jnp.zeros_like(n_ref)
        m_ref[...] = jnp.zeros_like(m_ref)

    valid = (j * ROW_TILE + _iota((ROW_TILE, 1), 0)) >= FP
    tri = _tri_incl(CHUNK)
    tri_b = tri.astype(BF16)
    ones_h = _head_ones()

    og_all = []
    pre = {}
    for b in range(nb):
        p = jnp.where(valid, p_ref[b], 0.0)
        a = p[:, 0:512]
        ext = jnp.concatenate([carry_ref[b], a], axis=0)
        carry_ref[b] = a[ROW_TILE - 8:ROW_TILE, :]
        conv = cb_ref[...] + a * cw_ref[CONV_W - 1:CONV_W, :]
        for s in range(1, CONV_W):
            conv = conv + pltpu.roll(ext, s, 0)[8:8 + ROW_TILE, :] * cw_ref[CONV_W - 1 - s:CONV_W - s, :]
        qk = _silu(conv)
        q = jnp.where(valid, qk[:, 0:MIX_W], 0.0)
        k = jnp.where(valid, qk[:, MIX_W:2 * MIX_W], 0.0) * (HEAD_DIM ** -0.5)
        v = p[:, 512:768]
        og_all.append(p[:, 768:1024])
        gates = p[:, 1024:1152]
        li_all = jnp.where(valid, gates + ib_ref[...], NEG)
        lf_all = jnp.where(valid, _log_sigmoid(gates + fb_ref[...]), 0.0)
        for c in range(n_chunks):
            sl = slice(c * CHUNK, (c + 1) * CHUNK)
            pre[b, c] = dict(q=q[sl], k=k[sl], v=v[sl], li=li_all[sl], lf=lf_all[sl])

    bc = [(b, c) for b in range(nb) for c in range(n_chunks)]
    keys = [(b, c, h) for (b, c) in bc for h in range(N_HEADS)]
    heads = [slice(h * HEAD_DIM, (h + 1) * HEAD_DIM) for h in range(N_HEADS)]
    part = lambda name, u: pre[u[0], u[1]][name][:, heads[u[2]]]
    b_cum = {u: _dot_exact_lhs(tri_b, pre[u]["lf"]) for u in bc}
    b_t = {u: b_cum[u].T for u in bc}
    li_t = {u: pre[u]["li"].T for u in bc}
    b_col = {u: b_cum[u[0], u[1]][:, N_HEADS + u[2]:N_HEADS + u[2] + 1] for u in keys}
    b_last = {u: b_col[u][CHUNK - 1:CHUNK, :] for u in keys}
    d_log = {u: jnp.where(tri, b_col[u] - b_t[u[0], u[1]][N_HEADS + u[2]:N_HEADS + u[2] + 1, :]
                          + li_t[u[0], u[1]][u[2]:u[2] + 1, :], -jnp.inf) for u in keys}
    dmax = {u: jnp.max(d_log[u], axis=1, keepdims=True) for u in keys}
    qk = {u: _bdot_nt(part("q", u), part("k", u)) for u in keys}
    s0 = {u: jnp.exp(d_log[u] - dmax[u]) * qk[u] for u in keys}
    sv = {u: _bdot(s0[u], part("v", u)) for u in keys}
    ssum = {u: jnp.sum(s0[u], axis=1, keepdims=True) for u in keys}
    g_loc = {u: b_last[u] - b_col[u] + pre[u[0], u[1]]["li"][:, u[2]:u[2] + 1] for u in keys}
    m_loc = {u: jnp.max(g_loc[u], axis=0, keepdims=True) for u in keys}
    kw = {u: part("k", u) * jnp.exp(g_loc[u] - m_loc[u]) for u in keys}
    kwv = {u: _bdot_tn(kw[u], part("v", u)) for u in keys}
    kwsum = {u: jnp.sum(kw[u], axis=0, keepdims=True) for u in keys}

    bh = [(b, h) for b in range(nb) for h in range(N_HEADS)]
    c_st = {q: c_ref[q[0], q[1]] for q in bh}
    n_st = {q: n_ref[q[0], q[1]] for q in bh}
    m_st = {q: m_ref[q[0], q[1]] for q in bh}
    h_parts = {}
    for c in range(n_chunks):
        full = lambda q: (q[0], c, q[1])
        qc = {q: _bdot(part("q", full(q)), c_st[q]) for q in bh}
        qn = {q: jnp.sum(part("q", full(q)) * n_st[q], axis=1, keepdims=True) for q in bh}
        inter = {q: b_col[full(q)] + m_st[q] for q in bh}
        m_t = {q: jnp.maximum(inter[q], dmax[full(q)]) for q in bh}
        e_loc = {q: jnp.exp(dmax[full(q)] - m_t[q]) for q in bh}
        w_int = {q: jnp.exp(inter[q] - m_t[q]) for q in bh}
        for q in bh:
            num = e_loc[q] * sv[full(q)] + w_int[q] * qc[q]
            den = e_loc[q] * ssum[full(q)] + w_int[q] * qn[q]
            h_parts[full(q)] = num / jnp.maximum(jnp.abs(den), jnp.exp(-m_t[q]))
        m_new = {q: jnp.maximum(b_last[full(q)] + m_st[q], m_loc[full(q)]) for q in bh}
        s_old = {q: jnp.exp(b_last[full(q)] + m_st[q] - m_new[q]) for q in bh}
        s_new = {q: jnp.exp(m_loc[full(q)] - m_new[q]) for q in bh}
        c_st = {q: s_old[q] * c_st[q] + s_new[q] * kwv[full(q)] for q in bh}
        n_st = {q: s_old[q] * n_st[q] + s_new[q] * kwsum[full(q)] for q in bh}
        m_st = m_new
    for q in bh:
        c_ref[q[0], q[1]], n_ref[q[0], q[1]], m_ref[q[0], q[1]] = c_st[q], n_st[q], m_st[q]

    for b in range(nb):
        hh = jnp.concatenate([jnp.concatenate([h_parts[b, c, h] for h in range(N_HEADS)], axis=1)
                              for c in range(n_chunks)], axis=0) * _sigmoid(og_all[b])
        mean = _dot_exact_rhs(hh, ones_h) * (1.0 / HEAD_DIM)
        hc = hh - mean
        var = _dot_exact_rhs(hc * hc, ones_h) * (1.0 / HEAD_DIM)
        y_ref[b] = (hc * lax.rsqrt(var + 1e-5) * ng_ref[...]).astype(y_ref.dtype)


def _mlstm(p_d, B, TP, conv_w, conv_b, i_b, f_b, norm_g):
    ib = jnp.zeros((1, LANES), F32).at[0, 0:N_HEADS].set(i_b)
    fb = jnp.zeros((1, LANES), F32).at[0, N_HEADS:2 * N_HEADS].set(f_b)
    full = lambda shape: pl.BlockSpec(shape, lambda j: (0,) * len(shape))
    return pl.pallas_call(
        _mlstm_kernel,
        grid=(TP // ROW_TILE,),
        in_specs=[pl.BlockSpec((B, ROW_TILE, 1152), lambda j: (0, j, 0)),
                  full((CONV_W, 512)), full((1, 512)), full((1, LANES)), full((1, LANES)), full((1, MIX_W))],
        out_specs=pl.BlockSpec((B, ROW_TILE, MIX_W), lambda j: (0, j, 0)),
        out_shape=jax.ShapeDtypeStruct((B, TP, MIX_W), BF16),
        scratch_shapes=[pltpu.VMEM((B, 8, 512), F32),
                        pltpu.VMEM((B, N_HEADS, HEAD_DIM, HEAD_DIM), F32),
                        pltpu.VMEM((B, N_HEADS, 1, HEAD_DIM), F32),
                        pltpu.VMEM((B, N_HEADS, 1, 1), F32)],
        compiler_params=_cparams("arbitrary"),
        name="mlstm",
    )(p_d.reshape(B, TP, 1152), conv_w.astype(F32), conv_b.reshape(1, 512).astype(F32), ib, fb,
      norm_g.reshape(1, MIX_W).astype(F32))


V_ROWS = 80
WT_ROWS = 528


def _dsa_prep_kernel(h_ref, wt_ref, wn_ref, kvg_ref, wuk_ref, wuvt_ref,
                     qt_ref, qit_ref, wit_ref, k_ref, ki_ref, vt_ref):
    hb = h_ref[...]
    tm = hb.shape[0]
    pt = lax.dot_general(wt_ref[...], hb, _NT, preferred_element_type=F32)
    pn = jnp.dot(hb, wn_ref[...], preferred_element_type=F32)
    ckv = pn[:, 0:DSA_KV_RANK]
    c = ckv * lax.rsqrt(jnp.mean(ckv * ckv, -1, keepdims=True) + 1e-6) * kvg_ref[...]
    cb = c.astype(BF16)
    k_ref[...] = jnp.dot(cb, wuk_ref[...], preferred_element_type=F32).astype(BF16)
    ki_ref[...] = pn[:, DSA_KV_RANK:DSA_KV_RANK + IDX_DIM].astype(BF16)
    vt = lax.dot_general(wuvt_ref[...], cb, _NT, preferred_element_type=F32)
    vt = jnp.where(_iota((V_ROWS, tm), 0) == HEAD_DIM, 1.0, vt)
    for t in range(tm // LANES):
        cs = slice(t * LANES, (t + 1) * LANES)
        for h in range(N_HEADS):
            qt_ref[t, :, h * LANES:(h + 1) * LANES] = (
                pt[h * HEAD_DIM:(h + 1) * HEAD_DIM, cs] * (HEAD_DIM ** -0.5)).astype(BF16)
        for h in range(IDX_HEADS):
            qit_ref[t, :, h * LANES:(h + 1) * LANES] = pt[MIX_W + h * IDX_DIM:MIX_W + (h + 1) * IDX_DIM, cs].astype(BF16)
        wit_ref[t] = pt[2 * MIX_W:2 * MIX_W + IDX_HEADS, cs] * ((IDX_HEADS * IDX_DIM) ** -0.5)
        vt_ref[t] = vt[:, cs].astype(BF16)


def _dsa_prep(hb, w_t, w_n, kv_norm_g, w_uk, w_uv):
    N, D = hb.shape
    tm = _pick_tile(N, 640)
    nt = tm // LANES
    full = lambda shape: pl.BlockSpec(shape, lambda i: (0,) * len(shape))
    wuvt = jnp.pad(w_uv.T, ((0, V_ROWS - HEAD_DIM), (0, 0))).astype(BF16)
    return pl.pallas_call(
        _dsa_prep_kernel,
        grid=(N // tm,),
        in_specs=[pl.BlockSpec((tm, D), lambda i: (i, 0)),
                  full((WT_ROWS, D)), full((D, 256)), full((1, DSA_KV_RANK)),
                  full((DSA_KV_RANK, HEAD_DIM)), full((V_ROWS, DSA_KV_RANK))],
        out_specs=[pl.BlockSpec((nt, HEAD_DIM, N_HEADS * LANES), lambda i: (i, 0, 0)),
                   pl.BlockSpec((nt, IDX_DIM, IDX_HEADS * LANES), lambda i: (i, 0, 0)),
                   pl.BlockSpec((nt, IDX_HEADS, LANES), lambda i: (i, 0, 0)),
                   pl.BlockSpec((tm, HEAD_DIM), lambda i: (i, 0)),
                   pl.BlockSpec((tm, IDX_DIM), lambda i: (i, 0)),
                   pl.BlockSpec((nt, V_ROWS, LANES), lambda i: (i, 0, 0))],
        out_shape=[jax.ShapeDtypeStruct((N // LANES, HEAD_DIM, N_HEADS * LANES), BF16),
                   jax.ShapeDtypeStruct((N // LANES, IDX_DIM, IDX_HEADS * LANES), BF16),
                   jax.ShapeDtypeStruct((N // LANES, IDX_HEADS, LANES), F32),
                   jax.ShapeDtypeStruct((N, HEAD_DIM), BF16),
                   jax.ShapeDtypeStruct((N, IDX_DIM), BF16),
                   jax.ShapeDtypeStruct((N // LANES, V_ROWS, LANES), BF16)],
        compiler_params=_cparams("arbitrary"),
        name="dsa_prep",
    )(hb, w_t, w_n, kv_norm_g.reshape(1, DSA_KV_RANK).astype(F32), w_uk.astype(BF16), wuvt)


def _dsa_kernel(qt_ref, qit_ref, wit_ref, k_ref, ki_ref, vt_ref, bias_ref, y_ref,
                sk_ref, rel_ref, m_ref, acc_ref, lg_ref, mg_ref, *, topk):
    i = pl.program_id(1)
    nk = i + 1
    QT = ROW_TILE
    HQ = N_HEADS * QT
    t_lane = i * QT + _iota((LANES, QT), 1)
    key_pos = lambda kt: kt * LANES + _iota((LANES, QT), 0)
    per_head = lambda fn: jnp.concatenate([fn(slice(h * QT, (h + 1) * QT)) for h in range(N_HEADS)], axis=1)

    qit = qit_ref[0]
    wit = wit_ref[0]

    GW = rel_ref.shape[1] // LANES
    n_tiles = sk_ref.shape[0] - 2
    n_trips = (i + 2 * GW) // (2 * GW)

    def group_base(g):
        return jnp.clip(GW * g, 0, n_tiles - GW)

    def issue(g, slot):
        span = pl.ds(pl.multiple_of(group_base(g) * LANES, LANES), GW * LANES)
        rel_ref[slot] = jnp.dot(ki_ref[span, :], qit, preferred_element_type=F32)

    def reduce(g, slot):
        for u in range(GW):
            kt = group_base(g) + u
            rows_u = slice(u * LANES, (u + 1) * LANES)
            score = jnp.maximum(rel_ref[slot, rows_u, 0:QT], 0.0) * wit[0:1, :]
            for h in range(1, IDX_HEADS):
                score = score + jnp.maximum(rel_ref[slot, rows_u, h * QT:(h + 1) * QT], 0.0) * wit[h:h + 1, :]
            mine = (kt >= GW * g) & (kt <= i)
            sk_ref[jnp.where(mine, kt, n_tiles)] = score

    issue(0, 0)

    def score_body(jj, c):
        issue(2 * jj + 1, 1)
        reduce(2 * jj, 0)
        issue(2 * jj + 2, 0)
        reduce(2 * jj + 1, 1)
        return c

    lax.fori_loop(0, n_trips, score_body, 0)
    first = sk_ref[0]
    first = jnp.where(key_pos(0) < FP + N_META, jnp.inf, first)
    sk_ref[0] = jnp.where(key_pos(0) >= FP, first, -jnp.inf)
    sk_ref[i] = jnp.where(key_pos(i) <= t_lane, sk_ref[i], -jnp.inf)
    sk_ref[n_tiles + 1] = jnp.full((LANES, QT), -jnp.inf, F32)

    def key_to_float(key):
        return lax.bitcast_convert_type(jnp.where(key < 0, key ^ jnp.int32(0x7FFFFFFF), key), F32)
    def count(pred_fn):
        def body(kt, acc):
            return acc + jnp.where(pred_fn(sk_ref[kt], kt), 1, 0)

        def body4(j, acc):
            for u in range(4):
                acc = body(4 * j + u, acc)
            return acc

        n4 = lax.shift_right_logical(nk, 2)
        acc = lax.fori_loop(0, n4, body4, jnp.zeros((LANES, QT), jnp.int32))
        acc = lax.fori_loop(4 * n4, nk, body, acc)
        return jnp.sum(acc, axis=0, keepdims=True)

    def bit_body(it, carry):
        tau, n_ge = carry
        cand = tau + jnp.left_shift(jnp.int32(1), 31 - it)
        cand_f = key_to_float(cand)
        cnt = count(lambda sk, kt: sk >= cand_f)
        return jnp.where(cnt >= topk, cand, tau), jnp.where(cnt >= topk, cnt, n_ge)

    tau_key, n_ge = lax.fori_loop(0, 32, bit_body, (jnp.full((1, QT), INT_MIN, jnp.int32),
                                                    jnp.zeros((1, QT), jnp.int32)))
    tau = jnp.where(tau_key < KEY_LOWEST, jnp.float32(FLT_LOWEST), key_to_float(tau_key))

    @pl.when(jnp.max(n_ge - topk) > 0)
    def _():
        n_bits = max(1, int(math.ceil(math.log2(sk_ref.shape[0] * LANES + 1))))
        need = topk - count(lambda sk, kt: sk > tau)

        def pos_body(it, x):
            cand = x + jnp.left_shift(jnp.int32(1), n_bits - 1 - it)
            cnt = count(lambda sk, kt: (sk == tau) & (key_pos(kt) < cand))
            return jnp.where(cnt < need, cand, x)

        x = lax.fori_loop(0, n_bits, pos_body, jnp.zeros((1, QT), jnp.int32))
        jmax = jnp.where(n_ge > topk, x, jnp.int32(2 ** 30))

        def drop_body(kt, c):
            sk = sk_ref[kt]
            sk_ref[kt] = jnp.where((sk == tau) & (key_pos(kt) > jmax), -jnp.inf, sk)
            return c

        lax.fori_loop(0, nk, drop_body, 0)

    qt = qt_ref[0]
    m_ref[...] = jnp.full((1, HQ), NEG, F32)
    acc_ref[...] = jnp.zeros((V_ROWS, HQ), F32)

    def park(g, slot):
        base = group_base(g)
        span = pl.ds(pl.multiple_of(base * LANES, LANES), GW * LANES)
        lg_all = jnp.dot(k_ref[span, :], qt, preferred_element_type=F32)
        tmax = None
        for u in range(GW):
            t = base + u
            mine = (t >= GW * g) & (t <= i)
            lg = lg_all[u * LANES:(u + 1) * LANES, :] + bias_ref[jnp.clip(i - t, 0, 2)]
            sel = sk_ref[jnp.where(mine, t, n_tiles + 1)] >= tau
            lgm = per_head(lambda hs: jnp.where(sel, lg[:, hs], NEG))
            lg_ref[slot, u] = lgm
            tmax = lgm if tmax is None else jnp.maximum(tmax, lgm)
        mg_ref[slot] = jnp.max(tmax, axis=0, keepdims=True)

    def weights(slot):
        m_old = m_ref[...]
        m_new = jnp.maximum(m_old, mg_ref[slot])
        m_ref[...] = m_new
        return jnp.exp(m_old - m_new), [jnp.exp(lg_ref[slot, u] - m_new).astype(BF16) for u in range(GW)]

    def fold(g, corr, prs):
        vt_all = jnp.concatenate([vt_ref[group_base(g) + u] for u in range(GW)], axis=1)
        pv = jnp.dot(vt_all, jnp.concatenate(prs, axis=0), preferred_element_type=F32)
        acc_ref[...] = acc_ref[...] * corr + pv

    park(0, 0)

    def pipe_body(jj, c):
        corr, prs = weights(0)
        park(2 * jj + 1, 1)
        fold(2 * jj, corr, prs)
        corr, prs = weights(1)
        park(2 * jj + 2, 0)
        fold(2 * jj + 1, corr, prs)
        return c

    lax.fori_loop(0, n_trips, pipe_body, 0)
    acc = acc_ref[...]
    out = acc[0:HEAD_DIM, :] / jnp.maximum(acc[HEAD_DIM:HEAD_DIM + 1, :], 1e-30)
    y_ref[...] = per_head(lambda hs: out[:, hs].T).astype(y_ref.dtype)


def _t5_bucket(dist):
    max_exact = N_BUCKETS // 2
    n = jnp.maximum(dist, 0)
    large = max_exact + (jnp.log(jnp.maximum(n, 1).astype(F32) / max_exact)
                         / math.log(MAX_DISTANCE / max_exact) * (N_BUCKETS - max_exact)).astype(jnp.int32)
    return jnp.where(n < max_exact, n, jnp.minimum(large, N_BUCKETS - 1))


def _bias_tables(rel_bias):
    per_dist = rel_bias[_t5_bucket(jnp.arange(2 * ROW_TILE, dtype=jnp.int32))]
    q_minus_s = np.arange(ROW_TILE)[None, :] - np.arange(ROW_TILE)[:, None]
    far = per_dist[2 * ROW_TILE - 1]
    tabs = [per_dist[np.clip(r * ROW_TILE + q_minus_s, 0, 2 * ROW_TILE - 1)] - far for r in (0, 1)]
    tabs.append(jnp.zeros_like(tabs[0]))
    return jnp.stack(tabs).transpose(0, 1, 3, 2).reshape(3, ROW_TILE, N_HEADS * ROW_TILE).astype(F32)


def _dsa(qt, qit, wit, k, ki, vt, bias_tab, B, TP, topk):
    nq = TP // ROW_TILE
    return pl.pallas_call(
        functools.partial(_dsa_kernel, topk=topk),
        grid=(B, nq),
        in_specs=[pl.BlockSpec((1, HEAD_DIM, N_HEADS * LANES), lambda b, i: (b * nq + i, 0, 0)),
                  pl.BlockSpec((1, IDX_DIM, IDX_HEADS * LANES), lambda b, i: (b * nq + i, 0, 0)),
                  pl.BlockSpec((1, IDX_HEADS, LANES), lambda b, i: (b * nq + i, 0, 0)),
                  pl.BlockSpec((TP, HEAD_DIM), lambda b, i: (b, 0)),
                  pl.BlockSpec((TP, IDX_DIM), lambda b, i: (b, 0)),
                  pl.BlockSpec((nq, V_ROWS, LANES), lambda b, i: (b, 0, 0)),
                  pl.BlockSpec((3, ROW_TILE, N_HEADS * ROW_TILE), lambda b, i: (0, 0, 0))],
        out_specs=pl.BlockSpec((ROW_TILE, MIX_W), lambda b, i: (b * nq + i, 0)),
        out_shape=jax.ShapeDtypeStruct((B * TP, MIX_W), BF16),
        scratch_shapes=[pltpu.VMEM((nq + 2, LANES, ROW_TILE), F32),
                        pltpu.VMEM((2, min(4, nq) * LANES, IDX_HEADS * ROW_TILE), F32),
                        pltpu.VMEM((1, N_HEADS * ROW_TILE), F32),
                        pltpu.VMEM((V_ROWS, N_HEADS * ROW_TILE), F32),
                        pltpu.VMEM((2, min(4, nq), LANES, N_HEADS * ROW_TILE), F32),
                        pltpu.VMEM((2, 1, N_HEADS * ROW_TILE), F32)],
        compiler_params=_cparams("parallel", "arbitrary"),
        name="dsa_attend",
    )(qt, qit, wit, k, ki, vt, bias_tab)


def _layer_norm_rows(z, g, b):
    mu = jnp.mean(z, -1, keepdims=True)
    zc = z - mu
    var = jnp.mean(zc * zc, -1, keepdims=True)
    return zc * lax.rsqrt(var + LN_EPS) * g + b


def _merge_kernel(h_ref, g_ref, ya_ref, yb_ref, yc_ref, yd_ref, wb_ref, wo_ref, lg_ref, lb_ref,
                  h1_ref, h1b_ref):
    merged = None
    for i, y_ref in enumerate((ya_ref, yb_ref, yc_ref, yd_ref)):
        t = g_ref[:, i * D_MODEL:(i + 1) * D_MODEL] * jnp.dot(y_ref[...], wb_ref[i], preferred_element_type=F32)
        merged = t if merged is None else merged + t
    z = DN_ALPHA * h_ref[...] + jnp.dot(merged.astype(BF16), wo_ref[...], preferred_element_type=F32)
    y = _layer_norm_rows(z, lg_ref[...], lb_ref[...])
    h1_ref[...] = y
    h1b_ref[...] = y.astype(BF16)


def _merge(h, gates, ys, w_branch, w_out, ln_g, ln_b):
    N, D = h.shape
    tm = _pick_tile(N, 640)
    full = lambda shape: pl.BlockSpec(shape, lambda i: (0,) * len(shape))
    tok = lambda w: pl.BlockSpec((tm, w), lambda i: (i, 0))
    return pl.pallas_call(
        _merge_kernel,
        grid=(N // tm,),
        in_specs=[tok(D), tok(4 * D), tok(MIX_W), tok(MIX_W), tok(MIX_W), tok(MIX_W),
                  full((4, MIX_W, D)), full((D, D)), full((1, D)), full((1, D))],
        out_specs=[tok(D), tok(D)],
        out_shape=[jax.ShapeDtypeStruct((N, D), F32), jax.ShapeDtypeStruct((N, D), BF16)],
        compiler_params=_cparams("arbitrary"),
        name="merge_out_ln",
    )(h, gates, *ys, w_branch.astype(BF16), w_out.astype(BF16),
      ln_g.reshape(1, D).astype(F32), ln_b.reshape(1, D).astype(F32))


def _moe_kernel(h_ref, hb_ref, wr_ref, br_ref, wg_ref, wu_ref, wd_ref, lg_ref, lb_ref, o_ref, ob_ref,
                gate_ref, acc_ref):
    e = pl.program_id(1)
    xb = hb_ref[...]
    tm = xb.shape[0]
    lane = _iota((tm, LANES), 1)

    @pl.when(e == 0)
    def _():
        logit = jnp.dot(xb, wr_ref[...], preferred_element_type=F32) + br_ref[...]
        big = jnp.int32(LANES)
        gl = jnp.where(lane < N_GROUPS, logit, -jnp.inf)
        gmax = jnp.max(gl, axis=1, keepdims=True)
        g_sel = jnp.min(jnp.where(gl == gmax, lane, big), axis=1, keepdims=True)
        p_grp = 1.0 / jnp.sum(jnp.exp(gl - gmax), axis=1, keepdims=True)
        lo = N_GROUPS + g_sel * EPG
        el = jnp.where((lane >= lo) & (lane < lo + EPG), logit, -jnp.inf)
        v1 = jnp.max(el, axis=1, keepdims=True)
        i1 = jnp.min(jnp.where(el == v1, lane, big), axis=1, keepdims=True)
        el2 = jnp.where(lane == i1, -jnp.inf, el)
        v2 = jnp.max(el2, axis=1, keepdims=True)
        i2 = jnp.min(jnp.where(el2 == v2, lane, big), axis=1, keepdims=True)
        e2 = jnp.exp(v2 - v1)
        w1 = p_grp / (1.0 + e2)
        w2 = p_grp * e2 / (1.0 + e2)
        gate_ref[...] = jnp.where(lane == i1, w1, 0.0) + jnp.where(lane == i2, w2, 0.0)
        acc_ref[...] = jnp.zeros_like(acc_ref)

    g_e = jnp.sum(jnp.where(lane == e + N_GROUPS, gate_ref[...], 0.0), axis=1, keepdims=True)
    hid = _silu(jnp.dot(xb, wg_ref[0], preferred_element_type=F32)) * jnp.dot(xb, wu_ref[0], preferred_element_type=F32)
    acc_ref[...] += g_e * jnp.dot(hid.astype(BF16), wd_ref[0], preferred_element_type=F32)

    @pl.when(e == N_EXPERTS - 1)
    def _():
        y = _layer_norm_rows(DN_ALPHA * h_ref[...] + acc_ref[...], lg_ref[...], lb_ref[...])
        o_ref[...] = y
        ob_ref[...] = y.astype(BF16)


def _moe(h1, h1b, w_grp, b_grp, w_rt, b_rt, w_gate, w_up, w_down, ln_g, ln_b):
    N, D = h1.shape
    tm = _pick_tile(N, 1280)
    w_r = jnp.zeros((D, LANES), F32).at[:, 0:N_GROUPS].set(w_grp).at[:, N_GROUPS:N_GROUPS + N_EXPERTS].set(w_rt)
    b_r = jnp.zeros((1, LANES), F32).at[0, 0:N_GROUPS].set(b_grp).at[0, N_GROUPS:N_GROUPS + N_EXPERTS].set(b_rt)
    full = lambda shape: pl.BlockSpec(shape, lambda i, e: (0,) * len(shape))
    tok = lambda w: pl.BlockSpec((tm, w), lambda i, e: (i, 0))
    return pl.pallas_call(
        _moe_kernel,
        grid=(N // tm, N_EXPERTS),
        in_specs=[tok(D), tok(D), full((D, LANES)), full((1, LANES)),
                  pl.BlockSpec((1, D, D_EXPERT), lambda i, e: (e, 0, 0)),
                  pl.BlockSpec((1, D, D_EXPERT), lambda i, e: (e, 0, 0)),
                  pl.BlockSpec((1, D_EXPERT, D), lambda i, e: (e, 0, 0)),
                  full((1, D)), full((1, D))],
        out_specs=[tok(D), tok(D)],
        out_shape=[jax.ShapeDtypeStruct((N, D), F32), jax.ShapeDtypeStruct((N, D), BF16)],
        scratch_shapes=[pltpu.VMEM((tm, LANES), F32), pltpu.VMEM((tm, D), F32)],
        compiler_params=_cparams("arbitrary", "arbitrary"),
        name="hier_moe_ln",
    )(h1, h1b, w_r.astype(BF16), b_r, w_gate.astype(BF16), w_up.astype(BF16), w_down.astype(BF16),
      ln_g.reshape(1, D).astype(F32), ln_b.reshape(1, D).astype(F32))


def _pad_cols(w, width):
    return jnp.pad(w, ((0, 0), (0, width - w.shape[1])))


def _split_w_in(w):
    o = 0
    w_a = w[:, o:o + 1024]; o += 1024
    gq, gk, gv, ga, gg = (w[:, o:o + 128], w[:, o + 128:o + 256], w[:, o + 256:o + 512],
                          w[:, o + 512:o + 528], w[:, o + 528:o + 784]); o += 784
    w_b = _pad_cols(jnp.concatenate([gq, gk, gv, gg, ga], axis=1), 896)
    cq, ckv, cqi, cki, cwi = (w[:, o:o + 256], w[:, o + 256:o + 384], w[:, o + 384:o + 640],
                              w[:, o + 640:o + 672], w[:, o + 672:o + 680]); o += 680
    w_t = jnp.pad(jnp.concatenate([cq.T, cqi.T, cwi.T], axis=0), ((0, WT_ROWS - 2 * MIX_W - IDX_HEADS), (0, 0)))
    w_n = _pad_cols(jnp.concatenate([ckv, cki], axis=1), 256)
    dq, dk, dv, di, df, do = (w[:, o:o + 256], w[:, o + 256:o + 512], w[:, o + 512:o + 768],
                              w[:, o + 768:o + 772], w[:, o + 772:o + 776], w[:, o + 776:o + 1032]); o += 1032
    w_d = _pad_cols(jnp.concatenate([dq, dk, dv, do, di, df], axis=1), 1152)
    w_g = w[:, o:o + 4096]
    bf = lambda a: a.astype(BF16)
    return bf(w_a), bf(w_b), bf(w_t), bf(w_n), bf(w_d), bf(w_g)


def kernel(x, meta, ln_in_g, ln_in_b, rel_bias, w_in, rwkv_mu, rwkv_w_up, rwkv_w0, rwkv_a_up, rwkv_a0, rwkv_g_up, rwkv_k_k, rwkv_k_a, rwkv_r_k, rwkv_gn_g, rwkv_gn_b, gla_a_up, gla_a_b, gla_norm_g, dsa_kv_norm_g, dsa_w_uk, dsa_w_uv, mlstm_conv_w, mlstm_conv_b, mlstm_i_b, mlstm_f_b, mlstm_norm_g, w_branch, w_out, ln1_g, ln1_b, moe_w_grp, moe_b_grp, moe_w_rt, moe_b_rt, moe_w_gate, moe_w_up, moe_w_down, ln2_g, ln2_b):
    B, S, D = x.shape
    assert D == D_MODEL and S % ROW_TILE == 0
    TP = S + FRONT
    N = B * TP
    topk = min(TOPK_MAX, S // 4)
    bias_tab = _bias_tables(rel_bias)

    h, hb = _embed(x, meta, ln_in_g, ln_in_b)
    h = h.reshape(N, D)
    hb = hb.reshape(N, D)
    for l in range(DEPTH):
        w_a, w_b, w_t, w_n, w_d, w_g = _split_w_in(w_in[l])
        p_a = _proj(hb, w_a)
        p_b = _proj(hb, w_b)
        p_d = _proj(hb, w_d)
        gates = _proj(hb, w_g, act="sigmoid")
        qt, qit, wit, k, ki, vt = _dsa_prep(hb, w_t, w_n, dsa_kv_norm_g[l], dsa_w_uk[l], dsa_w_uv[l])
        y_a = _rwkv(p_a, B, TP, rwkv_mu[l], rwkv_w_up[l], rwkv_w0[l], rwkv_a_up[l], rwkv_a0[l], rwkv_g_up[l],
                    rwkv_k_k[l], rwkv_k_a[l], rwkv_r_k[l], rwkv_gn_g[l], rwkv_gn_b[l])
        y_b = _gla(p_b, B, TP, gla_a_up[l], gla_a_b[l], gla_norm_g[l])
        y_c = _dsa(qt, qit, wit, k, ki, vt, bias_tab, B, TP, topk)
        y_d = _mlstm(p_d, B, TP, mlstm_conv_w[l], mlstm_conv_b[l], mlstm_i_b[l], mlstm_f_b[l], mlstm_norm_g[l])
        ys = (y_a.reshape(N, MIX_W), y_b.reshape(N, MIX_W), y_c, y_d.reshape(N, MIX_W))
        h1, h1b = _merge(h, gates, ys, w_branch[l], w_out[l], ln1_g[l], ln1_b[l])
        h, hb = _moe(h1, h1b, moe_w_grp[l], moe_b_grp[l], moe_w_rt[l], moe_b_rt[l],
                     moe_w_gate[l], moe_w_up[l], moe_w_down[l], ln2_g[l], ln2_b[l])
    return h.reshape(B, TP, D)[:, FRONT:]
```

```python
import functools
import math

import numpy as np
import jax
import jax.numpy as jnp
from jax import lax
from jax.experimental import pallas as pl
from jax.experimental.pallas import tpu as pltpu

F32 = jnp.float32
BF16 = jnp.bfloat16

D_MODEL = 1024
HEAD_DIM = 64
N_HEADS = 4
MIX_W = 256
N_META = 16
CHUNK = 64
LANES = 128
ROW_TILE = 128
FRONT = ROW_TILE
FP = FRONT - N_META
NEG = -1e30
LN_EPS = 1e-5
DEPTH = 2
DN_ALPHA = (2 * DEPTH) ** 0.25

RWKV_GN_EPS = HEAD_DIM * 1e-5
GLA_DK = 32
GLA_TAU = 16.0
DSA_KV_RANK = 128
IDX_HEADS = 8
IDX_DIM = 32
TOPK_MAX = 256
N_BUCKETS = 32
MAX_DISTANCE = 128
CONV_W = 4
N_GROUPS = 4
EPG = 4
N_EXPERTS = 16
D_EXPERT = 256

INT_MIN = -(2 ** 31)
FLT_LOWEST = float(np.finfo(np.float32).min)
KEY_LOWEST = -(2 ** 31) + 0x00800000
VMEM_LIMIT = 56 * 1024 * 1024


def _cparams(*sem):
    return pltpu.CompilerParams(dimension_semantics=tuple(sem), vmem_limit_bytes=VMEM_LIMIT)


def _pick_tile(n, target):
    best = LANES
    t = LANES
    while t <= min(n, target):
        if n % t == 0:
            best = t
        t += LANES
    return best


def _bdot(a, b):
    return jnp.dot(a.astype(BF16), b.astype(BF16), preferred_element_type=F32)


def _bdot_nt(a, b):
    return lax.dot_general(a.astype(BF16), b.astype(BF16), (((1,), (1,)), ((), ())),
                           preferred_element_type=F32)


def _bdot_tn(a, b):
    return lax.dot_general(a.astype(BF16), b.astype(BF16), (((0,), (0,)), ((), ())),
                           preferred_element_type=F32)


def _split(a):
    hi = a.astype(BF16)
    lo = (a - hi.astype(F32)).astype(BF16)
    return hi, lo


_NN = (((1,), (0,)), ((), ()))
_NT = (((1,), (1,)), ((), ()))
_TN = (((0,), (0,)), ((), ()))


def _dot3(a, b, dims=_NN):
    ah, al = _split(a)
    bh, bl = _split(b)
    dg = lambda x, y: lax.dot_general(x, y, dims, preferred_element_type=F32)
    return dg(ah, bh) + (dg(ah, bl) + dg(al, bh))


def _dot_exact_lhs(a_bf16, b):
    bh, bl = _split(b)
    return (jnp.dot(a_bf16, bh, preferred_element_type=F32)
            + jnp.dot(a_bf16, bl, preferred_element_type=F32))


def _dot_exact_rhs(a, b_bf16):
    ah, al = _split(a)
    return (jnp.dot(ah, b_bf16, preferred_element_type=F32)
            + jnp.dot(al, b_bf16, preferred_element_type=F32))


def _sigmoid(x):
    return 1.0 / (1.0 + jnp.exp(-x))


def _log_sigmoid(x):
    return jnp.minimum(x, 0.0) - jnp.log(1.0 + jnp.exp(-jnp.abs(x)))


def _silu(x):
    return x * _sigmoid(x)


def _iota(shape, dim):
    return lax.broadcasted_iota(jnp.int32, shape, dim)


def _tri_incl(n):
    return (_iota((n, n), 1) <= _iota((n, n), 0))


def _head_ones():
    return ((_iota((MIX_W, MIX_W), 0) // HEAD_DIM) == (_iota((MIX_W, MIX_W), 1) // HEAD_DIM)).astype(BF16)


def _row_ids(rows):
    return pl.program_id(1) * ROW_TILE + _iota((rows, 1), 0)


def _embed_kernel(x_ref, meta_ref, g_ref, b_ref, h_ref, hb_ref):
    j = pl.program_id(1)
    src = jnp.where(j == 0, meta_ref[...], x_ref[0])
    mu = jnp.mean(src, -1, keepdims=True)
    xc = src - mu
    var = jnp.mean(xc * xc, -1, keepdims=True)
    y = xc * lax.rsqrt(var + LN_EPS) * g_ref[...] + b_ref[...]
    h_ref[0] = y
    hb_ref[0] = y.astype(BF16)


def _embed(x, meta, g, b):
    B, S, D = x.shape
    TP = S + FRONT
    meta_pad = jnp.concatenate([jnp.zeros((FP, D), F32), meta.astype(F32)], axis=0)
    return pl.pallas_call(
        _embed_kernel,
        grid=(B, TP // ROW_TILE),
        in_specs=[
            pl.BlockSpec((1, ROW_TILE, D), lambda b, j: (b, jnp.maximum(j - 1, 0), 0)),
            pl.BlockSpec((ROW_TILE, D), lambda b, j: (0, 0)),
            pl.BlockSpec((1, D), lambda b, j: (0, 0)),
            pl.BlockSpec((1, D), lambda b, j: (0, 0)),
        ],
        out_specs=[
            pl.BlockSpec((1, ROW_TILE, D), lambda b, j: (b, j, 0)),
            pl.BlockSpec((1, ROW_TILE, D), lambda b, j: (b, j, 0)),
        ],
        out_shape=[jax.ShapeDtypeStruct((B, TP, D), F32), jax.ShapeDtypeStruct((B, TP, D), BF16)],
        compiler_params=_cparams("parallel", "arbitrary"),
        name="embed_ln",
    )(x, meta_pad, g.reshape(1, D), b.reshape(1, D))


def _proj_kernel(h_ref, w_ref, o_ref):
    o_ref[...] = jnp.dot(h_ref[...], w_ref[...], preferred_element_type=F32)


def _proj(hb, w):
    N, D = hb.shape
    W = w.shape[1]
    tm = _pick_tile(N, 1280)
    return pl.pallas_call(
        _proj_kernel,
        grid=(N // tm,),
        in_specs=[pl.BlockSpec((tm, D), lambda i: (i, 0)),
                  pl.BlockSpec((D, W), lambda i: (0, 0))],
        out_specs=pl.BlockSpec((tm, W), lambda i: (i, 0)),
        out_shape=jax.ShapeDtypeStruct((N, W), F32),
        compiler_params=_cparams("arbitrary"),
        name="in_proj",
    )(hb, w)


def _rwkv_kernel(p_ref, mu_ref, wup_ref, w0_ref, aup_ref, a0_ref, gup_ref, kk_ref, ka_ref, rk_ref,
                 gng_ref, gnb_ref, y_ref, carry_ref, s_ref):
    j = pl.program_id(0)
    nb = p_ref.shape[0]
    n_chunks = ROW_TILE // CHUNK

    @pl.when(j == 0)
    def _():
        carry_ref[...] = jnp.zeros_like(carry_ref)
        s_ref[...] = jnp.zeros_like(s_ref)

    valid = (j * ROW_TILE + _iota((ROW_TILE, 1), 0)) >= FP
    first_row = _iota((ROW_TILE, 1), 0) == 0
    ones_h = _head_ones()
    tri = _tri_incl(CHUNK)
    tri_b = tri.astype(BF16)
    strict = _iota((CHUNK, CHUNK), 1) < _iota((CHUNK, CHUNK), 0)
    eye = (_iota((CHUNK, CHUNK), 1) == _iota((CHUNK, CHUNK), 0)).astype(F32)
    heads = [slice(h * HEAD_DIM, (h + 1) * HEAD_DIM) for h in range(N_HEADS)]

    pro = []
    unit = {}
    for b in range(nb):
        p = jnp.where(valid, p_ref[b], 0.0)
        prev = jnp.where(first_row, carry_ref[b], pltpu.roll(p, 1, 0))
        carry_ref[b] = p[ROW_TILE - 1:ROW_TILE, :]
        ps = p + (prev - p) * mu_ref[...]
        r = ps[:, 0:256]
        k = ps[:, 256:512]
        v = ps[:, 512:768]
        lora_in = ps[:, 768:896]
        xg = ps[:, 896:1024]
        w_log = _log_sigmoid(w0_ref[...] + _bdot(jnp.tanh(lora_in), wup_ref[...])) - 0.5
        lw = jnp.where(valid, -jnp.exp(w_log), 0.0)
        alpha = _sigmoid(a0_ref[...] + _bdot(lora_in, aup_ref[...]))
        gate = _bdot(_sigmoid(xg), gup_ref[...])
        kk = k * kk_ref[...]
        kk = kk / jnp.maximum(jnp.sqrt(_dot_exact_rhs(kk * kk, ones_h)), 1e-12)
        k = k * (1.0 + (alpha - 1.0) * ka_ref[...])
        kka = kk * alpha
        pro.append((r, k, v, gate))
        for c in range(n_chunks):
            sl = slice(c * CHUNK, (c + 1) * CHUNK)
            lw_c = lw[sl]
            cum = _dot_exact_lhs(tri_b, lw_c)
            cum_last = cum[CHUNK - 1:CHUNK, :]
            p_inv = jnp.exp(-cum)
            p_tail = jnp.exp(cum_last - cum)
            unit[b, c] = dict(a=-kk[sl] * jnp.exp(cum - lw_c), b=kka[sl] * p_inv, k=k[sl] * p_inv,
                              r=r[sl] * jnp.exp(cum), kb=k[sl] * p_tail, bb=kka[sl] * p_tail,
                              pl=jnp.exp(cum_last), v=v[sl])

    keys = [(b, c, h) for b in range(nb) for c in range(n_chunks) for h in range(N_HEADS)]
    part = lambda name, key: unit[key[0], key[1]][name][:, heads[key[2]]]
    a_ab = {q: jnp.where(strict, _dot3(part("a", q), part("b", q), _NT), 0.0) for q in keys}
    a_ak = {q: jnp.where(strict, _bdot_nt(part("a", q), part("k", q)), 0.0) for q in keys}
    a_rb = {q: jnp.where(tri, _bdot_nt(part("r", q), part("b", q)), 0.0) for q in keys}
    a_rk = {q: jnp.where(tri, _bdot_nt(part("r", q), part("k", q)), 0.0) for q in keys}
    inv = {q: eye + a_ab[q] for q in keys}
    pw = a_ab
    for _ in range(5):
        pw = {q: _bdot(pw[q], pw[q]) for q in keys}
        inv = {q: inv[q] + _bdot(inv[q], pw[q]) for q in keys}
    ak_v = {q: _bdot(a_ak[q], part("v", q)) for q in keys}
    rk_v = {q: _bdot(a_rk[q], part("v", q)) for q in keys}
    kb_v = {q: _bdot_tn(part("v", q), part("kb", q)) for q in keys}

    bh = [(b, h) for b in range(nb) for h in range(N_HEADS)]
    state = {q: s_ref[q[0], q[1]] for q in bh}
    y_parts = {}
    for c in range(n_chunks):
        full = lambda q: (q[0], c, q[1])
        a_s = {q: _bdot_nt(part("a", full(q)), state[q]) for q in bh}
        r_s = {q: _bdot_nt(part("r", full(q)), state[q]) for q in bh}
        u = {q: _bdot(inv[full(q)], a_s[q] + ak_v[full(q)]) for q in bh}
        for q in bh:
            y_parts[full(q)] = r_s[q] + rk_v[full(q)] + _bdot(a_rb[full(q)], u[q])
        state = {q: (state[q] * part("pl", full(q)) + kb_v[full(q)] + _bdot_tn(u[q], part("bb", full(q))))
                 for q in bh}
    for q in bh:
        s_ref[q[0], q[1]] = state[q]

    for b in range(nb):
        r, k, v, gate = pro[b]
        y = jnp.concatenate([jnp.concatenate([y_parts[b, c, h] for h in range(N_HEADS)], axis=1)
                             for c in range(n_chunks)], axis=0)
        mean = _dot_exact_rhs(y, ones_h) * (1.0 / HEAD_DIM)
        yc = y - mean
        var = _dot_exact_rhs(yc * yc, ones_h) * (1.0 / HEAD_DIM)
        yn = yc * lax.rsqrt(var + RWKV_GN_EPS) * gng_ref[...] + gnb_ref[...]
        bonus = _dot_exact_rhs(r * k * rk_ref[...], ones_h) * v
        y_ref[b] = ((yn + bonus) * gate).astype(y_ref.dtype)


def _rwkv(p_a, B, TP, mu, w_up, w0, a_up, a0, g_up, k_k, k_a, r_k, gn_g, gn_b):
    W = MIX_W
    z64 = jnp.zeros((64, W), F32)
    wup_pad = jnp.concatenate([w_up, z64], axis=0).astype(BF16)
    aup_pad = jnp.concatenate([z64, a_up], axis=0).astype(BF16)
    row = lambda a: a.reshape(1, -1).astype(F32)
    full = lambda shape: pl.BlockSpec(shape, lambda j: (0,) * len(shape))
    return pl.pallas_call(
        _rwkv_kernel,
        grid=(TP // ROW_TILE,),
        in_specs=[pl.BlockSpec((B, ROW_TILE, 1024), lambda j: (0, j, 0)),
                  full((1, 1024)), full((128, W)), full((1, W)), full((128, W)), full((1, W)),
                  full((128, W)), full((1, W)), full((1, W)), full((1, W)), full((1, W)), full((1, W))],
        out_specs=pl.BlockSpec((B, ROW_TILE, W), lambda j: (0, j, 0)),
        out_shape=jax.ShapeDtypeStruct((B, TP, W), BF16),
        scratch_shapes=[pltpu.VMEM((B, 1, 1024), F32), pltpu.VMEM((B, N_HEADS, HEAD_DIM, HEAD_DIM), F32)],
        compiler_params=_cparams("arbitrary"),
        name="rwkv7",
    )(p_a.reshape(B, TP, 1024), row(mu), wup_pad, row(w0), aup_pad, row(a0), g_up.astype(BF16),
      row(k_k), row(k_a), row(r_k), row(gn_g), row(gn_b))


def _gla_kernel(p_ref, aup_ref, ab_ref, ng_ref, y_ref, s_ref):
    j = pl.program_id(0)
    nb = p_ref.shape[0]
    n_chunks = ROW_TILE // CHUNK

    @pl.when(j == 0)
    def _():
        s_ref[...] = jnp.zeros_like(s_ref)

    valid = (j * ROW_TILE + _iota((ROW_TILE, 1), 0)) >= FP
    tri = _tri_incl(CHUNK)
    tri_b = tri.astype(BF16)

    og_all = []
    pre = {}
    for b in range(nb):
        p = jnp.where(valid, p_ref[b], 0.0)
        la = _log_sigmoid(_bdot(p[:, 768:896], aup_ref[...]) + ab_ref[...]) * (1.0 / GLA_TAU)
        la = jnp.where(valid, la, 0.0)
        og_all.append(p[:, 512:768])
        for c in range(n_chunks):
            sl = slice(c * CHUNK, (c + 1) * CHUNK)
            pre[b, c] = dict(q=p[sl, 0:128] * (GLA_DK ** -0.5), k=p[sl, 128:256], v=p[sl, 256:512], la=la[sl])
    bc = [(b, c) for b in range(nb) for c in range(n_chunks)]
    keys = [(b, c, h) for (b, c) in bc for h in range(N_HEADS)]
    ks = [slice(h * GLA_DK, (h + 1) * GLA_DK) for h in range(N_HEADS)]
    vs = [slice(h * HEAD_DIM, (h + 1) * HEAD_DIM) for h in range(N_HEADS)]
    b_cum = {u: _dot_exact_lhs(tri_b, pre[u]["la"]) for u in bc}
    b_last = {u: b_cum[u][CHUNK - 1:CHUNK, :] for u in bc}
    q_g = {u: pre[u]["q"] * jnp.exp(b_cum[u]) for u in bc}
    k_g = {u: pre[u]["k"] * jnp.exp(-b_cum[u]) for u in bc}
    k_l = {u: pre[u]["k"] * jnp.exp(b_last[u] - b_cum[u]) for u in bc}
    dec = {u: jnp.exp(b_last[u]) for u in bc}
    att = {u: jnp.where(tri, _bdot_nt(q_g[u[0], u[1]][:, ks[u[2]]], k_g[u[0], u[1]][:, ks[u[2]]]), 0.0)
           for u in keys}
    att_v = {u: _bdot(att[u], pre[u[0], u[1]]["v"][:, vs[u[2]]]) for u in keys}
    kl_v = {u: _bdot_tn(pre[u[0], u[1]]["v"][:, vs[u[2]]], k_l[u[0], u[1]][:, ks[u[2]]]) for u in keys}

    bh = [(b, h) for b in range(nb) for h in range(N_HEADS)]
    state = {q: s_ref[q[0], q[1]] for q in bh}
    o_parts = {}
    for c in range(n_chunks):
        for q in bh:
            o_parts[q[0], c, q[1]] = att_v[q[0], c, q[1]] + _bdot_nt(q_g[q[0], c][:, ks[q[1]]], state[q])
        state = {q: state[q] * dec[q[0], c][:, ks[q[1]]] + kl_v[q[0], c, q[1]] for q in bh}
    for q in bh:
        s_ref[q[0], q[1]] = state[q]

    ones_h = _head_ones()
    for b in range(nb):
        o = jnp.concatenate([jnp.concatenate([o_parts[b, c, h] for h in range(N_HEADS)], axis=1)
                             for c in range(n_chunks)], axis=0)
        ms = _dot_exact_rhs(o * o, ones_h) * (1.0 / HEAD_DIM)
        y = o * lax.rsqrt(ms + 1e-6) * ng_ref[...] * _silu(og_all[b])
        y_ref[b] = y.astype(y_ref.dtype)


def _gla(p_b, B, TP, a_up, a_b, norm_g):
    aup_pad = jnp.zeros((128, 128), F32).at[:a_up.shape[0]].set(a_up).astype(BF16)
    full = lambda shape: pl.BlockSpec(shape, lambda j: (0,) * len(shape))
    return pl.pallas_call(
        _gla_kernel,
        grid=(TP // ROW_TILE,),
        in_specs=[pl.BlockSpec((B, ROW_TILE, 896), lambda j: (0, j, 0)),
                  full((128, 128)), full((1, 128)), full((1, MIX_W))],
        out_specs=pl.BlockSpec((B, ROW_TILE, MIX_W), lambda j: (0, j, 0)),
        out_shape=jax.ShapeDtypeStruct((B, TP, MIX_W), BF16),
        scratch_shapes=[pltpu.VMEM((B, N_HEADS, HEAD_DIM, GLA_DK), F32)],
        compiler_params=_cparams("arbitrary"),
        name="gla",
    )(p_b.reshape(B, TP, 896), aup_pad, a_b.reshape(1, 128).astype(F32),
      jnp.tile(norm_g.astype(F32), N_HEADS).reshape(1, MIX_W))


def _mlstm_kernel(p_ref, cw_ref, cb_ref, ib_ref, fb_ref, ng_ref, y_ref, carry_ref, c_ref, n_ref, m_ref):
    j = pl.program_id(0)
    nb = p_ref.shape[0]
    n_chunks = ROW_TILE // CHUNK

    @pl.when(j == 0)
    def _():
        carry_ref[...] = jnp.zeros_like(carry_ref)
        c_ref[...] = jnp.zeros_like(c_ref)
        n_ref[...] = jnp.zeros_like(n_ref)
        m_ref[...] = jnp.zeros_like(m_ref)

    valid = (j * ROW_TILE + _iota((ROW_TILE, 1), 0)) >= FP
    tri = _tri_incl(CHUNK)
    tri_b = tri.astype(BF16)
    ones_h = _head_ones()

    og_all = []
    pre = {}
    for b in range(nb):
        p = jnp.where(valid, p_ref[b], 0.0)
        a = p[:, 0:512]
        ext = jnp.concatenate([carry_ref[b], a], axis=0)
        carry_ref[b] = a[ROW_TILE - 8:ROW_TILE, :]
        conv = cb_ref[...] + a * cw_ref[CONV_W - 1:CONV_W, :]
        for s in range(1, CONV_W):
            conv = conv + pltpu.roll(ext, s, 0)[8:8 + ROW_TILE, :] * cw_ref[CONV_W - 1 - s:CONV_W - s, :]
        qk = _silu(conv)
        q = jnp.where(valid, qk[:, 0:MIX_W], 0.0)
        k = jnp.where(valid, qk[:, MIX_W:2 * MIX_W], 0.0) * (HEAD_DIM ** -0.5)
        v = p[:, 512:768]
        og_all.append(p[:, 768:1024])
        gates = p[:, 1024:1152]
        li_all = jnp.where(valid, gates + ib_ref[...], NEG)
        lf_all = jnp.where(valid, _log_sigmoid(gates + fb_ref[...]), 0.0)
        for c in range(n_chunks):
            sl = slice(c * CHUNK, (c + 1) * CHUNK)
            pre[b, c] = dict(q=q[sl], k=k[sl], v=v[sl], li=li_all[sl], lf=lf_all[sl])

    bc = [(b, c) for b in range(nb) for c in range(n_chunks)]
    keys = [(b, c, h) for (b, c) in bc for h in range(N_HEADS)]
    heads = [slice(h * HEAD_DIM, (h + 1) * HEAD_DIM) for h in range(N_HEADS)]
    part = lambda name, u: pre[u[0], u[1]][name][:, heads[u[2]]]
    b_cum = {u: _dot_exact_lhs(tri_b, pre[u]["lf"]) for u in bc}
    b_t = {u: b_cum[u].T for u in bc}
    li_t = {u: pre[u]["li"].T for u in bc}
    b_col = {u: b_cum[u[0], u[1]][:, N_HEADS + u[2]:N_HEADS + u[2] + 1] for u in keys}
    b_last = {u: b_col[u][CHUNK - 1:CHUNK, :] for u in keys}
    d_log = {u: jnp.where(tri, b_col[u] - b_t[u[0], u[1]][N_HEADS + u[2]:N_HEADS + u[2] + 1, :]
                          + li_t[u[0], u[1]][u[2]:u[2] + 1, :], -jnp.inf) for u in keys}
    dmax = {u: jnp.max(d_log[u], axis=1, keepdims=True) for u in keys}
    qk = {u: _bdot_nt(part("q", u), part("k", u)) for u in keys}
    s0 = {u: jnp.exp(d_log[u] - dmax[u]) * qk[u] for u in keys}
    sv = {u: _bdot(s0[u], part("v", u)) for u in keys}
    ssum = {u: jnp.sum(s0[u], axis=1, keepdims=True) for u in keys}
    g_loc = {u: b_last[u] - b_col[u] + pre[u[0], u[1]]["li"][:, u[2]:u[2] + 1] for u in keys}
    m_loc = {u: jnp.max(g_loc[u], axis=0, keepdims=True) for u in keys}
    kw = {u: part("k", u) * jnp.exp(g_loc[u] - m_loc[u]) for u in keys}
    kwv = {u: _bdot_tn(kw[u], part("v", u)) for u in keys}
    kwsum = {u: jnp.sum(kw[u], axis=0, keepdims=True) for u in keys}

    bh = [(b, h) for b in range(nb) for h in range(N_HEADS)]
    c_st = {q: c_ref[q[0], q[1]] for q in bh}
    n_st = {q: n_ref[q[0], q[1]] for q in bh}
    m_st = {q: m_ref[q[0], q[1]] for q in bh}
    h_parts = {}
    for c in range(n_chunks):
        full = lambda q: (q[0], c, q[1])
        qc = {q: _bdot(part("q", full(q)), c_st[q]) for q in bh}
        qn = {q: jnp.sum(part("q", full(q)) * n_st[q], axis=1, keepdims=True) for q in bh}
        inter = {q: b_col[full(q)] + m_st[q] for q in bh}
        m_t = {q: jnp.maximum(inter[q], dmax[full(q)]) for q in bh}
        e_loc = {q: jnp.exp(dmax[full(q)] - m_t[q]) for q in bh}
        w_int = {q: jnp.exp(inter[q] - m_t[q]) for q in bh}
        for q in bh:
            num = e_loc[q] * sv[full(q)] + w_int[q] * qc[q]
            den = e_loc[q] * ssum[full(q)] + w_int[q] * qn[q]
            h_parts[full(q)] = num / jnp.maximum(jnp.abs(den), jnp.exp(-m_t[q]))
        m_new = {q: jnp.maximum(b_last[full(q)] + m_st[q], m_loc[full(q)]) for q in bh}
        s_old = {q: jnp.exp(b_last[full(q)] + m_st[q] - m_new[q]) for q in bh}
        s_new = {q: jnp.exp(m_loc[full(q)] - m_new[q]) for q in bh}
        c_st = {q: s_old[q] * c_st[q] + s_new[q] * kwv[full(q)] for q in bh}
        n_st = {q: s_old[q] * n_st[q] + s_new[q] * kwsum[full(q)] for q in bh}
        m_st = m_new
    for q in bh:
        c_ref[q[0], q[1]], n_ref[q[0], q[1]], m_ref[q[0], q[1]] = c_st[q], n_st[q], m_st[q]

    for b in range(nb):
        hh = jnp.concatenate([jnp.concatenate([h_parts[b, c, h] for h in range(N_HEADS)], axis=1)
                              for c in range(n_chunks)], axis=0) * _sigmoid(og_all[b])
        mean = _dot_exact_rhs(hh, ones_h) * (1.0 / HEAD_DIM)
        hc = hh - mean
        var = _dot_exact_rhs(hc * hc, ones_h) * (1.0 / HEAD_DIM)
        y_ref[b] = (hc * lax.rsqrt(var + 1e-5) * ng_ref[...]).astype(y_ref.dtype)


def _mlstm(p_d, B, TP, conv_w, conv_b, i_b, f_b, norm_g):
    ib = jnp.zeros((1, LANES), F32).at[0, 0:N_HEADS].set(i_b)
    fb = jnp.zeros((1, LANES), F32).at[0, N_HEADS:2 * N_HEADS].set(f_b)
    full = lambda shape: pl.BlockSpec(shape, lambda j: (0,) * len(shape))
    return pl.pallas_call(
        _mlstm_kernel,
        grid=(TP // ROW_TILE,),
        in_specs=[pl.BlockSpec((B, ROW_TILE, 1152), lambda j: (0, j, 0)),
                  full((CONV_W, 512)), full((1, 512)), full((1, LANES)), full((1, LANES)), full((1, MIX_W))],
        out_specs=pl.BlockSpec((B, ROW_TILE, MIX_W), lambda j: (0, j, 0)),
        out_shape=jax.ShapeDtypeStruct((B, TP, MIX_W), BF16),
        scratch_shapes=[pltpu.VMEM((B, 8, 512), F32),
                        pltpu.VMEM((B, N_HEADS, HEAD_DIM, HEAD_DIM), F32),
                        pltpu.VMEM((B, N_HEADS, 1, HEAD_DIM), F32),
                        pltpu.VMEM((B, N_HEADS, 1, 1), F32)],
        compiler_params=_cparams("arbitrary"),
        name="mlstm",
    )(p_d.reshape(B, TP, 1152), conv_w.astype(F32), conv_b.reshape(1, 512).astype(F32), ib, fb,
      norm_g.reshape(1, MIX_W).astype(F32))


V_ROWS = 80
WT_ROWS = 528


def _dsa_prep_kernel(h_ref, wt_ref, wn_ref, kvg_ref, wuk_ref, wuvt_ref,
                     qt_ref, qit_ref, wit_ref, k_ref, ki_ref, vt_ref):
    hb = h_ref[...]
    tm = hb.shape[0]
    pt = lax.dot_general(wt_ref[...], hb, _NT, preferred_element_type=F32)
    pn = jnp.dot(hb, wn_ref[...], preferred_element_type=F32)
    ckv = pn[:, 0:DSA_KV_RANK]
    c = ckv * lax.rsqrt(jnp.mean(ckv * ckv, -1, keepdims=True) + 1e-6) * kvg_ref[...]
    cb = c.astype(BF16)
    k_ref[...] = jnp.dot(cb, wuk_ref[...], preferred_element_type=F32).astype(BF16)
    ki_ref[...] = pn[:, DSA_KV_RANK:DSA_KV_RANK + IDX_DIM].astype(BF16)
    vt = lax.dot_general(wuvt_ref[...], cb, _NT, preferred_element_type=F32)
    vt = jnp.where(_iota((V_ROWS, tm), 0) == HEAD_DIM, 1.0, vt)
    for t in range(tm // LANES):
        cs = slice(t * LANES, (t + 1) * LANES)
        for h in range(N_HEADS):
            qt_ref[t, :, h * LANES:(h + 1) * LANES] = (
                pt[h * HEAD_DIM:(h + 1) * HEAD_DIM, cs] * (HEAD_DIM ** -0.5)).astype(BF16)
        for h in range(IDX_HEADS):
            qit_ref[t, :, h * LANES:(h + 1) * LANES] = pt[MIX_W + h * IDX_DIM:MIX_W + (h + 1) * IDX_DIM, cs].astype(BF16)
        wit_ref[t] = pt[2 * MIX_W:2 * MIX_W + IDX_HEADS, cs] * ((IDX_HEADS * IDX_DIM) ** -0.5)
        vt_ref[t] = vt[:, cs].astype(BF16)


def _dsa_prep(hb, w_t, w_n, kv_norm_g, w_uk, w_uv):
    N, D = hb.shape
    tm = _pick_tile(N, 640)
    nt = tm // LANES
    full = lambda shape: pl.BlockSpec(shape, lambda i: (0,) * len(shape))
    wuvt = jnp.pad(w_uv.T, ((0, V_ROWS - HEAD_DIM), (0, 0))).astype(BF16)
    return pl.pallas_call(
        _dsa_prep_kernel,
        grid=(N // tm,),
        in_specs=[pl.BlockSpec((tm, D), lambda i: (i, 0)),
                  full((WT_ROWS, D)), full((D, 256)), full((1, DSA_KV_RANK)),
                  full((DSA_KV_RANK, HEAD_DIM)), full((V_ROWS, DSA_KV_RANK))],
        out_specs=[pl.BlockSpec((nt, HEAD_DIM, N_HEADS * LANES), lambda i: (i, 0, 0)),
                   pl.BlockSpec((nt, IDX_DIM, IDX_HEADS * LANES), lambda i: (i, 0, 0)),
                   pl.BlockSpec((nt, IDX_HEADS, LANES), lambda i: (i, 0, 0)),
                   pl.BlockSpec((tm, HEAD_DIM), lambda i: (i, 0)),
                   pl.BlockSpec((tm, IDX_DIM), lambda i: (i, 0)),
                   pl.BlockSpec((nt, V_ROWS, LANES), lambda i: (i, 0, 0))],
        out_shape=[jax.ShapeDtypeStruct((N // LANES, HEAD_DIM, N_HEADS * LANES), BF16),
                   jax.ShapeDtypeStruct((N // LANES, IDX_DIM, IDX_HEADS * LANES), BF16),
                   jax.ShapeDtypeStruct((N // LANES, IDX_HEADS, LANES), F32),
                   jax.ShapeDtypeStruct((N, HEAD_DIM), BF16),
                   jax.ShapeDtypeStruct((N, IDX_DIM), BF16),
                   jax.ShapeDtypeStruct((N // LANES, V_ROWS, LANES), BF16)],
        compiler_params=_cparams("arbitrary"),
        name="dsa_prep",
    )(hb, w_t, w_n, kv_norm_g.reshape(1, DSA_KV_RANK).astype(F32), w_uk.astype(BF16), wuvt)


def _dsa_kernel(qt_ref, qit_ref, wit_ref, k_ref, ki_ref, vt_ref, bias_ref, y_ref,
                sk_ref, rel_ref, m_ref, acc_ref, lg_ref, mg_ref, *, topk):
    i = pl.program_id(1)
    nk = i + 1
    QT = ROW_TILE
    HQ = N_HEADS * QT
    t_lane = i * QT + _iota((LANES, QT), 1)
    key_pos = lambda kt: kt * LANES + _iota((LANES, QT), 0)
    per_head = lambda fn: jnp.concatenate([fn(slice(h * QT, (h + 1) * QT)) for h in range(N_HEADS)], axis=1)

    qit = qit_ref[0]
    wit = wit_ref[0]

    GW = rel_ref.shape[1] // LANES
    n_tiles = sk_ref.shape[0] - 2
    n_trips = (i + 2 * GW) // (2 * GW)

    def group_base(g):
        return jnp.clip(GW * g, 0, n_tiles - GW)

    def issue(g, slot):
        span = pl.ds(pl.multiple_of(group_base(g) * LANES, LANES), GW * LANES)
        rel_ref[slot] = jnp.dot(ki_ref[span, :], qit, preferred_element_type=F32)

    def reduce(g, slot):
        for u in range(GW):
            kt = group_base(g) + u
            rows_u = slice(u * LANES, (u + 1) * LANES)
            score = jnp.maximum(rel_ref[slot, rows_u, 0:QT], 0.0) * wit[0:1, :]
            for h in range(1, IDX_HEADS):
                score = score + jnp.maximum(rel_ref[slot, rows_u, h * QT:(h + 1) * QT], 0.0) * wit[h:h + 1, :]
            mine = (kt >= GW * g) & (kt <= i)
            sk_ref[jnp.where(mine, kt, n_tiles)] = score

    issue(0, 0)

    def score_body(jj, c):
        issue(2 * jj + 1, 1)
        reduce(2 * jj, 0)
        issue(2 * jj + 2, 0)
        reduce(2 * jj + 1, 1)
        return c

    lax.fori_loop(0, n_trips, score_body, 0)
    first = sk_ref[0]
    first = jnp.where(key_pos(0) < FP + N_META, jnp.inf, first)
    sk_ref[0] = jnp.where(key_pos(0) >= FP, first, -jnp.inf)
    sk_ref[i] = jnp.where(key_pos(i) <= t_lane, sk_ref[i], -jnp.inf)
    sk_ref[n_tiles + 1] = jnp.full((LANES, QT), -jnp.inf, F32)

    def key_to_float(key):
        return lax.bitcast_convert_type(jnp.where(key < 0, key ^ jnp.int32(0x7FFFFFFF), key), F32)
    def count(pred_fn):
        def body(kt, acc):
            return acc + jnp.where(pred_fn(sk_ref[kt], kt), 1, 0)

        def body4(j, acc):
            for u in range(4):
                acc = body(4 * j + u, acc)
            return acc

        n4 = lax.shift_right_logical(nk, 2)
        acc = lax.fori_loop(0, n4, body4, jnp.zeros((LANES, QT), jnp.int32))
        acc = lax.fori_loop(4 * n4, nk, body, acc)
        return jnp.sum(acc, axis=0, keepdims=True)

    def bit_body(it, carry):
        tau, n_ge = carry
        cand = tau + jnp.left_shift(jnp.int32(1), 31 - it)
        cand_f = key_to_float(cand)
        cnt = count(lambda sk, kt: sk >= cand_f)
        return jnp.where(cnt >= topk, cand, tau), jnp.where(cnt >= topk, cnt, n_ge)

    tau_key, n_ge = lax.fori_loop(0, 32, bit_body, (jnp.full((1, QT), INT_MIN, jnp.int32),
                                                    jnp.zeros((1, QT), jnp.int32)))
    tau = jnp.where(tau_key < KEY_LOWEST, jnp.float32(FLT_LOWEST), key_to_float(tau_key))

    @pl.when(jnp.max(n_ge - topk) > 0)
    def _():
        n_bits = max(1, int(math.ceil(math.log2(sk_ref.shape[0] * LANES + 1))))
        need = topk - count(lambda sk, kt: sk > tau)

        def pos_body(it, x):
            cand = x + jnp.left_shift(jnp.int32(1), n_bits - 1 - it)
            cnt = count(lambda sk, kt: (sk == tau) & (key_pos(kt) < cand))
            return jnp.where(cnt < need, cand, x)

        x = lax.fori_loop(0, n_bits, pos_body, jnp.zeros((1, QT), jnp.int32))
        jmax = jnp.where(n_ge > topk, x, jnp.int32(2 ** 30))

        def drop_body(kt, c):
            sk = sk_ref[kt]
            sk_ref[kt] = jnp.where((sk == tau) & (key_pos(kt) > jmax), -jnp.inf, sk)
            return c

        lax.fori_loop(0, nk, drop_body, 0)

    qt = qt_ref[0]
    m_ref[...] = jnp.full((1, HQ), NEG, F32)
    acc_ref[...] = jnp.zeros((V_ROWS, HQ), F32)

    def park(g, slot):
        base = group_base(g)
        span = pl.ds(pl.multiple_of(base * LANES, LANES), GW * LANES)
        lg_all = jnp.dot(k_ref[span, :], qt, preferred_element_type=F32)
        tmax = None
        for u in range(GW):
            t = base + u
            mine = (t >= GW * g) & (t <= i)
            lg = lg_all[u * LANES:(u + 1) * LANES, :] + bias_ref[jnp.clip(i - t, 0, 2)]
            sel = sk_ref[jnp.where(mine, t, n_tiles + 1)] >= tau
            lgm = per_head(lambda hs: jnp.where(sel, lg[:, hs], NEG))
            lg_ref[slot, u] = lgm
            tmax = lgm if tmax is None else jnp.maximum(tmax, lgm)
        mg_ref[slot] = jnp.max(tmax, axis=0, keepdims=True)

    def weights(slot):
        m_old = m_ref[...]
        m_new = jnp.maximum(m_old, mg_ref[slot])
        m_ref[...] = m_new
        return jnp.exp(m_old - m_new), [jnp.exp(lg_ref[slot, u] - m_new).astype(BF16) for u in range(GW)]

    def fold(g, corr, prs):
        vt_all = jnp.concatenate([vt_ref[group_base(g) + u] for u in range(GW)], axis=1)
        pv = jnp.dot(vt_all, jnp.concatenate(prs, axis=0), preferred_element_type=F32)
        acc_ref[...] = acc_ref[...] * corr + pv

    park(0, 0)

    def pipe_body(jj, c):
        corr, prs = weights(0)
        park(2 * jj + 1, 1)
        fold(2 * jj, corr, prs)
        corr, prs = weights(1)
        park(2 * jj + 2, 0)
        fold(2 * jj + 1, corr, prs)
        return c

    lax.fori_loop(0, n_trips, pipe_body, 0)
    acc = acc_ref[...]
    out = acc[0:HEAD_DIM, :] / jnp.maximum(acc[HEAD_DIM:HEAD_DIM + 1, :], 1e-30)
    y_ref[...] = per_head(lambda hs: out[:, hs].T).astype(y_ref.dtype)


def _t5_bucket(dist):
    max_exact = N_BUCKETS // 2
    n = jnp.maximum(dist, 0)
    large = max_exact + (jnp.log(jnp.maximum(n, 1).astype(F32) / max_exact)
                         / math.log(MAX_DISTANCE / max_exact) * (N_BUCKETS - max_exact)).astype(jnp.int32)
    return jnp.where(n < max_exact, n, jnp.minimum(large, N_BUCKETS - 1))


def _bias_tables(rel_bias):
    per_dist = rel_bias[_t5_bucket(jnp.arange(2 * ROW_TILE, dtype=jnp.int32))]
    q_minus_s = np.arange(ROW_TILE)[None, :] - np.arange(ROW_TILE)[:, None]
    far = per_dist[2 * ROW_TILE - 1]
    tabs = [per_dist[np.clip(r * ROW_TILE + q_minus_s, 0, 2 * ROW_TILE - 1)] - far for r in (0, 1)]
    tabs.append(jnp.zeros_like(tabs[0]))
    return jnp.stack(tabs).transpose(0, 1, 3, 2).reshape(3, ROW_TILE, N_HEADS * ROW_TILE).astype(F32)


def _dsa(qt, qit, wit, k, ki, vt, bias_tab, B, TP, topk):
    nq = TP // ROW_TILE
    return pl.pallas_call(
        functools.partial(_dsa_kernel, topk=topk),
        grid=(B, nq),
        in_specs=[pl.BlockSpec((1, HEAD_DIM, N_HEADS * LANES), lambda b, i: (b * nq + i, 0, 0)),
                  pl.BlockSpec((1, IDX_DIM, IDX_HEADS * LANES), lambda b, i: (b * nq + i, 0, 0)),
                  pl.BlockSpec((1, IDX_HEADS, LANES), lambda b, i: (b * nq + i, 0, 0)),
                  pl.BlockSpec((TP, HEAD_DIM), lambda b, i: (b, 0)),
                  pl.BlockSpec((TP, IDX_DIM), lambda b, i: (b, 0)),
                  pl.BlockSpec((nq, V_ROWS, LANES), lambda b, i: (b, 0, 0)),
                  pl.BlockSpec((3, ROW_TILE, N_HEADS * ROW_TILE), lambda b, i: (0, 0, 0))],
        out_specs=pl.BlockSpec((ROW_TILE, MIX_W), lambda b, i: (b * nq + i, 0)),
        out_shape=jax.ShapeDtypeStruct((B * TP, MIX_W), BF16),
        scratch_shapes=[pltpu.VMEM((nq + 2, LANES, ROW_TILE), F32),
                        pltpu.VMEM((2, min(4, nq) * LANES, IDX_HEADS * ROW_TILE), F32),
                        pltpu.VMEM((1, N_HEADS * ROW_TILE), F32),
                        pltpu.VMEM((V_ROWS, N_HEADS * ROW_TILE), F32),
                        pltpu.VMEM((2, min(4, nq), LANES, N_HEADS * ROW_TILE), F32),
                        pltpu.VMEM((2, 1, N_HEADS * ROW_TILE), F32)],
        compiler_params=_cparams("parallel", "arbitrary"),
        name="dsa_attend",
    )(qt, qit, wit, k, ki, vt, bias_tab)


def _layer_norm_rows(z, g, b):
    mu = jnp.mean(z, -1, keepdims=True)
    zc = z - mu
    var = jnp.mean(zc * zc, -1, keepdims=True)
    return zc * lax.rsqrt(var + LN_EPS) * g + b


def _merge_kernel(h_ref, hb_ref, wg_ref, ya_ref, yb_ref, yc_ref, yd_ref, wb_ref, wo_ref, lg_ref, lb_ref,
                  h1_ref, h1b_ref):
    hb = hb_ref[...]
    merged = None
    for i, y_ref in enumerate((ya_ref, yb_ref, yc_ref, yd_ref)):
        gate = _sigmoid(jnp.dot(hb, wg_ref[:, i * D_MODEL:(i + 1) * D_MODEL], preferred_element_type=F32))
        t = gate * jnp.dot(y_ref[...], wb_ref[i], preferred_element_type=F32)
        merged = t if merged is None else merged + t
    z = DN_ALPHA * h_ref[...] + jnp.dot(merged.astype(BF16), wo_ref[...], preferred_element_type=F32)
    y = _layer_norm_rows(z, lg_ref[...], lb_ref[...])
    h1_ref[...] = y
    h1b_ref[...] = y.astype(BF16)


def _merge(h, hb, w_g, ys, w_branch, w_out, ln_g, ln_b):
    N, D = h.shape
    tm = _pick_tile(N, 640)
    full = lambda shape: pl.BlockSpec(shape, lambda i: (0,) * len(shape))
    tok = lambda w: pl.BlockSpec((tm, w), lambda i: (i, 0))
    return pl.pallas_call(
        _merge_kernel,
        grid=(N // tm,),
        in_specs=[tok(D), tok(D), full((D, 4 * D)), tok(MIX_W), tok(MIX_W), tok(MIX_W), tok(MIX_W),
                  full((4, MIX_W, D)), full((D, D)), full((1, D)), full((1, D))],
        out_specs=[tok(D), tok(D)],
        out_shape=[jax.ShapeDtypeStruct((N, D), F32), jax.ShapeDtypeStruct((N, D), BF16)],
        compiler_params=_cparams("arbitrary"),
        name="merge_out_ln",
    )(h, hb, w_g, *ys, w_branch.astype(BF16), w_out.astype(BF16),
      ln_g.reshape(1, D).astype(F32), ln_b.reshape(1, D).astype(F32))


def _moe_kernel(h_ref, hb_ref, wr_ref, br_ref, wg_ref, wu_ref, wd_ref, lg_ref, lb_ref, o_ref, ob_ref,
                gate_ref, acc_ref):
    e = pl.program_id(1)
    xb = hb_ref[...]
    tm = xb.shape[0]
    lane = _iota((tm, LANES), 1)

    @pl.when(e == 0)
    def _():
        logit = jnp.dot(xb, wr_ref[...], preferred_element_type=F32) + br_ref[...]
        big = jnp.int32(LANES)
        gl = jnp.where(lane < N_GROUPS, logit, -jnp.inf)
        gmax = jnp.max(gl, axis=1, keepdims=True)
        g_sel = jnp.min(jnp.where(gl == gmax, lane, big), axis=1, keepdims=True)
        p_grp = 1.0 / jnp.sum(jnp.exp(gl - gmax), axis=1, keepdims=True)
        lo = N_GROUPS + g_sel * EPG
        el = jnp.where((lane >= lo) & (lane < lo + EPG), logit, -jnp.inf)
        v1 = jnp.max(el, axis=1, keepdims=True)
        i1 = jnp.min(jnp.where(el == v1, lane, big), axis=1, keepdims=True)
        el2 = jnp.where(lane == i1, -jnp.inf, el)
        v2 = jnp.max(el2, axis=1, keepdims=True)
        i2 = jnp.min(jnp.where(el2 == v2, lane, big), axis=1, keepdims=True)
        e2 = jnp.exp(v2 - v1)
        w1 = p_grp / (1.0 + e2)
        w2 = p_grp * e2 / (1.0 + e2)
        gate_ref[...] = jnp.where(lane == i1, w1, 0.0) + jnp.where(lane == i2, w2, 0.0)
        acc_ref[...] = jnp.zeros_like(acc_ref)

    g_e = jnp.sum(jnp.where(lane == e + N_GROUPS, gate_ref[...], 0.0), axis=1, keepdims=True)
    hid = _silu(jnp.dot(xb, wg_ref[0], preferred_element_type=F32)) * jnp.dot(xb, wu_ref[0], preferred_element_type=F32)
    acc_ref[...] += g_e * jnp.dot(hid.astype(BF16), wd_ref[0], preferred_element_type=F32)

    @pl.when(e == N_EXPERTS - 1)
    def _():
        y = _layer_norm_rows(DN_ALPHA * h_ref[...] + acc_ref[...], lg_ref[...], lb_ref[...])
        o_ref[...] = y
        ob_ref[...] = y.astype(BF16)


def _moe(h1, h1b, w_grp, b_grp, w_rt, b_rt, w_gate, w_up, w_down, ln_g, ln_b):
    N, D = h1.shape
    tm = _pick_tile(N, 1280)
    w_r = jnp.zeros((D, LANES), F32).at[:, 0:N_GROUPS].set(w_grp).at[:, N_GROUPS:N_GROUPS + N_EXPERTS].set(w_rt)
    b_r = jnp.zeros((1, LANES), F32).at[0, 0:N_GROUPS].set(b_grp).at[0, N_GROUPS:N_GROUPS + N_EXPERTS].set(b_rt)
    full = lambda shape: pl.BlockSpec(shape, lambda i, e: (0,) * len(shape))
    tok = lambda w: pl.BlockSpec((tm, w), lambda i, e: (i, 0))
    return pl.pallas_call(
        _moe_kernel,
        grid=(N // tm, N_EXPERTS),
        in_specs=[tok(D), tok(D), full((D, LANES)), full((1, LANES)),
                  pl.BlockSpec((1, D, D_EXPERT), lambda i, e: (e, 0, 0)),
                  pl.BlockSpec((1, D, D_EXPERT), lambda i, e: (e, 0, 0)),
                  pl.BlockSpec((1, D_EXPERT, D), lambda i, e: (e, 0, 0)),
                  full((1, D)), full((1, D))],
        out_specs=[tok(D), tok(D)],
        out_shape=[jax.ShapeDtypeStruct((N, D), F32), jax.ShapeDtypeStruct((N, D), BF16)],
        scratch_shapes=[pltpu.VMEM((tm, LANES), F32), pltpu.VMEM((tm, D), F32)],
        compiler_params=_cparams("arbitrary", "arbitrary"),
        name="hier_moe_ln",
    )(h1, h1b, w_r.astype(BF16), b_r, w_gate.astype(BF16), w_up.astype(BF16), w_down.astype(BF16),
      ln_g.reshape(1, D).astype(F32), ln_b.reshape(1, D).astype(F32))


def _pad_cols(w, width):
    return jnp.pad(w, ((0, 0), (0, width - w.shape[1])))


def _split_w_in(w):
    o = 0
    w_a = w[:, o:o + 1024]; o += 1024
    gq, gk, gv, ga, gg = (w[:, o:o + 128], w[:, o + 128:o + 256], w[:, o + 256:o + 512],
                          w[:, o + 512:o + 528], w[:, o + 528:o + 784]); o += 784
    w_b = _pad_cols(jnp.concatenate([gq, gk, gv, gg, ga], axis=1), 896)
    cq, ckv, cqi, cki, cwi = (w[:, o:o + 256], w[:, o + 256:o + 384], w[:, o + 384:o + 640],
                              w[:, o + 640:o + 672], w[:, o + 672:o + 680]); o += 680
    w_t = jnp.pad(jnp.concatenate([cq.T, cqi.T, cwi.T], axis=0), ((0, WT_ROWS - 2 * MIX_W - IDX_HEADS), (0, 0)))
    w_n = _pad_cols(jnp.concatenate([ckv, cki], axis=1), 256)
    dq, dk, dv, di, df, do = (w[:, o:o + 256], w[:, o + 256:o + 512], w[:, o + 512:o + 768],
                              w[:, o + 768:o + 772], w[:, o + 772:o + 776], w[:, o + 776:o + 1032]); o += 1032
    w_d = _pad_cols(jnp.concatenate([dq, dk, dv, do, di, df], axis=1), 1152)
    w_g = w[:, o:o + 4096]
    bf = lambda a: a.astype(BF16)
    return bf(w_a), bf(w_b), bf(w_t), bf(w_n), bf(w_d), bf(w_g)


def kernel(x, meta, ln_in_g, ln_in_b, rel_bias, w_in, rwkv_mu, rwkv_w_up, rwkv_w0, rwkv_a_up, rwkv_a0, rwkv_g_up, rwkv_k_k, rwkv_k_a, rwkv_r_k, rwkv_gn_g, rwkv_gn_b, gla_a_up, gla_a_b, gla_norm_g, dsa_kv_norm_g, dsa_w_uk, dsa_w_uv, mlstm_conv_w, mlstm_conv_b, mlstm_i_b, mlstm_f_b, mlstm_norm_g, w_branch, w_out, ln1_g, ln1_b, moe_w_grp, moe_b_grp, moe_w_rt, moe_b_rt, moe_w_gate, moe_w_up, moe_w_down, ln2_g, ln2_b):
    B, S, D = x.shape
    assert D == D_MODEL and S % ROW_TILE == 0
    TP = S + FRONT
    N = B * TP
    topk = min(TOPK_MAX, S // 4)
    bias_tab = _bias_tables(rel_bias)

    h, hb = _embed(x, meta, ln_in_g, ln_in_b)
    h = h.reshape(N, D)
    hb = hb.reshape(N, D)
    for l in range(DEPTH):
        w_a, w_b, w_t, w_n, w_d, w_g = _split_w_in(w_in[l])
        p_a = _proj(hb, w_a)
        p_b = _proj(hb, w_b)
        p_d = _proj(hb, w_d)
        qt, qit, wit, k, ki, vt = _dsa_prep(hb, w_t, w_n, dsa_kv_norm_g[l], dsa_w_uk[l], dsa_w_uv[l])
        y_a = _rwkv(p_a, B, TP, rwkv_mu[l], rwkv_w_up[l], rwkv_w0[l], rwkv_a_up[l], rwkv_a0[l], rwkv_g_up[l],
                    rwkv_k_k[l], rwkv_k_a[l], rwkv_r_k[l], rwkv_gn_g[l], rwkv_gn_b[l])
        y_b = _gla(p_b, B, TP, gla_a_up[l], gla_a_b[l], gla_norm_g[l])
        y_c = _dsa(qt, qit, wit, k, ki, vt, bias_tab, B, TP, topk)
        y_d = _mlstm(p_d, B, TP, mlstm_conv_w[l], mlstm_conv_b[l], mlstm_i_b[l], mlstm_f_b[l], mlstm_norm_g[l])
        ys = (y_a.reshape(N, MIX_W), y_b.reshape(N, MIX_W), y_c, y_d.reshape(N, MIX_W))
        h1, h1b = _merge(h, hb, w_g, ys, w_branch[l], w_out[l], ln1_g[l], ln1_b[l])
        h, hb = _moe(h1, h1b, moe_w_grp[l], moe_b_grp[l], moe_w_rt[l], moe_b_rt[l],
                     moe_w_gate[l], moe_w_up[l], moe_w_down[l], ln2_g[l], ln2_b[l])
    return h.reshape(B, TP, D)[:, FRONT:]
```

```python
import functools
import math

import numpy as np
import jax
import jax.numpy as jnp
from jax import lax
from jax.experimental import pallas as pl
from jax.experimental.pallas import tpu as pltpu

F32 = jnp.float32
BF16 = jnp.bfloat16

D_MODEL = 1024
HEAD_DIM = 64
N_HEADS = 4
MIX_W = 256
N_META = 16
CHUNK = 64
LANES = 128
ROW_TILE = 128
FRONT = ROW_TILE
FP = FRONT - N_META
NEG = -1e30
LN_EPS = 1e-5
DEPTH = 2
DN_ALPHA = (2 * DEPTH) ** 0.25

RWKV_GN_EPS = HEAD_DIM * 1e-5
GLA_DK = 32
GLA_TAU = 16.0
DSA_KV_RANK = 128
IDX_HEADS = 8
IDX_DIM = 32
TOPK_MAX = 256
N_BUCKETS = 32
MAX_DISTANCE = 128
CONV_W = 4
N_GROUPS = 4
EPG = 4
N_EXPERTS = 16
D_EXPERT = 256

INT_MIN = -(2 ** 31)
FLT_LOWEST = float(np.finfo(np.float32).min)
KEY_LOWEST = -(2 ** 31) + 0x00800000
VMEM_LIMIT = 56 * 1024 * 1024


def _cparams(*sem):
    return pltpu.CompilerParams(dimension_semantics=tuple(sem), vmem_limit_bytes=VMEM_LIMIT)


def _pick_tile(n, target):
    best = LANES
    t = LANES
    while t <= min(n, target):
        if n % t == 0:
            best = t
        t += LANES
    return best


def _bdot(a, b):
    return jnp.dot(a.astype(BF16), b.astype(BF16), preferred_element_type=F32)


def _bdot_nt(a, b):
    return lax.dot_general(a.astype(BF16), b.astype(BF16), (((1,), (1,)), ((), ())),
                           preferred_element_type=F32)


def _bdot_tn(a, b):
    return lax.dot_general(a.astype(BF16), b.astype(BF16), (((0,), (0,)), ((), ())),
                           preferred_element_type=F32)


def _split(a):
    hi = a.astype(BF16)
    lo = (a - hi.astype(F32)).astype(BF16)
    return hi, lo


_NN = (((1,), (0,)), ((), ()))
_NT = (((1,), (1,)), ((), ()))
_TN = (((0,), (0,)), ((), ()))


def _dot3(a, b, dims=_NN):
    ah, al = _split(a)
    bh, bl = _split(b)
    dg = lambda x, y: lax.dot_general(x, y, dims, preferred_element_type=F32)
    return dg(ah, bh) + (dg(ah, bl) + dg(al, bh))


def _dot_exact_lhs(a_bf16, b):
    bh, bl = _split(b)
    return (jnp.dot(a_bf16, bh, preferred_element_type=F32)
            + jnp.dot(a_bf16, bl, preferred_element_type=F32))


def _dot_exact_rhs(a, b_bf16):
    ah, al = _split(a)
    return (jnp.dot(ah, b_bf16, preferred_element_type=F32)
            + jnp.dot(al, b_bf16, preferred_element_type=F32))


def _sigmoid(x):
    return 1.0 / (1.0 + jnp.exp(-x))


def _log_sigmoid(x):
    return jnp.minimum(x, 0.0) - jnp.log(1.0 + jnp.exp(-jnp.abs(x)))


def _silu(x):
    return x * _sigmoid(x)


def _iota(shape, dim):
    return lax.broadcasted_iota(jnp.int32, shape, dim)


def _tri_incl(n):
    return (_iota((n, n), 1) <= _iota((n, n), 0))


def _head_ones():
    return ((_iota((MIX_W, MIX_W), 0) // HEAD_DIM) == (_iota((MIX_W, MIX_W), 1) // HEAD_DIM)).astype(BF16)


def _row_ids(rows):
    return pl.program_id(1) * ROW_TILE + _iota((rows, 1), 0)


def _embed_kernel(x_ref, meta_ref, g_ref, b_ref, h_ref, hb_ref):
    j = pl.program_id(1)
    src = jnp.where(j == 0, meta_ref[...], x_ref[0])
    mu = jnp.mean(src, -1, keepdims=True)
    xc = src - mu
    var = jnp.mean(xc * xc, -1, keepdims=True)
    y = xc * lax.rsqrt(var + LN_EPS) * g_ref[...] + b_ref[...]
    h_ref[0] = y
    hb_ref[0] = y.astype(BF16)


def _embed(x, meta, g, b):
    B, S, D = x.shape
    TP = S + FRONT
    meta_pad = jnp.concatenate([jnp.zeros((FP, D), F32), meta.astype(F32)], axis=0)
    return pl.pallas_call(
        _embed_kernel,
        grid=(B, TP // ROW_TILE),
        in_specs=[
            pl.BlockSpec((1, ROW_TILE, D), lambda b, j: (b, jnp.maximum(j - 1, 0), 0)),
            pl.BlockSpec((ROW_TILE, D), lambda b, j: (0, 0)),
            pl.BlockSpec((1, D), lambda b, j: (0, 0)),
            pl.BlockSpec((1, D), lambda b, j: (0, 0)),
        ],
        out_specs=[
            pl.BlockSpec((1, ROW_TILE, D), lambda b, j: (b, j, 0)),
            pl.BlockSpec((1, ROW_TILE, D), lambda b, j: (b, j, 0)),
        ],
        out_shape=[jax.ShapeDtypeStruct((B, TP, D), F32), jax.ShapeDtypeStruct((B, TP, D), BF16)],
        compiler_params=_cparams("parallel", "arbitrary"),
        name="embed_ln",
    )(x, meta_pad, g.reshape(1, D), b.reshape(1, D))


def _proj_kernel(h_ref, w_ref, o_ref):
    o_ref[...] = jnp.dot(h_ref[...], w_ref[...], preferred_element_type=F32)


def _proj(hb, w):
    N, D = hb.shape
    W = w.shape[1]
    tm = _pick_tile(N, 1280)
    return pl.pallas_call(
        _proj_kernel,
        grid=(N // tm,),
        in_specs=[pl.BlockSpec((tm, D), lambda i: (i, 0)),
                  pl.BlockSpec((D, W), lambda i: (0, 0))],
        out_specs=pl.BlockSpec((tm, W), lambda i: (i, 0)),
        out_shape=jax.ShapeDtypeStruct((N, W), F32),
        compiler_params=_cparams("arbitrary"),
        name="in_proj",
    )(hb, w)


def _rwkv_kernel(p_ref, mu_ref, wup_ref, w0_ref, aup_ref, a0_ref, gup_ref, kk_ref, ka_ref, rk_ref,
                 gng_ref, gnb_ref, y_ref, carry_ref, s_ref):
    j = pl.program_id(0)
    nb = p_ref.shape[0]
    n_chunks = ROW_TILE // CHUNK

    @pl.when(j == 0)
    def _():
        carry_ref[...] = jnp.zeros_like(carry_ref)
        s_ref[...] = jnp.zeros_like(s_ref)

    valid = (j * ROW_TILE + _iota((ROW_TILE, 1), 0)) >= FP
    first_row = _iota((ROW_TILE, 1), 0) == 0
    ones_h = _head_ones()
    tri = _tri_incl(CHUNK)
    tri_b = tri.astype(BF16)
    strict = _iota((CHUNK, CHUNK), 1) < _iota((CHUNK, CHUNK), 0)
    eye = (_iota((CHUNK, CHUNK), 1) == _iota((CHUNK, CHUNK), 0)).astype(F32)
    heads = [slice(h * HEAD_DIM, (h + 1) * HEAD_DIM) for h in range(N_HEADS)]

    pro = []
    unit = {}
    for b in range(nb):
        p = jnp.where(valid, p_ref[b], 0.0)
        prev = jnp.where(first_row, carry_ref[b], pltpu.roll(p, 1, 0))
        carry_ref[b] = p[ROW_TILE - 1:ROW_TILE, :]
        ps = p + (prev - p) * mu_ref[...]
        r = ps[:, 0:256]
        k = ps[:, 256:512]
        v = ps[:, 512:768]
        lora_in = ps[:, 768:896]
        xg = ps[:, 896:1024]
        w_log = _log_sigmoid(w0_ref[...] + _bdot(jnp.tanh(lora_in), wup_ref[...])) - 0.5
        lw = jnp.where(valid, -jnp.exp(w_log), 0.0)
        alpha = _sigmoid(a0_ref[...] + _bdot(lora_in, aup_ref[...]))
        gate = _bdot(_sigmoid(xg), gup_ref[...])
        kk = k * kk_ref[...]
        kk = kk / jnp.maximum(jnp.sqrt(_dot_exact_rhs(kk * kk, ones_h)), 1e-12)
        k = k * (1.0 + (alpha - 1.0) * ka_ref[...])
        kka = kk * alpha
        pro.append((r, k, v, gate))
        for c in range(n_chunks):
            sl = slice(c * CHUNK, (c + 1) * CHUNK)
            lw_c = lw[sl]
            cum = _dot_exact_lhs(tri_b, lw_c)
            cum_last = cum[CHUNK - 1:CHUNK, :]
            p_inv = jnp.exp(-cum)
            p_tail = jnp.exp(cum_last - cum)
            unit[b, c] = dict(a=-kk[sl] * jnp.exp(cum - lw_c), b=kka[sl] * p_inv, k=k[sl] * p_inv,
                              r=r[sl] * jnp.exp(cum), kb=k[sl] * p_tail, bb=kka[sl] * p_tail,
                              pl=jnp.exp(cum_last), v=v[sl])

    keys = [(b, c, h) for b in range(nb) for c in range(n_chunks) for h in range(N_HEADS)]
    part = lambda name, key: unit[key[0], key[1]][name][:, heads[key[2]]]
    a_ab = {q: jnp.where(strict, _dot3(part("a", q), part("b", q), _NT), 0.0) for q in keys}
    a_ak = {q: jnp.where(strict, _bdot_nt(part("a", q), part("k", q)), 0.0) for q in keys}
    a_rb = {q: jnp.where(tri, _bdot_nt(part("r", q), part("b", q)), 0.0) for q in keys}
    a_rk = {q: jnp.where(tri, _bdot_nt(part("r", q), part("k", q)), 0.0) for q in keys}
    inv = {q: eye + a_ab[q] for q in keys}
    pw = a_ab
    for _ in range(5):
        pw = {q: _bdot(pw[q], pw[q]) for q in keys}
        inv = {q: inv[q] + _bdot(inv[q], pw[q]) for q in keys}
    ak_v = {q: _bdot(a_ak[q], part("v", q)) for q in keys}
    rk_v = {q: _bdot(a_rk[q], part("v", q)) for q in keys}
    kb_v = {q: _bdot_tn(part("v", q), part("kb", q)) for q in keys}

    bh = [(b, h) for b in range(nb) for h in range(N_HEADS)]
    state = {q: s_ref[q[0], q[1]] for q in bh}
    y_parts = {}
    for c in range(n_chunks):
        full = lambda q: (q[0], c, q[1])
        a_s = {q: _bdot_nt(part("a", full(q)), state[q]) for q in bh}
        r_s = {q: _bdot_nt(part("r", full(q)), state[q]) for q in bh}
        u = {q: _bdot(inv[full(q)], a_s[q] + ak_v[full(q)]) for q in bh}
        for q in bh:
            y_parts[full(q)] = r_s[q] + rk_v[full(q)] + _bdot(a_rb[full(q)], u[q])
        state = {q: (state[q] * part("pl", full(q)) + kb_v[full(q)] + _bdot_tn(u[q], part("bb", full(q))))
                 for q in bh}
    for q in bh:
        s_ref[q[0], q[1]] = state[q]

    for b in range(nb):
        r, k, v, gate = pro[b]
        y = jnp.concatenate([jnp.concatenate([y_parts[b, c, h] for h in range(N_HEADS)], axis=1)
                             for c in range(n_chunks)], axis=0)
        mean = _dot_exact_rhs(y, ones_h) * (1.0 / HEAD_DIM)
        yc = y - mean
        var = _dot_exact_rhs(yc * yc, ones_h) * (1.0 / HEAD_DIM)
        yn = yc * lax.rsqrt(var + RWKV_GN_EPS) * gng_ref[...] + gnb_ref[...]
        bonus = _dot_exact_rhs(r * k * rk_ref[...], ones_h) * v
        y_ref[b] = ((yn + bonus) * gate).astype(y_ref.dtype)


def _rwkv(p_a, B, TP, mu, w_up, w0, a_up, a0, g_up, k_k, k_a, r_k, gn_g, gn_b):
    W = MIX_W
    z64 = jnp.zeros((64, W), F32)
    wup_pad = jnp.concatenate([w_up, z64], axis=0).astype(BF16)
    aup_pad = jnp.concatenate([z64, a_up], axis=0).astype(BF16)
    row = lambda a: a.reshape(1, -1).astype(F32)
    full = lambda shape: pl.BlockSpec(shape, lambda j: (0,) * len(shape))
    return pl.pallas_call(
        _rwkv_kernel,
        grid=(TP // ROW_TILE,),
        in_specs=[pl.BlockSpec((B, ROW_TILE, 1024), lambda j: (0, j, 0)),
                  full((1, 1024)), full((128, W)), full((1, W)), full((128, W)), full((1, W)),
                  full((128, W)), full((1, W)), full((1, W)), full((1, W)), full((1, W)), full((1, W))],
        out_specs=pl.BlockSpec((B, ROW_TILE, W), lambda j: (0, j, 0)),
        out_shape=jax.ShapeDtypeStruct((B, TP, W), BF16),
        scratch_shapes=[pltpu.VMEM((B, 1, 1024), F32), pltpu.VMEM((B, N_HEADS, HEAD_DIM, HEAD_DIM), F32)],
        compiler_params=_cparams("arbitrary"),
        name="rwkv7",
    )(p_a.reshape(B, TP, 1024), row(mu), wup_pad, row(w0), aup_pad, row(a0), g_up.astype(BF16),
      row(k_k), row(k_a), row(r_k), row(gn_g), row(gn_b))


def _gla_kernel(p_ref, aup_ref, ab_ref, ng_ref, y_ref, s_ref):
    j = pl.program_id(0)
    nb = p_ref.shape[0]
    n_chunks = ROW_TILE // CHUNK

    @pl.when(j == 0)
    def _():
        s_ref[...] = jnp.zeros_like(s_ref)

    valid = (j * ROW_TILE + _iota((ROW_TILE, 1), 0)) >= FP
    tri = _tri_incl(CHUNK)
    tri_b = tri.astype(BF16)

    og_all = []
    pre = {}
    for b in range(nb):
        p = jnp.where(valid, p_ref[b], 0.0)
        la = _log_sigmoid(_bdot(p[:, 768:896], aup_ref[...]) + ab_ref[...]) * (1.0 / GLA_TAU)
        la = jnp.where(valid, la, 0.0)
        og_all.append(p[:, 512:768])
        for c in range(n_chunks):
            sl = slice(c * CHUNK, (c + 1) * CHUNK)
            pre[b, c] = dict(q=p[sl, 0:128] * (GLA_DK ** -0.5), k=p[sl, 128:256], v=p[sl, 256:512], la=la[sl])
    bc = [(b, c) for b in range(nb) for c in range(n_chunks)]
    keys = [(b, c, h) for (b, c) in bc for h in range(N_HEADS)]
    ks = [slice(h * GLA_DK, (h + 1) * GLA_DK) for h in range(N_HEADS)]
    vs = [slice(h * HEAD_DIM, (h + 1) * HEAD_DIM) for h in range(N_HEADS)]
    b_cum = {u: _dot_exact_lhs(tri_b, pre[u]["la"]) for u in bc}
    b_last = {u: b_cum[u][CHUNK - 1:CHUNK, :] for u in bc}
    q_g = {u: pre[u]["q"] * jnp.exp(b_cum[u]) for u in bc}
    k_g = {u: pre[u]["k"] * jnp.exp(-b_cum[u]) for u in bc}
    k_l = {u: pre[u]["k"] * jnp.exp(b_last[u] - b_cum[u]) for u in bc}
    dec = {u: jnp.exp(b_last[u]) for u in bc}
    att = {u: jnp.where(tri, _bdot_nt(q_g[u[0], u[1]][:, ks[u[2]]], k_g[u[0], u[1]][:, ks[u[2]]]), 0.0)
           for u in keys}
    att_v = {u: _bdot(att[u], pre[u[0], u[1]]["v"][:, vs[u[2]]]) for u in keys}
    kl_v = {u: _bdot_tn(pre[u[0], u[1]]["v"][:, vs[u[2]]], k_l[u[0], u[1]][:, ks[u[2]]]) for u in keys}

    bh = [(b, h) for b in range(nb) for h in range(N_HEADS)]
    state = {q: s_ref[q[0], q[1]] for q in bh}
    o_parts = {}
    for c in range(n_chunks):
        for q in bh:
            o_parts[q[0], c, q[1]] = att_v[q[0], c, q[1]] + _bdot_nt(q_g[q[0], c][:, ks[q[1]]], state[q])
        state = {q: state[q] * dec[q[0], c][:, ks[q[1]]] + kl_v[q[0], c, q[1]] for q in bh}
    for q in bh:
        s_ref[q[0], q[1]] = state[q]

    ones_h = _head_ones()
    for b in range(nb):
        o = jnp.concatenate([jnp.concatenate([o_parts[b, c, h] for h in range(N_HEADS)], axis=1)
                             for c in range(n_chunks)], axis=0)
        ms = _dot_exact_rhs(o * o, ones_h) * (1.0 / HEAD_DIM)
        y = o * lax.rsqrt(ms + 1e-6) * ng_ref[...] * _silu(og_all[b])
        y_ref[b] = y.astype(y_ref.dtype)


def _gla(p_b, B, TP, a_up, a_b, norm_g):
    aup_pad = jnp.zeros((128, 128), F32).at[:a_up.shape[0]].set(a_up).astype(BF16)
    full = lambda shape: pl.BlockSpec(shape, lambda j: (0,) * len(shape))
    return pl.pallas_call(
        _gla_kernel,
        grid=(TP // ROW_TILE,),
        in_specs=[pl.BlockSpec((B, ROW_TILE, 896), lambda j: (0, j, 0)),
                  full((128, 128)), full((1, 128)), full((1, MIX_W))],
        out_specs=pl.BlockSpec((B, ROW_TILE, MIX_W), lambda j: (0, j, 0)),
        out_shape=jax.ShapeDtypeStruct((B, TP, MIX_W), BF16),
        scratch_shapes=[pltpu.VMEM((B, N_HEADS, HEAD_DIM, GLA_DK), F32)],
        compiler_params=_cparams("arbitrary"),
        name="gla",
    )(p_b.reshape(B, TP, 896), aup_pad, a_b.reshape(1, 128).astype(F32),
      jnp.tile(norm_g.astype(F32), N_HEADS).reshape(1, MIX_W))


def _mlstm_kernel(p_ref, cw_ref, cb_ref, ib_ref, fb_ref, ng_ref, y_ref, carry_ref, c_ref, n_ref, m_ref):
    j = pl.program_id(0)
    nb = p_ref.shape[0]
    n_chunks = ROW_TILE // CHUNK

    @pl.when(j == 0)
    def _():
        carry_ref[...] = jnp.zeros_like(carry_ref)
        c_ref[...] = jnp.zeros_like(c_ref)
        n_ref[...] = jnp.zeros_like(n_ref)
        m_ref[...] = jnp.zeros_like(m_ref)

    valid = (j * ROW_TILE + _iota((ROW_TILE, 1), 0)) >= FP
    tri = _tri_incl(CHUNK)
    tri_b = tri.astype(BF16)
    ones_h = _head_ones()

    og_all = []
    pre = {}
    for b in range(nb):
        p = jnp.where(valid, p_ref[b], 0.0)
        a = p[:, 0:512]
        ext = jnp.concatenate([carry_ref[b], a], axis=0)
        carry_ref[b] = a[ROW_TILE - 8:ROW_TILE, :]
        conv = cb_ref[...] + a * cw_ref[CONV_W - 1:CONV_W, :]
        for s in range(1, CONV_W):
            conv = conv + pltpu.roll(ext, s, 0)[8:8 + ROW_TILE, :] * cw_ref[CONV_W - 1 - s:CONV_W - s, :]
        qk = _silu(conv)
        q = jnp.where(valid, qk[:, 0:MIX_W], 0.0)
        k = jnp.where(valid, qk[:, MIX_W:2 * MIX_W], 0.0) * (HEAD_DIM ** -0.5)
        v = p[:, 512:768]
        og_all.append(p[:, 768:1024])
        gates = p[:, 1024:1152]
        li_all = jnp.where(valid, gates + ib_ref[...], NEG)
        lf_all = jnp.where(valid, _log_sigmoid(gates + fb_ref[...]), 0.0)
        for c in range(n_chunks):
            sl = slice(c * CHUNK, (c + 1) * CHUNK)
            pre[b, c] = dict(q=q[sl], k=k[sl], v=v[sl], li=li_all[sl], lf=lf_all[sl])

    bc = [(b, c) for b in range(nb) for c in range(n_chunks)]
    keys = [(b, c, h) for (b, c) in bc for h in range(N_HEADS)]
    heads = [slice(h * HEAD_DIM, (h + 1) * HEAD_DIM) for h in range(N_HEADS)]
    part = lambda name, u: pre[u[0], u[1]][name][:, heads[u[2]]]
    b_cum = {u: _dot_exact_lhs(tri_b, pre[u]["lf"]) for u in bc}
    b_t = {u: b_cum[u].T for u in bc}
    li_t = {u: pre[u]["li"].T for u in bc}
    b_col = {u: b_cum[u[0], u[1]][:, N_HEADS + u[2]:N_HEADS + u[2] + 1] for u in keys}
    b_last = {u: b_col[u][CHUNK - 1:CHUNK, :] for u in keys}
    d_log = {u: jnp.where(tri, b_col[u] - b_t[u[0], u[1]][N_HEADS + u[2]:N_HEADS + u[2] + 1, :]
                          + li_t[u[0], u[1]][u[2]:u[2] + 1, :], -jnp.inf) for u in keys}
    dmax = {u: jnp.max(d_log[u], axis=1, keepdims=True) for u in keys}
    qk = {u: _bdot_nt(part("q", u), part("k", u)) for u in keys}
    s0 = {u: jnp.exp(d_log[u] - dmax[u]) * qk[u] for u in keys}
    sv = {u: _bdot(s0[u], part("v", u)) for u in keys}
    ssum = {u: jnp.sum(s0[u], axis=1, keepdims=True) for u in keys}
    g_loc = {u: b_last[u] - b_col[u] + pre[u[0], u[1]]["li"][:, u[2]:u[2] + 1] for u in keys}
    m_loc = {u: jnp.max(g_loc[u], axis=0, keepdims=True) for u in keys}
    kw = {u: part("k", u) * jnp.exp(g_loc[u] - m_loc[u]) for u in keys}
    kwv = {u: _bdot_tn(kw[u], part("v", u)) for u in keys}
    kwsum = {u: jnp.sum(kw[u], axis=0, keepdims=True) for u in keys}

    bh = [(b, h) for b in range(nb) for h in range(N_HEADS)]
    c_st = {q: c_ref[q[0], q[1]] for q in bh}
    n_st = {q: n_ref[q[0], q[1]] for q in bh}
    m_st = {q: m_ref[q[0], q[1]] for q in bh}
    h_parts = {}
    for c in range(n_chunks):
        full = lambda q: (q[0], c, q[1])
        qc = {q: _bdot(part("q", full(q)), c_st[q]) for q in bh}
        qn = {q: jnp.sum(part("q", full(q)) * n_st[q], axis=1, keepdims=True) for q in bh}
        inter = {q: b_col[full(q)] + m_st[q] for q in bh}
        m_t = {q: jnp.maximum(inter[q], dmax[full(q)]) for q in bh}
        e_loc = {q: jnp.exp(dmax[full(q)] - m_t[q]) for q in bh}
        w_int = {q: jnp.exp(inter[q] - m_t[q]) for q in bh}
        for q in bh:
            num = e_loc[q] * sv[full(q)] + w_int[q] * qc[q]
            den = e_loc[q] * ssum[full(q)] + w_int[q] * qn[q]
            h_parts[full(q)] = num / jnp.maximum(jnp.abs(den), jnp.exp(-m_t[q]))
        m_new = {q: jnp.maximum(b_last[full(q)] + m_st[q], m_loc[full(q)]) for q in bh}
        s_old = {q: jnp.exp(b_last[full(q)] + m_st[q] - m_new[q]) for q in bh}
        s_new = {q: jnp.exp(m_loc[full(q)] - m_new[q]) for q in bh}
        c_st = {q: s_old[q] * c_st[q] + s_new[q] * kwv[full(q)] for q in bh}
        n_st = {q: s_old[q] * n_st[q] + s_new[q] * kwsum[full(q)] for q in bh}
        m_st = m_new
    for q in bh:
        c_ref[q[0], q[1]], n_ref[q[0], q[1]], m_ref[q[0], q[1]] = c_st[q], n_st[q], m_st[q]

    for b in range(nb):
        hh = jnp.concatenate([jnp.concatenate([h_parts[b, c, h] for h in range(N_HEADS)], axis=1)
                              for c in range(n_chunks)], axis=0) * _sigmoid(og_all[b])
        mean = _dot_exact_rhs(hh, ones_h) * (1.0 / HEAD_DIM)
        hc = hh - mean
        var = _dot_exact_rhs(hc * hc, ones_h) * (1.0 / HEAD_DIM)
        y_ref[b] = (hc * lax.rsqrt(var + 1e-5) * ng_ref[...]).astype(y_ref.dtype)


def _mlstm(p_d, B, TP, conv_w, conv_b, i_b, f_b, norm_g):
    ib = jnp.zeros((1, LANES), F32).at[0, 0:N_HEADS].set(i_b)
    fb = jnp.zeros((1, LANES), F32).at[0, N_HEADS:2 * N_HEADS].set(f_b)
    full = lambda shape: pl.BlockSpec(shape, lambda j: (0,) * len(shape))
    return pl.pallas_call(
        _mlstm_kernel,
        grid=(TP // ROW_TILE,),
        in_specs=[pl.BlockSpec((B, ROW_TILE, 1152), lambda j: (0, j, 0)),
                  full((CONV_W, 512)), full((1, 512)), full((1, LANES)), full((1, LANES)), full((1, MIX_W))],
        out_specs=pl.BlockSpec((B, ROW_TILE, MIX_W), lambda j: (0, j, 0)),
        out_shape=jax.ShapeDtypeStruct((B, TP, MIX_W), BF16),
        scratch_shapes=[pltpu.VMEM((B, 8, 512), F32),
                        pltpu.VMEM((B, N_HEADS, HEAD_DIM, HEAD_DIM), F32),
                        pltpu.VMEM((B, N_HEADS, 1, HEAD_DIM), F32),
                        pltpu.VMEM((B, N_HEADS, 1, 1), F32)],
        compiler_params=_cparams("arbitrary"),
        name="mlstm",
    )(p_d.reshape(B, TP, 1152), conv_w.astype(F32), conv_b.reshape(1, 512).astype(F32), ib, fb,
      norm_g.reshape(1, MIX_W).astype(F32))


V_ROWS = 80
WT_ROWS = 528


def _dsa_prep_kernel(h_ref, wt_ref, wn_ref, kvg_ref, wuk_ref, wuvt_ref,
                     qt_ref, qit_ref, wit_ref, k_ref, ki_ref, vt_ref):
    hb = h_ref[...]
    tm = hb.shape[0]
    pt = lax.dot_general(wt_ref[...], hb, _NT, preferred_element_type=F32)
    pn = jnp.dot(hb, wn_ref[...], preferred_element_type=F32)
    ckv = pn[:, 0:DSA_KV_RANK]
    c = ckv * lax.rsqrt(jnp.mean(ckv * ckv, -1, keepdims=True) + 1e-6) * kvg_ref[...]
    cb = c.astype(BF16)
    k_ref[...] = jnp.dot(cb, wuk_ref[...], preferred_element_type=F32).astype(BF16)
    ki_ref[...] = pn[:, DSA_KV_RANK:DSA_KV_RANK + IDX_DIM].astype(BF16)
    vt = lax.dot_general(wuvt_ref[...], cb, _NT, preferred_element_type=F32)
    vt = jnp.where(_iota((V_ROWS, tm), 0) == HEAD_DIM, 1.0, vt)
    for t in range(tm // LANES):
        cs = slice(t * LANES, (t + 1) * LANES)
        for h in range(N_HEADS):
            qt_ref[t, :, h * LANES:(h + 1) * LANES] = (
                pt[h * HEAD_DIM:(h + 1) * HEAD_DIM, cs] * (HEAD_DIM ** -0.5)).astype(BF16)
        for h in range(IDX_HEADS):
            qit_ref[t, :, h * LANES:(h + 1) * LANES] = pt[MIX_W + h * IDX_DIM:MIX_W + (h + 1) * IDX_DIM, cs].astype(BF16)
        wit_ref[t] = pt[2 * MIX_W:2 * MIX_W + IDX_HEADS, cs] * ((IDX_HEADS * IDX_DIM) ** -0.5)
        vt_ref[t] = vt[:, cs].astype(BF16)


def _dsa_prep(hb, w_t, w_n, kv_norm_g, w_uk, w_uv):
    N, D = hb.shape
    tm = _pick_tile(N, 640)
    nt = tm // LANES
    full = lambda shape: pl.BlockSpec(shape, lambda i: (0,) * len(shape))
    wuvt = jnp.pad(w_uv.T, ((0, V_ROWS - HEAD_DIM), (0, 0))).astype(BF16)
    return pl.pallas_call(
        _dsa_prep_kernel,
        grid=(N // tm,),
        in_specs=[pl.BlockSpec((tm, D), lambda i: (i, 0)),
                  full((WT_ROWS, D)), full((D, 256)), full((1, DSA_KV_RANK)),
                  full((DSA_KV_RANK, HEAD_DIM)), full((V_ROWS, DSA_KV_RANK))],
        out_specs=[pl.BlockSpec((nt, HEAD_DIM, N_HEADS * LANES), lambda i: (i, 0, 0)),
                   pl.BlockSpec((nt, IDX_DIM, IDX_HEADS * LANES), lambda i: (i, 0, 0)),
                   pl.BlockSpec((nt, IDX_HEADS, LANES), lambda i: (i, 0, 0)),
                   pl.BlockSpec((tm, HEAD_DIM), lambda i: (i, 0)),
                   pl.BlockSpec((tm, IDX_DIM), lambda i: (i, 0)),
                   pl.BlockSpec((nt, V_ROWS, LANES), lambda i: (i, 0, 0))],
        out_shape=[jax.ShapeDtypeStruct((N // LANES, HEAD_DIM, N_HEADS * LANES), BF16),
                   jax.ShapeDtypeStruct((N // LANES, IDX_DIM, IDX_HEADS * LANES), BF16),
                   jax.ShapeDtypeStruct((N // LANES, IDX_HEADS, LANES), F32),
                   jax.ShapeDtypeStruct((N, HEAD_DIM), BF16),
                   jax.ShapeDtypeStruct((N, IDX_DIM), BF16),
                   jax.ShapeDtypeStruct((N // LANES, V_ROWS, LANES), BF16)],
        compiler_params=_cparams("arbitrary"),
        name="dsa_prep",
    )(hb, w_t, w_n, kv_norm_g.reshape(1, DSA_KV_RANK).astype(F32), w_uk.astype(BF16), wuvt)


def _dsa_kernel(qt_ref, qit_ref, wit_ref, k_ref, ki_ref, vt_ref, bias_ref, y_ref,
                sk_ref, rel_ref, m_ref, acc_ref, lg_ref, mg_ref, *, topk):
    i = pl.program_id(1)
    nk = i + 1
    QT = ROW_TILE
    HQ = N_HEADS * QT
    t_lane = i * QT + _iota((LANES, QT), 1)
    key_pos = lambda kt: kt * LANES + _iota((LANES, QT), 0)
    per_head = lambda fn: jnp.concatenate([fn(slice(h * QT, (h + 1) * QT)) for h in range(N_HEADS)], axis=1)

    qit = qit_ref[0]
    wit = wit_ref[0]

    GW = rel_ref.shape[1] // LANES
    n_tiles = sk_ref.shape[0] - 2
    n_trips = (i + 2 * GW) // (2 * GW)

    def group_base(g):
        return jnp.clip(GW * g, 0, n_tiles - GW)

    def issue(g, slot):
        span = pl.ds(pl.multiple_of(group_base(g) * LANES, LANES), GW * LANES)
        rel_ref[slot] = jnp.dot(ki_ref[span, :], qit, preferred_element_type=F32)

    def reduce(g, slot):
        for u in range(GW):
            kt = group_base(g) + u
            rows_u = slice(u * LANES, (u + 1) * LANES)
            score = jnp.maximum(rel_ref[slot, rows_u, 0:QT], 0.0) * wit[0:1, :]
            for h in range(1, IDX_HEADS):
                score = score + jnp.maximum(rel_ref[slot, rows_u, h * QT:(h + 1) * QT], 0.0) * wit[h:h + 1, :]
            mine = (kt >= GW * g) & (kt <= i)
            sk_ref[jnp.where(mine, kt, n_tiles)] = score

    issue(0, 0)

    def score_body(jj, c):
        issue(2 * jj + 1, 1)
        reduce(2 * jj, 0)
        issue(2 * jj + 2, 0)
        reduce(2 * jj + 1, 1)
        return c

    lax.fori_loop(0, n_trips, score_body, 0)
    first = sk_ref[0]
    first = jnp.where(key_pos(0) < FP + N_META, jnp.inf, first)
    sk_ref[0] = jnp.where(key_pos(0) >= FP, first, -jnp.inf)
    sk_ref[i] = jnp.where(key_pos(i) <= t_lane, sk_ref[i], -jnp.inf)
    sk_ref[n_tiles + 1] = jnp.full((LANES, QT), -jnp.inf, F32)

    def key_to_float(key):
        return lax.bitcast_convert_type(jnp.where(key < 0, key ^ jnp.int32(0x7FFFFFFF), key), F32)
    def count(pred_fn):
        def body(kt, acc):
            return acc + jnp.where(pred_fn(sk_ref[kt], kt), 1, 0)

        def body4(j, acc):
            for u in range(4):
                acc = body(4 * j + u, acc)
            return acc

        n4 = lax.shift_right_logical(nk, 2)
        acc = lax.fori_loop(0, n4, body4, jnp.zeros((LANES, QT), jnp.int32))
        acc = lax.fori_loop(4 * n4, nk, body, acc)
        return jnp.sum(acc, axis=0, keepdims=True)

    def bit_body(it, carry):
        tau, n_ge = carry
        cand = tau + jnp.left_shift(jnp.int32(1), 31 - it)
        cand_f = key_to_float(cand)
        cnt = count(lambda sk, kt: sk >= cand_f)
        return jnp.where(cnt >= topk, cand, tau), jnp.where(cnt >= topk, cnt, n_ge)

    tau_key, n_ge = lax.fori_loop(0, 32, bit_body, (jnp.full((1, QT), INT_MIN, jnp.int32),
                                                    jnp.zeros((1, QT), jnp.int32)))
    tau = jnp.where(tau_key < KEY_LOWEST, jnp.float32(FLT_LOWEST), key_to_float(tau_key))

    @pl.when(jnp.max(n_ge - topk) > 0)
    def _():
        n_bits = max(1, int(math.ceil(math.log2(sk_ref.shape[0] * LANES + 1))))
        need = topk - count(lambda sk, kt: sk > tau)

        def pos_body(it, x):
            cand = x + jnp.left_shift(jnp.int32(1), n_bits - 1 - it)
            cnt = count(lambda sk, kt: (sk == tau) & (key_pos(kt) < cand))
            return jnp.where(cnt < need, cand, x)

        x = lax.fori_loop(0, n_bits, pos_body, jnp.zeros((1, QT), jnp.int32))
        jmax = jnp.where(n_ge > topk, x, jnp.int32(2 ** 30))

        def drop_body(kt, c):
            sk = sk_ref[kt]
            sk_ref[kt] = jnp.where((sk == tau) & (key_pos(kt) > jmax), -jnp.inf, sk)
            return c

        lax.fori_loop(0, nk, drop_body, 0)

    qt = qt_ref[0]
    m_ref[...] = jnp.full((1, HQ), NEG, F32)
    acc_ref[...] = jnp.zeros((V_ROWS, HQ), F32)

    def park(g, slot):
        base = group_base(g)
        span = pl.ds(pl.multiple_of(base * LANES, LANES), GW * LANES)
        lg_all = jnp.dot(k_ref[span, :], qt, preferred_element_type=F32)
        tmax = None
        for u in range(GW):
            t = base + u
            mine = (t >= GW * g) & (t <= i)
            lg = lg_all[u * LANES:(u + 1) * LANES, :] + bias_ref[jnp.clip(i - t, 0, 2)]
            sel = sk_ref[jnp.where(mine, t, n_tiles + 1)] >= tau
            lgm = per_head(lambda hs: jnp.where(sel, lg[:, hs], NEG))
            lg_ref[slot, u] = lgm
            tmax = lgm if tmax is None else jnp.maximum(tmax, lgm)
        mg_ref[slot] = jnp.max(tmax, axis=0, keepdims=True)

    def weights(slot):
        m_old = m_ref[...]
        m_new = jnp.maximum(m_old, mg_ref[slot])
        m_ref[...] = m_new
        return jnp.exp(m_old - m_new), [jnp.exp(lg_ref[slot, u] - m_new).astype(BF16) for u in range(GW)]

    def fold(g, corr, prs):
        vt_all = jnp.concatenate([vt_ref[group_base(g) + u] for u in range(GW)], axis=1)
        pv = jnp.dot(vt_all, jnp.concatenate(prs, axis=0), preferred_element_type=F32)
        acc_ref[...] = acc_ref[...] * corr + pv

    park(0, 0)

    def pipe_body(jj, c):
        corr, prs = weights(0)
        park(2 * jj + 1, 1)
        fold(2 * jj, corr, prs)
        corr, prs = weights(1)
        park(2 * jj + 2, 0)
        fold(2 * jj + 1, corr, prs)
        return c

    lax.fori_loop(0, n_trips, pipe_body, 0)
    acc = acc_ref[...]
    out = acc[0:HEAD_DIM, :] / jnp.maximum(acc[HEAD_DIM:HEAD_DIM + 1, :], 1e-30)
    y_ref[...] = per_head(lambda hs: out[:, hs].T).astype(y_ref.dtype)


def _t5_bucket(dist):
    max_exact = N_BUCKETS // 2
    n = jnp.maximum(dist, 0)
    large = max_exact + (jnp.log(jnp.maximum(n, 1).astype(F32) / max_exact)
                         / math.log(MAX_DISTANCE / max_exact) * (N_BUCKETS - max_exact)).astype(jnp.int32)
    return jnp.where(n < max_exact, n, jnp.minimum(large, N_BUCKETS - 1))


def _bias_tables(rel_bias):
    per_dist = rel_bias[_t5_bucket(jnp.arange(2 * ROW_TILE, dtype=jnp.int32))]
    q_minus_s = np.arange(ROW_TILE)[None, :] - np.arange(ROW_TILE)[:, None]
    far = per_dist[2 * ROW_TILE - 1]
    tabs = [per_dist[np.clip(r * ROW_TILE + q_minus_s, 0, 2 * ROW_TILE - 1)] - far for r in (0, 1)]
    tabs.append(jnp.zeros_like(tabs[0]))
    return jnp.stack(tabs).transpose(0, 1, 3, 2).reshape(3, ROW_TILE, N_HEADS * ROW_TILE).astype(F32)


def _dsa(qt, qit, wit, k, ki, vt, bias_tab, B, TP, topk):
    nq = TP // ROW_TILE
    return pl.pallas_call(
        functools.partial(_dsa_kernel, topk=topk),
        grid=(B, nq),
        in_specs=[pl.BlockSpec((1, HEAD_DIM, N_HEADS * LANES), lambda b, i: (b * nq + i, 0, 0)),
                  pl.BlockSpec((1, IDX_DIM, IDX_HEADS * LANES), lambda b, i: (b * nq + i, 0, 0)),
                  pl.BlockSpec((1, IDX_HEADS, LANES), lambda b, i: (b * nq + i, 0, 0)),
                  pl.BlockSpec((TP, HEAD_DIM), lambda b, i: (b, 0)),
                  pl.BlockSpec((TP, IDX_DIM), lambda b, i: (b, 0)),
                  pl.BlockSpec((nq, V_ROWS, LANES), lambda b, i: (b, 0, 0)),
                  pl.BlockSpec((3, ROW_TILE, N_HEADS * ROW_TILE), lambda b, i: (0, 0, 0))],
        out_specs=pl.BlockSpec((ROW_TILE, MIX_W), lambda b, i: (b * nq + i, 0)),
        out_shape=jax.ShapeDtypeStruct((B * TP, MIX_W), BF16),
        scratch_shapes=[pltpu.VMEM((nq + 2, LANES, ROW_TILE), F32),
                        pltpu.VMEM((2, min(4, nq) * LANES, IDX_HEADS * ROW_TILE), F32),
                        pltpu.VMEM((1, N_HEADS * ROW_TILE), F32),
                        pltpu.VMEM((V_ROWS, N_HEADS * ROW_TILE), F32),
                        pltpu.VMEM((2, min(4, nq), LANES, N_HEADS * ROW_TILE), F32),
                        pltpu.VMEM((2, 1, N_HEADS * ROW_TILE), F32)],
        compiler_params=_cparams("parallel", "arbitrary"),
        name="dsa_attend",
    )(qt, qit, wit, k, ki, vt, bias_tab)


def _layer_norm_rows(z, g, b):
    mu = jnp.mean(z, -1, keepdims=True)
    zc = z - mu
    var = jnp.mean(zc * zc, -1, keepdims=True)
    return zc * lax.rsqrt(var + LN_EPS) * g + b


def _merge_kernel(h_ref, hb_ref, wg_ref, ya_ref, yb_ref, yc_ref, yd_ref, wb_ref, wo_ref, lg_ref, lb_ref,
                  h1_ref, h1b_ref):
    hb = hb_ref[...]
    merged = None
    for i, y_ref in enumerate((ya_ref, yb_ref, yc_ref, yd_ref)):
        gate = _sigmoid(jnp.dot(hb, wg_ref[:, i * D_MODEL:(i + 1) * D_MODEL], preferred_element_type=F32))
        t = gate * jnp.dot(y_ref[...], wb_ref[i], preferred_element_type=F32)
        merged = t if merged is None else merged + t
    z = DN_ALPHA * h_ref[...] + jnp.dot(merged.astype(BF16), wo_ref[...], preferred_element_type=F32)
    y = _layer_norm_rows(z, lg_ref[...], lb_ref[...])
    h1_ref[...] = y
    h1b_ref[...] = y.astype(BF16)


def _merge(h, hb, w_g, ys, w_branch, w_out, ln_g, ln_b):
    N, D = h.shape
    tm = _pick_tile(N, 640)
    full = lambda shape: pl.BlockSpec(shape, lambda i: (0,) * len(shape))
    tok = lambda w: pl.BlockSpec((tm, w), lambda i: (i, 0))
    return pl.pallas_call(
        _merge_kernel,
        grid=(N // tm,),
        in_specs=[tok(D), tok(D), full((D, 4 * D)), tok(MIX_W), tok(MIX_W), tok(MIX_W), tok(MIX_W),
                  full((4, MIX_W, D)), full((D, D)), full((1, D)), full((1, D))],
        out_specs=[tok(D), tok(D)],
        out_shape=[jax.ShapeDtypeStruct((N, D), F32), jax.ShapeDtypeStruct((N, D), BF16)],
        compiler_params=_cparams("arbitrary"),
        name="merge_out_ln",
    )(h, hb, w_g, *ys, w_branch.astype(BF16), w_out.astype(BF16),
      ln_g.reshape(1, D).astype(F32), ln_b.reshape(1, D).astype(F32))


def _moe_kernel(h_ref, hb_ref, wr_ref, br_ref, wg_ref, wu_ref, wd_ref, lg_ref, lb_ref, o_ref, ob_ref,
                gate_ref, acc_ref):
    e = pl.program_id(1)
    xb = hb_ref[...]
    tm = xb.shape[0]
    lane = _iota((tm, LANES), 1)

    @pl.when(e == 0)
    def _():
        logit = jnp.dot(xb, wr_ref[...], preferred_element_type=F32) + br_ref[...]
        big = jnp.int32(LANES)
        gl = jnp.where(lane < N_GROUPS, logit, -jnp.inf)
        gmax = jnp.max(gl, axis=1, keepdims=True)
        g_sel = jnp.min(jnp.where(gl == gmax, lane, big), axis=1, keepdims=True)
        p_grp = 1.0 / jnp.sum(jnp.exp(gl - gmax), axis=1, keepdims=True)
        lo = N_GROUPS + g_sel * EPG
        el = jnp.where((lane >= lo) & (lane < lo + EPG), logit, -jnp.inf)
        v1 = jnp.max(el, axis=1, keepdims=True)
        i1 = jnp.min(jnp.where(el == v1, lane, big), axis=1, keepdims=True)
        el2 = jnp.where(lane == i1, -jnp.inf, el)
        v2 = jnp.max(el2, axis=1, keepdims=True)
        i2 = jnp.min(jnp.where(el2 == v2, lane, big), axis=1, keepdims=True)
        e2 = jnp.exp(v2 - v1)
        w1 = p_grp / (1.0 + e2)
        w2 = p_grp * e2 / (1.0 + e2)
        gate_ref[...] = jnp.where(lane == i1, w1, 0.0) + jnp.where(lane == i2, w2, 0.0)
        acc_ref[...] = jnp.zeros_like(acc_ref)

    hid = _silu(jnp.dot(xb, wg_ref[0], preferred_element_type=F32)) * jnp.dot(xb, wu_ref[0], preferred_element_type=F32)
    gates = gate_ref[...]
    scaled = []
    for j in range(EPG):
        g_j = jnp.sum(jnp.where(lane == e * EPG + j + N_GROUPS, gates, 0.0), axis=1, keepdims=True)
        scaled.append((hid[:, j * D_EXPERT:(j + 1) * D_EXPERT] * g_j).astype(BF16))
    acc_ref[...] += jnp.dot(jnp.concatenate(scaled, axis=1), wd_ref[0], preferred_element_type=F32)

    @pl.when(e == N_GROUPS - 1)
    def _():
        y = _layer_norm_rows(DN_ALPHA * h_ref[...] + acc_ref[...], lg_ref[...], lb_ref[...])
        o_ref[...] = y
        ob_ref[...] = y.astype(BF16)


def _moe(h1, h1b, w_grp, b_grp, w_rt, b_rt, w_gate, w_up, w_down, ln_g, ln_b):
    N, D = h1.shape
    tm = _pick_tile(N, 640)
    w_r = jnp.zeros((D, LANES), F32).at[:, 0:N_GROUPS].set(w_grp).at[:, N_GROUPS:N_GROUPS + N_EXPERTS].set(w_rt)
    b_r = jnp.zeros((1, LANES), F32).at[0, 0:N_GROUPS].set(b_grp).at[0, N_GROUPS:N_GROUPS + N_EXPERTS].set(b_rt)
    GH = EPG * D_EXPERT
    by_group = lambda w: w.reshape(N_GROUPS, EPG, D, D_EXPERT).transpose(0, 2, 1, 3).reshape(N_GROUPS, D, GH)
    w_gate, w_up, w_down = by_group(w_gate), by_group(w_up), w_down.reshape(N_GROUPS, GH, D)
    full = lambda shape: pl.BlockSpec(shape, lambda i, e: (0,) * len(shape))
    tok = lambda w: pl.BlockSpec((tm, w), lambda i, e: (i, 0))
    return pl.pallas_call(
        _moe_kernel,
        grid=(N // tm, N_GROUPS),
        in_specs=[tok(D), tok(D), full((D, LANES)), full((1, LANES)),
                  pl.BlockSpec((1, D, GH), lambda i, e: (e, 0, 0)),
                  pl.BlockSpec((1, D, GH), lambda i, e: (e, 0, 0)),
                  pl.BlockSpec((1, GH, D), lambda i, e: (e, 0, 0)),
                  full((1, D)), full((1, D))],
        out_specs=[tok(D), tok(D)],
        out_shape=[jax.ShapeDtypeStruct((N, D), F32), jax.ShapeDtypeStruct((N, D), BF16)],
        scratch_shapes=[pltpu.VMEM((tm, LANES), F32), pltpu.VMEM((tm, D), F32)],
        compiler_params=_cparams("arbitrary", "arbitrary"),
        name="hier_moe_ln",
    )(h1, h1b, w_r.astype(BF16), b_r, w_gate.astype(BF16), w_up.astype(BF16), w_down.astype(BF16),
      ln_g.reshape(1, D).astype(F32), ln_b.reshape(1, D).astype(F32))


def _pad_cols(w, width):
    return jnp.pad(w, ((0, 0), (0, width - w.shape[1])))


def _split_w_in(w):
    o = 0
    w_a = w[:, o:o + 1024]; o += 1024
    gq, gk, gv, ga, gg = (w[:, o:o + 128], w[:, o + 128:o + 256], w[:, o + 256:o + 512],
                          w[:, o + 512:o + 528], w[:, o + 528:o + 784]); o += 784
    w_b = _pad_cols(jnp.concatenate([gq, gk, gv, gg, ga], axis=1), 896)
    cq, ckv, cqi, cki, cwi = (w[:, o:o + 256], w[:, o + 256:o + 384], w[:, o + 384:o + 640],
                              w[:, o + 640:o + 672], w[:, o + 672:o + 680]); o += 680
    w_t = jnp.pad(jnp.concatenate([cq.T, cqi.T, cwi.T], axis=0), ((0, WT_ROWS - 2 * MIX_W - IDX_HEADS), (0, 0)))
    w_n = _pad_cols(jnp.concatenate([ckv, cki], axis=1), 256)
    dq, dk, dv, di, df, do = (w[:, o:o + 256], w[:, o + 256:o + 512], w[:, o + 512:o + 768],
                              w[:, o + 768:o + 772], w[:, o + 772:o + 776], w[:, o + 776:o + 1032]); o += 1032
    w_d = _pad_cols(jnp.concatenate([dq, dk, dv, do, di, df], axis=1), 1152)
    w_g = w[:, o:o + 4096]
    bf = lambda a: a.astype(BF16)
    return bf(w_a), bf(w_b), bf(w_t), bf(w_n), bf(w_d), bf(w_g)


def kernel(x, meta, ln_in_g, ln_in_b, rel_bias, w_in, rwkv_mu, rwkv_w_up, rwkv_w0, rwkv_a_up, rwkv_a0, rwkv_g_up, rwkv_k_k, rwkv_k_a, rwkv_r_k, rwkv_gn_g, rwkv_gn_b, gla_a_up, gla_a_b, gla_norm_g, dsa_kv_norm_g, dsa_w_uk, dsa_w_uv, mlstm_conv_w, mlstm_conv_b, mlstm_i_b, mlstm_f_b, mlstm_norm_g, w_branch, w_out, ln1_g, ln1_b, moe_w_grp, moe_b_grp, moe_w_rt, moe_b_rt, moe_w_gate, moe_w_up, moe_w_down, ln2_g, ln2_b):
    B, S, D = x.shape
    assert D == D_MODEL and S % ROW_TILE == 0
    TP = S + FRONT
    N = B * TP
    topk = min(TOPK_MAX, S // 4)
    bias_tab = _bias_tables(rel_bias)

    h, hb = _embed(x, meta, ln_in_g, ln_in_b)
    h = h.reshape(N, D)
    hb = hb.reshape(N, D)
    for l in range(DEPTH):
        w_a, w_b, w_t, w_n, w_d, w_g = _split_w_in(w_in[l])
        p_a = _proj(hb, w_a)
        p_b = _proj(hb, w_b)
        p_d = _proj(hb, w_d)
        qt, qit, wit, k, ki, vt = _dsa_prep(hb, w_t, w_n, dsa_kv_norm_g[l], dsa_w_uk[l], dsa_w_uv[l])
        y_a = _rwkv(p_a, B, TP, rwkv_mu[l], rwkv_w_up[l], rwkv_w0[l], rwkv_a_up[l], rwkv_a0[l], rwkv_g_up[l],
                    rwkv_k_k[l], rwkv_k_a[l], rwkv_r_k[l], rwkv_gn_g[l], rwkv_gn_b[l])
        y_b = _gla(p_b, B, TP, gla_a_up[l], gla_a_b[l], gla_norm_g[l])
        y_c = _dsa(qt, qit, wit, k, ki, vt, bias_tab, B, TP, topk)
        y_d = _mlstm(p_d, B, TP, mlstm_conv_w[l], mlstm_conv_b[l], mlstm_i_b[l], mlstm_f_b[l], mlstm_norm_g[l])
        ys = (y_a.reshape(N, MIX_W), y_b.reshape(N, MIX_W), y_c, y_d.reshape(N, MIX_W))
        h1, h1b = _merge(h, hb, w_g, ys, w_branch[l], w_out[l], ln1_g[l], ln1_b[l])
        h, hb = _moe(h1, h1b, moe_w_grp[l], moe_b_grp[l], moe_w_rt[l], moe_b_rt[l],
                     moe_w_gate[l], moe_w_up[l], moe_w_down[l], ln2_g[l], ln2_b[l])
    return h.reshape(B, TP, D)[:, FRONT:]
```

```python
import functools
import math

import numpy as np
import jax
import jax.numpy as jnp
from jax import lax
from jax.experimental import pallas as pl
from jax.experimental.pallas import tpu as pltpu

F32 = jnp.float32
BF16 = jnp.bfloat16

D_MODEL = 1024
HEAD_DIM = 64
N_HEADS = 4
MIX_W = 256
N_META = 16
CHUNK = 64
LANES = 128
ROW_TILE = 128
FRONT = ROW_TILE
FP = FRONT - N_META
NEG = -1e30
LN_EPS = 1e-5
DEPTH = 2
DN_ALPHA = (2 * DEPTH) ** 0.25

RWKV_GN_EPS = HEAD_DIM * 1e-5
GLA_DK = 32
GLA_TAU = 16.0
DSA_KV_RANK = 128
IDX_HEADS = 8
IDX_DIM = 32
TOPK_MAX = 256
N_BUCKETS = 32
MAX_DISTANCE = 128
CONV_W = 4
N_GROUPS = 4
EPG = 4
N_EXPERTS = 16
D_EXPERT = 256

INT_MIN = -(2 ** 31)
FLT_LOWEST = float(np.finfo(np.float32).min)
KEY_LOWEST = -(2 ** 31) + 0x00800000
VMEM_LIMIT = 56 * 1024 * 1024


def _cparams(*sem):
    return pltpu.CompilerParams(dimension_semantics=tuple(sem), vmem_limit_bytes=VMEM_LIMIT)


def _pick_tile(n, target):
    best = LANES
    t = LANES
    while t <= min(n, target):
        if n % t == 0:
            best = t
        t += LANES
    return best


def _bdot(a, b):
    return jnp.dot(a.astype(BF16), b.astype(BF16), preferred_element_type=F32)


def _bdot_nt(a, b):
    return lax.dot_general(a.astype(BF16), b.astype(BF16), (((1,), (1,)), ((), ())),
                           preferred_element_type=F32)


def _bdot_tn(a, b):
    return lax.dot_general(a.astype(BF16), b.astype(BF16), (((0,), (0,)), ((), ())),
                           preferred_element_type=F32)


def _split(a):
    hi = a.astype(BF16)
    lo = (a - hi.astype(F32)).astype(BF16)
    return hi, lo


_NN = (((1,), (0,)), ((), ()))
_NT = (((1,), (1,)), ((), ()))


def _dot3(a, b, dims=_NN):
    ah, al = _split(a)
    bh, bl = _split(b)
    dg = lambda x, y: lax.dot_general(x, y, dims, preferred_element_type=F32)
    return dg(ah, bh) + (dg(ah, bl) + dg(al, bh))


def _dot_exact_lhs(a_bf16, b):
    bh, bl = _split(b)
    return (jnp.dot(a_bf16, bh, preferred_element_type=F32)
            + jnp.dot(a_bf16, bl, preferred_element_type=F32))


def _dot_exact_rhs(a, b_bf16):
    ah, al = _split(a)
    return (jnp.dot(ah, b_bf16, preferred_element_type=F32)
            + jnp.dot(al, b_bf16, preferred_element_type=F32))


def _sigmoid(x):
    return 1.0 / (1.0 + jnp.exp(-x))


def _log_sigmoid(x):
    return jnp.minimum(x, 0.0) - jnp.log(1.0 + jnp.exp(-jnp.abs(x)))


def _silu(x):
    return x * _sigmoid(x)


def _iota(shape, dim):
    return lax.broadcasted_iota(jnp.int32, shape, dim)


def _tri_incl(n):
    return (_iota((n, n), 1) <= _iota((n, n), 0))


def _head_ones():
    return ((_iota((MIX_W, MIX_W), 0) // HEAD_DIM) == (_iota((MIX_W, MIX_W), 1) // HEAD_DIM)).astype(BF16)


def _embed_kernel(x_ref, meta_ref, g_ref, b_ref, h_ref, hb_ref):
    j = pl.program_id(0)
    for bi in range(x_ref.shape[0]):
        src = jnp.where(j == 0, meta_ref[...], x_ref[bi])
        mu = jnp.mean(src, -1, keepdims=True)
        xc = src - mu
        var = jnp.mean(xc * xc, -1, keepdims=True)
        y = xc * lax.rsqrt(var + LN_EPS) * g_ref[...] + b_ref[...]
        h_ref[bi] = y
        hb_ref[bi] = y.astype(BF16)


def _embed(x, meta, g, b):
    B, S, D = x.shape
    TP = S + FRONT
    meta_pad = jnp.concatenate([jnp.zeros((FP, D), F32), meta.astype(F32)], axis=0)
    return pl.pallas_call(
        _embed_kernel,
        grid=(TP // ROW_TILE,),
        in_specs=[
            pl.BlockSpec((B, ROW_TILE, D), lambda j: (0, jnp.maximum(j - 1, 0), 0)),
            pl.BlockSpec((ROW_TILE, D), lambda j: (0, 0)),
            pl.BlockSpec((1, D), lambda j: (0, 0)),
            pl.BlockSpec((1, D), lambda j: (0, 0)),
        ],
        out_specs=[
            pl.BlockSpec((B, ROW_TILE, D), lambda j: (0, j, 0)),
            pl.BlockSpec((B, ROW_TILE, D), lambda j: (0, j, 0)),
        ],
        out_shape=[jax.ShapeDtypeStruct((B, TP, D), F32), jax.ShapeDtypeStruct((B, TP, D), BF16)],
        compiler_params=_cparams("arbitrary"),
        name="embed_ln",
    )(x, meta_pad, g.reshape(1, D), b.reshape(1, D))


def _proj_kernel(h_ref, w_ref, o_ref):
    o_ref[...] = jnp.dot(h_ref[...], w_ref[...], preferred_element_type=F32)


def _proj(hb, w):
    N, D = hb.shape
    W = w.shape[1]
    tm = _pick_tile(N, 1280)
    return pl.pallas_call(
        _proj_kernel,
        grid=(N // tm,),
        in_specs=[pl.BlockSpec((tm, D), lambda i: (i, 0)),
                  pl.BlockSpec((D, W), lambda i: (0, 0))],
        out_specs=pl.BlockSpec((tm, W), lambda i: (i, 0)),
        out_shape=jax.ShapeDtypeStruct((N, W), F32),
        compiler_params=_cparams("arbitrary"),
        name="in_proj",
    )(hb, w)


def _rwkv_kernel(p_ref, mu_ref, wup_ref, w0_ref, aup_ref, a0_ref, gup_ref, kk_ref, ka_ref, rk_ref,
                 gng_ref, gnb_ref, y_ref, carry_ref, s_ref):
    j = pl.program_id(0)
    nb = p_ref.shape[0]
    n_chunks = ROW_TILE // CHUNK

    @pl.when(j == 0)
    def _():
        carry_ref[...] = jnp.zeros_like(carry_ref)
        s_ref[...] = jnp.zeros_like(s_ref)

    valid = (j * ROW_TILE + _iota((ROW_TILE, 1), 0)) >= FP
    first_row = _iota((ROW_TILE, 1), 0) == 0
    ones_h = _head_ones()
    tri = _tri_incl(CHUNK)
    tri_b = tri.astype(BF16)
    strict = _iota((CHUNK, CHUNK), 1) < _iota((CHUNK, CHUNK), 0)
    eye = (_iota((CHUNK, CHUNK), 1) == _iota((CHUNK, CHUNK), 0)).astype(F32)
    heads = [slice(h * HEAD_DIM, (h + 1) * HEAD_DIM) for h in range(N_HEADS)]

    pro = []
    unit = {}
    for b in range(nb):
        p = jnp.where(valid, p_ref[b], 0.0)
        prev = jnp.where(first_row, carry_ref[b], pltpu.roll(p, 1, 0))
        carry_ref[b] = p[ROW_TILE - 1:ROW_TILE, :]
        ps = p + (prev - p) * mu_ref[...]
        r = ps[:, 0:256]
        k = ps[:, 256:512]
        v = ps[:, 512:768]
        lora_in = ps[:, 768:896]
        xg = ps[:, 896:1024]
        w_log = _log_sigmoid(w0_ref[...] + _bdot(jnp.tanh(lora_in), wup_ref[...])) - 0.5
        lw = jnp.where(valid, -jnp.exp(w_log), 0.0)
        alpha = _sigmoid(a0_ref[...] + _bdot(lora_in, aup_ref[...]))
        gate = _bdot(_sigmoid(xg), gup_ref[...])
        kk = k * kk_ref[...]
        kk = kk / jnp.maximum(jnp.sqrt(_dot_exact_rhs(kk * kk, ones_h)), 1e-12)
        k = k * (1.0 + (alpha - 1.0) * ka_ref[...])
        kka = kk * alpha
        pro.append((r, k, v, gate))
        for c in range(n_chunks):
            sl = slice(c * CHUNK, (c + 1) * CHUNK)
            lw_c = lw[sl]
            cum = _dot_exact_lhs(tri_b, lw_c)
            cum_last = cum[CHUNK - 1:CHUNK, :]
            p_inv = jnp.exp(-cum)
            p_tail = jnp.exp(cum_last - cum)
            unit[b, c] = dict(a=-kk[sl] * jnp.exp(cum - lw_c), b=kka[sl] * p_inv, k=k[sl] * p_inv,
                              r=r[sl] * jnp.exp(cum), kb=k[sl] * p_tail, bb=kka[sl] * p_tail,
                              pl=jnp.exp(cum_last), v=v[sl])

    keys = [(b, c, h) for b in range(nb) for c in range(n_chunks) for h in range(N_HEADS)]
    part = lambda name, key: unit[key[0], key[1]][name][:, heads[key[2]]]
    a_ab = {q: jnp.where(strict, _dot3(part("a", q), part("b", q), _NT), 0.0) for q in keys}
    a_ak = {q: jnp.where(strict, _bdot_nt(part("a", q), part("k", q)), 0.0) for q in keys}
    a_rb = {q: jnp.where(tri, _bdot_nt(part("r", q), part("b", q)), 0.0) for q in keys}
    a_rk = {q: jnp.where(tri, _bdot_nt(part("r", q), part("k", q)), 0.0) for q in keys}
    inv = {q: eye + a_ab[q] for q in keys}
    pw = a_ab
    for _ in range(5):
        pw = {q: _bdot(pw[q], pw[q]) for q in keys}
        inv = {q: inv[q] + _bdot(inv[q], pw[q]) for q in keys}
    ak_v = {q: _bdot(a_ak[q], part("v", q)) for q in keys}
    rk_v = {q: _bdot(a_rk[q], part("v", q)) for q in keys}
    kb_v = {q: _bdot_tn(part("v", q), part("kb", q)) for q in keys}

    bh = [(b, h) for b in range(nb) for h in range(N_HEADS)]
    state = {q: s_ref[q[0], q[1]] for q in bh}
    y_parts = {}
    for c in range(n_chunks):
        full = lambda q: (q[0], c, q[1])
        a_s = {q: _bdot_nt(part("a", full(q)), state[q]) for q in bh}
        r_s = {q: _bdot_nt(part("r", full(q)), state[q]) for q in bh}
        u = {q: _bdot(inv[full(q)], a_s[q] + ak_v[full(q)]) for q in bh}
        for q in bh:
            y_parts[full(q)] = r_s[q] + rk_v[full(q)] + _bdot(a_rb[full(q)], u[q])
        state = {q: (state[q] * part("pl", full(q)) + kb_v[full(q)] + _bdot_tn(u[q], part("bb", full(q))))
                 for q in bh}
    for q in bh:
        s_ref[q[0], q[1]] = state[q]

    for b in range(nb):
        r, k, v, gate = pro[b]
        y = jnp.concatenate([jnp.concatenate([y_parts[b, c, h] for h in range(N_HEADS)], axis=1)
                             for c in range(n_chunks)], axis=0)
        mean = _dot_exact_rhs(y, ones_h) * (1.0 / HEAD_DIM)
        yc = y - mean
        var = _dot_exact_rhs(yc * yc, ones_h) * (1.0 / HEAD_DIM)
        yn = yc * lax.rsqrt(var + RWKV_GN_EPS) * gng_ref[...] + gnb_ref[...]
        bonus = _dot_exact_rhs(r * k * rk_ref[...], ones_h) * v
        y_ref[b] = ((yn + bonus) * gate).astype(y_ref.dtype)


def _rwkv(p_a, B, TP, mu, w_up, w0, a_up, a0, g_up, k_k, k_a, r_k, gn_g, gn_b):
    W = MIX_W
    z64 = jnp.zeros((64, W), F32)
    wup_pad = jnp.concatenate([w_up, z64], axis=0).astype(BF16)
    aup_pad = jnp.concatenate([z64, a_up], axis=0).astype(BF16)
    row = lambda a: a.reshape(1, -1).astype(F32)
    full = lambda shape: pl.BlockSpec(shape, lambda j: (0,) * len(shape))
    return pl.pallas_call(
        _rwkv_kernel,
        grid=(TP // ROW_TILE,),
        in_specs=[pl.BlockSpec((B, ROW_TILE, 1024), lambda j: (0, j, 0)),
                  full((1, 1024)), full((128, W)), full((1, W)), full((128, W)), full((1, W)),
                  full((128, W)), full((1, W)), full((1, W)), full((1, W)), full((1, W)), full((1, W))],
        out_specs=pl.BlockSpec((B, ROW_TILE, W), lambda j: (0, j, 0)),
        out_shape=jax.ShapeDtypeStruct((B, TP, W), BF16),
        scratch_shapes=[pltpu.VMEM((B, 1, 1024), F32), pltpu.VMEM((B, N_HEADS, HEAD_DIM, HEAD_DIM), F32)],
        compiler_params=_cparams("arbitrary"),
        name="rwkv7",
    )(p_a.reshape(B, TP, 1024), row(mu), wup_pad, row(w0), aup_pad, row(a0), g_up.astype(BF16),
      row(k_k), row(k_a), row(r_k), row(gn_g), row(gn_b))


def _gla_kernel(p_ref, aup_ref, ab_ref, ng_ref, y_ref, s_ref):
    j = pl.program_id(0)
    nb = p_ref.shape[0]
    n_chunks = ROW_TILE // CHUNK

    @pl.when(j == 0)
    def _():
        s_ref[...] = jnp.zeros_like(s_ref)

    valid = (j * ROW_TILE + _iota((ROW_TILE, 1), 0)) >= FP
    tri = _tri_incl(CHUNK)
    tri_b = tri.astype(BF16)

    og_all = []
    pre = {}
    for b in range(nb):
        p = jnp.where(valid, p_ref[b], 0.0)
        la = _log_sigmoid(_bdot(p[:, 768:896], aup_ref[...]) + ab_ref[...]) * (1.0 / GLA_TAU)
        la = jnp.where(valid, la, 0.0)
        og_all.append(p[:, 512:768])
        for c in range(n_chunks):
            sl = slice(c * CHUNK, (c + 1) * CHUNK)
            pre[b, c] = dict(q=p[sl, 0:128] * (GLA_DK ** -0.5), k=p[sl, 128:256], v=p[sl, 256:512], la=la[sl])
    bc = [(b, c) for b in range(nb) for c in range(n_chunks)]
    keys = [(b, c, h) for (b, c) in bc for h in range(N_HEADS)]
    ks = [slice(h * GLA_DK, (h + 1) * GLA_DK) for h in range(N_HEADS)]
    vs = [slice(h * HEAD_DIM, (h + 1) * HEAD_DIM) for h in range(N_HEADS)]
    b_cum = {u: _dot_exact_lhs(tri_b, pre[u]["la"]) for u in bc}
    b_last = {u: b_cum[u][CHUNK - 1:CHUNK, :] for u in bc}
    q_g = {u: pre[u]["q"] * jnp.exp(b_cum[u]) for u in bc}
    k_g = {u: pre[u]["k"] * jnp.exp(-b_cum[u]) for u in bc}
    k_l = {u: pre[u]["k"] * jnp.exp(b_last[u] - b_cum[u]) for u in bc}
    dec = {u: jnp.exp(b_last[u]) for u in bc}
    att = {u: jnp.where(tri, _bdot_nt(q_g[u[0], u[1]][:, ks[u[2]]], k_g[u[0], u[1]][:, ks[u[2]]]), 0.0)
           for u in keys}
    att_v = {u: _bdot(att[u], pre[u[0], u[1]]["v"][:, vs[u[2]]]) for u in keys}
    kl_v = {u: _bdot_tn(pre[u[0], u[1]]["v"][:, vs[u[2]]], k_l[u[0], u[1]][:, ks[u[2]]]) for u in keys}

    bh = [(b, h) for b in range(nb) for h in range(N_HEADS)]
    state = {q: s_ref[q[0], q[1]] for q in bh}
    o_parts = {}
    for c in range(n_chunks):
        for q in bh:
            o_parts[q[0], c, q[1]] = att_v[q[0], c, q[1]] + _bdot_nt(q_g[q[0], c][:, ks[q[1]]], state[q])
        state = {q: state[q] * dec[q[0], c][:, ks[q[1]]] + kl_v[q[0], c, q[1]] for q in bh}
    for q in bh:
        s_ref[q[0], q[1]] = state[q]

    ones_h = _head_ones()
    for b in range(nb):
        o = jnp.concatenate([jnp.concatenate([o_parts[b, c, h] for h in range(N_HEADS)], axis=1)
                             for c in range(n_chunks)], axis=0)
        ms = _dot_exact_rhs(o * o, ones_h) * (1.0 / HEAD_DIM)
        y = o * lax.rsqrt(ms + 1e-6) * ng_ref[...] * _silu(og_all[b])
        y_ref[b] = y.astype(y_ref.dtype)


def _gla(p_b, B, TP, a_up, a_b, norm_g):
    aup_pad = jnp.zeros((128, 128), F32).at[:a_up.shape[0]].set(a_up).astype(BF16)
    full = lambda shape: pl.BlockSpec(shape, lambda j: (0,) * len(shape))
    return pl.pallas_call(
        _gla_kernel,
        grid=(TP // ROW_TILE,),
        in_specs=[pl.BlockSpec((B, ROW_TILE, 896), lambda j: (0, j, 0)),
                  full((128, 128)), full((1, 128)), full((1, MIX_W))],
        out_specs=pl.BlockSpec((B, ROW_TILE, MIX_W), lambda j: (0, j, 0)),
        out_shape=jax.ShapeDtypeStruct((B, TP, MIX_W), BF16),
        scratch_shapes=[pltpu.VMEM((B, N_HEADS, HEAD_DIM, GLA_DK), F32)],
        compiler_params=_cparams("arbitrary"),
        name="gla",
    )(p_b.reshape(B, TP, 896), aup_pad, a_b.reshape(1, 128).astype(F32),
      jnp.tile(norm_g.astype(F32), N_HEADS).reshape(1, MIX_W))


def _mlstm_kernel(p_ref, cw_ref, cb_ref, ib_ref, fb_ref, ng_ref, y_ref, carry_ref, c_ref, n_ref, m_ref):
    j = pl.program_id(0)
    nb = p_ref.shape[0]
    n_chunks = ROW_TILE // CHUNK

    @pl.when(j == 0)
    def _():
        carry_ref[...] = jnp.zeros_like(carry_ref)
        c_ref[...] = jnp.zeros_like(c_ref)
        n_ref[...] = jnp.zeros_like(n_ref)
        m_ref[...] = jnp.zeros_like(m_ref)

    valid = (j * ROW_TILE + _iota((ROW_TILE, 1), 0)) >= FP
    tri = _tri_incl(CHUNK)
    tri_b = tri.astype(BF16)
    ones_h = _head_ones()

    og_all = []
    pre = {}
    for b in range(nb):
        p = jnp.where(valid, p_ref[b], 0.0)
        a = p[:, 0:512]
        ext = jnp.concatenate([carry_ref[b], a], axis=0)
        carry_ref[b] = a[ROW_TILE - 8:ROW_TILE, :]
        conv = cb_ref[...] + a * cw_ref[CONV_W - 1:CONV_W, :]
        for s in range(1, CONV_W):
            conv = conv + pltpu.roll(ext, s, 0)[8:8 + ROW_TILE, :] * cw_ref[CONV_W - 1 - s:CONV_W - s, :]
        qk = _silu(conv)
        q = jnp.where(valid, qk[:, 0:MIX_W], 0.0)
        k = jnp.where(valid, qk[:, MIX_W:2 * MIX_W], 0.0) * (HEAD_DIM ** -0.5)
        v = p[:, 512:768]
        og_all.append(p[:, 768:1024])
        gates = p[:, 1024:1152]
        li_all = jnp.where(valid, gates + ib_ref[...], NEG)
        lf_all = jnp.where(valid, _log_sigmoid(gates + fb_ref[...]), 0.0)
        for c in range(n_chunks):
            sl = slice(c * CHUNK, (c + 1) * CHUNK)
            pre[b, c] = dict(q=q[sl], k=k[sl], v=v[sl], li=li_all[sl], lf=lf_all[sl])

    bc = [(b, c) for b in range(nb) for c in range(n_chunks)]
    keys = [(b, c, h) for (b, c) in bc for h in range(N_HEADS)]
    heads = [slice(h * HEAD_DIM, (h + 1) * HEAD_DIM) for h in range(N_HEADS)]
    part = lambda name, u: pre[u[0], u[1]][name][:, heads[u[2]]]
    b_cum = {u: _dot_exact_lhs(tri_b, pre[u]["lf"]) for u in bc}
    b_t = {u: b_cum[u].T for u in bc}
    li_t = {u: pre[u]["li"].T for u in bc}
    b_col = {u: b_cum[u[0], u[1]][:, N_HEADS + u[2]:N_HEADS + u[2] + 1] for u in keys}
    b_last = {u: b_col[u][CHUNK - 1:CHUNK, :] for u in keys}
    d_log = {u: jnp.where(tri, b_col[u] - b_t[u[0], u[1]][N_HEADS + u[2]:N_HEADS + u[2] + 1, :]
                          + li_t[u[0], u[1]][u[2]:u[2] + 1, :], -jnp.inf) for u in keys}
    dmax = {u: jnp.max(d_log[u], axis=1, keepdims=True) for u in keys}
    qk = {u: _bdot_nt(part("q", u), part("k", u)) for u in keys}
    s0 = {u: jnp.exp(d_log[u] - dmax[u]) * qk[u] for u in keys}
    sv = {u: _bdot(s0[u], part("v", u)) for u in keys}
    ssum = {u: jnp.sum(s0[u], axis=1, keepdims=True) for u in keys}
    g_loc = {u: b_last[u] - b_col[u] + pre[u[0], u[1]]["li"][:, u[2]:u[2] + 1] for u in keys}
    m_loc = {u: jnp.max(g_loc[u], axis=0, keepdims=True) for u in keys}
    kw = {u: part("k", u) * jnp.exp(g_loc[u] - m_loc[u]) for u in keys}
    kwv = {u: _bdot_tn(kw[u], part("v", u)) for u in keys}
    kwsum = {u: jnp.sum(kw[u], axis=0, keepdims=True) for u in keys}

    bh = [(b, h) for b in range(nb) for h in range(N_HEADS)]
    c_st = {q: c_ref[q[0], q[1]] for q in bh}
    n_st = {q: n_ref[q[0], q[1]] for q in bh}
    m_st = {q: m_ref[q[0], q[1]] for q in bh}
    h_parts = {}
    for c in range(n_chunks):
        full = lambda q: (q[0], c, q[1])
        qc = {q: _bdot(part("q", full(q)), c_st[q]) for q in bh}
        qn = {q: jnp.sum(part("q", full(q)) * n_st[q], axis=1, keepdims=True) for q in bh}
        inter = {q: b_col[full(q)] + m_st[q] for q in bh}
        m_t = {q: jnp.maximum(inter[q], dmax[full(q)]) for q in bh}
        e_loc = {q: jnp.exp(dmax[full(q)] - m_t[q]) for q in bh}
        w_int = {q: jnp.exp(inter[q] - m_t[q]) for q in bh}
        for q in bh:
            num = e_loc[q] * sv[full(q)] + w_int[q] * qc[q]
            den = e_loc[q] * ssum[full(q)] + w_int[q] * qn[q]
            h_parts[full(q)] = num / jnp.maximum(jnp.abs(den), jnp.exp(-m_t[q]))
        m_new = {q: jnp.maximum(b_last[full(q)] + m_st[q], m_loc[full(q)]) for q in bh}
        s_old = {q: jnp.exp(b_last[full(q)] + m_st[q] - m_new[q]) for q in bh}
        s_new = {q: jnp.exp(m_loc[full(q)] - m_new[q]) for q in bh}
        c_st = {q: s_old[q] * c_st[q] + s_new[q] * kwv[full(q)] for q in bh}
        n_st = {q: s_old[q] * n_st[q] + s_new[q] * kwsum[full(q)] for q in bh}
        m_st = m_new
    for q in bh:
        c_ref[q[0], q[1]], n_ref[q[0], q[1]], m_ref[q[0], q[1]] = c_st[q], n_st[q], m_st[q]

    for b in range(nb):
        hh = jnp.concatenate([jnp.concatenate([h_parts[b, c, h] for h in range(N_HEADS)], axis=1)
                              for c in range(n_chunks)], axis=0) * _sigmoid(og_all[b])
        mean = _dot_exact_rhs(hh, ones_h) * (1.0 / HEAD_DIM)
        hc = hh - mean
        var = _dot_exact_rhs(hc * hc, ones_h) * (1.0 / HEAD_DIM)
        y_ref[b] = (hc * lax.rsqrt(var + 1e-5) * ng_ref[...]).astype(y_ref.dtype)


def _mlstm(p_d, B, TP, conv_w, conv_b, i_b, f_b, norm_g):
    ib = jnp.zeros((1, LANES), F32).at[0, 0:N_HEADS].set(i_b)
    fb = jnp.zeros((1, LANES), F32).at[0, N_HEADS:2 * N_HEADS].set(f_b)
    full = lambda shape: pl.BlockSpec(shape, lambda j: (0,) * len(shape))
    return pl.pallas_call(
        _mlstm_kernel,
        grid=(TP // ROW_TILE,),
        in_specs=[pl.BlockSpec((B, ROW_TILE, 1152), lambda j: (0, j, 0)),
                  full((CONV_W, 512)), full((1, 512)), full((1, LANES)), full((1, LANES)), full((1, MIX_W))],
        out_specs=pl.BlockSpec((B, ROW_TILE, MIX_W), lambda j: (0, j, 0)),
        out_shape=jax.ShapeDtypeStruct((B, TP, MIX_W), BF16),
        scratch_shapes=[pltpu.VMEM((B, 8, 512), F32),
                        pltpu.VMEM((B, N_HEADS, HEAD_DIM, HEAD_DIM), F32),
                        pltpu.VMEM((B, N_HEADS, 1, HEAD_DIM), F32),
                        pltpu.VMEM((B, N_HEADS, 1, 1), F32)],
        compiler_params=_cparams("arbitrary"),
        name="mlstm",
    )(p_d.reshape(B, TP, 1152), conv_w.astype(F32), conv_b.reshape(1, 512).astype(F32), ib, fb,
      norm_g.reshape(1, MIX_W).astype(F32))


V_ROWS = 80
WT_ROWS = 528


def _dsa_prep_kernel(h_ref, wt_ref, wn_ref, kvg_ref, wuk_ref, wuvt_ref,
                     qt_ref, qit_ref, wit_ref, k_ref, ki_ref, vt_ref):
    hb = h_ref[...]
    tm = hb.shape[0]
    pt = lax.dot_general(wt_ref[...], hb, _NT, preferred_element_type=F32)
    pn = jnp.dot(hb, wn_ref[...], preferred_element_type=F32)
    ckv = pn[:, 0:DSA_KV_RANK]
    c = ckv * lax.rsqrt(jnp.mean(ckv * ckv, -1, keepdims=True) + 1e-6) * kvg_ref[...]
    cb = c.astype(BF16)
    k_ref[...] = jnp.dot(cb, wuk_ref[...], preferred_element_type=F32).astype(BF16)
    ki_ref[...] = pn[:, DSA_KV_RANK:DSA_KV_RANK + IDX_DIM].astype(BF16)
    vt = lax.dot_general(wuvt_ref[...], cb, _NT, preferred_element_type=F32)
    vt = jnp.where(_iota((V_ROWS, tm), 0) == HEAD_DIM, 1.0, vt)
    for t in range(tm // LANES):
        cs = slice(t * LANES, (t + 1) * LANES)
        for h in range(N_HEADS):
            qt_ref[t, :, h * LANES:(h + 1) * LANES] = (
                pt[h * HEAD_DIM:(h + 1) * HEAD_DIM, cs] * (HEAD_DIM ** -0.5)).astype(BF16)
        for h in range(IDX_HEADS):
            qit_ref[t, :, h * LANES:(h + 1) * LANES] = pt[MIX_W + h * IDX_DIM:MIX_W + (h + 1) * IDX_DIM, cs].astype(BF16)
        wit_ref[t] = pt[2 * MIX_W:2 * MIX_W + IDX_HEADS, cs] * ((IDX_HEADS * IDX_DIM) ** -0.5)
        vt_ref[t] = vt[:, cs].astype(BF16)


def _dsa_prep(hb, w_t, w_n, kv_norm_g, w_uk, w_uv):
    N, D = hb.shape
    tm = _pick_tile(N, 640)
    nt = tm // LANES
    full = lambda shape: pl.BlockSpec(shape, lambda i: (0,) * len(shape))
    wuvt = jnp.pad(w_uv.T, ((0, V_ROWS - HEAD_DIM), (0, 0))).astype(BF16)
    return pl.pallas_call(
        _dsa_prep_kernel,
        grid=(N // tm,),
        in_specs=[pl.BlockSpec((tm, D), lambda i: (i, 0)),
                  full((WT_ROWS, D)), full((D, 256)), full((1, DSA_KV_RANK)),
                  full((DSA_KV_RANK, HEAD_DIM)), full((V_ROWS, DSA_KV_RANK))],
        out_specs=[pl.BlockSpec((nt, HEAD_DIM, N_HEADS * LANES), lambda i: (i, 0, 0)),
                   pl.BlockSpec((nt, IDX_DIM, IDX_HEADS * LANES), lambda i: (i, 0, 0)),
                   pl.BlockSpec((nt, IDX_HEADS, LANES), lambda i: (i, 0, 0)),
                   pl.BlockSpec((tm, HEAD_DIM), lambda i: (i, 0)),
                   pl.BlockSpec((tm, IDX_DIM), lambda i: (i, 0)),
                   pl.BlockSpec((nt, V_ROWS, LANES), lambda i: (i, 0, 0))],
        out_shape=[jax.ShapeDtypeStruct((N // LANES, HEAD_DIM, N_HEADS * LANES), BF16),
                   jax.ShapeDtypeStruct((N // LANES, IDX_DIM, IDX_HEADS * LANES), BF16),
                   jax.ShapeDtypeStruct((N // LANES, IDX_HEADS, LANES), F32),
                   jax.ShapeDtypeStruct((N, HEAD_DIM), BF16),
                   jax.ShapeDtypeStruct((N, IDX_DIM), BF16),
                   jax.ShapeDtypeStruct((N // LANES, V_ROWS, LANES), BF16)],
        compiler_params=_cparams("arbitrary"),
        name="dsa_prep",
    )(hb, w_t, w_n, kv_norm_g.reshape(1, DSA_KV_RANK).astype(F32), w_uk.astype(BF16), wuvt)


def _dsa_kernel(qt_ref, qit_ref, wit_ref, k_ref, ki_ref, vt_ref, bias_ref, y_ref,
                sk_ref, rel_ref, m_ref, acc_ref, lg_ref, mg_ref, *, topk):
    i = pl.program_id(1)
    nk = i + 1
    QT = ROW_TILE
    HQ = N_HEADS * QT
    t_lane = i * QT + _iota((LANES, QT), 1)
    key_pos = lambda kt: kt * LANES + _iota((LANES, QT), 0)
    per_head = lambda fn: jnp.concatenate([fn(slice(h * QT, (h + 1) * QT)) for h in range(N_HEADS)], axis=1)

    qit = qit_ref[0]
    wit = wit_ref[0]

    GW = rel_ref.shape[1] // LANES
    n_tiles = sk_ref.shape[0] - 2
    n_trips = (i + 2 * GW) // (2 * GW)

    def group_base(g):
        return jnp.clip(GW * g, 0, n_tiles - GW)

    def issue(g, slot):
        span = pl.ds(pl.multiple_of(group_base(g) * LANES, LANES), GW * LANES)
        rel_ref[slot] = jnp.dot(ki_ref[span, :], qit, preferred_element_type=F32)

    def reduce(g, slot):
        for u in range(GW):
            kt = group_base(g) + u
            rows_u = slice(u * LANES, (u + 1) * LANES)
            score = jnp.maximum(rel_ref[slot, rows_u, 0:QT], 0.0) * wit[0:1, :]
            for h in range(1, IDX_HEADS):
                score = score + jnp.maximum(rel_ref[slot, rows_u, h * QT:(h + 1) * QT], 0.0) * wit[h:h + 1, :]
            mine = (kt >= GW * g) & (kt <= i)
            sk_ref[jnp.where(mine, kt, n_tiles)] = score

    issue(0, 0)

    def score_body(jj, c):
        issue(2 * jj + 1, 1)
        reduce(2 * jj, 0)
        issue(2 * jj + 2, 0)
        reduce(2 * jj + 1, 1)
        return c

    lax.fori_loop(0, n_trips, score_body, 0)
    first = sk_ref[0]
    first = jnp.where(key_pos(0) < FP + N_META, jnp.inf, first)
    sk_ref[0] = jnp.where(key_pos(0) >= FP, first, -jnp.inf)
    sk_ref[i] = jnp.where(key_pos(i) <= t_lane, sk_ref[i], -jnp.inf)
    sk_ref[n_tiles + 1] = jnp.full((LANES, QT), -jnp.inf, F32)

    def key_to_float(key):
        return lax.bitcast_convert_type(jnp.where(key < 0, key ^ jnp.int32(0x7FFFFFFF), key), F32)
    def count(pred_fn):
        def body(kt, acc):
            return acc + jnp.where(pred_fn(sk_ref[kt], kt), 1, 0)

        def body4(j, acc):
            for u in range(4):
                acc = body(4 * j + u, acc)
            return acc

        n4 = lax.shift_right_logical(nk, 2)
        acc = lax.fori_loop(0, n4, body4, jnp.zeros((LANES, QT), jnp.int32))
        acc = lax.fori_loop(4 * n4, nk, body, acc)
        return jnp.sum(acc, axis=0, keepdims=True)

    def bit_body(it, carry):
        tau, n_ge = carry
        cand = tau + jnp.left_shift(jnp.int32(1), 31 - it)
        cand_f = key_to_float(cand)
        cnt = count(lambda sk, kt: sk >= cand_f)
        return jnp.where(cnt >= topk, cand, tau), jnp.where(cnt >= topk, cnt, n_ge)

    tau_key, n_ge = lax.fori_loop(0, 32, bit_body, (jnp.full((1, QT), INT_MIN, jnp.int32),
                                                    jnp.zeros((1, QT), jnp.int32)))
    tau = jnp.where(tau_key < KEY_LOWEST, jnp.float32(FLT_LOWEST), key_to_float(tau_key))

    @pl.when(jnp.max(n_ge - topk) > 0)
    def _():
        n_bits = max(1, int(math.ceil(math.log2(sk_ref.shape[0] * LANES + 1))))
        need = topk - count(lambda sk, kt: sk > tau)

        def pos_body(it, x):
            cand = x + jnp.left_shift(jnp.int32(1), n_bits - 1 - it)
            cnt = count(lambda sk, kt: (sk == tau) & (key_pos(kt) < cand))
            return jnp.where(cnt < need, cand, x)

        x = lax.fori_loop(0, n_bits, pos_body, jnp.zeros((1, QT), jnp.int32))
        jmax = jnp.where(n_ge > topk, x, jnp.int32(2 ** 30))

        def drop_body(kt, c):
            sk = sk_ref[kt]
            sk_ref[kt] = jnp.where((sk == tau) & (key_pos(kt) > jmax), -jnp.inf, sk)
            return c

        lax.fori_loop(0, nk, drop_body, 0)

    qt = qt_ref[0]
    m_ref[...] = jnp.full((1, HQ), NEG, F32)
    acc_ref[...] = jnp.zeros((V_ROWS, HQ), F32)

    def park(g, slot):
        base = group_base(g)
        span = pl.ds(pl.multiple_of(base * LANES, LANES), GW * LANES)
        lg_all = jnp.dot(k_ref[span, :], qt, preferred_element_type=F32)
        tmax = None
        for u in range(GW):
            t = base + u
            mine = (t >= GW * g) & (t <= i)
            lg = lg_all[u * LANES:(u + 1) * LANES, :] + bias_ref[jnp.clip(i - t, 0, 2)]
            sel = sk_ref[jnp.where(mine, t, n_tiles + 1)] >= tau
            lgm = per_head(lambda hs: jnp.where(sel, lg[:, hs], NEG))
            lg_ref[slot, u] = lgm
            tmax = lgm if tmax is None else jnp.maximum(tmax, lgm)
        mg_ref[slot] = jnp.max(tmax, axis=0, keepdims=True)

    def weights(slot):
        m_old = m_ref[...]
        m_new = jnp.maximum(m_old, mg_ref[slot])
        m_ref[...] = m_new
        return jnp.exp(m_old - m_new), [jnp.exp(lg_ref[slot, u] - m_new).astype(BF16) for u in range(GW)]

    def fold(g, corr, prs):
        vt_all = jnp.concatenate([vt_ref[group_base(g) + u] for u in range(GW)], axis=1)
        pv = jnp.dot(vt_all, jnp.concatenate(prs, axis=0), preferred_element_type=F32)
        acc_ref[...] = acc_ref[...] * corr + pv

    park(0, 0)

    def pipe_body(jj, c):
        corr, prs = weights(0)
        park(2 * jj + 1, 1)
        fold(2 * jj, corr, prs)
        corr, prs = weights(1)
        park(2 * jj + 2, 0)
        fold(2 * jj + 1, corr, prs)
        return c

    lax.fori_loop(0, n_trips, pipe_body, 0)
    acc = acc_ref[...]
    out = acc[0:HEAD_DIM, :] / jnp.maximum(acc[HEAD_DIM:HEAD_DIM + 1, :], 1e-30)
    y_ref[...] = per_head(lambda hs: out[:, hs].T).astype(y_ref.dtype)


def _t5_bucket(dist):
    max_exact = N_BUCKETS // 2
    n = jnp.maximum(dist, 0)
    large = max_exact + (jnp.log(jnp.maximum(n, 1).astype(F32) / max_exact)
                         / math.log(MAX_DISTANCE / max_exact) * (N_BUCKETS - max_exact)).astype(jnp.int32)
    return jnp.where(n < max_exact, n, jnp.minimum(large, N_BUCKETS - 1))


def _bias_tables(rel_bias):
    per_dist = rel_bias[_t5_bucket(jnp.arange(2 * ROW_TILE, dtype=jnp.int32))]
    q_minus_s = np.arange(ROW_TILE)[None, :] - np.arange(ROW_TILE)[:, None]
    far = per_dist[2 * ROW_TILE - 1]
    tabs = [per_dist[np.clip(r * ROW_TILE + q_minus_s, 0, 2 * ROW_TILE - 1)] - far for r in (0, 1)]
    tabs.append(jnp.zeros_like(tabs[0]))
    return jnp.stack(tabs).transpose(0, 1, 3, 2).reshape(3, ROW_TILE, N_HEADS * ROW_TILE).astype(F32)


def _dsa(qt, qit, wit, k, ki, vt, bias_tab, B, TP, topk):
    nq = TP // ROW_TILE
    return pl.pallas_call(
        functools.partial(_dsa_kernel, topk=topk),
        grid=(B, nq),
        in_specs=[pl.BlockSpec((1, HEAD_DIM, N_HEADS * LANES), lambda b, i: (b * nq + i, 0, 0)),
                  pl.BlockSpec((1, IDX_DIM, IDX_HEADS * LANES), lambda b, i: (b * nq + i, 0, 0)),
                  pl.BlockSpec((1, IDX_HEADS, LANES), lambda b, i: (b * nq + i, 0, 0)),
                  pl.BlockSpec((TP, HEAD_DIM), lambda b, i: (b, 0)),
                  pl.BlockSpec((TP, IDX_DIM), lambda b, i: (b, 0)),
                  pl.BlockSpec((nq, V_ROWS, LANES), lambda b, i: (b, 0, 0)),
                  pl.BlockSpec((3, ROW_TILE, N_HEADS * ROW_TILE), lambda b, i: (0, 0, 0))],
        out_specs=pl.BlockSpec((ROW_TILE, MIX_W), lambda b, i: (b * nq + i, 0)),
        out_shape=jax.ShapeDtypeStruct((B * TP, MIX_W), BF16),
        scratch_shapes=[pltpu.VMEM((nq + 2, LANES, ROW_TILE), F32),
                        pltpu.VMEM((2, min(4, nq) * LANES, IDX_HEADS * ROW_TILE), F32),
                        pltpu.VMEM((1, N_HEADS * ROW_TILE), F32),
                        pltpu.VMEM((V_ROWS, N_HEADS * ROW_TILE), F32),
                        pltpu.VMEM((2, min(4, nq), LANES, N_HEADS * ROW_TILE), F32),
                        pltpu.VMEM((2, 1, N_HEADS * ROW_TILE), F32)],
        compiler_params=_cparams("parallel", "arbitrary"),
        name="dsa_attend",
    )(qt, qit, wit, k, ki, vt, bias_tab)


def _layer_norm_rows(z, g, b):
    mu = jnp.mean(z, -1, keepdims=True)
    zc = z - mu
    var = jnp.mean(zc * zc, -1, keepdims=True)
    return zc * lax.rsqrt(var + LN_EPS) * g + b


def _merge_kernel(h_ref, hb_ref, wg_ref, ya_ref, yb_ref, yc_ref, yd_ref, wb_ref, wo_ref, lg_ref, lb_ref,
                  h1_ref, h1b_ref):
    hb = hb_ref[...]
    merged = None
    for i, y_ref in enumerate((ya_ref, yb_ref, yc_ref, yd_ref)):
        gate = _sigmoid(jnp.dot(hb, wg_ref[:, i * D_MODEL:(i + 1) * D_MODEL], preferred_element_type=F32))
        t = gate * jnp.dot(y_ref[...], wb_ref[i], preferred_element_type=F32)
        merged = t if merged is None else merged + t
    z = DN_ALPHA * h_ref[...] + jnp.dot(merged.astype(BF16), wo_ref[...], preferred_element_type=F32)
    y = _layer_norm_rows(z, lg_ref[...], lb_ref[...])
    h1_ref[...] = y
    h1b_ref[...] = y.astype(BF16)


def _merge(h, hb, w_g, ys, w_branch, w_out, ln_g, ln_b):
    N, D = h.shape
    tm = _pick_tile(N, 640)
    full = lambda shape: pl.BlockSpec(shape, lambda i: (0,) * len(shape))
    tok = lambda w: pl.BlockSpec((tm, w), lambda i: (i, 0))
    return pl.pallas_call(
        _merge_kernel,
        grid=(N // tm,),
        in_specs=[tok(D), tok(D), full((D, 4 * D)), tok(MIX_W), tok(MIX_W), tok(MIX_W), tok(MIX_W),
                  full((4, MIX_W, D)), full((D, D)), full((1, D)), full((1, D))],
        out_specs=[tok(D), tok(D)],
        out_shape=[jax.ShapeDtypeStruct((N, D), F32), jax.ShapeDtypeStruct((N, D), BF16)],
        compiler_params=_cparams("arbitrary"),
        name="merge_out_ln",
    )(h, hb, w_g, *ys, w_branch.astype(BF16), w_out.astype(BF16),
      ln_g.reshape(1, D).astype(F32), ln_b.reshape(1, D).astype(F32))


def _moe_kernel(h_ref, hb_ref, wr_ref, br_ref, wg_ref, wu_ref, wd_ref, lg_ref, lb_ref, o_ref, ob_ref,
                gate_ref, acc_ref):
    e = pl.program_id(1)
    xb = hb_ref[...]
    tm = xb.shape[0]
    lane = _iota((tm, LANES), 1)

    @pl.when(e == 0)
    def _():
        logit = jnp.dot(xb, wr_ref[...], preferred_element_type=F32) + br_ref[...]
        big = jnp.int32(LANES)
        gl = jnp.where(lane < N_GROUPS, logit, -jnp.inf)
        gmax = jnp.max(gl, axis=1, keepdims=True)
        g_sel = jnp.min(jnp.where(gl == gmax, lane, big), axis=1, keepdims=True)
        p_grp = 1.0 / jnp.sum(jnp.exp(gl - gmax), axis=1, keepdims=True)
        lo = N_GROUPS + g_sel * EPG
        el = jnp.where((lane >= lo) & (lane < lo + EPG), logit, -jnp.inf)
        v1 = jnp.max(el, axis=1, keepdims=True)
        i1 = jnp.min(jnp.where(el == v1, lane, big), axis=1, keepdims=True)
        el2 = jnp.where(lane == i1, -jnp.inf, el)
        v2 = jnp.max(el2, axis=1, keepdims=True)
        i2 = jnp.min(jnp.where(el2 == v2, lane, big), axis=1, keepdims=True)
        e2 = jnp.exp(v2 - v1)
        w1 = p_grp / (1.0 + e2)
        w2 = p_grp * e2 / (1.0 + e2)
        gate_ref[...] = jnp.where(lane == i1, w1, 0.0) + jnp.where(lane == i2, w2, 0.0)
        acc_ref[...] = jnp.zeros_like(acc_ref)

    hid = _silu(jnp.dot(xb, wg_ref[0], preferred_element_type=F32)) * jnp.dot(xb, wu_ref[0], preferred_element_type=F32)
    gates = gate_ref[...]
    scaled = []
    for j in range(EPG):
        g_j = jnp.sum(jnp.where(lane == e * EPG + j + N_GROUPS, gates, 0.0), axis=1, keepdims=True)
        scaled.append((hid[:, j * D_EXPERT:(j + 1) * D_EXPERT] * g_j).astype(BF16))
    acc_ref[...] += jnp.dot(jnp.concatenate(scaled, axis=1), wd_ref[0], preferred_element_type=F32)

    @pl.when(e == N_GROUPS - 1)
    def _():
        y = _layer_norm_rows(DN_ALPHA * h_ref[...] + acc_ref[...], lg_ref[...], lb_ref[...])
        o_ref[...] = y
        ob_ref[...] = y.astype(BF16)


def _moe(h1, h1b, w_grp, b_grp, w_rt, b_rt, w_gate, w_up, w_down, ln_g, ln_b):
    N, D = h1.shape
    tm = _pick_tile(N, 640)
    w_r = jnp.zeros((D, LANES), F32).at[:, 0:N_GROUPS].set(w_grp).at[:, N_GROUPS:N_GROUPS + N_EXPERTS].set(w_rt)
    b_r = jnp.zeros((1, LANES), F32).at[0, 0:N_GROUPS].set(b_grp).at[0, N_GROUPS:N_GROUPS + N_EXPERTS].set(b_rt)
    GH = EPG * D_EXPERT
    by_group = lambda w: w.reshape(N_GROUPS, EPG, D, D_EXPERT).transpose(0, 2, 1, 3).reshape(N_GROUPS, D, GH)
    w_gate, w_up, w_down = by_group(w_gate), by_group(w_up), w_down.reshape(N_GROUPS, GH, D)
    full = lambda shape: pl.BlockSpec(shape, lambda i, e: (0,) * len(shape))
    tok = lambda w: pl.BlockSpec((tm, w), lambda i, e: (i, 0))
    return pl.pallas_call(
        _moe_kernel,
        grid=(N // tm, N_GROUPS),
        in_specs=[tok(D), tok(D), full((D, LANES)), full((1, LANES)),
                  pl.BlockSpec((1, D, GH), lambda i, e: (e, 0, 0)),
                  pl.BlockSpec((1, D, GH), lambda i, e: (e, 0, 0)),
                  pl.BlockSpec((1, GH, D), lambda i, e: (e, 0, 0)),
                  full((1, D)), full((1, D))],
        out_specs=[tok(D), tok(D)],
        out_shape=[jax.ShapeDtypeStruct((N, D), F32), jax.ShapeDtypeStruct((N, D), BF16)],
        scratch_shapes=[pltpu.VMEM((tm, LANES), F32), pltpu.VMEM((tm, D), F32)],
        compiler_params=_cparams("arbitrary", "arbitrary"),
        name="hier_moe_ln",
    )(h1, h1b, w_r.astype(BF16), b_r, w_gate.astype(BF16), w_up.astype(BF16), w_down.astype(BF16),
      ln_g.reshape(1, D).astype(F32), ln_b.reshape(1, D).astype(F32))


def _pad_cols(w, width):
    return jnp.pad(w, ((0, 0), (0, width - w.shape[1])))


def _split_w_in(w):
    o = 0
    w_a = w[:, o:o + 1024]; o += 1024
    gq, gk, gv, ga, gg = (w[:, o:o + 128], w[:, o + 128:o + 256], w[:, o + 256:o + 512],
                          w[:, o + 512:o + 528], w[:, o + 528:o + 784]); o += 784
    w_b = _pad_cols(jnp.concatenate([gq, gk, gv, gg, ga], axis=1), 896)
    cq, ckv, cqi, cki, cwi = (w[:, o:o + 256], w[:, o + 256:o + 384], w[:, o + 384:o + 640],
                              w[:, o + 640:o + 672], w[:, o + 672:o + 680]); o += 680
    w_t = jnp.pad(jnp.concatenate([cq.T, cqi.T, cwi.T], axis=0), ((0, WT_ROWS - 2 * MIX_W - IDX_HEADS), (0, 0)))
    w_n = _pad_cols(jnp.concatenate([ckv, cki], axis=1), 256)
    dq, dk, dv, di, df, do = (w[:, o:o + 256], w[:, o + 256:o + 512], w[:, o + 512:o + 768],
                              w[:, o + 768:o + 772], w[:, o + 772:o + 776], w[:, o + 776:o + 1032]); o += 1032
    w_d = _pad_cols(jnp.concatenate([dq, dk, dv, do, di, df], axis=1), 1152)
    w_g = w[:, o:o + 4096]
    bf = lambda a: a.astype(BF16)
    return bf(w_a), bf(w_b), bf(w_t), bf(w_n), bf(w_d), bf(w_g)


def kernel(x, meta, ln_in_g, ln_in_b, rel_bias, w_in, rwkv_mu, rwkv_w_up, rwkv_w0, rwkv_a_up, rwkv_a0, rwkv_g_up, rwkv_k_k, rwkv_k_a, rwkv_r_k, rwkv_gn_g, rwkv_gn_b, gla_a_up, gla_a_b, gla_norm_g, dsa_kv_norm_g, dsa_w_uk, dsa_w_uv, mlstm_conv_w, mlstm_conv_b, mlstm_i_b, mlstm_f_b, mlstm_norm_g, w_branch, w_out, ln1_g, ln1_b, moe_w_grp, moe_b_grp, moe_w_rt, moe_b_rt, moe_w_gate, moe_w_up, moe_w_down, ln2_g, ln2_b):
    B, S, D = x.shape
    assert D == D_MODEL and S % ROW_TILE == 0
    TP = S + FRONT
    N = B * TP
    topk = min(TOPK_MAX, S // 4)
    bias_tab = _bias_tables(rel_bias)

    h, hb = _embed(x, meta, ln_in_g, ln_in_b)
    h = h.reshape(N, D)
    hb = hb.reshape(N, D)
    for l in range(DEPTH):
        w_a, w_b, w_t, w_n, w_d, w_g = _split_w_in(w_in[l])
        p_a = _proj(hb, w_a)
        p_b = _proj(hb, w_b)
        p_d = _proj(hb, w_d)
        qt, qit, wit, k, ki, vt = _dsa_prep(hb, w_t, w_n, dsa_kv_norm_g[l], dsa_w_uk[l], dsa_w_uv[l])
        y_a = _rwkv(p_a, B, TP, rwkv_mu[l], rwkv_w_up[l], rwkv_w0[l], rwkv_a_up[l], rwkv_a0[l], rwkv_g_up[l],
                    rwkv_k_k[l], rwkv_k_a[l], rwkv_r_k[l], rwkv_gn_g[l], rwkv_gn_b[l])
        y_b = _gla(p_b, B, TP, gla_a_up[l], gla_a_b[l], gla_norm_g[l])
        y_c = _dsa(qt, qit, wit, k, ki, vt, bias_tab, B, TP, topk)
        y_d = _mlstm(p_d, B, TP, mlstm_conv_w[l], mlstm_conv_b[l], mlstm_i_b[l], mlstm_f_b[l], mlstm_norm_g[l])
        ys = (y_a.reshape(N, MIX_W), y_b.reshape(N, MIX_W), y_c, y_d.reshape(N, MIX_W))
        h1, h1b = _merge(h, hb, w_g, ys, w_branch[l], w_out[l], ln1_g[l], ln1_b[l])
        h, hb = _moe(h1, h1b, moe_w_grp[l], moe_b_grp[l], moe_w_rt[l], moe_b_rt[l],
                     moe_w_gate[l], moe_w_up[l], moe_w_down[l], ln2_g[l], ln2_b[l])
    return h.reshape(B, TP, D)[:, FRONT:]
```

```python
import functools
import math

import numpy as np
import jax
import jax.numpy as jnp
from jax import lax
from jax.experimental import pallas as pl
from jax.experimental.pallas import tpu as pltpu

F32 = jnp.float32
BF16 = jnp.bfloat16

D_MODEL = 1024
HEAD_DIM = 64
N_HEADS = 4
MIX_W = 256
N_META = 16
CHUNK = 64
LANES = 128
ROW_TILE = 128
FRONT = ROW_TILE
FP = FRONT - N_META
NEG = -1e30
LN_EPS = 1e-5
DEPTH = 2
DN_ALPHA = (2 * DEPTH) ** 0.25

RWKV_GN_EPS = HEAD_DIM * 1e-5
GLA_DK = 32
GLA_TAU = 16.0
DSA_KV_RANK = 128
IDX_HEADS = 8
IDX_DIM = 32
TOPK_MAX = 256
N_BUCKETS = 32
MAX_DISTANCE = 128
CONV_W = 4
N_GROUPS = 4
EPG = 4
N_EXPERTS = 16
D_EXPERT = 256

INT_MIN = -(2 ** 31)
FLT_LOWEST = float(np.finfo(np.float32).min)
KEY_LOWEST = -(2 ** 31) + 0x00800000
VMEM_LIMIT = 56 * 1024 * 1024


def _cparams(*sem):
    return pltpu.CompilerParams(dimension_semantics=tuple(sem), vmem_limit_bytes=VMEM_LIMIT)


def _pick_tile(n, target):
    best = LANES
    t = LANES
    while t <= min(n, target):
        if n % t == 0:
            best = t
        t += LANES
    return best


def _bdot(a, b):
    return jnp.dot(a.astype(BF16), b.astype(BF16), preferred_element_type=F32)


def _bdot_nt(a, b):
    return lax.dot_general(a.astype(BF16), b.astype(BF16), (((1,), (1,)), ((), ())),
                           preferred_element_type=F32)


def _bdot_tn(a, b):
    return lax.dot_general(a.astype(BF16), b.astype(BF16), (((0,), (0,)), ((), ())),
                           preferred_element_type=F32)


def _split(a):
    hi = a.astype(BF16)
    lo = (a - hi.astype(F32)).astype(BF16)
    return hi, lo


_NN = (((1,), (0,)), ((), ()))
_NT = (((1,), (1,)), ((), ()))


def _dot3(a, b, dims=_NN):
    ah, al = _split(a)
    bh, bl = _split(b)
    dg = lambda x, y: lax.dot_general(x, y, dims, preferred_element_type=F32)
    return dg(ah, bh) + (dg(ah, bl) + dg(al, bh))


def _dot_exact_lhs(a_bf16, b):
    bh, bl = _split(b)
    return (jnp.dot(a_bf16, bh, preferred_element_type=F32)
            + jnp.dot(a_bf16, bl, preferred_element_type=F32))


def _dot_exact_rhs(a, b_bf16):
    ah, al = _split(a)
    return (jnp.dot(ah, b_bf16, preferred_element_type=F32)
            + jnp.dot(al, b_bf16, preferred_element_type=F32))


def _sigmoid(x):
    return 1.0 / (1.0 + jnp.exp(-x))


def _log_sigmoid(x):
    return jnp.minimum(x, 0.0) - jnp.log(1.0 + jnp.exp(-jnp.abs(x)))


def _silu(x):
    return x * _sigmoid(x)


def _iota(shape, dim):
    return lax.broadcasted_iota(jnp.int32, shape, dim)


def _tri_incl(n):
    return (_iota((n, n), 1) <= _iota((n, n), 0))


def _head_ones():
    return ((_iota((MIX_W, MIX_W), 0) // HEAD_DIM) == (_iota((MIX_W, MIX_W), 1) // HEAD_DIM)).astype(BF16)


def _embed_kernel(x_ref, meta_ref, g_ref, b_ref, h_ref, hb_ref):
    j = pl.program_id(0)
    for bi in range(x_ref.shape[0]):
        src = jnp.where(j == 0, meta_ref[...], x_ref[bi])
        mu = jnp.mean(src, -1, keepdims=True)
        xc = src - mu
        var = jnp.mean(xc * xc, -1, keepdims=True)
        y = xc * lax.rsqrt(var + LN_EPS) * g_ref[...] + b_ref[...]
        h_ref[bi] = y
        hb_ref[bi] = y.astype(BF16)


def _embed(x, meta, g, b):
    B, S, D = x.shape
    TP = S + FRONT
    meta_pad = jnp.concatenate([jnp.zeros((FP, D), F32), meta.astype(F32)], axis=0)
    return pl.pallas_call(
        _embed_kernel,
        grid=(TP // ROW_TILE,),
        in_specs=[
            pl.BlockSpec((B, ROW_TILE, D), lambda j: (0, jnp.maximum(j - 1, 0), 0)),
            pl.BlockSpec((ROW_TILE, D), lambda j: (0, 0)),
            pl.BlockSpec((1, D), lambda j: (0, 0)),
            pl.BlockSpec((1, D), lambda j: (0, 0)),
        ],
        out_specs=[
            pl.BlockSpec((B, ROW_TILE, D), lambda j: (0, j, 0)),
            pl.BlockSpec((B, ROW_TILE, D), lambda j: (0, j, 0)),
        ],
        out_shape=[jax.ShapeDtypeStruct((B, TP, D), F32), jax.ShapeDtypeStruct((B, TP, D), BF16)],
        compiler_params=_cparams("arbitrary"),
        name="embed_ln",
    )(x, meta_pad, g.reshape(1, D), b.reshape(1, D))


def _proj_kernel(h_ref, *refs):
    n = len(refs) // 2
    hb = h_ref[...]
    for w_ref, o_ref in zip(refs[:n], refs[n:]):
        o_ref[...] = jnp.dot(hb, w_ref[...], preferred_element_type=F32)


def _proj(hb, ws):
    N, D = hb.shape
    tm = _pick_tile(N, 640)
    return pl.pallas_call(
        _proj_kernel,
        grid=(N // tm,),
        in_specs=[pl.BlockSpec((tm, D), lambda i: (i, 0))]
                 + [pl.BlockSpec((D, w.shape[1]), lambda i: (0, 0)) for w in ws],
        out_specs=[pl.BlockSpec((tm, w.shape[1]), lambda i: (i, 0)) for w in ws],
        out_shape=[jax.ShapeDtypeStruct((N, w.shape[1]), F32) for w in ws],
        compiler_params=_cparams("arbitrary"),
        name="in_proj",
    )(hb, *ws)


def _rwkv_kernel(p_ref, mu_ref, wup_ref, w0_ref, aup_ref, a0_ref, gup_ref, kk_ref, ka_ref, rk_ref,
                 gng_ref, gnb_ref, y_ref, carry_ref, s_ref):
    j = pl.program_id(0)
    nb = p_ref.shape[0]
    n_chunks = ROW_TILE // CHUNK

    @pl.when(j == 0)
    def _():
        carry_ref[...] = jnp.zeros_like(carry_ref)
        s_ref[...] = jnp.zeros_like(s_ref)

    valid = (j * ROW_TILE + _iota((ROW_TILE, 1), 0)) >= FP
    first_row = _iota((ROW_TILE, 1), 0) == 0
    ones_h = _head_ones()
    tri = _tri_incl(CHUNK)
    tri_b = tri.astype(BF16)
    strict = _iota((CHUNK, CHUNK), 1) < _iota((CHUNK, CHUNK), 0)
    eye = (_iota((CHUNK, CHUNK), 1) == _iota((CHUNK, CHUNK), 0)).astype(F32)
    heads = [slice(h * HEAD_DIM, (h + 1) * HEAD_DIM) for h in range(N_HEADS)]

    pro = []
    unit = {}
    for b in range(nb):
        p = jnp.where(valid, p_ref[b], 0.0)
        prev = jnp.where(first_row, carry_ref[b], pltpu.roll(p, 1, 0))
        carry_ref[b] = p[ROW_TILE - 1:ROW_TILE, :]
        ps = p + (prev - p) * mu_ref[...]
        r = ps[:, 0:256]
        k = ps[:, 256:512]
        v = ps[:, 512:768]
        lora_in = ps[:, 768:896]
        xg = ps[:, 896:1024]
        w_log = _log_sigmoid(w0_ref[...] + _bdot(jnp.tanh(lora_in), wup_ref[...])) - 0.5
        lw = jnp.where(valid, -jnp.exp(w_log), 0.0)
        alpha = _sigmoid(a0_ref[...] + _bdot(lora_in, aup_ref[...]))
        gate = _bdot(_sigmoid(xg), gup_ref[...])
        kk = k * kk_ref[...]
        kk = kk / jnp.maximum(jnp.sqrt(_dot_exact_rhs(kk * kk, ones_h)), 1e-12)
        k = k * (1.0 + (alpha - 1.0) * ka_ref[...])
        kka = kk * alpha
        pro.append((r, k, v, gate))
        for c in range(n_chunks):
            sl = slice(c * CHUNK, (c + 1) * CHUNK)
            lw_c = lw[sl]
            cum = _dot_exact_lhs(tri_b, lw_c)
            cum_last = cum[CHUNK - 1:CHUNK, :]
            p_inv = jnp.exp(-cum)
            p_tail = jnp.exp(cum_last - cum)
            unit[b, c] = dict(a=-kk[sl] * jnp.exp(cum - lw_c), b=kka[sl] * p_inv, k=k[sl] * p_inv,
                              r=r[sl] * jnp.exp(cum), kb=k[sl] * p_tail, bb=kka[sl] * p_tail,
                              pl=jnp.exp(cum_last), v=v[sl])

    keys = [(b, c, h) for b in range(nb) for c in range(n_chunks) for h in range(N_HEADS)]
    part = lambda name, key: unit[key[0], key[1]][name][:, heads[key[2]]]
    a_ab = {q: jnp.where(strict, _dot3(part("a", q), part("b", q), _NT), 0.0) for q in keys}
    a_ak = {q: jnp.where(strict, _bdot_nt(part("a", q), part("k", q)), 0.0) for q in keys}
    a_rb = {q: jnp.where(tri, _bdot_nt(part("r", q), part("b", q)), 0.0) for q in keys}
    a_rk = {q: jnp.where(tri, _bdot_nt(part("r", q), part("k", q)), 0.0) for q in keys}
    inv = {q: eye + a_ab[q] for q in keys}
    pw = a_ab
    for _ in range(5):
        pw = {q: _bdot(pw[q], pw[q]) for q in keys}
        inv = {q: inv[q] + _bdot(inv[q], pw[q]) for q in keys}
    ak_v = {q: _bdot(a_ak[q], part("v", q)) for q in keys}
    rk_v = {q: _bdot(a_rk[q], part("v", q)) for q in keys}
    kb_v = {q: _bdot_tn(part("v", q), part("kb", q)) for q in keys}

    bh = [(b, h) for b in range(nb) for h in range(N_HEADS)]
    state = {q: s_ref[q[0], q[1]] for q in bh}
    y_parts = {}
    for c in range(n_chunks):
        full = lambda q: (q[0], c, q[1])
        a_s = {q: _bdot_nt(part("a", full(q)), state[q]) for q in bh}
        r_s = {q: _bdot_nt(part("r", full(q)), state[q]) for q in bh}
        u = {q: _bdot(inv[full(q)], a_s[q] + ak_v[full(q)]) for q in bh}
        for q in bh:
            y_parts[full(q)] = r_s[q] + rk_v[full(q)] + _bdot(a_rb[full(q)], u[q])
        state = {q: (state[q] * part("pl", full(q)) + kb_v[full(q)] + _bdot_tn(u[q], part("bb", full(q))))
                 for q in bh}
    for q in bh:
        s_ref[q[0], q[1]] = state[q]

    for b in range(nb):
        r, k, v, gate = pro[b]
        y = jnp.concatenate([jnp.concatenate([y_parts[b, c, h] for h in range(N_HEADS)], axis=1)
                             for c in range(n_chunks)], axis=0)
        mean = _dot_exact_rhs(y, ones_h) * (1.0 / HEAD_DIM)
        yc = y - mean
        var = _dot_exact_rhs(yc * yc, ones_h) * (1.0 / HEAD_DIM)
        yn = yc * lax.rsqrt(var + RWKV_GN_EPS) * gng_ref[...] + gnb_ref[...]
        bonus = _dot_exact_rhs(r * k * rk_ref[...], ones_h) * v
        y_ref[b] = ((yn + bonus) * gate).astype(y_ref.dtype)


def _rwkv(p_a, B, TP, mu, w_up, w0, a_up, a0, g_up, k_k, k_a, r_k, gn_g, gn_b):
    W = MIX_W
    z64 = jnp.zeros((64, W), F32)
    wup_pad = jnp.concatenate([w_up, z64], axis=0).astype(BF16)
    aup_pad = jnp.concatenate([z64, a_up], axis=0).astype(BF16)
    row = lambda a: a.reshape(1, -1).astype(F32)
    full = lambda shape: pl.BlockSpec(shape, lambda j: (0,) * len(shape))
    return pl.pallas_call(
        _rwkv_kernel,
        grid=(TP // ROW_TILE,),
        in_specs=[pl.BlockSpec((B, ROW_TILE, 1024), lambda j: (0, j, 0)),
                  full((1, 1024)), full((128, W)), full((1, W)), full((128, W)), full((1, W)),
                  full((128, W)), full((1, W)), full((1, W)), full((1, W)), full((1, W)), full((1, W))],
        out_specs=pl.BlockSpec((B, ROW_TILE, W), lambda j: (0, j, 0)),
        out_shape=jax.ShapeDtypeStruct((B, TP, W), BF16),
        scratch_shapes=[pltpu.VMEM((B, 1, 1024), F32), pltpu.VMEM((B, N_HEADS, HEAD_DIM, HEAD_DIM), F32)],
        compiler_params=_cparams("arbitrary"),
        name="rwkv7",
    )(p_a.reshape(B, TP, 1024), row(mu), wup_pad, row(w0), aup_pad, row(a0), g_up.astype(BF16),
      row(k_k), row(k_a), row(r_k), row(gn_g), row(gn_b))


def _gla_kernel(p_ref, aup_ref, ab_ref, ng_ref, y_ref, s_ref):
    j = pl.program_id(0)
    nb = p_ref.shape[0]
    n_chunks = ROW_TILE // CHUNK

    @pl.when(j == 0)
    def _():
        s_ref[...] = jnp.zeros_like(s_ref)

    valid = (j * ROW_TILE + _iota((ROW_TILE, 1), 0)) >= FP
    tri = _tri_incl(CHUNK)
    tri_b = tri.astype(BF16)

    og_all = []
    pre = {}
    for b in range(nb):
        p = jnp.where(valid, p_ref[b], 0.0)
        la = _log_sigmoid(_bdot(p[:, 768:896], aup_ref[...]) + ab_ref[...]) * (1.0 / GLA_TAU)
        la = jnp.where(valid, la, 0.0)
        og_all.append(p[:, 512:768])
        for c in range(n_chunks):
            sl = slice(c * CHUNK, (c + 1) * CHUNK)
            pre[b, c] = dict(q=p[sl, 0:128] * (GLA_DK ** -0.5), k=p[sl, 128:256], v=p[sl, 256:512], la=la[sl])
    bc = [(b, c) for b in range(nb) for c in range(n_chunks)]
    keys = [(b, c, h) for (b, c) in bc for h in range(N_HEADS)]
    ks = [slice(h * GLA_DK, (h + 1) * GLA_DK) for h in range(N_HEADS)]
    vs = [slice(h * HEAD_DIM, (h + 1) * HEAD_DIM) for h in range(N_HEADS)]
    b_cum = {u: _dot_exact_lhs(tri_b, pre[u]["la"]) for u in bc}
    b_last = {u: b_cum[u][CHUNK - 1:CHUNK, :] for u in bc}
    q_g = {u: pre[u]["q"] * jnp.exp(b_cum[u]) for u in bc}
    k_g = {u: pre[u]["k"] * jnp.exp(-b_cum[u]) for u in bc}
    k_l = {u: pre[u]["k"] * jnp.exp(b_last[u] - b_cum[u]) for u in bc}
    dec = {u: jnp.exp(b_last[u]) for u in bc}
    att = {u: jnp.where(tri, _bdot_nt(q_g[u[0], u[1]][:, ks[u[2]]], k_g[u[0], u[1]][:, ks[u[2]]]), 0.0)
           for u in keys}
    att_v = {u: _bdot(att[u], pre[u[0], u[1]]["v"][:, vs[u[2]]]) for u in keys}
    kl_v = {u: _bdot_tn(pre[u[0], u[1]]["v"][:, vs[u[2]]], k_l[u[0], u[1]][:, ks[u[2]]]) for u in keys}

    bh = [(b, h) for b in range(nb) for h in range(N_HEADS)]
    state = {q: s_ref[q[0], q[1]] for q in bh}
    o_parts = {}
    for c in range(n_chunks):
        for q in bh:
            o_parts[q[0], c, q[1]] = att_v[q[0], c, q[1]] + _bdot_nt(q_g[q[0], c][:, ks[q[1]]], state[q])
        state = {q: state[q] * dec[q[0], c][:, ks[q[1]]] + kl_v[q[0], c, q[1]] for q in bh}
    for q in bh:
        s_ref[q[0], q[1]] = state[q]

    ones_h = _head_ones()
    for b in range(nb):
        o = jnp.concatenate([jnp.concatenate([o_parts[b, c, h] for h in range(N_HEADS)], axis=1)
                             for c in range(n_chunks)], axis=0)
        ms = _dot_exact_rhs(o * o, ones_h) * (1.0 / HEAD_DIM)
        y = o * lax.rsqrt(ms + 1e-6) * ng_ref[...] * _silu(og_all[b])
        y_ref[b] = y.astype(y_ref.dtype)


def _gla(p_b, B, TP, a_up, a_b, norm_g):
    aup_pad = jnp.zeros((128, 128), F32).at[:a_up.shape[0]].set(a_up).astype(BF16)
    full = lambda shape: pl.BlockSpec(shape, lambda j: (0,) * len(shape))
    return pl.pallas_call(
        _gla_kernel,
        grid=(TP // ROW_TILE,),
        in_specs=[pl.BlockSpec((B, ROW_TILE, 896), lambda j: (0, j, 0)),
                  full((128, 128)), full((1, 128)), full((1, MIX_W))],
        out_specs=pl.BlockSpec((B, ROW_TILE, MIX_W), lambda j: (0, j, 0)),
        out_shape=jax.ShapeDtypeStruct((B, TP, MIX_W), BF16),
        scratch_shapes=[pltpu.VMEM((B, N_HEADS, HEAD_DIM, GLA_DK), F32)],
        compiler_params=_cparams("arbitrary"),
        name="gla",
    )(p_b.reshape(B, TP, 896), aup_pad, a_b.reshape(1, 128).astype(F32),
      jnp.tile(norm_g.astype(F32), N_HEADS).reshape(1, MIX_W))


def _mlstm_kernel(p_ref, cw_ref, cb_ref, ib_ref, fb_ref, ng_ref, y_ref, carry_ref, c_ref, n_ref, m_ref):
    j = pl.program_id(0)
    nb = p_ref.shape[0]
    n_chunks = ROW_TILE // CHUNK

    @pl.when(j == 0)
    def _():
        carry_ref[...] = jnp.zeros_like(carry_ref)
        c_ref[...] = jnp.zeros_like(c_ref)
        n_ref[...] = jnp.zeros_like(n_ref)
        m_ref[...] = jnp.zeros_like(m_ref)

    valid = (j * ROW_TILE + _iota((ROW_TILE, 1), 0)) >= FP
    tri = _tri_incl(CHUNK)
    tri_b = tri.astype(BF16)
    ones_h = _head_ones()

    og_all = []
    pre = {}
    for b in range(nb):
        p = jnp.where(valid, p_ref[b], 0.0)
        a = p[:, 0:512]
        ext = jnp.concatenate([carry_ref[b], a], axis=0)
        carry_ref[b] = a[ROW_TILE - 8:ROW_TILE, :]
        conv = cb_ref[...] + a * cw_ref[CONV_W - 1:CONV_W, :]
        for s in range(1, CONV_W):
            conv = conv + pltpu.roll(ext, s, 0)[8:8 + ROW_TILE, :] * cw_ref[CONV_W - 1 - s:CONV_W - s, :]
        qk = _silu(conv)
        q = jnp.where(valid, qk[:, 0:MIX_W], 0.0)
        k = jnp.where(valid, qk[:, MIX_W:2 * MIX_W], 0.0) * (HEAD_DIM ** -0.5)
        v = p[:, 512:768]
        og_all.append(p[:, 768:1024])
        gates = p[:, 1024:1152]
        li_all = jnp.where(valid, gates + ib_ref[...], NEG)
        lf_all = jnp.where(valid, _log_sigmoid(gates + fb_ref[...]), 0.0)
        for c in range(n_chunks):
            sl = slice(c * CHUNK, (c + 1) * CHUNK)
            pre[b, c] = dict(q=q[sl], k=k[sl], v=v[sl], li=li_all[sl], lf=lf_all[sl])

    bc = [(b, c) for b in range(nb) for c in range(n_chunks)]
    keys = [(b, c, h) for (b, c) in bc for h in range(N_HEADS)]
    heads = [slice(h * HEAD_DIM, (h + 1) * HEAD_DIM) for h in range(N_HEADS)]
    part = lambda name, u: pre[u[0], u[1]][name][:, heads[u[2]]]
    b_cum = {u: _dot_exact_lhs(tri_b, pre[u]["lf"]) for u in bc}
    b_t = {u: b_cum[u].T for u in bc}
    li_t = {u: pre[u]["li"].T for u in bc}
    b_col = {u: b_cum[u[0], u[1]][:, N_HEADS + u[2]:N_HEADS + u[2] + 1] for u in keys}
    b_last = {u: b_col[u][CHUNK - 1:CHUNK, :] for u in keys}
    d_log = {u: jnp.where(tri, b_col[u] - b_t[u[0], u[1]][N_HEADS + u[2]:N_HEADS + u[2] + 1, :]
                          + li_t[u[0], u[1]][u[2]:u[2] + 1, :], -jnp.inf) for u in keys}
    dmax = {u: jnp.max(d_log[u], axis=1, keepdims=True) for u in keys}
    qk = {u: _bdot_nt(part("q", u), part("k", u)) for u in keys}
    s0 = {u: jnp.exp(d_log[u] - dmax[u]) * qk[u] for u in keys}
    sv = {u: _bdot(s0[u], part("v", u)) for u in keys}
    ssum = {u: jnp.sum(s0[u], axis=1, keepdims=True) for u in keys}
    g_loc = {u: b_last[u] - b_col[u] + pre[u[0], u[1]]["li"][:, u[2]:u[2] + 1] for u in keys}
    m_loc = {u: jnp.max(g_loc[u], axis=0, keepdims=True) for u in keys}
    kw = {u: part("k", u) * jnp.exp(g_loc[u] - m_loc[u]) for u in keys}
    kwv = {u: _bdot_tn(kw[u], part("v", u)) for u in keys}
    kwsum = {u: jnp.sum(kw[u], axis=0, keepdims=True) for u in keys}

    bh = [(b, h) for b in range(nb) for h in range(N_HEADS)]
    c_st = {q: c_ref[q[0], q[1]] for q in bh}
    n_st = {q: n_ref[q[0], q[1]] for q in bh}
    m_st = {q: m_ref[q[0], q[1]] for q in bh}
    h_parts = {}
    for c in range(n_chunks):
        full = lambda q: (q[0], c, q[1])
        qc = {q: _bdot(part("q", full(q)), c_st[q]) for q in bh}
        qn = {q: jnp.sum(part("q", full(q)) * n_st[q], axis=1, keepdims=True) for q in bh}
        inter = {q: b_col[full(q)] + m_st[q] for q in bh}
        m_t = {q: jnp.maximum(inter[q], dmax[full(q)]) for q in bh}
        e_loc = {q: jnp.exp(dmax[full(q)] - m_t[q]) for q in bh}
        w_int = {q: jnp.exp(inter[q] - m_t[q]) for q in bh}
        for q in bh:
            num = e_loc[q] * sv[full(q)] + w_int[q] * qc[q]
            den = e_loc[q] * ssum[full(q)] + w_int[q] * qn[q]
            h_parts[full(q)] = num / jnp.maximum(jnp.abs(den), jnp.exp(-m_t[q]))
        m_new = {q: jnp.maximum(b_last[full(q)] + m_st[q], m_loc[full(q)]) for q in bh}
        s_old = {q: jnp.exp(b_last[full(q)] + m_st[q] - m_new[q]) for q in bh}
        s_new = {q: jnp.exp(m_loc[full(q)] - m_new[q]) for q in bh}
        c_st = {q: s_old[q] * c_st[q] + s_new[q] * kwv[full(q)] for q in bh}
        n_st = {q: s_old[q] * n_st[q] + s_new[q] * kwsum[full(q)] for q in bh}
        m_st = m_new
    for q in bh:
        c_ref[q[0], q[1]], n_ref[q[0], q[1]], m_ref[q[0], q[1]] = c_st[q], n_st[q], m_st[q]

    for b in range(nb):
        hh = jnp.concatenate([jnp.concatenate([h_parts[b, c, h] for h in range(N_HEADS)], axis=1)
                              for c in range(n_chunks)], axis=0) * _sigmoid(og_all[b])
        mean = _dot_exact_rhs(hh, ones_h) * (1.0 / HEAD_DIM)
        hc = hh - mean
        var = _dot_exact_rhs(hc * hc, ones_h) * (1.0 / HEAD_DIM)
        y_ref[b] = (hc * lax.rsqrt(var + 1e-5) * ng_ref[...]).astype(y_ref.dtype)


def _mlstm(p_d, B, TP, conv_w, conv_b, i_b, f_b, norm_g):
    ib = jnp.zeros((1, LANES), F32).at[0, 0:N_HEADS].set(i_b)
    fb = jnp.zeros((1, LANES), F32).at[0, N_HEADS:2 * N_HEADS].set(f_b)
    full = lambda shape: pl.BlockSpec(shape, lambda j: (0,) * len(shape))
    return pl.pallas_call(
        _mlstm_kernel,
        grid=(TP // ROW_TILE,),
        in_specs=[pl.BlockSpec((B, ROW_TILE, 1152), lambda j: (0, j, 0)),
                  full((CONV_W, 512)), full((1, 512)), full((1, LANES)), full((1, LANES)), full((1, MIX_W))],
        out_specs=pl.BlockSpec((B, ROW_TILE, MIX_W), lambda j: (0, j, 0)),
        out_shape=jax.ShapeDtypeStruct((B, TP, MIX_W), BF16),
        scratch_shapes=[pltpu.VMEM((B, 8, 512), F32),
                        pltpu.VMEM((B, N_HEADS, HEAD_DIM, HEAD_DIM), F32),
                        pltpu.VMEM((B, N_HEADS, 1, HEAD_DIM), F32),
                        pltpu.VMEM((B, N_HEADS, 1, 1), F32)],
        compiler_params=_cparams("arbitrary"),
        name="mlstm",
    )(p_d.reshape(B, TP, 1152), conv_w.astype(F32), conv_b.reshape(1, 512).astype(F32), ib, fb,
      norm_g.reshape(1, MIX_W).astype(F32))


V_ROWS = 80
WT_ROWS = 528


def _dsa_prep_kernel(h_ref, wt_ref, wn_ref, kvg_ref, wuk_ref, wuvt_ref,
                     qt_ref, qit_ref, wit_ref, k_ref, ki_ref, vt_ref):
    hb = h_ref[...]
    tm = hb.shape[0]
    pt = lax.dot_general(wt_ref[...], hb, _NT, preferred_element_type=F32)
    pn = jnp.dot(hb, wn_ref[...], preferred_element_type=F32)
    ckv = pn[:, 0:DSA_KV_RANK]
    c = ckv * lax.rsqrt(jnp.mean(ckv * ckv, -1, keepdims=True) + 1e-6) * kvg_ref[...]
    cb = c.astype(BF16)
    k_ref[...] = jnp.dot(cb, wuk_ref[...], preferred_element_type=F32).astype(BF16)
    ki_ref[...] = pn[:, DSA_KV_RANK:DSA_KV_RANK + IDX_DIM].astype(BF16)
    vt = lax.dot_general(wuvt_ref[...], cb, _NT, preferred_element_type=F32)
    vt = jnp.where(_iota((V_ROWS, tm), 0) == HEAD_DIM, 1.0, vt)
    for t in range(tm // LANES):
        cs = slice(t * LANES, (t + 1) * LANES)
        for h in range(N_HEADS):
            qt_ref[t, :, h * LANES:(h + 1) * LANES] = (
                pt[h * HEAD_DIM:(h + 1) * HEAD_DIM, cs] * (HEAD_DIM ** -0.5)).astype(BF16)
        for h in range(IDX_HEADS):
            qit_ref[t, :, h * LANES:(h + 1) * LANES] = pt[MIX_W + h * IDX_DIM:MIX_W + (h + 1) * IDX_DIM, cs].astype(BF16)
        wit_ref[t] = pt[2 * MIX_W:2 * MIX_W + IDX_HEADS, cs] * ((IDX_HEADS * IDX_DIM) ** -0.5)
        vt_ref[t] = vt[:, cs].astype(BF16)


def _dsa_prep(hb, w_t, w_n, kv_norm_g, w_uk, w_uv):
    N, D = hb.shape
    tm = _pick_tile(N, 640)
    nt = tm // LANES
    full = lambda shape: pl.BlockSpec(shape, lambda i: (0,) * len(shape))
    wuvt = jnp.pad(w_uv.T, ((0, V_ROWS - HEAD_DIM), (0, 0))).astype(BF16)
    return pl.pallas_call(
        _dsa_prep_kernel,
        grid=(N // tm,),
        in_specs=[pl.BlockSpec((tm, D), lambda i: (i, 0)),
                  full((WT_ROWS, D)), full((D, 256)), full((1, DSA_KV_RANK)),
                  full((DSA_KV_RANK, HEAD_DIM)), full((V_ROWS, DSA_KV_RANK))],
        out_specs=[pl.BlockSpec((nt, HEAD_DIM, N_HEADS * LANES), lambda i: (i, 0, 0)),
                   pl.BlockSpec((nt, IDX_DIM, IDX_HEADS * LANES), lambda i: (i, 0, 0)),
                   pl.BlockSpec((nt, IDX_HEADS, LANES), lambda i: (i, 0, 0)),
                   pl.BlockSpec((tm, HEAD_DIM), lambda i: (i, 0)),
                   pl.BlockSpec((tm, IDX_DIM), lambda i: (i, 0)),
                   pl.BlockSpec((nt, V_ROWS, LANES), lambda i: (i, 0, 0))],
        out_shape=[jax.ShapeDtypeStruct((N // LANES, HEAD_DIM, N_HEADS * LANES), BF16),
                   jax.ShapeDtypeStruct((N // LANES, IDX_DIM, IDX_HEADS * LANES), BF16),
                   jax.ShapeDtypeStruct((N // LANES, IDX_HEADS, LANES), F32),
                   jax.ShapeDtypeStruct((N, HEAD_DIM), BF16),
                   jax.ShapeDtypeStruct((N, IDX_DIM), BF16),
                   jax.ShapeDtypeStruct((N // LANES, V_ROWS, LANES), BF16)],
        compiler_params=_cparams("arbitrary"),
        name="dsa_prep",
    )(hb, w_t, w_n, kv_norm_g.reshape(1, DSA_KV_RANK).astype(F32), w_uk.astype(BF16), wuvt)


def _dsa_kernel(qt_ref, qit_ref, wit_ref, k_ref, ki_ref, vt_ref, bias_ref, y_ref,
                sk_ref, rel_ref, m_ref, acc_ref, lg_ref, mg_ref, *, topk):
    i = pl.program_id(1)
    nk = i + 1
    QT = ROW_TILE
    HQ = N_HEADS * QT
    t_lane = i * QT + _iota((LANES, QT), 1)
    key_pos = lambda kt: kt * LANES + _iota((LANES, QT), 0)
    per_head = lambda fn: jnp.concatenate([fn(slice(h * QT, (h + 1) * QT)) for h in range(N_HEADS)], axis=1)

    qit = qit_ref[0]
    wit = wit_ref[0]

    GW = rel_ref.shape[1] // LANES
    n_tiles = sk_ref.shape[0] - 2
    n_trips = (i + 2 * GW) // (2 * GW)

    def group_base(g):
        return jnp.clip(GW * g, 0, n_tiles - GW)

    def issue(g, slot):
        span = pl.ds(pl.multiple_of(group_base(g) * LANES, LANES), GW * LANES)
        rel_ref[slot] = jnp.dot(ki_ref[span, :], qit, preferred_element_type=F32)

    def reduce(g, slot):
        for u in range(GW):
            kt = group_base(g) + u
            rows_u = slice(u * LANES, (u + 1) * LANES)
            score = jnp.maximum(rel_ref[slot, rows_u, 0:QT], 0.0) * wit[0:1, :]
            for h in range(1, IDX_HEADS):
                score = score + jnp.maximum(rel_ref[slot, rows_u, h * QT:(h + 1) * QT], 0.0) * wit[h:h + 1, :]
            mine = (kt >= GW * g) & (kt <= i)
            sk_ref[jnp.where(mine, kt, n_tiles)] = score

    issue(0, 0)

    def score_body(jj, c):
        issue(2 * jj + 1, 1)
        reduce(2 * jj, 0)
        issue(2 * jj + 2, 0)
        reduce(2 * jj + 1, 1)
        return c

    lax.fori_loop(0, n_trips, score_body, 0)
    first = sk_ref[0]
    first = jnp.where(key_pos(0) < FP + N_META, jnp.inf, first)
    sk_ref[0] = jnp.where(key_pos(0) >= FP, first, -jnp.inf)
    sk_ref[i] = jnp.where(key_pos(i) <= t_lane, sk_ref[i], -jnp.inf)
    sk_ref[n_tiles + 1] = jnp.full((LANES, QT), -jnp.inf, F32)

    def key_to_float(key):
        return lax.bitcast_convert_type(jnp.where(key < 0, key ^ jnp.int32(0x7FFFFFFF), key), F32)
    def count(pred_fn):
        def body(kt, acc):
            return acc + jnp.where(pred_fn(sk_ref[kt], kt), 1, 0)

        def body4(j, acc):
            for u in range(4):
                acc = body(4 * j + u, acc)
            return acc

        n4 = lax.shift_right_logical(nk, 2)
        acc = lax.fori_loop(0, n4, body4, jnp.zeros((LANES, QT), jnp.int32))
        acc = lax.fori_loop(4 * n4, nk, body, acc)
        return jnp.sum(acc, axis=0, keepdims=True)

    def bit_body(it, carry):
        tau, n_ge = carry
        cand = tau + jnp.left_shift(jnp.int32(1), 31 - it)
        cand_f = key_to_float(cand)
        cnt = count(lambda sk, kt: sk >= cand_f)
        return jnp.where(cnt >= topk, cand, tau), jnp.where(cnt >= topk, cnt, n_ge)

    tau_key, n_ge = lax.fori_loop(0, 32, bit_body, (jnp.full((1, QT), INT_MIN, jnp.int32),
                                                    jnp.zeros((1, QT), jnp.int32)))
    tau = jnp.where(tau_key < KEY_LOWEST, jnp.float32(FLT_LOWEST), key_to_float(tau_key))

    @pl.when(jnp.max(n_ge - topk) > 0)
    def _():
        n_bits = max(1, int(math.ceil(math.log2(sk_ref.shape[0] * LANES + 1))))
        need = topk - count(lambda sk, kt: sk > tau)

        def pos_body(it, x):
            cand = x + jnp.left_shift(jnp.int32(1), n_bits - 1 - it)
            cnt = count(lambda sk, kt: (sk == tau) & (key_pos(kt) < cand))
            return jnp.where(cnt < need, cand, x)

        x = lax.fori_loop(0, n_bits, pos_body, jnp.zeros((1, QT), jnp.int32))
        jmax = jnp.where(n_ge > topk, x, jnp.int32(2 ** 30))

        def drop_body(kt, c):
            sk = sk_ref[kt]
            sk_ref[kt] = jnp.where((sk == tau) & (key_pos(kt) > jmax), -jnp.inf, sk)
            return c

        lax.fori_loop(0, nk, drop_body, 0)

    qt = qt_ref[0]
    m_ref[...] = jnp.full((1, HQ), NEG, F32)
    acc_ref[...] = jnp.zeros((V_ROWS, HQ), F32)

    def park(g, slot):
        base = group_base(g)
        span = pl.ds(pl.multiple_of(base * LANES, LANES), GW * LANES)
        lg_all = jnp.dot(k_ref[span, :], qt, preferred_element_type=F32)
        tmax = None
        for u in range(GW):
            t = base + u
            mine = (t >= GW * g) & (t <= i)
            lg = lg_all[u * LANES:(u + 1) * LANES, :] + bias_ref[jnp.clip(i - t, 0, 2)]
            sel = sk_ref[jnp.where(mine, t, n_tiles + 1)] >= tau
            lgm = per_head(lambda hs: jnp.where(sel, lg[:, hs], NEG))
            lg_ref[slot, u] = lgm
            tmax = lgm if tmax is None else jnp.maximum(tmax, lgm)
        mg_ref[slot] = jnp.max(tmax, axis=0, keepdims=True)

    def weights(slot):
        m_old = m_ref[...]
        m_new = jnp.maximum(m_old, mg_ref[slot])
        m_ref[...] = m_new
        return jnp.exp(m_old - m_new), [jnp.exp(lg_ref[slot, u] - m_new).astype(BF16) for u in range(GW)]

    def fold(g, corr, prs):
        vt_all = jnp.concatenate([vt_ref[group_base(g) + u] for u in range(GW)], axis=1)
        pv = jnp.dot(vt_all, jnp.concatenate(prs, axis=0), preferred_element_type=F32)
        acc_ref[...] = acc_ref[...] * corr + pv

    park(0, 0)

    def pipe_body(jj, c):
        corr, prs = weights(0)
        park(2 * jj + 1, 1)
        fold(2 * jj, corr, prs)
        corr, prs = weights(1)
        park(2 * jj + 2, 0)
        fold(2 * jj + 1, corr, prs)
        return c

    lax.fori_loop(0, n_trips, pipe_body, 0)
    acc = acc_ref[...]
    out = acc[0:HEAD_DIM, :] / jnp.maximum(acc[HEAD_DIM:HEAD_DIM + 1, :], 1e-30)
    y_ref[...] = per_head(lambda hs: out[:, hs].T).astype(y_ref.dtype)


def _t5_bucket(dist):
    max_exact = N_BUCKETS // 2
    n = jnp.maximum(dist, 0)
    large = max_exact + (jnp.log(jnp.maximum(n, 1).astype(F32) / max_exact)
                         / math.log(MAX_DISTANCE / max_exact) * (N_BUCKETS - max_exact)).astype(jnp.int32)
    return jnp.where(n < max_exact, n, jnp.minimum(large, N_BUCKETS - 1))


def _bias_tables(rel_bias):
    per_dist = rel_bias[_t5_bucket(jnp.arange(2 * ROW_TILE, dtype=jnp.int32))]
    q_minus_s = np.arange(ROW_TILE)[None, :] - np.arange(ROW_TILE)[:, None]
    far = per_dist[2 * ROW_TILE - 1]
    tabs = [per_dist[np.clip(r * ROW_TILE + q_minus_s, 0, 2 * ROW_TILE - 1)] - far for r in (0, 1)]
    tabs.append(jnp.zeros_like(tabs[0]))
    return jnp.stack(tabs).transpose(0, 1, 3, 2).reshape(3, ROW_TILE, N_HEADS * ROW_TILE).astype(F32)


def _dsa(qt, qit, wit, k, ki, vt, bias_tab, B, TP, topk):
    nq = TP // ROW_TILE
    return pl.pallas_call(
        functools.partial(_dsa_kernel, topk=topk),
        grid=(B, nq),
        in_specs=[pl.BlockSpec((1, HEAD_DIM, N_HEADS * LANES), lambda b, i: (b * nq + i, 0, 0)),
                  pl.BlockSpec((1, IDX_DIM, IDX_HEADS * LANES), lambda b, i: (b * nq + i, 0, 0)),
                  pl.BlockSpec((1, IDX_HEADS, LANES), lambda b, i: (b * nq + i, 0, 0)),
                  pl.BlockSpec((TP, HEAD_DIM), lambda b, i: (b, 0)),
                  pl.BlockSpec((TP, IDX_DIM), lambda b, i: (b, 0)),
                  pl.BlockSpec((nq, V_ROWS, LANES), lambda b, i: (b, 0, 0)),
                  pl.BlockSpec((3, ROW_TILE, N_HEADS * ROW_TILE), lambda b, i: (0, 0, 0))],
        out_specs=pl.BlockSpec((ROW_TILE, MIX_W), lambda b, i: (b * nq + i, 0)),
        out_shape=jax.ShapeDtypeStruct((B * TP, MIX_W), BF16),
        scratch_shapes=[pltpu.VMEM((nq + 2, LANES, ROW_TILE), F32),
                        pltpu.VMEM((2, min(4, nq) * LANES, IDX_HEADS * ROW_TILE), F32),
                        pltpu.VMEM((1, N_HEADS * ROW_TILE), F32),
                        pltpu.VMEM((V_ROWS, N_HEADS * ROW_TILE), F32),
                        pltpu.VMEM((2, min(4, nq), LANES, N_HEADS * ROW_TILE), F32),
                        pltpu.VMEM((2, 1, N_HEADS * ROW_TILE), F32)],
        compiler_params=_cparams("parallel", "arbitrary"),
        name="dsa_attend",
    )(qt, qit, wit, k, ki, vt, bias_tab)


def _layer_norm_rows(z, g, b):
    mu = jnp.mean(z, -1, keepdims=True)
    zc = z - mu
    var = jnp.mean(zc * zc, -1, keepdims=True)
    return zc * lax.rsqrt(var + LN_EPS) * g + b


def _merge_kernel(h_ref, hb_ref, wg_ref, ya_ref, yb_ref, yc_ref, yd_ref, wb_ref, wo_ref, lg_ref, lb_ref,
                  h1_ref, h1b_ref):
    hb = hb_ref[...]
    merged = None
    for i, y_ref in enumerate((ya_ref, yb_ref, yc_ref, yd_ref)):
        gate = _sigmoid(jnp.dot(hb, wg_ref[:, i * D_MODEL:(i + 1) * D_MODEL], preferred_element_type=F32))
        t = gate * jnp.dot(y_ref[...], wb_ref[i], preferred_element_type=F32)
        merged = t if merged is None else merged + t
    z = DN_ALPHA * h_ref[...] + jnp.dot(merged.astype(BF16), wo_ref[...], preferred_element_type=F32)
    y = _layer_norm_rows(z, lg_ref[...], lb_ref[...])
    h1_ref[...] = y
    h1b_ref[...] = y.astype(BF16)


def _merge(h, hb, w_g, ys, w_branch, w_out, ln_g, ln_b):
    N, D = h.shape
    tm = _pick_tile(N, 640)
    full = lambda shape: pl.BlockSpec(shape, lambda i: (0,) * len(shape))
    tok = lambda w: pl.BlockSpec((tm, w), lambda i: (i, 0))
    return pl.pallas_call(
        _merge_kernel,
        grid=(N // tm,),
        in_specs=[tok(D), tok(D), full((D, 4 * D)), tok(MIX_W), tok(MIX_W), tok(MIX_W), tok(MIX_W),
                  full((4, MIX_W, D)), full((D, D)), full((1, D)), full((1, D))],
        out_specs=[tok(D), tok(D)],
        out_shape=[jax.ShapeDtypeStruct((N, D), F32), jax.ShapeDtypeStruct((N, D), BF16)],
        compiler_params=_cparams("arbitrary"),
        name="merge_out_ln",
    )(h, hb, w_g, *ys, w_branch.astype(BF16), w_out.astype(BF16),
      ln_g.reshape(1, D).astype(F32), ln_b.reshape(1, D).astype(F32))


def _moe_kernel(h_ref, hb_ref, wr_ref, br_ref, wg_ref, wu_ref, wd_ref, lg_ref, lb_ref, o_ref, ob_ref,
                gate_ref, acc_ref):
    e = pl.program_id(1)
    xb = hb_ref[...]
    tm = xb.shape[0]
    lane = _iota((tm, LANES), 1)

    @pl.when(e == 0)
    def _():
        logit = jnp.dot(xb, wr_ref[...], preferred_element_type=F32) + br_ref[...]
        big = jnp.int32(LANES)
        gl = jnp.where(lane < N_GROUPS, logit, -jnp.inf)
        gmax = jnp.max(gl, axis=1, keepdims=True)
        g_sel = jnp.min(jnp.where(gl == gmax, lane, big), axis=1, keepdims=True)
        p_grp = 1.0 / jnp.sum(jnp.exp(gl - gmax), axis=1, keepdims=True)
        lo = N_GROUPS + g_sel * EPG
        el = jnp.where((lane >= lo) & (lane < lo + EPG), logit, -jnp.inf)
        v1 = jnp.max(el, axis=1, keepdims=True)
        i1 = jnp.min(jnp.where(el == v1, lane, big), axis=1, keepdims=True)
        el2 = jnp.where(lane == i1, -jnp.inf, el)
        v2 = jnp.max(el2, axis=1, keepdims=True)
        i2 = jnp.min(jnp.where(el2 == v2, lane, big), axis=1, keepdims=True)
        e2 = jnp.exp(v2 - v1)
        w1 = p_grp / (1.0 + e2)
        w2 = p_grp * e2 / (1.0 + e2)
        gate_ref[...] = jnp.where(lane == i1, w1, 0.0) + jnp.where(lane == i2, w2, 0.0)
        acc_ref[...] = jnp.zeros_like(acc_ref)

    hid = _silu(jnp.dot(xb, wg_ref[0], preferred_element_type=F32)) * jnp.dot(xb, wu_ref[0], preferred_element_type=F32)
    gates = gate_ref[...]
    scaled = []
    for j in range(EPG):
        g_j = jnp.sum(jnp.where(lane == e * EPG + j + N_GROUPS, gates, 0.0), axis=1, keepdims=True)
        scaled.append((hid[:, j * D_EXPERT:(j + 1) * D_EXPERT] * g_j).astype(BF16))
    acc_ref[...] += jnp.dot(jnp.concatenate(scaled, axis=1), wd_ref[0], preferred_element_type=F32)

    @pl.when(e == N_GROUPS - 1)
    def _():
        y = _layer_norm_rows(DN_ALPHA * h_ref[...] + acc_ref[...], lg_ref[...], lb_ref[...])
        o_ref[...] = y
        ob_ref[...] = y.astype(BF16)


def _moe(h1, h1b, w_grp, b_grp, w_rt, b_rt, w_gate, w_up, w_down, ln_g, ln_b):
    N, D = h1.shape
    tm = _pick_tile(N, 640)
    w_r = jnp.zeros((D, LANES), F32).at[:, 0:N_GROUPS].set(w_grp).at[:, N_GROUPS:N_GROUPS + N_EXPERTS].set(w_rt)
    b_r = jnp.zeros((1, LANES), F32).at[0, 0:N_GROUPS].set(b_grp).at[0, N_GROUPS:N_GROUPS + N_EXPERTS].set(b_rt)
    GH = EPG * D_EXPERT
    by_group = lambda w: w.reshape(N_GROUPS, EPG, D, D_EXPERT).transpose(0, 2, 1, 3).reshape(N_GROUPS, D, GH)
    w_gate, w_up, w_down = by_group(w_gate), by_group(w_up), w_down.reshape(N_GROUPS, GH, D)
    full = lambda shape: pl.BlockSpec(shape, lambda i, e: (0,) * len(shape))
    tok = lambda w: pl.BlockSpec((tm, w), lambda i, e: (i, 0))
    return pl.pallas_call(
        _moe_kernel,
        grid=(N // tm, N_GROUPS),
        in_specs=[tok(D), tok(D), full((D, LANES)), full((1, LANES)),
                  pl.BlockSpec((1, D, GH), lambda i, e: (e, 0, 0)),
                  pl.BlockSpec((1, D, GH), lambda i, e: (e, 0, 0)),
                  pl.BlockSpec((1, GH, D), lambda i, e: (e, 0, 0)),
                  full((1, D)), full((1, D))],
        out_specs=[tok(D), tok(D)],
        out_shape=[jax.ShapeDtypeStruct((N, D), F32), jax.ShapeDtypeStruct((N, D), BF16)],
        scratch_shapes=[pltpu.VMEM((tm, LANES), F32), pltpu.VMEM((tm, D), F32)],
        compiler_params=_cparams("arbitrary", "arbitrary"),
        name="hier_moe_ln",
    )(h1, h1b, w_r.astype(BF16), b_r, w_gate.astype(BF16), w_up.astype(BF16), w_down.astype(BF16),
      ln_g.reshape(1, D).astype(F32), ln_b.reshape(1, D).astype(F32))


def _pad_cols(w, width):
    return jnp.pad(w, ((0, 0), (0, width - w.shape[1])))


def _split_w_in(w):
    o = 0
    w_a = w[:, o:o + 1024]; o += 1024
    gq, gk, gv, ga, gg = (w[:, o:o + 128], w[:, o + 128:o + 256], w[:, o + 256:o + 512],
                          w[:, o + 512:o + 528], w[:, o + 528:o + 784]); o += 784
    w_b = _pad_cols(jnp.concatenate([gq, gk, gv, gg, ga], axis=1), 896)
    cq, ckv, cqi, cki, cwi = (w[:, o:o + 256], w[:, o + 256:o + 384], w[:, o + 384:o + 640],
                              w[:, o + 640:o + 672], w[:, o + 672:o + 680]); o += 680
    w_t = jnp.pad(jnp.concatenate([cq.T, cqi.T, cwi.T], axis=0), ((0, WT_ROWS - 2 * MIX_W - IDX_HEADS), (0, 0)))
    w_n = _pad_cols(jnp.concatenate([ckv, cki], axis=1), 256)
    dq, dk, dv, di, df, do = (w[:, o:o + 256], w[:, o + 256:o + 512], w[:, o + 512:o + 768],
                              w[:, o + 768:o + 772], w[:, o + 772:o + 776], w[:, o + 776:o + 1032]); o += 1032
    w_d = _pad_cols(jnp.concatenate([dq, dk, dv, do, di, df], axis=1), 1152)
    w_g = w[:, o:o + 4096]
    bf = lambda a: a.astype(BF16)
    return bf(w_a), bf(w_b), bf(w_t), bf(w_n), bf(w_d), bf(w_g)


def kernel(x, meta, ln_in_g, ln_in_b, rel_bias, w_in, rwkv_mu, rwkv_w_up, rwkv_w0, rwkv_a_up, rwkv_a0, rwkv_g_up, rwkv_k_k, rwkv_k_a, rwkv_r_k, rwkv_gn_g, rwkv_gn_b, gla_a_up, gla_a_b, gla_norm_g, dsa_kv_norm_g, dsa_w_uk, dsa_w_uv, mlstm_conv_w, mlstm_conv_b, mlstm_i_b, mlstm_f_b, mlstm_norm_g, w_branch, w_out, ln1_g, ln1_b, moe_w_grp, moe_b_grp, moe_w_rt, moe_b_rt, moe_w_gate, moe_w_up, moe_w_down, ln2_g, ln2_b):
    B, S, D = x.shape
    assert D == D_MODEL and S % ROW_TILE == 0
    TP = S + FRONT
    N = B * TP
    topk = min(TOPK_MAX, S // 4)
    bias_tab = _bias_tables(rel_bias)

    h, hb = _embed(x, meta, ln_in_g, ln_in_b)
    h = h.reshape(N, D)
    hb = hb.reshape(N, D)
    for l in range(DEPTH):
        w_a, w_b, w_t, w_n, w_d, w_g = _split_w_in(w_in[l])
        p_a, p_b, p_d = _proj(hb, (w_a, w_b, w_d))
        qt, qit, wit, k, ki, vt = _dsa_prep(hb, w_t, w_n, dsa_kv_norm_g[l], dsa_w_uk[l], dsa_w_uv[l])
        y_a = _rwkv(p_a, B, TP, rwkv_mu[l], rwkv_w_up[l], rwkv_w0[l], rwkv_a_up[l], rwkv_a0[l], rwkv_g_up[l],
                    rwkv_k_k[l], rwkv_k_a[l], rwkv_r_k[l], rwkv_gn_g[l], rwkv_gn_b[l])
        y_b = _gla(p_b, B, TP, gla_a_up[l], gla_a_b[l], gla_norm_g[l])
        y_c = _dsa(qt, qit, wit, k, ki, vt, bias_tab, B, TP, topk)
        y_d = _mlstm(p_d, B, TP, mlstm_conv_w[l], mlstm_conv_b[l], mlstm_i_b[l], mlstm_f_b[l], mlstm_norm_g[l])
        ys = (y_a.reshape(N, MIX_W), y_b.reshape(N, MIX_W), y_c, y_d.reshape(N, MIX_W))
        h1, h1b = _merge(h, hb, w_g, ys, w_branch[l], w_out[l], ln1_g[l], ln1_b[l])
        h, hb = _moe(h1, h1b, moe_w_grp[l], moe_b_grp[l], moe_w_rt[l], moe_b_rt[l],
                     moe_w_gate[l], moe_w_up[l], moe_w_down[l], ln2_g[l], ln2_b[l])
    return h.reshape(B, TP, D)[:, FRONT:]
```

```python
import functools
import math

import numpy as np
import jax
import jax.numpy as jnp
from jax import lax
from jax.experimental import pallas as pl
from jax.experimental.pallas import tpu as pltpu

F32 = jnp.float32
BF16 = jnp.bfloat16

D_MODEL = 1024
HEAD_DIM = 64
N_HEADS = 4
MIX_W = 256
N_META = 16
CHUNK = 64
LANES = 128
ROW_TILE = 128
FRONT = ROW_TILE
FP = FRONT - N_META
NEG = -1e30
LN_EPS = 1e-5
DEPTH = 2
DN_ALPHA = (2 * DEPTH) ** 0.25

RWKV_GN_EPS = HEAD_DIM * 1e-5
GLA_DK = 32
GLA_TAU = 16.0
DSA_KV_RANK = 128
IDX_HEADS = 8
IDX_DIM = 32
TOPK_MAX = 256
N_BUCKETS = 32
MAX_DISTANCE = 128
CONV_W = 4
N_GROUPS = 4
EPG = 4
N_EXPERTS = 16
D_EXPERT = 256

INT_MIN = -(2 ** 31)
FLT_LOWEST = float(np.finfo(np.float32).min)
KEY_LOWEST = -(2 ** 31) + 0x00800000
VMEM_LIMIT = 56 * 1024 * 1024


def _cparams(*sem):
    return pltpu.CompilerParams(dimension_semantics=tuple(sem), vmem_limit_bytes=VMEM_LIMIT)


def _pick_tile(n, target):
    best = LANES
    t = LANES
    while t <= min(n, target):
        if n % t == 0:
            best = t
        t += LANES
    return best


def _bdot(a, b):
    return jnp.dot(a.astype(BF16), b.astype(BF16), preferred_element_type=F32)


def _bdot_nt(a, b):
    return lax.dot_general(a.astype(BF16), b.astype(BF16), (((1,), (1,)), ((), ())),
                           preferred_element_type=F32)


def _bdot_tn(a, b):
    return lax.dot_general(a.astype(BF16), b.astype(BF16), (((0,), (0,)), ((), ())),
                           preferred_element_type=F32)


def _split(a):
    hi = a.astype(BF16)
    lo = (a - hi.astype(F32)).astype(BF16)
    return hi, lo


_NT = (((1,), (1,)), ((), ()))


def _dot_exact_lhs(a_bf16, b):
    bh, bl = _split(b)
    return (jnp.dot(a_bf16, bh, preferred_element_type=F32)
            + jnp.dot(a_bf16, bl, preferred_element_type=F32))


def _dot_exact_rhs(a, b_bf16):
    ah, al = _split(a)
    return (jnp.dot(ah, b_bf16, preferred_element_type=F32)
            + jnp.dot(al, b_bf16, preferred_element_type=F32))


def _sigmoid(x):
    return 1.0 / (1.0 + jnp.exp(-x))


def _log_sigmoid(x):
    return jnp.minimum(x, 0.0) - jnp.log(1.0 + jnp.exp(-jnp.abs(x)))


def _silu(x):
    return x * _sigmoid(x)


def _iota(shape, dim):
    return lax.broadcasted_iota(jnp.int32, shape, dim)


def _tri_incl(n):
    return (_iota((n, n), 1) <= _iota((n, n), 0))


def _head_ones():
    return ((_iota((MIX_W, MIX_W), 0) // HEAD_DIM) == (_iota((MIX_W, MIX_W), 1) // HEAD_DIM)).astype(BF16)


def _embed_kernel(x_ref, meta_ref, g_ref, b_ref, h_ref, hb_ref):
    j = pl.program_id(0)
    for bi in range(x_ref.shape[0]):
        src = jnp.where(j == 0, meta_ref[...], x_ref[bi])
        mu = jnp.mean(src, -1, keepdims=True)
        xc = src - mu
        var = jnp.mean(xc * xc, -1, keepdims=True)
        y = xc * lax.rsqrt(var + LN_EPS) * g_ref[...] + b_ref[...]
        h_ref[bi] = y
        hb_ref[bi] = y.astype(BF16)


def _embed(x, meta, g, b):
    B, S, D = x.shape
    TP = S + FRONT
    meta_pad = jnp.concatenate([jnp.zeros((FP, D), F32), meta.astype(F32)], axis=0)
    return pl.pallas_call(
        _embed_kernel,
        grid=(TP // ROW_TILE,),
        in_specs=[
            pl.BlockSpec((B, ROW_TILE, D), lambda j: (0, jnp.maximum(j - 1, 0), 0)),
            pl.BlockSpec((ROW_TILE, D), lambda j: (0, 0)),
            pl.BlockSpec((1, D), lambda j: (0, 0)),
            pl.BlockSpec((1, D), lambda j: (0, 0)),
        ],
        out_specs=[
            pl.BlockSpec((B, ROW_TILE, D), lambda j: (0, j, 0)),
            pl.BlockSpec((B, ROW_TILE, D), lambda j: (0, j, 0)),
        ],
        out_shape=[jax.ShapeDtypeStruct((B, TP, D), F32), jax.ShapeDtypeStruct((B, TP, D), BF16)],
        compiler_params=_cparams("arbitrary"),
        name="embed_ln",
    )(x, meta_pad, g.reshape(1, D), b.reshape(1, D))


def _proj_kernel(h_ref, *refs):
    n = len(refs) // 2
    hb = h_ref[...]
    for w_ref, o_ref in zip(refs[:n], refs[n:]):
        o_ref[...] = jnp.dot(hb, w_ref[...], preferred_element_type=F32)


def _proj(hb, ws):
    N, D = hb.shape
    tm = _pick_tile(N, 640)
    return pl.pallas_call(
        _proj_kernel,
        grid=(N // tm,),
        in_specs=[pl.BlockSpec((tm, D), lambda i: (i, 0))]
                 + [pl.BlockSpec((D, w.shape[1]), lambda i: (0, 0)) for w in ws],
        out_specs=[pl.BlockSpec((tm, w.shape[1]), lambda i: (i, 0)) for w in ws],
        out_shape=[jax.ShapeDtypeStruct((N, w.shape[1]), F32) for w in ws],
        compiler_params=_cparams("arbitrary"),
        name="in_proj",
    )(hb, *ws)


def _rwkv_kernel(p_ref, mu_ref, wup_ref, w0_ref, aup_ref, a0_ref, gup_ref, kk_ref, ka_ref, rk_ref,
                 gng_ref, gnb_ref, y_ref, carry_ref, s_ref):
    j = pl.program_id(0)
    nb = p_ref.shape[0]
    n_chunks = ROW_TILE // CHUNK

    @pl.when(j == 0)
    def _():
        carry_ref[...] = jnp.zeros_like(carry_ref)
        s_ref[...] = jnp.zeros_like(s_ref)

    valid = (j * ROW_TILE + _iota((ROW_TILE, 1), 0)) >= FP
    first_row = _iota((ROW_TILE, 1), 0) == 0
    ones_h = _head_ones()
    tri = _tri_incl(CHUNK)
    tri_b = tri.astype(BF16)
    strict = _iota((CHUNK, CHUNK), 1) < _iota((CHUNK, CHUNK), 0)
    eye = (_iota((CHUNK, CHUNK), 1) == _iota((CHUNK, CHUNK), 0)).astype(F32)
    heads = [slice(h * HEAD_DIM, (h + 1) * HEAD_DIM) for h in range(N_HEADS)]

    pro = []
    unit = {}
    for b in range(nb):
        p = jnp.where(valid, p_ref[b], 0.0)
        prev = jnp.where(first_row, carry_ref[b], pltpu.roll(p, 1, 0))
        carry_ref[b] = p[ROW_TILE - 1:ROW_TILE, :]
        ps = p + (prev - p) * mu_ref[...]
        r = ps[:, 0:256]
        k = ps[:, 256:512]
        v = ps[:, 512:768]
        lora_in = ps[:, 768:896]
        xg = ps[:, 896:1024]
        w_log = _log_sigmoid(w0_ref[...] + _bdot(jnp.tanh(lora_in), wup_ref[...])) - 0.5
        lw = jnp.where(valid, -jnp.exp(w_log), 0.0)
        alpha = _sigmoid(a0_ref[...] + _bdot(lora_in, aup_ref[...]))
        gate = _bdot(_sigmoid(xg), gup_ref[...])
        kk = k * kk_ref[...]
        kk = kk / jnp.maximum(jnp.sqrt(_dot_exact_rhs(kk * kk, ones_h)), 1e-12)
        k = k * (1.0 + (alpha - 1.0) * ka_ref[...])
        kka = kk * alpha
        pro.append((r, k, v, gate))
        for c in range(n_chunks):
            sl = slice(c * CHUNK, (c + 1) * CHUNK)
            lw_c = lw[sl]
            cum = _dot_exact_lhs(tri_b, lw_c)
            cum_last = cum[CHUNK - 1:CHUNK, :]
            p_inv = jnp.exp(-cum)
            p_tail = jnp.exp(cum_last - cum)
            unit[b, c] = dict(a=-kk[sl] * jnp.exp(cum - lw_c), b=kka[sl] * p_inv, k=k[sl] * p_inv,
                              r=r[sl] * jnp.exp(cum), kb=k[sl] * p_tail, bb=kka[sl] * p_tail,
                              pl=jnp.exp(cum_last), v=v[sl])

    keys = [(b, c, h) for b in range(nb) for c in range(n_chunks) for h in range(N_HEADS)]
    part = lambda name, key: unit[key[0], key[1]][name][:, heads[key[2]]]
    a_ab = {q: jnp.where(strict, _bdot_nt(part("a", q), part("b", q)), 0.0) for q in keys}
    a_ak = {q: jnp.where(strict, _bdot_nt(part("a", q), part("k", q)), 0.0) for q in keys}
    a_rb = {q: jnp.where(tri, _bdot_nt(part("r", q), part("b", q)), 0.0) for q in keys}
    a_rk = {q: jnp.where(tri, _bdot_nt(part("r", q), part("k", q)), 0.0) for q in keys}
    inv = {q: eye + a_ab[q] for q in keys}
    pw = a_ab
    for _ in range(5):
        pw = {q: _bdot(pw[q], pw[q]) for q in keys}
        inv = {q: inv[q] + _bdot(inv[q], pw[q]) for q in keys}
    ak_v = {q: _bdot(a_ak[q], part("v", q)) for q in keys}
    rk_v = {q: _bdot(a_rk[q], part("v", q)) for q in keys}
    kb_v = {q: _bdot_tn(part("v", q), part("kb", q)) for q in keys}

    bh = [(b, h) for b in range(nb) for h in range(N_HEADS)]
    state = {q: s_ref[q[0], q[1]] for q in bh}
    y_parts = {}
    for c in range(n_chunks):
        full = lambda q: (q[0], c, q[1])
        a_s = {q: _bdot_nt(part("a", full(q)), state[q]) for q in bh}
        r_s = {q: _bdot_nt(part("r", full(q)), state[q]) for q in bh}
        u = {q: _bdot(inv[full(q)], a_s[q] + ak_v[full(q)]) for q in bh}
        for q in bh:
            y_parts[full(q)] = r_s[q] + rk_v[full(q)] + _bdot(a_rb[full(q)], u[q])
        state = {q: (state[q] * part("pl", full(q)) + kb_v[full(q)] + _bdot_tn(u[q], part("bb", full(q))))
                 for q in bh}
    for q in bh:
        s_ref[q[0], q[1]] = state[q]

    for b in range(nb):
        r, k, v, gate = pro[b]
        y = jnp.concatenate([jnp.concatenate([y_parts[b, c, h] for h in range(N_HEADS)], axis=1)
                             for c in range(n_chunks)], axis=0)
        mean = _dot_exact_rhs(y, ones_h) * (1.0 / HEAD_DIM)
        yc = y - mean
        var = _dot_exact_rhs(yc * yc, ones_h) * (1.0 / HEAD_DIM)
        yn = yc * lax.rsqrt(var + RWKV_GN_EPS) * gng_ref[...] + gnb_ref[...]
        bonus = _dot_exact_rhs(r * k * rk_ref[...], ones_h) * v
        y_ref[b] = ((yn + bonus) * gate).astype(y_ref.dtype)


def _rwkv(p_a, B, TP, mu, w_up, w0, a_up, a0, g_up, k_k, k_a, r_k, gn_g, gn_b):
    W = MIX_W
    z64 = jnp.zeros((64, W), F32)
    wup_pad = jnp.concatenate([w_up, z64], axis=0).astype(BF16)
    aup_pad = jnp.concatenate([z64, a_up], axis=0).astype(BF16)
    row = lambda a: a.reshape(1, -1).astype(F32)
    full = lambda shape: pl.BlockSpec(shape, lambda j: (0,) * len(shape))
    return pl.pallas_call(
        _rwkv_kernel,
        grid=(TP // ROW_TILE,),
        in_specs=[pl.BlockSpec((B, ROW_TILE, 1024), lambda j: (0, j, 0)),
                  full((1, 1024)), full((128, W)), full((1, W)), full((128, W)), full((1, W)),
                  full((128, W)), full((1, W)), full((1, W)), full((1, W)), full((1, W)), full((1, W))],
        out_specs=pl.BlockSpec((B, ROW_TILE, W), lambda j: (0, j, 0)),
        out_shape=jax.ShapeDtypeStruct((B, TP, W), BF16),
        scratch_shapes=[pltpu.VMEM((B, 1, 1024), F32), pltpu.VMEM((B, N_HEADS, HEAD_DIM, HEAD_DIM), F32)],
        compiler_params=_cparams("arbitrary"),
        name="rwkv7",
    )(p_a.reshape(B, TP, 1024), row(mu), wup_pad, row(w0), aup_pad, row(a0), g_up.astype(BF16),
      row(k_k), row(k_a), row(r_k), row(gn_g), row(gn_b))


def _gla_kernel(p_ref, aup_ref, ab_ref, ng_ref, y_ref, s_ref):
    j = pl.program_id(0)
    nb = p_ref.shape[0]
    n_chunks = ROW_TILE // CHUNK

    @pl.when(j == 0)
    def _():
        s_ref[...] = jnp.zeros_like(s_ref)

    valid = (j * ROW_TILE + _iota((ROW_TILE, 1), 0)) >= FP
    tri = _tri_incl(CHUNK)
    tri_b = tri.astype(BF16)

    og_all = []
    pre = {}
    for b in range(nb):
        p = jnp.where(valid, p_ref[b], 0.0)
        la = _log_sigmoid(_bdot(p[:, 768:896], aup_ref[...]) + ab_ref[...]) * (1.0 / GLA_TAU)
        la = jnp.where(valid, la, 0.0)
        og_all.append(p[:, 512:768])
        for c in range(n_chunks):
            sl = slice(c * CHUNK, (c + 1) * CHUNK)
            pre[b, c] = dict(q=p[sl, 0:128] * (GLA_DK ** -0.5), k=p[sl, 128:256], v=p[sl, 256:512], la=la[sl])
    bc = [(b, c) for b in range(nb) for c in range(n_chunks)]
    keys = [(b, c, h) for (b, c) in bc for h in range(N_HEADS)]
    ks = [slice(h * GLA_DK, (h + 1) * GLA_DK) for h in range(N_HEADS)]
    vs = [slice(h * HEAD_DIM, (h + 1) * HEAD_DIM) for h in range(N_HEADS)]
    b_cum = {u: _dot_exact_lhs(tri_b, pre[u]["la"]) for u in bc}
    b_last = {u: b_cum[u][CHUNK - 1:CHUNK, :] for u in bc}
    q_g = {u: pre[u]["q"] * jnp.exp(b_cum[u]) for u in bc}
    k_g = {u: pre[u]["k"] * jnp.exp(-b_cum[u]) for u in bc}
    k_l = {u: pre[u]["k"] * jnp.exp(b_last[u] - b_cum[u]) for u in bc}
    dec = {u: jnp.exp(b_last[u]) for u in bc}
    att = {u: jnp.where(tri, _bdot_nt(q_g[u[0], u[1]][:, ks[u[2]]], k_g[u[0], u[1]][:, ks[u[2]]]), 0.0)
           for u in keys}
    att_v = {u: _bdot(att[u], pre[u[0], u[1]]["v"][:, vs[u[2]]]) for u in keys}
    kl_v = {u: _bdot_tn(pre[u[0], u[1]]["v"][:, vs[u[2]]], k_l[u[0], u[1]][:, ks[u[2]]]) for u in keys}

    bh = [(b, h) for b in range(nb) for h in range(N_HEADS)]
    state = {q: s_ref[q[0], q[1]] for q in bh}
    o_parts = {}
    for c in range(n_chunks):
        for q in bh:
            o_parts[q[0], c, q[1]] = att_v[q[0], c, q[1]] + _bdot_nt(q_g[q[0], c][:, ks[q[1]]], state[q])
        state = {q: state[q] * dec[q[0], c][:, ks[q[1]]] + kl_v[q[0], c, q[1]] for q in bh}
    for q in bh:
        s_ref[q[0], q[1]] = state[q]

    ones_h = _head_ones()
    for b in range(nb):
        o = jnp.concatenate([jnp.concatenate([o_parts[b, c, h] for h in range(N_HEADS)], axis=1)
                             for c in range(n_chunks)], axis=0)
        ms = _dot_exact_rhs(o * o, ones_h) * (1.0 / HEAD_DIM)
        y = o * lax.rsqrt(ms + 1e-6) * ng_ref[...] * _silu(og_all[b])
        y_ref[b] = y.astype(y_ref.dtype)


def _gla(p_b, B, TP, a_up, a_b, norm_g):
    aup_pad = jnp.zeros((128, 128), F32).at[:a_up.shape[0]].set(a_up).astype(BF16)
    full = lambda shape: pl.BlockSpec(shape, lambda j: (0,) * len(shape))
    return pl.pallas_call(
        _gla_kernel,
        grid=(TP // ROW_TILE,),
        in_specs=[pl.BlockSpec((B, ROW_TILE, 896), lambda j: (0, j, 0)),
                  full((128, 128)), full((1, 128)), full((1, MIX_W))],
        out_specs=pl.BlockSpec((B, ROW_TILE, MIX_W), lambda j: (0, j, 0)),
        out_shape=jax.ShapeDtypeStruct((B, TP, MIX_W), BF16),
        scratch_shapes=[pltpu.VMEM((B, N_HEADS, HEAD_DIM, GLA_DK), F32)],
        compiler_params=_cparams("arbitrary"),
        name="gla",
    )(p_b.reshape(B, TP, 896), aup_pad, a_b.reshape(1, 128).astype(F32),
      jnp.tile(norm_g.astype(F32), N_HEADS).reshape(1, MIX_W))


def _mlstm_kernel(p_ref, cw_ref, cb_ref, ib_ref, fb_ref, ng_ref, y_ref, carry_ref, c_ref, n_ref, m_ref):
    j = pl.program_id(0)
    nb = p_ref.shape[0]
    n_chunks = ROW_TILE // CHUNK

    @pl.when(j == 0)
    def _():
        carry_ref[...] = jnp.zeros_like(carry_ref)
        c_ref[...] = jnp.zeros_like(c_ref)
        n_ref[...] = jnp.zeros_like(n_ref)
        m_ref[...] = jnp.zeros_like(m_ref)

    valid = (j * ROW_TILE + _iota((ROW_TILE, 1), 0)) >= FP
    tri = _tri_incl(CHUNK)
    tri_b = tri.astype(BF16)
    ones_h = _head_ones()

    og_all = []
    pre = {}
    for b in range(nb):
        p = jnp.where(valid, p_ref[b], 0.0)
        a = p[:, 0:512]
        ext = jnp.concatenate([carry_ref[b], a], axis=0)
        carry_ref[b] = a[ROW_TILE - 8:ROW_TILE, :]
        conv = cb_ref[...] + a * cw_ref[CONV_W - 1:CONV_W, :]
        for s in range(1, CONV_W):
            conv = conv + pltpu.roll(ext, s, 0)[8:8 + ROW_TILE, :] * cw_ref[CONV_W - 1 - s:CONV_W - s, :]
        qk = _silu(conv)
        q = jnp.where(valid, qk[:, 0:MIX_W], 0.0)
        k = jnp.where(valid, qk[:, MIX_W:2 * MIX_W], 0.0) * (HEAD_DIM ** -0.5)
        v = p[:, 512:768]
        og_all.append(p[:, 768:1024])
        gates = p[:, 1024:1152]
        li_all = jnp.where(valid, gates + ib_ref[...], NEG)
        lf_all = jnp.where(valid, _log_sigmoid(gates + fb_ref[...]), 0.0)
        for c in range(n_chunks):
            sl = slice(c * CHUNK, (c + 1) * CHUNK)
            pre[b, c] = dict(q=q[sl], k=k[sl], v=v[sl], li=li_all[sl], lf=lf_all[sl])

    bc = [(b, c) for b in range(nb) for c in range(n_chunks)]
    keys = [(b, c, h) for (b, c) in bc for h in range(N_HEADS)]
    heads = [slice(h * HEAD_DIM, (h + 1) * HEAD_DIM) for h in range(N_HEADS)]
    part = lambda name, u: pre[u[0], u[1]][name][:, heads[u[2]]]
    b_cum = {u: _dot_exact_lhs(tri_b, pre[u]["lf"]) for u in bc}
    b_t = {u: b_cum[u].T for u in bc}
    li_t = {u: pre[u]["li"].T for u in bc}
    b_col = {u: b_cum[u[0], u[1]][:, N_HEADS + u[2]:N_HEADS + u[2] + 1] for u in keys}
    b_last = {u: b_col[u][CHUNK - 1:CHUNK, :] for u in keys}
    d_log = {u: jnp.where(tri, b_col[u] - b_t[u[0], u[1]][N_HEADS + u[2]:N_HEADS + u[2] + 1, :]
                          + li_t[u[0], u[1]][u[2]:u[2] + 1, :], -jnp.inf) for u in keys}
    dmax = {u: jnp.max(d_log[u], axis=1, keepdims=True) for u in keys}
    qk = {u: _bdot_nt(part("q", u), part("k", u)) for u in keys}
    s0 = {u: jnp.exp(d_log[u] - dmax[u]) * qk[u] for u in keys}
    sv = {u: _bdot(s0[u], part("v", u)) for u in keys}
    ssum = {u: jnp.sum(s0[u], axis=1, keepdims=True) for u in keys}
    g_loc = {u: b_last[u] - b_col[u] + pre[u[0], u[1]]["li"][:, u[2]:u[2] + 1] for u in keys}
    m_loc = {u: jnp.max(g_loc[u], axis=0, keepdims=True) for u in keys}
    kw = {u: part("k", u) * jnp.exp(g_loc[u] - m_loc[u]) for u in keys}
    kwv = {u: _bdot_tn(kw[u], part("v", u)) for u in keys}
    kwsum = {u: jnp.sum(kw[u], axis=0, keepdims=True) for u in keys}

    bh = [(b, h) for b in range(nb) for h in range(N_HEADS)]
    c_st = {q: c_ref[q[0], q[1]] for q in bh}
    n_st = {q: n_ref[q[0], q[1]] for q in bh}
    m_st = {q: m_ref[q[0], q[1]] for q in bh}
    h_parts = {}
    for c in range(n_chunks):
        full = lambda q: (q[0], c, q[1])
        qc = {q: _bdot(part("q", full(q)), c_st[q]) for q in bh}
        qn = {q: jnp.sum(part("q", full(q)) * n_st[q], axis=1, keepdims=True) for q in bh}
        inter = {q: b_col[full(q)] + m_st[q] for q in bh}
        m_t = {q: jnp.maximum(inter[q], dmax[full(q)]) for q in bh}
        e_loc = {q: jnp.exp(dmax[full(q)] - m_t[q]) for q in bh}
        w_int = {q: jnp.exp(inter[q] - m_t[q]) for q in bh}
        for q in bh:
            num = e_loc[q] * sv[full(q)] + w_int[q] * qc[q]
            den = e_loc[q] * ssum[full(q)] + w_int[q] * qn[q]
            h_parts[full(q)] = num / jnp.maximum(jnp.abs(den), jnp.exp(-m_t[q]))
        m_new = {q: jnp.maximum(b_last[full(q)] + m_st[q], m_loc[full(q)]) for q in bh}
        s_old = {q: jnp.exp(b_last[full(q)] + m_st[q] - m_new[q]) for q in bh}
        s_new = {q: jnp.exp(m_loc[full(q)] - m_new[q]) for q in bh}
        c_st = {q: s_old[q] * c_st[q] + s_new[q] * kwv[full(q)] for q in bh}
        n_st = {q: s_old[q] * n_st[q] + s_new[q] * kwsum[full(q)] for q in bh}
        m_st = m_new
    for q in bh:
        c_ref[q[0], q[1]], n_ref[q[0], q[1]], m_ref[q[0], q[1]] = c_st[q], n_st[q], m_st[q]

    for b in range(nb):
        hh = jnp.concatenate([jnp.concatenate([h_parts[b, c, h] for h in range(N_HEADS)], axis=1)
                              for c in range(n_chunks)], axis=0) * _sigmoid(og_all[b])
        mean = _dot_exact_rhs(hh, ones_h) * (1.0 / HEAD_DIM)
        hc = hh - mean
        var = _dot_exact_rhs(hc * hc, ones_h) * (1.0 / HEAD_DIM)
        y_ref[b] = (hc * lax.rsqrt(var + 1e-5) * ng_ref[...]).astype(y_ref.dtype)


def _mlstm(p_d, B, TP, conv_w, conv_b, i_b, f_b, norm_g):
    ib = jnp.zeros((1, LANES), F32).at[0, 0:N_HEADS].set(i_b)
    fb = jnp.zeros((1, LANES), F32).at[0, N_HEADS:2 * N_HEADS].set(f_b)
    full = lambda shape: pl.BlockSpec(shape, lambda j: (0,) * len(shape))
    return pl.pallas_call(
        _mlstm_kernel,
        grid=(TP // ROW_TILE,),
        in_specs=[pl.BlockSpec((B, ROW_TILE, 1152), lambda j: (0, j, 0)),
                  full((CONV_W, 512)), full((1, 512)), full((1, LANES)), full((1, LANES)), full((1, MIX_W))],
        out_specs=pl.BlockSpec((B, ROW_TILE, MIX_W), lambda j: (0, j, 0)),
        out_shape=jax.ShapeDtypeStruct((B, TP, MIX_W), BF16),
        scratch_shapes=[pltpu.VMEM((B, 8, 512), F32),
                        pltpu.VMEM((B, N_HEADS, HEAD_DIM, HEAD_DIM), F32),
                        pltpu.VMEM((B, N_HEADS, 1, HEAD_DIM), F32),
                        pltpu.VMEM((B, N_HEADS, 1, 1), F32)],
        compiler_params=_cparams("arbitrary"),
        name="mlstm",
    )(p_d.reshape(B, TP, 1152), conv_w.astype(F32), conv_b.reshape(1, 512).astype(F32), ib, fb,
      norm_g.reshape(1, MIX_W).astype(F32))


V_ROWS = 80
WT_ROWS = 528


def _dsa_prep_kernel(h_ref, wt_ref, wn_ref, kvg_ref, wuk_ref, wuvt_ref,
                     qt_ref, qit_ref, wit_ref, k_ref, ki_ref, vt_ref):
    hb = h_ref[...]
    tm = hb.shape[0]
    pt = lax.dot_general(wt_ref[...], hb, _NT, preferred_element_type=F32)
    pn = jnp.dot(hb, wn_ref[...], preferred_element_type=F32)
    ckv = pn[:, 0:DSA_KV_RANK]
    c = ckv * lax.rsqrt(jnp.mean(ckv * ckv, -1, keepdims=True) + 1e-6) * kvg_ref[...]
    cb = c.astype(BF16)
    k_ref[...] = jnp.dot(cb, wuk_ref[...], preferred_element_type=F32).astype(BF16)
    ki_ref[...] = pn[:, DSA_KV_RANK:DSA_KV_RANK + IDX_DIM].astype(BF16)
    vt = lax.dot_general(wuvt_ref[...], cb, _NT, preferred_element_type=F32)
    vt = jnp.where(_iota((V_ROWS, tm), 0) == HEAD_DIM, 1.0, vt)
    for t in range(tm // LANES):
        cs = slice(t * LANES, (t + 1) * LANES)
        for h in range(N_HEADS):
            qt_ref[t, :, h * LANES:(h + 1) * LANES] = (
                pt[h * HEAD_DIM:(h + 1) * HEAD_DIM, cs] * (HEAD_DIM ** -0.5)).astype(BF16)
        for h in range(IDX_HEADS):
            qit_ref[t, :, h * LANES:(h + 1) * LANES] = pt[MIX_W + h * IDX_DIM:MIX_W + (h + 1) * IDX_DIM, cs].astype(BF16)
        wit_ref[t] = pt[2 * MIX_W:2 * MIX_W + IDX_HEADS, cs] * ((IDX_HEADS * IDX_DIM) ** -0.5)
        vt_ref[t] = vt[:, cs].astype(BF16)


def _dsa_prep(hb, w_t, w_n, kv_norm_g, w_uk, w_uv):
    N, D = hb.shape
    tm = _pick_tile(N, 640)
    nt = tm // LANES
    full = lambda shape: pl.BlockSpec(shape, lambda i: (0,) * len(shape))
    wuvt = jnp.pad(w_uv.T, ((0, V_ROWS - HEAD_DIM), (0, 0))).astype(BF16)
    return pl.pallas_call(
        _dsa_prep_kernel,
        grid=(N // tm,),
        in_specs=[pl.BlockSpec((tm, D), lambda i: (i, 0)),
                  full((WT_ROWS, D)), full((D, 256)), full((1, DSA_KV_RANK)),
                  full((DSA_KV_RANK, HEAD_DIM)), full((V_ROWS, DSA_KV_RANK))],
        out_specs=[pl.BlockSpec((nt, HEAD_DIM, N_HEADS * LANES), lambda i: (i, 0, 0)),
                   pl.BlockSpec((nt, IDX_DIM, IDX_HEADS * LANES), lambda i: (i, 0, 0)),
                   pl.BlockSpec((nt, IDX_HEADS, LANES), lambda i: (i, 0, 0)),
                   pl.BlockSpec((tm, HEAD_DIM), lambda i: (i, 0)),
                   pl.BlockSpec((tm, IDX_DIM), lambda i: (i, 0)),
                   pl.BlockSpec((nt, V_ROWS, LANES), lambda i: (i, 0, 0))],
        out_shape=[jax.ShapeDtypeStruct((N // LANES, HEAD_DIM, N_HEADS * LANES), BF16),
                   jax.ShapeDtypeStruct((N // LANES, IDX_DIM, IDX_HEADS * LANES), BF16),
                   jax.ShapeDtypeStruct((N // LANES, IDX_HEADS, LANES), F32),
                   jax.ShapeDtypeStruct((N, HEAD_DIM), BF16),
                   jax.ShapeDtypeStruct((N, IDX_DIM), BF16),
                   jax.ShapeDtypeStruct((N // LANES, V_ROWS, LANES), BF16)],
        compiler_params=_cparams("arbitrary"),
        name="dsa_prep",
    )(hb, w_t, w_n, kv_norm_g.reshape(1, DSA_KV_RANK).astype(F32), w_uk.astype(BF16), wuvt)


def _dsa_kernel(qt_ref, qit_ref, wit_ref, k_ref, ki_ref, vt_ref, bias_ref, y_ref,
                sk_ref, rel_ref, m_ref, acc_ref, lg_ref, mg_ref, *, topk):
    i = pl.program_id(1)
    nk = i + 1
    QT = ROW_TILE
    HQ = N_HEADS * QT
    t_lane = i * QT + _iota((LANES, QT), 1)
    key_pos = lambda kt: kt * LANES + _iota((LANES, QT), 0)
    per_head = lambda fn: jnp.concatenate([fn(slice(h * QT, (h + 1) * QT)) for h in range(N_HEADS)], axis=1)

    qit = qit_ref[0]
    wit = wit_ref[0]

    GW = rel_ref.shape[1] // LANES
    n_tiles = sk_ref.shape[0] - 2
    n_trips = (i + 2 * GW) // (2 * GW)

    def group_base(g):
        return jnp.clip(GW * g, 0, n_tiles - GW)

    def issue(g, slot):
        span = pl.ds(pl.multiple_of(group_base(g) * LANES, LANES), GW * LANES)
        rel_ref[slot] = jnp.dot(ki_ref[span, :], qit, preferred_element_type=F32)

    def reduce(g, slot):
        for u in range(GW):
            kt = group_base(g) + u
            rows_u = slice(u * LANES, (u + 1) * LANES)
            score = jnp.maximum(rel_ref[slot, rows_u, 0:QT], 0.0) * wit[0:1, :]
            for h in range(1, IDX_HEADS):
                score = score + jnp.maximum(rel_ref[slot, rows_u, h * QT:(h + 1) * QT], 0.0) * wit[h:h + 1, :]
            mine = (kt >= GW * g) & (kt <= i)
            sk_ref[jnp.where(mine, kt, n_tiles)] = score

    issue(0, 0)

    def score_body(jj, c):
        issue(2 * jj + 1, 1)
        reduce(2 * jj, 0)
        issue(2 * jj + 2, 0)
        reduce(2 * jj + 1, 1)
        return c

    lax.fori_loop(0, n_trips, score_body, 0)
    first = sk_ref[0]
    first = jnp.where(key_pos(0) < FP + N_META, jnp.inf, first)
    sk_ref[0] = jnp.where(key_pos(0) >= FP, first, -jnp.inf)
    sk_ref[i] = jnp.where(key_pos(i) <= t_lane, sk_ref[i], -jnp.inf)
    sk_ref[n_tiles + 1] = jnp.full((LANES, QT), -jnp.inf, F32)

    def key_to_float(key):
        return lax.bitcast_convert_type(jnp.where(key < 0, key ^ jnp.int32(0x7FFFFFFF), key), F32)
    def count(pred_fn):
        def body(kt, acc):
            return acc + jnp.where(pred_fn(sk_ref[kt], kt), 1, 0)

        def body4(j, acc):
            for u in range(4):
                acc = body(4 * j + u, acc)
            return acc

        n4 = lax.shift_right_logical(nk, 2)
        acc = lax.fori_loop(0, n4, body4, jnp.zeros((LANES, QT), jnp.int32))
        acc = lax.fori_loop(4 * n4, nk, body, acc)
        return jnp.sum(acc, axis=0, keepdims=True)

    def bit_body(it, carry):
        tau, n_ge = carry
        cand = tau + jnp.left_shift(jnp.int32(1), 31 - it)
        cand_f = key_to_float(cand)
        cnt = count(lambda sk, kt: sk >= cand_f)
        return jnp.where(cnt >= topk, cand, tau), jnp.where(cnt >= topk, cnt, n_ge)

    tau_key, n_ge = lax.fori_loop(0, 32, bit_body, (jnp.full((1, QT), INT_MIN, jnp.int32),
                                                    jnp.zeros((1, QT), jnp.int32)))
    tau = jnp.where(tau_key < KEY_LOWEST, jnp.float32(FLT_LOWEST), key_to_float(tau_key))

    @pl.when(jnp.max(n_ge - topk) > 0)
    def _():
        n_bits = max(1, int(math.ceil(math.log2(sk_ref.shape[0] * LANES + 1))))
        need = topk - count(lambda sk, kt: sk > tau)

        def pos_body(it, x):
            cand = x + jnp.left_shift(jnp.int32(1), n_bits - 1 - it)
            cnt = count(lambda sk, kt: (sk == tau) & (key_pos(kt) < cand))
            return jnp.where(cnt < need, cand, x)

        x = lax.fori_loop(0, n_bits, pos_body, jnp.zeros((1, QT), jnp.int32))
        jmax = jnp.where(n_ge > topk, x, jnp.int32(2 ** 30))

        def drop_body(kt, c):
            sk = sk_ref[kt]
            sk_ref[kt] = jnp.where((sk == tau) & (key_pos(kt) > jmax), -jnp.inf, sk)
            return c

        lax.fori_loop(0, nk, drop_body, 0)

    qt = qt_ref[0]
    m_ref[...] = jnp.full((1, HQ), NEG, F32)
    acc_ref[...] = jnp.zeros((V_ROWS, HQ), F32)

    def park(g, slot):
        base = group_base(g)
        span = pl.ds(pl.multiple_of(base * LANES, LANES), GW * LANES)
        lg_all = jnp.dot(k_ref[span, :], qt, preferred_element_type=F32)
        tmax = None
        for u in range(GW):
            t = base + u
            mine = (t >= GW * g) & (t <= i)
            lg = lg_all[u * LANES:(u + 1) * LANES, :] + bias_ref[jnp.clip(i - t, 0, 2)]
            sel = sk_ref[jnp.where(mine, t, n_tiles + 1)] >= tau
            lgm = per_head(lambda hs: jnp.where(sel, lg[:, hs], NEG))
            lg_ref[slot, u] = lgm
            tmax = lgm if tmax is None else jnp.maximum(tmax, lgm)
        mg_ref[slot] = jnp.max(tmax, axis=0, keepdims=True)

    def weights(slot):
        m_old = m_ref[...]
        m_new = jnp.maximum(m_old, mg_ref[slot])
        m_ref[...] = m_new
        return jnp.exp(m_old - m_new), [jnp.exp(lg_ref[slot, u] - m_new).astype(BF16) for u in range(GW)]

    def fold(g, corr, prs):
        vt_all = jnp.concatenate([vt_ref[group_base(g) + u] for u in range(GW)], axis=1)
        pv = jnp.dot(vt_all, jnp.concatenate(prs, axis=0), preferred_element_type=F32)
        acc_ref[...] = acc_ref[...] * corr + pv

    park(0, 0)

    def pipe_body(jj, c):
        corr, prs = weights(0)
        park(2 * jj + 1, 1)
        fold(2 * jj, corr, prs)
        corr, prs = weights(1)
        park(2 * jj + 2, 0)
        fold(2 * jj + 1, corr, prs)
        return c

    lax.fori_loop(0, n_trips, pipe_body, 0)
    acc = acc_ref[...]
    out = acc[0:HEAD_DIM, :] / jnp.maximum(acc[HEAD_DIM:HEAD_DIM + 1, :], 1e-30)
    y_ref[...] = per_head(lambda hs: out[:, hs].T).astype(y_ref.dtype)


def _t5_bucket(dist):
    max_exact = N_BUCKETS // 2
    n = jnp.maximum(dist, 0)
    large = max_exact + (jnp.log(jnp.maximum(n, 1).astype(F32) / max_exact)
                         / math.log(MAX_DISTANCE / max_exact) * (N_BUCKETS - max_exact)).astype(jnp.int32)
    return jnp.where(n < max_exact, n, jnp.minimum(large, N_BUCKETS - 1))


def _bias_tables(rel_bias):
    per_dist = rel_bias[_t5_bucket(jnp.arange(2 * ROW_TILE, dtype=jnp.int32))]
    q_minus_s = np.arange(ROW_TILE)[None, :] - np.arange(ROW_TILE)[:, None]
    far = per_dist[2 * ROW_TILE - 1]
    tabs = [per_dist[np.clip(r * ROW_TILE + q_minus_s, 0, 2 * ROW_TILE - 1)] - far for r in (0, 1)]
    tabs.append(jnp.zeros_like(tabs[0]))
    return jnp.stack(tabs).transpose(0, 1, 3, 2).reshape(3, ROW_TILE, N_HEADS * ROW_TILE).astype(F32)


def _dsa(qt, qit, wit, k, ki, vt, bias_tab, B, TP, topk):
    nq = TP // ROW_TILE
    return pl.pallas_call(
        functools.partial(_dsa_kernel, topk=topk),
        grid=(B, nq),
        in_specs=[pl.BlockSpec((1, HEAD_DIM, N_HEADS * LANES), lambda b, i: (b * nq + i, 0, 0)),
                  pl.BlockSpec((1, IDX_DIM, IDX_HEADS * LANES), lambda b, i: (b * nq + i, 0, 0)),
                  pl.BlockSpec((1, IDX_HEADS, LANES), lambda b, i: (b * nq + i, 0, 0)),
                  pl.BlockSpec((TP, HEAD_DIM), lambda b, i: (b, 0)),
                  pl.BlockSpec((TP, IDX_DIM), lambda b, i: (b, 0)),
                  pl.BlockSpec((nq, V_ROWS, LANES), lambda b, i: (b, 0, 0)),
                  pl.BlockSpec((3, ROW_TILE, N_HEADS * ROW_TILE), lambda b, i: (0, 0, 0))],
        out_specs=pl.BlockSpec((ROW_TILE, MIX_W), lambda b, i: (b * nq + i, 0)),
        out_shape=jax.ShapeDtypeStruct((B * TP, MIX_W), BF16),
        scratch_shapes=[pltpu.VMEM((nq + 2, LANES, ROW_TILE), F32),
                        pltpu.VMEM((2, min(4, nq) * LANES, IDX_HEADS * ROW_TILE), F32),
                        pltpu.VMEM((1, N_HEADS * ROW_TILE), F32),
                        pltpu.VMEM((V_ROWS, N_HEADS * ROW_TILE), F32),
                        pltpu.VMEM((2, min(4, nq), LANES, N_HEADS * ROW_TILE), F32),
                        pltpu.VMEM((2, 1, N_HEADS * ROW_TILE), F32)],
        compiler_params=_cparams("parallel", "arbitrary"),
        name="dsa_attend",
    )(qt, qit, wit, k, ki, vt, bias_tab)


def _layer_norm_rows(z, g, b):
    mu = jnp.mean(z, -1, keepdims=True)
    zc = z - mu
    var = jnp.mean(zc * zc, -1, keepdims=True)
    return zc * lax.rsqrt(var + LN_EPS) * g + b


def _merge_kernel(h_ref, hb_ref, wg_ref, ya_ref, yb_ref, yc_ref, yd_ref, wb_ref, wo_ref, lg_ref, lb_ref,
                  h1_ref, h1b_ref):
    hb = hb_ref[...]
    merged = None
    for i, y_ref in enumerate((ya_ref, yb_ref, yc_ref, yd_ref)):
        gate = _sigmoid(jnp.dot(hb, wg_ref[:, i * D_MODEL:(i + 1) * D_MODEL], preferred_element_type=F32))
        t = gate * jnp.dot(y_ref[...], wb_ref[i], preferred_element_type=F32)
        merged = t if merged is None else merged + t
    z = DN_ALPHA * h_ref[...] + jnp.dot(merged.astype(BF16), wo_ref[...], preferred_element_type=F32)
    y = _layer_norm_rows(z, lg_ref[...], lb_ref[...])
    h1_ref[...] = y
    h1b_ref[...] = y.astype(BF16)


def _merge(h, hb, w_g, ys, w_branch, w_out, ln_g, ln_b):
    N, D = h.shape
    tm = _pick_tile(N, 640)
    full = lambda shape: pl.BlockSpec(shape, lambda i: (0,) * len(shape))
    tok = lambda w: pl.BlockSpec((tm, w), lambda i: (i, 0))
    return pl.pallas_call(
        _merge_kernel,
        grid=(N // tm,),
        in_specs=[tok(D), tok(D), full((D, 4 * D)), tok(MIX_W), tok(MIX_W), tok(MIX_W), tok(MIX_W),
                  full((4, MIX_W, D)), full((D, D)), full((1, D)), full((1, D))],
        out_specs=[tok(D), tok(D)],
        out_shape=[jax.ShapeDtypeStruct((N, D), F32), jax.ShapeDtypeStruct((N, D), BF16)],
        compiler_params=_cparams("arbitrary"),
        name="merge_out_ln",
    )(h, hb, w_g, *ys, w_branch.astype(BF16), w_out.astype(BF16),
      ln_g.reshape(1, D).astype(F32), ln_b.reshape(1, D).astype(F32))


def _moe_kernel(h_ref, hb_ref, wr_ref, br_ref, wg_ref, wu_ref, wd_ref, lg_ref, lb_ref, o_ref, ob_ref,
                gate_ref, acc_ref):
    e = pl.program_id(1)
    xb = hb_ref[...]
    tm = xb.shape[0]
    lane = _iota((tm, LANES), 1)

    @pl.when(e == 0)
    def _():
        logit = jnp.dot(xb, wr_ref[...], preferred_element_type=F32) + br_ref[...]
        big = jnp.int32(LANES)
        gl = jnp.where(lane < N_GROUPS, logit, -jnp.inf)
        gmax = jnp.max(gl, axis=1, keepdims=True)
        g_sel = jnp.min(jnp.where(gl == gmax, lane, big), axis=1, keepdims=True)
        p_grp = 1.0 / jnp.sum(jnp.exp(gl - gmax), axis=1, keepdims=True)
        lo = N_GROUPS + g_sel * EPG
        el = jnp.where((lane >= lo) & (lane < lo + EPG), logit, -jnp.inf)
        v1 = jnp.max(el, axis=1, keepdims=True)
        i1 = jnp.min(jnp.where(el == v1, lane, big), axis=1, keepdims=True)
        el2 = jnp.where(lane == i1, -jnp.inf, el)
        v2 = jnp.max(el2, axis=1, keepdims=True)
        i2 = jnp.min(jnp.where(el2 == v2, lane, big), axis=1, keepdims=True)
        e2 = jnp.exp(v2 - v1)
        w1 = p_grp / (1.0 + e2)
        w2 = p_grp * e2 / (1.0 + e2)
        gate_ref[...] = jnp.where(lane == i1, w1, 0.0) + jnp.where(lane == i2, w2, 0.0)
        acc_ref[...] = jnp.zeros_like(acc_ref)

    hid = _silu(jnp.dot(xb, wg_ref[0], preferred_element_type=F32)) * jnp.dot(xb, wu_ref[0], preferred_element_type=F32)
    gates = gate_ref[...]
    scaled = []
    for j in range(EPG):
        g_j = jnp.sum(jnp.where(lane == e * EPG + j + N_GROUPS, gates, 0.0), axis=1, keepdims=True)
        scaled.append((hid[:, j * D_EXPERT:(j + 1) * D_EXPERT] * g_j).astype(BF16))
    acc_ref[...] += jnp.dot(jnp.concatenate(scaled, axis=1), wd_ref[0], preferred_element_type=F32)

    @pl.when(e == N_GROUPS - 1)
    def _():
        y = _layer_norm_rows(DN_ALPHA * h_ref[...] + acc_ref[...], lg_ref[...], lb_ref[...])
        o_ref[...] = y
        ob_ref[...] = y.astype(BF16)


def _moe(h1, h1b, w_grp, b_grp, w_rt, b_rt, w_gate, w_up, w_down, ln_g, ln_b):
    N, D = h1.shape
    tm = _pick_tile(N, 640)
    w_r = jnp.zeros((D, LANES), F32).at[:, 0:N_GROUPS].set(w_grp).at[:, N_GROUPS:N_GROUPS + N_EXPERTS].set(w_rt)
    b_r = jnp.zeros((1, LANES), F32).at[0, 0:N_GROUPS].set(b_grp).at[0, N_GROUPS:N_GROUPS + N_EXPERTS].set(b_rt)
    GH = EPG * D_EXPERT
    by_group = lambda w: w.reshape(N_GROUPS, EPG, D, D_EXPERT).transpose(0, 2, 1, 3).reshape(N_GROUPS, D, GH)
    w_gate, w_up, w_down = by_group(w_gate), by_group(w_up), w_down.reshape(N_GROUPS, GH, D)
    full = lambda shape: pl.BlockSpec(shape, lambda i, e: (0,) * len(shape))
    tok = lambda w: pl.BlockSpec((tm, w), lambda i, e: (i, 0))
    return pl.pallas_call(
        _moe_kernel,
        grid=(N // tm, N_GROUPS),
        in_specs=[tok(D), tok(D), full((D, LANES)), full((1, LANES)),
                  pl.BlockSpec((1, D, GH), lambda i, e: (e, 0, 0)),
                  pl.BlockSpec((1, D, GH), lambda i, e: (e, 0, 0)),
                  pl.BlockSpec((1, GH, D), lambda i, e: (e, 0, 0)),
                  full((1, D)), full((1, D))],
        out_specs=[tok(D), tok(D)],
        out_shape=[jax.ShapeDtypeStruct((N, D), F32), jax.ShapeDtypeStruct((N, D), BF16)],
        scratch_shapes=[pltpu.VMEM((tm, LANES), F32), pltpu.VMEM((tm, D), F32)],
        compiler_params=_cparams("arbitrary", "arbitrary"),
        name="hier_moe_ln",
    )(h1, h1b, w_r.astype(BF16), b_r, w_gate.astype(BF16), w_up.astype(BF16), w_down.astype(BF16),
      ln_g.reshape(1, D).astype(F32), ln_b.reshape(1, D).astype(F32))


def _pad_cols(w, width):
    return jnp.pad(w, ((0, 0), (0, width - w.shape[1])))


def _split_w_in(w):
    o = 0
    w_a = w[:, o:o + 1024]; o += 1024
    gq, gk, gv, ga, gg = (w[:, o:o + 128], w[:, o + 128:o + 256], w[:, o + 256:o + 512],
                          w[:, o + 512:o + 528], w[:, o + 528:o + 784]); o += 784
    w_b = _pad_cols(jnp.concatenate([gq, gk, gv, gg, ga], axis=1), 896)
    cq, ckv, cqi, cki, cwi = (w[:, o:o + 256], w[:, o + 256:o + 384], w[:, o + 384:o + 640],
                              w[:, o + 640:o + 672], w[:, o + 672:o + 680]); o += 680
    w_t = jnp.pad(jnp.concatenate([cq.T, cqi.T, cwi.T], axis=0), ((0, WT_ROWS - 2 * MIX_W - IDX_HEADS), (0, 0)))
    w_n = _pad_cols(jnp.concatenate([ckv, cki], axis=1), 256)
    dq, dk, dv, di, df, do = (w[:, o:o + 256], w[:, o + 256:o + 512], w[:, o + 512:o + 768],
                              w[:, o + 768:o + 772], w[:, o + 772:o + 776], w[:, o + 776:o + 1032]); o += 1032
    w_d = _pad_cols(jnp.concatenate([dq, dk, dv, do, di, df], axis=1), 1152)
    w_g = w[:, o:o + 4096]
    bf = lambda a: a.astype(BF16)
    return bf(w_a), bf(w_b), bf(w_t), bf(w_n), bf(w_d), bf(w_g)


def kernel(x, meta, ln_in_g, ln_in_b, rel_bias, w_in, rwkv_mu, rwkv_w_up, rwkv_w0, rwkv_a_up, rwkv_a0, rwkv_g_up, rwkv_k_k, rwkv_k_a, rwkv_r_k, rwkv_gn_g, rwkv_gn_b, gla_a_up, gla_a_b, gla_norm_g, dsa_kv_norm_g, dsa_w_uk, dsa_w_uv, mlstm_conv_w, mlstm_conv_b, mlstm_i_b, mlstm_f_b, mlstm_norm_g, w_branch, w_out, ln1_g, ln1_b, moe_w_grp, moe_b_grp, moe_w_rt, moe_b_rt, moe_w_gate, moe_w_up, moe_w_down, ln2_g, ln2_b):
    B, S, D = x.shape
    assert D == D_MODEL and S % ROW_TILE == 0
    TP = S + FRONT
    N = B * TP
    topk = min(TOPK_MAX, S // 4)
    bias_tab = _bias_tables(rel_bias)

    h, hb = _embed(x, meta, ln_in_g, ln_in_b)
    h = h.reshape(N, D)
    hb = hb.reshape(N, D)
    for l in range(DEPTH):
        w_a, w_b, w_t, w_n, w_d, w_g = _split_w_in(w_in[l])
        p_a, p_b, p_d = _proj(hb, (w_a, w_b, w_d))
        qt, qit, wit, k, ki, vt = _dsa_prep(hb, w_t, w_n, dsa_kv_norm_g[l], dsa_w_uk[l], dsa_w_uv[l])
        y_a = _rwkv(p_a, B, TP, rwkv_mu[l], rwkv_w_up[l], rwkv_w0[l], rwkv_a_up[l], rwkv_a0[l], rwkv_g_up[l],
                    rwkv_k_k[l], rwkv_k_a[l], rwkv_r_k[l], rwkv_gn_g[l], rwkv_gn_b[l])
        y_b = _gla(p_b, B, TP, gla_a_up[l], gla_a_b[l], gla_norm_g[l])
        y_c = _dsa(qt, qit, wit, k, ki, vt, bias_tab, B, TP, topk)
        y_d = _mlstm(p_d, B, TP, mlstm_conv_w[l], mlstm_conv_b[l], mlstm_i_b[l], mlstm_f_b[l], mlstm_norm_g[l])
        ys = (y_a.reshape(N, MIX_W), y_b.reshape(N, MIX_W), y_c, y_d.reshape(N, MIX_W))
        h1, h1b = _merge(h, hb, w_g, ys, w_branch[l], w_out[l], ln1_g[l], ln1_b[l])
        h, hb = _moe(h1, h1b, moe_w_grp[l], moe_b_grp[l], moe_w_rt[l], moe_b_rt[l],
                     moe_w_gate[l], moe_w_up[l], moe_w_down[l], ln2_g[l], ln2_b[l])
    return h.reshape(B, TP, D)[:, FRONT:]
```

```python
import functools
import math

import numpy as np
import jax
import jax.numpy as jnp
from jax import lax
from jax.experimental import pallas as pl
from jax.experimental.pallas import tpu as pltpu

F32 = jnp.float32
BF16 = jnp.bfloat16

D_MODEL = 1024
HEAD_DIM = 64
N_HEADS = 4
MIX_W = 256
N_META = 16
CHUNK = 64
LANES = 128
ROW_TILE = 128
FRONT = ROW_TILE
FP = FRONT - N_META
NEG = -1e30
LN_EPS = 1e-5
DEPTH = 2
DN_ALPHA = (2 * DEPTH) ** 0.25

RWKV_GN_EPS = HEAD_DIM * 1e-5
GLA_DK = 32
GLA_TAU = 16.0
DSA_KV_RANK = 128
IDX_HEADS = 8
IDX_DIM = 32
TOPK_MAX = 256
N_BUCKETS = 32
MAX_DISTANCE = 128
CONV_W = 4
N_GROUPS = 4
EPG = 4
N_EXPERTS = 16
D_EXPERT = 256

INT_MIN = -(2 ** 31)
FLT_LOWEST = float(np.finfo(np.float32).min)
KEY_LOWEST = -(2 ** 31) + 0x00800000
VMEM_LIMIT = 56 * 1024 * 1024


def _cparams(*sem):
    return pltpu.CompilerParams(dimension_semantics=tuple(sem), vmem_limit_bytes=VMEM_LIMIT)


def _pick_tile(n, target):
    best = LANES
    t = LANES
    while t <= min(n, target):
        if n % t == 0:
            best = t
        t += LANES
    return best


def _bdot(a, b):
    return jnp.dot(a.astype(BF16), b.astype(BF16), preferred_element_type=F32)


def _bdot_nt(a, b):
    return lax.dot_general(a.astype(BF16), b.astype(BF16), (((1,), (1,)), ((), ())),
                           preferred_element_type=F32)


def _bdot_tn(a, b):
    return lax.dot_general(a.astype(BF16), b.astype(BF16), (((0,), (0,)), ((), ())),
                           preferred_element_type=F32)


def _split(a):
    hi = a.astype(BF16)
    lo = (a - hi.astype(F32)).astype(BF16)
    return hi, lo


_NT = (((1,), (1,)), ((), ()))


def _dot_exact_lhs(a_bf16, b):
    bh, bl = _split(b)
    return (jnp.dot(a_bf16, bh, preferred_element_type=F32)
            + jnp.dot(a_bf16, bl, preferred_element_type=F32))


def _dot_exact_rhs(a, b_bf16):
    ah, al = _split(a)
    return (jnp.dot(ah, b_bf16, preferred_element_type=F32)
            + jnp.dot(al, b_bf16, preferred_element_type=F32))


def _sigmoid(x):
    return 1.0 / (1.0 + jnp.exp(-x))


def _log_sigmoid(x):
    return jnp.minimum(x, 0.0) - jnp.log(1.0 + jnp.exp(-jnp.abs(x)))


def _silu(x):
    return x * _sigmoid(x)


def _iota(shape, dim):
    return lax.broadcasted_iota(jnp.int32, shape, dim)


def _tri_incl(n):
    return (_iota((n, n), 1) <= _iota((n, n), 0))


def _head_ones():
    return ((_iota((MIX_W, MIX_W), 0) // HEAD_DIM) == (_iota((MIX_W, MIX_W), 1) // HEAD_DIM)).astype(BF16)


def _embed_kernel(x_ref, meta_ref, g_ref, b_ref, h_ref, hb_ref):
    j = pl.program_id(0)
    for bi in range(x_ref.shape[0]):
        src = jnp.where(j == 0, meta_ref[...], x_ref[bi])
        mu = jnp.mean(src, -1, keepdims=True)
        xc = src - mu
        var = jnp.mean(xc * xc, -1, keepdims=True)
        y = xc * lax.rsqrt(var + LN_EPS) * g_ref[...] + b_ref[...]
        h_ref[bi] = y
        hb_ref[bi] = y.astype(BF16)


def _embed(x, meta, g, b):
    B, S, D = x.shape
    TP = S + FRONT
    meta_pad = jnp.concatenate([jnp.zeros((FP, D), F32), meta.astype(F32)], axis=0)
    return pl.pallas_call(
        _embed_kernel,
        grid=(TP // ROW_TILE,),
        in_specs=[
            pl.BlockSpec((B, ROW_TILE, D), lambda j: (0, jnp.maximum(j - 1, 0), 0)),
            pl.BlockSpec((ROW_TILE, D), lambda j: (0, 0)),
            pl.BlockSpec((1, D), lambda j: (0, 0)),
            pl.BlockSpec((1, D), lambda j: (0, 0)),
        ],
        out_specs=[
            pl.BlockSpec((B, ROW_TILE, D), lambda j: (0, j, 0)),
            pl.BlockSpec((B, ROW_TILE, D), lambda j: (0, j, 0)),
        ],
        out_shape=[jax.ShapeDtypeStruct((B, TP, D), F32), jax.ShapeDtypeStruct((B, TP, D), BF16)],
        compiler_params=_cparams("arbitrary"),
        name="embed_ln",
    )(x, meta_pad, g.reshape(1, D), b.reshape(1, D))


def _proj_kernel(h_ref, *refs):
    n = len(refs) // 2
    hb = h_ref[...]
    for w_ref, o_ref in zip(refs[:n], refs[n:]):
        o_ref[...] = jnp.dot(hb, w_ref[...], preferred_element_type=F32)


def _proj(hb, ws):
    N, D = hb.shape
    tm = _pick_tile(N, 640)
    return pl.pallas_call(
        _proj_kernel,
        grid=(N // tm,),
        in_specs=[pl.BlockSpec((tm, D), lambda i: (i, 0))]
                 + [pl.BlockSpec((D, w.shape[1]), lambda i: (0, 0)) for w in ws],
        out_specs=[pl.BlockSpec((tm, w.shape[1]), lambda i: (i, 0)) for w in ws],
        out_shape=[jax.ShapeDtypeStruct((N, w.shape[1]), F32) for w in ws],
        compiler_params=_cparams("arbitrary"),
        name="in_proj",
    )(hb, *ws)


def _rwkv_kernel(p_ref, mu_ref, wup_ref, w0_ref, aup_ref, a0_ref, gup_ref, kk_ref, ka_ref, rk_ref,
                 gng_ref, gnb_ref, y_ref, carry_ref, s_ref):
    j = pl.program_id(0)
    nb = p_ref.shape[0]
    n_chunks = ROW_TILE // CHUNK

    @pl.when(j == 0)
    def _():
        carry_ref[...] = jnp.zeros_like(carry_ref)
        s_ref[...] = jnp.zeros_like(s_ref)

    valid = (j * ROW_TILE + _iota((ROW_TILE, 1), 0)) >= FP
    first_row = _iota((ROW_TILE, 1), 0) == 0
    ones_h = _head_ones()
    tri = _tri_incl(CHUNK)
    tri_b = tri.astype(BF16)
    strict = _iota((CHUNK, CHUNK), 1) < _iota((CHUNK, CHUNK), 0)
    eye = (_iota((CHUNK, CHUNK), 1) == _iota((CHUNK, CHUNK), 0)).astype(F32)
    heads = [slice(h * HEAD_DIM, (h + 1) * HEAD_DIM) for h in range(N_HEADS)]

    pro = []
    unit = {}
    for b in range(nb):
        p = jnp.where(valid, p_ref[b], 0.0)
        prev = jnp.where(first_row, carry_ref[b], pltpu.roll(p, 1, 0))
        carry_ref[b] = p[ROW_TILE - 1:ROW_TILE, :]
        ps = p + (prev - p) * mu_ref[...]
        r = ps[:, 0:256]
        k = ps[:, 256:512]
        v = ps[:, 512:768]
        lora_in = ps[:, 768:896]
        xg = ps[:, 896:1024]
        w_log = _log_sigmoid(w0_ref[...] + _bdot(jnp.tanh(lora_in), wup_ref[...])) - 0.5
        lw = jnp.where(valid, -jnp.exp(w_log), 0.0)
        alpha = _sigmoid(a0_ref[...] + _bdot(lora_in, aup_ref[...]))
        gate = _bdot(_sigmoid(xg), gup_ref[...])
        kk = k * kk_ref[...]
        kk = kk / jnp.maximum(jnp.sqrt(_dot_exact_rhs(kk * kk, ones_h)), 1e-12)
        k = k * (1.0 + (alpha - 1.0) * ka_ref[...])
        kka = kk * alpha
        pro.append((r, k, v, gate))
        for c in range(n_chunks):
            sl = slice(c * CHUNK, (c + 1) * CHUNK)
            lw_c = lw[sl]
            cum = _dot_exact_lhs(tri_b, lw_c)
            cum_last = cum[CHUNK - 1:CHUNK, :]
            p_inv = jnp.exp(-cum)
            p_tail = jnp.exp(cum_last - cum)
            unit[b, c] = dict(a=-kk[sl] * jnp.exp(cum - lw_c), b=kka[sl] * p_inv, k=k[sl] * p_inv,
                              r=r[sl] * jnp.exp(cum), kb=k[sl] * p_tail, bb=kka[sl] * p_tail,
                              pl=jnp.exp(cum_last), v=v[sl])

    keys = [(b, c, h) for b in range(nb) for c in range(n_chunks) for h in range(N_HEADS)]
    part = lambda name, key: unit[key[0], key[1]][name][:, heads[key[2]]]
    a_ab = {q: jnp.where(strict, _bdot_nt(part("a", q), part("b", q)), 0.0) for q in keys}
    a_ak = {q: jnp.where(strict, _bdot_nt(part("a", q), part("k", q)), 0.0) for q in keys}
    a_rb = {q: jnp.where(tri, _bdot_nt(part("r", q), part("b", q)), 0.0) for q in keys}
    a_rk = {q: jnp.where(tri, _bdot_nt(part("r", q), part("k", q)), 0.0) for q in keys}
    inv = {q: eye + a_ab[q] for q in keys}
    pw = a_ab
    for _ in range(5):
        pw = {q: _bdot(pw[q], pw[q]) for q in keys}
        inv = {q: inv[q] + _bdot(inv[q], pw[q]) for q in keys}
    ak_v = {q: _bdot(a_ak[q], part("v", q)) for q in keys}
    rk_v = {q: _bdot(a_rk[q], part("v", q)) for q in keys}
    kb_v = {q: _bdot_tn(part("v", q), part("kb", q)) for q in keys}

    bh = [(b, h) for b in range(nb) for h in range(N_HEADS)]
    state = {q: s_ref[q[0], q[1]] for q in bh}
    y_parts = {}
    for c in range(n_chunks):
        full = lambda q: (q[0], c, q[1])
        a_s = {q: _bdot_nt(part("a", full(q)), state[q]) for q in bh}
        r_s = {q: _bdot_nt(part("r", full(q)), state[q]) for q in bh}
        u = {q: _bdot(inv[full(q)], a_s[q] + ak_v[full(q)]) for q in bh}
        for q in bh:
            y_parts[full(q)] = r_s[q] + rk_v[full(q)] + _bdot(a_rb[full(q)], u[q])
        state = {q: (state[q] * part("pl", full(q)) + kb_v[full(q)] + _bdot_tn(u[q], part("bb", full(q))))
                 for q in bh}
    for q in bh:
        s_ref[q[0], q[1]] = state[q]

    for b in range(nb):
        r, k, v, gate = pro[b]
        y = jnp.concatenate([jnp.concatenate([y_parts[b, c, h] for h in range(N_HEADS)], axis=1)
                             for c in range(n_chunks)], axis=0)
        mean = _dot_exact_rhs(y, ones_h) * (1.0 / HEAD_DIM)
        yc = y - mean
        var = _dot_exact_rhs(yc * yc, ones_h) * (1.0 / HEAD_DIM)
        yn = yc * lax.rsqrt(var + RWKV_GN_EPS) * gng_ref[...] + gnb_ref[...]
        bonus = _dot_exact_rhs(r * k * rk_ref[...], ones_h) * v
        y_ref[b] = ((yn + bonus) * gate).astype(y_ref.dtype)


def _rwkv(p_a, B, TP, mu, w_up, w0, a_up, a0, g_up, k_k, k_a, r_k, gn_g, gn_b):
    W = MIX_W
    z64 = jnp.zeros((64, W), F32)
    wup_pad = jnp.concatenate([w_up, z64], axis=0).astype(BF16)
    aup_pad = jnp.concatenate([z64, a_up], axis=0).astype(BF16)
    row = lambda a: a.reshape(1, -1).astype(F32)
    full = lambda shape: pl.BlockSpec(shape, lambda j: (0,) * len(shape))
    return pl.pallas_call(
        _rwkv_kernel,
        grid=(TP // ROW_TILE,),
        in_specs=[pl.BlockSpec((B, ROW_TILE, 1024), lambda j: (0, j, 0)),
                  full((1, 1024)), full((128, W)), full((1, W)), full((128, W)), full((1, W)),
                  full((128, W)), full((1, W)), full((1, W)), full((1, W)), full((1, W)), full((1, W))],
        out_specs=pl.BlockSpec((B, ROW_TILE, W), lambda j: (0, j, 0)),
        out_shape=jax.ShapeDtypeStruct((B, TP, W), BF16),
        scratch_shapes=[pltpu.VMEM((B, 1, 1024), F32), pltpu.VMEM((B, N_HEADS, HEAD_DIM, HEAD_DIM), F32)],
        compiler_params=_cparams("arbitrary"),
        name="rwkv7",
    )(p_a.reshape(B, TP, 1024), row(mu), wup_pad, row(w0), aup_pad, row(a0), g_up.astype(BF16),
      row(k_k), row(k_a), row(r_k), row(gn_g), row(gn_b))


def _gla_kernel(p_ref, aup_ref, ab_ref, ng_ref, y_ref, s_ref):
    j = pl.program_id(0)
    nb = p_ref.shape[0]
    n_chunks = ROW_TILE // CHUNK

    @pl.when(j == 0)
    def _():
        s_ref[...] = jnp.zeros_like(s_ref)

    valid = (j * ROW_TILE + _iota((ROW_TILE, 1), 0)) >= FP
    tri = _tri_incl(CHUNK)
    tri_b = tri.astype(BF16)

    og_all = []
    pre = {}
    for b in range(nb):
        p = jnp.where(valid, p_ref[b], 0.0)
        la = _log_sigmoid(_bdot(p[:, 768:896], aup_ref[...]) + ab_ref[...]) * (1.0 / GLA_TAU)
        la = jnp.where(valid, la, 0.0)
        og_all.append(p[:, 512:768])
        for c in range(n_chunks):
            sl = slice(c * CHUNK, (c + 1) * CHUNK)
            pre[b, c] = dict(q=p[sl, 0:128] * (GLA_DK ** -0.5), k=p[sl, 128:256], v=p[sl, 256:512], la=la[sl])
    bc = [(b, c) for b in range(nb) for c in range(n_chunks)]
    keys = [(b, c, h) for (b, c) in bc for h in range(N_HEADS)]
    ks = [slice(h * GLA_DK, (h + 1) * GLA_DK) for h in range(N_HEADS)]
    vs = [slice(h * HEAD_DIM, (h + 1) * HEAD_DIM) for h in range(N_HEADS)]
    b_cum = {u: _dot_exact_lhs(tri_b, pre[u]["la"]) for u in bc}
    b_last = {u: b_cum[u][CHUNK - 1:CHUNK, :] for u in bc}
    q_g = {u: pre[u]["q"] * jnp.exp(b_cum[u]) for u in bc}
    k_g = {u: pre[u]["k"] * jnp.exp(-b_cum[u]) for u in bc}
    k_l = {u: pre[u]["k"] * jnp.exp(b_last[u] - b_cum[u]) for u in bc}
    dec = {u: jnp.exp(b_last[u]) for u in bc}
    att = {u: jnp.where(tri, _bdot_nt(q_g[u[0], u[1]][:, ks[u[2]]], k_g[u[0], u[1]][:, ks[u[2]]]), 0.0)
           for u in keys}
    att_v = {u: _bdot(att[u], pre[u[0], u[1]]["v"][:, vs[u[2]]]) for u in keys}
    kl_v = {u: _bdot_tn(pre[u[0], u[1]]["v"][:, vs[u[2]]], k_l[u[0], u[1]][:, ks[u[2]]]) for u in keys}

    bh = [(b, h) for b in range(nb) for h in range(N_HEADS)]
    state = {q: s_ref[q[0], q[1]] for q in bh}
    o_parts = {}
    for c in range(n_chunks):
        for q in bh:
            o_parts[q[0], c, q[1]] = att_v[q[0], c, q[1]] + _bdot_nt(q_g[q[0], c][:, ks[q[1]]], state[q])
        state = {q: state[q] * dec[q[0], c][:, ks[q[1]]] + kl_v[q[0], c, q[1]] for q in bh}
    for q in bh:
        s_ref[q[0], q[1]] = state[q]

    ones_h = _head_ones()
    for b in range(nb):
        o = jnp.concatenate([jnp.concatenate([o_parts[b, c, h] for h in range(N_HEADS)], axis=1)
                             for c in range(n_chunks)], axis=0)
        ms = _dot_exact_rhs(o * o, ones_h) * (1.0 / HEAD_DIM)
        y = o * lax.rsqrt(ms + 1e-6) * ng_ref[...] * _silu(og_all[b])
        y_ref[b] = y.astype(y_ref.dtype)


def _gla(p_b, B, TP, a_up, a_b, norm_g):
    aup_pad = jnp.zeros((128, 128), F32).at[:a_up.shape[0]].set(a_up).astype(BF16)
    full = lambda shape: pl.BlockSpec(shape, lambda j: (0,) * len(shape))
    return pl.pallas_call(
        _gla_kernel,
        grid=(TP // ROW_TILE,),
        in_specs=[pl.BlockSpec((B, ROW_TILE, 896), lambda j: (0, j, 0)),
                  full((128, 128)), full((1, 128)), full((1, MIX_W))],
        out_specs=pl.BlockSpec((B, ROW_TILE, MIX_W), lambda j: (0, j, 0)),
        out_shape=jax.ShapeDtypeStruct((B, TP, MIX_W), BF16),
        scratch_shapes=[pltpu.VMEM((B, N_HEADS, HEAD_DIM, GLA_DK), F32)],
        compiler_params=_cparams("arbitrary"),
        name="gla",
    )(p_b.reshape(B, TP, 896), aup_pad, a_b.reshape(1, 128).astype(F32),
      jnp.tile(norm_g.astype(F32), N_HEADS).reshape(1, MIX_W))


def _mlstm_kernel(p_ref, cw_ref, cb_ref, ib_ref, fb_ref, ng_ref, y_ref, carry_ref, c_ref, n_ref, m_ref):
    j = pl.program_id(0)
    nb = p_ref.shape[0]
    n_chunks = ROW_TILE // CHUNK

    @pl.when(j == 0)
    def _():
        carry_ref[...] = jnp.zeros_like(carry_ref)
        c_ref[...] = jnp.zeros_like(c_ref)
        n_ref[...] = jnp.zeros_like(n_ref)
        m_ref[...] = jnp.zeros_like(m_ref)

    valid = (j * ROW_TILE + _iota((ROW_TILE, 1), 0)) >= FP
    tri = _tri_incl(CHUNK)
    tri_b = tri.astype(BF16)
    ones_h = _head_ones()

    og_all = []
    pre = {}
    for b in range(nb):
        p = jnp.where(valid, p_ref[b], 0.0)
        a = p[:, 0:512]
        ext = jnp.concatenate([carry_ref[b], a], axis=0)
        carry_ref[b] = a[ROW_TILE - 8:ROW_TILE, :]
        conv = cb_ref[...] + a * cw_ref[CONV_W - 1:CONV_W, :]
        for s in range(1, CONV_W):
            conv = conv + pltpu.roll(ext, s, 0)[8:8 + ROW_TILE, :] * cw_ref[CONV_W - 1 - s:CONV_W - s, :]
        qk = _silu(conv)
        q = jnp.where(valid, qk[:, 0:MIX_W], 0.0)
        k = jnp.where(valid, qk[:, MIX_W:2 * MIX_W], 0.0) * (HEAD_DIM ** -0.5)
        v = p[:, 512:768]
        og_all.append(p[:, 768:1024])
        gates = p[:, 1024:1152]
        li_all = jnp.where(valid, gates + ib_ref[...], NEG)
        lf_all = jnp.where(valid, _log_sigmoid(gates + fb_ref[...]), 0.0)
        for c in range(n_chunks):
            sl = slice(c * CHUNK, (c + 1) * CHUNK)
            pre[b, c] = dict(q=q[sl], k=k[sl], v=v[sl], li=li_all[sl], lf=lf_all[sl])

    bc = [(b, c) for b in range(nb) for c in range(n_chunks)]
    keys = [(b, c, h) for (b, c) in bc for h in range(N_HEADS)]
    heads = [slice(h * HEAD_DIM, (h + 1) * HEAD_DIM) for h in range(N_HEADS)]
    part = lambda name, u: pre[u[0], u[1]][name][:, heads[u[2]]]
    b_cum = {u: _dot_exact_lhs(tri_b, pre[u]["lf"]) for u in bc}
    b_t = {u: b_cum[u].T for u in bc}
    li_t = {u: pre[u]["li"].T for u in bc}
    b_col = {u: b_cum[u[0], u[1]][:, N_HEADS + u[2]:N_HEADS + u[2] + 1] for u in keys}
    b_last = {u: b_col[u][CHUNK - 1:CHUNK, :] for u in keys}
    d_log = {u: jnp.where(tri, b_col[u] - b_t[u[0], u[1]][N_HEADS + u[2]:N_HEADS + u[2] + 1, :]
                          + li_t[u[0], u[1]][u[2]:u[2] + 1, :], -jnp.inf) for u in keys}
    dmax = {u: jnp.max(d_log[u], axis=1, keepdims=True) for u in keys}
    qk = {u: _bdot_nt(part("q", u), part("k", u)) for u in keys}
    s0 = {u: jnp.exp(d_log[u] - dmax[u]) * qk[u] for u in keys}
    sv = {u: _bdot(s0[u], part("v", u)) for u in keys}
    ssum = {u: jnp.sum(s0[u], axis=1, keepdims=True) for u in keys}
    g_loc = {u: b_last[u] - b_col[u] + pre[u[0], u[1]]["li"][:, u[2]:u[2] + 1] for u in keys}
    m_loc = {u: jnp.max(g_loc[u], axis=0, keepdims=True) for u in keys}
    kw = {u: part("k", u) * jnp.exp(g_loc[u] - m_loc[u]) for u in keys}
    kwv = {u: _bdot_tn(kw[u], part("v", u)) for u in keys}
    kwsum = {u: jnp.sum(kw[u], axis=0, keepdims=True) for u in keys}

    bh = [(b, h) for b in range(nb) for h in range(N_HEADS)]
    c_st = {q: c_ref[q[0], q[1]] for q in bh}
    n_st = {q: n_ref[q[0], q[1]] for q in bh}
    m_st = {q: m_ref[q[0], q[1]] for q in bh}
    h_parts = {}
    for c in range(n_chunks):
        full = lambda q: (q[0], c, q[1])
        qc = {q: _bdot(part("q", full(q)), c_st[q]) for q in bh}
        qn = {q: jnp.sum(part("q", full(q)) * n_st[q], axis=1, keepdims=True) for q in bh}
        inter = {q: b_col[full(q)] + m_st[q] for q in bh}
        m_t = {q: jnp.maximum(inter[q], dmax[full(q)]) for q in bh}
        e_loc = {q: jnp.exp(dmax[full(q)] - m_t[q]) for q in bh}
        w_int = {q: jnp.exp(inter[q] - m_t[q]) for q in bh}
        for q in bh:
            num = e_loc[q] * sv[full(q)] + w_int[q] * qc[q]
            den = e_loc[q] * ssum[full(q)] + w_int[q] * qn[q]
            h_parts[full(q)] = num / jnp.maximum(jnp.abs(den), jnp.exp(-m_t[q]))
        m_new = {q: jnp.maximum(b_last[full(q)] + m_st[q], m_loc[full(q)]) for q in bh}
        s_old = {q: jnp.exp(b_last[full(q)] + m_st[q] - m_new[q]) for q in bh}
        s_new = {q: jnp.exp(m_loc[full(q)] - m_new[q]) for q in bh}
        c_st = {q: s_old[q] * c_st[q] + s_new[q] * kwv[full(q)] for q in bh}
        n_st = {q: s_old[q] * n_st[q] + s_new[q] * kwsum[full(q)] for q in bh}
        m_st = m_new
    for q in bh:
        c_ref[q[0], q[1]], n_ref[q[0], q[1]], m_ref[q[0], q[1]] = c_st[q], n_st[q], m_st[q]

    for b in range(nb):
        hh = jnp.concatenate([jnp.concatenate([h_parts[b, c, h] for h in range(N_HEADS)], axis=1)
                              for c in range(n_chunks)], axis=0) * _sigmoid(og_all[b])
        mean = _dot_exact_rhs(hh, ones_h) * (1.0 / HEAD_DIM)
        hc = hh - mean
        var = _dot_exact_rhs(hc * hc, ones_h) * (1.0 / HEAD_DIM)
        y_ref[b] = (hc * lax.rsqrt(var + 1e-5) * ng_ref[...]).astype(y_ref.dtype)


def _mlstm(p_d, B, TP, conv_w, conv_b, i_b, f_b, norm_g):
    ib = jnp.zeros((1, LANES), F32).at[0, 0:N_HEADS].set(i_b)
    fb = jnp.zeros((1, LANES), F32).at[0, N_HEADS:2 * N_HEADS].set(f_b)
    full = lambda shape: pl.BlockSpec(shape, lambda j: (0,) * len(shape))
    return pl.pallas_call(
        _mlstm_kernel,
        grid=(TP // ROW_TILE,),
        in_specs=[pl.BlockSpec((B, ROW_TILE, 1152), lambda j: (0, j, 0)),
                  full((CONV_W, 512)), full((1, 512)), full((1, LANES)), full((1, LANES)), full((1, MIX_W))],
        out_specs=pl.BlockSpec((B, ROW_TILE, MIX_W), lambda j: (0, j, 0)),
        out_shape=jax.ShapeDtypeStruct((B, TP, MIX_W), BF16),
        scratch_shapes=[pltpu.VMEM((B, 8, 512), F32),
                        pltpu.VMEM((B, N_HEADS, HEAD_DIM, HEAD_DIM), F32),
                        pltpu.VMEM((B, N_HEADS, 1, HEAD_DIM), F32),
                        pltpu.VMEM((B, N_HEADS, 1, 1), F32)],
        compiler_params=_cparams("arbitrary"),
        name="mlstm",
    )(p_d.reshape(B, TP, 1152), conv_w.astype(F32), conv_b.reshape(1, 512).astype(F32), ib, fb,
      norm_g.reshape(1, MIX_W).astype(F32))


V_ROWS = 80
WT_ROWS = 528


def _dsa_prep_kernel(h_ref, wt_ref, wn_ref, kvg_ref, wuk_ref, wuvt_ref,
                     qt_ref, qit_ref, wit_ref, k_ref, ki_ref, vt_ref):
    hb = h_ref[...]
    tm = hb.shape[0]
    pt = lax.dot_general(wt_ref[...], hb, _NT, preferred_element_type=F32)
    pn = jnp.dot(hb, wn_ref[...], preferred_element_type=F32)
    ckv = pn[:, 0:DSA_KV_RANK]
    c = ckv * lax.rsqrt(jnp.mean(ckv * ckv, -1, keepdims=True) + 1e-6) * kvg_ref[...]
    cb = c.astype(BF16)
    k_ref[...] = jnp.dot(cb, wuk_ref[...], preferred_element_type=F32).astype(BF16)
    ki_ref[...] = pn[:, DSA_KV_RANK:DSA_KV_RANK + IDX_DIM].astype(BF16)
    vt = lax.dot_general(wuvt_ref[...], cb, _NT, preferred_element_type=F32)
    vt = jnp.where(_iota((V_ROWS, tm), 0) == HEAD_DIM, 1.0, vt)
    for t in range(tm // LANES):
        cs = slice(t * LANES, (t + 1) * LANES)
        for h in range(N_HEADS):
            qt_ref[t, :, h * LANES:(h + 1) * LANES] = (
                pt[h * HEAD_DIM:(h + 1) * HEAD_DIM, cs] * (HEAD_DIM ** -0.5)).astype(BF16)
        for h in range(IDX_HEADS):
            qit_ref[t, :, h * LANES:(h + 1) * LANES] = pt[MIX_W + h * IDX_DIM:MIX_W + (h + 1) * IDX_DIM, cs].astype(BF16)
        wit_ref[t] = pt[2 * MIX_W:2 * MIX_W + IDX_HEADS, cs] * ((IDX_HEADS * IDX_DIM) ** -0.5)
        vt_ref[t] = vt[:, cs].astype(BF16)


def _dsa_prep(hb, w_t, w_n, kv_norm_g, w_uk, w_uv):
    N, D = hb.shape
    tm = _pick_tile(N, 640)
    nt = tm // LANES
    full = lambda shape: pl.BlockSpec(shape, lambda i: (0,) * len(shape))
    wuvt = jnp.pad(w_uv.T, ((0, V_ROWS - HEAD_DIM), (0, 0))).astype(BF16)
    return pl.pallas_call(
        _dsa_prep_kernel,
        grid=(N // tm,),
        in_specs=[pl.BlockSpec((tm, D), lambda i: (i, 0)),
                  full((WT_ROWS, D)), full((D, 256)), full((1, DSA_KV_RANK)),
                  full((DSA_KV_RANK, HEAD_DIM)), full((V_ROWS, DSA_KV_RANK))],
        out_specs=[pl.BlockSpec((nt, HEAD_DIM, N_HEADS * LANES), lambda i: (i, 0, 0)),
                   pl.BlockSpec((nt, IDX_DIM, IDX_HEADS * LANES), lambda i: (i, 0, 0)),
                   pl.BlockSpec((nt, IDX_HEADS, LANES), lambda i: (i, 0, 0)),
                   pl.BlockSpec((tm, HEAD_DIM), lambda i: (i, 0)),
                   pl.BlockSpec((tm, IDX_DIM), lambda i: (i, 0)),
                   pl.BlockSpec((nt, V_ROWS, LANES), lambda i: (i, 0, 0))],
        out_shape=[jax.ShapeDtypeStruct((N // LANES, HEAD_DIM, N_HEADS * LANES), BF16),
                   jax.ShapeDtypeStruct((N // LANES, IDX_DIM, IDX_HEADS * LANES), BF16),
                   jax.ShapeDtypeStruct((N // LANES, IDX_HEADS, LANES), F32),
                   jax.ShapeDtypeStruct((N, HEAD_DIM), BF16),
                   jax.ShapeDtypeStruct((N, IDX_DIM), BF16),
                   jax.ShapeDtypeStruct((N // LANES, V_ROWS, LANES), BF16)],
        compiler_params=_cparams("arbitrary"),
        name="dsa_prep",
    )(hb, w_t, w_n, kv_norm_g.reshape(1, DSA_KV_RANK).astype(F32), w_uk.astype(BF16), wuvt)


def _dsa_kernel(qt_ref, qit_ref, wit_ref, k_ref, ki_ref, vt_ref, bias_ref, y_ref,
                sk_ref, rel_ref, m_ref, acc_ref, lg_ref, mg_ref, *, topk):
    i = pl.program_id(1)
    nk = i + 1
    QT = ROW_TILE
    HQ = N_HEADS * QT
    t_lane = i * QT + _iota((LANES, QT), 1)
    key_pos = lambda kt: kt * LANES + _iota((LANES, QT), 0)
    per_head = lambda fn: jnp.concatenate([fn(slice(h * QT, (h + 1) * QT)) for h in range(N_HEADS)], axis=1)

    qit = qit_ref[0]
    wit = wit_ref[0]

    GW = rel_ref.shape[1] // LANES
    n_tiles = sk_ref.shape[0] - 2
    n_trips = (i + 2 * GW) // (2 * GW)

    def group_base(g):
        return jnp.clip(GW * g, 0, n_tiles - GW)

    def issue(g, slot):
        span = pl.ds(pl.multiple_of(group_base(g) * LANES, LANES), GW * LANES)
        rel_ref[slot] = jnp.dot(ki_ref[span, :], qit, preferred_element_type=F32)

    def reduce(g, slot):
        for u in range(GW):
            kt = group_base(g) + u
            rows_u = slice(u * LANES, (u + 1) * LANES)
            score = jnp.maximum(rel_ref[slot, rows_u, 0:QT], 0.0) * wit[0:1, :]
            for h in range(1, IDX_HEADS):
                score = score + jnp.maximum(rel_ref[slot, rows_u, h * QT:(h + 1) * QT], 0.0) * wit[h:h + 1, :]
            mine = (kt >= GW * g) & (kt <= i)
            sk_ref[jnp.where(mine, kt, n_tiles)] = score

    issue(0, 0)

    def score_body(jj, c):
        issue(2 * jj + 1, 1)
        reduce(2 * jj, 0)
        issue(2 * jj + 2, 0)
        reduce(2 * jj + 1, 1)
        return c

    lax.fori_loop(0, n_trips, score_body, 0)
    first = sk_ref[0]
    first = jnp.where(key_pos(0) < FP + N_META, jnp.inf, first)
    sk_ref[0] = jnp.where(key_pos(0) >= FP, first, -jnp.inf)
    sk_ref[i] = jnp.where(key_pos(i) <= t_lane, sk_ref[i], -jnp.inf)
    sk_ref[n_tiles + 1] = jnp.full((LANES, QT), -jnp.inf, F32)

    def key_to_float(key):
        return lax.bitcast_convert_type(jnp.where(key < 0, key ^ jnp.int32(0x7FFFFFFF), key), F32)
    def count(pred_fn):
        def body(kt, acc):
            return acc + jnp.where(pred_fn(sk_ref[kt], kt), 1, 0)

        def body4(j, acc):
            for u in range(4):
                acc = body(4 * j + u, acc)
            return acc

        n4 = lax.shift_right_logical(nk, 2)
        acc = lax.fori_loop(0, n4, body4, jnp.zeros((LANES, QT), jnp.int32))
        acc = lax.fori_loop(4 * n4, nk, body, acc)
        return jnp.sum(acc, axis=0, keepdims=True)

    def bit_body(it, carry):
        tau, n_ge = carry
        cand = tau + jnp.left_shift(jnp.int32(1), 31 - it)
        cand_f = key_to_float(cand)
        cnt = count(lambda sk, kt: sk >= cand_f)
        return jnp.where(cnt >= topk, cand, tau), jnp.where(cnt >= topk, cnt, n_ge)

    tau_key, n_ge = lax.fori_loop(0, 32, bit_body, (jnp.full((1, QT), INT_MIN, jnp.int32),
                                                    jnp.zeros((1, QT), jnp.int32)))
    tau = jnp.where(tau_key < KEY_LOWEST, jnp.float32(FLT_LOWEST), key_to_float(tau_key))

    @pl.when(jnp.max(n_ge - topk) > 0)
    def _():
        n_bits = max(1, int(math.ceil(math.log2(sk_ref.shape[0] * LANES + 1))))
        need = topk - count(lambda sk, kt: sk > tau)

        def pos_body(it, x):
            cand = x + jnp.left_shift(jnp.int32(1), n_bits - 1 - it)
            cnt = count(lambda sk, kt: (sk == tau) & (key_pos(kt) < cand))
            return jnp.where(cnt < need, cand, x)

        x = lax.fori_loop(0, n_bits, pos_body, jnp.zeros((1, QT), jnp.int32))
        jmax = jnp.where(n_ge > topk, x, jnp.int32(2 ** 30))

        def drop_body(kt, c):
            sk = sk_ref[kt]
            sk_ref[kt] = jnp.where((sk == tau) & (key_pos(kt) > jmax), -jnp.inf, sk)
            return c

        lax.fori_loop(0, nk, drop_body, 0)

    qt = qt_ref[0]
    m_ref[...] = jnp.full((1, HQ), NEG, F32)
    acc_ref[...] = jnp.zeros((V_ROWS, HQ), F32)

    def park(g, slot):
        base = group_base(g)
        span = pl.ds(pl.multiple_of(base * LANES, LANES), GW * LANES)
        lg_all = jnp.dot(k_ref[span, :], qt, preferred_element_type=F32)
        tmax = None
        for u in range(GW):
            t = base + u
            mine = (t >= GW * g) & (t <= i)
            lg = lg_all[u * LANES:(u + 1) * LANES, :] + bias_ref[jnp.clip(i - t, 0, 2)]
            sel = sk_ref[jnp.where(mine, t, n_tiles + 1)] >= tau
            lgm = per_head(lambda hs: jnp.where(sel, lg[:, hs], NEG))
            lg_ref[slot, u] = lgm
            tmax = lgm if tmax is None else jnp.maximum(tmax, lgm)
        mg_ref[slot] = jnp.max(tmax, axis=0, keepdims=True)

    def weights(slot):
        m_old = m_ref[...]
        m_new = jnp.maximum(m_old, mg_ref[slot])
        m_ref[...] = m_new
        return jnp.exp(m_old - m_new), [jnp.exp(lg_ref[slot, u] - m_new).astype(BF16) for u in range(GW)]

    def fold(g, corr, prs):
        vt_all = jnp.concatenate([vt_ref[group_base(g) + u] for u in range(GW)], axis=1)
        pv = jnp.dot(vt_all, jnp.concatenate(prs, axis=0), preferred_element_type=F32)
        acc_ref[...] = acc_ref[...] * corr + pv

    park(0, 0)

    def pipe_body(jj, c):
        corr, prs = weights(0)
        park(2 * jj + 1, 1)
        fold(2 * jj, corr, prs)
        corr, prs = weights(1)
        park(2 * jj + 2, 0)
        fold(2 * jj + 1, corr, prs)
        return c

    lax.fori_loop(0, n_trips, pipe_body, 0)
    acc = acc_ref[...]
    out = acc[0:HEAD_DIM, :] / jnp.maximum(acc[HEAD_DIM:HEAD_DIM + 1, :], 1e-30)
    y_ref[...] = per_head(lambda hs: out[:, hs].T).astype(y_ref.dtype)


def _t5_bucket(dist):
    max_exact = N_BUCKETS // 2
    n = jnp.maximum(dist, 0)
    large = max_exact + (jnp.log(jnp.maximum(n, 1).astype(F32) / max_exact)
                         / math.log(MAX_DISTANCE / max_exact) * (N_BUCKETS - max_exact)).astype(jnp.int32)
    return jnp.where(n < max_exact, n, jnp.minimum(large, N_BUCKETS - 1))


def _bias_tables(rel_bias):
    per_dist = rel_bias[_t5_bucket(jnp.arange(2 * ROW_TILE, dtype=jnp.int32))]
    q_minus_s = np.arange(ROW_TILE)[None, :] - np.arange(ROW_TILE)[:, None]
    far = per_dist[2 * ROW_TILE - 1]
    tabs = [per_dist[np.clip(r * ROW_TILE + q_minus_s, 0, 2 * ROW_TILE - 1)] - far for r in (0, 1)]
    tabs.append(jnp.zeros_like(tabs[0]))
    return jnp.stack(tabs).transpose(0, 1, 3, 2).reshape(3, ROW_TILE, N_HEADS * ROW_TILE).astype(F32)


def _dsa(qt, qit, wit, k, ki, vt, bias_tab, B, TP, topk):
    nq = TP // ROW_TILE
    return pl.pallas_call(
        functools.partial(_dsa_kernel, topk=topk),
        grid=(B, nq),
        in_specs=[pl.BlockSpec((1, HEAD_DIM, N_HEADS * LANES), lambda b, i: (b * nq + i, 0, 0)),
                  pl.BlockSpec((1, IDX_DIM, IDX_HEADS * LANES), lambda b, i: (b * nq + i, 0, 0)),
                  pl.BlockSpec((1, IDX_HEADS, LANES), lambda b, i: (b * nq + i, 0, 0)),
                  pl.BlockSpec((TP, HEAD_DIM), lambda b, i: (b, 0)),
                  pl.BlockSpec((TP, IDX_DIM), lambda b, i: (b, 0)),
                  pl.BlockSpec((nq, V_ROWS, LANES), lambda b, i: (b, 0, 0)),
                  pl.BlockSpec((3, ROW_TILE, N_HEADS * ROW_TILE), lambda b, i: (0, 0, 0))],
        out_specs=pl.BlockSpec((ROW_TILE, MIX_W), lambda b, i: (b * nq + i, 0)),
        out_shape=jax.ShapeDtypeStruct((B * TP, MIX_W), BF16),
        scratch_shapes=[pltpu.VMEM((nq + 2, LANES, ROW_TILE), F32),
                        pltpu.VMEM((2, min(4, nq) * LANES, IDX_HEADS * ROW_TILE), F32),
                        pltpu.VMEM((1, N_HEADS * ROW_TILE), F32),
                        pltpu.VMEM((V_ROWS, N_HEADS * ROW_TILE), F32),
                        pltpu.VMEM((2, min(4, nq), LANES, N_HEADS * ROW_TILE), F32),
                        pltpu.VMEM((2, 1, N_HEADS * ROW_TILE), F32)],
        compiler_params=_cparams("parallel", "arbitrary"),
        name="dsa_attend",
    )(qt, qit, wit, k, ki, vt, bias_tab)


def _layer_norm_rows(z, g, b):
    mu = jnp.mean(z, -1, keepdims=True)
    zc = z - mu
    var = jnp.mean(zc * zc, -1, keepdims=True)
    return zc * lax.rsqrt(var + LN_EPS) * g + b


def _merge_kernel(h_ref, hb_ref, wg_ref, ya_ref, yb_ref, yc_ref, yd_ref, wb_ref, wo_ref, lg_ref, lb_ref,
                  h1_ref, h1b_ref):
    hb = hb_ref[...]
    merged = None
    for i, y_ref in enumerate((ya_ref, yb_ref, yc_ref, yd_ref)):
        gate = _sigmoid(jnp.dot(hb, wg_ref[:, i * D_MODEL:(i + 1) * D_MODEL], preferred_element_type=F32))
        t = gate * jnp.dot(y_ref[...], wb_ref[i], preferred_element_type=F32)
        merged = t if merged is None else merged + t
    z = DN_ALPHA * h_ref[...] + jnp.dot(merged.astype(BF16), wo_ref[...], preferred_element_type=F32)
    y = _layer_norm_rows(z, lg_ref[...], lb_ref[...])
    h1_ref[...] = y
    h1b_ref[...] = y.astype(BF16)


def _merge(h, hb, w_g, ys, w_branch, w_out, ln_g, ln_b):
    N, D = h.shape
    tm = _pick_tile(N, 640)
    full = lambda shape: pl.BlockSpec(shape, lambda i: (0,) * len(shape), pipeline_mode=pl.Buffered(1))
    tok = lambda w: pl.BlockSpec((tm, w), lambda i: (i, 0))
    return pl.pallas_call(
        _merge_kernel,
        grid=(N // tm,),
        in_specs=[tok(D), tok(D), full((D, 4 * D)), tok(MIX_W), tok(MIX_W), tok(MIX_W), tok(MIX_W),
                  full((4, MIX_W, D)), full((D, D)), full((1, D)), full((1, D))],
        out_specs=[tok(D), tok(D)],
        out_shape=[jax.ShapeDtypeStruct((N, D), F32), jax.ShapeDtypeStruct((N, D), BF16)],
        compiler_params=_cparams("arbitrary"),
        name="merge_out_ln",
    )(h, hb, w_g, *ys, w_branch.astype(BF16), w_out.astype(BF16),
      ln_g.reshape(1, D).astype(F32), ln_b.reshape(1, D).astype(F32))


def _moe_kernel(h_ref, hb_ref, wr_ref, br_ref, wg_ref, wu_ref, wd_ref, lg_ref, lb_ref, o_ref, ob_ref,
                gate_ref, acc_ref):
    e = pl.program_id(1)
    xb = hb_ref[...]
    tm = xb.shape[0]
    lane = _iota((tm, LANES), 1)

    @pl.when(e == 0)
    def _():
        logit = jnp.dot(xb, wr_ref[...], preferred_element_type=F32) + br_ref[...]
        big = jnp.int32(LANES)
        gl = jnp.where(lane < N_GROUPS, logit, -jnp.inf)
        gmax = jnp.max(gl, axis=1, keepdims=True)
        g_sel = jnp.min(jnp.where(gl == gmax, lane, big), axis=1, keepdims=True)
        p_grp = 1.0 / jnp.sum(jnp.exp(gl - gmax), axis=1, keepdims=True)
        lo = N_GROUPS + g_sel * EPG
        el = jnp.where((lane >= lo) & (lane < lo + EPG), logit, -jnp.inf)
        v1 = jnp.max(el, axis=1, keepdims=True)
        i1 = jnp.min(jnp.where(el == v1, lane, big), axis=1, keepdims=True)
        el2 = jnp.where(lane == i1, -jnp.inf, el)
        v2 = jnp.max(el2, axis=1, keepdims=True)
        i2 = jnp.min(jnp.where(el2 == v2, lane, big), axis=1, keepdims=True)
        e2 = jnp.exp(v2 - v1)
        w1 = p_grp / (1.0 + e2)
        w2 = p_grp * e2 / (1.0 + e2)
        gate_ref[...] = jnp.where(lane == i1, w1, 0.0) + jnp.where(lane == i2, w2, 0.0)
        acc_ref[...] = jnp.zeros_like(acc_ref)

    hid = _silu(jnp.dot(xb, wg_ref[0], preferred_element_type=F32)) * jnp.dot(xb, wu_ref[0], preferred_element_type=F32)
    gates = gate_ref[...]
    scaled = []
    for j in range(EPG):
        g_j = jnp.sum(jnp.where(lane == e * EPG + j + N_GROUPS, gates, 0.0), axis=1, keepdims=True)
        scaled.append((hid[:, j * D_EXPERT:(j + 1) * D_EXPERT] * g_j).astype(BF16))
    acc_ref[...] += jnp.dot(jnp.concatenate(scaled, axis=1), wd_ref[0], preferred_element_type=F32)

    @pl.when(e == N_GROUPS - 1)
    def _():
        y = _layer_norm_rows(DN_ALPHA * h_ref[...] + acc_ref[...], lg_ref[...], lb_ref[...])
        o_ref[...] = y
        ob_ref[...] = y.astype(BF16)


def _moe(h1, h1b, w_grp, b_grp, w_rt, b_rt, w_gate, w_up, w_down, ln_g, ln_b):
    N, D = h1.shape
    tm = _pick_tile(N, 640)
    w_r = jnp.zeros((D, LANES), F32).at[:, 0:N_GROUPS].set(w_grp).at[:, N_GROUPS:N_GROUPS + N_EXPERTS].set(w_rt)
    b_r = jnp.zeros((1, LANES), F32).at[0, 0:N_GROUPS].set(b_grp).at[0, N_GROUPS:N_GROUPS + N_EXPERTS].set(b_rt)
    GH = EPG * D_EXPERT
    by_group = lambda w: w.reshape(N_GROUPS, EPG, D, D_EXPERT).transpose(0, 2, 1, 3).reshape(N_GROUPS, D, GH)
    w_gate, w_up, w_down = by_group(w_gate), by_group(w_up), w_down.reshape(N_GROUPS, GH, D)
    full = lambda shape: pl.BlockSpec(shape, lambda i, e: (0,) * len(shape))
    tok = lambda w: pl.BlockSpec((tm, w), lambda i, e: (i, 0))
    return pl.pallas_call(
        _moe_kernel,
        grid=(N // tm, N_GROUPS),
        in_specs=[tok(D), tok(D), full((D, LANES)), full((1, LANES)),
                  pl.BlockSpec((1, D, GH), lambda i, e: (e, 0, 0)),
                  pl.BlockSpec((1, D, GH), lambda i, e: (e, 0, 0)),
                  pl.BlockSpec((1, GH, D), lambda i, e: (e, 0, 0)),
                  full((1, D)), full((1, D))],
        out_specs=[tok(D), tok(D)],
        out_shape=[jax.ShapeDtypeStruct((N, D), F32), jax.ShapeDtypeStruct((N, D), BF16)],
        scratch_shapes=[pltpu.VMEM((tm, LANES), F32), pltpu.VMEM((tm, D), F32)],
        compiler_params=_cparams("arbitrary", "arbitrary"),
        name="hier_moe_ln",
    )(h1, h1b, w_r.astype(BF16), b_r, w_gate.astype(BF16), w_up.astype(BF16), w_down.astype(BF16),
      ln_g.reshape(1, D).astype(F32), ln_b.reshape(1, D).astype(F32))


def _pad_cols(w, width):
    return jnp.pad(w, ((0, 0), (0, width - w.shape[1])))


def _split_w_in(w):
    o = 0
    w_a = w[:, o:o + 1024]; o += 1024
    gq, gk, gv, ga, gg = (w[:, o:o + 128], w[:, o + 128:o + 256], w[:, o + 256:o + 512],
                          w[:, o + 512:o + 528], w[:, o + 528:o + 784]); o += 784
    w_b = _pad_cols(jnp.concatenate([gq, gk, gv, gg, ga], axis=1), 896)
    cq, ckv, cqi, cki, cwi = (w[:, o:o + 256], w[:, o + 256:o + 384], w[:, o + 384:o + 640],
                              w[:, o + 640:o + 672], w[:, o + 672:o + 680]); o += 680
    w_t = jnp.pad(jnp.concatenate([cq.T, cqi.T, cwi.T], axis=0), ((0, WT_ROWS - 2 * MIX_W - IDX_HEADS), (0, 0)))
    w_n = _pad_cols(jnp.concatenate([ckv, cki], axis=1), 256)
    dq, dk, dv, di, df, do = (w[:, o:o + 256], w[:, o + 256:o + 512], w[:, o + 512:o + 768],
                              w[:, o + 768:o + 772], w[:, o + 772:o + 776], w[:, o + 776:o + 1032]); o += 1032
    w_d = _pad_cols(jnp.concatenate([dq, dk, dv, do, di, df], axis=1), 1152)
    w_g = w[:, o:o + 4096]
    bf = lambda a: a.astype(BF16)
    return bf(w_a), bf(w_b), bf(w_t), bf(w_n), bf(w_d), bf(w_g)


def kernel(x, meta, ln_in_g, ln_in_b, rel_bias, w_in, rwkv_mu, rwkv_w_up, rwkv_w0, rwkv_a_up, rwkv_a0, rwkv_g_up, rwkv_k_k, rwkv_k_a, rwkv_r_k, rwkv_gn_g, rwkv_gn_b, gla_a_up, gla_a_b, gla_norm_g, dsa_kv_norm_g, dsa_w_uk, dsa_w_uv, mlstm_conv_w, mlstm_conv_b, mlstm_i_b, mlstm_f_b, mlstm_norm_g, w_branch, w_out, ln1_g, ln1_b, moe_w_grp, moe_b_grp, moe_w_rt, moe_b_rt, moe_w_gate, moe_w_up, moe_w_down, ln2_g, ln2_b):
    B, S, D = x.shape
    assert D == D_MODEL and S % ROW_TILE == 0
    TP = S + FRONT
    N = B * TP
    topk = min(TOPK_MAX, S // 4)
    bias_tab = _bias_tables(rel_bias)

    h, hb = _embed(x, meta, ln_in_g, ln_in_b)
    h = h.reshape(N, D)
    hb = hb.reshape(N, D)
    for l in range(DEPTH):
        w_a, w_b, w_t, w_n, w_d, w_g = _split_w_in(w_in[l])
        p_a, p_b, p_d = _proj(hb, (w_a, w_b, w_d))
        qt, qit, wit, k, ki, vt = _dsa_prep(hb, w_t, w_n, dsa_kv_norm_g[l], dsa_w_uk[l], dsa_w_uv[l])
        y_a = _rwkv(p_a, B, TP, rwkv_mu[l], rwkv_w_up[l], rwkv_w0[l], rwkv_a_up[l], rwkv_a0[l], rwkv_g_up[l],
                    rwkv_k_k[l], rwkv_k_a[l], rwkv_r_k[l], rwkv_gn_g[l], rwkv_gn_b[l])
        y_b = _gla(p_b, B, TP, gla_a_up[l], gla_a_b[l], gla_norm_g[l])
        y_c = _dsa(qt, qit, wit, k, ki, vt, bias_tab, B, TP, topk)
        y_d = _mlstm(p_d, B, TP, mlstm_conv_w[l], mlstm_conv_b[l], mlstm_i_b[l], mlstm_f_b[l], mlstm_norm_g[l])
        ys = (y_a.reshape(N, MIX_W), y_b.reshape(N, MIX_W), y_c, y_d.reshape(N, MIX_W))
        h1, h1b = _merge(h, hb, w_g, ys, w_branch[l], w_out[l], ln1_g[l], ln1_b[l])
        h, hb = _moe(h1, h1b, moe_w_grp[l], moe_b_grp[l], moe_w_rt[l], moe_b_rt[l],
                     moe_w_gate[l], moe_w_up[l], moe_w_down[l], ln2_g[l], ln2_b[l])
    return h.reshape(B, TP, D)[:, FRONT:]
```
